```python
import math
import jax, jax.numpy as jnp
from jax import lax
import numpy as np

D_MODEL = 1024
BATCH = 8
SEQ = 4096
DEPTH = 2

GRID_W = 64
CTX_LEN = 256
N_MIXERS = 4
GROUP_W = D_MODEL // N_MIXERS
MIX_W = N_MIXERS * GROUP_W
NORM_EPS = 1e-6
GLA_HEADS = 4
GLA_DV = GROUP_W // GLA_HEADS
GLA_DK = GLA_DV // 2
GLA_GATE_RANK = 16
GLA_TAU = 16.0
GLA_CHUNK = 64
ROPE_BASE = 10000.0
NA_HEADS = 4
NA_DH = GROUP_W // NA_HEADS
NA_KR = 8
NA_KC = 16
S5_GROUP = 16
S5_GROUPS = GROUP_W // S5_GROUP
S5_STATE = 64
POOL_WINDOWS = (2, 4, 8, 16)
POOL_GW = GROUP_W // len(POOL_WINDOWS)
CTX_SPLIT = (GLA_HEADS * GLA_DK, GROUP_W, GLA_GATE_RANK, GLA_GATE_RANK, GROUP_W, GROUP_W, GROUP_W)
LAT_SPLIT = (GLA_HEADS * GLA_DK, GROUP_W, GROUP_W, MIX_W)
N_CTX_COLS = sum(CTX_SPLIT)
N_IN_COLS = N_CTX_COLS + sum(LAT_SPLIT)

kernel_name = "hybrid_parallel_head_groups_diffusion_block"


def rmsnorm(x, g):
    xf = x.astype(jnp.float32)
    y = xf * lax.rsqrt(jnp.mean(xf * xf, axis=-1, keepdims=True) + NORM_EPS)
    return (y * g.astype(jnp.float32)).astype(x.dtype)


def split_cols(p, sizes):
    idx = [int(i) for i in np.cumsum(sizes)[:-1]]
    return jnp.split(p, idx, axis=-1)


def flip_seq(a):
    return jnp.flip(a, axis=1)


def axial_rope_tables(n_tokens):
    t = jnp.arange(n_tokens)
    half = GLA_DK // 2
    freqs = ROPE_BASE ** (-jnp.arange(0, half, 2, dtype=jnp.float32) / half)
    def table(pos):
        ang = pos.astype(jnp.float32)[:, None] * freqs[None, :]
        ang = jnp.concatenate([ang, ang], axis=-1)
        return jnp.cos(ang)[:, None, :], jnp.sin(ang)[:, None, :]
    return table(t // GRID_W) + table(t % GRID_W)


def rotate_half(x):
    x1, x2 = jnp.split(x, 2, axis=-1)
    return jnp.concatenate([-x2, x1], axis=-1)


def apply_axial_rope(x, tables):
    cos_r, sin_r, cos_c, sin_c = tables
    xr, xc = jnp.split(x, 2, axis=-1)
    return jnp.concatenate([xr * cos_r + rotate_half(xr) * sin_r,
                            xc * cos_c + rotate_half(xc) * sin_c], axis=-1)


def gla_chunked(q, k, v, logg, s0):
    bsz, n_tok, n_h, _ = q.shape
    dv = v.shape[-1]
    n_chunks = n_tok // GLA_CHUNK
    def to_chunks(a):
        return a.reshape(bsz, n_chunks, GLA_CHUNK, n_h, a.shape[-1]).transpose(1, 0, 3, 2, 4)
    lower = jnp.tril(jnp.ones((GLA_CHUNK, GLA_CHUNK), dtype=bool))
    def step(state, inp):
        qc, kc, vc, gc = inp
        b = jnp.cumsum(gc, axis=-2)
        diff = b[:, :, :, None, :] - b[:, :, None, :, :]
        decay = jnp.exp(jnp.where(lower[:, :, None], diff, -jnp.inf))
        att = jnp.einsum('bhid,bhjd,bhijd->bhij', qc, kc, decay)
        o = (jnp.einsum('bhij,bhje->bhie', att, vc)
             + jnp.einsum('bhid,bhde->bhie', qc * jnp.exp(b), state))
        b_last = b[:, :, -1:, :]
        state = (jnp.exp(b_last)[:, :, 0, :, None] * state
                 + jnp.einsum('bhjd,bhje->bhde', kc * jnp.exp(b_last - b), vc))
        return state, o
    _, o = lax.scan(step, s0, (to_chunks(q), to_chunks(k), to_chunks(v), to_chunks(logg)))
    return o.transpose(1, 0, 3, 2, 4).reshape(bsz, n_tok, n_h, dv)


def gla_final_state(k, v, logg):
    b = jnp.cumsum(logg, axis=1)
    return jnp.einsum('bnhd,bnhe->bhde', k * jnp.exp(b[:, -1:] - b), v)


def gla_head_norm(o, g):
    y = o * lax.rsqrt(jnp.mean(o * o, axis=-1, keepdims=True) + NORM_EPS) * g.astype(jnp.float32)
    return y.reshape(o.shape[0], o.shape[1], GLA_HEADS * GLA_DV)


def gla_mixer(q, k, v, gf, gb, kc, vc, gfc, gbc, qc, w_gate, b_gate, g_norm, rope):
    f32 = jnp.float32
    def heads(a, d):
        return a.astype(f32).reshape(a.shape[0], a.shape[1], GLA_HEADS, d)
    def log_gate(g_lr, direction):
        z = jnp.einsum('bnr,re->bne', g_lr.astype(f32), w_gate[direction].astype(f32)) + b_gate[direction].astype(f32)
        return heads(jax.nn.log_sigmoid(z) / GLA_TAU, GLA_DK)
    kch, vch = heads(kc, GLA_DK), heads(vc, GLA_DV)
    lgc_f, lgc_b = log_gate(gfc, 0), log_gate(gbc, 1)
    s_f = gla_final_state(kch, vch, lgc_f)
    s_b = gla_final_state(flip_seq(kch), flip_seq(vch), flip_seq(lgc_b))
    qh = apply_axial_rope(heads(q, GLA_DK), rope) * GLA_DK ** -0.5
    kh = apply_axial_rope(heads(k, GLA_DK), rope)
    vh = heads(v, GLA_DV)
    o = (gla_chunked(qh, kh, vh, log_gate(gf, 0), s_f)
         + flip_seq(gla_chunked(flip_seq(qh), flip_seq(kh), flip_seq(vh), flip_seq(log_gate(gb, 1)), s_b)))
    y = gla_head_norm(o, g_norm)
    y_ctx = None
    if qc is not None:
        qch = heads(qc, GLA_DK) * GLA_DK ** -0.5
        zero = jnp.zeros_like(s_f)
        oc = (gla_chunked(qch, kch, vch, lgc_f, zero)
              + flip_seq(gla_chunked(flip_seq(qch), flip_seq(kch), flip_seq(vch), flip_seq(lgc_b), zero)))
        y_ctx = gla_head_norm(oc, g_norm)
    return y, y_ctx


def na_mixer(q, k, v, kc, vc, qc, rpb, rows):
    f32 = jnp.float32
    bsz, n_tok, _ = q.shape
    scale = NA_DH ** -0.5
    kr = min(NA_KR, rows)
    def grid(a):
        return a.astype(f32).reshape(bsz, rows, GRID_W, NA_HEADS, NA_DH).transpose(0, 3, 1, 2, 4)
    def seq_heads(a):
        return a.astype(f32).reshape(a.shape[0], a.shape[1], NA_HEADS, NA_DH).transpose(0, 2, 1, 3)
    qg, kg, vg = grid(q) * scale, grid(k), grid(v)
    kch, vch = seq_heads(kc), seq_heads(vc)
    r = jnp.arange(rows)
    row_idx = jnp.clip(r - kr // 2, 0, rows - kr)[:, None] + jnp.arange(kr)[None, :]
    col = jnp.arange(GRID_W)
    cs = jnp.clip(col - NA_KC // 2, 0, GRID_W - NA_KC)
    col_mask = (col[None, :] >= cs[:, None]) & (col[None, :] < cs[:, None] + NA_KC)
    ro = row_idx - r[:, None] + (NA_KR - 1)
    co = jnp.clip(col[None, :] - col[:, None], -(NA_KC - 1), NA_KC - 1) + (NA_KC - 1)
    bias = rpb.astype(f32)[:, ro[:, None, :, None], co[None, :, None, :]]
    k_rows, v_rows = kg[:, :, row_idx], vg[:, :, row_idx]
    s_win = jnp.einsum('bhrqd,bhrikd->bhrqik', qg, k_rows) + bias[None]
    s_win = jnp.where(col_mask[:, None, :], s_win, -jnp.inf)
    s_ctx = jnp.einsum('bhrqd,bhnd->bhrqn', qg, kch)
    n_win = kr * GRID_W
    p = jax.nn.softmax(jnp.concatenate([s_win.reshape(bsz, NA_HEADS, rows, GRID_W, n_win), s_ctx], axis=-1), axis=-1)
    o = (jnp.einsum('bhrqik,bhrikd->bhrqd', p[..., :n_win].reshape(s_win.shape), v_rows)
         + jnp.einsum('bhrqn,bhnd->bhrqd', p[..., n_win:], vch))
    y = o.transpose(0, 2, 3, 1, 4).reshape(bsz, n_tok, NA_HEADS * NA_DH)
    y_ctx = None
    if qc is not None:
        pc = jax.nn.softmax(jnp.einsum('bhnd,bhmd->bhnm', seq_heads(qc) * scale, kch), axis=-1)
        oc = jnp.einsum('bhnm,bhmd->bhnd', pc, vch)
        y_ctx = oc.transpose(0, 2, 1, 3).reshape(qc.shape[0], qc.shape[1], NA_HEADS * NA_DH)
    return y, y_ctx


def s5_discretise(lam_re, lam_im, log_dt, b_re, b_im):
    dt = jnp.exp(log_dt)[:, None]
    mag = jnp.exp(lam_re * dt)
    ang = lam_im * dt
    lb_re, lb_im = mag * jnp.cos(ang), mag * jnp.sin(ang)
    num_re, num_im = lb_re - 1.0, lb_im
    den = lam_re * lam_re + lam_im * lam_im
    coef_re = ((num_re * lam_re + num_im * lam_im) / den)[..., None]
    coef_im = ((num_im * lam_re - num_re * lam_im) / den)[..., None]
    bb_re = coef_re * b_re - coef_im * b_im
    bb_im = coef_re * b_im + coef_im * b_re
    return lb_re, lb_im, bb_re, bb_im


def complex_affine_combine(e1, e2):
    a1r, a1i, b1r, b1i = e1
    a2r, a2i, b2r, b2i = e2
    return (a1r * a2r - a1i * a2i, a1r * a2i + a1i * a2r,
            a2r * b1r - a2i * b1i + b2r, a2r * b1i + a2i * b1r + b2i)


def s5_scan(u, disc, x0, reverse):
    lb_re, lb_im, bb_re, bb_im = disc
    bu_re = jnp.einsum('bngh,gph->bngp', u, bb_re)
    bu_im = jnp.einsum('bngh,gph->bngp', u, bb_im)
    a_re = jnp.broadcast_to(lb_re, bu_re.shape)
    a_im = jnp.broadcast_to(lb_im, bu_im.shape)
    ar, ai, xr, xi = lax.associative_scan(complex_affine_combine, (a_re, a_im, bu_re, bu_im), reverse=reverse, axis=1)
    if x0 is not None:
        x0r, x0i = x0[0][:, None], x0[1][:, None]
        xr, xi = xr + ar * x0r - ai * x0i, xi + ar * x0i + ai * x0r
    return xr, xi


def s5_readout(xr, xi, c_re, c_im):
    y = (jnp.einsum('bngp,ghp->bngh', xr, c_re.astype(jnp.float32))
         - jnp.einsum('bngp,ghp->bngh', xi, c_im.astype(jnp.float32)))
    return y.reshape(y.shape[0], y.shape[1], S5_GROUPS * S5_GROUP)


def s5_glu(y, w, b):
    g = jax.nn.gelu(y)
    return g * jax.nn.sigmoid(g @ w.astype(jnp.float32) + b.astype(jnp.float32))


def s5_mixer(u, uc, ctx_out, lam_re, lam_im, log_dt, b_re, b_im, c_re, c_im, d_skip, w_glu, b_glu):
    f32 = jnp.float32
    def groups(a):
        return a.astype(f32).reshape(a.shape[0], a.shape[1], S5_GROUPS, S5_GROUP)
    ul, ucg = groups(u), groups(uc)
    d_skip = d_skip.astype(f32)
    y = u.astype(f32) * d_skip
    yc = uc.astype(f32) * d_skip if ctx_out else None
    for direction, rev in enumerate((False, True)):
        disc = s5_discretise(lam_re[direction].astype(f32), lam_im[direction].astype(f32),
                             log_dt[direction].astype(f32), b_re[direction].astype(f32), b_im[direction].astype(f32))
        xcr, xci = s5_scan(ucg, disc, None, rev)
        fin = 0 if rev else -1
        xr, xi = s5_scan(ul, disc, (xcr[:, fin], xci[:, fin]), rev)
        y = y + s5_readout(xr, xi, c_re[direction], c_im[direction])
        if ctx_out:
            yc = yc + s5_readout(xcr, xci, c_re[direction], c_im[direction])
    return s5_glu(y, w_glu, b_glu), (s5_glu(yc, w_glu, b_glu) if ctx_out else None)


def centred_mean(u, w):
    n = u.shape[1]
    csum = jnp.concatenate([jnp.zeros_like(u[:, :1]), jnp.cumsum(u, axis=1)], axis=1)
    t = jnp.arange(n)
    lo = jnp.clip(t - w // 2, 0, n)
    hi = jnp.clip(t - w // 2 + w, 0, n)
    return (csum[:, hi] - csum[:, lo]) / (hi - lo).astype(u.dtype)[None, :, None]


def pool_mixer(u, w_pool, scale):
    uf = u.astype(jnp.float32)
    parts = jnp.split(uf, len(POOL_WINDOWS), axis=-1)
    outs = [jnp.einsum('bnc,ce->bne', centred_mean(g, w) - g, w_pool[i].astype(jnp.float32))
            for i, (g, w) in enumerate(zip(parts, POOL_WINDOWS))]
    return jnp.concatenate(outs, axis=-1) * scale.astype(jnp.float32)


def _fwd_setup_inputs(seed: int = 0) -> dict:
    key = jax.random.key(seed)
    ks = jax.random.split(key, 32)
    f32 = jnp.float32
    def nrm(k, shape, std):
        return std * jax.random.normal(k, shape, f32)
    L = DEPTH
    lam_im0 = jnp.pi * jnp.arange(S5_STATE, dtype=f32)
    return {
        "x": nrm(ks[0], (BATCH, SEQ, D_MODEL), 1.0),
        "c": nrm(ks[1], (BATCH, D_MODEL), 1.0),
        "ctx": nrm(ks[2], (BATCH, CTX_LEN, D_MODEL), 1.0),
        "c_ctx": nrm(ks[3], (D_MODEL,), 1.0),
        "w_mod": nrm(ks[4], (L, D_MODEL, 3 * D_MODEL), 0.5 * D_MODEL ** -0.5),
        "b_mod": nrm(ks[5], (L, 3 * D_MODEL), 0.02),
        "g_pre": 1.0 + nrm(ks[6], (L, D_MODEL), 0.02),
        "g_post": 1.0 + nrm(ks[7], (L, D_MODEL), 0.02),
        "w_in": nrm(ks[8], (L, D_MODEL, N_IN_COLS), D_MODEL ** -0.5),
        "w_out": nrm(ks[9], (L, MIX_W, D_MODEL), MIX_W ** -0.5),
        "gla_w_gate": nrm(ks[10], (L, 2, GLA_GATE_RANK, GLA_HEADS * GLA_DK), GLA_GATE_RANK ** -0.5),
        "gla_b_gate": nrm(ks[11], (L, 2, GLA_HEADS * GLA_DK), 0.1),
        "gla_g_norm": 1.0 + nrm(ks[12], (L, GLA_DV), 0.02),
        "na_rpb": nrm(ks[13], (L, NA_HEADS, 2 * NA_KR - 1, 2 * NA_KC - 1), 0.1),
        "s5_lam_re": -0.5 + nrm(ks[14], (L, 2, S5_GROUPS, S5_STATE), 0.01),
        "s5_lam_im": lam_im0 + nrm(ks[15], (L, 2, S5_GROUPS, S5_STATE), 0.01),
        "s5_log_dt": jax.random.uniform(ks[16], (L, 2, S5_GROUPS), f32, math.log(1e-3), math.log(1e-1)),
        "s5_b_re": nrm(ks[17], (L, 2, S5_GROUPS, S5_STATE, S5_GROUP), (2.0 * S5_GROUP) ** -0.5),
        "s5_b_im": nrm(ks[18], (L, 2, S5_GROUPS, S5_STATE, S5_GROUP), (2.0 * S5_GROUP) ** -0.5),
        "s5_c_re": nrm(ks[19], (L, 2, S5_GROUPS, S5_GROUP, S5_STATE), S5_STATE ** -0.5),
        "s5_c_im": nrm(ks[20], (L, 2, S5_GROUPS, S5_GROUP, S5_STATE), S5_STATE ** -0.5),
        "s5_d": nrm(ks[21], (L, GROUP_W), 1.0),
        "s5_w_glu": nrm(ks[22], (L, GROUP_W, GROUP_W), GROUP_W ** -0.5),
        "s5_b_glu": nrm(ks[23], (L, GROUP_W), 0.02),
        "pool_w": nrm(ks[24], (L, len(POOL_WINDOWS), POOL_GW, POOL_GW), POOL_GW ** -0.5),
        "pool_scale": 1.0 + nrm(ks[25], (L, GROUP_W), 0.02),
    }


def _fwd_reference(x, c, ctx, c_ctx, w_mod, b_mod, g_pre, g_post, w_in, w_out,
              gla_w_gate, gla_b_gate, gla_g_norm, na_rpb,
              s5_lam_re, s5_lam_im, s5_log_dt, s5_b_re, s5_b_im, s5_c_re, s5_c_im, s5_d, s5_w_glu, s5_b_glu,
              pool_w, pool_scale):
    dt = x.dtype
    n_tok = x.shape[1]
    rows = n_tok // GRID_W
    rope = axial_rope_tables(n_tok)
    xc = ctx
    for l in range(DEPTH):
        last = l == DEPTH - 1
        shift, scale, gate = jnp.split(jax.nn.silu(c) @ w_mod[l] + b_mod[l], 3, axis=-1)
        shift_c, scale_c, gate_c = jnp.split(jax.nn.silu(c_ctx) @ w_mod[l] + b_mod[l], 3, axis=-1)
        h = rmsnorm(x, g_pre[l]) * (1.0 + scale[:, None]) + shift[:, None]
        hc = rmsnorm(xc, g_pre[l]) * (1.0 + scale_c) + shift_c
        p_kv, p_q = jnp.split(h @ w_in[l], [N_CTX_COLS], axis=-1)
        gla_k, gla_v, gla_gf, gla_gb, na_k, na_v, s5_u = split_cols(p_kv, CTX_SPLIT)
        gla_q, na_q, pool_u, gate_cols = split_cols(p_q, LAT_SPLIT)
        if last:
            pc_kv = hc @ w_in[l][:, :N_CTX_COLS]
            gla_qc = na_qc = pool_uc = gate_cols_c = None
        else:
            pc_kv, pc_q = jnp.split(hc @ w_in[l], [N_CTX_COLS], axis=-1)
            gla_qc, na_qc, pool_uc, gate_cols_c = split_cols(pc_q, LAT_SPLIT)
        gla_kc, gla_vc, gla_gfc, gla_gbc, na_kc, na_vc, s5_uc = split_cols(pc_kv, CTX_SPLIT)
        y_gla, yc_gla = gla_mixer(gla_q, gla_k, gla_v, gla_gf, gla_gb, gla_kc, gla_vc, gla_gfc, gla_gbc, gla_qc,
                                  gla_w_gate[l], gla_b_gate[l], gla_g_norm[l], rope)
        y_na, yc_na = na_mixer(na_q, na_k, na_v, na_kc, na_vc, na_qc, na_rpb[l], rows)
        y_s5, yc_s5 = s5_mixer(s5_u, s5_uc, not last, s5_lam_re[l], s5_lam_im[l], s5_log_dt[l], s5_b_re[l],
                               s5_b_im[l], s5_c_re[l], s5_c_im[l], s5_d[l], s5_w_glu[l], s5_b_glu[l])
        y_pool = pool_mixer(pool_u, pool_w[l], pool_scale[l])
        y = jnp.concatenate([y_gla, y_na, y_s5, y_pool], axis=-1).astype(dt) * jax.nn.silu(gate_cols)
        x = x + gate[:, None] * rmsnorm(y @ w_out[l], g_post[l])
        if not last:
            yc_pool = pool_mixer(pool_uc, pool_w[l], pool_scale[l])
            yc = jnp.concatenate([yc_gla, yc_na, yc_s5, yc_pool], axis=-1).astype(dt) * jax.nn.silu(gate_cols_c)
            xc = xc + gate_c * rmsnorm(yc @ w_out[l], g_post[l])
    return x


import jax as _jax
import jax.numpy as _jnp

TWIN_FORMAT = 'train_step'
FWD_PARAMS = ['x', 'c', 'ctx', 'c_ctx', 'w_mod', 'b_mod', 'g_pre', 'g_post', 'w_in', 'w_out', 'gla_w_gate', 'gla_b_gate', 'gla_g_norm', 'na_rpb', 's5_lam_re', 's5_lam_im', 's5_log_dt', 's5_b_re', 's5_b_im', 's5_c_re', 's5_c_im', 's5_d', 's5_w_glu', 's5_b_glu', 'pool_w', 'pool_scale']
TWIN_WEIGHTS = ['c_ctx', 'w_mod', 'b_mod', 'g_pre', 'g_post', 'w_in', 'w_out', 'gla_w_gate', 'gla_b_gate', 'gla_g_norm', 'na_rpb', 's5_lam_re', 's5_lam_im', 's5_log_dt', 's5_b_re', 's5_b_im', 's5_c_re', 's5_c_im', 's5_d', 's5_w_glu', 's5_b_glu', 'pool_w', 'pool_scale']
TWIN_DIFF_INPUT = 'x'
TWIN_INPUTS = ['x', 'c', 'ctx', 'c_ctx', 'w_mod', 'b_mod', 'g_pre', 'g_post', 'w_in', 'w_out', 'gla_w_gate', 'gla_b_gate', 'gla_g_norm', 'na_rpb', 's5_lam_re', 's5_lam_im', 's5_log_dt', 's5_b_re', 's5_b_im', 's5_c_re', 's5_c_im', 's5_d', 's5_w_glu', 's5_b_glu', 'pool_w', 'pool_scale', 'loss_target', 'm_c_ctx', 'm_w_mod', 'm_b_mod', 'm_g_pre', 'm_g_post', 'm_w_in', 'm_w_out', 'm_gla_w_gate', 'm_gla_b_gate', 'm_gla_g_norm', 'm_na_rpb', 'm_s5_lam_re', 'm_s5_lam_im', 'm_s5_log_dt', 'm_s5_b_re', 'm_s5_b_im', 'm_s5_c_re', 'm_s5_c_im', 'm_s5_d', 'm_s5_w_glu', 'm_s5_b_glu', 'm_pool_w', 'm_pool_scale', 'v_c_ctx', 'v_w_mod', 'v_b_mod', 'v_g_pre', 'v_g_post', 'v_w_in', 'v_w_out', 'v_gla_w_gate', 'v_gla_b_gate', 'v_gla_g_norm', 'v_na_rpb', 'v_s5_lam_re', 'v_s5_lam_im', 'v_s5_log_dt', 'v_s5_b_re', 'v_s5_b_im', 'v_s5_c_re', 'v_s5_c_im', 'v_s5_d', 'v_s5_w_glu', 'v_s5_b_glu', 'v_pool_w', 'v_pool_scale']
TWIN_OUTPUTS = ['loss', 'grad_x', 'grad_c_ctx', 'grad_w_mod', 'grad_b_mod', 'grad_g_pre', 'grad_g_post', 'grad_w_in', 'grad_w_out', 'grad_gla_w_gate', 'grad_gla_b_gate', 'grad_gla_g_norm', 'grad_na_rpb', 'grad_s5_lam_re', 'grad_s5_lam_im', 'grad_s5_log_dt', 'grad_s5_b_re', 'grad_s5_b_im', 'grad_s5_c_re', 'grad_s5_c_im', 'grad_s5_d', 'grad_s5_w_glu', 'grad_s5_b_glu', 'grad_pool_w', 'grad_pool_scale', 'delta_c_ctx', 'delta_w_mod', 'delta_b_mod', 'delta_g_pre', 'delta_g_post', 'delta_w_in', 'delta_w_out', 'delta_gla_w_gate', 'delta_gla_b_gate', 'delta_gla_g_norm', 'delta_na_rpb', 'delta_s5_lam_re', 'delta_s5_lam_im', 'delta_s5_log_dt', 'delta_s5_b_re', 'delta_s5_b_im', 'delta_s5_c_re', 'delta_s5_c_im', 'delta_s5_d', 'delta_s5_w_glu', 'delta_s5_b_glu', 'delta_pool_w', 'delta_pool_scale', 'new_m_c_ctx', 'new_m_w_mod', 'new_m_b_mod', 'new_m_g_pre', 'new_m_g_post', 'new_m_w_in', 'new_m_w_out', 'new_m_gla_w_gate', 'new_m_gla_b_gate', 'new_m_gla_g_norm', 'new_m_na_rpb', 'new_m_s5_lam_re', 'new_m_s5_lam_im', 'new_m_s5_log_dt', 'new_m_s5_b_re', 'new_m_s5_b_im', 'new_m_s5_c_re', 'new_m_s5_c_im', 'new_m_s5_d', 'new_m_s5_w_glu', 'new_m_s5_b_glu', 'new_m_pool_w', 'new_m_pool_scale', 'new_v_c_ctx', 'new_v_w_mod', 'new_v_b_mod', 'new_v_g_pre', 'new_v_g_post', 'new_v_w_in', 'new_v_w_out', 'new_v_gla_w_gate', 'new_v_gla_b_gate', 'new_v_gla_g_norm', 'new_v_na_rpb', 'new_v_s5_lam_re', 'new_v_s5_lam_im', 'new_v_s5_log_dt', 'new_v_s5_b_re', 'new_v_s5_b_im', 'new_v_s5_c_re', 'new_v_s5_c_im', 'new_v_s5_d', 'new_v_s5_w_glu', 'new_v_s5_b_glu', 'new_v_pool_w', 'new_v_pool_scale']
TWIN_LEAF_KINDS = {'loss': 'loss', 'grad_x': 'grad_x', 'grad_c_ctx': 'grad_w', 'grad_w_mod': 'grad_w', 'grad_b_mod': 'grad_w', 'grad_g_pre': 'grad_w', 'grad_g_post': 'grad_w', 'grad_w_in': 'grad_w', 'grad_w_out': 'grad_w', 'grad_gla_w_gate': 'grad_w', 'grad_gla_b_gate': 'grad_w', 'grad_gla_g_norm': 'grad_w', 'grad_na_rpb': 'grad_w', 'grad_s5_lam_re': 'grad_w', 'grad_s5_lam_im': 'grad_w', 'grad_s5_log_dt': 'grad_w', 'grad_s5_b_re': 'grad_w', 'grad_s5_b_im': 'grad_w', 'grad_s5_c_re': 'grad_w', 'grad_s5_c_im': 'grad_w', 'grad_s5_d': 'grad_w', 'grad_s5_w_glu': 'grad_w', 'grad_s5_b_glu': 'grad_w', 'grad_pool_w': 'grad_w', 'grad_pool_scale': 'grad_w', 'delta_c_ctx': 'delta_w', 'delta_w_mod': 'delta_w', 'delta_b_mod': 'delta_w', 'delta_g_pre': 'delta_w', 'delta_g_post': 'delta_w', 'delta_w_in': 'delta_w', 'delta_w_out': 'delta_w', 'delta_gla_w_gate': 'delta_w', 'delta_gla_b_gate': 'delta_w', 'delta_gla_g_norm': 'delta_w', 'delta_na_rpb': 'delta_w', 'delta_s5_lam_re': 'delta_w', 'delta_s5_lam_im': 'delta_w', 'delta_s5_log_dt': 'delta_w', 'delta_s5_b_re': 'delta_w', 'delta_s5_b_im': 'delta_w', 'delta_s5_c_re': 'delta_w', 'delta_s5_c_im': 'delta_w', 'delta_s5_d': 'delta_w', 'delta_s5_w_glu': 'delta_w', 'delta_s5_b_glu': 'delta_w', 'delta_pool_w': 'delta_w', 'delta_pool_scale': 'delta_w', 'new_m_c_ctx': 'new_m', 'new_m_w_mod': 'new_m', 'new_m_b_mod': 'new_m', 'new_m_g_pre': 'new_m', 'new_m_g_post': 'new_m', 'new_m_w_in': 'new_m', 'new_m_w_out': 'new_m', 'new_m_gla_w_gate': 'new_m', 'new_m_gla_b_gate': 'new_m', 'new_m_gla_g_norm': 'new_m', 'new_m_na_rpb': 'new_m', 'new_m_s5_lam_re': 'new_m', 'new_m_s5_lam_im': 'new_m', 'new_m_s5_log_dt': 'new_m', 'new_m_s5_b_re': 'new_m', 'new_m_s5_b_im': 'new_m', 'new_m_s5_c_re': 'new_m', 'new_m_s5_c_im': 'new_m', 'new_m_s5_d': 'new_m', 'new_m_s5_w_glu': 'new_m', 'new_m_s5_b_glu': 'new_m', 'new_m_pool_w': 'new_m', 'new_m_pool_scale': 'new_m', 'new_v_c_ctx': 'new_v', 'new_v_w_mod': 'new_v', 'new_v_b_mod': 'new_v', 'new_v_g_pre': 'new_v', 'new_v_g_post': 'new_v', 'new_v_w_in': 'new_v', 'new_v_w_out': 'new_v', 'new_v_gla_w_gate': 'new_v', 'new_v_gla_b_gate': 'new_v', 'new_v_gla_g_norm': 'new_v', 'new_v_na_rpb': 'new_v', 'new_v_s5_lam_re': 'new_v', 'new_v_s5_lam_im': 'new_v', 'new_v_s5_log_dt': 'new_v', 'new_v_s5_b_re': 'new_v', 'new_v_s5_b_im': 'new_v', 'new_v_s5_c_re': 'new_v', 'new_v_s5_c_im': 'new_v', 'new_v_s5_d': 'new_v', 'new_v_s5_w_glu': 'new_v', 'new_v_s5_b_glu': 'new_v', 'new_v_pool_w': 'new_v', 'new_v_pool_scale': 'new_v'}


def _forward(args):
    return _fwd_reference(*[args[k] for k in FWD_PARAMS])


def _output_shape():
    def fwd():
        inp = _fwd_setup_inputs(0)
        return _fwd_reference(*[inp[k] for k in FWD_PARAMS])
    out = _jax.eval_shape(fwd)
    return out.shape, out.dtype

N_MICROBATCH = 1
ADAM_LR = 0.001
ADAM_B1 = 0.9
ADAM_B2 = 0.999
ADAM_EPS = 1e-08
ADAM_WD = 0.01
ADAM_STEP = 10
PER_EXAMPLE_BATCH_AXIS = {'x': 0, 'c': 0, 'ctx': 0, 'loss_target': 0}
SHARED_INPUTS = []
_WEIGHT_DTYPES = {'c_ctx': _jnp.float32, 'w_mod': _jnp.float32, 'b_mod': _jnp.float32, 'g_pre': _jnp.float32, 'g_post': _jnp.float32, 'w_in': _jnp.float32, 'w_out': _jnp.float32, 'gla_w_gate': _jnp.float32, 'gla_b_gate': _jnp.float32, 'gla_g_norm': _jnp.float32, 'na_rpb': _jnp.float32, 's5_lam_re': _jnp.float32, 's5_lam_im': _jnp.float32, 's5_log_dt': _jnp.float32, 's5_b_re': _jnp.float32, 's5_b_im': _jnp.float32, 's5_c_re': _jnp.float32, 's5_c_im': _jnp.float32, 's5_d': _jnp.float32, 's5_w_glu': _jnp.float32, 's5_b_glu': _jnp.float32, 'pool_w': _jnp.float32, 'pool_scale': _jnp.float32}
MOMENT_SCALE = {'c_ctx': 2.668124e-02, 'w_mod': 1.228668e+00, 'b_mod': 2.654153e+00, 'g_pre': 1.470339e-01, 'g_post': 3.322620e+00, 'w_in': 9.885117e-02, 'w_out': 1.061473e-01, 'gla_w_gate': 2.060730e-02, 'gla_b_gate': 5.439645e-02, 'gla_g_norm': 4.383036e-01, 'na_rpb': 3.364025e-03, 's5_lam_re': 3.991136e-03, 's5_lam_im': 3.739347e-03, 's5_log_dt': 1.687020e+00, 's5_b_re': 2.813784e-03, 's5_b_im': 2.964879e-03, 's5_c_re': 3.951617e-03, 's5_c_im': 4.124960e-03, 's5_d': 5.955044e-02, 's5_w_glu': 1.547376e-02, 's5_b_glu': 2.377591e-02, 'pool_w': 1.215221e-01, 'pool_scale': 1.343875e-01}


def _to_microbatches(a, axis):
    t = _jnp.moveaxis(a, axis, 0)
    t = t.reshape((N_MICROBATCH, t.shape[0] // N_MICROBATCH) + t.shape[1:])
    return _jnp.moveaxis(t, 1, axis + 1)


def setup_inputs(seed: int = 0) -> dict:
    inp = _fwd_setup_inputs(seed)
    key = _jax.random.fold_in(_jax.random.key(seed), 7919)
    shape, _ = _output_shape()
    out = dict(inp)
    out["loss_target"] = _jax.random.normal(_jax.random.fold_in(key, 0), shape, _jnp.float32)
    for i, name in enumerate(TWIN_WEIGHTS):
        w = inp[name].astype(_jnp.float32)
        if MOMENT_SCALE is None:
            s = _jnp.sqrt(_jnp.mean(_jnp.square(w)) + 1e-30)
        else:
            s = MOMENT_SCALE[name]
        km, kv = _jax.random.split(_jax.random.fold_in(key, i + 1))
        out[name] = w
        out["m_" + name] = s * _jax.random.normal(km, w.shape, _jnp.float32)
        out["v_" + name] = (s * s) * _jax.random.uniform(kv, w.shape, _jnp.float32, 0.5, 1.5)
    if N_MICROBATCH > 1:
        for name, axis in PER_EXAMPLE_BATCH_AXIS.items():
            out[name] = _to_microbatches(out[name], axis)
    return {'x': out['x'], 'c': out['c'], 'ctx': out['ctx'], 'c_ctx': out['c_ctx'], 'w_mod': out['w_mod'], 'b_mod': out['b_mod'], 'g_pre': out['g_pre'], 'g_post': out['g_post'], 'w_in': out['w_in'], 'w_out': out['w_out'], 'gla_w_gate': out['gla_w_gate'], 'gla_b_gate': out['gla_b_gate'], 'gla_g_norm': out['gla_g_norm'], 'na_rpb': out['na_rpb'], 's5_lam_re': out['s5_lam_re'], 's5_lam_im': out['s5_lam_im'], 's5_log_dt': out['s5_log_dt'], 's5_b_re': out['s5_b_re'], 's5_b_im': out['s5_b_im'], 's5_c_re': out['s5_c_re'], 's5_c_im': out['s5_c_im'], 's5_d': out['s5_d'], 's5_w_glu': out['s5_w_glu'], 's5_b_glu': out['s5_b_glu'], 'pool_w': out['pool_w'], 'pool_scale': out['pool_scale'], 'loss_target': out['loss_target'], 'm_c_ctx': out['m_c_ctx'], 'm_w_mod': out['m_w_mod'], 'm_b_mod': out['m_b_mod'], 'm_g_pre': out['m_g_pre'], 'm_g_post': out['m_g_post'], 'm_w_in': out['m_w_in'], 'm_w_out': out['m_w_out'], 'm_gla_w_gate': out['m_gla_w_gate'], 'm_gla_b_gate': out['m_gla_b_gate'], 'm_gla_g_norm': out['m_gla_g_norm'], 'm_na_rpb': out['m_na_rpb'], 'm_s5_lam_re': out['m_s5_lam_re'], 'm_s5_lam_im': out['m_s5_lam_im'], 'm_s5_log_dt': out['m_s5_log_dt'], 'm_s5_b_re': out['m_s5_b_re'], 'm_s5_b_im': out['m_s5_b_im'], 'm_s5_c_re': out['m_s5_c_re'], 'm_s5_c_im': out['m_s5_c_im'], 'm_s5_d': out['m_s5_d'], 'm_s5_w_glu': out['m_s5_w_glu'], 'm_s5_b_glu': out['m_s5_b_glu'], 'm_pool_w': out['m_pool_w'], 'm_pool_scale': out['m_pool_scale'], 'v_c_ctx': out['v_c_ctx'], 'v_w_mod': out['v_w_mod'], 'v_b_mod': out['v_b_mod'], 'v_g_pre': out['v_g_pre'], 'v_g_post': out['v_g_post'], 'v_w_in': out['v_w_in'], 'v_w_out': out['v_w_out'], 'v_gla_w_gate': out['v_gla_w_gate'], 'v_gla_b_gate': out['v_gla_b_gate'], 'v_gla_g_norm': out['v_gla_g_norm'], 'v_na_rpb': out['v_na_rpb'], 'v_s5_lam_re': out['v_s5_lam_re'], 'v_s5_lam_im': out['v_s5_lam_im'], 'v_s5_log_dt': out['v_s5_log_dt'], 'v_s5_b_re': out['v_s5_b_re'], 'v_s5_b_im': out['v_s5_b_im'], 'v_s5_c_re': out['v_s5_c_re'], 'v_s5_c_im': out['v_s5_c_im'], 'v_s5_d': out['v_s5_d'], 'v_s5_w_glu': out['v_s5_w_glu'], 'v_s5_b_glu': out['v_s5_b_glu'], 'v_pool_w': out['v_pool_w'], 'v_pool_scale': out['v_pool_scale']}


def _loss(weights, diff, rest, loss_target):
    with _jax.named_scope("forward"):
        args = {**rest, TWIN_DIFF_INPUT: diff, **{k: w.astype(_WEIGHT_DTYPES[k]) for k, w in weights.items()}}
        y = _forward(args)
    with _jax.named_scope("loss_head"):
        err = _jnp.square(y.astype(_jnp.float32) - loss_target)
        return 0.5 * _jnp.sum(_jnp.mean(err, axis=-1)) if err.ndim else 0.5 * err


def _adamw(w, g, m, v):
    m = ADAM_B1 * m + (1.0 - ADAM_B1) * g
    v = ADAM_B2 * v + (1.0 - ADAM_B2) * _jnp.square(g)
    m_hat = m / (1.0 - ADAM_B1 ** ADAM_STEP)
    v_hat = v / (1.0 - ADAM_B2 ** ADAM_STEP)
    delta = -ADAM_LR * (m_hat / (_jnp.sqrt(v_hat) + ADAM_EPS) + ADAM_WD * w)
    return delta, m, v


def reference(x, c, ctx, c_ctx, w_mod, b_mod, g_pre, g_post, w_in, w_out, gla_w_gate, gla_b_gate, gla_g_norm, na_rpb, s5_lam_re, s5_lam_im, s5_log_dt, s5_b_re, s5_b_im, s5_c_re, s5_c_im, s5_d, s5_w_glu, s5_b_glu, pool_w, pool_scale, loss_target, m_c_ctx, m_w_mod, m_b_mod, m_g_pre, m_g_post, m_w_in, m_w_out, m_gla_w_gate, m_gla_b_gate, m_gla_g_norm, m_na_rpb, m_s5_lam_re, m_s5_lam_im, m_s5_log_dt, m_s5_b_re, m_s5_b_im, m_s5_c_re, m_s5_c_im, m_s5_d, m_s5_w_glu, m_s5_b_glu, m_pool_w, m_pool_scale, v_c_ctx, v_w_mod, v_b_mod, v_g_pre, v_g_post, v_w_in, v_w_out, v_gla_w_gate, v_gla_b_gate, v_gla_g_norm, v_na_rpb, v_s5_lam_re, v_s5_lam_im, v_s5_log_dt, v_s5_b_re, v_s5_b_im, v_s5_c_re, v_s5_c_im, v_s5_d, v_s5_w_glu, v_s5_b_glu, v_pool_w, v_pool_scale):
    given = dict(x=x, c=c, ctx=ctx, c_ctx=c_ctx, w_mod=w_mod, b_mod=b_mod, g_pre=g_pre, g_post=g_post, w_in=w_in, w_out=w_out, gla_w_gate=gla_w_gate, gla_b_gate=gla_b_gate, gla_g_norm=gla_g_norm, na_rpb=na_rpb, s5_lam_re=s5_lam_re, s5_lam_im=s5_lam_im, s5_log_dt=s5_log_dt, s5_b_re=s5_b_re, s5_b_im=s5_b_im, s5_c_re=s5_c_re, s5_c_im=s5_c_im, s5_d=s5_d, s5_w_glu=s5_w_glu, s5_b_glu=s5_b_glu, pool_w=pool_w, pool_scale=pool_scale, loss_target=loss_target, m_c_ctx=m_c_ctx, m_w_mod=m_w_mod, m_b_mod=m_b_mod, m_g_pre=m_g_pre, m_g_post=m_g_post, m_w_in=m_w_in, m_w_out=m_w_out, m_gla_w_gate=m_gla_w_gate, m_gla_b_gate=m_gla_b_gate, m_gla_g_norm=m_gla_g_norm, m_na_rpb=m_na_rpb, m_s5_lam_re=m_s5_lam_re, m_s5_lam_im=m_s5_lam_im, m_s5_log_dt=m_s5_log_dt, m_s5_b_re=m_s5_b_re, m_s5_b_im=m_s5_b_im, m_s5_c_re=m_s5_c_re, m_s5_c_im=m_s5_c_im, m_s5_d=m_s5_d, m_s5_w_glu=m_s5_w_glu, m_s5_b_glu=m_s5_b_glu, m_pool_w=m_pool_w, m_pool_scale=m_pool_scale, v_c_ctx=v_c_ctx, v_w_mod=v_w_mod, v_b_mod=v_b_mod, v_g_pre=v_g_pre, v_g_post=v_g_post, v_w_in=v_w_in, v_w_out=v_w_out, v_gla_w_gate=v_gla_w_gate, v_gla_b_gate=v_gla_b_gate, v_gla_g_norm=v_gla_g_norm, v_na_rpb=v_na_rpb, v_s5_lam_re=v_s5_lam_re, v_s5_lam_im=v_s5_lam_im, v_s5_log_dt=v_s5_log_dt, v_s5_b_re=v_s5_b_re, v_s5_b_im=v_s5_b_im, v_s5_c_re=v_s5_c_re, v_s5_c_im=v_s5_c_im, v_s5_d=v_s5_d, v_s5_w_glu=v_s5_w_glu, v_s5_b_glu=v_s5_b_glu, v_pool_w=v_pool_w, v_pool_scale=v_pool_scale)
    weights = {n: given[n] for n in TWIN_WEIGHTS}
    shared = {n: given[n] for n in SHARED_INPUTS}
    per_example = {n: given[n] for n in ['x', 'c', 'ctx']}
    grad_fn = _jax.value_and_grad(_loss, argnums=(0, 1))

    def one_microbatch(ex, loss_target):
        ex = dict(ex)
        diff = ex.pop(TWIN_DIFF_INPUT)
        return grad_fn(weights, diff, {**shared, **ex}, loss_target)

    if N_MICROBATCH == 1:
        loss, (grad_w, grad_x) = one_microbatch(per_example, given["loss_target"])
    else:
        def body(carry, xs):
            loss_sum, grad_sum = carry
            l_k, (gw_k, gx_k) = one_microbatch(xs[0], xs[1])
            with _jax.named_scope("update"):
                return (loss_sum + l_k, _jax.tree.map(_jnp.add, grad_sum, gw_k)), gx_k

        init = (_jnp.zeros((), _jnp.float32), _jax.tree.map(_jnp.zeros_like, weights))
        (loss, grad_w), grad_x = _jax.lax.scan(body, init, (per_example, given["loss_target"]))
    with _jax.named_scope("update"):
        delta_w, new_m, new_v = {}, {}, {}
        for n in TWIN_WEIGHTS:
            delta_w[n], new_m[n], new_v[n] = _adamw(weights[n], grad_w[n], given["m_" + n], given["v_" + n])
    return (loss, grad_x, *[grad_w[n] for n in TWIN_WEIGHTS], *[delta_w[n] for n in TWIN_WEIGHTS],
            *[new_m[n] for n in TWIN_WEIGHTS], *[new_v[n] for n in TWIN_WEIGHTS])
```

```python
import functools
import math

import numpy as np
import jax
import jax.numpy as jnp
from jax import lax
from jax.experimental import pallas as pl
from jax.experimental.pallas import tpu as pltpu

F32 = jnp.float32
BF16 = jnp.bfloat16
HIGHEST = lax.Precision.HIGHEST

D = 1024
GRID_W = 64
EPS = 1e-6
N_DEV = 8
C_GT, C_GV, C_NK, C_NV, C_SU, C_NQ, C_PU, C_GK, C_GG, C_GQ, C_END = 0, 1024, 1280, 1536, 1792, 2048, 2304, 2560, 2688, 2816, 2944
PW = 3072
N_CTX_ORIG = 416
N_IN = 2848
GLA_CHUNK = 64
S5_CHUNK = 256
ROW_TILE = 256
VMEM_LIMIT = 56 * 1024 * 1024

ADAM_LR, ADAM_B1, ADAM_B2, ADAM_EPS, ADAM_WD, ADAM_STEP = 0.001, 0.9, 0.999, 1e-08, 0.01, 10


def _cparams(**kw):
    return pltpu.CompilerParams(vmem_limit_bytes=VMEM_LIMIT, **kw)


def _dg(a, b, ca, cb, precision=None):
    return lax.dot_general(a, b, (((ca,), (cb,)), ((), ())), precision=precision, preferred_element_type=F32)


def hdot(a, b):
    return _dg(a, b, 1, 0, HIGHEST)


def hdot_nt(a, b):
    return _dg(a, b, 1, 1, HIGHEST)


def hdot_tn(a, b):
    return _dg(a, b, 0, 0, HIGHEST)


def b_nn(a, b):
    return _dg(a.astype(BF16), b.astype(BF16), 1, 0)


def b_nt(a, b):
    return _dg(a.astype(BF16), b.astype(BF16), 1, 1)


def b_tn(a, b):
    return _dg(a.astype(BF16), b.astype(BF16), 0, 0)


@jax.custom_vjp
def bdot(a, b):
    return b_nn(a, b)


def _bdot_fwd(a, b):
    return b_nn(a, b), (a, b)


def _bdot_bwd(res, ct):
    a, b = res
    return b_nt(ct, b).astype(a.dtype), b_tn(a, ct).astype(b.dtype)


bdot.defvjp(_bdot_fwd, _bdot_bwd)


def _log_sigmoid(z):
    return jnp.minimum(z, 0.0) - jnp.log(1.0 + jnp.exp(-jnp.abs(z)))


def _silu(z):
    return z * jax.nn.sigmoid(z)


def _gelu(z):
    return 0.5 * z * (1.0 + jnp.tanh(math.sqrt(2.0 / math.pi) * (z + 0.044715 * (z * z * z))))


def _cat(vals):
    return vals[0] if len(vals) == 1 else jnp.concatenate(vals, axis=-1)


def mm_nn(a_parts, b, name, tm=ROW_TILE, tn=1024):
    t = a_parts[0].shape[0]
    k, n = b.shape
    na = len(a_parts)
    tn = min(tn, n)

    def body(*refs):
        a = _cat([r[...].astype(BF16) for r in refs[:na]])
        refs[na + 1][...] = _dg(a, refs[na][...].astype(BF16), 1, 0)

    return pl.pallas_call(
        body, name=name, grid=(n // tn, t // tm),
        in_specs=[pl.BlockSpec((tm, p.shape[1]), lambda j, i: (i, 0)) for p in a_parts]
        + [pl.BlockSpec((k, tn), lambda j, i: (0, j))],
        out_specs=pl.BlockSpec((tm, tn), lambda j, i: (i, j)),
        out_shape=jax.ShapeDtypeStruct((t, n), F32),
        compiler_params=_cparams(dimension_semantics=("arbitrary", "arbitrary")),
    )(*a_parts, b)


def mm_nt(a_parts, b, name, tm=ROW_TILE):
    t = a_parts[0].shape[0]
    n, k = b.shape
    na = len(a_parts)

    def body(*refs):
        a = _cat([r[...].astype(BF16) for r in refs[:na]])
        refs[na + 1][...] = _dg(a, refs[na][...].astype(BF16), 1, 1)

    return pl.pallas_call(
        body, name=name, grid=(t // tm,),
        in_specs=[pl.BlockSpec((tm, p.shape[1]), lambda i: (i, 0)) for p in a_parts]
        + [pl.BlockSpec((n, k), lambda i: (0, 0))],
        out_specs=pl.BlockSpec((tm, n), lambda i: (i, 0)),
        out_shape=jax.ShapeDtypeStruct((t, n), F32),
        compiler_params=_cparams(dimension_semantics=("arbitrary",)),
    )(*a_parts, b)


def mm_tn(a, b_parts, name, tm=ROW_TILE, tn=1024):
    t, k = a.shape
    widths = [p.shape[1] for p in b_parts]
    n = sum(widths)
    assert n % tn == 0
    groups, cur, acc = [], [], 0
    for idx, w in enumerate(widths):
        cur.append(idx)
        acc += w
        if acc == tn:
            groups.append(cur)
            cur, acc = [], 0
        assert acc < tn
    assert not cur
    outs = []
    for gi, grp in enumerate(groups):
        parts = [b_parts[i] for i in grp]
        npart = len(parts)
        nsteps = t // tm

        def body(*refs, npart=npart):
            a_v = refs[0][...].astype(BF16)
            b_v = _cat([r[...].astype(BF16) for r in refs[1:1 + npart]])
            o_ref = refs[1 + npart]
            r = _dg(a_v, b_v, 0, 0)

            @pl.when(pl.program_id(0) == 0)
            def _():
                o_ref[...] = r

            @pl.when(pl.program_id(0) != 0)
            def _():
                o_ref[...] += r

        outs.append(pl.pallas_call(
            body, name=f"{name}_{gi}", grid=(nsteps,),
            in_specs=[pl.BlockSpec((tm, k), lambda i: (i, 0))]
            + [pl.BlockSpec((tm, p.shape[1]), lambda i: (i, 0)) for p in parts],
            out_specs=pl.BlockSpec((k, tn), lambda i: (0, 0)),
            out_shape=jax.ShapeDtypeStruct((k, tn), F32),
            compiler_params=_cparams(dimension_semantics=("arbitrary",)),
        )(a, *parts))
    return outs


def _seg_of(i, nct):
    return jnp.where(i < nct, 1, 0)


def rowwise_fwd(fn, name, rows, segs, globs, out_widths, tile, nct):
    t = rows[0].shape[0]
    nr, ns, ng = len(rows), len(segs), len(globs)

    def body(*refs):
        vals = [r[...] for r in refs[:nr]] + [r[0] for r in refs[nr:nr + ns]] + [r[...] for r in refs[nr + ns:nr + ns + ng]]
        outs = fn(*vals)
        for o_ref, o in zip(refs[nr + ns + ng:], outs):
            o_ref[...] = o

    return pl.pallas_call(
        body, name=name, grid=(t // tile,),
        in_specs=[pl.BlockSpec((tile, r.shape[1]), lambda i: (i, 0)) for r in rows]
        + [pl.BlockSpec((1, 1, s.shape[2]), lambda i: (_seg_of(i, nct), 0, 0)) for s in segs]
        + [pl.BlockSpec(g.shape, lambda i: (0, 0)) for g in globs],
        out_specs=[pl.BlockSpec((tile, w), lambda i: (i, 0)) for w in out_widths],
        out_shape=[jax.ShapeDtypeStruct((t, w), F32) for w in out_widths],
        compiler_params=_cparams(dimension_semantics=("arbitrary",)),
    )(*rows, *segs, *globs)


def rowwise_bwd(fn, name, rows, segs, globs, cts, tile, nct, row_diff, glob_diff):
    t = rows[0].shape[0]
    nr, ns, ng, nc = len(rows), len(segs), len(globs), len(cts)
    d_rows = [i for i in range(nr) if row_diff[i]]
    d_globs = [i for i in range(ng) if glob_diff[i]]

    def body(*refs):
        in_refs, out_refs = refs[:nr + ns + ng + nc], refs[nr + ns + ng + nc:]
        row_v = [r[...] for r in in_refs[:nr]]
        seg_v = [r[0] for r in in_refs[nr:nr + ns]]
        glob_v = [r[...] for r in in_refs[nr + ns:nr + ns + ng]]
        ct_v = tuple(r[...] for r in in_refs[nr + ns + ng:])

        def wrapped(dr, sv, dg):
            rv = list(row_v)
            for j, i in enumerate(d_rows):
                rv[i] = dr[j]
            gv = list(glob_v)
            for j, i in enumerate(d_globs):
                gv[i] = dg[j]
            return tuple(fn(*rv, *sv, *gv))

        _, vjp = jax.vjp(wrapped, [row_v[i] for i in d_rows], seg_v, [glob_v[i] for i in d_globs])
        c_rows, c_segs, c_globs = vjp(ct_v)
        i = pl.program_id(0)
        k = 0
        for c in c_rows:
            out_refs[k][...] = c
            k += 1
        seg_first = jnp.logical_or(i == 0, i == nct)
        for c in c_segs:
            ref = out_refs[k]
            k += 1

            @pl.when(seg_first)
            def _(ref=ref, c=c):
                ref[0] = c

            @pl.when(jnp.logical_not(seg_first))
            def _(ref=ref, c=c):
                ref[0] += c
        for c in c_globs:
            ref = out_refs[k]
            k += 1

            @pl.when(i == 0)
            def _(ref=ref, c=c):
                ref[...] = c

            @pl.when(i != 0)
            def _(ref=ref, c=c):
                ref[...] += c

    return pl.pallas_call(
        body, name=name, grid=(t // tile,),
        in_specs=[pl.BlockSpec((tile, r.shape[1]), lambda i: (i, 0)) for r in rows]
        + [pl.BlockSpec((1, 1, s.shape[2]), lambda i: (_seg_of(i, nct), 0, 0)) for s in segs]
        + [pl.BlockSpec(g.shape, lambda i: (0, 0)) for g in globs]
        + [pl.BlockSpec((tile, c.shape[1]), lambda i: (i, 0)) for c in cts],
        out_specs=[pl.BlockSpec((tile, rows[i].shape[1]), lambda i: (i, 0)) for i in d_rows]
        + [pl.BlockSpec((1, 1, s.shape[2]), lambda i: (_seg_of(i, nct), 0, 0)) for s in segs]
        + [pl.BlockSpec(globs[i].shape, lambda i: (0, 0)) for i in d_globs],
        out_shape=[jax.ShapeDtypeStruct(rows[i].shape, F32) for i in d_rows]
        + [jax.ShapeDtypeStruct(s.shape, F32) for s in segs]
        + [jax.ShapeDtypeStruct(globs[i].shape, F32) for i in d_globs],
        compiler_params=_cparams(dimension_semantics=("arbitrary",)),
    )(*rows, *segs, *globs, *cts)


def f_pre(x, mod, g_pre):
    shift, scale = mod[:, :D], mod[:, D:2 * D]
    rs = lax.rsqrt(jnp.mean(x * x, axis=-1, keepdims=True) + EPS)
    return ((x * rs) * g_pre * (1.0 + scale) + shift,)


def f_post(x, out, mod, g_post):
    gate = mod[:, 2 * D:]
    rs = lax.rsqrt(jnp.mean(out * out, axis=-1, keepdims=True) + EPS)
    return (x + gate * ((out * rs) * g_post),)


def f_mix(o_gla, o_na, y5, u5, pm, gcols, g_norm, s5_d, w_glu, b_glu, wpool, pool_scale, havg, e4):
    ms = hdot(o_gla * o_gla, havg)
    y_gla = o_gla * lax.rsqrt(ms + EPS) * jnp.sum(hdot(g_norm, e4), axis=0, keepdims=True)
    g = _gelu(u5 * s5_d + y5)
    y_s5 = g * jax.nn.sigmoid(bdot(g, w_glu) + b_glu)
    y_pool = bdot(pm, wpool) * pool_scale
    ycat = jnp.concatenate([y_gla, o_na, y_s5, y_pool], axis=-1)
    return (ycat * _silu(gcols),)


@jax.custom_vjp
def _rot_half16(x):
    lane = lax.broadcasted_iota(jnp.int32, x.shape, 1)
    first = jnp.bitwise_and(lane, 15) < 8
    return jnp.where(first, -pltpu.roll(x, x.shape[1] - 8, 1), pltpu.roll(x, 8, 1))


def _rot_fwd(x):
    return _rot_half16(x), None


def _rot_bwd(_, ct):
    return (-_rot_half16(ct),)


_rot_half16.defvjp(_rot_fwd, _rot_bwd)


def f_gla_prep(pk, pg, pq, cos, sin, wg, bg):
    z = bdot(pg, wg) + bg
    lg = _log_sigmoid(z) * (1.0 / 16.0)
    k_r = pk * cos + _rot_half16(pk) * sin
    q_r = (pq * cos + _rot_half16(pq) * sin) * (32.0 ** -0.5)
    return q_r, k_r, lg[:, :128], lg[:, 128:]


def _gla_consts(rev):
    c = GLA_CHUNK
    i = np.arange(c)
    inc = (i[None, :] >= i[:, None]) if rev else (i[None, :] <= i[:, None])
    mq = np.stack([(np.arange(128) // 32 == h) for h in range(4)]).astype(np.float32).reshape(4, 1, 128)
    mv = np.stack([(np.arange(256) // 64 == h) for h in range(4)]).astype(np.float32).reshape(4, 1, 256)
    bdt = (np.arange(256)[:, None] // 64 == np.arange(128)[None, :] // 32).astype(np.float32)
    return jnp.asarray(inc.astype(np.float32)), jnp.asarray(mq), jnp.asarray(mv), jnp.asarray(bdt)


def _gla_chunk_of(s, n_ctx_chunks, n_chunks, rev):
    if not rev:
        return s
    return jnp.where(s < n_ctx_chunks, n_ctx_chunks - 1 - s, n_ctx_chunks + n_chunks - 1 - s)


def gla_scan_fwd(q, k, v, lg, acc, n_ctx_rows, rev, name):
    t = q.shape[0]
    nch, ncc = t // GLA_CHUNK, n_ctx_rows // GLA_CHUNK
    inc, mq, mv, bdt = _gla_consts(rev)

    def body(q_ref, k_ref, v_ref, lg_ref, acc_ref, inc_ref, mq_ref, mv_ref, bdt_ref, o_ref, st_ref):
        lmask = inc_ref[...]
        bd = bdt_ref[...]

        def step(s, st):
            c = _gla_chunk_of(s, ncc, nch, rev)
            rows = pl.ds(pl.multiple_of(c * GLA_CHUNK, GLA_CHUNK), GLA_CHUNK)
            qc, kc, vc, lgc = q_ref[rows, :], k_ref[rows, :], v_ref[rows, :], lg_ref[rows, :]
            st_ref[c] = st
            b = hdot(lmask, lgc)
            blast = jnp.sum(lgc, axis=0, keepdims=True)
            qe, ke, kd = qc * jnp.exp(b), kc * jnp.exp(-b), kc * jnp.exp(blast - b)
            o = acc_ref[rows, :] + hdot_nt(qe, st)
            for h in range(4):
                a = lmask * hdot_nt(qe * mq_ref[h], ke)
                o = o + hdot(a, vc * mv_ref[h])
            o_ref[rows, :] = o
            return st * jnp.exp(blast) + bd * hdot_tn(vc, kd)

        lax.fori_loop(0, nch, step, jnp.zeros((256, 128), F32))

    vm = pl.BlockSpec(memory_space=pltpu.VMEM)
    return pl.pallas_call(
        body, name=name, in_specs=[vm] * 9, out_specs=[vm, vm],
        out_shape=[jax.ShapeDtypeStruct((t, 256), F32), jax.ShapeDtypeStruct((nch, 256, 128), F32)],
        compiler_params=_cparams(),
    )(q, k, v, lg, acc, inc, mq, mv, bdt)


def gla_scan_bwd(q, k, v, lg, st, do, acc, n_ctx_rows, rev, name):
    t = q.shape[0]
    nch, ncc = t // GLA_CHUNK, n_ctx_rows // GLA_CHUNK
    inc, mq, mv, bdt = _gla_consts(rev)

    def body(q_ref, k_ref, v_ref, lg_ref, st_ref, do_ref, aq_ref, ak_ref, av_ref, inc_ref, mq_ref, mv_ref, bdt_ref,
             dq_ref, dk_ref, dv_ref, dlg_ref):
        lmask = inc_ref[...]
        bd = bdt_ref[...]

        def step(j, carry):
            dst, gsum = carry
            s = nch - 1 - j
            c = _gla_chunk_of(s, ncc, nch, rev)
            rows = pl.ds(pl.multiple_of(c * GLA_CHUNK, GLA_CHUNK), GLA_CHUNK)
            qc, kc, vc, lgc, doc = q_ref[rows, :], k_ref[rows, :], v_ref[rows, :], lg_ref[rows, :], do_ref[rows, :]
            stc = st_ref[c]
            b = hdot(lmask, lgc)
            blast = jnp.sum(lgc, axis=0, keepdims=True)
            eb, enb, edb = jnp.exp(b), jnp.exp(-b), jnp.exp(blast - b)
            qe, ke, kd = qc * eb, kc * enb, kc * edb
            dqe = hdot(doc, stc)
            dke = jnp.zeros_like(ke)
            dv = hdot_nt(kd, dst)
            for h in range(4):
                qh = qe * mq_ref[h]
                a = lmask * hdot_nt(qh, ke)
                doh = doc * mv_ref[h]
                da = lmask * hdot_nt(doh, vc)
                dqe = dqe + mq_ref[h] * hdot(da, ke)
                dke = dke + mq_ref[h] * hdot_tn(da, qe)
                dv = dv + mv_ref[h] * hdot_tn(a, doc)
            dkd = hdot(vc, dst)
            dq = dqe * eb
            dk = dke * enb + dkd * edb
            g = qc * dq - kc * dk
            dlg_ref[rows, :] = hdot_tn(lmask, g) + gsum
            dq_ref[rows, :] = aq_ref[rows, :] + dq
            dk_ref[rows, :] = ak_ref[rows, :] + dk
            dv_ref[rows, :] = av_ref[rows, :] + dv
            dst_new = dst * jnp.exp(blast) + bd * hdot_tn(doc, qe)
            return dst_new, gsum + jnp.sum(g, axis=0, keepdims=True)

        lax.fori_loop(0, nch, step, (jnp.zeros((256, 128), F32), jnp.zeros((1, 128), F32)))

    vm = pl.BlockSpec(memory_space=pltpu.VMEM)
    return pl.pallas_call(
        body, name=name, in_specs=[vm] * 13, out_specs=[vm] * 4,
        out_shape=[jax.ShapeDtypeStruct((t, 128), F32), jax.ShapeDtypeStruct((t, 128), F32),
                   jax.ShapeDtypeStruct((t, 256), F32), jax.ShapeDtypeStruct((t, 128), F32)],
        compiler_params=_cparams(),
    )(q, k, v, lg, st, do, *acc, inc, mq, mv, bdt)


def whole_fwd(fn, name, args, out_shapes):
    def body(*refs):
        outs = fn(*[r[...] for r in refs[:len(args)]])
        for o_ref, o in zip(refs[len(args):], outs):
            o_ref[...] = o

    vm = pl.BlockSpec(memory_space=pltpu.VMEM)
    return pl.pallas_call(
        body, name=name, in_specs=[vm] * len(args), out_specs=[vm] * len(out_shapes),
        out_shape=[jax.ShapeDtypeStruct(s, F32) for s in out_shapes], compiler_params=_cparams(),
    )(*args)


def whole_bwd(fn, name, args, cts, diff):
    d_idx = [i for i in range(len(args)) if diff[i]]

    def body(*refs):
        vals = [r[...] for r in refs[:len(args)]]
        ct_v = tuple(r[...] for r in refs[len(args):len(args) + len(cts)])

        def wrapped(dv):
            av = list(vals)
            for j, i in enumerate(d_idx):
                av[i] = dv[j]
            return tuple(fn(*av))

        _, vjp = jax.vjp(wrapped, [vals[i] for i in d_idx])
        (c_args,) = vjp(ct_v)
        for o_ref, c in zip(refs[len(args) + len(cts):], c_args):
            o_ref[...] = c

    vm = pl.BlockSpec(memory_space=pltpu.VMEM)
    return pl.pallas_call(
        body, name=name, in_specs=[vm] * (len(args) + len(cts)), out_specs=[vm] * len(d_idx),
        out_shape=[jax.ShapeDtypeStruct(args[i].shape, F32) for i in d_idx], compiler_params=_cparams(),
    )(*args, *cts)


def _s5_consts():
    e_rep = (np.arange(256)[:, None] // 16 == np.arange(16)[None, :]).astype(np.float32)
    e_tile = (np.arange(64)[:, None] == np.arange(1024)[None, :] % 64).astype(np.float32)
    gmask = (np.arange(16)[:, None] == np.arange(1024)[None, :] // 64).astype(np.float32)
    bdm = (np.arange(256)[:, None] // 16 == np.arange(1024)[None, :] // 64).astype(np.float32)
    return jnp.asarray(e_rep), jnp.asarray(e_tile), jnp.asarray(gmask), jnp.asarray(bdm)


def f_s5_params(lam_re, lam_im, log_dt, bt_re, bt_im, ct_re, ct_im, e_rep, e_tile, gmask, bdm):
    dt = jnp.exp(log_dt)
    mag = jnp.exp(lam_re * dt)
    ang = lam_im * dt
    lb_re, lb_im = mag * jnp.cos(ang), mag * jnp.sin(ang)
    num_re, num_im = lb_re - 1.0, lb_im
    den = lam_re * lam_re + lam_im * lam_im
    coef_re = (num_re * lam_re + num_im * lam_im) / den
    coef_im = (num_im * lam_re - num_re * lam_im) / den
    cr, ci = hdot(e_rep, coef_re), hdot(e_rep, coef_im)
    bbt_re = cr * bt_re - ci * bt_im
    bbt_im = cr * bt_im + ci * bt_re
    a_re = jnp.sum(hdot(lb_re, e_tile) * gmask, axis=0, keepdims=True)
    a_im = jnp.sum(hdot(lb_im, e_tile) * gmask, axis=0, keepdims=True)
    return (a_re, a_im, hdot(bbt_re, e_tile) * bdm, hdot(bbt_im, e_tile) * bdm,
            hdot(ct_re, e_tile) * bdm, hdot(ct_im, e_tile) * bdm)


def _s5_scan(xr, xi, a_re, a_im, rev, chunk):
    row = lax.broadcasted_iota(jnp.int32, xr.shape, 0)
    pr, pi = a_re, a_im
    s = 1
    while s < chunk:
        if rev:
            keep = row < (chunk - s)
            sr, si = pltpu.roll(xr, chunk - s, 0), pltpu.roll(xi, chunk - s, 0)
        else:
            keep = row >= s
            sr, si = pltpu.roll(xr, s, 0), pltpu.roll(xi, s, 0)
        sr, si = jnp.where(keep, sr, 0.0), jnp.where(keep, si, 0.0)
        xr, xi = xr + pr * sr - pi * si, xi + pr * si + pi * sr
        pr, pi = pr * pr - pi * pi, 2.0 * pr * pi
        s *= 2
    return xr, xi


def _s5_chunk_states(u_c, x0r, x0i, a_re, a_im, bb_re, bb_im, rev, chunk):
    row = lax.broadcasted_iota(jnp.int32, (chunk, 1024), 0)
    first = row == (chunk - 1 if rev else 0)
    inj_r = a_re * x0r - a_im * x0i
    inj_i = a_re * x0i + a_im * x0r
    xr = b_nn(u_c, bb_re) + jnp.where(first, inj_r, 0.0)
    xi = b_nn(u_c, bb_im) + jnp.where(first, inj_i, 0.0)
    return _s5_scan(xr, xi, a_re, a_im, rev, chunk)


def _row_pick(x, idx):
    row = lax.broadcasted_iota(jnp.int32, x.shape, 0)
    return jnp.sum(jnp.where(row == idx, x, 0.0), axis=0, keepdims=True)


def s5_scan_fwd(u, acc, a_re, a_im, bb_re, bb_im, cc_re, cc_im, n_ctx_rows, chunk, rev, name):
    t = u.shape[0]
    nch, ncc = t // chunk, n_ctx_rows // chunk

    def body(u_ref, acc_ref, ar_ref, ai_ref, br_ref, bi_ref, cr_ref, ci_ref, y_ref, x0r_ref, x0i_ref):
        a_r, a_i = ar_ref[...], ai_ref[...]

        def step(s, carry):
            x0r, x0i = carry
            c = _gla_chunk_of(s, ncc, nch, rev)
            rows = pl.ds(pl.multiple_of(c * chunk, chunk), chunk)
            x0r_ref[c] = x0r
            x0i_ref[c] = x0i
            xr, xi = _s5_chunk_states(u_ref[rows, :], x0r, x0i, a_r, a_i, br_ref[...], bi_ref[...], rev, chunk)
            y_ref[rows, :] = acc_ref[rows, :] + b_nt(xr, cr_ref[...]) - b_nt(xi, ci_ref[...])
            last = 0 if rev else chunk - 1
            return _row_pick(xr, last), _row_pick(xi, last)

        lax.fori_loop(0, nch, step, (jnp.zeros((1, 1024), F32), jnp.zeros((1, 1024), F32)))

    vm = pl.BlockSpec(memory_space=pltpu.VMEM)
    return pl.pallas_call(
        body, name=name, in_specs=[vm] * 8, out_specs=[vm] * 3,
        out_shape=[jax.ShapeDtypeStruct((t, 256), F32), jax.ShapeDtypeStruct((nch, 1, 1024), F32),
                   jax.ShapeDtypeStruct((nch, 1, 1024), F32)],
        compiler_params=_cparams(),
    )(u, acc, a_re, a_im, bb_re, bb_im, cc_re, cc_im)


def s5_scan_bwd(u, dy, du_acc, x0r, x0i, a_re, a_im, bb_re, bb_im, cc_re, cc_im, n_ctx_rows, chunk, rev, name):
    t = u.shape[0]
    nch, ncc = t // chunk, n_ctx_rows // chunk

    def body(u_ref, dy_ref, dua_ref, x0r_ref, x0i_ref, ar_ref, ai_ref, br_ref, bi_ref, cr_ref, ci_ref,
             du_ref, dar_ref, dai_ref, dbr_ref, dbi_ref, dcr_ref, dci_ref):
        a_r, a_i = ar_ref[...], ai_ref[...]
        for ref in (dbr_ref, dbi_ref, dcr_ref, dci_ref):
            ref[...] = jnp.zeros_like(ref)
        row = lax.broadcasted_iota(jnp.int32, (chunk, 1024), 0)
        first_idx, last_idx = (chunk - 1, 0) if rev else (0, chunk - 1)

        def step(j, carry):
            lcr, lci, dar, dai = carry
            s = nch - 1 - j
            c = _gla_chunk_of(s, ncc, nch, rev)
            rows = pl.ds(pl.multiple_of(c * chunk, chunk), chunk)
            u_c, dy_c = u_ref[rows, :], dy_ref[rows, :]
            x0r_c, x0i_c = x0r_ref[c], x0i_ref[c]
            xr, xi = _s5_chunk_states(u_c, x0r_c, x0i_c, a_r, a_i, br_ref[...], bi_ref[...], rev, chunk)
            dcr_ref[...] += b_tn(dy_c, xr)
            dci_ref[...] -= b_tn(dy_c, xi)
            inj_r = a_r * lcr + a_i * lci
            inj_i = a_r * lci - a_i * lcr
            is_last = row == last_idx
            lr = b_nn(dy_c, cr_ref[...]) + jnp.where(is_last, inj_r, 0.0)
            li = -b_nn(dy_c, ci_ref[...]) + jnp.where(is_last, inj_i, 0.0)
            lr, li = _s5_scan(lr, li, a_r, -a_i, not rev, chunk)
            du_ref[rows, :] = dua_ref[rows, :] + b_nt(lr, br_ref[...]) + b_nt(li, bi_ref[...])
            dbr_ref[...] += b_tn(u_c, lr)
            dbi_ref[...] += b_tn(u_c, li)
            if rev:
                pr, pi = pltpu.roll(xr, chunk - 1, 0), pltpu.roll(xi, chunk - 1, 0)
            else:
                pr, pi = pltpu.roll(xr, 1, 0), pltpu.roll(xi, 1, 0)
            is_first = row == first_idx
            pr, pi = jnp.where(is_first, x0r_c, pr), jnp.where(is_first, x0i_c, pi)
            dar = dar + jnp.sum(lr * pr + li * pi, axis=0, keepdims=True)
            dai = dai + jnp.sum(li * pr - lr * pi, axis=0, keepdims=True)
            return _row_pick(lr, first_idx), _row_pick(li, first_idx), dar, dai

        z = jnp.zeros((1, 1024), F32)
        _, _, dar, dai = lax.fori_loop(0, nch, step, (z, z, z, z))
        dar_ref[...] = dar
        dai_ref[...] = dai

    vm = pl.BlockSpec(memory_space=pltpu.VMEM)
    big = jax.ShapeDtypeStruct((256, 1024), F32)
    vec = jax.ShapeDtypeStruct((1, 1024), F32)
    return pl.pallas_call(
        body, name=name, in_specs=[vm] * 11, out_specs=[vm] * 7,
        out_shape=[jax.ShapeDtypeStruct((t, 256), F32), vec, vec, big, big, big, big],
        compiler_params=_cparams(),
    )(u, dy, du_acc, x0r, x0i, a_re, a_im, bb_re, bb_im, cc_re, cc_im)


POOL_HALO = 8


def pool_apply(u_pad, n, transpose, name, tile=ROW_TILE):
    tile = min(tile, n)
    ext = tile + 2 * POOL_HALO

    def body(u_ref, o_ref):
        lax.fori_loop(0, n // tile, functools.partial(step, u_ref, o_ref), 0)

    def step(u_ref, o_ref, i, carry):
        val = u_ref[pl.ds(pl.multiple_of(i * tile, tile), ext), :]
        lane = lax.broadcasted_iota(jnp.int32, (ext, 256), 1)
        half = jnp.left_shift(1, jnp.right_shift(lane, 6))
        trow = lax.broadcasted_iota(jnp.int32, (ext, 256), 0) + (i * tile - POOL_HALO)
        cnt = jnp.minimum(trow + half, n) - jnp.maximum(trow - half, 0)
        inv = 1.0 / jnp.maximum(cnt, 1).astype(F32)
        src = val * inv if transpose else val
        acc = jnp.zeros((tile, 256), F32)
        for d in range(-POOL_HALO, POOL_HALO):
            in_win = jnp.logical_and(d >= -half, d <= half - 1)[POOL_HALO:POOL_HALO + tile]
            shift = d if transpose else -d
            rolled = pltpu.roll(src, shift % ext, 0)[POOL_HALO:POOL_HALO + tile]
            acc = acc + jnp.where(in_win, rolled, 0.0)
        centre = val[POOL_HALO:POOL_HALO + tile]
        if not transpose:
            acc = acc * inv[POOL_HALO:POOL_HALO + tile]
        o_ref[pl.ds(pl.multiple_of(i * tile, tile), tile), :] = acc - centre
        return carry

    vm = pl.BlockSpec(memory_space=pltpu.VMEM)
    return pl.pallas_call(
        body, name=name, in_specs=[vm], out_specs=vm,
        out_shape=jax.ShapeDtypeStruct((n, 256), F32), compiler_params=_cparams(),
    )(u_pad)


NA_SCALE = 64.0 ** -0.5
NEG = -1e30


def _na_head_masks():
    return jnp.asarray(np.stack([(np.arange(256) // 64 == h) for h in range(4)]).astype(np.float32).reshape(4, 1, 256))


def _na_window(r, rows):
    start = jnp.clip(r - 4, 0, rows - 8)
    return start, start - r + 7


def _na_probs(qh, kw, kc, bias):
    s_c = b_nt(qh, kc)
    m = jnp.max(s_c, axis=-1, keepdims=True)
    if kw is not None:
        s_w = b_nt(qh, kw) + bias
        m = jnp.maximum(m, jnp.max(s_w, axis=-1, keepdims=True))
        p_w = jnp.exp(s_w - m)
    p_c = jnp.exp(s_c - m)
    l = jnp.sum(p_c, axis=-1, keepdims=True)
    if kw is not None:
        l = l + jnp.sum(p_w, axis=-1, keepdims=True)
        return p_w / l, p_c / l
    return None, p_c / l


def na_fwd(q, k, v, bias8, n_ctx_rows, name):
    t = q.shape[0]
    m_ctx = n_ctx_rows
    rows = (t - m_ctx) // GRID_W
    hm = _na_head_masks()

    def body(q_ref, k_ref, v_ref, b_ref, hm_ref, o_ref):
        kc, vc = k_ref[0:m_ctx, :], v_ref[0:m_ctx, :]

        def ctx_step(i, _):
            rs = pl.ds(pl.multiple_of(i * 64, 64), 64)
            qr = q_ref[rs, :] * NA_SCALE
            o = jnp.zeros((64, 256), F32)
            for h in range(4):
                _, p_c = _na_probs(qr * hm_ref[h], None, kc, None)
                o = o + b_nn(p_c, vc * hm_ref[h])
            o_ref[rs, :] = o
            return 0

        lax.fori_loop(0, m_ctx // 64, ctx_step, 0)

        def lat_step(r, _):
            start, off = _na_window(r, rows)
            rs = pl.ds(pl.multiple_of(m_ctx + r * 64, 64), 64)
            ws = pl.ds(pl.multiple_of(m_ctx + start * 64, 64), 512)
            qr = q_ref[rs, :] * NA_SCALE
            kw, vw = k_ref[ws, :], v_ref[ws, :]
            o = jnp.zeros((64, 256), F32)
            for h in range(4):
                p_w, p_c = _na_probs(qr * hm_ref[h], kw, kc, b_ref[h, off])
                o = o + b_nn(p_w, vw * hm_ref[h]) + b_nn(p_c, vc * hm_ref[h])
            o_ref[rs, :] = o
            return 0

        lax.fori_loop(0, rows, lat_step, 0)

    vm = pl.BlockSpec(memory_space=pltpu.VMEM)
    return pl.pallas_call(
        body, name=name, in_specs=[vm] * 5, out_specs=vm,
        out_shape=jax.ShapeDtypeStruct((t, 256), F32), compiler_params=_cparams(),
    )(q, k, v, bias8, hm)


def na_bwd(q, k, v, do, bias8, n_ctx_rows, name):
    t = q.shape[0]
    m_ctx = n_ctx_rows
    rows = (t - m_ctx) // GRID_W
    hm = _na_head_masks()

    def body(q_ref, k_ref, v_ref, do_ref, b_ref, hm_ref, dq_ref, dk_ref, dv_ref, db_ref):
        kc, vc = k_ref[0:m_ctx, :], v_ref[0:m_ctx, :]
        dk_ref[...] = jnp.zeros_like(dk_ref)
        dv_ref[...] = jnp.zeros_like(dv_ref)
        db_ref[...] = jnp.zeros_like(db_ref)

        def head_terms(qh, doh, kw, vw, bias):
            p_w, p_c = _na_probs(qh, kw, kc, bias)
            dp_c = b_nt(doh, vc)
            delta = jnp.sum(p_c * dp_c, axis=-1, keepdims=True)
            if kw is not None:
                dp_w = b_nt(doh, vw)
                delta = delta + jnp.sum(p_w * dp_w, axis=-1, keepdims=True)
                ds_w = p_w * (dp_w - delta)
            else:
                ds_w = None
            ds_c = p_c * (dp_c - delta)
            return p_w, p_c, ds_w, ds_c

        def ctx_step(i, carry):
            dkc, dvc = carry
            rs = pl.ds(pl.multiple_of(i * 64, 64), 64)
            qr, dor = q_ref[rs, :] * NA_SCALE, do_ref[rs, :]
            dq = jnp.zeros((64, 256), F32)
            for h in range(4):
                qh, doh = qr * hm_ref[h], dor * hm_ref[h]
                _, p_c, _, ds_c = head_terms(qh, doh, None, None, None)
                dq = dq + hm_ref[h] * b_nn(ds_c, kc)
                dkc = dkc + b_tn(ds_c, qh)
                dvc = dvc + b_tn(p_c, doh)
            dq_ref[rs, :] = dq * NA_SCALE
            return dkc, dvc

        zc = jnp.zeros((m_ctx, 256), F32)
        carry = lax.fori_loop(0, m_ctx // 64, ctx_step, (zc, zc))

        def lat_step(r, carry):
            dkc, dvc = carry
            start, off = _na_window(r, rows)
            rs = pl.ds(pl.multiple_of(m_ctx + r * 64, 64), 64)
            ws = pl.ds(pl.multiple_of(m_ctx + start * 64, 64), 512)
            qr, dor = q_ref[rs, :] * NA_SCALE, do_ref[rs, :]
            kw, vw = k_ref[ws, :], v_ref[ws, :]
            dq = jnp.zeros((64, 256), F32)
            dkw = jnp.zeros((512, 256), F32)
            dvw = jnp.zeros((512, 256), F32)
            for h in range(4):
                qh, doh = qr * hm_ref[h], dor * hm_ref[h]
                p_w, p_c, ds_w, ds_c = head_terms(qh, doh, kw, vw, b_ref[h, off])
                dq = dq + hm_ref[h] * (b_nn(ds_w, kw) + b_nn(ds_c, kc))
                dkw = dkw + b_tn(ds_w, qh)
                dvw = dvw + b_tn(p_w, doh)
                dkc = dkc + b_tn(ds_c, qh)
                dvc = dvc + b_tn(p_c, doh)
                db_ref[h, off] += ds_w
            dq_ref[rs, :] = dq * NA_SCALE
            dk_ref[ws, :] += dkw
            dv_ref[ws, :] += dvw
            return dkc, dvc

        dkc, dvc = lax.fori_loop(0, rows, lat_step, carry)
        dk_ref[0:m_ctx, :] = dkc
        dv_ref[0:m_ctx, :] = dvc

    vm = pl.BlockSpec(memory_space=pltpu.VMEM)
    row = jax.ShapeDtypeStruct((t, 256), F32)
    return pl.pallas_call(
        body, name=name, in_specs=[vm] * 6, out_specs=[vm] * 4,
        out_shape=[row, row, row, jax.ShapeDtypeStruct(bias8.shape, F32)], compiler_params=_cparams(),
    )(q, k, v, do, bias8, hm)


def _na_toeplitz():
    col = np.arange(GRID_W)
    dd = (col[None, :] - col[:, None] + 15).reshape(-1)
    tt = np.zeros((GRID_W * GRID_W, 128), np.float32)
    ok = (dd >= 0) & (dd <= 30)
    tt[np.arange(GRID_W * GRID_W)[ok], dd[ok]] = 1.0
    return tt


def _na_bias8(rpb, name):
    col = np.arange(GRID_W)
    cs = np.clip(col - 8, 0, GRID_W - 16)
    col_mask = (col[None, :] >= cs[:, None]) & (col[None, :] < cs[:, None] + 16)
    rpb2 = jnp.pad(rpb.reshape(60, 31), ((0, 4), (0, 97)))
    (toe,) = whole_fwd(lambda r_, t_: (hdot_nt(r_, t_),), name, [rpb2, jnp.asarray(_na_toeplitz())], [(64, GRID_W * GRID_W)])
    toe = toe[:60].reshape(4, 15, GRID_W, GRID_W)
    b = jnp.stack([toe[:, off:off + 8] for off in range(8)], axis=1)
    b = jnp.where(jnp.asarray(col_mask)[None, None, None], b, NEG)
    return b.transpose(0, 1, 3, 2, 4).reshape(4, 8, GRID_W, 8 * GRID_W)


def _na_rpb_grad(dbias8, name):
    tt = _na_toeplitz()
    sel = np.zeros((64, 256), np.float32)
    for h in range(4):
        for off in range(8):
            for i in range(8):
                sel[h * 15 + off + i, h * 64 + off * 8 + i] = 1.0
    a2 = dbias8.reshape(4, 8, GRID_W, 8, GRID_W).transpose(0, 1, 3, 2, 4).reshape(256, GRID_W * GRID_W)
    (out,) = whole_fwd(lambda a, t_, s_: (hdot(s_, hdot(a, t_)),), name, [a2, jnp.asarray(tt), jnp.asarray(sel)], [(64, 128)])
    return out[:60, :31].reshape(4, 15, 31)


def f_mod(cs, b_mod, w_mod):
    s = _silu(cs)
    return bdot(s, w_mod) + b_mod, s


def loss_and_grad(z, tgt, n_ctx_rows, name, tile=ROW_TILE):
    t, d = z.shape
    tile = min(tile, n_ctx_rows)
    nct = n_ctx_rows // tile

    def body(z_ref, t_ref, dz_ref, loss_ref):
        i = pl.program_id(0)

        @pl.when(i == 0)
        def _():
            loss_ref[...] = jnp.zeros_like(loss_ref)

        @pl.when(i < nct)
        def _():
            dz_ref[...] = jnp.zeros_like(dz_ref)

        @pl.when(i >= nct)
        def _():
            diff = z_ref[...] - t_ref[...]
            dz_ref[...] = diff * (1.0 / d)
            loss_ref[...] += 0.5 * jnp.sum(jnp.sum(diff * diff, axis=-1, keepdims=True) * (1.0 / d), axis=0, keepdims=True)

    dz, loss = pl.pallas_call(
        body, name=name, grid=(t // tile,),
        in_specs=[pl.BlockSpec((tile, d), lambda i: (i, 0)),
                  pl.BlockSpec((tile, d), lambda i: (jnp.maximum(i - nct, 0), 0))],
        out_specs=[pl.BlockSpec((tile, d), lambda i: (i, 0)), pl.BlockSpec((8, 128), lambda i: (0, 0))],
        out_shape=[jax.ShapeDtypeStruct((t, d), F32), jax.ShapeDtypeStruct((8, 128), F32)],
        compiler_params=_cparams(dimension_semantics=("arbitrary",)),
    )(z, tgt)
    return loss[0, 0], dz


def adamw(parts, w, m, v, name, tile=256):
    npart, r, c = parts.shape
    tile = min(tile, r)
    assert r % tile == 0
    c1 = 1.0 / (1.0 - ADAM_B1 ** ADAM_STEP)
    c2 = 1.0 / (1.0 - ADAM_B2 ** ADAM_STEP)

    def body(p_ref, w_ref, m_ref, v_ref, g_ref, d_ref, nm_ref, nv_ref):
        g = p_ref[0]
        for i in range(1, npart):
            g = g + p_ref[i]
        nm = ADAM_B1 * m_ref[...] + (1.0 - ADAM_B1) * g
        nv = ADAM_B2 * v_ref[...] + (1.0 - ADAM_B2) * (g * g)
        g_ref[...] = g
        nm_ref[...] = nm
        nv_ref[...] = nv
        d_ref[...] = -ADAM_LR * ((nm * c1) / (jnp.sqrt(nv * c2) + ADAM_EPS) + ADAM_WD * w_ref[...])

    blk = pl.BlockSpec((tile, c), lambda i: (i, 0))
    return pl.pallas_call(
        body, name=name, grid=(r // tile,),
        in_specs=[pl.BlockSpec((npart, tile, c), lambda i: (0, i, 0)), blk, blk, blk],
        out_specs=[blk] * 4, out_shape=[jax.ShapeDtypeStruct((r, c), F32)] * 4,
        compiler_params=_cparams(dimension_semantics=("arbitrary",)),
    )(parts, w, m, v)


def _peer(x, y, c, k):
    return (1 - x if k & 4 else x, 1 - y if k & 2 else y, 1 - c if k & 1 else c)


def exchange(arrays, scatter, name):
    n = len(arrays)
    out_shapes = [jax.ShapeDtypeStruct(a.shape if s else (N_DEV,) + a.shape, a.dtype) for a, s in zip(arrays, scatter)]

    def body(*refs):
        ins, outs = refs[:n], refs[n:2 * n]
        send_sems, recv_sems, local_sems = refs[2 * n:]
        x, y, c = lax.axis_index("x"), lax.axis_index("y"), lax.axis_index("c")
        me = 4 * x + 2 * y + c

        def index_of(p):
            return 4 * p[0] + 2 * p[1] + p[2]

        started = []
        for a in range(n):
            src_me = ins[a].at[me] if scatter[a] else ins[a]
            loc = pltpu.make_async_copy(src_me, outs[a].at[me], local_sems.at[a])
            loc.start()
            started.append(loc)
        for k in range(1, N_DEV):
            peer = _peer(x, y, c, k)
            for a in range(n):
                src = ins[a].at[index_of(peer)] if scatter[a] else ins[a]
                cp = pltpu.make_async_remote_copy(
                    src_ref=src, dst_ref=outs[a].at[me], send_sem=send_sems.at[a, k - 1], recv_sem=recv_sems.at[a, k - 1],
                    device_id=peer, device_id_type=pl.DeviceIdType.MESH)
                cp.start()
        for k in range(1, N_DEV):
            peer = _peer(x, y, c, k)
            for a in range(n):
                src = ins[a].at[index_of(peer)] if scatter[a] else ins[a]
                slot = outs[a].at[index_of(peer)]
                cp = pltpu.make_async_remote_copy(
                    src_ref=src, dst_ref=slot, send_sem=send_sems.at[a, k - 1], recv_sem=recv_sems.at[a, k - 1],
                    device_id=peer, device_id_type=pl.DeviceIdType.MESH)
                cp.wait_send()
                cp.wait_recv()
        for loc in started:
            loc.wait()

    hbm = pl.BlockSpec(memory_space=pl.ANY)
    return pl.pallas_call(
        body, name=name, in_specs=[hbm] * n, out_specs=[hbm] * n, out_shape=out_shapes,
        scratch_shapes=[pltpu.SemaphoreType.DMA((n, N_DEV - 1)), pltpu.SemaphoreType.DMA((n, N_DEV - 1)),
                        pltpu.SemaphoreType.DMA((n,))],
        compiler_params=pltpu.CompilerParams(has_side_effects=True),
    )(*arrays)


def _rope_tables(n_lat, n_ctx):
    tok = np.arange(n_lat)
    freqs = 10000.0 ** (-np.arange(0, 16, 2, dtype=np.float32) / 16.0)

    def table(pos):
        ang = pos.astype(np.float32)[:, None] * freqs[None, :]
        ang = np.concatenate([ang, ang], axis=-1)
        return np.cos(ang), np.sin(ang)

    cr, sr = table(tok // GRID_W)
    cc, sc = table(tok % GRID_W)
    cos = np.tile(np.concatenate([cr, cc], axis=-1), (1, 4))
    sin = np.tile(np.concatenate([sr, sc], axis=-1), (1, 4))
    cos = np.concatenate([np.ones((n_ctx, 128), np.float32), cos], axis=0)
    sin = np.concatenate([np.zeros((n_ctx, 128), np.float32), sin], axis=0)
    return jnp.asarray(cos, F32), jnp.asarray(sin, F32)


def _pad_w_in(w):
    z = lambda n: jnp.zeros((w.shape[0], n), w.dtype)
    return jnp.concatenate([w[:, 1824:2848], w[:, 128:384], w[:, 416:672], w[:, 672:928], w[:, 928:1184], w[:, 1312:1568],
                            w[:, 1568:1824], w[:, 0:128], w[:, 384:416], z(96), w[:, 1184:1312], z(128)], axis=1)


def _unpad_w_in(wp):
    return jnp.concatenate([wp[:, C_GK:C_GK + 128], wp[:, C_GV:C_GV + 256], wp[:, C_GG:C_GG + 32], wp[:, C_NK:C_NK + 256],
                            wp[:, C_NV:C_NV + 256], wp[:, C_SU:C_SU + 256], wp[:, C_GQ:C_GQ + 128], wp[:, C_NQ:C_NQ + 256],
                            wp[:, C_PU:C_PU + 256], wp[:, C_GT:C_GT + 1024]], axis=1)


def _pad_rows(u):
    return jnp.pad(u, ((POOL_HALO, POOL_HALO), (0, 0)))


def _block_diag4(w):
    out = jnp.zeros((256, 256), w.dtype)
    for i in range(4):
        out = lax.dynamic_update_slice(out, w[i], (64 * i, 64 * i))
    return out


def _layer_params(p, l):
    e_rep, e_tile, gmask, bdm = _s5_consts()
    wg = jnp.zeros((128, 256), F32)
    wg = lax.dynamic_update_slice(wg, p["gla_w_gate"][l, 0], (0, 0))
    wg = lax.dynamic_update_slice(wg, p["gla_w_gate"][l, 1], (16, 128))
    s5 = []
    for d in range(2):
        s5.append([p["s5_lam_re"][l, d], p["s5_lam_im"][l, d], p["s5_log_dt"][l, d].reshape(16, 1),
                   p["s5_b_re"][l, d].transpose(0, 2, 1).reshape(256, 64), p["s5_b_im"][l, d].transpose(0, 2, 1).reshape(256, 64),
                   p["s5_c_re"][l, d].reshape(256, 64), p["s5_c_im"][l, d].reshape(256, 64), e_rep, e_tile, gmask, bdm])
    havg = jnp.asarray((np.arange(256)[:, None] // 64 == np.arange(256)[None, :] // 64).astype(np.float32) / 64.0)
    e4 = jnp.asarray((np.arange(64)[:, None] == np.arange(256)[None, :] % 64).astype(np.float32))
    return dict(
        g_pre=p["g_pre"][l].reshape(1, D), g_post=p["g_post"][l].reshape(1, D), b_mod=p["b_mod"][l].reshape(1, 3 * D),
        w_mod=p["w_mod"][l], w_in=_pad_w_in(p["w_in"][l]), w_out=p["w_out"][l],
        wg=wg, bg=p["gla_b_gate"][l].reshape(1, 256), g_norm=jnp.pad(p["gla_g_norm"][l].reshape(1, 64), ((0, 7), (0, 0))),
        bias8=_na_bias8(p["na_rpb"][l], f"na_bias_l{l}"), s5=s5, s5_d=p["s5_d"][l].reshape(1, 256), w_glu=p["s5_w_glu"][l].astype(F32),
        b_glu=p["s5_b_glu"][l].reshape(1, 256), wpool=_block_diag4(p["pool_w"][l]), pool_scale=p["pool_scale"][l].reshape(1, 256),
        havg=havg, e4=e4)


def _cols(pz, start, width):
    return pz[:, start:start + width]


def _layer_fwd(z, modseg, lp, cos, sin, m_ctx, tile, s5_chunk, l):
    t = z.shape[0]
    nct = m_ctx // tile
    nm = lambda s: f"{s}_l{l}"
    (h,) = rowwise_fwd(f_pre, nm("pre"), [z], [modseg], [lp["g_pre"]], [D], tile, nct)
    pz = mm_nn([h], lp["w_in"], nm("in_proj"), tm=tile)
    gt, pv, nk, nv, su = _cols(pz, C_GT, 1024), _cols(pz, C_GV, 256), _cols(pz, C_NK, 256), _cols(pz, C_NV, 256), _cols(pz, C_SU, 256)
    nq, pu, pk, pg, pq = _cols(pz, C_NQ, 256), _cols(pz, C_PU, 256), _cols(pz, C_GK, 128), _cols(pz, C_GG, 128), _cols(pz, C_GQ, 128)
    q_r, k_r, lgf, lgb = rowwise_fwd(f_gla_prep, nm("gla_prep"), [pk, pg, pq, cos, sin], [], [lp["wg"], lp["bg"]], [128] * 4, tile, nct)
    o1, st_f = gla_scan_fwd(q_r, k_r, pv, lgf, jnp.zeros((t, 256), F32), m_ctx, False, nm("gla_f"))
    o_gla, st_b = gla_scan_fwd(q_r, k_r, pv, lgb, o1, m_ctx, True, nm("gla_r"))
    o_na = na_fwd(nq, nk, nv, lp["bias8"], m_ctx, nm("na"))
    s5p = [whole_fwd(f_s5_params, nm(f"s5_par{d}"), lp["s5"][d], [(1, 1024)] * 2 + [(256, 1024)] * 4) for d in range(2)]
    y1, x0r_f, x0i_f = s5_scan_fwd(su, jnp.zeros((t, 256), F32), *s5p[0], m_ctx, s5_chunk, False, nm("s5_f"))
    y5, x0r_b, x0i_b = s5_scan_fwd(su, y1, *s5p[1], m_ctx, s5_chunk, True, nm("s5_r"))
    pm = jnp.concatenate([pool_apply(_pad_rows(pu[:m_ctx]), m_ctx, False, nm("pool_c")),
                          pool_apply(_pad_rows(pu[m_ctx:]), t - m_ctx, False, nm("pool_x"))], axis=0)
    mix_rows = [o_gla, o_na, y5, su, pm, gt]
    mix_globs = [lp["g_norm"], lp["s5_d"], lp["w_glu"], lp["b_glu"], lp["wpool"], lp["pool_scale"], lp["havg"], lp["e4"]]
    (yg,) = rowwise_fwd(f_mix, nm("mix"), mix_rows, [], mix_globs, [D], tile, nct)
    out = mm_nn([yg], lp["w_out"], nm("out_proj"), tm=tile)
    (z_new,) = rowwise_fwd(f_post, nm("post"), [z, out], [modseg], [lp["g_post"]], [D], tile, nct)
    saved = dict(z=z, h=h, pv=pv, nk=nk, nv=nv, su=su, nq=nq, pk=pk, pg=pg, pq=pq, q_r=q_r, k_r=k_r, lgf=lgf, lgb=lgb,
                 st_f=st_f, st_b=st_b, s5p=s5p, x0f=(x0r_f, x0i_f), x0b=(x0r_b, x0i_b), mix_rows=mix_rows, mix_globs=mix_globs,
                 yg=yg, out=out)
    return z_new, saved


def _f_pre_res(x, mod, g_pre):
    return f_pre(x, mod, g_pre)[0], x


def _layer_bwd(dz_new, sv, modseg, lp, cos, sin, m_ctx, tile, s5_chunk, l):
    t = dz_new.shape[0]
    nct = m_ctx // tile
    nm = lambda s: f"{s}_l{l}"
    g = {}
    dz_res, dout, dmod_post, g["g_post"] = rowwise_bwd(f_post, nm("post_b"), [sv["z"], sv["out"]], [modseg], [lp["g_post"]],
                                                       [dz_new], tile, nct, [True, True], [True])
    dyg = mm_nt([dout], lp["w_out"], nm("out_proj_dx"), tm=tile)
    (g["w_out"],) = mm_tn(sv["yg"], [dout], nm("out_proj_dw"), tm=tile)
    res = rowwise_bwd(f_mix, nm("mix_b"), sv["mix_rows"], [], sv["mix_globs"], [dyg], tile, nct, [True] * 6, [True] * 6 + [False] * 2)
    do_gla, do_na, dy5, dsu_a, dpm, dgt = res[:6]
    g["g_norm"], g["s5_d"], g["w_glu"], g["b_glu"], g["wpool"], g["pool_scale"] = res[6:]
    dpu = jnp.concatenate([pool_apply(_pad_rows(dpm[:m_ctx]), m_ctx, True, nm("pool_c_b")),
                           pool_apply(_pad_rows(dpm[m_ctx:]), t - m_ctx, True, nm("pool_x_b"))], axis=0)
    r_b = s5_scan_bwd(sv["su"], dy5, dsu_a, *sv["x0b"], *sv["s5p"][1], m_ctx, s5_chunk, True, nm("s5_r_b"))
    r_f = s5_scan_bwd(sv["su"], dy5, r_b[0], *sv["x0f"], *sv["s5p"][0], m_ctx, s5_chunk, False, nm("s5_f_b"))
    dsu = r_f[0]
    g["s5"] = [whole_bwd(f_s5_params, nm(f"s5_par{d}_b"), lp["s5"][d], list(r[1:]), [True] * 7 + [False] * 4)
               for d, r in ((0, r_f), (1, r_b))]
    dnq, dnk, dnv, dbias8 = na_bwd(sv["nq"], sv["nk"], sv["nv"], do_na, lp["bias8"], m_ctx, nm("na_b"))
    g["rpb"] = _na_rpb_grad(dbias8, nm("na_rpb_b"))
    zq, zv = jnp.zeros((t, 128), F32), jnp.zeros((t, 256), F32)
    dq1, dk1, dv1, dlgb = gla_scan_bwd(sv["q_r"], sv["k_r"], sv["pv"], sv["lgb"], sv["st_b"], do_gla, (zq, zq, zv), m_ctx, True, nm("gla_r_b"))
    dq_r, dk_r, dpv, dlgf = gla_scan_bwd(sv["q_r"], sv["k_r"], sv["pv"], sv["lgf"], sv["st_f"], do_gla, (dq1, dk1, dv1), m_ctx, False, nm("gla_f_b"))
    dpk, dpg, dpq, g["wg"], g["bg"] = rowwise_bwd(f_gla_prep, nm("gla_prep_b"), [sv["pk"], sv["pg"], sv["pq"], cos, sin], [],
                                                  [lp["wg"], lp["bg"]], [dq_r, dk_r, dlgf, dlgb], tile, nct,
                                                  [True, True, True, False, False], [True, True])
    parts = [dgt, dpv, dnk, dnv, dsu, dnq, dpu, dpk, dpg, dpq, jnp.zeros((t, 128), F32)]
    dh = mm_nt(parts, lp["w_in"], nm("in_proj_dx"), tm=tile)
    g["w_in"] = _unpad_w_in(jnp.concatenate(mm_tn(sv["h"], parts, nm("in_proj_dw"), tm=tile), axis=1))
    dz, dmod_pre, g["g_pre"] = rowwise_bwd(_f_pre_res, nm("pre_b"), [sv["z"]], [modseg], [lp["g_pre"]], [dh, dz_res], tile, nct, [True], [True])
    return dz, dmod_pre, dmod_post, g


def _f_mod_sum(cs, b_mod, w_mod):
    mod, _ = f_mod(cs, b_mod, w_mod)
    return mod, cs


def local_step(x, c, ctx, tgt, p, tile=ROW_TILE, s5_chunk=S5_CHUNK):
    n_lat, m_ctx = x.shape[0], ctx.shape[0]
    n_layers = p["w_in"].shape[0]
    z = jnp.concatenate([ctx, x], axis=0)
    cos, sin = _rope_tables(n_lat, m_ctx)
    cs = jnp.concatenate([c.reshape(1, D), p["c_ctx"].reshape(1, D), jnp.zeros((6, D), F32)], axis=0)
    lps, mods, silus, saves = [], [], [], []
    for l in range(n_layers):
        lp = _layer_params(p, l)
        mod8, s8 = whole_fwd(f_mod, f"mod_l{l}", [cs, lp["b_mod"], lp["w_mod"]], [(8, 3 * D), (8, D)])
        modseg = mod8[:2].reshape(2, 1, 3 * D)
        z, sv = _layer_fwd(z, modseg, lp, cos, sin, m_ctx, tile, s5_chunk, l)
        lps.append(lp); mods.append(modseg); silus.append(s8); saves.append(sv)
    loss, dz = loss_and_grad(z, tgt, m_ctx, "loss", tile)
    grads = [None] * n_layers
    dcs = jnp.zeros((8, D), F32)
    for l in reversed(range(n_layers)):
        lp = lps[l]
        dz, dmod_pre, dmod_post, g = _layer_bwd(dz, saves[l], mods[l], lp, cos, sin, m_ctx, tile, s5_chunk, l)
        dmod = jnp.concatenate([dmod_pre.reshape(2, 3 * D)[:, :2 * D], dmod_post.reshape(2, 3 * D)[:, 2 * D:]], axis=1)
        dmod8 = jnp.pad(dmod, ((0, 6), (0, 0)))
        dcs, g["b_mod"] = whole_bwd(_f_mod_sum, f"mod_b_l{l}", [cs, lp["b_mod"], lp["w_mod"]], [dmod8, dcs], [True, True, False])
        g["w_mod"] = jnp.concatenate(mm_tn(silus[l], [dmod8[:, :D], dmod8[:, D:2 * D], dmod8[:, 2 * D:]], f"mod_dw_l{l}", tm=8), axis=1)
        grads[l] = g
    return loss, dz[m_ctx:], dcs[1], grads


_WEIGHTS = ["c_ctx", "w_mod", "b_mod", "g_pre", "g_post", "w_in", "w_out", "gla_w_gate", "gla_b_gate", "gla_g_norm", "na_rpb",
            "s5_lam_re", "s5_lam_im", "s5_log_dt", "s5_b_re", "s5_b_im", "s5_c_re", "s5_c_im", "s5_d", "s5_w_glu", "s5_b_glu",
            "pool_w", "pool_scale"]
_INPUTS = ["x", "c", "ctx"] + _WEIGHTS + ["loss_target"] + ["m_" + n for n in _WEIGHTS] + ["v_" + n for n in _WEIGHTS]
_SHARDED = ["w_mod", "w_in", "w_out", "s5_w_glu"]
_SMALL = [n for n in _WEIGHTS if n not in _SHARDED]
_PACK_ROWS = 256


def _pack(arrs):
    flat = jnp.concatenate([a.reshape(-1) for a in arrs])
    quantum = _PACK_ROWS * 128
    total = -(-flat.shape[0] // quantum) * quantum
    return jnp.pad(flat, (0, total - flat.shape[0])).reshape(-1, 128)


def _unpack(packed, like):
    flat, out, pos = packed.reshape(-1), [], 0
    for a in like:
        out.append(flat[pos:pos + a.size].reshape(a.shape))
        pos += a.size
    return out


def _gathered(g, cols):
    n_layers = g.shape[1]
    if cols:
        return g.transpose(1, 2, 0, 3).reshape(n_layers, g.shape[2], N_DEV * g.shape[3])
    return g.transpose(1, 0, 2, 3).reshape(n_layers, N_DEV * g.shape[2], g.shape[3])


def _slabs(w, cols):
    n_layers, r, c = w.shape
    if cols:
        return w.reshape(n_layers, r, N_DEV, c // N_DEV).transpose(2, 0, 1, 3)
    return w.reshape(n_layers, N_DEV, r // N_DEV, c).transpose(1, 0, 2, 3)


def _small_grads(d_c_ctx, grads):
    n_layers = len(grads)
    st = lambda f: jnp.stack([f(grads[l]) for l in range(n_layers)])
    s5 = lambda i, f: st(lambda g: jnp.stack([f(g["s5"][d][i]) for d in range(2)]))
    return {
        "c_ctx": d_c_ctx,
        "b_mod": st(lambda g: g["b_mod"].reshape(3 * D)), "g_pre": st(lambda g: g["g_pre"].reshape(D)),
        "g_post": st(lambda g: g["g_post"].reshape(D)),
        "gla_w_gate": st(lambda g: jnp.stack([g["wg"][0:16, 0:128], g["wg"][16:32, 128:256]])),
        "gla_b_gate": st(lambda g: g["bg"].reshape(2, 128)), "gla_g_norm": st(lambda g: g["g_norm"][0]),
        "na_rpb": st(lambda g: g["rpb"]),
        "s5_lam_re": s5(0, lambda a: a), "s5_lam_im": s5(1, lambda a: a), "s5_log_dt": s5(2, lambda a: a.reshape(16)),
        "s5_b_re": s5(3, lambda a: a.reshape(16, 16, 64).transpose(0, 2, 1)),
        "s5_b_im": s5(4, lambda a: a.reshape(16, 16, 64).transpose(0, 2, 1)),
        "s5_c_re": s5(5, lambda a: a.reshape(16, 16, 64)), "s5_c_im": s5(6, lambda a: a.reshape(16, 16, 64)),
        "s5_d": st(lambda g: g["s5_d"].reshape(256)), "s5_b_glu": st(lambda g: g["b_glu"].reshape(256)),
        "pool_w": st(lambda g: jnp.stack([g["wpool"][64 * i:64 * i + 64, 64 * i:64 * i + 64] for i in range(4)])),
        "pool_scale": st(lambda g: g["pool_scale"].reshape(256)),
    }


def kernel(x, c, ctx, c_ctx, w_mod, b_mod, g_pre, g_post, w_in, w_out, gla_w_gate, gla_b_gate, gla_g_norm, na_rpb, s5_lam_re, s5_lam_im, s5_log_dt, s5_b_re, s5_b_im, s5_c_re, s5_c_im, s5_d, s5_w_glu, s5_b_glu, pool_w, pool_scale, loss_target, m_c_ctx, m_w_mod, m_b_mod, m_g_pre, m_g_post, m_w_in, m_w_out, m_gla_w_gate, m_gla_b_gate, m_gla_g_norm, m_na_rpb, m_s5_lam_re, m_s5_lam_im, m_s5_log_dt, m_s5_b_re, m_s5_b_im, m_s5_c_re, m_s5_c_im, m_s5_d, m_s5_w_glu, m_s5_b_glu, m_pool_w, m_pool_scale, v_c_ctx, v_w_mod, v_b_mod, v_g_pre, v_g_post, v_w_in, v_w_out, v_gla_w_gate, v_gla_b_gate, v_gla_g_norm, v_na_rpb, v_s5_lam_re, v_s5_lam_im, v_s5_log_dt, v_s5_b_re, v_s5_b_im, v_s5_c_re, v_s5_c_im, v_s5_d, v_s5_w_glu, v_s5_b_glu, v_pool_w, v_pool_scale):
    given = dict(zip(_INPUTS, (x, c, ctx, c_ctx, w_mod, b_mod, g_pre, g_post, w_in, w_out, gla_w_gate, gla_b_gate, gla_g_norm, na_rpb, s5_lam_re, s5_lam_im, s5_log_dt, s5_b_re, s5_b_im, s5_c_re, s5_c_im, s5_d, s5_w_glu, s5_b_glu, pool_w, pool_scale, loss_target, m_c_ctx, m_w_mod, m_b_mod, m_g_pre, m_g_post, m_w_in, m_w_out, m_gla_w_gate, m_gla_b_gate, m_gla_g_norm, m_na_rpb, m_s5_lam_re, m_s5_lam_im, m_s5_log_dt, m_s5_b_re, m_s5_b_im, m_s5_c_re, m_s5_c_im, m_s5_d, m_s5_w_glu, m_s5_b_glu, m_pool_w, m_pool_scale, v_c_ctx, v_w_mod, v_b_mod, v_g_pre, v_g_post, v_w_in, v_w_out, v_gla_w_gate, v_gla_b_gate, v_gla_g_norm, v_na_rpb, v_s5_lam_re, v_s5_lam_im, v_s5_log_dt, v_s5_b_re, v_s5_b_im, v_s5_c_re, v_s5_c_im, v_s5_d, v_s5_w_glu, v_s5_b_glu, v_pool_w, v_pool_scale)))
    by_cols = {"w_mod": True, "w_in": True, "w_out": False, "s5_w_glu": False}
    gathered = exchange([given[n].astype(BF16) for n in _SHARDED], [False] * len(_SHARDED), "gather_weights")
    p = {n: given[n] for n in _SMALL}
    for n, g in zip(_SHARDED, gathered):
        p[n] = _gathered(g, by_cols[n])
    loss, grad_x, d_c_ctx, grads = local_step(x[0], c, ctx[0], loss_target[0], p)
    full = {"w_mod": jnp.stack([g["w_mod"] for g in grads]), "w_in": jnp.stack([g["w_in"] for g in grads]),
            "w_out": jnp.stack([g["w_out"] for g in grads]), "s5_w_glu": jnp.stack([g["w_glu"] for g in grads])}
    small = _small_grads(d_c_ctx, grads)
    sends = [_slabs(full[n], by_cols[n]) for n in _SHARDED] + [_pack([small[n] for n in _SMALL])]
    recvd = exchange(sends, [True] * len(_SHARDED) + [False], "exchange_grads")
    outs = {}
    for n, r in zip(_SHARDED, recvd):
        w = given[n]
        two_d = (w.shape[0] * w.shape[1], w.shape[2])
        res = adamw(r.reshape((N_DEV,) + two_d), w.reshape(two_d), given["m_" + n].reshape(two_d), given["v_" + n].reshape(two_d),
                    "adamw_" + n)
        outs[n] = [a.reshape(w.shape) for a in res]
    like = [given[n] for n in _SMALL]
    res = adamw(recvd[-1], _pack(like), _pack([given["m_" + n] for n in _SMALL]), _pack([given["v_" + n] for n in _SMALL]),
                "adamw_small")
    for kind, packed in enumerate(res):
        for n, a in zip(_SMALL, _unpack(packed, like)):
            outs.setdefault(n, [None] * 4)[kind] = a
    loss = lax.psum(loss, ("x", "y", "c"))
    return (loss, grad_x[None], *[outs[n][0] for n in _WEIGHTS], *[outs[n][1] for n in _WEIGHTS],
            *[outs[n][2] for n in _WEIGHTS], *[outs[n][3] for n in _WEIGHTS])
```

```python
import functools
import math

import numpy as np
import jax
import jax.numpy as jnp
from jax import lax
from jax.experimental import pallas as pl
from jax.experimental.pallas import tpu as pltpu

F32 = jnp.float32
BF16 = jnp.bfloat16
HIGHEST = lax.Precision.HIGHEST

D = 1024
GRID_W = 64
EPS = 1e-6
N_DEV = 8
C_GT, C_GV, C_NK, C_NV, C_SU, C_NQ, C_PU, C_GK, C_GG, C_GQ, C_END = 0, 1024, 1280, 1536, 1792, 2048, 2304, 2560, 2688, 2816, 2944
PW = 3072
N_CTX_ORIG = 416
N_IN = 2848
GLA_CHUNK = 64
S5_CHUNK = 256
ROW_TILE = 256
VMEM_LIMIT = 56 * 1024 * 1024

ADAM_LR, ADAM_B1, ADAM_B2, ADAM_EPS, ADAM_WD, ADAM_STEP = 0.001, 0.9, 0.999, 1e-08, 0.01, 10


def _cparams(**kw):
    return pltpu.CompilerParams(vmem_limit_bytes=VMEM_LIMIT, **kw)


def _dg(a, b, ca, cb, precision=None):
    return lax.dot_general(a, b, (((ca,), (cb,)), ((), ())), precision=precision, preferred_element_type=F32)


def hdot(a, b):
    return _dg(a, b, 1, 0, HIGHEST)


def hdot_nt(a, b):
    return _dg(a, b, 1, 1, HIGHEST)


def hdot_tn(a, b):
    return _dg(a, b, 0, 0, HIGHEST)


def b_nn(a, b):
    return _dg(a.astype(BF16), b.astype(BF16), 1, 0)


def b_nt(a, b):
    return _dg(a.astype(BF16), b.astype(BF16), 1, 1)


def b_tn(a, b):
    return _dg(a.astype(BF16), b.astype(BF16), 0, 0)


@jax.custom_vjp
def bdot(a, b):
    return b_nn(a, b)


def _bdot_fwd(a, b):
    return b_nn(a, b), (a, b)


def _bdot_bwd(res, ct):
    a, b = res
    return b_nt(ct, b).astype(a.dtype), b_tn(a, ct).astype(b.dtype)


bdot.defvjp(_bdot_fwd, _bdot_bwd)


def _log_sigmoid(z):
    return jnp.minimum(z, 0.0) - jnp.log(1.0 + jnp.exp(-jnp.abs(z)))


def _silu(z):
    return z * jax.nn.sigmoid(z)


def _gelu(z):
    return 0.5 * z * (1.0 + jnp.tanh(math.sqrt(2.0 / math.pi) * (z + 0.044715 * (z * z * z))))


def _cat(vals):
    return vals[0] if len(vals) == 1 else jnp.concatenate(vals, axis=-1)


def mm_nn(a_parts, b, name, tm=ROW_TILE, tn=1024):
    t = a_parts[0].shape[0]
    k, n = b.shape
    na = len(a_parts)
    tn = min(tn, n)

    def body(*refs):
        a = _cat([r[...].astype(BF16) for r in refs[:na]])
        refs[na + 1][...] = _dg(a, refs[na][...].astype(BF16), 1, 0)

    return pl.pallas_call(
        body, name=name, grid=(n // tn, t // tm),
        in_specs=[pl.BlockSpec((tm, p.shape[1]), lambda j, i: (i, 0)) for p in a_parts]
        + [pl.BlockSpec((k, tn), lambda j, i: (0, j))],
        out_specs=pl.BlockSpec((tm, tn), lambda j, i: (i, j)),
        out_shape=jax.ShapeDtypeStruct((t, n), F32),
        compiler_params=_cparams(dimension_semantics=("arbitrary", "arbitrary")),
    )(*a_parts, b)


def mm_nt(a_parts, b, name, tm=ROW_TILE):
    t = a_parts[0].shape[0]
    n, k = b.shape
    na = len(a_parts)

    def body(*refs):
        a = _cat([r[...].astype(BF16) for r in refs[:na]])
        refs[na + 1][...] = _dg(a, refs[na][...].astype(BF16), 1, 1)

    return pl.pallas_call(
        body, name=name, grid=(t // tm,),
        in_specs=[pl.BlockSpec((tm, p.shape[1]), lambda i: (i, 0)) for p in a_parts]
        + [pl.BlockSpec((n, k), lambda i: (0, 0))],
        out_specs=pl.BlockSpec((tm, n), lambda i: (i, 0)),
        out_shape=jax.ShapeDtypeStruct((t, n), F32),
        compiler_params=_cparams(dimension_semantics=("arbitrary",)),
    )(*a_parts, b)


def mm_tn(a, b_parts, name, tm=ROW_TILE, tn=1024):
    t, k = a.shape
    widths = [p.shape[1] for p in b_parts]
    n = sum(widths)
    assert n % tn == 0
    groups, cur, acc = [], [], 0
    for idx, w in enumerate(widths):
        cur.append(idx)
        acc += w
        if acc == tn:
            groups.append(cur)
            cur, acc = [], 0
        assert acc < tn
    assert not cur
    outs = []
    for gi, grp in enumerate(groups):
        parts = [b_parts[i] for i in grp]
        npart = len(parts)
        nsteps = t // tm

        def body(*refs, npart=npart):
            a_v = refs[0][...].astype(BF16)
            b_v = _cat([r[...].astype(BF16) for r in refs[1:1 + npart]])
            o_ref = refs[1 + npart]
            r = _dg(a_v, b_v, 0, 0)

            @pl.when(pl.program_id(0) == 0)
            def _():
                o_ref[...] = r

            @pl.when(pl.program_id(0) != 0)
            def _():
                o_ref[...] += r

        outs.append(pl.pallas_call(
            body, name=f"{name}_{gi}", grid=(nsteps,),
            in_specs=[pl.BlockSpec((tm, k), lambda i: (i, 0))]
            + [pl.BlockSpec((tm, p.shape[1]), lambda i: (i, 0)) for p in parts],
            out_specs=pl.BlockSpec((k, tn), lambda i: (0, 0)),
            out_shape=jax.ShapeDtypeStruct((k, tn), F32),
            compiler_params=_cparams(dimension_semantics=("arbitrary",)),
        )(a, *parts))
    return outs


def _seg_of(i, nct):
    return jnp.where(i < nct, 1, 0)


def rowwise_fwd(fn, name, rows, segs, globs, out_widths, tile, nct):
    t = rows[0].shape[0]
    nr, ns, ng = len(rows), len(segs), len(globs)

    def body(*refs):
        vals = [r[...] for r in refs[:nr]] + [r[0] for r in refs[nr:nr + ns]] + [r[...] for r in refs[nr + ns:nr + ns + ng]]
        outs = fn(*vals)
        for o_ref, o in zip(refs[nr + ns + ng:], outs):
            o_ref[...] = o

    return pl.pallas_call(
        body, name=name, grid=(t // tile,),
        in_specs=[pl.BlockSpec((tile, r.shape[1]), lambda i: (i, 0)) for r in rows]
        + [pl.BlockSpec((1, 1, s.shape[2]), lambda i: (_seg_of(i, nct), 0, 0)) for s in segs]
        + [pl.BlockSpec(g.shape, lambda i: (0, 0)) for g in globs],
        out_specs=[pl.BlockSpec((tile, w), lambda i: (i, 0)) for w in out_widths],
        out_shape=[jax.ShapeDtypeStruct((t, w), F32) for w in out_widths],
        compiler_params=_cparams(dimension_semantics=("arbitrary",)),
    )(*rows, *segs, *globs)


def rowwise_bwd(fn, name, rows, segs, globs, cts, tile, nct, row_diff, glob_diff):
    t = rows[0].shape[0]
    nr, ns, ng, nc = len(rows), len(segs), len(globs), len(cts)
    d_rows = [i for i in range(nr) if row_diff[i]]
    d_globs = [i for i in range(ng) if glob_diff[i]]

    def body(*refs):
        in_refs, out_refs = refs[:nr + ns + ng + nc], refs[nr + ns + ng + nc:]
        row_v = [r[...] for r in in_refs[:nr]]
        seg_v = [r[0] for r in in_refs[nr:nr + ns]]
        glob_v = [r[...] for r in in_refs[nr + ns:nr + ns + ng]]
        ct_v = tuple(r[...] for r in in_refs[nr + ns + ng:])

        def wrapped(dr, sv, dg):
            rv = list(row_v)
            for j, i in enumerate(d_rows):
                rv[i] = dr[j]
            gv = list(glob_v)
            for j, i in enumerate(d_globs):
                gv[i] = dg[j]
            return tuple(fn(*rv, *sv, *gv))

        _, vjp = jax.vjp(wrapped, [row_v[i] for i in d_rows], seg_v, [glob_v[i] for i in d_globs])
        c_rows, c_segs, c_globs = vjp(ct_v)
        i = pl.program_id(0)
        k = 0
        for c in c_rows:
            out_refs[k][...] = c
            k += 1
        seg_first = jnp.logical_or(i == 0, i == nct)
        for c in c_segs:
            ref = out_refs[k]
            k += 1

            @pl.when(seg_first)
            def _(ref=ref, c=c):
                ref[0] = c

            @pl.when(jnp.logical_not(seg_first))
            def _(ref=ref, c=c):
                ref[0] += c
        for c in c_globs:
            ref = out_refs[k]
            k += 1

            @pl.when(i == 0)
            def _(ref=ref, c=c):
                ref[...] = c

            @pl.when(i != 0)
            def _(ref=ref, c=c):
                ref[...] += c

    return pl.pallas_call(
        body, name=name, grid=(t // tile,),
        in_specs=[pl.BlockSpec((tile, r.shape[1]), lambda i: (i, 0)) for r in rows]
        + [pl.BlockSpec((1, 1, s.shape[2]), lambda i: (_seg_of(i, nct), 0, 0)) for s in segs]
        + [pl.BlockSpec(g.shape, lambda i: (0, 0)) for g in globs]
        + [pl.BlockSpec((tile, c.shape[1]), lambda i: (i, 0)) for c in cts],
        out_specs=[pl.BlockSpec((tile, rows[i].shape[1]), lambda i: (i, 0)) for i in d_rows]
        + [pl.BlockSpec((1, 1, s.shape[2]), lambda i: (_seg_of(i, nct), 0, 0)) for s in segs]
        + [pl.BlockSpec(globs[i].shape, lambda i: (0, 0)) for i in d_globs],
        out_shape=[jax.ShapeDtypeStruct(rows[i].shape, F32) for i in d_rows]
        + [jax.ShapeDtypeStruct(s.shape, F32) for s in segs]
        + [jax.ShapeDtypeStruct(globs[i].shape, F32) for i in d_globs],
        compiler_params=_cparams(dimension_semantics=("arbitrary",)),
    )(*rows, *segs, *globs, *cts)


def f_pre(x, mod, g_pre):
    shift, scale = mod[:, :D], mod[:, D:2 * D]
    rs = lax.rsqrt(jnp.mean(x * x, axis=-1, keepdims=True) + EPS)
    return ((x * rs) * g_pre * (1.0 + scale) + shift,)


def f_post(x, out, mod, g_post):
    gate = mod[:, 2 * D:]
    rs = lax.rsqrt(jnp.mean(out * out, axis=-1, keepdims=True) + EPS)
    return (x + gate * ((out * rs) * g_post),)


def f_mix(o_gla, o_na, y5, u5, pm, gcols, g_norm, s5_d, w_glu, b_glu, wpool, pool_scale, havg, e4):
    ms = hdot(o_gla * o_gla, havg)
    y_gla = o_gla * lax.rsqrt(ms + EPS) * jnp.sum(hdot(g_norm, e4), axis=0, keepdims=True)
    g = _gelu(u5 * s5_d + y5)
    y_s5 = g * jax.nn.sigmoid(bdot(g, w_glu) + b_glu)
    y_pool = bdot(pm, wpool) * pool_scale
    ycat = jnp.concatenate([y_gla, o_na, y_s5, y_pool], axis=-1)
    return (ycat * _silu(gcols),)


@jax.custom_vjp
def _rot_half16(x):
    lane = lax.broadcasted_iota(jnp.int32, x.shape, 1)
    first = jnp.bitwise_and(lane, 15) < 8
    return jnp.where(first, -pltpu.roll(x, x.shape[1] - 8, 1), pltpu.roll(x, 8, 1))


def _rot_fwd(x):
    return _rot_half16(x), None


def _rot_bwd(_, ct):
    return (-_rot_half16(ct),)


_rot_half16.defvjp(_rot_fwd, _rot_bwd)


def f_gla_prep(pk, pg, pq, cos, sin, wg, bg):
    z = bdot(pg, wg) + bg
    lg = _log_sigmoid(z) * (1.0 / 16.0)
    k_r = pk * cos + _rot_half16(pk) * sin
    q_r = (pq * cos + _rot_half16(pq) * sin) * (32.0 ** -0.5)
    return q_r, k_r, lg[:, :128], lg[:, 128:]


def _gla_consts(rev):
    c = GLA_CHUNK
    i = np.arange(c)
    inc = (i[None, :] >= i[:, None]) if rev else (i[None, :] <= i[:, None])
    mq = np.stack([(np.arange(128) // 32 == h) for h in range(4)]).astype(np.float32).reshape(4, 1, 128)
    mv = np.stack([(np.arange(256) // 64 == h) for h in range(4)]).astype(np.float32).reshape(4, 1, 256)
    bdt = (np.arange(256)[:, None] // 64 == np.arange(128)[None, :] // 32).astype(np.float32)
    return jnp.asarray(inc.astype(np.float32)), jnp.asarray(mq), jnp.asarray(mv), jnp.asarray(bdt)


def _gla_chunk_of(s, n_ctx_chunks, n_chunks, rev):
    if not rev:
        return s
    return jnp.where(s < n_ctx_chunks, n_ctx_chunks - 1 - s, n_ctx_chunks + n_chunks - 1 - s)


def gla_scan_fwd(q, k, v, lg, acc, n_ctx_rows, rev, name):
    t = q.shape[0]
    nch, ncc = t // GLA_CHUNK, n_ctx_rows // GLA_CHUNK
    inc, mq, mv, bdt = _gla_consts(rev)

    def body(q_ref, k_ref, v_ref, lg_ref, acc_ref, inc_ref, mq_ref, mv_ref, bdt_ref, o_ref, st_ref):
        lmask = inc_ref[...]
        bd = bdt_ref[...]

        def step(s, st):
            c = _gla_chunk_of(s, ncc, nch, rev)
            rows = pl.ds(pl.multiple_of(c * GLA_CHUNK, GLA_CHUNK), GLA_CHUNK)
            qc, kc, vc, lgc = q_ref[rows, :], k_ref[rows, :], v_ref[rows, :], lg_ref[rows, :]
            st_ref[c] = st
            b = hdot(lmask, lgc)
            blast = jnp.sum(lgc, axis=0, keepdims=True)
            qe, ke, kd = qc * jnp.exp(b), kc * jnp.exp(-b), kc * jnp.exp(blast - b)
            o = acc_ref[rows, :] + b_nt(qe, st)
            for h in range(4):
                a = lmask * b_nt(qe * mq_ref[h], ke)
                o = o + b_nn(a, vc * mv_ref[h])
            o_ref[rows, :] = o
            return st * jnp.exp(blast) + bd * hdot_tn(vc, kd)

        lax.fori_loop(0, nch, step, jnp.zeros((256, 128), F32))

    vm = pl.BlockSpec(memory_space=pltpu.VMEM)
    return pl.pallas_call(
        body, name=name, in_specs=[vm] * 9, out_specs=[vm, vm],
        out_shape=[jax.ShapeDtypeStruct((t, 256), F32), jax.ShapeDtypeStruct((nch, 256, 128), F32)],
        compiler_params=_cparams(),
    )(q, k, v, lg, acc, inc, mq, mv, bdt)


def gla_scan_bwd(q, k, v, lg, st, do, acc, n_ctx_rows, rev, name):
    t = q.shape[0]
    nch, ncc = t // GLA_CHUNK, n_ctx_rows // GLA_CHUNK
    inc, mq, mv, bdt = _gla_consts(rev)

    def body(q_ref, k_ref, v_ref, lg_ref, st_ref, do_ref, aq_ref, ak_ref, av_ref, inc_ref, mq_ref, mv_ref, bdt_ref,
             dq_ref, dk_ref, dv_ref, dlg_ref):
        lmask = inc_ref[...]
        bd = bdt_ref[...]

        def step(j, carry):
            dst, gsum = carry
            s = nch - 1 - j
            c = _gla_chunk_of(s, ncc, nch, rev)
            rows = pl.ds(pl.multiple_of(c * GLA_CHUNK, GLA_CHUNK), GLA_CHUNK)
            qc, kc, vc, lgc, doc = q_ref[rows, :], k_ref[rows, :], v_ref[rows, :], lg_ref[rows, :], do_ref[rows, :]
            stc = st_ref[c]
            b = hdot(lmask, lgc)
            blast = jnp.sum(lgc, axis=0, keepdims=True)
            eb, enb, edb = jnp.exp(b), jnp.exp(-b), jnp.exp(blast - b)
            qe, ke, kd = qc * eb, kc * enb, kc * edb
            dqe = hdot(doc, stc)
            dke = jnp.zeros_like(ke)
            dv = b_nt(kd, dst)
            for h in range(4):
                qh = qe * mq_ref[h]
                a = lmask * b_nt(qh, ke)
                doh = doc * mv_ref[h]
                da = lmask * hdot_nt(doh, vc)
                dqe = dqe + mq_ref[h] * hdot(da, ke)
                dke = dke + mq_ref[h] * hdot_tn(da, qe)
                dv = dv + mv_ref[h] * b_tn(a, doc)
            dkd = hdot(vc, dst)
            dq = dqe * eb
            dk = dke * enb + dkd * edb
            g = qc * dq - kc * dk
            dlg_ref[rows, :] = hdot_tn(lmask, g) + gsum
            dq_ref[rows, :] = aq_ref[rows, :] + dq
            dk_ref[rows, :] = ak_ref[rows, :] + dk
            dv_ref[rows, :] = av_ref[rows, :] + dv
            dst_new = dst * jnp.exp(blast) + bd * hdot_tn(doc, qe)
            return dst_new, gsum + jnp.sum(g, axis=0, keepdims=True)

        lax.fori_loop(0, nch, step, (jnp.zeros((256, 128), F32), jnp.zeros((1, 128), F32)))

    vm = pl.BlockSpec(memory_space=pltpu.VMEM)
    return pl.pallas_call(
        body, name=name, in_specs=[vm] * 13, out_specs=[vm] * 4,
        out_shape=[jax.ShapeDtypeStruct((t, 128), F32), jax.ShapeDtypeStruct((t, 128), F32),
                   jax.ShapeDtypeStruct((t, 256), F32), jax.ShapeDtypeStruct((t, 128), F32)],
        compiler_params=_cparams(),
    )(q, k, v, lg, st, do, *acc, inc, mq, mv, bdt)


def whole_fwd(fn, name, args, out_shapes):
    def body(*refs):
        outs = fn(*[r[...] for r in refs[:len(args)]])
        for o_ref, o in zip(refs[len(args):], outs):
            o_ref[...] = o

    vm = pl.BlockSpec(memory_space=pltpu.VMEM)
    return pl.pallas_call(
        body, name=name, in_specs=[vm] * len(args), out_specs=[vm] * len(out_shapes),
        out_shape=[jax.ShapeDtypeStruct(s, F32) for s in out_shapes], compiler_params=_cparams(),
    )(*args)


def whole_bwd(fn, name, args, cts, diff):
    d_idx = [i for i in range(len(args)) if diff[i]]

    def body(*refs):
        vals = [r[...] for r in refs[:len(args)]]
        ct_v = tuple(r[...] for r in refs[len(args):len(args) + len(cts)])

        def wrapped(dv):
            av = list(vals)
            for j, i in enumerate(d_idx):
                av[i] = dv[j]
            return tuple(fn(*av))

        _, vjp = jax.vjp(wrapped, [vals[i] for i in d_idx])
        (c_args,) = vjp(ct_v)
        for o_ref, c in zip(refs[len(args) + len(cts):], c_args):
            o_ref[...] = c

    vm = pl.BlockSpec(memory_space=pltpu.VMEM)
    return pl.pallas_call(
        body, name=name, in_specs=[vm] * (len(args) + len(cts)), out_specs=[vm] * len(d_idx),
        out_shape=[jax.ShapeDtypeStruct(args[i].shape, F32) for i in d_idx], compiler_params=_cparams(),
    )(*args, *cts)


def _s5_consts():
    e_rep = (np.arange(256)[:, None] // 16 == np.arange(16)[None, :]).astype(np.float32)
    e_tile = (np.arange(64)[:, None] == np.arange(1024)[None, :] % 64).astype(np.float32)
    gmask = (np.arange(16)[:, None] == np.arange(1024)[None, :] // 64).astype(np.float32)
    bdm = (np.arange(256)[:, None] // 16 == np.arange(1024)[None, :] // 64).astype(np.float32)
    return jnp.asarray(e_rep), jnp.asarray(e_tile), jnp.asarray(gmask), jnp.asarray(bdm)


def f_s5_params(lam_re, lam_im, log_dt, bt_re, bt_im, ct_re, ct_im, e_rep, e_tile, gmask, bdm):
    dt = jnp.exp(log_dt)
    mag = jnp.exp(lam_re * dt)
    ang = lam_im * dt
    lb_re, lb_im = mag * jnp.cos(ang), mag * jnp.sin(ang)
    num_re, num_im = lb_re - 1.0, lb_im
    den = lam_re * lam_re + lam_im * lam_im
    coef_re = (num_re * lam_re + num_im * lam_im) / den
    coef_im = (num_im * lam_re - num_re * lam_im) / den
    cr, ci = hdot(e_rep, coef_re), hdot(e_rep, coef_im)
    bbt_re = cr * bt_re - ci * bt_im
    bbt_im = cr * bt_im + ci * bt_re
    a_re = jnp.sum(hdot(lb_re, e_tile) * gmask, axis=0, keepdims=True)
    a_im = jnp.sum(hdot(lb_im, e_tile) * gmask, axis=0, keepdims=True)
    return (a_re, a_im, hdot(bbt_re, e_tile) * bdm, hdot(bbt_im, e_tile) * bdm,
            hdot(ct_re, e_tile) * bdm, hdot(ct_im, e_tile) * bdm)


def _s5_scan(xr, xi, a_re, a_im, rev, chunk):
    row = lax.broadcasted_iota(jnp.int32, xr.shape, 0)
    pr, pi = a_re, a_im
    s = 1
    while s < chunk:
        if rev:
            keep = row < (chunk - s)
            sr, si = pltpu.roll(xr, chunk - s, 0), pltpu.roll(xi, chunk - s, 0)
        else:
            keep = row >= s
            sr, si = pltpu.roll(xr, s, 0), pltpu.roll(xi, s, 0)
        sr, si = jnp.where(keep, sr, 0.0), jnp.where(keep, si, 0.0)
        xr, xi = xr + pr * sr - pi * si, xi + pr * si + pi * sr
        pr, pi = pr * pr - pi * pi, 2.0 * pr * pi
        s *= 2
    return xr, xi


def _s5_chunk_states(u_c, x0r, x0i, a_re, a_im, bb_re, bb_im, rev, chunk):
    row = lax.broadcasted_iota(jnp.int32, (chunk, 1024), 0)
    first = row == (chunk - 1 if rev else 0)
    inj_r = a_re * x0r - a_im * x0i
    inj_i = a_re * x0i + a_im * x0r
    xr = b_nn(u_c, bb_re) + jnp.where(first, inj_r, 0.0)
    xi = b_nn(u_c, bb_im) + jnp.where(first, inj_i, 0.0)
    return _s5_scan(xr, xi, a_re, a_im, rev, chunk)


def _row_pick(x, idx):
    row = lax.broadcasted_iota(jnp.int32, x.shape, 0)
    return jnp.sum(jnp.where(row == idx, x, 0.0), axis=0, keepdims=True)


def s5_scan_fwd(u, acc, a_re, a_im, bb_re, bb_im, cc_re, cc_im, n_ctx_rows, chunk, rev, name):
    t = u.shape[0]
    nch, ncc = t // chunk, n_ctx_rows // chunk

    def body(u_ref, acc_ref, ar_ref, ai_ref, br_ref, bi_ref, cr_ref, ci_ref, y_ref, x0r_ref, x0i_ref):
        a_r, a_i = ar_ref[...], ai_ref[...]

        def step(s, carry):
            x0r, x0i = carry
            c = _gla_chunk_of(s, ncc, nch, rev)
            rows = pl.ds(pl.multiple_of(c * chunk, chunk), chunk)
            x0r_ref[c] = x0r
            x0i_ref[c] = x0i
            xr, xi = _s5_chunk_states(u_ref[rows, :], x0r, x0i, a_r, a_i, br_ref[...], bi_ref[...], rev, chunk)
            y_ref[rows, :] = acc_ref[rows, :] + b_nt(xr, cr_ref[...]) - b_nt(xi, ci_ref[...])
            last = 0 if rev else chunk - 1
            return _row_pick(xr, last), _row_pick(xi, last)

        lax.fori_loop(0, nch, step, (jnp.zeros((1, 1024), F32), jnp.zeros((1, 1024), F32)))

    vm = pl.BlockSpec(memory_space=pltpu.VMEM)
    return pl.pallas_call(
        body, name=name, in_specs=[vm] * 8, out_specs=[vm] * 3,
        out_shape=[jax.ShapeDtypeStruct((t, 256), F32), jax.ShapeDtypeStruct((nch, 1, 1024), F32),
                   jax.ShapeDtypeStruct((nch, 1, 1024), F32)],
        compiler_params=_cparams(),
    )(u, acc, a_re, a_im, bb_re, bb_im, cc_re, cc_im)


def s5_scan_bwd(u, dy, du_acc, x0r, x0i, a_re, a_im, bb_re, bb_im, cc_re, cc_im, n_ctx_rows, chunk, rev, name):
    t = u.shape[0]
    nch, ncc = t // chunk, n_ctx_rows // chunk

    def body(u_ref, dy_ref, dua_ref, x0r_ref, x0i_ref, ar_ref, ai_ref, br_ref, bi_ref, cr_ref, ci_ref,
             du_ref, dar_ref, dai_ref, dbr_ref, dbi_ref, dcr_ref, dci_ref):
        a_r, a_i = ar_ref[...], ai_ref[...]
        for ref in (dbr_ref, dbi_ref, dcr_ref, dci_ref):
            ref[...] = jnp.zeros_like(ref)
        row = lax.broadcasted_iota(jnp.int32, (chunk, 1024), 0)
        first_idx, last_idx = (chunk - 1, 0) if rev else (0, chunk - 1)

        def step(j, carry):
            lcr, lci, dar, dai = carry
            s = nch - 1 - j
            c = _gla_chunk_of(s, ncc, nch, rev)
            rows = pl.ds(pl.multiple_of(c * chunk, chunk), chunk)
            u_c, dy_c = u_ref[rows, :], dy_ref[rows, :]
            x0r_c, x0i_c = x0r_ref[c], x0i_ref[c]
            xr, xi = _s5_chunk_states(u_c, x0r_c, x0i_c, a_r, a_i, br_ref[...], bi_ref[...], rev, chunk)
            dcr_ref[...] += b_tn(dy_c, xr)
            dci_ref[...] -= b_tn(dy_c, xi)
            inj_r = a_r * lcr + a_i * lci
            inj_i = a_r * lci - a_i * lcr
            is_last = row == last_idx
            lr = b_nn(dy_c, cr_ref[...]) + jnp.where(is_last, inj_r, 0.0)
            li = -b_nn(dy_c, ci_ref[...]) + jnp.where(is_last, inj_i, 0.0)
            lr, li = _s5_scan(lr, li, a_r, -a_i, not rev, chunk)
            du_ref[rows, :] = dua_ref[rows, :] + b_nt(lr, br_ref[...]) + b_nt(li, bi_ref[...])
            dbr_ref[...] += b_tn(u_c, lr)
            dbi_ref[...] += b_tn(u_c, li)
            if rev:
                pr, pi = pltpu.roll(xr, chunk - 1, 0), pltpu.roll(xi, chunk - 1, 0)
            else:
                pr, pi = pltpu.roll(xr, 1, 0), pltpu.roll(xi, 1, 0)
            is_first = row == first_idx
            pr, pi = jnp.where(is_first, x0r_c, pr), jnp.where(is_first, x0i_c, pi)
            dar = dar + jnp.sum(lr * pr + li * pi, axis=0, keepdims=True)
            dai = dai + jnp.sum(li * pr - lr * pi, axis=0, keepdims=True)
            return _row_pick(lr, first_idx), _row_pick(li, first_idx), dar, dai

        z = jnp.zeros((1, 1024), F32)
        _, _, dar, dai = lax.fori_loop(0, nch, step, (z, z, z, z))
        dar_ref[...] = dar
        dai_ref[...] = dai

    vm = pl.BlockSpec(memory_space=pltpu.VMEM)
    big = jax.ShapeDtypeStruct((256, 1024), F32)
    vec = jax.ShapeDtypeStruct((1, 1024), F32)
    return pl.pallas_call(
        body, name=name, in_specs=[vm] * 11, out_specs=[vm] * 7,
        out_shape=[jax.ShapeDtypeStruct((t, 256), F32), vec, vec, big, big, big, big],
        compiler_params=_cparams(),
    )(u, dy, du_acc, x0r, x0i, a_re, a_im, bb_re, bb_im, cc_re, cc_im)


POOL_HALO = 8


def pool_apply(u_pad, n, transpose, name, tile=ROW_TILE):
    tile = min(tile, n)
    ext = tile + 2 * POOL_HALO

    def body(u_ref, o_ref):
        lax.fori_loop(0, n // tile, functools.partial(step, u_ref, o_ref), 0)

    def step(u_ref, o_ref, i, carry):
        val = u_ref[pl.ds(pl.multiple_of(i * tile, tile), ext), :]
        lane = lax.broadcasted_iota(jnp.int32, (ext, 256), 1)
        half = jnp.left_shift(1, jnp.right_shift(lane, 6))
        trow = lax.broadcasted_iota(jnp.int32, (ext, 256), 0) + (i * tile - POOL_HALO)
        cnt = jnp.minimum(trow + half, n) - jnp.maximum(trow - half, 0)
        inv = 1.0 / jnp.maximum(cnt, 1).astype(F32)
        src = val * inv if transpose else val
        acc = jnp.zeros((tile, 256), F32)
        for d in range(-POOL_HALO, POOL_HALO):
            in_win = jnp.logical_and(d >= -half, d <= half - 1)[POOL_HALO:POOL_HALO + tile]
            shift = d if transpose else -d
            rolled = pltpu.roll(src, shift % ext, 0)[POOL_HALO:POOL_HALO + tile]
            acc = acc + jnp.where(in_win, rolled, 0.0)
        centre = val[POOL_HALO:POOL_HALO + tile]
        if not transpose:
            acc = acc * inv[POOL_HALO:POOL_HALO + tile]
        o_ref[pl.ds(pl.multiple_of(i * tile, tile), tile), :] = acc - centre
        return carry

    vm = pl.BlockSpec(memory_space=pltpu.VMEM)
    return pl.pallas_call(
        body, name=name, in_specs=[vm], out_specs=vm,
        out_shape=jax.ShapeDtypeStruct((n, 256), F32), compiler_params=_cparams(),
    )(u_pad)


NA_SCALE = 64.0 ** -0.5
NEG = -1e30


def _na_head_masks():
    return jnp.asarray(np.stack([(np.arange(256) // 64 == h) for h in range(4)]).astype(np.float32).reshape(4, 1, 256))


def _na_window(r, rows):
    start = jnp.clip(r - 4, 0, rows - 8)
    return start, start - r + 7


def _na_probs(qh, kw, kc, bias):
    s_c = b_nt(qh, kc)
    m = jnp.max(s_c, axis=-1, keepdims=True)
    if kw is not None:
        s_w = b_nt(qh, kw) + bias
        m = jnp.maximum(m, jnp.max(s_w, axis=-1, keepdims=True))
        p_w = jnp.exp(s_w - m)
    p_c = jnp.exp(s_c - m)
    l = jnp.sum(p_c, axis=-1, keepdims=True)
    if kw is not None:
        l = l + jnp.sum(p_w, axis=-1, keepdims=True)
        return p_w / l, p_c / l
    return None, p_c / l


def na_fwd(q, k, v, bias8, n_ctx_rows, name):
    t = q.shape[0]
    m_ctx = n_ctx_rows
    rows = (t - m_ctx) // GRID_W
    hm = _na_head_masks()

    def body(q_ref, k_ref, v_ref, b_ref, hm_ref, o_ref):
        kc, vc = k_ref[0:m_ctx, :], v_ref[0:m_ctx, :]

        def ctx_step(i, _):
            rs = pl.ds(pl.multiple_of(i * 64, 64), 64)
            qr = q_ref[rs, :] * NA_SCALE
            o = jnp.zeros((64, 256), F32)
            for h in range(4):
                _, p_c = _na_probs(qr * hm_ref[h], None, kc, None)
                o = o + b_nn(p_c, vc * hm_ref[h])
            o_ref[rs, :] = o
            return 0

        lax.fori_loop(0, m_ctx // 64, ctx_step, 0)

        def lat_step(r, _):
            start, off = _na_window(r, rows)
            rs = pl.ds(pl.multiple_of(m_ctx + r * 64, 64), 64)
            ws = pl.ds(pl.multiple_of(m_ctx + start * 64, 64), 512)
            qr = q_ref[rs, :] * NA_SCALE
            kw, vw = k_ref[ws, :], v_ref[ws, :]
            o = jnp.zeros((64, 256), F32)
            for h in range(4):
                p_w, p_c = _na_probs(qr * hm_ref[h], kw, kc, b_ref[h, off])
                o = o + b_nn(p_w, vw * hm_ref[h]) + b_nn(p_c, vc * hm_ref[h])
            o_ref[rs, :] = o
            return 0

        lax.fori_loop(0, rows, lat_step, 0)

    vm = pl.BlockSpec(memory_space=pltpu.VMEM)
    return pl.pallas_call(
        body, name=name, in_specs=[vm] * 5, out_specs=vm,
        out_shape=jax.ShapeDtypeStruct((t, 256), F32), compiler_params=_cparams(),
    )(q, k, v, bias8, hm)


def na_bwd(q, k, v, do, bias8, n_ctx_rows, name):
    t = q.shape[0]
    m_ctx = n_ctx_rows
    rows = (t - m_ctx) // GRID_W
    hm = _na_head_masks()

    def body(q_ref, k_ref, v_ref, do_ref, b_ref, hm_ref, dq_ref, dk_ref, dv_ref, db_ref):
        kc, vc = k_ref[0:m_ctx, :], v_ref[0:m_ctx, :]
        dk_ref[...] = jnp.zeros_like(dk_ref)
        dv_ref[...] = jnp.zeros_like(dv_ref)
        db_ref[...] = jnp.zeros_like(db_ref)

        def head_terms(qh, doh, kw, vw, bias):
            p_w, p_c = _na_probs(qh, kw, kc, bias)
            dp_c = b_nt(doh, vc)
            delta = jnp.sum(p_c * dp_c, axis=-1, keepdims=True)
            if kw is not None:
                dp_w = b_nt(doh, vw)
                delta = delta + jnp.sum(p_w * dp_w, axis=-1, keepdims=True)
                ds_w = p_w * (dp_w - delta)
            else:
                ds_w = None
            ds_c = p_c * (dp_c - delta)
            return p_w, p_c, ds_w, ds_c

        def ctx_step(i, carry):
            dkc, dvc = carry
            rs = pl.ds(pl.multiple_of(i * 64, 64), 64)
            qr, dor = q_ref[rs, :] * NA_SCALE, do_ref[rs, :]
            dq = jnp.zeros((64, 256), F32)
            for h in range(4):
                qh, doh = qr * hm_ref[h], dor * hm_ref[h]
                _, p_c, _, ds_c = head_terms(qh, doh, None, None, None)
                dq = dq + hm_ref[h] * b_nn(ds_c, kc)
                dkc = dkc + b_tn(ds_c, qh)
                dvc = dvc + b_tn(p_c, doh)
            dq_ref[rs, :] = dq * NA_SCALE
            return dkc, dvc

        zc = jnp.zeros((m_ctx, 256), F32)
        carry = lax.fori_loop(0, m_ctx // 64, ctx_step, (zc, zc))

        def lat_step(r, carry):
            dkc, dvc = carry
            start, off = _na_window(r, rows)
            rs = pl.ds(pl.multiple_of(m_ctx + r * 64, 64), 64)
            ws = pl.ds(pl.multiple_of(m_ctx + start * 64, 64), 512)
            qr, dor = q_ref[rs, :] * NA_SCALE, do_ref[rs, :]
            kw, vw = k_ref[ws, :], v_ref[ws, :]
            dq = jnp.zeros((64, 256), F32)
            dkw = jnp.zeros((512, 256), F32)
            dvw = jnp.zeros((512, 256), F32)
            for h in range(4):
                qh, doh = qr * hm_ref[h], dor * hm_ref[h]
                p_w, p_c, ds_w, ds_c = head_terms(qh, doh, kw, vw, b_ref[h, off])
                dq = dq + hm_ref[h] * (b_nn(ds_w, kw) + b_nn(ds_c, kc))
                dkw = dkw + b_tn(ds_w, qh)
                dvw = dvw + b_tn(p_w, doh)
                dkc = dkc + b_tn(ds_c, qh)
                dvc = dvc + b_tn(p_c, doh)
                db_ref[h, off] += ds_w
            dq_ref[rs, :] = dq * NA_SCALE
            dk_ref[ws, :] += dkw
            dv_ref[ws, :] += dvw
            return dkc, dvc

        dkc, dvc = lax.fori_loop(0, rows, lat_step, carry)
        dk_ref[0:m_ctx, :] = dkc
        dv_ref[0:m_ctx, :] = dvc

    vm = pl.BlockSpec(memory_space=pltpu.VMEM)
    row = jax.ShapeDtypeStruct((t, 256), F32)
    return pl.pallas_call(
        body, name=name, in_specs=[vm] * 6, out_specs=[vm] * 4,
        out_shape=[row, row, row, jax.ShapeDtypeStruct(bias8.shape, F32)], compiler_params=_cparams(),
    )(q, k, v, do, bias8, hm)


def _na_toeplitz():
    col = np.arange(GRID_W)
    dd = (col[None, :] - col[:, None] + 15).reshape(-1)
    tt = np.zeros((GRID_W * GRID_W, 128), np.float32)
    ok = (dd >= 0) & (dd <= 30)
    tt[np.arange(GRID_W * GRID_W)[ok], dd[ok]] = 1.0
    return tt


def _na_bias8(rpb, name):
    col = np.arange(GRID_W)
    cs = np.clip(col - 8, 0, GRID_W - 16)
    col_mask = (col[None, :] >= cs[:, None]) & (col[None, :] < cs[:, None] + 16)
    rpb2 = jnp.pad(rpb.reshape(60, 31), ((0, 4), (0, 97)))
    (toe,) = whole_fwd(lambda r_, t_: (hdot_nt(r_, t_),), name, [rpb2, jnp.asarray(_na_toeplitz())], [(64, GRID_W * GRID_W)])
    toe = toe[:60].reshape(4, 15, GRID_W, GRID_W)
    b = jnp.stack([toe[:, off:off + 8] for off in range(8)], axis=1)
    b = jnp.where(jnp.asarray(col_mask)[None, None, None], b, NEG)
    return b.transpose(0, 1, 3, 2, 4).reshape(4, 8, GRID_W, 8 * GRID_W)


def _na_rpb_grad(dbias8, name):
    tt = _na_toeplitz()
    sel = np.zeros((64, 256), np.float32)
    for h in range(4):
        for off in range(8):
            for i in range(8):
                sel[h * 15 + off + i, h * 64 + off * 8 + i] = 1.0
    a2 = dbias8.reshape(4, 8, GRID_W, 8, GRID_W).transpose(0, 1, 3, 2, 4).reshape(256, GRID_W * GRID_W)
    (out,) = whole_fwd(lambda a, t_, s_: (hdot(s_, hdot(a, t_)),), name, [a2, jnp.asarray(tt), jnp.asarray(sel)], [(64, 128)])
    return out[:60, :31].reshape(4, 15, 31)


def f_mod(cs, b_mod, w_mod):
    s = _silu(cs)
    return bdot(s, w_mod) + b_mod, s


def loss_and_grad(z, tgt, n_ctx_rows, name, tile=ROW_TILE):
    t, d = z.shape
    tile = min(tile, n_ctx_rows)
    nct = n_ctx_rows // tile

    def body(z_ref, t_ref, dz_ref, loss_ref):
        i = pl.program_id(0)

        @pl.when(i == 0)
        def _():
            loss_ref[...] = jnp.zeros_like(loss_ref)

        @pl.when(i < nct)
        def _():
            dz_ref[...] = jnp.zeros_like(dz_ref)

        @pl.when(i >= nct)
        def _():
            diff = z_ref[...] - t_ref[...]
            dz_ref[...] = diff * (1.0 / d)
            loss_ref[...] += 0.5 * jnp.sum(jnp.sum(diff * diff, axis=-1, keepdims=True) * (1.0 / d), axis=0, keepdims=True)

    dz, loss = pl.pallas_call(
        body, name=name, grid=(t // tile,),
        in_specs=[pl.BlockSpec((tile, d), lambda i: (i, 0)),
                  pl.BlockSpec((tile, d), lambda i: (jnp.maximum(i - nct, 0), 0))],
        out_specs=[pl.BlockSpec((tile, d), lambda i: (i, 0)), pl.BlockSpec((8, 128), lambda i: (0, 0))],
        out_shape=[jax.ShapeDtypeStruct((t, d), F32), jax.ShapeDtypeStruct((8, 128), F32)],
        compiler_params=_cparams(dimension_semantics=("arbitrary",)),
    )(z, tgt)
    return loss[0, 0], dz


def adamw(parts, w, m, v, name, tile=256):
    npart, r, c = parts.shape
    tile = min(tile, r)
    assert r % tile == 0
    c1 = 1.0 / (1.0 - ADAM_B1 ** ADAM_STEP)
    c2 = 1.0 / (1.0 - ADAM_B2 ** ADAM_STEP)

    def body(p_ref, w_ref, m_ref, v_ref, g_ref, d_ref, nm_ref, nv_ref):
        g = p_ref[0].astype(F32)
        for i in range(1, npart):
            g = g + p_ref[i].astype(F32)
        nm = ADAM_B1 * m_ref[...] + (1.0 - ADAM_B1) * g
        nv = ADAM_B2 * v_ref[...] + (1.0 - ADAM_B2) * (g * g)
        g_ref[...] = g
        nm_ref[...] = nm
        nv_ref[...] = nv
        d_ref[...] = -ADAM_LR * ((nm * c1) / (jnp.sqrt(nv * c2) + ADAM_EPS) + ADAM_WD * w_ref[...])

    blk = pl.BlockSpec((tile, c), lambda i: (i, 0))
    return pl.pallas_call(
        body, name=name, grid=(r // tile,),
        in_specs=[pl.BlockSpec((npart, tile, c), lambda i: (0, i, 0)), blk, blk, blk],
        out_specs=[blk] * 4, out_shape=[jax.ShapeDtypeStruct((r, c), F32)] * 4,
        compiler_params=_cparams(dimension_semantics=("arbitrary",)),
    )(parts, w, m, v)


def _peer(x, y, c, k):
    return (1 - x if k & 4 else x, 1 - y if k & 2 else y, 1 - c if k & 1 else c)


def exchange(arrays, scatter, name):
    n = len(arrays)
    out_shapes = [jax.ShapeDtypeStruct(a.shape if s else (N_DEV,) + a.shape, a.dtype) for a, s in zip(arrays, scatter)]

    def body(*refs):
        ins, outs = refs[:n], refs[n:2 * n]
        send_sems, recv_sems, local_sems = refs[2 * n:]
        x, y, c = lax.axis_index("x"), lax.axis_index("y"), lax.axis_index("c")
        me = 4 * x + 2 * y + c

        def index_of(p):
            return 4 * p[0] + 2 * p[1] + p[2]

        started = []
        for a in range(n):
            src_me = ins[a].at[me] if scatter[a] else ins[a]
            loc = pltpu.make_async_copy(src_me, outs[a].at[me], local_sems.at[a])
            loc.start()
            started.append(loc)
        for k in range(1, N_DEV):
            peer = _peer(x, y, c, k)
            for a in range(n):
                src = ins[a].at[index_of(peer)] if scatter[a] else ins[a]
                cp = pltpu.make_async_remote_copy(
                    src_ref=src, dst_ref=outs[a].at[me], send_sem=send_sems.at[a, k - 1], recv_sem=recv_sems.at[a, k - 1],
                    device_id=peer, device_id_type=pl.DeviceIdType.MESH)
                cp.start()
        for k in range(1, N_DEV):
            peer = _peer(x, y, c, k)
            for a in range(n):
                src = ins[a].at[index_of(peer)] if scatter[a] else ins[a]
                slot = outs[a].at[index_of(peer)]
                cp = pltpu.make_async_remote_copy(
                    src_ref=src, dst_ref=slot, send_sem=send_sems.at[a, k - 1], recv_sem=recv_sems.at[a, k - 1],
                    device_id=peer, device_id_type=pl.DeviceIdType.MESH)
                cp.wait_send()
                cp.wait_recv()
        for loc in started:
            loc.wait()

    hbm = pl.BlockSpec(memory_space=pl.ANY)
    return pl.pallas_call(
        body, name=name, in_specs=[hbm] * n, out_specs=[hbm] * n, out_shape=out_shapes,
        scratch_shapes=[pltpu.SemaphoreType.DMA((n, N_DEV - 1)), pltpu.SemaphoreType.DMA((n, N_DEV - 1)),
                        pltpu.SemaphoreType.DMA((n,))],
        compiler_params=pltpu.CompilerParams(has_side_effects=True),
    )(*arrays)


def _rope_tables(n_lat, n_ctx):
    tok = np.arange(n_lat)
    freqs = 10000.0 ** (-np.arange(0, 16, 2, dtype=np.float32) / 16.0)

    def table(pos):
        ang = pos.astype(np.float32)[:, None] * freqs[None, :]
        ang = np.concatenate([ang, ang], axis=-1)
        return np.cos(ang), np.sin(ang)

    cr, sr = table(tok // GRID_W)
    cc, sc = table(tok % GRID_W)
    cos = np.tile(np.concatenate([cr, cc], axis=-1), (1, 4))
    sin = np.tile(np.concatenate([sr, sc], axis=-1), (1, 4))
    cos = np.concatenate([np.ones((n_ctx, 128), np.float32), cos], axis=0)
    sin = np.concatenate([np.zeros((n_ctx, 128), np.float32), sin], axis=0)
    return jnp.asarray(cos, F32), jnp.asarray(sin, F32)


def _pad_w_in(w):
    z = lambda n: jnp.zeros((w.shape[0], n), w.dtype)
    return jnp.concatenate([w[:, 1824:2848], w[:, 128:384], w[:, 416:672], w[:, 672:928], w[:, 928:1184], w[:, 1312:1568],
                            w[:, 1568:1824], w[:, 0:128], w[:, 384:416], z(96), w[:, 1184:1312], z(128)], axis=1)


def _unpad_w_in(wp):
    return jnp.concatenate([wp[:, C_GK:C_GK + 128], wp[:, C_GV:C_GV + 256], wp[:, C_GG:C_GG + 32], wp[:, C_NK:C_NK + 256],
                            wp[:, C_NV:C_NV + 256], wp[:, C_SU:C_SU + 256], wp[:, C_GQ:C_GQ + 128], wp[:, C_NQ:C_NQ + 256],
                            wp[:, C_PU:C_PU + 256], wp[:, C_GT:C_GT + 1024]], axis=1)


def _pad_rows(u):
    return jnp.pad(u, ((POOL_HALO, POOL_HALO), (0, 0)))


def _block_diag4(w):
    out = jnp.zeros((256, 256), w.dtype)
    for i in range(4):
        out = lax.dynamic_update_slice(out, w[i], (64 * i, 64 * i))
    return out


def _layer_params(p, l):
    e_rep, e_tile, gmask, bdm = _s5_consts()
    wg = jnp.zeros((128, 256), F32)
    wg = lax.dynamic_update_slice(wg, p["gla_w_gate"][l, 0], (0, 0))
    wg = lax.dynamic_update_slice(wg, p["gla_w_gate"][l, 1], (16, 128))
    s5 = []
    for d in range(2):
        s5.append([p["s5_lam_re"][l, d], p["s5_lam_im"][l, d], p["s5_log_dt"][l, d].reshape(16, 1),
                   p["s5_b_re"][l, d].transpose(0, 2, 1).reshape(256, 64), p["s5_b_im"][l, d].transpose(0, 2, 1).reshape(256, 64),
                   p["s5_c_re"][l, d].reshape(256, 64), p["s5_c_im"][l, d].reshape(256, 64), e_rep, e_tile, gmask, bdm])
    havg = jnp.asarray((np.arange(256)[:, None] // 64 == np.arange(256)[None, :] // 64).astype(np.float32) / 64.0)
    e4 = jnp.asarray((np.arange(64)[:, None] == np.arange(256)[None, :] % 64).astype(np.float32))
    return dict(
        g_pre=p["g_pre"][l].reshape(1, D), g_post=p["g_post"][l].reshape(1, D), b_mod=p["b_mod"][l].reshape(1, 3 * D),
        w_mod=p["w_mod"][l], w_in=_pad_w_in(p["w_in"][l]), w_out=p["w_out"][l],
        wg=wg, bg=p["gla_b_gate"][l].reshape(1, 256), g_norm=jnp.pad(p["gla_g_norm"][l].reshape(1, 64), ((0, 7), (0, 0))),
        bias8=_na_bias8(p["na_rpb"][l], f"na_bias_l{l}"), s5=s5, s5_d=p["s5_d"][l].reshape(1, 256), w_glu=p["s5_w_glu"][l].astype(F32),
        b_glu=p["s5_b_glu"][l].reshape(1, 256), wpool=_block_diag4(p["pool_w"][l]), pool_scale=p["pool_scale"][l].reshape(1, 256),
        havg=havg, e4=e4)


def _cols(pz, start, width):
    return pz[:, start:start + width]


def _layer_fwd(z, modseg, lp, cos, sin, m_ctx, tile, s5_chunk, l):
    t = z.shape[0]
    nct = m_ctx // tile
    nm = lambda s: f"{s}_l{l}"
    (h,) = rowwise_fwd(f_pre, nm("pre"), [z], [modseg], [lp["g_pre"]], [D], tile, nct)
    pz = mm_nn([h], lp["w_in"], nm("in_proj"), tm=tile)
    gt, pv, nk, nv, su = _cols(pz, C_GT, 1024), _cols(pz, C_GV, 256), _cols(pz, C_NK, 256), _cols(pz, C_NV, 256), _cols(pz, C_SU, 256)
    nq, pu, pk, pg, pq = _cols(pz, C_NQ, 256), _cols(pz, C_PU, 256), _cols(pz, C_GK, 128), _cols(pz, C_GG, 128), _cols(pz, C_GQ, 128)
    q_r, k_r, lgf, lgb = rowwise_fwd(f_gla_prep, nm("gla_prep"), [pk, pg, pq, cos, sin], [], [lp["wg"], lp["bg"]], [128] * 4, tile, nct)
    o1, st_f = gla_scan_fwd(q_r, k_r, pv, lgf, jnp.zeros((t, 256), F32), m_ctx, False, nm("gla_f"))
    o_gla, st_b = gla_scan_fwd(q_r, k_r, pv, lgb, o1, m_ctx, True, nm("gla_r"))
    o_na = na_fwd(nq, nk, nv, lp["bias8"], m_ctx, nm("na"))
    s5p = [whole_fwd(f_s5_params, nm(f"s5_par{d}"), lp["s5"][d], [(1, 1024)] * 2 + [(256, 1024)] * 4) for d in range(2)]
    y1, x0r_f, x0i_f = s5_scan_fwd(su, jnp.zeros((t, 256), F32), *s5p[0], m_ctx, s5_chunk, False, nm("s5_f"))
    y5, x0r_b, x0i_b = s5_scan_fwd(su, y1, *s5p[1], m_ctx, s5_chunk, True, nm("s5_r"))
    pm = jnp.concatenate([pool_apply(_pad_rows(pu[:m_ctx]), m_ctx, False, nm("pool_c")),
                          pool_apply(_pad_rows(pu[m_ctx:]), t - m_ctx, False, nm("pool_x"))], axis=0)
    mix_rows = [o_gla, o_na, y5, su, pm, gt]
    mix_globs = [lp["g_norm"], lp["s5_d"], lp["w_glu"], lp["b_glu"], lp["wpool"], lp["pool_scale"], lp["havg"], lp["e4"]]
    (yg,) = rowwise_fwd(f_mix, nm("mix"), mix_rows, [], mix_globs, [D], tile, nct)
    out = mm_nn([yg], lp["w_out"], nm("out_proj"), tm=tile)
    (z_new,) = rowwise_fwd(f_post, nm("post"), [z, out], [modseg], [lp["g_post"]], [D], tile, nct)
    saved = dict(z=z, h=h, pv=pv, nk=nk, nv=nv, su=su, nq=nq, pk=pk, pg=pg, pq=pq, q_r=q_r, k_r=k_r, lgf=lgf, lgb=lgb,
                 st_f=st_f, st_b=st_b, s5p=s5p, x0f=(x0r_f, x0i_f), x0b=(x0r_b, x0i_b), mix_rows=mix_rows, mix_globs=mix_globs,
                 yg=yg, out=out)
    return z_new, saved


def _f_pre_res(x, mod, g_pre):
    return f_pre(x, mod, g_pre)[0], x


def _layer_bwd(dz_new, sv, modseg, lp, cos, sin, m_ctx, tile, s5_chunk, l):
    t = dz_new.shape[0]
    nct = m_ctx // tile
    nm = lambda s: f"{s}_l{l}"
    g = {}
    dz_res, dout, dmod_post, g["g_post"] = rowwise_bwd(f_post, nm("post_b"), [sv["z"], sv["out"]], [modseg], [lp["g_post"]],
                                                       [dz_new], tile, nct, [True, True], [True])
    dyg = mm_nt([dout], lp["w_out"], nm("out_proj_dx"), tm=tile)
    (g["w_out"],) = mm_tn(sv["yg"], [dout], nm("out_proj_dw"), tm=tile)
    res = rowwise_bwd(f_mix, nm("mix_b"), sv["mix_rows"], [], sv["mix_globs"], [dyg], tile, nct, [True] * 6, [True] * 6 + [False] * 2)
    do_gla, do_na, dy5, dsu_a, dpm, dgt = res[:6]
    g["g_norm"], g["s5_d"], g["w_glu"], g["b_glu"], g["wpool"], g["pool_scale"] = res[6:]
    dpu = jnp.concatenate([pool_apply(_pad_rows(dpm[:m_ctx]), m_ctx, True, nm("pool_c_b")),
                           pool_apply(_pad_rows(dpm[m_ctx:]), t - m_ctx, True, nm("pool_x_b"))], axis=0)
    r_b = s5_scan_bwd(sv["su"], dy5, dsu_a, *sv["x0b"], *sv["s5p"][1], m_ctx, s5_chunk, True, nm("s5_r_b"))
    r_f = s5_scan_bwd(sv["su"], dy5, r_b[0], *sv["x0f"], *sv["s5p"][0], m_ctx, s5_chunk, False, nm("s5_f_b"))
    dsu = r_f[0]
    g["s5"] = [whole_bwd(f_s5_params, nm(f"s5_par{d}_b"), lp["s5"][d], list(r[1:]), [True] * 7 + [False] * 4)
               for d, r in ((0, r_f), (1, r_b))]
    dnq, dnk, dnv, dbias8 = na_bwd(sv["nq"], sv["nk"], sv["nv"], do_na, lp["bias8"], m_ctx, nm("na_b"))
    g["rpb"] = _na_rpb_grad(dbias8, nm("na_rpb_b"))
    zq, zv = jnp.zeros((t, 128), F32), jnp.zeros((t, 256), F32)
    dq1, dk1, dv1, dlgb = gla_scan_bwd(sv["q_r"], sv["k_r"], sv["pv"], sv["lgb"], sv["st_b"], do_gla, (zq, zq, zv), m_ctx, True, nm("gla_r_b"))
    dq_r, dk_r, dpv, dlgf = gla_scan_bwd(sv["q_r"], sv["k_r"], sv["pv"], sv["lgf"], sv["st_f"], do_gla, (dq1, dk1, dv1), m_ctx, False, nm("gla_f_b"))
    dpk, dpg, dpq, g["wg"], g["bg"] = rowwise_bwd(f_gla_prep, nm("gla_prep_b"), [sv["pk"], sv["pg"], sv["pq"], cos, sin], [],
                                                  [lp["wg"], lp["bg"]], [dq_r, dk_r, dlgf, dlgb], tile, nct,
                                                  [True, True, True, False, False], [True, True])
    parts = [dgt, dpv, dnk, dnv, dsu, dnq, dpu, dpk, dpg, dpq, jnp.zeros((t, 128), F32)]
    dh = mm_nt(parts, lp["w_in"], nm("in_proj_dx"), tm=tile)
    g["w_in"] = _unpad_w_in(jnp.concatenate(mm_tn(sv["h"], parts, nm("in_proj_dw"), tm=tile), axis=1))
    dz, dmod_pre, g["g_pre"] = rowwise_bwd(_f_pre_res, nm("pre_b"), [sv["z"]], [modseg], [lp["g_pre"]], [dh, dz_res], tile, nct, [True], [True])
    return dz, dmod_pre, dmod_post, g


def _f_mod_sum(cs, b_mod, w_mod):
    mod, _ = f_mod(cs, b_mod, w_mod)
    return mod, cs


def local_step(x, c, ctx, tgt, p, tile=ROW_TILE, s5_chunk=S5_CHUNK):
    n_lat, m_ctx = x.shape[0], ctx.shape[0]
    n_layers = p["w_in"].shape[0]
    z = jnp.concatenate([ctx, x], axis=0)
    cos, sin = _rope_tables(n_lat, m_ctx)
    cs = jnp.concatenate([c.reshape(1, D), p["c_ctx"].reshape(1, D), jnp.zeros((6, D), F32)], axis=0)
    lps, mods, silus, saves = [], [], [], []
    for l in range(n_layers):
        lp = _layer_params(p, l)
        mod8, s8 = whole_fwd(f_mod, f"mod_l{l}", [cs, lp["b_mod"], lp["w_mod"]], [(8, 3 * D), (8, D)])
        modseg = mod8[:2].reshape(2, 1, 3 * D)
        z, sv = _layer_fwd(z, modseg, lp, cos, sin, m_ctx, tile, s5_chunk, l)
        lps.append(lp); mods.append(modseg); silus.append(s8); saves.append(sv)
    loss, dz = loss_and_grad(z, tgt, m_ctx, "loss", tile)
    grads = [None] * n_layers
    dcs = jnp.zeros((8, D), F32)
    for l in reversed(range(n_layers)):
        lp = lps[l]
        dz, dmod_pre, dmod_post, g = _layer_bwd(dz, saves[l], mods[l], lp, cos, sin, m_ctx, tile, s5_chunk, l)
        dmod = jnp.concatenate([dmod_pre.reshape(2, 3 * D)[:, :2 * D], dmod_post.reshape(2, 3 * D)[:, 2 * D:]], axis=1)
        dmod8 = jnp.pad(dmod, ((0, 6), (0, 0)))
        dcs, g["b_mod"] = whole_bwd(_f_mod_sum, f"mod_b_l{l}", [cs, lp["b_mod"], lp["w_mod"]], [dmod8, dcs], [True, True, False])
        g["w_mod"] = jnp.concatenate(mm_tn(silus[l], [dmod8[:, :D], dmod8[:, D:2 * D], dmod8[:, 2 * D:]], f"mod_dw_l{l}", tm=8), axis=1)
        grads[l] = g
    return loss, dz[m_ctx:], dcs[1], grads


_WEIGHTS = ["c_ctx", "w_mod", "b_mod", "g_pre", "g_post", "w_in", "w_out", "gla_w_gate", "gla_b_gate", "gla_g_norm", "na_rpb",
            "s5_lam_re", "s5_lam_im", "s5_log_dt", "s5_b_re", "s5_b_im", "s5_c_re", "s5_c_im", "s5_d", "s5_w_glu", "s5_b_glu",
            "pool_w", "pool_scale"]
_INPUTS = ["x", "c", "ctx"] + _WEIGHTS + ["loss_target"] + ["m_" + n for n in _WEIGHTS] + ["v_" + n for n in _WEIGHTS]
_SHARDED = ["w_mod", "w_in", "w_out", "s5_w_glu"]
_SMALL = [n for n in _WEIGHTS if n not in _SHARDED]
_PACK_ROWS = 256


def _pack(arrs):
    flat = jnp.concatenate([a.reshape(-1) for a in arrs])
    quantum = _PACK_ROWS * 128
    total = -(-flat.shape[0] // quantum) * quantum
    return jnp.pad(flat, (0, total - flat.shape[0])).reshape(-1, 128)


def _unpack(packed, like):
    flat, out, pos = packed.reshape(-1), [], 0
    for a in like:
        out.append(flat[pos:pos + a.size].reshape(a.shape))
        pos += a.size
    return out


def _gathered(g, cols):
    n_layers = g.shape[1]
    if cols:
        return g.transpose(1, 2, 0, 3).reshape(n_layers, g.shape[2], N_DEV * g.shape[3])
    return g.transpose(1, 0, 2, 3).reshape(n_layers, N_DEV * g.shape[2], g.shape[3])


def _slabs(w, cols):
    n_layers, r, c = w.shape
    if cols:
        return w.reshape(n_layers, r, N_DEV, c // N_DEV).transpose(2, 0, 1, 3)
    return w.reshape(n_layers, N_DEV, r // N_DEV, c).transpose(1, 0, 2, 3)


def _small_grads(d_c_ctx, grads):
    n_layers = len(grads)
    st = lambda f: jnp.stack([f(grads[l]) for l in range(n_layers)])
    s5 = lambda i, f: st(lambda g: jnp.stack([f(g["s5"][d][i]) for d in range(2)]))
    return {
        "c_ctx": d_c_ctx,
        "b_mod": st(lambda g: g["b_mod"].reshape(3 * D)), "g_pre": st(lambda g: g["g_pre"].reshape(D)),
        "g_post": st(lambda g: g["g_post"].reshape(D)),
        "gla_w_gate": st(lambda g: jnp.stack([g["wg"][0:16, 0:128], g["wg"][16:32, 128:256]])),
        "gla_b_gate": st(lambda g: g["bg"].reshape(2, 128)), "gla_g_norm": st(lambda g: g["g_norm"][0]),
        "na_rpb": st(lambda g: g["rpb"]),
        "s5_lam_re": s5(0, lambda a: a), "s5_lam_im": s5(1, lambda a: a), "s5_log_dt": s5(2, lambda a: a.reshape(16)),
        "s5_b_re": s5(3, lambda a: a.reshape(16, 16, 64).transpose(0, 2, 1)),
        "s5_b_im": s5(4, lambda a: a.reshape(16, 16, 64).transpose(0, 2, 1)),
        "s5_c_re": s5(5, lambda a: a.reshape(16, 16, 64)), "s5_c_im": s5(6, lambda a: a.reshape(16, 16, 64)),
        "s5_d": st(lambda g: g["s5_d"].reshape(256)), "s5_b_glu": st(lambda g: g["b_glu"].reshape(256)),
        "pool_w": st(lambda g: jnp.stack([g["wpool"][64 * i:64 * i + 64, 64 * i:64 * i + 64] for i in range(4)])),
        "pool_scale": st(lambda g: g["pool_scale"].reshape(256)),
    }


def kernel(x, c, ctx, c_ctx, w_mod, b_mod, g_pre, g_post, w_in, w_out, gla_w_gate, gla_b_gate, gla_g_norm, na_rpb, s5_lam_re, s5_lam_im, s5_log_dt, s5_b_re, s5_b_im, s5_c_re, s5_c_im, s5_d, s5_w_glu, s5_b_glu, pool_w, pool_scale, loss_target, m_c_ctx, m_w_mod, m_b_mod, m_g_pre, m_g_post, m_w_in, m_w_out, m_gla_w_gate, m_gla_b_gate, m_gla_g_norm, m_na_rpb, m_s5_lam_re, m_s5_lam_im, m_s5_log_dt, m_s5_b_re, m_s5_b_im, m_s5_c_re, m_s5_c_im, m_s5_d, m_s5_w_glu, m_s5_b_glu, m_pool_w, m_pool_scale, v_c_ctx, v_w_mod, v_b_mod, v_g_pre, v_g_post, v_w_in, v_w_out, v_gla_w_gate, v_gla_b_gate, v_gla_g_norm, v_na_rpb, v_s5_lam_re, v_s5_lam_im, v_s5_log_dt, v_s5_b_re, v_s5_b_im, v_s5_c_re, v_s5_c_im, v_s5_d, v_s5_w_glu, v_s5_b_glu, v_pool_w, v_pool_scale):
    given = dict(zip(_INPUTS, (x, c, ctx, c_ctx, w_mod, b_mod, g_pre, g_post, w_in, w_out, gla_w_gate, gla_b_gate, gla_g_norm, na_rpb, s5_lam_re, s5_lam_im, s5_log_dt, s5_b_re, s5_b_im, s5_c_re, s5_c_im, s5_d, s5_w_glu, s5_b_glu, pool_w, pool_scale, loss_target, m_c_ctx, m_w_mod, m_b_mod, m_g_pre, m_g_post, m_w_in, m_w_out, m_gla_w_gate, m_gla_b_gate, m_gla_g_norm, m_na_rpb, m_s5_lam_re, m_s5_lam_im, m_s5_log_dt, m_s5_b_re, m_s5_b_im, m_s5_c_re, m_s5_c_im, m_s5_d, m_s5_w_glu, m_s5_b_glu, m_pool_w, m_pool_scale, v_c_ctx, v_w_mod, v_b_mod, v_g_pre, v_g_post, v_w_in, v_w_out, v_gla_w_gate, v_gla_b_gate, v_gla_g_norm, v_na_rpb, v_s5_lam_re, v_s5_lam_im, v_s5_log_dt, v_s5_b_re, v_s5_b_im, v_s5_c_re, v_s5_c_im, v_s5_d, v_s5_w_glu, v_s5_b_glu, v_pool_w, v_pool_scale)))
    by_cols = {"w_mod": True, "w_in": True, "w_out": False, "s5_w_glu": False}
    gathered = exchange([given[n].astype(BF16) for n in _SHARDED], [False] * len(_SHARDED), "gather_weights")
    p = {n: given[n] for n in _SMALL}
    for n, g in zip(_SHARDED, gathered):
        p[n] = _gathered(g, by_cols[n])
    loss, grad_x, d_c_ctx, grads = local_step(x[0], c, ctx[0], loss_target[0], p)
    full = {"w_mod": jnp.stack([g["w_mod"] for g in grads]), "w_in": jnp.stack([g["w_in"] for g in grads]),
            "w_out": jnp.stack([g["w_out"] for g in grads]), "s5_w_glu": jnp.stack([g["w_glu"] for g in grads])}
    small = _small_grads(d_c_ctx, grads)
    sends = [_slabs(full[n], by_cols[n]).astype(BF16) for n in _SHARDED] + [_pack([small[n] for n in _SMALL]).astype(BF16)]
    recvd = exchange(sends, [True] * len(_SHARDED) + [False], "exchange_grads")
    outs = {}
    for n, r in zip(_SHARDED, recvd):
        w = given[n]
        two_d = (w.shape[0] * w.shape[1], w.shape[2])
        res = adamw(r.reshape((N_DEV,) + two_d), w.reshape(two_d), given["m_" + n].reshape(two_d), given["v_" + n].reshape(two_d),
                    "adamw_" + n)
        outs[n] = [a.reshape(w.shape) for a in res]
    like = [given[n] for n in _SMALL]
    res = adamw(recvd[-1], _pack(like), _pack([given["m_" + n] for n in _SMALL]), _pack([given["v_" + n] for n in _SMALL]),
                "adamw_small")
    for kind, packed in enumerate(res):
        for n, a in zip(_SMALL, _unpack(packed, like)):
            outs.setdefault(n, [None] * 4)[kind] = a
    loss = lax.psum(loss, ("x", "y", "c"))
    return (loss, grad_x[None], *[outs[n][0] for n in _WEIGHTS], *[outs[n][1] for n in _WEIGHTS],
            *[outs[n][2] for n in _WEIGHTS], *[outs[n][3] for n in _WEIGHTS])
```

```python
import functools
import math

import numpy as np
import jax
import jax.numpy as jnp
from jax import lax
from jax.experimental import pallas as pl
from jax.experimental.pallas import tpu as pltpu

F32 = jnp.float32
BF16 = jnp.bfloat16
HIGHEST = lax.Precision.HIGHEST

D = 1024
GRID_W = 64
EPS = 1e-6
N_DEV = 8
C_GT, C_GV, C_NK, C_NV, C_SU, C_NQ, C_PU, C_GK, C_GG, C_GQ, C_END = 0, 1024, 1280, 1536, 1792, 2048, 2304, 2560, 2688, 2816, 2944
PW = 3072
N_CTX_ORIG = 416
N_IN = 2848
GLA_CHUNK = 64
S5_CHUNK = 256
ROW_TILE = 256
VMEM_LIMIT = 56 * 1024 * 1024

ADAM_LR, ADAM_B1, ADAM_B2, ADAM_EPS, ADAM_WD, ADAM_STEP = 0.001, 0.9, 0.999, 1e-08, 0.01, 10


def _cparams(**kw):
    return pltpu.CompilerParams(vmem_limit_bytes=VMEM_LIMIT, **kw)


def _dg(a, b, ca, cb, precision=None):
    return lax.dot_general(a, b, (((ca,), (cb,)), ((), ())), precision=precision, preferred_element_type=F32)


def hdot(a, b):
    return _dg(a, b, 1, 0, HIGHEST)


def hdot_nt(a, b):
    return _dg(a, b, 1, 1, HIGHEST)


def hdot_tn(a, b):
    return _dg(a, b, 0, 0, HIGHEST)


def b_nn(a, b):
    return _dg(a.astype(BF16), b.astype(BF16), 1, 0)


def b_nt(a, b):
    return _dg(a.astype(BF16), b.astype(BF16), 1, 1)


def b_tn(a, b):
    return _dg(a.astype(BF16), b.astype(BF16), 0, 0)


@jax.custom_vjp
def bdot(a, b):
    return b_nn(a, b)


def _bdot_fwd(a, b):
    return b_nn(a, b), (a, b)


def _bdot_bwd(res, ct):
    a, b = res
    return b_nt(ct, b).astype(a.dtype), b_tn(a, ct).astype(b.dtype)


bdot.defvjp(_bdot_fwd, _bdot_bwd)


def _log_sigmoid(z):
    return jnp.minimum(z, 0.0) - jnp.log(1.0 + jnp.exp(-jnp.abs(z)))


def _silu(z):
    return z * jax.nn.sigmoid(z)


def _gelu(z):
    return 0.5 * z * (1.0 + jnp.tanh(math.sqrt(2.0 / math.pi) * (z + 0.044715 * (z * z * z))))


def _cat(vals):
    return vals[0] if len(vals) == 1 else jnp.concatenate(vals, axis=-1)


def mm_nn(a_parts, b, name, tm=ROW_TILE, tn=1024):
    t = a_parts[0].shape[0]
    k, n = b.shape
    na = len(a_parts)
    tn = min(tn, n)

    def body(*refs):
        a = _cat([r[...].astype(BF16) for r in refs[:na]])
        refs[na + 1][...] = _dg(a, refs[na][...].astype(BF16), 1, 0)

    return pl.pallas_call(
        body, name=name, grid=(n // tn, t // tm),
        in_specs=[pl.BlockSpec((tm, p.shape[1]), lambda j, i: (i, 0)) for p in a_parts]
        + [pl.BlockSpec((k, tn), lambda j, i: (0, j))],
        out_specs=pl.BlockSpec((tm, tn), lambda j, i: (i, j)),
        out_shape=jax.ShapeDtypeStruct((t, n), F32),
        compiler_params=_cparams(dimension_semantics=("arbitrary", "arbitrary")),
    )(*a_parts, b)


def mm_nt(a_parts, b, name, tm=ROW_TILE):
    t = a_parts[0].shape[0]
    n, k = b.shape
    na = len(a_parts)

    def body(*refs):
        a = _cat([r[...].astype(BF16) for r in refs[:na]])
        refs[na + 1][...] = _dg(a, refs[na][...].astype(BF16), 1, 1)

    return pl.pallas_call(
        body, name=name, grid=(t // tm,),
        in_specs=[pl.BlockSpec((tm, p.shape[1]), lambda i: (i, 0)) for p in a_parts]
        + [pl.BlockSpec((n, k), lambda i: (0, 0))],
        out_specs=pl.BlockSpec((tm, n), lambda i: (i, 0)),
        out_shape=jax.ShapeDtypeStruct((t, n), F32),
        compiler_params=_cparams(dimension_semantics=("arbitrary",)),
    )(*a_parts, b)


def mm_tn(a, b_parts, name, tm=ROW_TILE, tn=1024):
    t, k = a.shape
    widths = [p.shape[1] for p in b_parts]
    n = sum(widths)
    assert n % tn == 0
    groups, cur, acc = [], [], 0
    for idx, w in enumerate(widths):
        cur.append(idx)
        acc += w
        if acc == tn:
            groups.append(cur)
            cur, acc = [], 0
        assert acc < tn
    assert not cur
    outs = []
    for gi, grp in enumerate(groups):
        parts = [b_parts[i] for i in grp]
        npart = len(parts)
        nsteps = t // tm

        def body(*refs, npart=npart):
            a_v = refs[0][...].astype(BF16)
            b_v = _cat([r[...].astype(BF16) for r in refs[1:1 + npart]])
            o_ref = refs[1 + npart]
            r = _dg(a_v, b_v, 0, 0)

            @pl.when(pl.program_id(0) == 0)
            def _():
                o_ref[...] = r

            @pl.when(pl.program_id(0) != 0)
            def _():
                o_ref[...] += r

        outs.append(pl.pallas_call(
            body, name=f"{name}_{gi}", grid=(nsteps,),
            in_specs=[pl.BlockSpec((tm, k), lambda i: (i, 0))]
            + [pl.BlockSpec((tm, p.shape[1]), lambda i: (i, 0)) for p in parts],
            out_specs=pl.BlockSpec((k, tn), lambda i: (0, 0)),
            out_shape=jax.ShapeDtypeStruct((k, tn), F32),
            compiler_params=_cparams(dimension_semantics=("arbitrary",)),
        )(a, *parts))
    return outs


def _seg_of(i, nct):
    return jnp.where(i < nct, 1, 0)


def rowwise_fwd(fn, name, rows, segs, globs, out_widths, tile, nct):
    t = rows[0].shape[0]
    nr, ns, ng = len(rows), len(segs), len(globs)

    def body(*refs):
        vals = [r[...] for r in refs[:nr]] + [r[0] for r in refs[nr:nr + ns]] + [r[...] for r in refs[nr + ns:nr + ns + ng]]
        outs = fn(*vals)
        for o_ref, o in zip(refs[nr + ns + ng:], outs):
            o_ref[...] = o

    return pl.pallas_call(
        body, name=name, grid=(t // tile,),
        in_specs=[pl.BlockSpec((tile, r.shape[1]), lambda i: (i, 0)) for r in rows]
        + [pl.BlockSpec((1, 1, s.shape[2]), lambda i: (_seg_of(i, nct), 0, 0)) for s in segs]
        + [pl.BlockSpec(g.shape, lambda i: (0, 0)) for g in globs],
        out_specs=[pl.BlockSpec((tile, w), lambda i: (i, 0)) for w in out_widths],
        out_shape=[jax.ShapeDtypeStruct((t, w), F32) for w in out_widths],
        compiler_params=_cparams(dimension_semantics=("arbitrary",)),
    )(*rows, *segs, *globs)


def rowwise_bwd(fn, name, rows, segs, globs, cts, tile, nct, row_diff, glob_diff):
    t = rows[0].shape[0]
    nr, ns, ng, nc = len(rows), len(segs), len(globs), len(cts)
    d_rows = [i for i in range(nr) if row_diff[i]]
    d_globs = [i for i in range(ng) if glob_diff[i]]

    def body(*refs):
        in_refs, out_refs = refs[:nr + ns + ng + nc], refs[nr + ns + ng + nc:]
        row_v = [r[...] for r in in_refs[:nr]]
        seg_v = [r[0] for r in in_refs[nr:nr + ns]]
        glob_v = [r[...] for r in in_refs[nr + ns:nr + ns + ng]]
        ct_v = tuple(r[...] for r in in_refs[nr + ns + ng:])

        def wrapped(dr, sv, dg):
            rv = list(row_v)
            for j, i in enumerate(d_rows):
                rv[i] = dr[j]
            gv = list(glob_v)
            for j, i in enumerate(d_globs):
                gv[i] = dg[j]
            return tuple(fn(*rv, *sv, *gv))

        _, vjp = jax.vjp(wrapped, [row_v[i] for i in d_rows], seg_v, [glob_v[i] for i in d_globs])
        c_rows, c_segs, c_globs = vjp(ct_v)
        i = pl.program_id(0)
        k = 0
        for c in c_rows:
            out_refs[k][...] = c
            k += 1
        seg_first = jnp.logical_or(i == 0, i == nct)
        for c in c_segs:
            ref = out_refs[k]
            k += 1

            @pl.when(seg_first)
            def _(ref=ref, c=c):
                ref[0] = c

            @pl.when(jnp.logical_not(seg_first))
            def _(ref=ref, c=c):
                ref[0] += c
        for c in c_globs:
            ref = out_refs[k]
            k += 1

            @pl.when(i == 0)
            def _(ref=ref, c=c):
                ref[...] = c

            @pl.when(i != 0)
            def _(ref=ref, c=c):
                ref[...] += c

    return pl.pallas_call(
        body, name=name, grid=(t // tile,),
        in_specs=[pl.BlockSpec((tile, r.shape[1]), lambda i: (i, 0)) for r in rows]
        + [pl.BlockSpec((1, 1, s.shape[2]), lambda i: (_seg_of(i, nct), 0, 0)) for s in segs]
        + [pl.BlockSpec(g.shape, lambda i: (0, 0)) for g in globs]
        + [pl.BlockSpec((tile, c.shape[1]), lambda i: (i, 0)) for c in cts],
        out_specs=[pl.BlockSpec((tile, rows[i].shape[1]), lambda i: (i, 0)) for i in d_rows]
        + [pl.BlockSpec((1, 1, s.shape[2]), lambda i: (_seg_of(i, nct), 0, 0)) for s in segs]
        + [pl.BlockSpec(globs[i].shape, lambda i: (0, 0)) for i in d_globs],
        out_shape=[jax.ShapeDtypeStruct(rows[i].shape, F32) for i in d_rows]
        + [jax.ShapeDtypeStruct(s.shape, F32) for s in segs]
        + [jax.ShapeDtypeStruct(globs[i].shape, F32) for i in d_globs],
        compiler_params=_cparams(dimension_semantics=("arbitrary",)),
    )(*rows, *segs, *globs, *cts)


def f_pre(x, mod, g_pre):
    shift, scale = mod[:, :D], mod[:, D:2 * D]
    rs = lax.rsqrt(jnp.mean(x * x, axis=-1, keepdims=True) + EPS)
    return ((x * rs) * g_pre * (1.0 + scale) + shift,)


def f_post(x, out, mod, g_post):
    gate = mod[:, 2 * D:]
    rs = lax.rsqrt(jnp.mean(out * out, axis=-1, keepdims=True) + EPS)
    return (x + gate * ((out * rs) * g_post),)


def f_mix(o_gla, o_na, y5, u5, pm, gcols, g_norm, s5_d, w_glu, b_glu, wpool, pool_scale, havg, e4):
    ms = hdot(o_gla * o_gla, havg)
    y_gla = o_gla * lax.rsqrt(ms + EPS) * jnp.sum(hdot(g_norm, e4), axis=0, keepdims=True)
    g = _gelu(u5 * s5_d + y5)
    y_s5 = g * jax.nn.sigmoid(bdot(g, w_glu) + b_glu)
    y_pool = bdot(pm, wpool) * pool_scale
    ycat = jnp.concatenate([y_gla, o_na, y_s5, y_pool], axis=-1)
    return (ycat * _silu(gcols),)


@jax.custom_vjp
def _rot_half16(x):
    lane = lax.broadcasted_iota(jnp.int32, x.shape, 1)
    first = jnp.bitwise_and(lane, 15) < 8
    return jnp.where(first, -pltpu.roll(x, x.shape[1] - 8, 1), pltpu.roll(x, 8, 1))


def _rot_fwd(x):
    return _rot_half16(x), None


def _rot_bwd(_, ct):
    return (-_rot_half16(ct),)


_rot_half16.defvjp(_rot_fwd, _rot_bwd)


def f_gla_prep(pk, pg, pq, cos, sin, wg, bg):
    z = bdot(pg, wg) + bg
    lg = _log_sigmoid(z) * (1.0 / 16.0)
    k_r = pk * cos + _rot_half16(pk) * sin
    q_r = (pq * cos + _rot_half16(pq) * sin) * (32.0 ** -0.5)
    return q_r, k_r, lg[:, :128], lg[:, 128:]


def _gla_consts(rev):
    c = GLA_CHUNK
    i = np.arange(c)
    inc = (i[None, :] >= i[:, None]) if rev else (i[None, :] <= i[:, None])
    mq = np.stack([(np.arange(128) // 32 == h) for h in range(4)]).astype(np.float32).reshape(4, 1, 128)
    mv = np.stack([(np.arange(256) // 64 == h) for h in range(4)]).astype(np.float32).reshape(4, 1, 256)
    bdt = (np.arange(256)[:, None] // 64 == np.arange(128)[None, :] // 32).astype(np.float32)
    return jnp.asarray(inc.astype(np.float32)), jnp.asarray(mq), jnp.asarray(mv), jnp.asarray(bdt)


def _gla_chunk_of(s, n_ctx_chunks, n_chunks, rev):
    if not rev:
        return s
    return jnp.where(s < n_ctx_chunks, n_ctx_chunks - 1 - s, n_ctx_chunks + n_chunks - 1 - s)


def gla_scan_fwd(q, k, v, lg, acc, n_ctx_rows, rev, name):
    t = q.shape[0]
    nch, ncc = t // GLA_CHUNK, n_ctx_rows // GLA_CHUNK
    inc, mq, mv, bdt = _gla_consts(rev)

    def body(q_ref, k_ref, v_ref, lg_ref, acc_ref, inc_ref, mq_ref, mv_ref, bdt_ref, o_ref, st_ref):
        lmask = inc_ref[...]
        bd = bdt_ref[...]

        def step(s, st):
            c = _gla_chunk_of(s, ncc, nch, rev)
            rows = pl.ds(pl.multiple_of(c * GLA_CHUNK, GLA_CHUNK), GLA_CHUNK)
            qc, kc, vc, lgc = q_ref[rows, :], k_ref[rows, :], v_ref[rows, :], lg_ref[rows, :]
            st_ref[c] = st
            b = hdot(lmask, lgc)
            blast = jnp.sum(lgc, axis=0, keepdims=True)
            qe, ke, kd = qc * jnp.exp(b), kc * jnp.exp(-b), kc * jnp.exp(blast - b)
            o = acc_ref[rows, :] + b_nt(qe, st)
            for h in range(4):
                a = lmask * b_nt(qe * mq_ref[h], ke)
                o = o + b_nn(a, vc * mv_ref[h])
            o_ref[rows, :] = o
            return st * jnp.exp(blast) + bd * hdot_tn(vc, kd)

        lax.fori_loop(0, nch, step, jnp.zeros((256, 128), F32))

    vm = pl.BlockSpec(memory_space=pltpu.VMEM)
    return pl.pallas_call(
        body, name=name, in_specs=[vm] * 9, out_specs=[vm, vm],
        out_shape=[jax.ShapeDtypeStruct((t, 256), F32), jax.ShapeDtypeStruct((nch, 256, 128), F32)],
        compiler_params=_cparams(),
    )(q, k, v, lg, acc, inc, mq, mv, bdt)


def gla_scan_bwd(q, k, v, lg, st, do, acc, n_ctx_rows, rev, name):
    t = q.shape[0]
    nch, ncc = t // GLA_CHUNK, n_ctx_rows // GLA_CHUNK
    inc, mq, mv, bdt = _gla_consts(rev)

    def body(q_ref, k_ref, v_ref, lg_ref, st_ref, do_ref, aq_ref, ak_ref, av_ref, inc_ref, mq_ref, mv_ref, bdt_ref,
             dq_ref, dk_ref, dv_ref, dlg_ref):
        lmask = inc_ref[...]
        bd = bdt_ref[...]

        def step(j, carry):
            dst, gsum = carry
            s = nch - 1 - j
            c = _gla_chunk_of(s, ncc, nch, rev)
            rows = pl.ds(pl.multiple_of(c * GLA_CHUNK, GLA_CHUNK), GLA_CHUNK)
            qc, kc, vc, lgc, doc = q_ref[rows, :], k_ref[rows, :], v_ref[rows, :], lg_ref[rows, :], do_ref[rows, :]
            stc = st_ref[c]
            b = hdot(lmask, lgc)
            blast = jnp.sum(lgc, axis=0, keepdims=True)
            eb, enb, edb = jnp.exp(b), jnp.exp(-b), jnp.exp(blast - b)
            qe, ke, kd = qc * eb, kc * enb, kc * edb
            dqe = hdot(doc, stc)
            dke = jnp.zeros_like(ke)
            dv = b_nt(kd, dst)
            for h in range(4):
                qh = qe * mq_ref[h]
                a = lmask * b_nt(qh, ke)
                doh = doc * mv_ref[h]
                da = lmask * hdot_nt(doh, vc)
                dqe = dqe + mq_ref[h] * hdot(da, ke)
                dke = dke + mq_ref[h] * hdot_tn(da, qe)
                dv = dv + mv_ref[h] * b_tn(a, doc)
            dkd = hdot(vc, dst)
            dq = dqe * eb
            dk = dke * enb + dkd * edb
            g = qc * dq - kc * dk
            dlg_ref[rows, :] = hdot_tn(lmask, g) + gsum
            dq_ref[rows, :] = aq_ref[rows, :] + dq
            dk_ref[rows, :] = ak_ref[rows, :] + dk
            dv_ref[rows, :] = av_ref[rows, :] + dv
            dst_new = dst * jnp.exp(blast) + bd * hdot_tn(doc, qe)
            return dst_new, gsum + jnp.sum(g, axis=0, keepdims=True)

        lax.fori_loop(0, nch, step, (jnp.zeros((256, 128), F32), jnp.zeros((1, 128), F32)))

    vm = pl.BlockSpec(memory_space=pltpu.VMEM)
    return pl.pallas_call(
        body, name=name, in_specs=[vm] * 13, out_specs=[vm] * 4,
        out_shape=[jax.ShapeDtypeStruct((t, 128), F32), jax.ShapeDtypeStruct((t, 128), F32),
                   jax.ShapeDtypeStruct((t, 256), F32), jax.ShapeDtypeStruct((t, 128), F32)],
        compiler_params=_cparams(),
    )(q, k, v, lg, st, do, *acc, inc, mq, mv, bdt)


def whole_fwd(fn, name, args, out_shapes):
    def body(*refs):
        outs = fn(*[r[...] for r in refs[:len(args)]])
        for o_ref, o in zip(refs[len(args):], outs):
            o_ref[...] = o

    vm = pl.BlockSpec(memory_space=pltpu.VMEM)
    return pl.pallas_call(
        body, name=name, in_specs=[vm] * len(args), out_specs=[vm] * len(out_shapes),
        out_shape=[jax.ShapeDtypeStruct(s, F32) for s in out_shapes], compiler_params=_cparams(),
    )(*args)


def whole_bwd(fn, name, args, cts, diff):
    d_idx = [i for i in range(len(args)) if diff[i]]

    def body(*refs):
        vals = [r[...] for r in refs[:len(args)]]
        ct_v = tuple(r[...] for r in refs[len(args):len(args) + len(cts)])

        def wrapped(dv):
            av = list(vals)
            for j, i in enumerate(d_idx):
                av[i] = dv[j]
            return tuple(fn(*av))

        _, vjp = jax.vjp(wrapped, [vals[i] for i in d_idx])
        (c_args,) = vjp(ct_v)
        for o_ref, c in zip(refs[len(args) + len(cts):], c_args):
            o_ref[...] = c

    vm = pl.BlockSpec(memory_space=pltpu.VMEM)
    return pl.pallas_call(
        body, name=name, in_specs=[vm] * (len(args) + len(cts)), out_specs=[vm] * len(d_idx),
        out_shape=[jax.ShapeDtypeStruct(args[i].shape, F32) for i in d_idx], compiler_params=_cparams(),
    )(*args, *cts)


def _s5_consts():
    e_rep = (np.arange(256)[:, None] // 16 == np.arange(16)[None, :]).astype(np.float32)
    e_tile = (np.arange(64)[:, None] == np.arange(1024)[None, :] % 64).astype(np.float32)
    gmask = (np.arange(16)[:, None] == np.arange(1024)[None, :] // 64).astype(np.float32)
    bdm = (np.arange(256)[:, None] // 16 == np.arange(1024)[None, :] // 64).astype(np.float32)
    return jnp.asarray(e_rep), jnp.asarray(e_tile), jnp.asarray(gmask), jnp.asarray(bdm)


def f_s5_params(lam_re, lam_im, log_dt, bt_re, bt_im, ct_re, ct_im, e_rep, e_tile, gmask, bdm):
    dt = jnp.exp(log_dt)
    mag = jnp.exp(lam_re * dt)
    ang = lam_im * dt
    lb_re, lb_im = mag * jnp.cos(ang), mag * jnp.sin(ang)
    num_re, num_im = lb_re - 1.0, lb_im
    den = lam_re * lam_re + lam_im * lam_im
    coef_re = (num_re * lam_re + num_im * lam_im) / den
    coef_im = (num_im * lam_re - num_re * lam_im) / den
    cr, ci = hdot(e_rep, coef_re), hdot(e_rep, coef_im)
    bbt_re = cr * bt_re - ci * bt_im
    bbt_im = cr * bt_im + ci * bt_re
    a_re = jnp.sum(hdot(lb_re, e_tile) * gmask, axis=0, keepdims=True)
    a_im = jnp.sum(hdot(lb_im, e_tile) * gmask, axis=0, keepdims=True)
    return (a_re, a_im, hdot(bbt_re, e_tile) * bdm, hdot(bbt_im, e_tile) * bdm,
            hdot(ct_re, e_tile) * bdm, hdot(ct_im, e_tile) * bdm)


def _s5_scan(xr, xi, a_re, a_im, rev, chunk):
    row = lax.broadcasted_iota(jnp.int32, xr.shape, 0)
    pr, pi = a_re, a_im
    s = 1
    while s < chunk:
        if rev:
            keep = row < (chunk - s)
            sr, si = pltpu.roll(xr, chunk - s, 0), pltpu.roll(xi, chunk - s, 0)
        else:
            keep = row >= s
            sr, si = pltpu.roll(xr, s, 0), pltpu.roll(xi, s, 0)
        sr, si = jnp.where(keep, sr, 0.0), jnp.where(keep, si, 0.0)
        xr, xi = xr + pr * sr - pi * si, xi + pr * si + pi * sr
        pr, pi = pr * pr - pi * pi, 2.0 * pr * pi
        s *= 2
    return xr, xi


def _s5_chunk_states(u_c, x0r, x0i, a_re, a_im, bb_re, bb_im, rev, chunk):
    row = lax.broadcasted_iota(jnp.int32, (chunk, 1024), 0)
    first = row == (chunk - 1 if rev else 0)
    inj_r = a_re * x0r - a_im * x0i
    inj_i = a_re * x0i + a_im * x0r
    xr = b_nn(u_c, bb_re) + jnp.where(first, inj_r, 0.0)
    xi = b_nn(u_c, bb_im) + jnp.where(first, inj_i, 0.0)
    return _s5_scan(xr, xi, a_re, a_im, rev, chunk)


def _row_pick(x, idx):
    row = lax.broadcasted_iota(jnp.int32, x.shape, 0)
    return jnp.sum(jnp.where(row == idx, x, 0.0), axis=0, keepdims=True)


def s5_scan_fwd(u, acc, a_re, a_im, bb_re, bb_im, cc_re, cc_im, n_ctx_rows, chunk, rev, name):
    t = u.shape[0]
    nch, ncc = t // chunk, n_ctx_rows // chunk

    def body(u_ref, acc_ref, ar_ref, ai_ref, br_ref, bi_ref, cr_ref, ci_ref, y_ref, x0r_ref, x0i_ref):
        a_r, a_i = ar_ref[...], ai_ref[...]

        def step(s, carry):
            x0r, x0i = carry
            c = _gla_chunk_of(s, ncc, nch, rev)
            rows = pl.ds(pl.multiple_of(c * chunk, chunk), chunk)
            x0r_ref[c] = x0r
            x0i_ref[c] = x0i
            xr, xi = _s5_chunk_states(u_ref[rows, :], x0r, x0i, a_r, a_i, br_ref[...], bi_ref[...], rev, chunk)
            y_ref[rows, :] = acc_ref[rows, :] + b_nt(xr, cr_ref[...]) - b_nt(xi, ci_ref[...])
            last = 0 if rev else chunk - 1
            return _row_pick(xr, last), _row_pick(xi, last)

        lax.fori_loop(0, nch, step, (jnp.zeros((1, 1024), F32), jnp.zeros((1, 1024), F32)))

    vm = pl.BlockSpec(memory_space=pltpu.VMEM)
    return pl.pallas_call(
        body, name=name, in_specs=[vm] * 8, out_specs=[vm] * 3,
        out_shape=[jax.ShapeDtypeStruct((t, 256), F32), jax.ShapeDtypeStruct((nch, 1, 1024), F32),
                   jax.ShapeDtypeStruct((nch, 1, 1024), F32)],
        compiler_params=_cparams(),
    )(u, acc, a_re, a_im, bb_re, bb_im, cc_re, cc_im)


def s5_scan_bwd(u, dy, du_acc, x0r, x0i, a_re, a_im, bb_re, bb_im, cc_re, cc_im, n_ctx_rows, chunk, rev, name):
    t = u.shape[0]
    nch, ncc = t // chunk, n_ctx_rows // chunk

    def body(u_ref, dy_ref, dua_ref, x0r_ref, x0i_ref, ar_ref, ai_ref, br_ref, bi_ref, cr_ref, ci_ref,
             du_ref, dar_ref, dai_ref, dbr_ref, dbi_ref, dcr_ref, dci_ref):
        a_r, a_i = ar_ref[...], ai_ref[...]
        for ref in (dbr_ref, dbi_ref, dcr_ref, dci_ref):
            ref[...] = jnp.zeros_like(ref)
        row = lax.broadcasted_iota(jnp.int32, (chunk, 1024), 0)
        first_idx, last_idx = (chunk - 1, 0) if rev else (0, chunk - 1)

        def step(j, carry):
            lcr, lci, dar, dai = carry
            s = nch - 1 - j
            c = _gla_chunk_of(s, ncc, nch, rev)
            rows = pl.ds(pl.multiple_of(c * chunk, chunk), chunk)
            u_c, dy_c = u_ref[rows, :], dy_ref[rows, :]
            x0r_c, x0i_c = x0r_ref[c], x0i_ref[c]
            xr, xi = _s5_chunk_states(u_c, x0r_c, x0i_c, a_r, a_i, br_ref[...], bi_ref[...], rev, chunk)
            dcr_ref[...] += b_tn(dy_c, xr)
            dci_ref[...] -= b_tn(dy_c, xi)
            inj_r = a_r * lcr + a_i * lci
            inj_i = a_r * lci - a_i * lcr
            is_last = row == last_idx
            lr = b_nn(dy_c, cr_ref[...]) + jnp.where(is_last, inj_r, 0.0)
            li = -b_nn(dy_c, ci_ref[...]) + jnp.where(is_last, inj_i, 0.0)
            lr, li = _s5_scan(lr, li, a_r, -a_i, not rev, chunk)
            du_ref[rows, :] = dua_ref[rows, :] + b_nt(lr, br_ref[...]) + b_nt(li, bi_ref[...])
            dbr_ref[...] += b_tn(u_c, lr)
            dbi_ref[...] += b_tn(u_c, li)
            if rev:
                pr, pi = pltpu.roll(xr, chunk - 1, 0), pltpu.roll(xi, chunk - 1, 0)
            else:
                pr, pi = pltpu.roll(xr, 1, 0), pltpu.roll(xi, 1, 0)
            is_first = row == first_idx
            pr, pi = jnp.where(is_first, x0r_c, pr), jnp.where(is_first, x0i_c, pi)
            dar = dar + jnp.sum(lr * pr + li * pi, axis=0, keepdims=True)
            dai = dai + jnp.sum(li * pr - lr * pi, axis=0, keepdims=True)
            return _row_pick(lr, first_idx), _row_pick(li, first_idx), dar, dai

        z = jnp.zeros((1, 1024), F32)
        _, _, dar, dai = lax.fori_loop(0, nch, step, (z, z, z, z))
        dar_ref[...] = dar
        dai_ref[...] = dai

    vm = pl.BlockSpec(memory_space=pltpu.VMEM)
    big = jax.ShapeDtypeStruct((256, 1024), F32)
    vec = jax.ShapeDtypeStruct((1, 1024), F32)
    return pl.pallas_call(
        body, name=name, in_specs=[vm] * 11, out_specs=[vm] * 7,
        out_shape=[jax.ShapeDtypeStruct((t, 256), F32), vec, vec, big, big, big, big],
        compiler_params=_cparams(),
    )(u, dy, du_acc, x0r, x0i, a_re, a_im, bb_re, bb_im, cc_re, cc_im)


POOL_HALO = 8


def pool_apply(u_pad, n, transpose, name, tile=ROW_TILE):
    tile = min(tile, n)
    ext = tile + 2 * POOL_HALO

    def body(u_ref, o_ref):
        lax.fori_loop(0, n // tile, functools.partial(step, u_ref, o_ref), 0)

    def step(u_ref, o_ref, i, carry):
        val = u_ref[pl.ds(pl.multiple_of(i * tile, tile), ext), :]
        lane = lax.broadcasted_iota(jnp.int32, (ext, 256), 1)
        half = jnp.left_shift(1, jnp.right_shift(lane, 6))
        trow = lax.broadcasted_iota(jnp.int32, (ext, 256), 0) + (i * tile - POOL_HALO)
        cnt = jnp.minimum(trow + half, n) - jnp.maximum(trow - half, 0)
        inv = 1.0 / jnp.maximum(cnt, 1).astype(F32)
        src = val * inv if transpose else val
        acc = jnp.zeros((tile, 256), F32)
        for d in range(-POOL_HALO, POOL_HALO):
            in_win = jnp.logical_and(d >= -half, d <= half - 1)[POOL_HALO:POOL_HALO + tile]
            shift = d if transpose else -d
            rolled = pltpu.roll(src, shift % ext, 0)[POOL_HALO:POOL_HALO + tile]
            acc = acc + jnp.where(in_win, rolled, 0.0)
        centre = val[POOL_HALO:POOL_HALO + tile]
        if not transpose:
            acc = acc * inv[POOL_HALO:POOL_HALO + tile]
        o_ref[pl.ds(pl.multiple_of(i * tile, tile), tile), :] = acc - centre
        return carry

    vm = pl.BlockSpec(memory_space=pltpu.VMEM)
    return pl.pallas_call(
        body, name=name, in_specs=[vm], out_specs=vm,
        out_shape=jax.ShapeDtypeStruct((n, 256), F32), compiler_params=_cparams(),
    )(u_pad)


NA_SCALE = 64.0 ** -0.5
NEG = -1e30


def _call_with_exchange(compute, name, args, out_shapes, comm):
    vm = pl.BlockSpec(memory_space=pltpu.VMEM)
    n_in, n_out = len(args), len(out_shapes)
    if comm is None:
        outs = pl.pallas_call(compute, name=name, in_specs=[vm] * n_in, out_specs=[vm] * n_out, out_shape=out_shapes,
                              compiler_params=_cparams())(*args)
        return outs, None
    arrays, scatter = comm
    n = len(arrays)

    def body(*refs):
        c_in = refs[n_in:n_in + n]
        c_out = refs[n_in + n + n_out:n_in + 2 * n + n_out]
        finish = _exchange_issue(c_in, c_out, scatter, *refs[n_in + 2 * n + n_out:])
        compute(*refs[:n_in], *refs[n_in + n:n_in + n + n_out])
        finish()

    hbm = pl.BlockSpec(memory_space=pl.ANY)
    outs = pl.pallas_call(
        body, name=name, in_specs=[vm] * n_in + [hbm] * n, out_specs=[vm] * n_out + [hbm] * n,
        out_shape=list(out_shapes) + _exchange_out_shapes(arrays, scatter), scratch_shapes=_exchange_sems(n),
        compiler_params=_cparams(has_side_effects=True),
    )(*args, *arrays)
    return outs[:n_out], outs[n_out:]


def _na_head_masks():
    return jnp.asarray(np.stack([(np.arange(256) // 64 == h) for h in range(4)]).astype(np.float32).reshape(4, 1, 256))


def _na_window(r, rows):
    start = jnp.clip(r - 4, 0, rows - 8)
    return start, start - r + 7


def _na_probs(qh, kw, kc, bias):
    s_c = b_nt(qh, kc)
    m = jnp.max(s_c, axis=-1, keepdims=True)
    if kw is not None:
        s_w = b_nt(qh, kw) + bias
        m = jnp.maximum(m, jnp.max(s_w, axis=-1, keepdims=True))
        p_w = jnp.exp(s_w - m)
    p_c = jnp.exp(s_c - m)
    l = jnp.sum(p_c, axis=-1, keepdims=True)
    if kw is not None:
        l = l + jnp.sum(p_w, axis=-1, keepdims=True)
        return p_w / l, p_c / l
    return None, p_c / l


def na_fwd(q, k, v, bias8, n_ctx_rows, name, comm=None):
    t = q.shape[0]
    m_ctx = n_ctx_rows
    rows = (t - m_ctx) // GRID_W
    hm = _na_head_masks()

    def body(q_ref, k_ref, v_ref, b_ref, hm_ref, o_ref):
        kc, vc = k_ref[0:m_ctx, :], v_ref[0:m_ctx, :]

        def ctx_step(i, _):
            rs = pl.ds(pl.multiple_of(i * 64, 64), 64)
            qr = q_ref[rs, :] * NA_SCALE
            o = jnp.zeros((64, 256), F32)
            for h in range(4):
                _, p_c = _na_probs(qr * hm_ref[h], None, kc, None)
                o = o + b_nn(p_c, vc * hm_ref[h])
            o_ref[rs, :] = o
            return 0

        lax.fori_loop(0, m_ctx // 64, ctx_step, 0)

        def lat_step(r, _):
            start, off = _na_window(r, rows)
            rs = pl.ds(pl.multiple_of(m_ctx + r * 64, 64), 64)
            ws = pl.ds(pl.multiple_of(m_ctx + start * 64, 64), 512)
            qr = q_ref[rs, :] * NA_SCALE
            kw, vw = k_ref[ws, :], v_ref[ws, :]
            o = jnp.zeros((64, 256), F32)
            for h in range(4):
                p_w, p_c = _na_probs(qr * hm_ref[h], kw, kc, b_ref[h, off])
                o = o + b_nn(p_w, vw * hm_ref[h]) + b_nn(p_c, vc * hm_ref[h])
            o_ref[rs, :] = o
            return 0

        lax.fori_loop(0, rows, lat_step, 0)

    (o,), received = _call_with_exchange(body, name, [q, k, v, bias8, hm], [jax.ShapeDtypeStruct((t, 256), F32)], comm)
    return o if comm is None else (o, received)


def na_bwd(q, k, v, do, bias8, n_ctx_rows, name, comm=None):
    t = q.shape[0]
    m_ctx = n_ctx_rows
    rows = (t - m_ctx) // GRID_W
    hm = _na_head_masks()

    def body(q_ref, k_ref, v_ref, do_ref, b_ref, hm_ref, dq_ref, dk_ref, dv_ref, db_ref):
        kc, vc = k_ref[0:m_ctx, :], v_ref[0:m_ctx, :]
        dk_ref[...] = jnp.zeros_like(dk_ref)
        dv_ref[...] = jnp.zeros_like(dv_ref)
        db_ref[...] = jnp.zeros_like(db_ref)

        def head_terms(qh, doh, kw, vw, bias):
            p_w, p_c = _na_probs(qh, kw, kc, bias)
            dp_c = b_nt(doh, vc)
            delta = jnp.sum(p_c * dp_c, axis=-1, keepdims=True)
            if kw is not None:
                dp_w = b_nt(doh, vw)
                delta = delta + jnp.sum(p_w * dp_w, axis=-1, keepdims=True)
                ds_w = p_w * (dp_w - delta)
            else:
                ds_w = None
            ds_c = p_c * (dp_c - delta)
            return p_w, p_c, ds_w, ds_c

        def ctx_step(i, carry):
            dkc, dvc = carry
            rs = pl.ds(pl.multiple_of(i * 64, 64), 64)
            qr, dor = q_ref[rs, :] * NA_SCALE, do_ref[rs, :]
            dq = jnp.zeros((64, 256), F32)
            for h in range(4):
                qh, doh = qr * hm_ref[h], dor * hm_ref[h]
                _, p_c, _, ds_c = head_terms(qh, doh, None, None, None)
                dq = dq + hm_ref[h] * b_nn(ds_c, kc)
                dkc = dkc + b_tn(ds_c, qh)
                dvc = dvc + b_tn(p_c, doh)
            dq_ref[rs, :] = dq * NA_SCALE
            return dkc, dvc

        zc = jnp.zeros((m_ctx, 256), F32)
        carry = lax.fori_loop(0, m_ctx // 64, ctx_step, (zc, zc))

        def lat_step(r, carry):
            dkc, dvc = carry
            start, off = _na_window(r, rows)
            rs = pl.ds(pl.multiple_of(m_ctx + r * 64, 64), 64)
            ws = pl.ds(pl.multiple_of(m_ctx + start * 64, 64), 512)
            qr, dor = q_ref[rs, :] * NA_SCALE, do_ref[rs, :]
            kw, vw = k_ref[ws, :], v_ref[ws, :]
            dq = jnp.zeros((64, 256), F32)
            dkw = jnp.zeros((512, 256), F32)
            dvw = jnp.zeros((512, 256), F32)
            for h in range(4):
                qh, doh = qr * hm_ref[h], dor * hm_ref[h]
                p_w, p_c, ds_w, ds_c = head_terms(qh, doh, kw, vw, b_ref[h, off])
                dq = dq + hm_ref[h] * (b_nn(ds_w, kw) + b_nn(ds_c, kc))
                dkw = dkw + b_tn(ds_w, qh)
                dvw = dvw + b_tn(p_w, doh)
                dkc = dkc + b_tn(ds_c, qh)
                dvc = dvc + b_tn(p_c, doh)
                db_ref[h, off] += ds_w
            dq_ref[rs, :] = dq * NA_SCALE
            dk_ref[ws, :] += dkw
            dv_ref[ws, :] += dvw
            return dkc, dvc

        dkc, dvc = lax.fori_loop(0, rows, lat_step, carry)
        dk_ref[0:m_ctx, :] = dkc
        dv_ref[0:m_ctx, :] = dvc

    row = jax.ShapeDtypeStruct((t, 256), F32)
    return _call_with_exchange(body, name, [q, k, v, do, bias8, hm], [row, row, row, jax.ShapeDtypeStruct(bias8.shape, F32)], comm)


def _na_toeplitz():
    col = np.arange(GRID_W)
    dd = (col[None, :] - col[:, None] + 15).reshape(-1)
    tt = np.zeros((GRID_W * GRID_W, 128), np.float32)
    ok = (dd >= 0) & (dd <= 30)
    tt[np.arange(GRID_W * GRID_W)[ok], dd[ok]] = 1.0
    return tt


def _na_bias8(rpb, name):
    col = np.arange(GRID_W)
    cs = np.clip(col - 8, 0, GRID_W - 16)
    col_mask = (col[None, :] >= cs[:, None]) & (col[None, :] < cs[:, None] + 16)
    rpb2 = jnp.pad(rpb.reshape(60, 31), ((0, 4), (0, 97)))
    (toe,) = whole_fwd(lambda r_, t_: (hdot_nt(r_, t_),), name, [rpb2, jnp.asarray(_na_toeplitz())], [(64, GRID_W * GRID_W)])
    toe = toe[:60].reshape(4, 15, GRID_W, GRID_W)
    b = jnp.stack([toe[:, off:off + 8] for off in range(8)], axis=1)
    b = jnp.where(jnp.asarray(col_mask)[None, None, None], b, NEG)
    return b.transpose(0, 1, 3, 2, 4).reshape(4, 8, GRID_W, 8 * GRID_W)


def _na_rpb_grad(dbias8, name):
    tt = _na_toeplitz()
    sel = np.zeros((64, 256), np.float32)
    for h in range(4):
        for off in range(8):
            for i in range(8):
                sel[h * 15 + off + i, h * 64 + off * 8 + i] = 1.0
    a2 = dbias8.reshape(4, 8, GRID_W, 8, GRID_W).transpose(0, 1, 3, 2, 4).reshape(256, GRID_W * GRID_W)
    (out,) = whole_fwd(lambda a, t_, s_: (hdot(s_, hdot(a, t_)),), name, [a2, jnp.asarray(tt), jnp.asarray(sel)], [(64, 128)])
    return out[:60, :31].reshape(4, 15, 31)


def f_mod(cs, b_mod, w_mod):
    s = _silu(cs)
    return bdot(s, w_mod) + b_mod, s


def loss_and_grad(z, tgt, n_ctx_rows, name, tile=ROW_TILE):
    t, d = z.shape
    tile = min(tile, n_ctx_rows)
    nct = n_ctx_rows // tile

    def body(z_ref, t_ref, dz_ref, loss_ref):
        i = pl.program_id(0)

        @pl.when(i == 0)
        def _():
            loss_ref[...] = jnp.zeros_like(loss_ref)

        @pl.when(i < nct)
        def _():
            dz_ref[...] = jnp.zeros_like(dz_ref)

        @pl.when(i >= nct)
        def _():
            diff = z_ref[...] - t_ref[...]
            dz_ref[...] = diff * (1.0 / d)
            loss_ref[...] += 0.5 * jnp.sum(jnp.sum(diff * diff, axis=-1, keepdims=True) * (1.0 / d), axis=0, keepdims=True)

    dz, loss = pl.pallas_call(
        body, name=name, grid=(t // tile,),
        in_specs=[pl.BlockSpec((tile, d), lambda i: (i, 0)),
                  pl.BlockSpec((tile, d), lambda i: (jnp.maximum(i - nct, 0), 0))],
        out_specs=[pl.BlockSpec((tile, d), lambda i: (i, 0)), pl.BlockSpec((8, 128), lambda i: (0, 0))],
        out_shape=[jax.ShapeDtypeStruct((t, d), F32), jax.ShapeDtypeStruct((8, 128), F32)],
        compiler_params=_cparams(dimension_semantics=("arbitrary",)),
    )(z, tgt)
    return loss[0, 0], dz


def adamw(parts, w, m, v, name, tile=256):
    npart, r, c = parts.shape
    tile = min(tile, r)
    assert r % tile == 0
    c1 = 1.0 / (1.0 - ADAM_B1 ** ADAM_STEP)
    c2 = 1.0 / (1.0 - ADAM_B2 ** ADAM_STEP)

    def body(p_ref, w_ref, m_ref, v_ref, g_ref, d_ref, nm_ref, nv_ref):
        g = p_ref[0].astype(F32)
        for i in range(1, npart):
            g = g + p_ref[i].astype(F32)
        nm = ADAM_B1 * m_ref[...] + (1.0 - ADAM_B1) * g
        nv = ADAM_B2 * v_ref[...] + (1.0 - ADAM_B2) * (g * g)
        g_ref[...] = g
        nm_ref[...] = nm
        nv_ref[...] = nv
        d_ref[...] = -ADAM_LR * ((nm * c1) / (jnp.sqrt(nv * c2) + ADAM_EPS) + ADAM_WD * w_ref[...])

    blk = pl.BlockSpec((tile, c), lambda i: (i, 0))
    return pl.pallas_call(
        body, name=name, grid=(r // tile,),
        in_specs=[pl.BlockSpec((npart, tile, c), lambda i: (0, i, 0)), blk, blk, blk],
        out_specs=[blk] * 4, out_shape=[jax.ShapeDtypeStruct((r, c), F32)] * 4,
        compiler_params=_cparams(dimension_semantics=("arbitrary",)),
    )(parts, w, m, v)


def _peer(x, y, c, k):
    return (1 - x if k & 4 else x, 1 - y if k & 2 else y, 1 - c if k & 1 else c)


def _exchange_out_shapes(arrays, scatter):
    return [jax.ShapeDtypeStruct(a.shape if s else (N_DEV,) + a.shape, a.dtype) for a, s in zip(arrays, scatter)]


def _exchange_sems(n):
    return [pltpu.SemaphoreType.DMA((n, N_DEV - 1)), pltpu.SemaphoreType.DMA((n, N_DEV - 1)), pltpu.SemaphoreType.DMA((n,))]


def _exchange_issue(ins, outs, scatter, send_sems, recv_sems, local_sems):
    n = len(ins)
    x, y, c = lax.axis_index("x"), lax.axis_index("y"), lax.axis_index("c")
    me = 4 * x + 2 * y + c

    def index_of(p):
        return 4 * p[0] + 2 * p[1] + p[2]

    local = []
    for a in range(n):
        src_me = ins[a].at[me] if scatter[a] else ins[a]
        loc = pltpu.make_async_copy(src_me, outs[a].at[me], local_sems.at[a])
        loc.start()
        local.append(loc)
    for k in range(1, N_DEV):
        peer = _peer(x, y, c, k)
        for a in range(n):
            src = ins[a].at[index_of(peer)] if scatter[a] else ins[a]
            pltpu.make_async_remote_copy(
                src_ref=src, dst_ref=outs[a].at[me], send_sem=send_sems.at[a, k - 1], recv_sem=recv_sems.at[a, k - 1],
                device_id=peer, device_id_type=pl.DeviceIdType.MESH).start()

    def finish():
        for k in range(1, N_DEV):
            peer = _peer(x, y, c, k)
            for a in range(n):
                src = ins[a].at[index_of(peer)] if scatter[a] else ins[a]
                cp = pltpu.make_async_remote_copy(
                    src_ref=src, dst_ref=outs[a].at[index_of(peer)], send_sem=send_sems.at[a, k - 1],
                    recv_sem=recv_sems.at[a, k - 1], device_id=peer, device_id_type=pl.DeviceIdType.MESH)
                cp.wait_send()
                cp.wait_recv()
        for loc in local:
            loc.wait()

    return finish


def exchange(arrays, scatter, name):
    n = len(arrays)

    def body(*refs):
        _exchange_issue(refs[:n], refs[n:2 * n], scatter, *refs[2 * n:])()

    hbm = pl.BlockSpec(memory_space=pl.ANY)
    return pl.pallas_call(
        body, name=name, in_specs=[hbm] * n, out_specs=[hbm] * n, out_shape=_exchange_out_shapes(arrays, scatter),
        scratch_shapes=_exchange_sems(n), compiler_params=pltpu.CompilerParams(has_side_effects=True),
    )(*arrays)


def _rope_tables(n_lat, n_ctx):
    tok = np.arange(n_lat)
    freqs = 10000.0 ** (-np.arange(0, 16, 2, dtype=np.float32) / 16.0)

    def table(pos):
        ang = pos.astype(np.float32)[:, None] * freqs[None, :]
        ang = np.concatenate([ang, ang], axis=-1)
        return np.cos(ang), np.sin(ang)

    cr, sr = table(tok // GRID_W)
    cc, sc = table(tok % GRID_W)
    cos = np.tile(np.concatenate([cr, cc], axis=-1), (1, 4))
    sin = np.tile(np.concatenate([sr, sc], axis=-1), (1, 4))
    cos = np.concatenate([np.ones((n_ctx, 128), np.float32), cos], axis=0)
    sin = np.concatenate([np.zeros((n_ctx, 128), np.float32), sin], axis=0)
    return jnp.asarray(cos, F32), jnp.asarray(sin, F32)


def _pad_w_in(w):
    z = lambda n: jnp.zeros((w.shape[0], n), w.dtype)
    return jnp.concatenate([w[:, 1824:2848], w[:, 128:384], w[:, 416:672], w[:, 672:928], w[:, 928:1184], w[:, 1312:1568],
                            w[:, 1568:1824], w[:, 0:128], w[:, 384:416], z(96), w[:, 1184:1312], z(128)], axis=1)


def _unpad_w_in(wp):
    return jnp.concatenate([wp[:, C_GK:C_GK + 128], wp[:, C_GV:C_GV + 256], wp[:, C_GG:C_GG + 32], wp[:, C_NK:C_NK + 256],
                            wp[:, C_NV:C_NV + 256], wp[:, C_SU:C_SU + 256], wp[:, C_GQ:C_GQ + 128], wp[:, C_NQ:C_NQ + 256],
                            wp[:, C_PU:C_PU + 256], wp[:, C_GT:C_GT + 1024]], axis=1)


def _pad_rows(u):
    return jnp.pad(u, ((POOL_HALO, POOL_HALO), (0, 0)))


def _block_diag4(w):
    out = jnp.zeros((256, 256), w.dtype)
    for i in range(4):
        out = lax.dynamic_update_slice(out, w[i], (64 * i, 64 * i))
    return out


def _layer_params(p, big, l):
    e_rep, e_tile, gmask, bdm = _s5_consts()
    wg = jnp.zeros((128, 256), F32)
    wg = lax.dynamic_update_slice(wg, p["gla_w_gate"][l, 0], (0, 0))
    wg = lax.dynamic_update_slice(wg, p["gla_w_gate"][l, 1], (16, 128))
    s5 = []
    for d in range(2):
        s5.append([p["s5_lam_re"][l, d], p["s5_lam_im"][l, d], p["s5_log_dt"][l, d].reshape(16, 1),
                   p["s5_b_re"][l, d].transpose(0, 2, 1).reshape(256, 64), p["s5_b_im"][l, d].transpose(0, 2, 1).reshape(256, 64),
                   p["s5_c_re"][l, d].reshape(256, 64), p["s5_c_im"][l, d].reshape(256, 64), e_rep, e_tile, gmask, bdm])
    havg = jnp.asarray((np.arange(256)[:, None] // 64 == np.arange(256)[None, :] // 64).astype(np.float32) / 64.0)
    e4 = jnp.asarray((np.arange(64)[:, None] == np.arange(256)[None, :] % 64).astype(np.float32))
    return dict(
        g_pre=p["g_pre"][l].reshape(1, D), g_post=p["g_post"][l].reshape(1, D), b_mod=p["b_mod"][l].reshape(1, 3 * D),
        w_mod=big["w_mod"], w_in=_pad_w_in(big["w_in"]), w_out=big["w_out"],
        wg=wg, bg=p["gla_b_gate"][l].reshape(1, 256), g_norm=jnp.pad(p["gla_g_norm"][l].reshape(1, 64), ((0, 7), (0, 0))),
        bias8=_na_bias8(p["na_rpb"][l], f"na_bias_l{l}"), s5=s5, s5_d=p["s5_d"][l].reshape(1, 256), w_glu=big["s5_w_glu"].astype(F32),
        b_glu=p["s5_b_glu"][l].reshape(1, 256), wpool=_block_diag4(p["pool_w"][l]), pool_scale=p["pool_scale"][l].reshape(1, 256),
        havg=havg, e4=e4)


def _cols(pz, start, width):
    return pz[:, start:start + width]


def _layer_fwd(z, modseg, lp, cos, sin, m_ctx, tile, s5_chunk, l, comm=None):
    t = z.shape[0]
    nct = m_ctx // tile
    nm = lambda s: f"{s}_l{l}"
    (h,) = rowwise_fwd(f_pre, nm("pre"), [z], [modseg], [lp["g_pre"]], [D], tile, nct)
    pz = mm_nn([h], lp["w_in"], nm("in_proj"), tm=tile)
    gt, pv, nk, nv, su = _cols(pz, C_GT, 1024), _cols(pz, C_GV, 256), _cols(pz, C_NK, 256), _cols(pz, C_NV, 256), _cols(pz, C_SU, 256)
    nq, pu, pk, pg, pq = _cols(pz, C_NQ, 256), _cols(pz, C_PU, 256), _cols(pz, C_GK, 128), _cols(pz, C_GG, 128), _cols(pz, C_GQ, 128)
    q_r, k_r, lgf, lgb = rowwise_fwd(f_gla_prep, nm("gla_prep"), [pk, pg, pq, cos, sin], [], [lp["wg"], lp["bg"]], [128] * 4, tile, nct)
    o1, st_f = gla_scan_fwd(q_r, k_r, pv, lgf, jnp.zeros((t, 256), F32), m_ctx, False, nm("gla_f"))
    o_gla, st_b = gla_scan_fwd(q_r, k_r, pv, lgb, o1, m_ctx, True, nm("gla_r"))
    received = None
    if comm is None:
        o_na = na_fwd(nq, nk, nv, lp["bias8"], m_ctx, nm("na"))
    else:
        o_na, received = na_fwd(nq, nk, nv, lp["bias8"], m_ctx, nm("na"), comm)
    s5p = [whole_fwd(f_s5_params, nm(f"s5_par{d}"), lp["s5"][d], [(1, 1024)] * 2 + [(256, 1024)] * 4) for d in range(2)]
    y1, x0r_f, x0i_f = s5_scan_fwd(su, jnp.zeros((t, 256), F32), *s5p[0], m_ctx, s5_chunk, False, nm("s5_f"))
    y5, x0r_b, x0i_b = s5_scan_fwd(su, y1, *s5p[1], m_ctx, s5_chunk, True, nm("s5_r"))
    pm = jnp.concatenate([pool_apply(_pad_rows(pu[:m_ctx]), m_ctx, False, nm("pool_c")),
                          pool_apply(_pad_rows(pu[m_ctx:]), t - m_ctx, False, nm("pool_x"))], axis=0)
    mix_rows = [o_gla, o_na, y5, su, pm, gt]
    mix_globs = [lp["g_norm"], lp["s5_d"], lp["w_glu"], lp["b_glu"], lp["wpool"], lp["pool_scale"], lp["havg"], lp["e4"]]
    (yg,) = rowwise_fwd(f_mix, nm("mix"), mix_rows, [], mix_globs, [D], tile, nct)
    out = mm_nn([yg], lp["w_out"], nm("out_proj"), tm=tile)
    (z_new,) = rowwise_fwd(f_post, nm("post"), [z, out], [modseg], [lp["g_post"]], [D], tile, nct)
    saved = dict(z=z, h=h, pv=pv, nk=nk, nv=nv, su=su, nq=nq, pk=pk, pg=pg, pq=pq, q_r=q_r, k_r=k_r, lgf=lgf, lgb=lgb,
                 st_f=st_f, st_b=st_b, s5p=s5p, x0f=(x0r_f, x0i_f), x0b=(x0r_b, x0i_b), mix_rows=mix_rows, mix_globs=mix_globs,
                 yg=yg, out=out)
    return z_new, saved, received


def _f_pre_res(x, mod, g_pre):
    return f_pre(x, mod, g_pre)[0], x


def _layer_bwd(dz_new, sv, modseg, lp, cos, sin, m_ctx, tile, s5_chunk, l, comm=None):
    t = dz_new.shape[0]
    nct = m_ctx // tile
    nm = lambda s: f"{s}_l{l}"
    g = {}
    dz_res, dout, dmod_post, g["g_post"] = rowwise_bwd(f_post, nm("post_b"), [sv["z"], sv["out"]], [modseg], [lp["g_post"]],
                                                       [dz_new], tile, nct, [True, True], [True])
    dyg = mm_nt([dout], lp["w_out"], nm("out_proj_dx"), tm=tile)
    (g["w_out"],) = mm_tn(sv["yg"], [dout], nm("out_proj_dw"), tm=tile)
    res = rowwise_bwd(f_mix, nm("mix_b"), sv["mix_rows"], [], sv["mix_globs"], [dyg], tile, nct, [True] * 6, [True] * 6 + [False] * 2)
    do_gla, do_na, dy5, dsu_a, dpm, dgt = res[:6]
    g["g_norm"], g["s5_d"], g["w_glu"], g["b_glu"], g["wpool"], g["pool_scale"] = res[6:]
    dpu = jnp.concatenate([pool_apply(_pad_rows(dpm[:m_ctx]), m_ctx, True, nm("pool_c_b")),
                           pool_apply(_pad_rows(dpm[m_ctx:]), t - m_ctx, True, nm("pool_x_b"))], axis=0)
    r_b = s5_scan_bwd(sv["su"], dy5, dsu_a, *sv["x0b"], *sv["s5p"][1], m_ctx, s5_chunk, True, nm("s5_r_b"))
    r_f = s5_scan_bwd(sv["su"], dy5, r_b[0], *sv["x0f"], *sv["s5p"][0], m_ctx, s5_chunk, False, nm("s5_f_b"))
    dsu = r_f[0]
    g["s5"] = [whole_bwd(f_s5_params, nm(f"s5_par{d}_b"), lp["s5"][d], list(r[1:]), [True] * 7 + [False] * 4)
               for d, r in ((0, r_f), (1, r_b))]
    (dnq, dnk, dnv, dbias8), received = na_bwd(sv["nq"], sv["nk"], sv["nv"], do_na, lp["bias8"], m_ctx, nm("na_b"), comm)
    g["rpb"] = _na_rpb_grad(dbias8, nm("na_rpb_b"))
    zq, zv = jnp.zeros((t, 128), F32), jnp.zeros((t, 256), F32)
    dq1, dk1, dv1, dlgb = gla_scan_bwd(sv["q_r"], sv["k_r"], sv["pv"], sv["lgb"], sv["st_b"], do_gla, (zq, zq, zv), m_ctx, True, nm("gla_r_b"))
    dq_r, dk_r, dpv, dlgf = gla_scan_bwd(sv["q_r"], sv["k_r"], sv["pv"], sv["lgf"], sv["st_f"], do_gla, (dq1, dk1, dv1), m_ctx, False, nm("gla_f_b"))
    dpk, dpg, dpq, g["wg"], g["bg"] = rowwise_bwd(f_gla_prep, nm("gla_prep_b"), [sv["pk"], sv["pg"], sv["pq"], cos, sin], [],
                                                  [lp["wg"], lp["bg"]], [dq_r, dk_r, dlgf, dlgb], tile, nct,
                                                  [True, True, True, False, False], [True, True])
    parts = [dgt, dpv, dnk, dnv, dsu, dnq, dpu, dpk, dpg, dpq, jnp.zeros((t, 128), F32)]
    dh = mm_nt(parts, lp["w_in"], nm("in_proj_dx"), tm=tile)
    g["w_in"] = _unpad_w_in(jnp.concatenate(mm_tn(sv["h"], parts, nm("in_proj_dw"), tm=tile), axis=1))
    dz, dmod_pre, g["g_pre"] = rowwise_bwd(_f_pre_res, nm("pre_b"), [sv["z"]], [modseg], [lp["g_pre"]], [dh, dz_res], tile, nct, [True], [True])
    return dz, dmod_pre, dmod_post, g, received


def _f_mod_sum(cs, b_mod, w_mod):
    mod, _ = f_mod(cs, b_mod, w_mod)
    return mod, cs


def local_step(x, c, ctx, tgt, p, shards=None, tile=ROW_TILE, s5_chunk=S5_CHUNK):
    n_lat, m_ctx = x.shape[0], ctx.shape[0]
    n_layers = p["g_pre"].shape[0]
    z = jnp.concatenate([ctx, x], axis=0)
    cos, sin = _rope_tables(n_lat, m_ctx)
    cs = jnp.concatenate([c.reshape(1, D), p["c_ctx"].reshape(1, D), jnp.zeros((6, D), F32)], axis=0)
    gather = [False] * len(_SHARDED)
    lps, mods, silus, saves = [], [], [], []
    got = exchange(shards[0], gather, "gather_weights_l0") if shards is not None else None
    for l in range(n_layers):
        if shards is None:
            big = {n: p[n][l] for n in _SHARDED}
        else:
            big = {n: _gathered(g, _BY_COLS[n]) for n, g in zip(_SHARDED, got)}
        lp = _layer_params(p, big, l)
        mod8, s8 = whole_fwd(f_mod, f"mod_l{l}", [cs, lp["b_mod"], lp["w_mod"]], [(8, 3 * D), (8, D)])
        modseg = mod8[:2].reshape(2, 1, 3 * D)
        comm = (shards[l + 1], gather) if shards is not None and l + 1 < n_layers else None
        z, sv, got = _layer_fwd(z, modseg, lp, cos, sin, m_ctx, tile, s5_chunk, l, comm)
        lps.append(lp); mods.append(modseg); silus.append(s8); saves.append(sv)
    loss, dz = loss_and_grad(z, tgt, m_ctx, "loss", tile)
    grads, received = [None] * n_layers, [None] * n_layers
    dcs = jnp.zeros((8, D), F32)
    pending = None
    for l in reversed(range(n_layers)):
        lp = lps[l]
        dz, dmod_pre, dmod_post, g, got = _layer_bwd(dz, saves[l], mods[l], lp, cos, sin, m_ctx, tile, s5_chunk, l, pending)
        if pending is not None:
            received[l + 1] = got
        dmod = jnp.concatenate([dmod_pre.reshape(2, 3 * D)[:, :2 * D], dmod_post.reshape(2, 3 * D)[:, 2 * D:]], axis=1)
        dmod8 = jnp.pad(dmod, ((0, 6), (0, 0)))
        dcs, g["b_mod"] = whole_bwd(_f_mod_sum, f"mod_b_l{l}", [cs, lp["b_mod"], lp["w_mod"]], [dmod8, dcs], [True, True, False])
        g["w_mod"] = jnp.concatenate(mm_tn(silus[l], [dmod8[:, :D], dmod8[:, D:2 * D], dmod8[:, 2 * D:]], f"mod_dw_l{l}", tm=8), axis=1)
        grads[l] = g
        if shards is not None:
            pending = (_layer_sends(g, dcs[1] if l == 0 else None), [True] * len(_SHARDED) + [False])
    if shards is not None:
        received[0] = exchange(pending[0], pending[1], "exchange_grads_l0")
    return loss, dz[m_ctx:], dcs[1], grads, received


_WEIGHTS = ["c_ctx", "w_mod", "b_mod", "g_pre", "g_post", "w_in", "w_out", "gla_w_gate", "gla_b_gate", "gla_g_norm", "na_rpb",
            "s5_lam_re", "s5_lam_im", "s5_log_dt", "s5_b_re", "s5_b_im", "s5_c_re", "s5_c_im", "s5_d", "s5_w_glu", "s5_b_glu",
            "pool_w", "pool_scale"]
_INPUTS = ["x", "c", "ctx"] + _WEIGHTS + ["loss_target"] + ["m_" + n for n in _WEIGHTS] + ["v_" + n for n in _WEIGHTS]
_SHARDED = ["w_mod", "w_in", "w_out", "s5_w_glu"]
_BY_COLS = {"w_mod": True, "w_in": True, "w_out": False, "s5_w_glu": False}
_SMALL = [n for n in _WEIGHTS if n not in _SHARDED]
_SMALL_PER_LAYER = [n for n in _SMALL if n != "c_ctx"]
_PACK_ROWS = 256


def _pack(arrs):
    flat = jnp.concatenate([a.reshape(-1) for a in arrs])
    quantum = _PACK_ROWS * 128
    total = -(-flat.shape[0] // quantum) * quantum
    return jnp.pad(flat, (0, total - flat.shape[0])).reshape(-1, 128)


def _unpack(packed, like):
    flat, out, pos = packed.reshape(-1), [], 0
    for a in like:
        out.append(flat[pos:pos + a.size].reshape(a.shape))
        pos += a.size
    return out


def _gathered(g, cols):
    if cols:
        return g.transpose(1, 0, 2).reshape(g.shape[1], N_DEV * g.shape[2])
    return g.reshape(N_DEV * g.shape[1], g.shape[2])


def _slabs(w, cols):
    r, c = w.shape
    if cols:
        return w.reshape(r, N_DEV, c // N_DEV).transpose(1, 0, 2)
    return w.reshape(N_DEV, r // N_DEV, c)


def _layer_small(g):
    s5 = lambda i, f: jnp.stack([f(g["s5"][d][i]) for d in range(2)])
    return {
        "b_mod": g["b_mod"].reshape(3 * D), "g_pre": g["g_pre"].reshape(D), "g_post": g["g_post"].reshape(D),
        "gla_w_gate": jnp.stack([g["wg"][0:16, 0:128], g["wg"][16:32, 128:256]]),
        "gla_b_gate": g["bg"].reshape(2, 128), "gla_g_norm": g["g_norm"][0], "na_rpb": g["rpb"],
        "s5_lam_re": s5(0, lambda a: a), "s5_lam_im": s5(1, lambda a: a), "s5_log_dt": s5(2, lambda a: a.reshape(16)),
        "s5_b_re": s5(3, lambda a: a.reshape(16, 16, 64).transpose(0, 2, 1)),
        "s5_b_im": s5(4, lambda a: a.reshape(16, 16, 64).transpose(0, 2, 1)),
        "s5_c_re": s5(5, lambda a: a.reshape(16, 16, 64)), "s5_c_im": s5(6, lambda a: a.reshape(16, 16, 64)),
        "s5_d": g["s5_d"].reshape(256), "s5_b_glu": g["b_glu"].reshape(256),
        "pool_w": jnp.stack([g["wpool"][64 * i:64 * i + 64, 64 * i:64 * i + 64] for i in range(4)]),
        "pool_scale": g["pool_scale"].reshape(256),
    }


def _layer_sends(g, d_c_ctx):
    big = {"w_mod": g["w_mod"], "w_in": g["w_in"], "w_out": g["w_out"], "s5_w_glu": g["w_glu"]}
    small = _layer_small(g)
    packed = ([] if d_c_ctx is None else [d_c_ctx]) + [small[n] for n in _SMALL_PER_LAYER]
    return [_slabs(big[n], _BY_COLS[n]).astype(BF16) for n in _SHARDED] + [_pack(packed).astype(BF16)]


def kernel(x, c, ctx, c_ctx, w_mod, b_mod, g_pre, g_post, w_in, w_out, gla_w_gate, gla_b_gate, gla_g_norm, na_rpb, s5_lam_re, s5_lam_im, s5_log_dt, s5_b_re, s5_b_im, s5_c_re, s5_c_im, s5_d, s5_w_glu, s5_b_glu, pool_w, pool_scale, loss_target, m_c_ctx, m_w_mod, m_b_mod, m_g_pre, m_g_post, m_w_in, m_w_out, m_gla_w_gate, m_gla_b_gate, m_gla_g_norm, m_na_rpb, m_s5_lam_re, m_s5_lam_im, m_s5_log_dt, m_s5_b_re, m_s5_b_im, m_s5_c_re, m_s5_c_im, m_s5_d, m_s5_w_glu, m_s5_b_glu, m_pool_w, m_pool_scale, v_c_ctx, v_w_mod, v_b_mod, v_g_pre, v_g_post, v_w_in, v_w_out, v_gla_w_gate, v_gla_b_gate, v_gla_g_norm, v_na_rpb, v_s5_lam_re, v_s5_lam_im, v_s5_log_dt, v_s5_b_re, v_s5_b_im, v_s5_c_re, v_s5_c_im, v_s5_d, v_s5_w_glu, v_s5_b_glu, v_pool_w, v_pool_scale):
    given = dict(zip(_INPUTS, (x, c, ctx, c_ctx, w_mod, b_mod, g_pre, g_post, w_in, w_out, gla_w_gate, gla_b_gate, gla_g_norm, na_rpb, s5_lam_re, s5_lam_im, s5_log_dt, s5_b_re, s5_b_im, s5_c_re, s5_c_im, s5_d, s5_w_glu, s5_b_glu, pool_w, pool_scale, loss_target, m_c_ctx, m_w_mod, m_b_mod, m_g_pre, m_g_post, m_w_in, m_w_out, m_gla_w_gate, m_gla_b_gate, m_gla_g_norm, m_na_rpb, m_s5_lam_re, m_s5_lam_im, m_s5_log_dt, m_s5_b_re, m_s5_b_im, m_s5_c_re, m_s5_c_im, m_s5_d, m_s5_w_glu, m_s5_b_glu, m_pool_w, m_pool_scale, v_c_ctx, v_w_mod, v_b_mod, v_g_pre, v_g_post, v_w_in, v_w_out, v_gla_w_gate, v_gla_b_gate, v_gla_g_norm, v_na_rpb, v_s5_lam_re, v_s5_lam_im, v_s5_log_dt, v_s5_b_re, v_s5_b_im, v_s5_c_re, v_s5_c_im, v_s5_d, v_s5_w_glu, v_s5_b_glu, v_pool_w, v_pool_scale)))
    n_layers = w_in.shape[0]
    shards = [[given[n][l].astype(BF16) for n in _SHARDED] for l in range(n_layers)]
    p = {n: given[n] for n in _SMALL}
    loss, grad_x, _, _, received = local_step(x[0], c, ctx[0], loss_target[0], p, shards)
    outs = {n: [[None] * n_layers for _ in range(4)] for n in _WEIGHTS}
    for l in range(n_layers):
        for n, r in zip(_SHARDED, received[l]):
            res = adamw(r, given[n][l], given["m_" + n][l], given["v_" + n][l], f"adamw_{n}_l{l}")
            for kind in range(4):
                outs[n][kind][l] = res[kind]
        names = (["c_ctx"] if l == 0 else []) + _SMALL_PER_LAYER
        pick = lambda pre: [given[pre + n] if n == "c_ctx" else given[pre + n][l] for n in names]
        like = pick("")
        res = adamw(received[l][-1], _pack(like), _pack(pick("m_")), _pack(pick("v_")), f"adamw_small_l{l}")
        for kind, packed in enumerate(res):
            for n, a in zip(names, _unpack(packed, like)):
                if n == "c_ctx":
                    outs[n][kind] = a
                else:
                    outs[n][kind][l] = a
    final = {n: [v if n == "c_ctx" else jnp.stack(v) for v in outs[n]] for n in _WEIGHTS}
    loss = lax.psum(loss, ("x", "y", "c"))
    return (loss, grad_x[None], *[final[n][0] for n in _WEIGHTS], *[final[n][1] for n in _WEIGHTS],
            *[final[n][2] for n in _WEIGHTS], *[final[n][3] for n in _WEIGHTS])
```

```python
import functools
import math

import numpy as np
import jax
import jax.numpy as jnp
from jax import lax
from jax.experimental import pallas as pl
from jax.experimental.pallas import tpu as pltpu

F32 = jnp.float32
BF16 = jnp.bfloat16
HIGHEST = lax.Precision.HIGHEST

D = 1024
GRID_W = 64
EPS = 1e-6
N_DEV = 8
C_GT, C_GV, C_NK, C_NV, C_SU, C_NQ, C_PU, C_GK, C_GG, C_GQ, C_END = 0, 1024, 1280, 1536, 1792, 2048, 2304, 2560, 2688, 2816, 2944
PW = 3072
N_CTX_ORIG = 416
N_IN = 2848
GLA_CHUNK = 64
S5_CHUNK = 256
ROW_TILE = 256
VMEM_LIMIT = 56 * 1024 * 1024

ADAM_LR, ADAM_B1, ADAM_B2, ADAM_EPS, ADAM_WD, ADAM_STEP = 0.001, 0.9, 0.999, 1e-08, 0.01, 10


def _cparams(**kw):
    return pltpu.CompilerParams(vmem_limit_bytes=VMEM_LIMIT, **kw)


def _dg(a, b, ca, cb, precision=None):
    return lax.dot_general(a, b, (((ca,), (cb,)), ((), ())), precision=precision, preferred_element_type=F32)


def hdot(a, b):
    return _dg(a, b, 1, 0, HIGHEST)


def hdot_nt(a, b):
    return _dg(a, b, 1, 1, HIGHEST)


def hdot_tn(a, b):
    return _dg(a, b, 0, 0, HIGHEST)


def b_nn(a, b):
    return _dg(a.astype(BF16), b.astype(BF16), 1, 0)


def b_nt(a, b):
    return _dg(a.astype(BF16), b.astype(BF16), 1, 1)


def b_tn(a, b):
    return _dg(a.astype(BF16), b.astype(BF16), 0, 0)


@jax.custom_vjp
def bdot(a, b):
    return b_nn(a, b)


def _bdot_fwd(a, b):
    return b_nn(a, b), (a, b)


def _bdot_bwd(res, ct):
    a, b = res
    return b_nt(ct, b).astype(a.dtype), b_tn(a, ct).astype(b.dtype)


bdot.defvjp(_bdot_fwd, _bdot_bwd)


def _log_sigmoid(z):
    return jnp.minimum(z, 0.0) - jnp.log(1.0 + jnp.exp(-jnp.abs(z)))


def _silu(z):
    return z * jax.nn.sigmoid(z)


def _gelu(z):
    return 0.5 * z * (1.0 + jnp.tanh(math.sqrt(2.0 / math.pi) * (z + 0.044715 * (z * z * z))))


def _cat(vals):
    return vals[0] if len(vals) == 1 else jnp.concatenate(vals, axis=-1)


def mm_nn(a_parts, b, name, tm=ROW_TILE, tn=1024):
    t = a_parts[0].shape[0]
    k, n = b.shape
    na = len(a_parts)
    tn = min(tn, n)

    def body(*refs):
        a = _cat([r[...].astype(BF16) for r in refs[:na]])
        refs[na + 1][...] = _dg(a, refs[na][...].astype(BF16), 1, 0)

    return pl.pallas_call(
        body, name=name, grid=(n // tn, t // tm),
        in_specs=[pl.BlockSpec((tm, p.shape[1]), lambda j, i: (i, 0)) for p in a_parts]
        + [pl.BlockSpec((k, tn), lambda j, i: (0, j))],
        out_specs=pl.BlockSpec((tm, tn), lambda j, i: (i, j)),
        out_shape=jax.ShapeDtypeStruct((t, n), F32),
        compiler_params=_cparams(dimension_semantics=("arbitrary", "arbitrary")),
    )(*a_parts, b)


def mm_nn_cols(a, b, start, widths, name, tm=ROW_TILE):
    t, k = a.shape
    tn = 1024
    assert start % tn == 0 and sum(widths) <= tn

    def body(a_ref, b_ref, *o_refs):
        r = _dg(a_ref[...].astype(BF16), b_ref[...].astype(BF16), 1, 0)
        off = 0
        for o_ref, w in zip(o_refs, widths):
            o_ref[...] = r[:, off:off + w]
            off += w

    return pl.pallas_call(
        body, name=name, grid=(t // tm,),
        in_specs=[pl.BlockSpec((tm, k), lambda i: (i, 0)), pl.BlockSpec((k, tn), lambda i: (0, start // tn))],
        out_specs=[pl.BlockSpec((tm, w), lambda i: (i, 0)) for w in widths],
        out_shape=[jax.ShapeDtypeStruct((t, w), F32) for w in widths],
        compiler_params=_cparams(dimension_semantics=("arbitrary",)),
    )(a, b)


def mm_nt(a_parts, b, name, tm=ROW_TILE):
    t = a_parts[0].shape[0]
    n, k = b.shape
    na = len(a_parts)

    def body(*refs):
        a = _cat([r[...].astype(BF16) for r in refs[:na]])
        refs[na + 1][...] = _dg(a, refs[na][...].astype(BF16), 1, 1)

    return pl.pallas_call(
        body, name=name, grid=(t // tm,),
        in_specs=[pl.BlockSpec((tm, p.shape[1]), lambda i: (i, 0)) for p in a_parts]
        + [pl.BlockSpec((n, k), lambda i: (0, 0))],
        out_specs=pl.BlockSpec((tm, n), lambda i: (i, 0)),
        out_shape=jax.ShapeDtypeStruct((t, n), F32),
        compiler_params=_cparams(dimension_semantics=("arbitrary",)),
    )(*a_parts, b)


def mm_tn(a, b_parts, name, tm=ROW_TILE, tn=1024, out_dtype=F32):
    t, k = a.shape
    widths = [p.shape[1] for p in b_parts]
    n = sum(widths)
    assert n % tn == 0
    groups, cur, acc = [], [], 0
    for idx, w in enumerate(widths):
        cur.append(idx)
        acc += w
        if acc == tn:
            groups.append(cur)
            cur, acc = [], 0
        assert acc < tn
    assert not cur
    outs = []
    for gi, grp in enumerate(groups):
        parts = [b_parts[i] for i in grp]
        npart = len(parts)
        nsteps = t // tm

        def body(*refs, npart=npart, nsteps=nsteps):
            a_v = refs[0][...].astype(BF16)
            b_v = _cat([r[...].astype(BF16) for r in refs[1:1 + npart]])
            o_ref, acc_ref = refs[1 + npart], refs[2 + npart]
            r = _dg(a_v, b_v, 0, 0)

            @pl.when(pl.program_id(0) == 0)
            def _():
                acc_ref[...] = r

            @pl.when(pl.program_id(0) != 0)
            def _():
                acc_ref[...] += r

            @pl.when(pl.program_id(0) == nsteps - 1)
            def _():
                o_ref[...] = acc_ref[...].astype(o_ref.dtype)

        outs.append(pl.pallas_call(
            body, name=f"{name}_{gi}", grid=(nsteps,),
            in_specs=[pl.BlockSpec((tm, k), lambda i: (i, 0))]
            + [pl.BlockSpec((tm, p.shape[1]), lambda i: (i, 0)) for p in parts],
            out_specs=pl.BlockSpec((k, tn), lambda i: (0, 0)),
            out_shape=jax.ShapeDtypeStruct((k, tn), out_dtype),
            scratch_shapes=[pltpu.VMEM((k, tn), F32)],
            compiler_params=_cparams(dimension_semantics=("arbitrary",)),
        )(a, *parts))
    return outs


def _seg_of(i, nct):
    return jnp.where(i < nct, 1, 0)


def rowwise_fwd(fn, name, rows, segs, globs, out_widths, tile, nct):
    t = rows[0].shape[0]
    nr, ns, ng = len(rows), len(segs), len(globs)

    def body(*refs):
        vals = [r[...] for r in refs[:nr]] + [r[0] for r in refs[nr:nr + ns]] + [r[...] for r in refs[nr + ns:nr + ns + ng]]
        outs = fn(*vals)
        for o_ref, o in zip(refs[nr + ns + ng:], outs):
            o_ref[...] = o

    return pl.pallas_call(
        body, name=name, grid=(t // tile,),
        in_specs=[pl.BlockSpec((tile, r.shape[1]), lambda i: (i, 0)) for r in rows]
        + [pl.BlockSpec((1, 1, s.shape[2]), lambda i: (_seg_of(i, nct), 0, 0)) for s in segs]
        + [pl.BlockSpec(g.shape, lambda i: (0, 0)) for g in globs],
        out_specs=[pl.BlockSpec((tile, w), lambda i: (i, 0)) for w in out_widths],
        out_shape=[jax.ShapeDtypeStruct((t, w), F32) for w in out_widths],
        compiler_params=_cparams(dimension_semantics=("arbitrary",)),
    )(*rows, *segs, *globs)


def rowwise_bwd(fn, name, rows, segs, globs, cts, tile, nct, row_diff, glob_diff):
    t = rows[0].shape[0]
    nr, ns, ng, nc = len(rows), len(segs), len(globs), len(cts)
    d_rows = [i for i in range(nr) if row_diff[i]]
    d_globs = [i for i in range(ng) if glob_diff[i]]

    def body(*refs):
        in_refs, out_refs = refs[:nr + ns + ng + nc], refs[nr + ns + ng + nc:]
        row_v = [r[...] for r in in_refs[:nr]]
        seg_v = [r[0] for r in in_refs[nr:nr + ns]]
        glob_v = [r[...] for r in in_refs[nr + ns:nr + ns + ng]]
        ct_v = tuple(r[...] for r in in_refs[nr + ns + ng:])

        def wrapped(dr, sv, dg):
            rv = list(row_v)
            for j, i in enumerate(d_rows):
                rv[i] = dr[j]
            gv = list(glob_v)
            for j, i in enumerate(d_globs):
                gv[i] = dg[j]
            return tuple(fn(*rv, *sv, *gv))

        _, vjp = jax.vjp(wrapped, [row_v[i] for i in d_rows], seg_v, [glob_v[i] for i in d_globs])
        c_rows, c_segs, c_globs = vjp(ct_v)
        i = pl.program_id(0)
        k = 0
        for c in c_rows:
            out_refs[k][...] = c
            k += 1
        seg_first = jnp.logical_or(i == 0, i == nct)
        for c in c_segs:
            ref = out_refs[k]
            k += 1

            @pl.when(seg_first)
            def _(ref=ref, c=c):
                ref[0] = c

            @pl.when(jnp.logical_not(seg_first))
            def _(ref=ref, c=c):
                ref[0] += c
        for c in c_globs:
            ref = out_refs[k]
            k += 1

            @pl.when(i == 0)
            def _(ref=ref, c=c):
                ref[...] = c

            @pl.when(i != 0)
            def _(ref=ref, c=c):
                ref[...] += c

    return pl.pallas_call(
        body, name=name, grid=(t // tile,),
        in_specs=[pl.BlockSpec((tile, r.shape[1]), lambda i: (i, 0)) for r in rows]
        + [pl.BlockSpec((1, 1, s.shape[2]), lambda i: (_seg_of(i, nct), 0, 0)) for s in segs]
        + [pl.BlockSpec(g.shape, lambda i: (0, 0)) for g in globs]
        + [pl.BlockSpec((tile, c.shape[1]), lambda i: (i, 0)) for c in cts],
        out_specs=[pl.BlockSpec((tile, rows[i].shape[1]), lambda i: (i, 0)) for i in d_rows]
        + [pl.BlockSpec((1, 1, s.shape[2]), lambda i: (_seg_of(i, nct), 0, 0)) for s in segs]
        + [pl.BlockSpec(globs[i].shape, lambda i: (0, 0)) for i in d_globs],
        out_shape=[jax.ShapeDtypeStruct(rows[i].shape, F32) for i in d_rows]
        + [jax.ShapeDtypeStruct(s.shape, F32) for s in segs]
        + [jax.ShapeDtypeStruct(globs[i].shape, F32) for i in d_globs],
        compiler_params=_cparams(dimension_semantics=("arbitrary",)),
    )(*rows, *segs, *globs, *cts)


def f_pre(x, mod, g_pre):
    shift, scale = mod[:, :D], mod[:, D:2 * D]
    rs = lax.rsqrt(jnp.mean(x * x, axis=-1, keepdims=True) + EPS)
    return ((x * rs) * g_pre * (1.0 + scale) + shift,)


def f_post(x, out, mod, g_post):
    gate = mod[:, 2 * D:]
    rs = lax.rsqrt(jnp.mean(out * out, axis=-1, keepdims=True) + EPS)
    return (x + gate * ((out * rs) * g_post),)


def f_mix(o_gla, o_na, y5, u5, pm, gcols, g_norm, s5_d, w_glu, b_glu, wpool, pool_scale, havg, e4):
    ms = hdot(o_gla * o_gla, havg)
    y_gla = o_gla * lax.rsqrt(ms + EPS) * jnp.sum(hdot(g_norm, e4), axis=0, keepdims=True)
    g = _gelu(u5 * s5_d + y5)
    y_s5 = g * jax.nn.sigmoid(bdot(g, w_glu) + b_glu)
    y_pool = bdot(pm, wpool) * pool_scale
    ycat = jnp.concatenate([y_gla, o_na, y_s5, y_pool], axis=-1)
    return (ycat * _silu(gcols),)


@jax.custom_vjp
def _rot_half16(x):
    lane = lax.broadcasted_iota(jnp.int32, x.shape, 1)
    first = jnp.bitwise_and(lane, 15) < 8
    return jnp.where(first, -pltpu.roll(x, x.shape[1] - 8, 1), pltpu.roll(x, 8, 1))


def _rot_fwd(x):
    return _rot_half16(x), None


def _rot_bwd(_, ct):
    return (-_rot_half16(ct),)


_rot_half16.defvjp(_rot_fwd, _rot_bwd)


def f_gla_prep(pk, pg, pq, cos, sin, wg, bg):
    z = bdot(pg, wg) + bg
    lg = _log_sigmoid(z) * (1.0 / 16.0)
    k_r = pk * cos + _rot_half16(pk) * sin
    q_r = (pq * cos + _rot_half16(pq) * sin) * (32.0 ** -0.5)
    return q_r, k_r, lg[:, :128], lg[:, 128:]


def _gla_consts(rev):
    c = GLA_CHUNK
    i = np.arange(c)
    inc = (i[None, :] >= i[:, None]) if rev else (i[None, :] <= i[:, None])
    mq = np.stack([(np.arange(128) // 32 == h) for h in range(4)]).astype(np.float32).reshape(4, 1, 128)
    mv = np.stack([(np.arange(256) // 64 == h) for h in range(4)]).astype(np.float32).reshape(4, 1, 256)
    bdt = (np.arange(256)[:, None] // 64 == np.arange(128)[None, :] // 32).astype(np.float32)
    return jnp.asarray(inc.astype(np.float32)), jnp.asarray(mq), jnp.asarray(mv), jnp.asarray(bdt)


def _gla_chunk_of(s, n_ctx_chunks, n_chunks, rev):
    if not rev:
        return s
    return jnp.where(s < n_ctx_chunks, n_ctx_chunks - 1 - s, n_ctx_chunks + n_chunks - 1 - s)


def gla_scan_fwd(q, k, v, lg, acc, n_ctx_rows, rev, name):
    t = q.shape[0]
    nch, ncc = t // GLA_CHUNK, n_ctx_rows // GLA_CHUNK
    inc, mq, mv, bdt = _gla_consts(rev)

    def body(q_ref, k_ref, v_ref, lg_ref, acc_ref, inc_ref, mq_ref, mv_ref, bdt_ref, o_ref, st_ref):
        lmask = inc_ref[...]
        bd = bdt_ref[...]

        def step(s, st):
            c = _gla_chunk_of(s, ncc, nch, rev)
            rows = pl.ds(pl.multiple_of(c * GLA_CHUNK, GLA_CHUNK), GLA_CHUNK)
            qc, kc, vc, lgc = q_ref[rows, :], k_ref[rows, :], v_ref[rows, :], lg_ref[rows, :]
            st_ref[c] = st
            b = hdot(lmask, lgc)
            blast = jnp.sum(lgc, axis=0, keepdims=True)
            qe, ke, kd = qc * jnp.exp(b), kc * jnp.exp(-b), kc * jnp.exp(blast - b)
            o = acc_ref[rows, :] + b_nt(qe, st)
            for h in range(4):
                a = lmask * b_nt(qe * mq_ref[h], ke)
                o = o + b_nn(a, vc * mv_ref[h])
            o_ref[rows, :] = o
            return st * jnp.exp(blast) + bd * hdot_tn(vc, kd)

        lax.fori_loop(0, nch, step, jnp.zeros((256, 128), F32))

    vm = pl.BlockSpec(memory_space=pltpu.VMEM)
    return pl.pallas_call(
        body, name=name, in_specs=[vm] * 9, out_specs=[vm, vm],
        out_shape=[jax.ShapeDtypeStruct((t, 256), F32), jax.ShapeDtypeStruct((nch, 256, 128), F32)],
        compiler_params=_cparams(),
    )(q, k, v, lg, acc, inc, mq, mv, bdt)


def gla_scan_bwd(q, k, v, lg, st, do, acc, n_ctx_rows, rev, name):
    t = q.shape[0]
    nch, ncc = t // GLA_CHUNK, n_ctx_rows // GLA_CHUNK
    inc, mq, mv, bdt = _gla_consts(rev)

    def body(q_ref, k_ref, v_ref, lg_ref, st_ref, do_ref, aq_ref, ak_ref, av_ref, inc_ref, mq_ref, mv_ref, bdt_ref,
             dq_ref, dk_ref, dv_ref, dlg_ref):
        lmask = inc_ref[...]
        bd = bdt_ref[...]

        def step(j, carry):
            dst, gsum = carry
            s = nch - 1 - j
            c = _gla_chunk_of(s, ncc, nch, rev)
            rows = pl.ds(pl.multiple_of(c * GLA_CHUNK, GLA_CHUNK), GLA_CHUNK)
            qc, kc, vc, lgc, doc = q_ref[rows, :], k_ref[rows, :], v_ref[rows, :], lg_ref[rows, :], do_ref[rows, :]
            stc = st_ref[c]
            b = hdot(lmask, lgc)
            blast = jnp.sum(lgc, axis=0, keepdims=True)
            eb, enb, edb = jnp.exp(b), jnp.exp(-b), jnp.exp(blast - b)
            qe, ke, kd = qc * eb, kc * enb, kc * edb
            dqe = hdot(doc, stc)
            dke = jnp.zeros_like(ke)
            dv = b_nt(kd, dst)
            for h in range(4):
                qh = qe * mq_ref[h]
                a = lmask * b_nt(qh, ke)
                doh = doc * mv_ref[h]
                da = lmask * hdot_nt(doh, vc)
                dqe = dqe + mq_ref[h] * hdot(da, ke)
                dke = dke + mq_ref[h] * hdot_tn(da, qe)
                dv = dv + mv_ref[h] * b_tn(a, doc)
            dkd = hdot(vc, dst)
            dq = dqe * eb
            dk = dke * enb + dkd * edb
            g = qc * dq - kc * dk
            dlg_ref[rows, :] = hdot_tn(lmask, g) + gsum
            dq_ref[rows, :] = aq_ref[rows, :] + dq
            dk_ref[rows, :] = ak_ref[rows, :] + dk
            dv_ref[rows, :] = av_ref[rows, :] + dv
            dst_new = dst * jnp.exp(blast) + bd * hdot_tn(doc, qe)
            return dst_new, gsum + jnp.sum(g, axis=0, keepdims=True)

        lax.fori_loop(0, nch, step, (jnp.zeros((256, 128), F32), jnp.zeros((1, 128), F32)))

    vm = pl.BlockSpec(memory_space=pltpu.VMEM)
    return pl.pallas_call(
        body, name=name, in_specs=[vm] * 13, out_specs=[vm] * 4,
        out_shape=[jax.ShapeDtypeStruct((t, 128), F32), jax.ShapeDtypeStruct((t, 128), F32),
                   jax.ShapeDtypeStruct((t, 256), F32), jax.ShapeDtypeStruct((t, 128), F32)],
        compiler_params=_cparams(),
    )(q, k, v, lg, st, do, *acc, inc, mq, mv, bdt)


def whole_fwd(fn, name, args, out_shapes):
    def body(*refs):
        outs = fn(*[r[...] for r in refs[:len(args)]])
        for o_ref, o in zip(refs[len(args):], outs):
            o_ref[...] = o

    vm = pl.BlockSpec(memory_space=pltpu.VMEM)
    return pl.pallas_call(
        body, name=name, in_specs=[vm] * len(args), out_specs=[vm] * len(out_shapes),
        out_shape=[jax.ShapeDtypeStruct(s, F32) for s in out_shapes], compiler_params=_cparams(),
    )(*args)


def whole_bwd(fn, name, args, cts, diff):
    d_idx = [i for i in range(len(args)) if diff[i]]

    def body(*refs):
        vals = [r[...] for r in refs[:len(args)]]
        ct_v = tuple(r[...] for r in refs[len(args):len(args) + len(cts)])

        def wrapped(dv):
            av = list(vals)
            for j, i in enumerate(d_idx):
                av[i] = dv[j]
            return tuple(fn(*av))

        _, vjp = jax.vjp(wrapped, [vals[i] for i in d_idx])
        (c_args,) = vjp(ct_v)
        for o_ref, c in zip(refs[len(args) + len(cts):], c_args):
            o_ref[...] = c

    vm = pl.BlockSpec(memory_space=pltpu.VMEM)
    return pl.pallas_call(
        body, name=name, in_specs=[vm] * (len(args) + len(cts)), out_specs=[vm] * len(d_idx),
        out_shape=[jax.ShapeDtypeStruct(args[i].shape, F32) for i in d_idx], compiler_params=_cparams(),
    )(*args, *cts)


def _s5_consts():
    e_rep = (np.arange(256)[:, None] // 16 == np.arange(16)[None, :]).astype(np.float32)
    e_tile = (np.arange(64)[:, None] == np.arange(1024)[None, :] % 64).astype(np.float32)
    gmask = (np.arange(16)[:, None] == np.arange(1024)[None, :] // 64).astype(np.float32)
    bdm = (np.arange(256)[:, None] // 16 == np.arange(1024)[None, :] // 64).astype(np.float32)
    return jnp.asarray(e_rep), jnp.asarray(e_tile), jnp.asarray(gmask), jnp.asarray(bdm)


def f_s5_params(lam_re, lam_im, log_dt, bt_re, bt_im, ct_re, ct_im, e_rep, e_tile, gmask, bdm):
    dt = jnp.exp(log_dt)
    mag = jnp.exp(lam_re * dt)
    ang = lam_im * dt
    lb_re, lb_im = mag * jnp.cos(ang), mag * jnp.sin(ang)
    num_re, num_im = lb_re - 1.0, lb_im
    den = lam_re * lam_re + lam_im * lam_im
    coef_re = (num_re * lam_re + num_im * lam_im) / den
    coef_im = (num_im * lam_re - num_re * lam_im) / den
    cr, ci = hdot(e_rep, coef_re), hdot(e_rep, coef_im)
    bbt_re = cr * bt_re - ci * bt_im
    bbt_im = cr * bt_im + ci * bt_re
    a_re = jnp.sum(hdot(lb_re, e_tile) * gmask, axis=0, keepdims=True)
    a_im = jnp.sum(hdot(lb_im, e_tile) * gmask, axis=0, keepdims=True)
    return (a_re, a_im, hdot(bbt_re, e_tile) * bdm, hdot(bbt_im, e_tile) * bdm,
            hdot(ct_re, e_tile) * bdm, hdot(ct_im, e_tile) * bdm)


def _s5_doubling(xr, xi, pr, pi, pos, n, steps, rev):
    rows = xr.shape[0]
    for s in steps:
        if rev:
            keep = pos < (n - s)
            sr, si = pltpu.roll(xr, rows - s, 0), pltpu.roll(xi, rows - s, 0)
        else:
            keep = pos >= s
            sr, si = pltpu.roll(xr, s, 0), pltpu.roll(xi, s, 0)
        sr, si = jnp.where(keep, sr, 0.0), jnp.where(keep, si, 0.0)
        xr, xi = xr + pr * sr - pi * si, xi + pr * si + pi * sr
        pr, pi = pr * pr - pi * pi, 2.0 * pr * pi
    return xr, xi, pr, pi


SUBLANES = 8


def _s5_scan(xr, xi, a_re, a_im, rev, chunk, scr):
    xs_r, xs_i, yp_r, yp_i = scr
    ng = chunk // SUBLANES
    x3r, x3i = xr.reshape(ng, SUBLANES, 1024), xi.reshape(ng, SUBLANES, 1024)
    sub = lax.broadcasted_iota(jnp.int32, (SUBLANES, 1024), 0)
    a8r, a8i = a_re, a_im
    for s in (1, 2, 4):
        keep = sub < (SUBLANES - s) if rev else sub >= s
        mr, mi = jnp.where(keep, a8r, 0.0)[None], jnp.where(keep, a8i, 0.0)[None]
        shift = SUBLANES - s if rev else s
        sr, si = pltpu.roll(x3r, shift, 1), pltpu.roll(x3i, shift, 1)
        x3r, x3i = x3r + mr * sr - mi * si, x3i + mr * si + mi * sr
        a8r, a8i = a8r * a8r - a8i * a8i, 2.0 * a8r * a8i
    xr, xi = x3r.reshape(chunk, 1024), x3i.reshape(chunk, 1024)
    nblk = 1024 // 128
    for j in range(nblk):
        xs_r[j] = xr[:, 128 * j:128 * (j + 1)]
        xs_i[j] = xi[:, 128 * j:128 * (j + 1)]
    edge = pl.ds(0 if rev else SUBLANES - 1, ng, stride=SUBLANES)
    gr = jnp.concatenate([xs_r[j, edge, :] for j in range(nblk)], axis=-1)
    gi = jnp.concatenate([xs_i[j, edge, :] for j in range(nblk)], axis=-1)
    grow = lax.broadcasted_iota(jnp.int32, (ng, 1024), 0)
    steps = tuple(1 << k for k in range((ng - 1).bit_length()))
    gr, gi, _, _ = _s5_doubling(gr, gi, a8r, a8i, grow, ng, steps, rev)
    if rev:
        yp_r[...] = jnp.where(grow < ng - 1, pltpu.roll(gr, ng - 1, 0), 0.0)
        yp_i[...] = jnp.where(grow < ng - 1, pltpu.roll(gi, ng - 1, 0), 0.0)
    else:
        yp_r[...] = jnp.where(grow >= 1, pltpu.roll(gr, 1, 0), 0.0)
        yp_i[...] = jnp.where(grow >= 1, pltpu.roll(gi, 1, 0), 0.0)
    sub = lax.broadcasted_iota(jnp.int32, (SUBLANES, 1024), 0)
    tr, ti = jnp.zeros((SUBLANES, 1024), F32), jnp.zeros((SUBLANES, 1024), F32)
    cr, ci = a_re, a_im
    for n in range(1, SUBLANES + 1):
        r = SUBLANES - n if rev else n - 1
        tr, ti = jnp.where(sub == r, cr, tr), jnp.where(sub == r, ci, ti)
        cr, ci = cr * a_re - ci * a_im, cr * a_im + ci * a_re
    for j in range(nblk):
        lanes = slice(128 * j, 128 * (j + 1))
        tr_j, ti_j = tr[:, lanes], ti[:, lanes]
        for g in range(ng):
            rows = slice(g * SUBLANES, (g + 1) * SUBLANES)
            er, ei = yp_r[g:g + 1, lanes], yp_i[g:g + 1, lanes]
            xs_r[j, rows, :] = xs_r[j, rows, :] + tr_j * er - ti_j * ei
            xs_i[j, rows, :] = xs_i[j, rows, :] + tr_j * ei + ti_j * er
    return (jnp.concatenate([xs_r[j] for j in range(nblk)], axis=-1),
            jnp.concatenate([xs_i[j] for j in range(nblk)], axis=-1))


def _s5_scratch(chunk):
    return [pltpu.VMEM((8, chunk, 128), F32), pltpu.VMEM((8, chunk, 128), F32),
            pltpu.VMEM((chunk // SUBLANES, 1024), F32), pltpu.VMEM((chunk // SUBLANES, 1024), F32)]


def _s5_chunk_states(u_c, x0r, x0i, a_re, a_im, bb_re, bb_im, rev, chunk, scr):
    row = lax.broadcasted_iota(jnp.int32, (chunk, 1024), 0)
    first = row == (chunk - 1 if rev else 0)
    inj_r = a_re * x0r - a_im * x0i
    inj_i = a_re * x0i + a_im * x0r
    xr = b_nn(u_c, bb_re) + jnp.where(first, inj_r, 0.0)
    xi = b_nn(u_c, bb_im) + jnp.where(first, inj_i, 0.0)
    return _s5_scan(xr, xi, a_re, a_im, rev, chunk, scr)


def _row_pick(x, idx):
    row = lax.broadcasted_iota(jnp.int32, x.shape, 0)
    return jnp.sum(jnp.where(row == idx, x, 0.0), axis=0, keepdims=True)


def s5_scan_fwd(u, acc, a_re, a_im, bb_re, bb_im, cc_re, cc_im, n_ctx_rows, chunk, rev, name):
    t = u.shape[0]
    nch, ncc = t // chunk, n_ctx_rows // chunk

    def body(u_ref, acc_ref, ar_ref, ai_ref, br_ref, bi_ref, cr_ref, ci_ref, y_ref, x0r_ref, x0i_ref, *scr):
        a_r, a_i = ar_ref[...], ai_ref[...]

        def step(s, carry):
            x0r, x0i = carry
            c = _gla_chunk_of(s, ncc, nch, rev)
            rows = pl.ds(pl.multiple_of(c * chunk, chunk), chunk)
            x0r_ref[c] = x0r
            x0i_ref[c] = x0i
            xr, xi = _s5_chunk_states(u_ref[rows, :], x0r, x0i, a_r, a_i, br_ref[...], bi_ref[...], rev, chunk, scr)
            y_ref[rows, :] = acc_ref[rows, :] + b_nt(xr, cr_ref[...]) - b_nt(xi, ci_ref[...])
            last = 0 if rev else chunk - 1
            return _row_pick(xr, last), _row_pick(xi, last)

        lax.fori_loop(0, nch, step, (jnp.zeros((1, 1024), F32), jnp.zeros((1, 1024), F32)))

    vm = pl.BlockSpec(memory_space=pltpu.VMEM)
    return pl.pallas_call(
        body, name=name, in_specs=[vm] * 8, out_specs=[vm] * 3,
        out_shape=[jax.ShapeDtypeStruct((t, 256), F32), jax.ShapeDtypeStruct((nch, 1, 1024), F32),
                   jax.ShapeDtypeStruct((nch, 1, 1024), F32)],
        scratch_shapes=_s5_scratch(chunk), compiler_params=_cparams(),
    )(u, acc, a_re, a_im, bb_re, bb_im, cc_re, cc_im)


def s5_scan_bwd(u, dy, du_acc, x0r, x0i, a_re, a_im, bb_re, bb_im, cc_re, cc_im, n_ctx_rows, chunk, rev, name):
    t = u.shape[0]
    nch, ncc = t // chunk, n_ctx_rows // chunk

    def body(u_ref, dy_ref, dua_ref, x0r_ref, x0i_ref, ar_ref, ai_ref, br_ref, bi_ref, cr_ref, ci_ref,
             du_ref, dar_ref, dai_ref, dbr_ref, dbi_ref, dcr_ref, dci_ref, *scr):
        a_r, a_i = ar_ref[...], ai_ref[...]
        for ref in (dbr_ref, dbi_ref, dcr_ref, dci_ref):
            ref[...] = jnp.zeros_like(ref)
        row = lax.broadcasted_iota(jnp.int32, (chunk, 1024), 0)
        first_idx, last_idx = (chunk - 1, 0) if rev else (0, chunk - 1)

        def step(j, carry):
            lcr, lci, dar, dai = carry
            s = nch - 1 - j
            c = _gla_chunk_of(s, ncc, nch, rev)
            rows = pl.ds(pl.multiple_of(c * chunk, chunk), chunk)
            u_c, dy_c = u_ref[rows, :], dy_ref[rows, :]
            x0r_c, x0i_c = x0r_ref[c], x0i_ref[c]
            xr, xi = _s5_chunk_states(u_c, x0r_c, x0i_c, a_r, a_i, br_ref[...], bi_ref[...], rev, chunk, scr[:4])
            dcr_ref[...] += b_tn(dy_c, xr)
            dci_ref[...] -= b_tn(dy_c, xi)
            inj_r = a_r * lcr + a_i * lci
            inj_i = a_r * lci - a_i * lcr
            is_last = row == last_idx
            lr = b_nn(dy_c, cr_ref[...]) + jnp.where(is_last, inj_r, 0.0)
            li = -b_nn(dy_c, ci_ref[...]) + jnp.where(is_last, inj_i, 0.0)
            lr, li = _s5_scan(lr, li, a_r, -a_i, not rev, chunk, scr[4:])
            du_ref[rows, :] = dua_ref[rows, :] + b_nt(lr, br_ref[...]) + b_nt(li, bi_ref[...])
            dbr_ref[...] += b_tn(u_c, lr)
            dbi_ref[...] += b_tn(u_c, li)
            if rev:
                pr, pi = pltpu.roll(xr, chunk - 1, 0), pltpu.roll(xi, chunk - 1, 0)
            else:
                pr, pi = pltpu.roll(xr, 1, 0), pltpu.roll(xi, 1, 0)
            is_first = row == first_idx
            pr, pi = jnp.where(is_first, x0r_c, pr), jnp.where(is_first, x0i_c, pi)
            dar = dar + jnp.sum(lr * pr + li * pi, axis=0, keepdims=True)
            dai = dai + jnp.sum(li * pr - lr * pi, axis=0, keepdims=True)
            return _row_pick(lr, first_idx), _row_pick(li, first_idx), dar, dai

        z = jnp.zeros((1, 1024), F32)
        _, _, dar, dai = lax.fori_loop(0, nch, step, (z, z, z, z))
        dar_ref[...] = dar
        dai_ref[...] = dai

    vm = pl.BlockSpec(memory_space=pltpu.VMEM)
    big = jax.ShapeDtypeStruct((256, 1024), F32)
    vec = jax.ShapeDtypeStruct((1, 1024), F32)
    return pl.pallas_call(
        body, name=name, in_specs=[vm] * 11, out_specs=[vm] * 7,
        out_shape=[jax.ShapeDtypeStruct((t, 256), F32), vec, vec, big, big, big, big],
        scratch_shapes=_s5_scratch(chunk) + _s5_scratch(chunk), compiler_params=_cparams(),
    )(u, dy, du_acc, x0r, x0i, a_re, a_im, bb_re, bb_im, cc_re, cc_im)


POOL_HALO = 8


def pool_apply(u_pad, n, transpose, name, tile=ROW_TILE):
    tile = min(tile, n)
    ext = tile + 2 * POOL_HALO

    def body(u_ref, o_ref):
        lax.fori_loop(0, n // tile, functools.partial(step, u_ref, o_ref), 0)

    def step(u_ref, o_ref, i, carry):
        val = u_ref[pl.ds(pl.multiple_of(i * tile, tile), ext), :]
        lane = lax.broadcasted_iota(jnp.int32, (ext, 256), 1)
        half = jnp.left_shift(1, jnp.right_shift(lane, 6))
        trow = lax.broadcasted_iota(jnp.int32, (ext, 256), 0) + (i * tile - POOL_HALO)
        cnt = jnp.minimum(trow + half, n) - jnp.maximum(trow - half, 0)
        inv = 1.0 / jnp.maximum(cnt, 1).astype(F32)
        src = val * inv if transpose else val
        acc = jnp.zeros((tile, 256), F32)
        for d in range(-POOL_HALO, POOL_HALO):
            in_win = jnp.logical_and(d >= -half, d <= half - 1)[POOL_HALO:POOL_HALO + tile]
            shift = d if transpose else -d
            rolled = pltpu.roll(src, shift % ext, 0)[POOL_HALO:POOL_HALO + tile]
            acc = acc + jnp.where(in_win, rolled, 0.0)
        centre = val[POOL_HALO:POOL_HALO + tile]
        if not transpose:
            acc = acc * inv[POOL_HALO:POOL_HALO + tile]
        o_ref[pl.ds(pl.multiple_of(i * tile, tile), tile), :] = acc - centre
        return carry

    vm = pl.BlockSpec(memory_space=pltpu.VMEM)
    return pl.pallas_call(
        body, name=name, in_specs=[vm], out_specs=vm,
        out_shape=jax.ShapeDtypeStruct((n, 256), F32), compiler_params=_cparams(),
    )(u_pad)


NA_SCALE = 64.0 ** -0.5
NEG = -1e30


def _call_with_exchange(compute, name, args, out_shapes, comm):
    vm = pl.BlockSpec(memory_space=pltpu.VMEM)
    n_in, n_out = len(args), len(out_shapes)
    if comm is None:
        outs = pl.pallas_call(compute, name=name, in_specs=[vm] * n_in, out_specs=[vm] * n_out, out_shape=out_shapes,
                              compiler_params=_cparams())(*args)
        return outs, None
    arrays, scatter = comm
    n = len(arrays)

    def body(*refs):
        c_in = refs[n_in:n_in + n]
        c_out = refs[n_in + n + n_out:n_in + 2 * n + n_out]
        finish = _exchange_issue(c_in, c_out, scatter, *refs[n_in + 2 * n + n_out:])
        compute(*refs[:n_in], *refs[n_in + n:n_in + n + n_out])
        finish()

    hbm = pl.BlockSpec(memory_space=pl.ANY)
    outs = pl.pallas_call(
        body, name=name, in_specs=[vm] * n_in + [hbm] * n, out_specs=[vm] * n_out + [hbm] * n,
        out_shape=list(out_shapes) + _exchange_out_shapes(arrays, scatter), scratch_shapes=_exchange_sems(n),
        compiler_params=_cparams(has_side_effects=True),
    )(*args, *arrays)
    return outs[:n_out], outs[n_out:]


def _na_head_masks():
    return jnp.asarray(np.stack([(np.arange(256) // 64 == h) for h in range(4)]).astype(np.float32).reshape(4, 1, 256))


def _na_window(r, rows):
    start = jnp.clip(r - 4, 0, rows - 8)
    return start, start - r + 7


def _na_probs(qh, kw, kc, bias):
    s_c = b_nt(qh, kc)
    m = jnp.max(s_c, axis=-1, keepdims=True)
    if kw is not None:
        s_w = b_nt(qh, kw) + bias
        m = jnp.maximum(m, jnp.max(s_w, axis=-1, keepdims=True))
        p_w = jnp.exp(s_w - m)
    p_c = jnp.exp(s_c - m)
    l = jnp.sum(p_c, axis=-1, keepdims=True)
    if kw is not None:
        l = l + jnp.sum(p_w, axis=-1, keepdims=True)
        return p_w / l, p_c / l
    return None, p_c / l


def na_fwd(q, k, v, bias8, n_ctx_rows, name, comm=None):
    t = q.shape[0]
    m_ctx = n_ctx_rows
    rows = (t - m_ctx) // GRID_W
    hm = _na_head_masks()

    def body(q_ref, k_ref, v_ref, b_ref, hm_ref, o_ref):
        kc, vc = k_ref[0:m_ctx, :], v_ref[0:m_ctx, :]

        def ctx_step(i, _):
            rs = pl.ds(pl.multiple_of(i * 64, 64), 64)
            qr = q_ref[rs, :] * NA_SCALE
            o = jnp.zeros((64, 256), F32)
            for h in range(4):
                _, p_c = _na_probs(qr * hm_ref[h], None, kc, None)
                o = o + b_nn(p_c, vc * hm_ref[h])
            o_ref[rs, :] = o
            return 0

        lax.fori_loop(0, m_ctx // 64, ctx_step, 0)

        def lat_step(r, _):
            start, off = _na_window(r, rows)
            rs = pl.ds(pl.multiple_of(m_ctx + r * 64, 64), 64)
            ws = pl.ds(pl.multiple_of(m_ctx + start * 64, 64), 512)
            qr = q_ref[rs, :] * NA_SCALE
            kw, vw = k_ref[ws, :], v_ref[ws, :]
            o = jnp.zeros((64, 256), F32)
            for h in range(4):
                p_w, p_c = _na_probs(qr * hm_ref[h], kw, kc, b_ref[h, off])
                o = o + b_nn(p_w, vw * hm_ref[h]) + b_nn(p_c, vc * hm_ref[h])
            o_ref[rs, :] = o
            return 0

        lax.fori_loop(0, rows, lat_step, 0)

    (o,), received = _call_with_exchange(body, name, [q, k, v, bias8, hm], [jax.ShapeDtypeStruct((t, 256), F32)], comm)
    return o if comm is None else (o, received)


def na_bwd(q, k, v, do, bias8, n_ctx_rows, name, comm=None):
    t = q.shape[0]
    m_ctx = n_ctx_rows
    rows = (t - m_ctx) // GRID_W
    hm = _na_head_masks()

    def body(q_ref, k_ref, v_ref, do_ref, b_ref, hm_ref, dq_ref, dk_ref, dv_ref, db_ref):
        kc, vc = k_ref[0:m_ctx, :], v_ref[0:m_ctx, :]
        dk_ref[...] = jnp.zeros_like(dk_ref)
        dv_ref[...] = jnp.zeros_like(dv_ref)
        db_ref[...] = jnp.zeros_like(db_ref)

        def head_terms(qh, doh, kw, vw, bias):
            p_w, p_c = _na_probs(qh, kw, kc, bias)
            dp_c = b_nt(doh, vc)
            delta = jnp.sum(p_c * dp_c, axis=-1, keepdims=True)
            if kw is not None:
                dp_w = b_nt(doh, vw)
                delta = delta + jnp.sum(p_w * dp_w, axis=-1, keepdims=True)
                ds_w = p_w * (dp_w - delta)
            else:
                ds_w = None
            ds_c = p_c * (dp_c - delta)
            return p_w, p_c, ds_w, ds_c

        def ctx_step(i, carry):
            dkc, dvc = carry
            rs = pl.ds(pl.multiple_of(i * 64, 64), 64)
            qr, dor = q_ref[rs, :] * NA_SCALE, do_ref[rs, :]
            dq = jnp.zeros((64, 256), F32)
            for h in range(4):
                qh, doh = qr * hm_ref[h], dor * hm_ref[h]
                _, p_c, _, ds_c = head_terms(qh, doh, None, None, None)
                dq = dq + hm_ref[h] * b_nn(ds_c, kc)
                dkc = dkc + b_tn(ds_c, qh)
                dvc = dvc + b_tn(p_c, doh)
            dq_ref[rs, :] = dq * NA_SCALE
            return dkc, dvc

        zc = jnp.zeros((m_ctx, 256), F32)
        carry = lax.fori_loop(0, m_ctx // 64, ctx_step, (zc, zc))

        def lat_step(r, carry):
            dkc, dvc = carry
            start, off = _na_window(r, rows)
            rs = pl.ds(pl.multiple_of(m_ctx + r * 64, 64), 64)
            ws = pl.ds(pl.multiple_of(m_ctx + start * 64, 64), 512)
            qr, dor = q_ref[rs, :] * NA_SCALE, do_ref[rs, :]
            kw, vw = k_ref[ws, :], v_ref[ws, :]
            dq = jnp.zeros((64, 256), F32)
            dkw = jnp.zeros((512, 256), F32)
            dvw = jnp.zeros((512, 256), F32)
            for h in range(4):
                qh, doh = qr * hm_ref[h], dor * hm_ref[h]
                p_w, p_c, ds_w, ds_c = head_terms(qh, doh, kw, vw, b_ref[h, off])
                dq = dq + hm_ref[h] * (b_nn(ds_w, kw) + b_nn(ds_c, kc))
                dkw = dkw + b_tn(ds_w, qh)
                dvw = dvw + b_tn(p_w, doh)
                dkc = dkc + b_tn(ds_c, qh)
                dvc = dvc + b_tn(p_c, doh)
                db_ref[h, off] += ds_w
            dq_ref[rs, :] = dq * NA_SCALE
            dk_ref[ws, :] += dkw
            dv_ref[ws, :] += dvw
            return dkc, dvc

        dkc, dvc = lax.fori_loop(0, rows, lat_step, carry)
        dk_ref[0:m_ctx, :] = dkc
        dv_ref[0:m_ctx, :] = dvc

    row = jax.ShapeDtypeStruct((t, 256), F32)
    return _call_with_exchange(body, name, [q, k, v, do, bias8, hm], [row, row, row, jax.ShapeDtypeStruct(bias8.shape, F32)], comm)


def _na_toeplitz():
    col = np.arange(GRID_W)
    dd = (col[None, :] - col[:, None] + 15).reshape(-1)
    tt = np.zeros((GRID_W * GRID_W, 128), np.float32)
    ok = (dd >= 0) & (dd <= 30)
    tt[np.arange(GRID_W * GRID_W)[ok], dd[ok]] = 1.0
    return tt


def _na_bias8(rpb, name):
    col = np.arange(GRID_W)
    cs = np.clip(col - 8, 0, GRID_W - 16)
    col_mask = (col[None, :] >= cs[:, None]) & (col[None, :] < cs[:, None] + 16)
    rpb2 = jnp.pad(rpb.reshape(60, 31), ((0, 4), (0, 97)))
    (toe,) = whole_fwd(lambda r_, t_: (hdot_nt(r_, t_),), name, [rpb2, jnp.asarray(_na_toeplitz())], [(64, GRID_W * GRID_W)])
    toe = toe[:60].reshape(4, 15, GRID_W, GRID_W)
    b = jnp.stack([toe[:, off:off + 8] for off in range(8)], axis=1)
    b = jnp.where(jnp.asarray(col_mask)[None, None, None], b, NEG)
    return b.transpose(0, 1, 3, 2, 4).reshape(4, 8, GRID_W, 8 * GRID_W)


def _na_rpb_grad(dbias8, name):
    tt = _na_toeplitz()
    sel = np.zeros((64, 256), np.float32)
    for h in range(4):
        for off in range(8):
            for i in range(8):
                sel[h * 15 + off + i, h * 64 + off * 8 + i] = 1.0
    a2 = dbias8.reshape(4, 8, GRID_W, 8, GRID_W).transpose(0, 1, 3, 2, 4).reshape(256, GRID_W * GRID_W)
    (out,) = whole_fwd(lambda a, t_, s_: (hdot(s_, hdot(a, t_)),), name, [a2, jnp.asarray(tt), jnp.asarray(sel)], [(64, 128)])
    return out[:60, :31].reshape(4, 15, 31)


def f_mod(cs, b_mod, w_mod):
    s = _silu(cs)
    return bdot(s, w_mod) + b_mod, s


def loss_and_grad(z, tgt, n_ctx_rows, name, tile=ROW_TILE):
    t, d = z.shape
    tile = min(tile, n_ctx_rows)
    nct = n_ctx_rows // tile

    def body(z_ref, t_ref, dz_ref, loss_ref):
        i = pl.program_id(0)

        @pl.when(i == 0)
        def _():
            loss_ref[...] = jnp.zeros_like(loss_ref)

        @pl.when(i < nct)
        def _():
            dz_ref[...] = jnp.zeros_like(dz_ref)

        @pl.when(i >= nct)
        def _():
            diff = z_ref[...] - t_ref[...]
            dz_ref[...] = diff * (1.0 / d)
            loss_ref[...] += 0.5 * jnp.sum(jnp.sum(diff * diff, axis=-1, keepdims=True) * (1.0 / d), axis=0, keepdims=True)

    dz, loss = pl.pallas_call(
        body, name=name, grid=(t // tile,),
        in_specs=[pl.BlockSpec((tile, d), lambda i: (i, 0)),
                  pl.BlockSpec((tile, d), lambda i: (jnp.maximum(i - nct, 0), 0))],
        out_specs=[pl.BlockSpec((tile, d), lambda i: (i, 0)), pl.BlockSpec((8, 128), lambda i: (0, 0))],
        out_shape=[jax.ShapeDtypeStruct((t, d), F32), jax.ShapeDtypeStruct((8, 128), F32)],
        compiler_params=_cparams(dimension_semantics=("arbitrary",)),
    )(z, tgt)
    return loss[0, 0], dz


def adamw(parts, w, m, v, name, tile=256):
    npart, r, c = parts.shape
    tile = min(tile, r)
    assert r % tile == 0
    c1 = 1.0 / (1.0 - ADAM_B1 ** ADAM_STEP)
    c2 = 1.0 / (1.0 - ADAM_B2 ** ADAM_STEP)

    def body(p_ref, w_ref, m_ref, v_ref, g_ref, d_ref, nm_ref, nv_ref):
        g = p_ref[0].astype(F32)
        for i in range(1, npart):
            g = g + p_ref[i].astype(F32)
        nm = ADAM_B1 * m_ref[...] + (1.0 - ADAM_B1) * g
        nv = ADAM_B2 * v_ref[...] + (1.0 - ADAM_B2) * (g * g)
        g_ref[...] = g
        nm_ref[...] = nm
        nv_ref[...] = nv
        d_ref[...] = -ADAM_LR * ((nm * c1) / (jnp.sqrt(nv * c2) + ADAM_EPS) + ADAM_WD * w_ref[...])

    blk = pl.BlockSpec((tile, c), lambda i: (i, 0))
    return pl.pallas_call(
        body, name=name, grid=(r // tile,),
        in_specs=[pl.BlockSpec((npart, tile, c), lambda i: (0, i, 0)), blk, blk, blk],
        out_specs=[blk] * 4, out_shape=[jax.ShapeDtypeStruct((r, c), F32)] * 4,
        compiler_params=_cparams(dimension_semantics=("arbitrary",)),
    )(parts, w, m, v)


def _peer(x, y, c, k):
    return (1 - x if k & 4 else x, 1 - y if k & 2 else y, 1 - c if k & 1 else c)


def _exchange_out_shapes(arrays, scatter):
    return [jax.ShapeDtypeStruct(a.shape if s else (N_DEV,) + a.shape, a.dtype) for a, s in zip(arrays, scatter)]


def _exchange_sems(n):
    return [pltpu.SemaphoreType.DMA((n, N_DEV - 1)), pltpu.SemaphoreType.DMA((n, N_DEV - 1)), pltpu.SemaphoreType.DMA((n,))]


def _exchange_issue(ins, outs, scatter, send_sems, recv_sems, local_sems):
    n = len(ins)
    x, y, c = lax.axis_index("x"), lax.axis_index("y"), lax.axis_index("c")
    me = 4 * x + 2 * y + c

    def index_of(p):
        return 4 * p[0] + 2 * p[1] + p[2]

    local = []
    for a in range(n):
        src_me = ins[a].at[me] if scatter[a] else ins[a]
        loc = pltpu.make_async_copy(src_me, outs[a].at[me], local_sems.at[a])
        loc.start()
        local.append(loc)
    for k in range(1, N_DEV):
        peer = _peer(x, y, c, k)
        for a in range(n):
            src = ins[a].at[index_of(peer)] if scatter[a] else ins[a]
            pltpu.make_async_remote_copy(
                src_ref=src, dst_ref=outs[a].at[me], send_sem=send_sems.at[a, k - 1], recv_sem=recv_sems.at[a, k - 1],
                device_id=peer, device_id_type=pl.DeviceIdType.MESH).start()

    def finish():
        for k in range(1, N_DEV):
            peer = _peer(x, y, c, k)
            for a in range(n):
                src = ins[a].at[index_of(peer)] if scatter[a] else ins[a]
                cp = pltpu.make_async_remote_copy(
                    src_ref=src, dst_ref=outs[a].at[index_of(peer)], send_sem=send_sems.at[a, k - 1],
                    recv_sem=recv_sems.at[a, k - 1], device_id=peer, device_id_type=pl.DeviceIdType.MESH)
                cp.wait_send()
                cp.wait_recv()
        for loc in local:
            loc.wait()

    return finish


def exchange(arrays, scatter, name):
    n = len(arrays)

    def body(*refs):
        _exchange_issue(refs[:n], refs[n:2 * n], scatter, *refs[2 * n:])()

    hbm = pl.BlockSpec(memory_space=pl.ANY)
    return pl.pallas_call(
        body, name=name, in_specs=[hbm] * n, out_specs=[hbm] * n, out_shape=_exchange_out_shapes(arrays, scatter),
        scratch_shapes=_exchange_sems(n), compiler_params=pltpu.CompilerParams(has_side_effects=True),
    )(*arrays)


def _rope_tables(n_lat, n_ctx):
    tok = np.arange(n_lat)
    freqs = 10000.0 ** (-np.arange(0, 16, 2, dtype=np.float32) / 16.0)

    def table(pos):
        ang = pos.astype(np.float32)[:, None] * freqs[None, :]
        ang = np.concatenate([ang, ang], axis=-1)
        return np.cos(ang), np.sin(ang)

    cr, sr = table(tok // GRID_W)
    cc, sc = table(tok % GRID_W)
    cos = np.tile(np.concatenate([cr, cc], axis=-1), (1, 4))
    sin = np.tile(np.concatenate([sr, sc], axis=-1), (1, 4))
    cos = np.concatenate([np.ones((n_ctx, 128), np.float32), cos], axis=0)
    sin = np.concatenate([np.zeros((n_ctx, 128), np.float32), sin], axis=0)
    return jnp.asarray(cos, F32), jnp.asarray(sin, F32)


def _pad_w_in(w):
    z = lambda n: jnp.zeros((w.shape[0], n), w.dtype)
    return jnp.concatenate([w[:, 1824:2848], w[:, 128:384], w[:, 416:672], w[:, 672:928], w[:, 928:1184], w[:, 1312:1568],
                            w[:, 1568:1824], w[:, 0:128], w[:, 384:416], z(96), w[:, 1184:1312], z(128)], axis=1)


def _unpad_w_in(wp):
    return jnp.concatenate([wp[:, C_GK:C_GK + 128], wp[:, C_GV:C_GV + 256], wp[:, C_GG:C_GG + 32], wp[:, C_NK:C_NK + 256],
                            wp[:, C_NV:C_NV + 256], wp[:, C_SU:C_SU + 256], wp[:, C_GQ:C_GQ + 128], wp[:, C_NQ:C_NQ + 256],
                            wp[:, C_PU:C_PU + 256], wp[:, C_GT:C_GT + 1024]], axis=1)


def _pad_rows(u):
    return jnp.pad(u, ((POOL_HALO, POOL_HALO), (0, 0)))


def _block_diag4(w):
    out = jnp.zeros((256, 256), w.dtype)
    for i in range(4):
        out = lax.dynamic_update_slice(out, w[i], (64 * i, 64 * i))
    return out


def _layer_params(p, big, l):
    e_rep, e_tile, gmask, bdm = _s5_consts()
    wg = jnp.zeros((128, 256), F32)
    wg = lax.dynamic_update_slice(wg, p["gla_w_gate"][l, 0], (0, 0))
    wg = lax.dynamic_update_slice(wg, p["gla_w_gate"][l, 1], (16, 128))
    s5 = []
    for d in range(2):
        s5.append([p["s5_lam_re"][l, d], p["s5_lam_im"][l, d], p["s5_log_dt"][l, d].reshape(16, 1),
                   p["s5_b_re"][l, d].transpose(0, 2, 1).reshape(256, 64), p["s5_b_im"][l, d].transpose(0, 2, 1).reshape(256, 64),
                   p["s5_c_re"][l, d].reshape(256, 64), p["s5_c_im"][l, d].reshape(256, 64), e_rep, e_tile, gmask, bdm])
    havg = jnp.asarray((np.arange(256)[:, None] // 64 == np.arange(256)[None, :] // 64).astype(np.float32) / 64.0)
    e4 = jnp.asarray((np.arange(64)[:, None] == np.arange(256)[None, :] % 64).astype(np.float32))
    return dict(
        g_pre=p["g_pre"][l].reshape(1, D), g_post=p["g_post"][l].reshape(1, D), b_mod=p["b_mod"][l].reshape(1, 3 * D),
        w_mod=big["w_mod"], w_in=_pad_w_in(big["w_in"]), w_out=big["w_out"],
        wg=wg, bg=p["gla_b_gate"][l].reshape(1, 256), g_norm=jnp.pad(p["gla_g_norm"][l].reshape(1, 64), ((0, 7), (0, 0))),
        bias8=_na_bias8(p["na_rpb"][l], f"na_bias_l{l}"), s5=s5, s5_d=p["s5_d"][l].reshape(1, 256), w_glu=big["s5_w_glu"].astype(F32),
        b_glu=p["s5_b_glu"][l].reshape(1, 256), wpool=_block_diag4(p["pool_w"][l]), pool_scale=p["pool_scale"][l].reshape(1, 256),
        havg=havg, e4=e4)


def _cols(pz, start, width):
    return pz[:, start:start + width]


def _layer_fwd(z, modseg, lp, cos, sin, m_ctx, tile, s5_chunk, l, comm=None):
    t = z.shape[0]
    nct = m_ctx // tile
    nm = lambda s: f"{s}_l{l}"
    (h,) = rowwise_fwd(f_pre, nm("pre"), [z], [modseg], [lp["g_pre"]], [D], tile, nct)
    (gt,) = mm_nn_cols(h, lp["w_in"], C_GT, [1024], nm("in_proj_a"), tm=tile)
    pv, nk, nv, su = mm_nn_cols(h, lp["w_in"], C_GV, [256] * 4, nm("in_proj_b"), tm=tile)
    nq, pu, pk, pg, pq = mm_nn_cols(h, lp["w_in"], C_NQ, [256, 256, 128, 128, 128], nm("in_proj_c"), tm=tile)
    q_r, k_r, lgf, lgb = rowwise_fwd(f_gla_prep, nm("gla_prep"), [pk, pg, pq, cos, sin], [], [lp["wg"], lp["bg"]], [128] * 4, tile, nct)
    o1, st_f = gla_scan_fwd(q_r, k_r, pv, lgf, jnp.zeros((t, 256), F32), m_ctx, False, nm("gla_f"))
    o_gla, st_b = gla_scan_fwd(q_r, k_r, pv, lgb, o1, m_ctx, True, nm("gla_r"))
    received = None
    if comm is None:
        o_na = na_fwd(nq, nk, nv, lp["bias8"], m_ctx, nm("na"))
    else:
        o_na, received = na_fwd(nq, nk, nv, lp["bias8"], m_ctx, nm("na"), comm)
    s5p = [whole_fwd(f_s5_params, nm(f"s5_par{d}"), lp["s5"][d], [(1, 1024)] * 2 + [(256, 1024)] * 4) for d in range(2)]
    y1, x0r_f, x0i_f = s5_scan_fwd(su, jnp.zeros((t, 256), F32), *s5p[0], m_ctx, s5_chunk, False, nm("s5_f"))
    y5, x0r_b, x0i_b = s5_scan_fwd(su, y1, *s5p[1], m_ctx, s5_chunk, True, nm("s5_r"))
    pm = jnp.concatenate([pool_apply(_pad_rows(pu[:m_ctx]), m_ctx, False, nm("pool_c")),
                          pool_apply(_pad_rows(pu[m_ctx:]), t - m_ctx, False, nm("pool_x"))], axis=0)
    mix_rows = [o_gla, o_na, y5, su, pm, gt]
    mix_globs = [lp["g_norm"], lp["s5_d"], lp["w_glu"], lp["b_glu"], lp["wpool"], lp["pool_scale"], lp["havg"], lp["e4"]]
    (yg,) = rowwise_fwd(f_mix, nm("mix"), mix_rows, [], mix_globs, [D], tile, nct)
    out = mm_nn([yg], lp["w_out"], nm("out_proj"), tm=tile)
    (z_new,) = rowwise_fwd(f_post, nm("post"), [z, out], [modseg], [lp["g_post"]], [D], tile, nct)
    saved = dict(z=z, h=h, pv=pv, nk=nk, nv=nv, su=su, nq=nq, pk=pk, pg=pg, pq=pq, q_r=q_r, k_r=k_r, lgf=lgf, lgb=lgb,
                 st_f=st_f, st_b=st_b, s5p=s5p, x0f=(x0r_f, x0i_f), x0b=(x0r_b, x0i_b), mix_rows=mix_rows, mix_globs=mix_globs,
                 yg=yg, out=out)
    return z_new, saved, received


def _f_pre_res(x, mod, g_pre):
    return f_pre(x, mod, g_pre)[0], x


def _layer_bwd(dz_new, sv, modseg, lp, cos, sin, m_ctx, tile, s5_chunk, l, comm=None, gdt=F32):
    t = dz_new.shape[0]
    nct = m_ctx // tile
    nm = lambda s: f"{s}_l{l}"
    g = {}
    dz_res, dout, dmod_post, g["g_post"] = rowwise_bwd(f_post, nm("post_b"), [sv["z"], sv["out"]], [modseg], [lp["g_post"]],
                                                       [dz_new], tile, nct, [True, True], [True])
    dyg = mm_nt([dout], lp["w_out"], nm("out_proj_dx"), tm=tile)
    (g["w_out"],) = mm_tn(sv["yg"], [dout], nm("out_proj_dw"), tm=tile, out_dtype=gdt)
    res = rowwise_bwd(f_mix, nm("mix_b"), sv["mix_rows"], [], sv["mix_globs"], [dyg], tile, nct, [True] * 6, [True] * 6 + [False] * 2)
    do_gla, do_na, dy5, dsu_a, dpm, dgt = res[:6]
    g["g_norm"], g["s5_d"], g["w_glu"], g["b_glu"], g["wpool"], g["pool_scale"] = res[6:]
    dpu = jnp.concatenate([pool_apply(_pad_rows(dpm[:m_ctx]), m_ctx, True, nm("pool_c_b")),
                           pool_apply(_pad_rows(dpm[m_ctx:]), t - m_ctx, True, nm("pool_x_b"))], axis=0)
    r_b = s5_scan_bwd(sv["su"], dy5, dsu_a, *sv["x0b"], *sv["s5p"][1], m_ctx, s5_chunk, True, nm("s5_r_b"))
    r_f = s5_scan_bwd(sv["su"], dy5, r_b[0], *sv["x0f"], *sv["s5p"][0], m_ctx, s5_chunk, False, nm("s5_f_b"))
    dsu = r_f[0]
    g["s5"] = [whole_bwd(f_s5_params, nm(f"s5_par{d}_b"), lp["s5"][d], list(r[1:]), [True] * 7 + [False] * 4)
               for d, r in ((0, r_f), (1, r_b))]
    (dnq, dnk, dnv, dbias8), received = na_bwd(sv["nq"], sv["nk"], sv["nv"], do_na, lp["bias8"], m_ctx, nm("na_b"), comm)
    g["rpb"] = _na_rpb_grad(dbias8, nm("na_rpb_b"))
    zq, zv = jnp.zeros((t, 128), F32), jnp.zeros((t, 256), F32)
    dq1, dk1, dv1, dlgb = gla_scan_bwd(sv["q_r"], sv["k_r"], sv["pv"], sv["lgb"], sv["st_b"], do_gla, (zq, zq, zv), m_ctx, True, nm("gla_r_b"))
    dq_r, dk_r, dpv, dlgf = gla_scan_bwd(sv["q_r"], sv["k_r"], sv["pv"], sv["lgf"], sv["st_f"], do_gla, (dq1, dk1, dv1), m_ctx, False, nm("gla_f_b"))
    dpk, dpg, dpq, g["wg"], g["bg"] = rowwise_bwd(f_gla_prep, nm("gla_prep_b"), [sv["pk"], sv["pg"], sv["pq"], cos, sin], [],
                                                  [lp["wg"], lp["bg"]], [dq_r, dk_r, dlgf, dlgb], tile, nct,
                                                  [True, True, True, False, False], [True, True])
    parts = [dgt, dpv, dnk, dnv, dsu, dnq, dpu, dpk, dpg, dpq, jnp.zeros((t, 128), F32)]
    dh = mm_nt(parts, lp["w_in"], nm("in_proj_dx"), tm=tile)
    g["w_in"] = _unpad_w_in(jnp.concatenate(mm_tn(sv["h"], parts, nm("in_proj_dw"), tm=tile, out_dtype=gdt), axis=1))
    dz, dmod_pre, g["g_pre"] = rowwise_bwd(_f_pre_res, nm("pre_b"), [sv["z"]], [modseg], [lp["g_pre"]], [dh, dz_res], tile, nct, [True], [True])
    return dz, dmod_pre, dmod_post, g, received


def _f_mod_sum(cs, b_mod, w_mod):
    mod, _ = f_mod(cs, b_mod, w_mod)
    return mod, cs


def local_step(x, c, ctx, tgt, p, shards=None, tile=ROW_TILE, s5_chunk=S5_CHUNK):
    n_lat, m_ctx = x.shape[0], ctx.shape[0]
    n_layers = p["g_pre"].shape[0]
    z = jnp.concatenate([ctx, x], axis=0)
    cos, sin = _rope_tables(n_lat, m_ctx)
    cs = jnp.concatenate([c.reshape(1, D), p["c_ctx"].reshape(1, D), jnp.zeros((6, D), F32)], axis=0)
    gather = [False] * len(_SHARDED)
    lps, mods, silus, saves = [], [], [], []
    got = exchange(shards[0], gather, "gather_weights_l0") if shards is not None else None
    for l in range(n_layers):
        if shards is None:
            big = {n: p[n][l] for n in _SHARDED}
        else:
            big = {n: _gathered(g, _BY_COLS[n]) for n, g in zip(_SHARDED, got)}
        lp = _layer_params(p, big, l)
        mod8, s8 = whole_fwd(f_mod, f"mod_l{l}", [cs, lp["b_mod"], lp["w_mod"]], [(8, 3 * D), (8, D)])
        modseg = mod8[:2].reshape(2, 1, 3 * D)
        comm = (shards[l + 1], gather) if shards is not None and l + 1 < n_layers else None
        z, sv, got = _layer_fwd(z, modseg, lp, cos, sin, m_ctx, tile, s5_chunk, l, comm)
        lps.append(lp); mods.append(modseg); silus.append(s8); saves.append(sv)
    loss, dz = loss_and_grad(z, tgt, m_ctx, "loss", tile)
    grads, received = [None] * n_layers, [None] * n_layers
    gdt = F32 if shards is None else BF16
    dcs = jnp.zeros((8, D), F32)
    pending = None
    for l in reversed(range(n_layers)):
        lp = lps[l]
        dz, dmod_pre, dmod_post, g, got = _layer_bwd(dz, saves[l], mods[l], lp, cos, sin, m_ctx, tile, s5_chunk, l, pending, gdt)
        if pending is not None:
            received[l + 1] = got
        dmod = jnp.concatenate([dmod_pre.reshape(2, 3 * D)[:, :2 * D], dmod_post.reshape(2, 3 * D)[:, 2 * D:]], axis=1)
        dmod8 = jnp.pad(dmod, ((0, 6), (0, 0)))
        dcs, g["b_mod"] = whole_bwd(_f_mod_sum, f"mod_b_l{l}", [cs, lp["b_mod"], lp["w_mod"]], [dmod8, dcs], [True, True, False])
        g["w_mod"] = jnp.concatenate(mm_tn(silus[l], [dmod8[:, :D], dmod8[:, D:2 * D], dmod8[:, 2 * D:]], f"mod_dw_l{l}", tm=8, out_dtype=gdt), axis=1)
        grads[l] = g
        if shards is not None:
            pending = (_layer_sends(g), [True] * len(_SHARDED))
    if shards is not None:
        received[0] = exchange(pending[0] + [_small_sends(dcs[1], grads)], pending[1] + [False], "exchange_grads_l0")
    return loss, dz[m_ctx:], dcs[1], grads, received


_WEIGHTS = ["c_ctx", "w_mod", "b_mod", "g_pre", "g_post", "w_in", "w_out", "gla_w_gate", "gla_b_gate", "gla_g_norm", "na_rpb",
            "s5_lam_re", "s5_lam_im", "s5_log_dt", "s5_b_re", "s5_b_im", "s5_c_re", "s5_c_im", "s5_d", "s5_w_glu", "s5_b_glu",
            "pool_w", "pool_scale"]
_INPUTS = ["x", "c", "ctx"] + _WEIGHTS + ["loss_target"] + ["m_" + n for n in _WEIGHTS] + ["v_" + n for n in _WEIGHTS]
_SHARDED = ["w_mod", "w_in", "w_out", "s5_w_glu"]
_BY_COLS = {"w_mod": True, "w_in": True, "w_out": False, "s5_w_glu": False}
_SMALL = [n for n in _WEIGHTS if n not in _SHARDED]
_SMALL_PER_LAYER = [n for n in _SMALL if n != "c_ctx"]
_PACK_ROWS = 256


def _pack_plan(like):
    tiled = [i for i, a in enumerate(like) if a.size % 1024 == 0]
    loose = [i for i, a in enumerate(like) if a.size % 1024 != 0]
    tail = -(-sum(like[i].size for i in loose) // 1024) * 8
    rows = sum(like[i].size // 128 for i in tiled) + tail
    return tiled, loose, tail, -(-rows // _PACK_ROWS) * _PACK_ROWS - rows


def _pack(arrs):
    tiled, loose, tail, fill = _pack_plan(arrs)
    dt = arrs[0].dtype
    flat = jnp.concatenate([arrs[i].reshape(-1) for i in loose])
    flat = jnp.pad(flat, (0, tail * 128 - flat.shape[0])).reshape(tail, 128)
    return jnp.concatenate([arrs[i].reshape(-1, 128) for i in tiled] + [flat, jnp.zeros((fill, 128), dt)], axis=0)


def _unpack(packed, like):
    tiled, loose, tail, _ = _pack_plan(like)
    out, row = [None] * len(like), 0
    for i in tiled:
        n = like[i].size // 128
        out[i] = packed[row:row + n].reshape(like[i].shape)
        row += n
    flat, pos = packed[row:row + tail].reshape(-1), 0
    for i in loose:
        out[i] = flat[pos:pos + like[i].size].reshape(like[i].shape)
        pos += like[i].size
    return out


def _gathered(g, cols):
    if cols:
        return g.transpose(1, 0, 2).reshape(g.shape[1], N_DEV * g.shape[2])
    return g.reshape(N_DEV * g.shape[1], g.shape[2])


def _slabs(w, cols):
    r, c = w.shape
    if cols:
        return w.reshape(r, N_DEV, c // N_DEV).transpose(1, 0, 2)
    return w.reshape(N_DEV, r // N_DEV, c)


def _layer_small(g):
    s5 = lambda i, f: jnp.stack([f(g["s5"][d][i]) for d in range(2)])
    return {
        "b_mod": g["b_mod"].reshape(3 * D), "g_pre": g["g_pre"].reshape(D), "g_post": g["g_post"].reshape(D),
        "gla_w_gate": jnp.stack([g["wg"][0:16, 0:128], g["wg"][16:32, 128:256]]),
        "gla_b_gate": g["bg"].reshape(2, 128), "gla_g_norm": g["g_norm"][0], "na_rpb": g["rpb"],
        "s5_lam_re": s5(0, lambda a: a), "s5_lam_im": s5(1, lambda a: a), "s5_log_dt": s5(2, lambda a: a.reshape(16)),
        "s5_b_re": s5(3, lambda a: a.reshape(16, 16, 64).transpose(0, 2, 1)),
        "s5_b_im": s5(4, lambda a: a.reshape(16, 16, 64).transpose(0, 2, 1)),
        "s5_c_re": s5(5, lambda a: a.reshape(16, 16, 64)), "s5_c_im": s5(6, lambda a: a.reshape(16, 16, 64)),
        "s5_d": g["s5_d"].reshape(256), "s5_b_glu": g["b_glu"].reshape(256),
        "pool_w": jnp.stack([g["wpool"][64 * i:64 * i + 64, 64 * i:64 * i + 64] for i in range(4)]),
        "pool_scale": g["pool_scale"].reshape(256),
    }


def _layer_sends(g):
    big = {"w_mod": g["w_mod"], "w_in": g["w_in"], "w_out": g["w_out"], "s5_w_glu": g["w_glu"]}
    return [_slabs(big[n], _BY_COLS[n]).astype(BF16) for n in _SHARDED]


def _small_sends(d_c_ctx, grads):
    per_layer = [_layer_small(g) for g in grads]
    full = {n: jnp.stack([s[n] for s in per_layer]) for n in _SMALL_PER_LAYER}
    full["c_ctx"] = d_c_ctx
    return _pack([full[n] for n in _SMALL]).astype(BF16)


def kernel(x, c, ctx, c_ctx, w_mod, b_mod, g_pre, g_post, w_in, w_out, gla_w_gate, gla_b_gate, gla_g_norm, na_rpb, s5_lam_re, s5_lam_im, s5_log_dt, s5_b_re, s5_b_im, s5_c_re, s5_c_im, s5_d, s5_w_glu, s5_b_glu, pool_w, pool_scale, loss_target, m_c_ctx, m_w_mod, m_b_mod, m_g_pre, m_g_post, m_w_in, m_w_out, m_gla_w_gate, m_gla_b_gate, m_gla_g_norm, m_na_rpb, m_s5_lam_re, m_s5_lam_im, m_s5_log_dt, m_s5_b_re, m_s5_b_im, m_s5_c_re, m_s5_c_im, m_s5_d, m_s5_w_glu, m_s5_b_glu, m_pool_w, m_pool_scale, v_c_ctx, v_w_mod, v_b_mod, v_g_pre, v_g_post, v_w_in, v_w_out, v_gla_w_gate, v_gla_b_gate, v_gla_g_norm, v_na_rpb, v_s5_lam_re, v_s5_lam_im, v_s5_log_dt, v_s5_b_re, v_s5_b_im, v_s5_c_re, v_s5_c_im, v_s5_d, v_s5_w_glu, v_s5_b_glu, v_pool_w, v_pool_scale):
    given = dict(zip(_INPUTS, (x, c, ctx, c_ctx, w_mod, b_mod, g_pre, g_post, w_in, w_out, gla_w_gate, gla_b_gate, gla_g_norm, na_rpb, s5_lam_re, s5_lam_im, s5_log_dt, s5_b_re, s5_b_im, s5_c_re, s5_c_im, s5_d, s5_w_glu, s5_b_glu, pool_w, pool_scale, loss_target, m_c_ctx, m_w_mod, m_b_mod, m_g_pre, m_g_post, m_w_in, m_w_out, m_gla_w_gate, m_gla_b_gate, m_gla_g_norm, m_na_rpb, m_s5_lam_re, m_s5_lam_im, m_s5_log_dt, m_s5_b_re, m_s5_b_im, m_s5_c_re, m_s5_c_im, m_s5_d, m_s5_w_glu, m_s5_b_glu, m_pool_w, m_pool_scale, v_c_ctx, v_w_mod, v_b_mod, v_g_pre, v_g_post, v_w_in, v_w_out, v_gla_w_gate, v_gla_b_gate, v_gla_g_norm, v_na_rpb, v_s5_lam_re, v_s5_lam_im, v_s5_log_dt, v_s5_b_re, v_s5_b_im, v_s5_c_re, v_s5_c_im, v_s5_d, v_s5_w_glu, v_s5_b_glu, v_pool_w, v_pool_scale)))
    n_layers = w_in.shape[0]
    shards = [[given[n][l].astype(BF16) for n in _SHARDED] for l in range(n_layers)]
    p = {n: given[n] for n in _SMALL}
    loss, grad_x, _, _, received = local_step(x[0], c, ctx[0], loss_target[0], p, shards)
    final = {}
    for n in _SHARDED:
        per_layer = [adamw(received[l][_SHARDED.index(n)], given[n][l], given["m_" + n][l], given["v_" + n][l], f"adamw_{n}_l{l}")
                     for l in range(n_layers)]
        final[n] = [jnp.stack([res[kind] for res in per_layer]) for kind in range(4)]
    like = [given[n] for n in _SMALL]
    res = adamw(received[0][-1], _pack(like), _pack([given["m_" + n] for n in _SMALL]), _pack([given["v_" + n] for n in _SMALL]),
                "adamw_small")
    unpacked = [_unpack(packed, like) for packed in res]
    for i, n in enumerate(_SMALL):
        final[n] = [unpacked[kind][i] for kind in range(4)]
    loss = lax.psum(loss, ("x", "y", "c"))
    return (loss, grad_x[None], *[final[n][0] for n in _WEIGHTS], *[final[n][1] for n in _WEIGHTS],
            *[final[n][2] for n in _WEIGHTS], *[final[n][3] for n in _WEIGHTS])
```

```python
import functools
import math

import numpy as np
import jax
import jax.numpy as jnp
from jax import lax
from jax.experimental import pallas as pl
from jax.experimental.pallas import tpu as pltpu

F32 = jnp.float32
BF16 = jnp.bfloat16
HIGHEST = lax.Precision.HIGHEST

D = 1024
GRID_W = 64
EPS = 1e-6
N_DEV = 8
C_GT, C_GV, C_NK, C_NV, C_SU, C_NQ, C_PU, C_GK, C_GG, C_GQ, C_END = 0, 1024, 1280, 1536, 1792, 2048, 2304, 2560, 2688, 2816, 2944
PW = 3072
N_CTX_ORIG = 416
N_IN = 2848
GLA_CHUNK = 64
S5_CHUNK = 256
ROW_TILE = 256
VMEM_LIMIT = 56 * 1024 * 1024

ADAM_LR, ADAM_B1, ADAM_B2, ADAM_EPS, ADAM_WD, ADAM_STEP = 0.001, 0.9, 0.999, 1e-08, 0.01, 10


def _cparams(**kw):
    return pltpu.CompilerParams(vmem_limit_bytes=VMEM_LIMIT, **kw)


def _dg(a, b, ca, cb, precision=None):
    return lax.dot_general(a, b, (((ca,), (cb,)), ((), ())), precision=precision, preferred_element_type=F32)


def hdot(a, b):
    return _dg(a, b, 1, 0, HIGHEST)


def hdot_nt(a, b):
    return _dg(a, b, 1, 1, HIGHEST)


def hdot_tn(a, b):
    return _dg(a, b, 0, 0, HIGHEST)


def b_nn(a, b):
    return _dg(a.astype(BF16), b.astype(BF16), 1, 0)


def b_nt(a, b):
    return _dg(a.astype(BF16), b.astype(BF16), 1, 1)


def b_tn(a, b):
    return _dg(a.astype(BF16), b.astype(BF16), 0, 0)


@jax.custom_vjp
def bdot(a, b):
    return b_nn(a, b)


def _bdot_fwd(a, b):
    return b_nn(a, b), (a, b)


def _bdot_bwd(res, ct):
    a, b = res
    return b_nt(ct, b).astype(a.dtype), b_tn(a, ct).astype(b.dtype)


bdot.defvjp(_bdot_fwd, _bdot_bwd)


def _log_sigmoid(z):
    return jnp.minimum(z, 0.0) - jnp.log(1.0 + jnp.exp(-jnp.abs(z)))


def _silu(z):
    return z * jax.nn.sigmoid(z)


def _gelu(z):
    return 0.5 * z * (1.0 + jnp.tanh(math.sqrt(2.0 / math.pi) * (z + 0.044715 * (z * z * z))))


def _cat(vals):
    return vals[0] if len(vals) == 1 else jnp.concatenate(vals, axis=-1)


def mm_nn(a_parts, b, name, tm=ROW_TILE, tn=1024):
    t = a_parts[0].shape[0]
    k, n = b.shape
    na = len(a_parts)
    tn = min(tn, n)

    def body(*refs):
        a = _cat([r[...].astype(BF16) for r in refs[:na]])
        refs[na + 1][...] = _dg(a, refs[na][...].astype(BF16), 1, 0)

    return pl.pallas_call(
        body, name=name, grid=(n // tn, t // tm),
        in_specs=[pl.BlockSpec((tm, p.shape[1]), lambda j, i: (i, 0)) for p in a_parts]
        + [pl.BlockSpec((k, tn), lambda j, i: (0, j))],
        out_specs=pl.BlockSpec((tm, tn), lambda j, i: (i, j)),
        out_shape=jax.ShapeDtypeStruct((t, n), F32),
        compiler_params=_cparams(dimension_semantics=("arbitrary", "arbitrary")),
    )(*a_parts, b)


def mm_nn_cols(a, b, start, widths, name, tm=ROW_TILE):
    t, k = a.shape
    tn = 1024
    assert start % tn == 0 and sum(widths) <= tn

    def body(a_ref, b_ref, *o_refs):
        r = _dg(a_ref[...].astype(BF16), b_ref[...].astype(BF16), 1, 0)
        off = 0
        for o_ref, w in zip(o_refs, widths):
            o_ref[...] = r[:, off:off + w]
            off += w

    return pl.pallas_call(
        body, name=name, grid=(t // tm,),
        in_specs=[pl.BlockSpec((tm, k), lambda i: (i, 0)), pl.BlockSpec((k, tn), lambda i: (0, start // tn))],
        out_specs=[pl.BlockSpec((tm, w), lambda i: (i, 0)) for w in widths],
        out_shape=[jax.ShapeDtypeStruct((t, w), F32) for w in widths],
        compiler_params=_cparams(dimension_semantics=("arbitrary",)),
    )(a, b)


def mm_nt(a_parts, b, name, tm=ROW_TILE):
    t = a_parts[0].shape[0]
    n, k = b.shape
    na = len(a_parts)

    def body(*refs):
        a = _cat([r[...].astype(BF16) for r in refs[:na]])
        refs[na + 1][...] = _dg(a, refs[na][...].astype(BF16), 1, 1)

    return pl.pallas_call(
        body, name=name, grid=(t // tm,),
        in_specs=[pl.BlockSpec((tm, p.shape[1]), lambda i: (i, 0)) for p in a_parts]
        + [pl.BlockSpec((n, k), lambda i: (0, 0))],
        out_specs=pl.BlockSpec((tm, n), lambda i: (i, 0)),
        out_shape=jax.ShapeDtypeStruct((t, n), F32),
        compiler_params=_cparams(dimension_semantics=("arbitrary",)),
    )(*a_parts, b)


def mm_tn(a, b_parts, name, tm=ROW_TILE, tn=1024, out_dtype=F32):
    t, k = a.shape
    widths = [p.shape[1] for p in b_parts]
    n = sum(widths)
    assert n % tn == 0
    groups, cur, acc = [], [], 0
    for idx, w in enumerate(widths):
        cur.append(idx)
        acc += w
        if acc == tn:
            groups.append(cur)
            cur, acc = [], 0
        assert acc < tn
    assert not cur
    outs = []
    for gi, grp in enumerate(groups):
        parts = [b_parts[i] for i in grp]
        npart = len(parts)
        nsteps = t // tm

        def body(*refs, npart=npart, nsteps=nsteps):
            a_v = refs[0][...].astype(BF16)
            b_v = _cat([r[...].astype(BF16) for r in refs[1:1 + npart]])
            o_ref, acc_ref = refs[1 + npart], refs[2 + npart]
            r = _dg(a_v, b_v, 0, 0)

            @pl.when(pl.program_id(0) == 0)
            def _():
                acc_ref[...] = r

            @pl.when(pl.program_id(0) != 0)
            def _():
                acc_ref[...] += r

            @pl.when(pl.program_id(0) == nsteps - 1)
            def _():
                o_ref[...] = acc_ref[...].astype(o_ref.dtype)

        outs.append(pl.pallas_call(
            body, name=f"{name}_{gi}", grid=(nsteps,),
            in_specs=[pl.BlockSpec((tm, k), lambda i: (i, 0))]
            + [pl.BlockSpec((tm, p.shape[1]), lambda i: (i, 0)) for p in parts],
            out_specs=pl.BlockSpec((k, tn), lambda i: (0, 0)),
            out_shape=jax.ShapeDtypeStruct((k, tn), out_dtype),
            scratch_shapes=[pltpu.VMEM((k, tn), F32)],
            compiler_params=_cparams(dimension_semantics=("arbitrary",)),
        )(a, *parts))
    return outs


def _seg_of(i, nct):
    return jnp.where(i < nct, 1, 0)


def rowwise_fwd(fn, name, rows, segs, globs, out_widths, tile, nct):
    t = rows[0].shape[0]
    nr, ns, ng = len(rows), len(segs), len(globs)

    def body(*refs):
        vals = [r[...] for r in refs[:nr]] + [r[0] for r in refs[nr:nr + ns]] + [r[...] for r in refs[nr + ns:nr + ns + ng]]
        outs = fn(*vals)
        for o_ref, o in zip(refs[nr + ns + ng:], outs):
            o_ref[...] = o

    return pl.pallas_call(
        body, name=name, grid=(t // tile,),
        in_specs=[pl.BlockSpec((tile, r.shape[1]), lambda i: (i, 0)) for r in rows]
        + [pl.BlockSpec((1, 1, s.shape[2]), lambda i: (_seg_of(i, nct), 0, 0)) for s in segs]
        + [pl.BlockSpec(g.shape, lambda i: (0, 0)) for g in globs],
        out_specs=[pl.BlockSpec((tile, w), lambda i: (i, 0)) for w in out_widths],
        out_shape=[jax.ShapeDtypeStruct((t, w), F32) for w in out_widths],
        compiler_params=_cparams(dimension_semantics=("arbitrary",)),
    )(*rows, *segs, *globs)


def rowwise_bwd(fn, name, rows, segs, globs, cts, tile, nct, row_diff, glob_diff):
    t = rows[0].shape[0]
    nr, ns, ng, nc = len(rows), len(segs), len(globs), len(cts)
    d_rows = [i for i in range(nr) if row_diff[i]]
    d_globs = [i for i in range(ng) if glob_diff[i]]

    def body(*refs):
        in_refs, out_refs = refs[:nr + ns + ng + nc], refs[nr + ns + ng + nc:]
        row_v = [r[...] for r in in_refs[:nr]]
        seg_v = [r[0] for r in in_refs[nr:nr + ns]]
        glob_v = [r[...] for r in in_refs[nr + ns:nr + ns + ng]]
        ct_v = tuple(r[...] for r in in_refs[nr + ns + ng:])

        def wrapped(dr, sv, dg):
            rv = list(row_v)
            for j, i in enumerate(d_rows):
                rv[i] = dr[j]
            gv = list(glob_v)
            for j, i in enumerate(d_globs):
                gv[i] = dg[j]
            return tuple(fn(*rv, *sv, *gv))

        _, vjp = jax.vjp(wrapped, [row_v[i] for i in d_rows], seg_v, [glob_v[i] for i in d_globs])
        c_rows, c_segs, c_globs = vjp(ct_v)
        i = pl.program_id(0)
        k = 0
        for c in c_rows:
            out_refs[k][...] = c
            k += 1
        seg_first = jnp.logical_or(i == 0, i == nct)
        for c in c_segs:
            ref = out_refs[k]
            k += 1

            @pl.when(seg_first)
            def _(ref=ref, c=c):
                ref[0] = c

            @pl.when(jnp.logical_not(seg_first))
            def _(ref=ref, c=c):
                ref[0] += c
        for c in c_globs:
            ref = out_refs[k]
            k += 1

            @pl.when(i == 0)
            def _(ref=ref, c=c):
                ref[...] = c

            @pl.when(i != 0)
            def _(ref=ref, c=c):
                ref[...] += c

    return pl.pallas_call(
        body, name=name, grid=(t // tile,),
        in_specs=[pl.BlockSpec((tile, r.shape[1]), lambda i: (i, 0)) for r in rows]
        + [pl.BlockSpec((1, 1, s.shape[2]), lambda i: (_seg_of(i, nct), 0, 0)) for s in segs]
        + [pl.BlockSpec(g.shape, lambda i: (0, 0)) for g in globs]
        + [pl.BlockSpec((tile, c.shape[1]), lambda i: (i, 0)) for c in cts],
        out_specs=[pl.BlockSpec((tile, rows[i].shape[1]), lambda i: (i, 0)) for i in d_rows]
        + [pl.BlockSpec((1, 1, s.shape[2]), lambda i: (_seg_of(i, nct), 0, 0)) for s in segs]
        + [pl.BlockSpec(globs[i].shape, lambda i: (0, 0)) for i in d_globs],
        out_shape=[jax.ShapeDtypeStruct(rows[i].shape, F32) for i in d_rows]
        + [jax.ShapeDtypeStruct(s.shape, F32) for s in segs]
        + [jax.ShapeDtypeStruct(globs[i].shape, F32) for i in d_globs],
        compiler_params=_cparams(dimension_semantics=("arbitrary",)),
    )(*rows, *segs, *globs, *cts)


def f_pre(x, mod, g_pre):
    shift, scale = mod[:, :D], mod[:, D:2 * D]
    rs = lax.rsqrt(jnp.mean(x * x, axis=-1, keepdims=True) + EPS)
    return ((x * rs) * g_pre * (1.0 + scale) + shift,)


def f_post(x, out, mod, g_post):
    gate = mod[:, 2 * D:]
    rs = lax.rsqrt(jnp.mean(out * out, axis=-1, keepdims=True) + EPS)
    return (x + gate * ((out * rs) * g_post),)


def f_mix(o_gla, o_na, y5, u5, pm, gcols, g_norm, s5_d, w_glu, b_glu, wpool, pool_scale, havg, e4):
    ms = hdot(o_gla * o_gla, havg)
    y_gla = o_gla * lax.rsqrt(ms + EPS) * jnp.sum(hdot(g_norm, e4), axis=0, keepdims=True)
    g = _gelu(u5 * s5_d + y5)
    y_s5 = g * jax.nn.sigmoid(bdot(g, w_glu) + b_glu)
    y_pool = bdot(pm, wpool) * pool_scale
    ycat = jnp.concatenate([y_gla, o_na, y_s5, y_pool], axis=-1)
    return (ycat * _silu(gcols),)


@jax.custom_vjp
def _rot_half16(x):
    lane = lax.broadcasted_iota(jnp.int32, x.shape, 1)
    first = jnp.bitwise_and(lane, 15) < 8
    return jnp.where(first, -pltpu.roll(x, x.shape[1] - 8, 1), pltpu.roll(x, 8, 1))


def _rot_fwd(x):
    return _rot_half16(x), None


def _rot_bwd(_, ct):
    return (-_rot_half16(ct),)


_rot_half16.defvjp(_rot_fwd, _rot_bwd)


def f_gla_prep(pk, pg, pq, cos, sin, wg, bg):
    z = bdot(pg, wg) + bg
    lg = _log_sigmoid(z) * (1.0 / 16.0)
    k_r = pk * cos + _rot_half16(pk) * sin
    q_r = (pq * cos + _rot_half16(pq) * sin) * (32.0 ** -0.5)
    return q_r, k_r, lg[:, :128], lg[:, 128:]


def _gla_consts(rev):
    c = GLA_CHUNK
    i = np.arange(c)
    inc = (i[None, :] >= i[:, None]) if rev else (i[None, :] <= i[:, None])
    mq = np.stack([(np.arange(128) // 32 == h) for h in range(4)]).astype(np.float32).reshape(4, 1, 128)
    mv = np.stack([(np.arange(256) // 64 == h) for h in range(4)]).astype(np.float32).reshape(4, 1, 256)
    bdt = (np.arange(256)[:, None] // 64 == np.arange(128)[None, :] // 32).astype(np.float32)
    inc = inc.astype(np.float32)
    return jnp.asarray(inc), jnp.asarray(inc.T.copy()), jnp.asarray(mq), jnp.asarray(mv), jnp.asarray(bdt)


def _stack_heads(x, m_ref):
    return jnp.concatenate([x * m_ref[h] for h in range(4)], axis=0)


def _tile4(m):
    return jnp.concatenate([m, m, m, m], axis=0)


def _fold_heads(r4, m_ref):
    r = r4.shape[0] // 4
    out = m_ref[0] * r4[0:r]
    for h in range(1, 4):
        out = out + m_ref[h] * r4[h * r:(h + 1) * r]
    return out


def _gla_chunk_of(s, n_ctx_chunks, n_chunks, rev):
    if not rev:
        return s
    return jnp.where(s < n_ctx_chunks, n_ctx_chunks - 1 - s, n_ctx_chunks + n_chunks - 1 - s)


def gla_scan_fwd(q, k, v, lg, acc, n_ctx_rows, rev, name, comm=None):
    t = q.shape[0]
    nch, ncc = t // GLA_CHUNK, n_ctx_rows // GLA_CHUNK
    inc, inc_t, mq, mv, bdt = _gla_consts(rev)

    def body(q_ref, k_ref, v_ref, lg_ref, acc_ref, inc_ref, inct_ref, mq_ref, mv_ref, bdt_ref, o_ref, st_ref):
        lmask, lmask_t = inc_ref[...], inct_ref[...]
        bd = bdt_ref[...]

        def step(s, st):
            c = _gla_chunk_of(s, ncc, nch, rev)
            rows = pl.ds(pl.multiple_of(c * GLA_CHUNK, GLA_CHUNK), GLA_CHUNK)
            qc, kc, vc, lgc = q_ref[rows, :], k_ref[rows, :], v_ref[rows, :], lg_ref[rows, :]
            st_ref[c] = st
            b = hdot(lmask, lgc)
            blast = jnp.sum(lgc, axis=0, keepdims=True)
            qe, ke, kd = qc * jnp.exp(b), kc * jnp.exp(-b), kc * jnp.exp(blast - b)
            ke4, v4 = _stack_heads(ke, mq_ref), _stack_heads(vc, mv_ref)
            at = _tile4(lmask_t) * b_nt(ke4, qe)
            o_ref[rows, :] = acc_ref[rows, :] + b_nt(qe, st) + b_tn(at, v4)
            return st * jnp.exp(blast) + bd * hdot_tn(vc, kd)

        lax.fori_loop(0, nch, step, jnp.zeros((256, 128), F32))

    return _call_with_exchange(body, name, [q, k, v, lg, acc, inc, inc_t, mq, mv, bdt],
                               [jax.ShapeDtypeStruct((t, 256), F32), jax.ShapeDtypeStruct((nch, 256, 128), F32)], comm)


def gla_scan_bwd(q, k, v, lg, st, do, acc, n_ctx_rows, rev, name, comm=None):
    t = q.shape[0]
    nch, ncc = t // GLA_CHUNK, n_ctx_rows // GLA_CHUNK
    inc, inc_t, mq, mv, bdt = _gla_consts(rev)

    def body(q_ref, k_ref, v_ref, lg_ref, st_ref, do_ref, aq_ref, ak_ref, av_ref, inc_ref, inct_ref, mq_ref, mv_ref, bdt_ref,
             dq_ref, dk_ref, dv_ref, dlg_ref):
        lmask, lmask_t = inc_ref[...], inct_ref[...]
        bd = bdt_ref[...]

        def step(j, carry):
            dst, gsum = carry
            s = nch - 1 - j
            c = _gla_chunk_of(s, ncc, nch, rev)
            rows = pl.ds(pl.multiple_of(c * GLA_CHUNK, GLA_CHUNK), GLA_CHUNK)
            qc, kc, vc, lgc, doc = q_ref[rows, :], k_ref[rows, :], v_ref[rows, :], lg_ref[rows, :], do_ref[rows, :]
            stc = st_ref[c]
            b = hdot(lmask, lgc)
            blast = jnp.sum(lgc, axis=0, keepdims=True)
            eb, enb, edb = jnp.exp(b), jnp.exp(-b), jnp.exp(blast - b)
            qe, ke, kd = qc * eb, kc * enb, kc * edb
            ke4, v4 = _stack_heads(ke, mq_ref), _stack_heads(vc, mv_ref)
            lm4 = _tile4(lmask_t)
            at = lm4 * b_nt(ke4, qe)
            dat = lm4 * hdot_nt(v4, doc)
            dqe = hdot(doc, stc) + hdot_tn(dat, ke4)
            dke = _fold_heads(hdot(dat, qe), mq_ref)
            dv = b_nt(kd, dst) + _fold_heads(b_nn(at, doc), mv_ref)
            dkd = hdot(vc, dst)
            dq = dqe * eb
            dk = dke * enb + dkd * edb
            g = qc * dq - kc * dk
            dlg_ref[rows, :] = hdot_tn(lmask, g) + gsum
            dq_ref[rows, :] = aq_ref[rows, :] + dq
            dk_ref[rows, :] = ak_ref[rows, :] + dk
            dv_ref[rows, :] = av_ref[rows, :] + dv
            dst_new = dst * jnp.exp(blast) + bd * hdot_tn(doc, qe)
            return dst_new, gsum + jnp.sum(g, axis=0, keepdims=True)

        lax.fori_loop(0, nch, step, (jnp.zeros((256, 128), F32), jnp.zeros((1, 128), F32)))

    return _call_with_exchange(body, name, [q, k, v, lg, st, do, *acc, inc, inc_t, mq, mv, bdt],
                               [jax.ShapeDtypeStruct((t, 128), F32), jax.ShapeDtypeStruct((t, 128), F32),
                                jax.ShapeDtypeStruct((t, 256), F32), jax.ShapeDtypeStruct((t, 128), F32)], comm)


def whole_fwd(fn, name, args, out_shapes):
    def body(*refs):
        outs = fn(*[r[...] for r in refs[:len(args)]])
        for o_ref, o in zip(refs[len(args):], outs):
            o_ref[...] = o

    vm = pl.BlockSpec(memory_space=pltpu.VMEM)
    return pl.pallas_call(
        body, name=name, in_specs=[vm] * len(args), out_specs=[vm] * len(out_shapes),
        out_shape=[jax.ShapeDtypeStruct(s, F32) for s in out_shapes], compiler_params=_cparams(),
    )(*args)


def whole_bwd(fn, name, args, cts, diff):
    d_idx = [i for i in range(len(args)) if diff[i]]

    def body(*refs):
        vals = [r[...] for r in refs[:len(args)]]
        ct_v = tuple(r[...] for r in refs[len(args):len(args) + len(cts)])

        def wrapped(dv):
            av = list(vals)
            for j, i in enumerate(d_idx):
                av[i] = dv[j]
            return tuple(fn(*av))

        _, vjp = jax.vjp(wrapped, [vals[i] for i in d_idx])
        (c_args,) = vjp(ct_v)
        for o_ref, c in zip(refs[len(args) + len(cts):], c_args):
            o_ref[...] = c

    vm = pl.BlockSpec(memory_space=pltpu.VMEM)
    return pl.pallas_call(
        body, name=name, in_specs=[vm] * (len(args) + len(cts)), out_specs=[vm] * len(d_idx),
        out_shape=[jax.ShapeDtypeStruct(args[i].shape, F32) for i in d_idx], compiler_params=_cparams(),
    )(*args, *cts)


def _s5_consts():
    e_rep = (np.arange(256)[:, None] // 16 == np.arange(16)[None, :]).astype(np.float32)
    e_tile = (np.arange(64)[:, None] == np.arange(1024)[None, :] % 64).astype(np.float32)
    gmask = (np.arange(16)[:, None] == np.arange(1024)[None, :] // 64).astype(np.float32)
    bdm = (np.arange(256)[:, None] // 16 == np.arange(1024)[None, :] // 64).astype(np.float32)
    return jnp.asarray(e_rep), jnp.asarray(e_tile), jnp.asarray(gmask), jnp.asarray(bdm)


def f_s5_params(lam_re, lam_im, log_dt, bt_re, bt_im, ct_re, ct_im, e_rep, e_tile, gmask, bdm):
    dt = jnp.exp(log_dt)
    mag = jnp.exp(lam_re * dt)
    ang = lam_im * dt
    lb_re, lb_im = mag * jnp.cos(ang), mag * jnp.sin(ang)
    num_re, num_im = lb_re - 1.0, lb_im
    den = lam_re * lam_re + lam_im * lam_im
    coef_re = (num_re * lam_re + num_im * lam_im) / den
    coef_im = (num_im * lam_re - num_re * lam_im) / den
    cr, ci = hdot(e_rep, coef_re), hdot(e_rep, coef_im)
    bbt_re = cr * bt_re - ci * bt_im
    bbt_im = cr * bt_im + ci * bt_re
    a_re = jnp.sum(hdot(lb_re, e_tile) * gmask, axis=0, keepdims=True)
    a_im = jnp.sum(hdot(lb_im, e_tile) * gmask, axis=0, keepdims=True)
    return (a_re, a_im, hdot(bbt_re, e_tile) * bdm, hdot(bbt_im, e_tile) * bdm,
            hdot(ct_re, e_tile) * bdm, hdot(ct_im, e_tile) * bdm)


def _s5_doubling(xr, xi, pr, pi, pos, n, steps, rev):
    rows = xr.shape[0]
    for s in steps:
        if rev:
            keep = pos < (n - s)
            sr, si = pltpu.roll(xr, rows - s, 0), pltpu.roll(xi, rows - s, 0)
        else:
            keep = pos >= s
            sr, si = pltpu.roll(xr, s, 0), pltpu.roll(xi, s, 0)
        sr, si = jnp.where(keep, sr, 0.0), jnp.where(keep, si, 0.0)
        xr, xi = xr + pr * sr - pi * si, xi + pr * si + pi * sr
        pr, pi = pr * pr - pi * pi, 2.0 * pr * pi
    return xr, xi, pr, pi


SUBLANES = 8


def _s5_scan(xr, xi, a_re, a_im, rev, chunk, scr):
    xs_r, xs_i, yp_r, yp_i = scr
    ng = chunk // SUBLANES
    x3r, x3i = xr.reshape(ng, SUBLANES, 1024), xi.reshape(ng, SUBLANES, 1024)
    sub = lax.broadcasted_iota(jnp.int32, (SUBLANES, 1024), 0)
    a8r, a8i = a_re, a_im
    for s in (1, 2, 4):
        keep = sub < (SUBLANES - s) if rev else sub >= s
        mr, mi = jnp.where(keep, a8r, 0.0)[None], jnp.where(keep, a8i, 0.0)[None]
        shift = SUBLANES - s if rev else s
        sr, si = pltpu.roll(x3r, shift, 1), pltpu.roll(x3i, shift, 1)
        x3r, x3i = x3r + mr * sr - mi * si, x3i + mr * si + mi * sr
        a8r, a8i = a8r * a8r - a8i * a8i, 2.0 * a8r * a8i
    xr, xi = x3r.reshape(chunk, 1024), x3i.reshape(chunk, 1024)
    nblk = 1024 // 128
    for j in range(nblk):
        xs_r[j] = xr[:, 128 * j:128 * (j + 1)]
        xs_i[j] = xi[:, 128 * j:128 * (j + 1)]
    edge = pl.ds(0 if rev else SUBLANES - 1, ng, stride=SUBLANES)
    gr = jnp.concatenate([xs_r[j, edge, :] for j in range(nblk)], axis=-1)
    gi = jnp.concatenate([xs_i[j, edge, :] for j in range(nblk)], axis=-1)
    grow = lax.broadcasted_iota(jnp.int32, (ng, 1024), 0)
    steps = tuple(1 << k for k in range((ng - 1).bit_length()))
    gr, gi, _, _ = _s5_doubling(gr, gi, a8r, a8i, grow, ng, steps, rev)
    if rev:
        yp_r[...] = jnp.where(grow < ng - 1, pltpu.roll(gr, ng - 1, 0), 0.0)
        yp_i[...] = jnp.where(grow < ng - 1, pltpu.roll(gi, ng - 1, 0), 0.0)
    else:
        yp_r[...] = jnp.where(grow >= 1, pltpu.roll(gr, 1, 0), 0.0)
        yp_i[...] = jnp.where(grow >= 1, pltpu.roll(gi, 1, 0), 0.0)
    sub = lax.broadcasted_iota(jnp.int32, (SUBLANES, 1024), 0)
    tr, ti = jnp.zeros((SUBLANES, 1024), F32), jnp.zeros((SUBLANES, 1024), F32)
    cr, ci = a_re, a_im
    for n in range(1, SUBLANES + 1):
        r = SUBLANES - n if rev else n - 1
        tr, ti = jnp.where(sub == r, cr, tr), jnp.where(sub == r, ci, ti)
        cr, ci = cr * a_re - ci * a_im, cr * a_im + ci * a_re
    for j in range(nblk):
        lanes = slice(128 * j, 128 * (j + 1))
        tr_j, ti_j = tr[:, lanes], ti[:, lanes]
        for g in range(ng):
            rows = slice(g * SUBLANES, (g + 1) * SUBLANES)
            er, ei = yp_r[g:g + 1, lanes], yp_i[g:g + 1, lanes]
            xs_r[j, rows, :] = xs_r[j, rows, :] + tr_j * er - ti_j * ei
            xs_i[j, rows, :] = xs_i[j, rows, :] + tr_j * ei + ti_j * er
    return (jnp.concatenate([xs_r[j] for j in range(nblk)], axis=-1),
            jnp.concatenate([xs_i[j] for j in range(nblk)], axis=-1))


def _s5_scratch(chunk):
    return [pltpu.VMEM((8, chunk, 128), F32), pltpu.VMEM((8, chunk, 128), F32),
            pltpu.VMEM((chunk // SUBLANES, 1024), F32), pltpu.VMEM((chunk // SUBLANES, 1024), F32)]


def _s5_chunk_states(u_c, x0r, x0i, a_re, a_im, bb_re, bb_im, rev, chunk, scr):
    row = lax.broadcasted_iota(jnp.int32, (chunk, 1024), 0)
    first = row == (chunk - 1 if rev else 0)
    inj_r = a_re * x0r - a_im * x0i
    inj_i = a_re * x0i + a_im * x0r
    xr = b_nn(u_c, bb_re) + jnp.where(first, inj_r, 0.0)
    xi = b_nn(u_c, bb_im) + jnp.where(first, inj_i, 0.0)
    return _s5_scan(xr, xi, a_re, a_im, rev, chunk, scr)


def _row_pick(x, idx):
    row = lax.broadcasted_iota(jnp.int32, x.shape, 0)
    return jnp.sum(jnp.where(row == idx, x, 0.0), axis=0, keepdims=True)


def s5_scan_fwd(u, acc, a_re, a_im, bb_re, bb_im, cc_re, cc_im, n_ctx_rows, chunk, rev, name):
    t = u.shape[0]
    nch, ncc = t // chunk, n_ctx_rows // chunk

    def body(u_ref, acc_ref, ar_ref, ai_ref, br_ref, bi_ref, cr_ref, ci_ref, y_ref, x0r_ref, x0i_ref, *scr):
        a_r, a_i = ar_ref[...], ai_ref[...]

        def step(s, carry):
            x0r, x0i = carry
            c = _gla_chunk_of(s, ncc, nch, rev)
            rows = pl.ds(pl.multiple_of(c * chunk, chunk), chunk)
            x0r_ref[c] = x0r
            x0i_ref[c] = x0i
            xr, xi = _s5_chunk_states(u_ref[rows, :], x0r, x0i, a_r, a_i, br_ref[...], bi_ref[...], rev, chunk, scr)
            y_ref[rows, :] = acc_ref[rows, :] + b_nt(xr, cr_ref[...]) - b_nt(xi, ci_ref[...])
            last = 0 if rev else chunk - 1
            return _row_pick(xr, last), _row_pick(xi, last)

        lax.fori_loop(0, nch, step, (jnp.zeros((1, 1024), F32), jnp.zeros((1, 1024), F32)))

    vm = pl.BlockSpec(memory_space=pltpu.VMEM)
    return pl.pallas_call(
        body, name=name, in_specs=[vm] * 8, out_specs=[vm] * 3,
        out_shape=[jax.ShapeDtypeStruct((t, 256), F32), jax.ShapeDtypeStruct((nch, 1, 1024), F32),
                   jax.ShapeDtypeStruct((nch, 1, 1024), F32)],
        scratch_shapes=_s5_scratch(chunk), compiler_params=_cparams(),
    )(u, acc, a_re, a_im, bb_re, bb_im, cc_re, cc_im)


def s5_scan_bwd(u, dy, du_acc, x0r, x0i, a_re, a_im, bb_re, bb_im, cc_re, cc_im, n_ctx_rows, chunk, rev, name):
    t = u.shape[0]
    nch, ncc = t // chunk, n_ctx_rows // chunk

    def body(u_ref, dy_ref, dua_ref, x0r_ref, x0i_ref, ar_ref, ai_ref, br_ref, bi_ref, cr_ref, ci_ref,
             du_ref, dar_ref, dai_ref, dbr_ref, dbi_ref, dcr_ref, dci_ref, *scr):
        a_r, a_i = ar_ref[...], ai_ref[...]
        for ref in (dbr_ref, dbi_ref, dcr_ref, dci_ref):
            ref[...] = jnp.zeros_like(ref)
        row = lax.broadcasted_iota(jnp.int32, (chunk, 1024), 0)
        first_idx, last_idx = (chunk - 1, 0) if rev else (0, chunk - 1)

        def step(j, carry):
            lcr, lci, dar, dai = carry
            s = nch - 1 - j
            c = _gla_chunk_of(s, ncc, nch, rev)
            rows = pl.ds(pl.multiple_of(c * chunk, chunk), chunk)
            u_c, dy_c = u_ref[rows, :], dy_ref[rows, :]
            x0r_c, x0i_c = x0r_ref[c], x0i_ref[c]
            xr, xi = _s5_chunk_states(u_c, x0r_c, x0i_c, a_r, a_i, br_ref[...], bi_ref[...], rev, chunk, scr[:4])
            dcr_ref[...] += b_tn(dy_c, xr)
            dci_ref[...] -= b_tn(dy_c, xi)
            inj_r = a_r * lcr + a_i * lci
            inj_i = a_r * lci - a_i * lcr
            is_last = row == last_idx
            lr = b_nn(dy_c, cr_ref[...]) + jnp.where(is_last, inj_r, 0.0)
            li = -b_nn(dy_c, ci_ref[...]) + jnp.where(is_last, inj_i, 0.0)
            lr, li = _s5_scan(lr, li, a_r, -a_i, not rev, chunk, scr[4:])
            du_ref[rows, :] = dua_ref[rows, :] + b_nt(lr, br_ref[...]) + b_nt(li, bi_ref[...])
            dbr_ref[...] += b_tn(u_c, lr)
            dbi_ref[...] += b_tn(u_c, li)
            if rev:
                pr, pi = pltpu.roll(xr, chunk - 1, 0), pltpu.roll(xi, chunk - 1, 0)
            else:
                pr, pi = pltpu.roll(xr, 1, 0), pltpu.roll(xi, 1, 0)
            is_first = row == first_idx
            pr, pi = jnp.where(is_first, x0r_c, pr), jnp.where(is_first, x0i_c, pi)
            dar = dar + jnp.sum(lr * pr + li * pi, axis=0, keepdims=True)
            dai = dai + jnp.sum(li * pr - lr * pi, axis=0, keepdims=True)
            return _row_pick(lr, first_idx), _row_pick(li, first_idx), dar, dai

        z = jnp.zeros((1, 1024), F32)
        _, _, dar, dai = lax.fori_loop(0, nch, step, (z, z, z, z))
        dar_ref[...] = dar
        dai_ref[...] = dai

    vm = pl.BlockSpec(memory_space=pltpu.VMEM)
    big = jax.ShapeDtypeStruct((256, 1024), F32)
    vec = jax.ShapeDtypeStruct((1, 1024), F32)
    return pl.pallas_call(
        body, name=name, in_specs=[vm] * 11, out_specs=[vm] * 7,
        out_shape=[jax.ShapeDtypeStruct((t, 256), F32), vec, vec, big, big, big, big],
        scratch_shapes=_s5_scratch(chunk) + _s5_scratch(chunk), compiler_params=_cparams(),
    )(u, dy, du_acc, x0r, x0i, a_re, a_im, bb_re, bb_im, cc_re, cc_im)


POOL_HALO = 8


def pool_apply(u_pad, n, transpose, name, tile=ROW_TILE):
    tile = min(tile, n)
    ext = tile + 2 * POOL_HALO

    def body(u_ref, o_ref):
        lax.fori_loop(0, n // tile, functools.partial(step, u_ref, o_ref), 0)

    def step(u_ref, o_ref, i, carry):
        val = u_ref[pl.ds(pl.multiple_of(i * tile, tile), ext), :]
        lane = lax.broadcasted_iota(jnp.int32, (ext, 256), 1)
        half = jnp.left_shift(1, jnp.right_shift(lane, 6))
        trow = lax.broadcasted_iota(jnp.int32, (ext, 256), 0) + (i * tile - POOL_HALO)
        cnt = jnp.minimum(trow + half, n) - jnp.maximum(trow - half, 0)
        inv = 1.0 / jnp.maximum(cnt, 1).astype(F32)
        src = val * inv if transpose else val
        acc = jnp.zeros((tile, 256), F32)
        for d in range(-POOL_HALO, POOL_HALO):
            in_win = jnp.logical_and(d >= -half, d <= half - 1)[POOL_HALO:POOL_HALO + tile]
            shift = d if transpose else -d
            rolled = pltpu.roll(src, shift % ext, 0)[POOL_HALO:POOL_HALO + tile]
            acc = acc + jnp.where(in_win, rolled, 0.0)
        centre = val[POOL_HALO:POOL_HALO + tile]
        if not transpose:
            acc = acc * inv[POOL_HALO:POOL_HALO + tile]
        o_ref[pl.ds(pl.multiple_of(i * tile, tile), tile), :] = acc - centre
        return carry

    vm = pl.BlockSpec(memory_space=pltpu.VMEM)
    return pl.pallas_call(
        body, name=name, in_specs=[vm], out_specs=vm,
        out_shape=jax.ShapeDtypeStruct((n, 256), F32), compiler_params=_cparams(),
    )(u_pad)


NA_SCALE = 64.0 ** -0.5
NEG = -1e30


def _call_with_exchange(compute, name, args, out_shapes, comm):
    vm = pl.BlockSpec(memory_space=pltpu.VMEM)
    n_in, n_out = len(args), len(out_shapes)
    if comm is None:
        outs = pl.pallas_call(compute, name=name, in_specs=[vm] * n_in, out_specs=[vm] * n_out, out_shape=out_shapes,
                              compiler_params=_cparams())(*args)
        return outs, None
    arrays, scatter = comm
    n = len(arrays)

    def body(*refs):
        c_in = refs[n_in:n_in + n]
        c_out = refs[n_in + n + n_out:n_in + 2 * n + n_out]
        finish = _exchange_issue(c_in, c_out, scatter, *refs[n_in + 2 * n + n_out:])
        compute(*refs[:n_in], *refs[n_in + n:n_in + n + n_out])
        finish()

    hbm = pl.BlockSpec(memory_space=pl.ANY)
    outs = pl.pallas_call(
        body, name=name, in_specs=[vm] * n_in + [hbm] * n, out_specs=[vm] * n_out + [hbm] * n,
        out_shape=list(out_shapes) + _exchange_out_shapes(arrays, scatter), scratch_shapes=_exchange_sems(n),
        compiler_params=_cparams(has_side_effects=True),
    )(*args, *arrays)
    return outs[:n_out], outs[n_out:]


def _na_head_masks():
    return jnp.asarray(np.stack([(np.arange(256) // 64 == h) for h in range(4)]).astype(np.float32).reshape(4, 1, 256))


def _na_window(r, rows):
    start = jnp.clip(r - 4, 0, rows - 8)
    return start, start - r + 7


def _na_probs(qh, kw, kc, bias):
    s_c = b_nt(qh, kc)
    m = jnp.max(s_c, axis=-1, keepdims=True)
    if kw is not None:
        s_w = b_nt(qh, kw) + bias
        m = jnp.maximum(m, jnp.max(s_w, axis=-1, keepdims=True))
        p_w = jnp.exp(s_w - m)
    p_c = jnp.exp(s_c - m)
    l = jnp.sum(p_c, axis=-1, keepdims=True)
    if kw is not None:
        l = l + jnp.sum(p_w, axis=-1, keepdims=True)
        return p_w / l, p_c / l
    return None, p_c / l


def na_fwd(q, k, v, bias8, n_ctx_rows, name, comm=None):
    t = q.shape[0]
    m_ctx = n_ctx_rows
    rows = (t - m_ctx) // GRID_W
    hm = _na_head_masks()

    def body(q_ref, k_ref, v_ref, b_ref, hm_ref, o_ref):
        kc, vc = k_ref[0:m_ctx, :], v_ref[0:m_ctx, :]

        def ctx_step(i, _):
            rs = pl.ds(pl.multiple_of(i * 64, 64), 64)
            q4 = _stack_heads(q_ref[rs, :] * NA_SCALE, hm_ref)
            _, p_c = _na_probs(q4, None, kc, None)
            o_ref[rs, :] = _fold_heads(b_nn(p_c, vc), hm_ref)
            return 0

        lax.fori_loop(0, m_ctx // 64, ctx_step, 0)

        def lat_step(r, _):
            start, off = _na_window(r, rows)
            rs = pl.ds(pl.multiple_of(m_ctx + r * 64, 64), 64)
            ws = pl.ds(pl.multiple_of(m_ctx + start * 64, 64), 512)
            q4 = _stack_heads(q_ref[rs, :] * NA_SCALE, hm_ref)
            kw, vw = k_ref[ws, :], v_ref[ws, :]
            p_w, p_c = _na_probs(q4, kw, kc, b_ref[off])
            o_ref[rs, :] = _fold_heads(b_nn(p_w, vw) + b_nn(p_c, vc), hm_ref)
            return 0

        lax.fori_loop(0, rows, lat_step, 0)

    (o,), received = _call_with_exchange(body, name, [q, k, v, bias8, hm], [jax.ShapeDtypeStruct((t, 256), F32)], comm)
    return o if comm is None else (o, received)


def na_bwd(q, k, v, do, bias8, n_ctx_rows, name, comm=None):
    t = q.shape[0]
    m_ctx = n_ctx_rows
    rows = (t - m_ctx) // GRID_W
    hm = _na_head_masks()

    def body(q_ref, k_ref, v_ref, do_ref, b_ref, hm_ref, dq_ref, dk_ref, dv_ref, db_ref):
        kc, vc = k_ref[0:m_ctx, :], v_ref[0:m_ctx, :]
        dk_ref[...] = jnp.zeros_like(dk_ref)
        dv_ref[...] = jnp.zeros_like(dv_ref)
        db_ref[...] = jnp.zeros_like(db_ref)

        def head_terms(qh, doh, kw, vw, bias):
            p_w, p_c = _na_probs(qh, kw, kc, bias)
            dp_c = b_nt(doh, vc)
            delta = jnp.sum(p_c * dp_c, axis=-1, keepdims=True)
            if kw is not None:
                dp_w = b_nt(doh, vw)
                delta = delta + jnp.sum(p_w * dp_w, axis=-1, keepdims=True)
                ds_w = p_w * (dp_w - delta)
            else:
                ds_w = None
            ds_c = p_c * (dp_c - delta)
            return p_w, p_c, ds_w, ds_c

        def ctx_step(i, carry):
            dkc, dvc = carry
            rs = pl.ds(pl.multiple_of(i * 64, 64), 64)
            q4, do4 = _stack_heads(q_ref[rs, :] * NA_SCALE, hm_ref), _stack_heads(do_ref[rs, :], hm_ref)
            _, p_c, _, ds_c = head_terms(q4, do4, None, None, None)
            dq_ref[rs, :] = _fold_heads(b_nn(ds_c, kc), hm_ref) * NA_SCALE
            return dkc + b_tn(ds_c, q4), dvc + b_tn(p_c, do4)

        zc = jnp.zeros((m_ctx, 256), F32)
        carry = lax.fori_loop(0, m_ctx // 64, ctx_step, (zc, zc))

        def lat_step(r, carry):
            dkc, dvc = carry
            start, off = _na_window(r, rows)
            rs = pl.ds(pl.multiple_of(m_ctx + r * 64, 64), 64)
            ws = pl.ds(pl.multiple_of(m_ctx + start * 64, 64), 512)
            q4, do4 = _stack_heads(q_ref[rs, :] * NA_SCALE, hm_ref), _stack_heads(do_ref[rs, :], hm_ref)
            kw, vw = k_ref[ws, :], v_ref[ws, :]
            p_w, p_c, ds_w, ds_c = head_terms(q4, do4, kw, vw, b_ref[off])
            dq_ref[rs, :] = _fold_heads(b_nn(ds_w, kw) + b_nn(ds_c, kc), hm_ref) * NA_SCALE
            dk_ref[ws, :] += b_tn(ds_w, q4)
            dv_ref[ws, :] += b_tn(p_w, do4)
            db_ref[off] += ds_w
            return dkc + b_tn(ds_c, q4), dvc + b_tn(p_c, do4)

        dkc, dvc = lax.fori_loop(0, rows, lat_step, carry)
        dk_ref[0:m_ctx, :] = dkc
        dv_ref[0:m_ctx, :] = dvc

    row = jax.ShapeDtypeStruct((t, 256), F32)
    return _call_with_exchange(body, name, [q, k, v, do, bias8, hm], [row, row, row, jax.ShapeDtypeStruct(bias8.shape, F32)], comm)


def _na_toeplitz():
    col = np.arange(GRID_W)
    dd = (col[None, :] - col[:, None] + 15).reshape(-1)
    tt = np.zeros((GRID_W * GRID_W, 128), np.float32)
    ok = (dd >= 0) & (dd <= 30)
    tt[np.arange(GRID_W * GRID_W)[ok], dd[ok]] = 1.0
    return tt


def _na_bias8(rpb, name):
    col = np.arange(GRID_W)
    cs = np.clip(col - 8, 0, GRID_W - 16)
    col_mask = (col[None, :] >= cs[:, None]) & (col[None, :] < cs[:, None] + 16)
    rpb2 = jnp.pad(rpb.reshape(60, 31), ((0, 4), (0, 97)))
    (toe,) = whole_fwd(lambda r_, t_: (hdot_nt(r_, t_),), name, [rpb2, jnp.asarray(_na_toeplitz())], [(64, GRID_W * GRID_W)])
    toe = toe[:60].reshape(4, 15, GRID_W, GRID_W)
    b = jnp.stack([toe[:, off:off + 8] for off in range(8)], axis=1)
    b = jnp.where(jnp.asarray(col_mask)[None, None, None], b, NEG)
    return b.transpose(1, 0, 3, 2, 4).reshape(8, 4 * GRID_W, 8 * GRID_W)


def _na_rpb_grad(dbias8, name):
    tt = _na_toeplitz()
    sel = np.zeros((64, 256), np.float32)
    for h in range(4):
        for off in range(8):
            for i in range(8):
                sel[h * 15 + off + i, h * 64 + off * 8 + i] = 1.0
    a2 = dbias8.reshape(8, 4, GRID_W, 8, GRID_W).transpose(1, 0, 3, 2, 4).reshape(256, GRID_W * GRID_W)
    (out,) = whole_fwd(lambda a, t_, s_: (hdot(s_, hdot(a, t_)),), name, [a2, jnp.asarray(tt), jnp.asarray(sel)], [(64, 128)])
    return out[:60, :31].reshape(4, 15, 31)


def f_mod(cs, b_mod, w_mod):
    s = _silu(cs)
    return bdot(s, w_mod) + b_mod, s


def loss_and_grad(z, tgt, n_ctx_rows, name, tile=ROW_TILE):
    t, d = z.shape
    tile = min(tile, n_ctx_rows)
    nct = n_ctx_rows // tile

    def body(z_ref, t_ref, dz_ref, loss_ref):
        i = pl.program_id(0)

        @pl.when(i == 0)
        def _():
            loss_ref[...] = jnp.zeros_like(loss_ref)

        @pl.when(i < nct)
        def _():
            dz_ref[...] = jnp.zeros_like(dz_ref)

        @pl.when(i >= nct)
        def _():
            diff = z_ref[...] - t_ref[...]
            dz_ref[...] = diff * (1.0 / d)
            loss_ref[...] += 0.5 * jnp.sum(jnp.sum(diff * diff, axis=-1, keepdims=True) * (1.0 / d), axis=0, keepdims=True)

    dz, loss = pl.pallas_call(
        body, name=name, grid=(t // tile,),
        in_specs=[pl.BlockSpec((tile, d), lambda i: (i, 0)),
                  pl.BlockSpec((tile, d), lambda i: (jnp.maximum(i - nct, 0), 0))],
        out_specs=[pl.BlockSpec((tile, d), lambda i: (i, 0)), pl.BlockSpec((8, 128), lambda i: (0, 0))],
        out_shape=[jax.ShapeDtypeStruct((t, d), F32), jax.ShapeDtypeStruct((8, 128), F32)],
        compiler_params=_cparams(dimension_semantics=("arbitrary",)),
    )(z, tgt)
    return loss[0, 0], dz


def adamw(parts, w, m, v, name, tile=256):
    npart, r, c = parts.shape
    tile = min(tile, r)
    assert r % tile == 0
    c1 = 1.0 / (1.0 - ADAM_B1 ** ADAM_STEP)
    c2 = 1.0 / (1.0 - ADAM_B2 ** ADAM_STEP)

    def body(p_ref, w_ref, m_ref, v_ref, g_ref, d_ref, nm_ref, nv_ref):
        g = p_ref[0].astype(F32)
        for i in range(1, npart):
            g = g + p_ref[i].astype(F32)
        nm = ADAM_B1 * m_ref[...] + (1.0 - ADAM_B1) * g
        nv = ADAM_B2 * v_ref[...] + (1.0 - ADAM_B2) * (g * g)
        g_ref[...] = g
        nm_ref[...] = nm
        nv_ref[...] = nv
        d_ref[...] = -ADAM_LR * ((nm * c1) / (jnp.sqrt(nv * c2) + ADAM_EPS) + ADAM_WD * w_ref[...])

    blk = pl.BlockSpec((tile, c), lambda i: (i, 0))
    return pl.pallas_call(
        body, name=name, grid=(r // tile,),
        in_specs=[pl.BlockSpec((npart, tile, c), lambda i: (0, i, 0)), blk, blk, blk],
        out_specs=[blk] * 4, out_shape=[jax.ShapeDtypeStruct((r, c), F32)] * 4,
        compiler_params=_cparams(dimension_semantics=("arbitrary",)),
    )(parts, w, m, v)


def _peer(x, y, c, k):
    return (1 - x if k & 4 else x, 1 - y if k & 2 else y, 1 - c if k & 1 else c)


def _exchange_out_shapes(arrays, scatter):
    return [jax.ShapeDtypeStruct(a.shape if s else (N_DEV,) + a.shape, a.dtype) for a, s in zip(arrays, scatter)]


def _exchange_sems(n):
    return [pltpu.SemaphoreType.DMA((n, N_DEV - 1)), pltpu.SemaphoreType.DMA((n, N_DEV - 1)), pltpu.SemaphoreType.DMA((n,))]


def _exchange_issue(ins, outs, scatter, send_sems, recv_sems, local_sems):
    n = len(ins)
    x, y, c = lax.axis_index("x"), lax.axis_index("y"), lax.axis_index("c")
    me = 4 * x + 2 * y + c

    def index_of(p):
        return 4 * p[0] + 2 * p[1] + p[2]

    local = []
    for a in range(n):
        src_me = ins[a].at[me] if scatter[a] else ins[a]
        loc = pltpu.make_async_copy(src_me, outs[a].at[me], local_sems.at[a])
        loc.start()
        local.append(loc)
    for k in range(1, N_DEV):
        peer = _peer(x, y, c, k)
        for a in range(n):
            src = ins[a].at[index_of(peer)] if scatter[a] else ins[a]
            pltpu.make_async_remote_copy(
                src_ref=src, dst_ref=outs[a].at[me], send_sem=send_sems.at[a, k - 1], recv_sem=recv_sems.at[a, k - 1],
                device_id=peer, device_id_type=pl.DeviceIdType.MESH).start()

    def finish():
        for k in range(1, N_DEV):
            peer = _peer(x, y, c, k)
            for a in range(n):
                src = ins[a].at[index_of(peer)] if scatter[a] else ins[a]
                cp = pltpu.make_async_remote_copy(
                    src_ref=src, dst_ref=outs[a].at[index_of(peer)], send_sem=send_sems.at[a, k - 1],
                    recv_sem=recv_sems.at[a, k - 1], device_id=peer, device_id_type=pl.DeviceIdType.MESH)
                cp.wait_send()
                cp.wait_recv()
        for loc in local:
            loc.wait()

    return finish


def exchange(arrays, scatter, name):
    n = len(arrays)

    def body(*refs):
        _exchange_issue(refs[:n], refs[n:2 * n], scatter, *refs[2 * n:])()

    hbm = pl.BlockSpec(memory_space=pl.ANY)
    return pl.pallas_call(
        body, name=name, in_specs=[hbm] * n, out_specs=[hbm] * n, out_shape=_exchange_out_shapes(arrays, scatter),
        scratch_shapes=_exchange_sems(n), compiler_params=pltpu.CompilerParams(has_side_effects=True),
    )(*arrays)


def _rope_tables(n_lat, n_ctx):
    tok = np.arange(n_lat)
    freqs = 10000.0 ** (-np.arange(0, 16, 2, dtype=np.float32) / 16.0)

    def table(pos):
        ang = pos.astype(np.float32)[:, None] * freqs[None, :]
        ang = np.concatenate([ang, ang], axis=-1)
        return np.cos(ang), np.sin(ang)

    cr, sr = table(tok // GRID_W)
    cc, sc = table(tok % GRID_W)
    cos = np.tile(np.concatenate([cr, cc], axis=-1), (1, 4))
    sin = np.tile(np.concatenate([sr, sc], axis=-1), (1, 4))
    cos = np.concatenate([np.ones((n_ctx, 128), np.float32), cos], axis=0)
    sin = np.concatenate([np.zeros((n_ctx, 128), np.float32), sin], axis=0)
    return jnp.asarray(cos, F32), jnp.asarray(sin, F32)


def _pad_w_in(w):
    z = lambda n: jnp.zeros((w.shape[0], n), w.dtype)
    return jnp.concatenate([w[:, 1824:2848], w[:, 128:384], w[:, 416:672], w[:, 672:928], w[:, 928:1184], w[:, 1312:1568],
                            w[:, 1568:1824], w[:, 0:128], w[:, 384:416], z(96), w[:, 1184:1312], z(128)], axis=1)


def _unpad_w_in(wp):
    return jnp.concatenate([wp[:, C_GK:C_GK + 128], wp[:, C_GV:C_GV + 256], wp[:, C_GG:C_GG + 32], wp[:, C_NK:C_NK + 256],
                            wp[:, C_NV:C_NV + 256], wp[:, C_SU:C_SU + 256], wp[:, C_GQ:C_GQ + 128], wp[:, C_NQ:C_NQ + 256],
                            wp[:, C_PU:C_PU + 256], wp[:, C_GT:C_GT + 1024]], axis=1)


def _pad_rows(u):
    return jnp.pad(u, ((POOL_HALO, POOL_HALO), (0, 0)))


def _block_diag4(w):
    out = jnp.zeros((256, 256), w.dtype)
    for i in range(4):
        out = lax.dynamic_update_slice(out, w[i], (64 * i, 64 * i))
    return out


def _layer_params(p, big, l):
    e_rep, e_tile, gmask, bdm = _s5_consts()
    wg = jnp.zeros((128, 256), F32)
    wg = lax.dynamic_update_slice(wg, p["gla_w_gate"][l, 0], (0, 0))
    wg = lax.dynamic_update_slice(wg, p["gla_w_gate"][l, 1], (16, 128))
    s5 = []
    for d in range(2):
        s5.append([p["s5_lam_re"][l, d], p["s5_lam_im"][l, d], p["s5_log_dt"][l, d].reshape(16, 1),
                   p["s5_b_re"][l, d].transpose(0, 2, 1).reshape(256, 64), p["s5_b_im"][l, d].transpose(0, 2, 1).reshape(256, 64),
                   p["s5_c_re"][l, d].reshape(256, 64), p["s5_c_im"][l, d].reshape(256, 64), e_rep, e_tile, gmask, bdm])
    havg = jnp.asarray((np.arange(256)[:, None] // 64 == np.arange(256)[None, :] // 64).astype(np.float32) / 64.0)
    e4 = jnp.asarray((np.arange(64)[:, None] == np.arange(256)[None, :] % 64).astype(np.float32))
    return dict(
        g_pre=p["g_pre"][l].reshape(1, D), g_post=p["g_post"][l].reshape(1, D), b_mod=p["b_mod"][l].reshape(1, 3 * D),
        w_mod=big["w_mod"], w_in=_pad_w_in(big["w_in"]), w_out=big["w_out"],
        wg=wg, bg=p["gla_b_gate"][l].reshape(1, 256), g_norm=jnp.pad(p["gla_g_norm"][l].reshape(1, 64), ((0, 7), (0, 0))),
        bias8=_na_bias8(p["na_rpb"][l], f"na_bias_l{l}"), s5=s5, s5_d=p["s5_d"][l].reshape(1, 256), w_glu=big["s5_w_glu"].astype(F32),
        b_glu=p["s5_b_glu"][l].reshape(1, 256), wpool=_block_diag4(p["pool_w"][l]), pool_scale=p["pool_scale"][l].reshape(1, 256),
        havg=havg, e4=e4)


def _cols(pz, start, width):
    return pz[:, start:start + width]


def _layer_fwd(z, modseg, lp, cos, sin, m_ctx, tile, s5_chunk, l, comm=None):
    t = z.shape[0]
    nct = m_ctx // tile
    nm = lambda s: f"{s}_l{l}"
    (h,) = rowwise_fwd(f_pre, nm("pre"), [z], [modseg], [lp["g_pre"]], [D], tile, nct)
    (gt,) = mm_nn_cols(h, lp["w_in"], C_GT, [1024], nm("in_proj_a"), tm=tile)
    pv, nk, nv, su = mm_nn_cols(h, lp["w_in"], C_GV, [256] * 4, nm("in_proj_b"), tm=tile)
    nq, pu, pk, pg, pq = mm_nn_cols(h, lp["w_in"], C_NQ, [256, 256, 128, 128, 128], nm("in_proj_c"), tm=tile)
    q_r, k_r, lgf, lgb = rowwise_fwd(f_gla_prep, nm("gla_prep"), [pk, pg, pq, cos, sin], [], [lp["wg"], lp["bg"]], [128] * 4, tile, nct)
    part = (lambda idx: None) if comm is None else (lambda idx: ([comm[i] for i in idx], [False] * len(idx)))
    (o1, st_f), got_mod = gla_scan_fwd(q_r, k_r, pv, lgf, jnp.zeros((t, 256), F32), m_ctx, False, nm("gla_f"), part([0]))
    (o_gla, st_b), got_out = gla_scan_fwd(q_r, k_r, pv, lgb, o1, m_ctx, True, nm("gla_r"), part([2, 3]))
    received = None
    if comm is None:
        o_na = na_fwd(nq, nk, nv, lp["bias8"], m_ctx, nm("na"))
    else:
        o_na, got_in = na_fwd(nq, nk, nv, lp["bias8"], m_ctx, nm("na"), part([1]))
        received = [got_mod[0], got_in[0], got_out[0], got_out[1]]
    s5p = [whole_fwd(f_s5_params, nm(f"s5_par{d}"), lp["s5"][d], [(1, 1024)] * 2 + [(256, 1024)] * 4) for d in range(2)]
    y1, x0r_f, x0i_f = s5_scan_fwd(su, jnp.zeros((t, 256), F32), *s5p[0], m_ctx, s5_chunk, False, nm("s5_f"))
    y5, x0r_b, x0i_b = s5_scan_fwd(su, y1, *s5p[1], m_ctx, s5_chunk, True, nm("s5_r"))
    pm = jnp.concatenate([pool_apply(_pad_rows(pu[:m_ctx]), m_ctx, False, nm("pool_c")),
                          pool_apply(_pad_rows(pu[m_ctx:]), t - m_ctx, False, nm("pool_x"))], axis=0)
    mix_rows = [o_gla, o_na, y5, su, pm, gt]
    mix_globs = [lp["g_norm"], lp["s5_d"], lp["w_glu"], lp["b_glu"], lp["wpool"], lp["pool_scale"], lp["havg"], lp["e4"]]
    (yg,) = rowwise_fwd(f_mix, nm("mix"), mix_rows, [], mix_globs, [D], tile, nct)
    out = mm_nn([yg], lp["w_out"], nm("out_proj"), tm=tile)
    (z_new,) = rowwise_fwd(f_post, nm("post"), [z, out], [modseg], [lp["g_post"]], [D], tile, nct)
    saved = dict(z=z, h=h, pv=pv, nk=nk, nv=nv, su=su, nq=nq, pk=pk, pg=pg, pq=pq, q_r=q_r, k_r=k_r, lgf=lgf, lgb=lgb,
                 st_f=st_f, st_b=st_b, s5p=s5p, x0f=(x0r_f, x0i_f), x0b=(x0r_b, x0i_b), mix_rows=mix_rows, mix_globs=mix_globs,
                 yg=yg, out=out)
    return z_new, saved, received


def _f_pre_res(x, mod, g_pre):
    return f_pre(x, mod, g_pre)[0], x


def _layer_bwd(dz_new, sv, modseg, lp, cos, sin, m_ctx, tile, s5_chunk, l, comm=None, gdt=F32):
    t = dz_new.shape[0]
    nct = m_ctx // tile
    nm = lambda s: f"{s}_l{l}"
    g = {}
    dz_res, dout, dmod_post, g["g_post"] = rowwise_bwd(f_post, nm("post_b"), [sv["z"], sv["out"]], [modseg], [lp["g_post"]],
                                                       [dz_new], tile, nct, [True, True], [True])
    dyg = mm_nt([dout], lp["w_out"], nm("out_proj_dx"), tm=tile)
    (g["w_out"],) = mm_tn(sv["yg"], [dout], nm("out_proj_dw"), tm=tile, out_dtype=gdt)
    res = rowwise_bwd(f_mix, nm("mix_b"), sv["mix_rows"], [], sv["mix_globs"], [dyg], tile, nct, [True] * 6, [True] * 6 + [False] * 2)
    do_gla, do_na, dy5, dsu_a, dpm, dgt = res[:6]
    g["g_norm"], g["s5_d"], g["w_glu"], g["b_glu"], g["wpool"], g["pool_scale"] = res[6:]
    dpu = jnp.concatenate([pool_apply(_pad_rows(dpm[:m_ctx]), m_ctx, True, nm("pool_c_b")),
                           pool_apply(_pad_rows(dpm[m_ctx:]), t - m_ctx, True, nm("pool_x_b"))], axis=0)
    r_b = s5_scan_bwd(sv["su"], dy5, dsu_a, *sv["x0b"], *sv["s5p"][1], m_ctx, s5_chunk, True, nm("s5_r_b"))
    r_f = s5_scan_bwd(sv["su"], dy5, r_b[0], *sv["x0f"], *sv["s5p"][0], m_ctx, s5_chunk, False, nm("s5_f_b"))
    dsu = r_f[0]
    g["s5"] = [whole_bwd(f_s5_params, nm(f"s5_par{d}_b"), lp["s5"][d], list(r[1:]), [True] * 7 + [False] * 4)
               for d, r in ((0, r_f), (1, r_b))]
    part = (lambda idx: None) if comm is None else (lambda idx: ([comm[i] for i in idx], [True] * len(idx)))
    (dnq, dnk, dnv, dbias8), got_in = na_bwd(sv["nq"], sv["nk"], sv["nv"], do_na, lp["bias8"], m_ctx, nm("na_b"), part([1]))
    g["rpb"] = _na_rpb_grad(dbias8, nm("na_rpb_b"))
    zq, zv = jnp.zeros((t, 128), F32), jnp.zeros((t, 256), F32)
    (dq1, dk1, dv1, dlgb), got_mod = gla_scan_bwd(sv["q_r"], sv["k_r"], sv["pv"], sv["lgb"], sv["st_b"], do_gla, (zq, zq, zv), m_ctx, True,
                                                  nm("gla_r_b"), part([0]))
    (dq_r, dk_r, dpv, dlgf), got_out = gla_scan_bwd(sv["q_r"], sv["k_r"], sv["pv"], sv["lgf"], sv["st_f"], do_gla, (dq1, dk1, dv1), m_ctx, False,
                                                    nm("gla_f_b"), part([2, 3]))
    received = None if comm is None else [got_mod[0], got_in[0], got_out[0], got_out[1]]
    dpk, dpg, dpq, g["wg"], g["bg"] = rowwise_bwd(f_gla_prep, nm("gla_prep_b"), [sv["pk"], sv["pg"], sv["pq"], cos, sin], [],
                                                  [lp["wg"], lp["bg"]], [dq_r, dk_r, dlgf, dlgb], tile, nct,
                                                  [True, True, True, False, False], [True, True])
    parts = [dgt, dpv, dnk, dnv, dsu, dnq, dpu, dpk, dpg, dpq, jnp.zeros((t, 128), F32)]
    dh = mm_nt(parts, lp["w_in"], nm("in_proj_dx"), tm=tile)
    g["w_in"] = _unpad_w_in(jnp.concatenate(mm_tn(sv["h"], parts, nm("in_proj_dw"), tm=tile, out_dtype=gdt), axis=1))
    dz, dmod_pre, g["g_pre"] = rowwise_bwd(_f_pre_res, nm("pre_b"), [sv["z"]], [modseg], [lp["g_pre"]], [dh, dz_res], tile, nct, [True], [True])
    return dz, dmod_pre, dmod_post, g, received


def _f_mod_sum(cs, b_mod, w_mod):
    mod, _ = f_mod(cs, b_mod, w_mod)
    return mod, cs


def local_step(x, c, ctx, tgt, p, shards=None, tile=ROW_TILE, s5_chunk=S5_CHUNK):
    n_lat, m_ctx = x.shape[0], ctx.shape[0]
    n_layers = p["g_pre"].shape[0]
    z = jnp.concatenate([ctx, x], axis=0)
    cos, sin = _rope_tables(n_lat, m_ctx)
    cs = jnp.concatenate([c.reshape(1, D), p["c_ctx"].reshape(1, D), jnp.zeros((6, D), F32)], axis=0)
    gather = [False] * len(_SHARDED)
    lps, mods, silus, saves = [], [], [], []
    got = exchange(shards[0], gather, "gather_weights_l0") if shards is not None else None
    for l in range(n_layers):
        if shards is None:
            big = {n: p[n][l] for n in _SHARDED}
        else:
            big = {n: _gathered(g, _BY_COLS[n]) for n, g in zip(_SHARDED, got)}
        lp = _layer_params(p, big, l)
        mod8, s8 = whole_fwd(f_mod, f"mod_l{l}", [cs, lp["b_mod"], lp["w_mod"]], [(8, 3 * D), (8, D)])
        modseg = mod8[:2].reshape(2, 1, 3 * D)
        comm = shards[l + 1] if shards is not None and l + 1 < n_layers else None
        z, sv, got = _layer_fwd(z, modseg, lp, cos, sin, m_ctx, tile, s5_chunk, l, comm)
        lps.append(lp); mods.append(modseg); silus.append(s8); saves.append(sv)
    loss, dz = loss_and_grad(z, tgt, m_ctx, "loss", tile)
    grads, received = [None] * n_layers, [None] * n_layers
    gdt = F32 if shards is None else BF16
    dcs = jnp.zeros((8, D), F32)
    pending = None
    for l in reversed(range(n_layers)):
        lp = lps[l]
        dz, dmod_pre, dmod_post, g, got = _layer_bwd(dz, saves[l], mods[l], lp, cos, sin, m_ctx, tile, s5_chunk, l, pending, gdt)
        if pending is not None:
            received[l + 1] = got
        dmod = jnp.concatenate([dmod_pre.reshape(2, 3 * D)[:, :2 * D], dmod_post.reshape(2, 3 * D)[:, 2 * D:]], axis=1)
        dmod8 = jnp.pad(dmod, ((0, 6), (0, 0)))
        dcs, g["b_mod"] = whole_bwd(_f_mod_sum, f"mod_b_l{l}", [cs, lp["b_mod"], lp["w_mod"]], [dmod8, dcs], [True, True, False])
        g["w_mod"] = jnp.concatenate(mm_tn(silus[l], [dmod8[:, :D], dmod8[:, D:2 * D], dmod8[:, 2 * D:]], f"mod_dw_l{l}", tm=8, out_dtype=gdt), axis=1)
        grads[l] = g
        if shards is not None:
            pending = _layer_sends(g)
    if shards is not None:
        received[0] = exchange(pending + [_small_sends(dcs[1], grads)], [True] * len(_SHARDED) + [False], "exchange_grads_l0")
    return loss, dz[m_ctx:], dcs[1], grads, received


_WEIGHTS = ["c_ctx", "w_mod", "b_mod", "g_pre", "g_post", "w_in", "w_out", "gla_w_gate", "gla_b_gate", "gla_g_norm", "na_rpb",
            "s5_lam_re", "s5_lam_im", "s5_log_dt", "s5_b_re", "s5_b_im", "s5_c_re", "s5_c_im", "s5_d", "s5_w_glu", "s5_b_glu",
            "pool_w", "pool_scale"]
_INPUTS = ["x", "c", "ctx"] + _WEIGHTS + ["loss_target"] + ["m_" + n for n in _WEIGHTS] + ["v_" + n for n in _WEIGHTS]
_SHARDED = ["w_mod", "w_in", "w_out", "s5_w_glu"]
_BY_COLS = {"w_mod": True, "w_in": True, "w_out": False, "s5_w_glu": False}
_SMALL = [n for n in _WEIGHTS if n not in _SHARDED]
_SMALL_PER_LAYER = [n for n in _SMALL if n != "c_ctx"]
_PACK_ROWS = 256


def _pack_plan(like):
    tiled = [i for i, a in enumerate(like) if a.size % 1024 == 0]
    loose = [i for i, a in enumerate(like) if a.size % 1024 != 0]
    tail = -(-sum(like[i].size for i in loose) // 1024) * 8
    rows = sum(like[i].size // 128 for i in tiled) + tail
    return tiled, loose, tail, -(-rows // _PACK_ROWS) * _PACK_ROWS - rows


def _pack(arrs):
    tiled, loose, tail, fill = _pack_plan(arrs)
    dt = arrs[0].dtype
    flat = jnp.concatenate([arrs[i].reshape(-1) for i in loose])
    flat = jnp.pad(flat, (0, tail * 128 - flat.shape[0])).reshape(tail, 128)
    return jnp.concatenate([arrs[i].reshape(-1, 128) for i in tiled] + [flat, jnp.zeros((fill, 128), dt)], axis=0)


def _unpack(packed, like):
    tiled, loose, tail, _ = _pack_plan(like)
    out, row = [None] * len(like), 0
    for i in tiled:
        n = like[i].size // 128
        out[i] = packed[row:row + n].reshape(like[i].shape)
        row += n
    flat, pos = packed[row:row + tail].reshape(-1), 0
    for i in loose:
        out[i] = flat[pos:pos + like[i].size].reshape(like[i].shape)
        pos += like[i].size
    return out


def _gathered(g, cols):
    if cols:
        return g.transpose(1, 0, 2).reshape(g.shape[1], N_DEV * g.shape[2])
    return g.reshape(N_DEV * g.shape[1], g.shape[2])


def _slabs(w, cols):
    r, c = w.shape
    if cols:
        return w.reshape(r, N_DEV, c // N_DEV).transpose(1, 0, 2)
    return w.reshape(N_DEV, r // N_DEV, c)


def _layer_small(g):
    s5 = lambda i, f: jnp.stack([f(g["s5"][d][i]) for d in range(2)])
    return {
        "b_mod": g["b_mod"].reshape(3 * D), "g_pre": g["g_pre"].reshape(D), "g_post": g["g_post"].reshape(D),
        "gla_w_gate": jnp.stack([g["wg"][0:16, 0:128], g["wg"][16:32, 128:256]]),
        "gla_b_gate": g["bg"].reshape(2, 128), "gla_g_norm": g["g_norm"][0], "na_rpb": g["rpb"],
        "s5_lam_re": s5(0, lambda a: a), "s5_lam_im": s5(1, lambda a: a), "s5_log_dt": s5(2, lambda a: a.reshape(16)),
        "s5_b_re": s5(3, lambda a: a.reshape(16, 16, 64).transpose(0, 2, 1)),
        "s5_b_im": s5(4, lambda a: a.reshape(16, 16, 64).transpose(0, 2, 1)),
        "s5_c_re": s5(5, lambda a: a.reshape(16, 16, 64)), "s5_c_im": s5(6, lambda a: a.reshape(16, 16, 64)),
        "s5_d": g["s5_d"].reshape(256), "s5_b_glu": g["b_glu"].reshape(256),
        "pool_w": jnp.stack([g["wpool"][64 * i:64 * i + 64, 64 * i:64 * i + 64] for i in range(4)]),
        "pool_scale": g["pool_scale"].reshape(256),
    }


def _layer_sends(g):
    big = {"w_mod": g["w_mod"], "w_in": g["w_in"], "w_out": g["w_out"], "s5_w_glu": g["w_glu"]}
    return [_slabs(big[n], _BY_COLS[n]).astype(BF16) for n in _SHARDED]


def _small_sends(d_c_ctx, grads):
    per_layer = [_layer_small(g) for g in grads]
    full = {n: jnp.stack([s[n] for s in per_layer]) for n in _SMALL_PER_LAYER}
    full["c_ctx"] = d_c_ctx
    return _pack([full[n] for n in _SMALL]).astype(BF16)


def kernel(x, c, ctx, c_ctx, w_mod, b_mod, g_pre, g_post, w_in, w_out, gla_w_gate, gla_b_gate, gla_g_norm, na_rpb, s5_lam_re, s5_lam_im, s5_log_dt, s5_b_re, s5_b_im, s5_c_re, s5_c_im, s5_d, s5_w_glu, s5_b_glu, pool_w, pool_scale, loss_target, m_c_ctx, m_w_mod, m_b_mod, m_g_pre, m_g_post, m_w_in, m_w_out, m_gla_w_gate, m_gla_b_gate, m_gla_g_norm, m_na_rpb, m_s5_lam_re, m_s5_lam_im, m_s5_log_dt, m_s5_b_re, m_s5_b_im, m_s5_c_re, m_s5_c_im, m_s5_d, m_s5_w_glu, m_s5_b_glu, m_pool_w, m_pool_scale, v_c_ctx, v_w_mod, v_b_mod, v_g_pre, v_g_post, v_w_in, v_w_out, v_gla_w_gate, v_gla_b_gate, v_gla_g_norm, v_na_rpb, v_s5_lam_re, v_s5_lam_im, v_s5_log_dt, v_s5_b_re, v_s5_b_im, v_s5_c_re, v_s5_c_im, v_s5_d, v_s5_w_glu, v_s5_b_glu, v_pool_w, v_pool_scale):
    given = dict(zip(_INPUTS, (x, c, ctx, c_ctx, w_mod, b_mod, g_pre, g_post, w_in, w_out, gla_w_gate, gla_b_gate, gla_g_norm, na_rpb, s5_lam_re, s5_lam_im, s5_log_dt, s5_b_re, s5_b_im, s5_c_re, s5_c_im, s5_d, s5_w_glu, s5_b_glu, pool_w, pool_scale, loss_target, m_c_ctx, m_w_mod, m_b_mod, m_g_pre, m_g_post, m_w_in, m_w_out, m_gla_w_gate, m_gla_b_gate, m_gla_g_norm, m_na_rpb, m_s5_lam_re, m_s5_lam_im, m_s5_log_dt, m_s5_b_re, m_s5_b_im, m_s5_c_re, m_s5_c_im, m_s5_d, m_s5_w_glu, m_s5_b_glu, m_pool_w, m_pool_scale, v_c_ctx, v_w_mod, v_b_mod, v_g_pre, v_g_post, v_w_in, v_w_out, v_gla_w_gate, v_gla_b_gate, v_gla_g_norm, v_na_rpb, v_s5_lam_re, v_s5_lam_im, v_s5_log_dt, v_s5_b_re, v_s5_b_im, v_s5_c_re, v_s5_c_im, v_s5_d, v_s5_w_glu, v_s5_b_glu, v_pool_w, v_pool_scale)))
    n_layers = w_in.shape[0]
    shards = [[given[n][l].astype(BF16) for n in _SHARDED] for l in range(n_layers)]
    p = {n: given[n] for n in _SMALL}
    loss, grad_x, _, _, received = local_step(x[0], c, ctx[0], loss_target[0], p, shards)
    final = {}
    for n in _SHARDED:
        per_layer = [adamw(received[l][_SHARDED.index(n)], given[n][l], given["m_" + n][l], given["v_" + n][l], f"adamw_{n}_l{l}")
                     for l in range(n_layers)]
        final[n] = [jnp.stack([res[kind] for res in per_layer]) for kind in range(4)]
    like = [given[n] for n in _SMALL]
    res = adamw(received[0][-1], _pack(like), _pack([given["m_" + n] for n in _SMALL]), _pack([given["v_" + n] for n in _SMALL]),
                "adamw_small")
    unpacked = [_unpack(packed, like) for packed in res]
    for i, n in enumerate(_SMALL):
        final[n] = [unpacked[kind][i] for kind in range(4)]
    loss = lax.psum(loss, ("x", "y", "c"))
    return (loss, grad_x[None], *[final[n][0] for n in _WEIGHTS], *[final[n][1] for n in _WEIGHTS],
            *[final[n][2] for n in _WEIGHTS], *[final[n][3] for n in _WEIGHTS])
```

```python
import functools
import math

import numpy as np
import jax
import jax.numpy as jnp
from jax import lax
from jax.experimental import pallas as pl
from jax.experimental.pallas import tpu as pltpu

F32 = jnp.float32
BF16 = jnp.bfloat16
HIGHEST = lax.Precision.HIGHEST
HIGH = lax.Precision.HIGH

D = 1024
GRID_W = 64
EPS = 1e-6
N_DEV = 8
C_GT, C_GV, C_NK, C_NV, C_SU, C_NQ, C_PU, C_GK, C_GG, C_GQ, C_END = 0, 1024, 1280, 1536, 1792, 2048, 2304, 2560, 2688, 2816, 2944
PW = 3072
N_CTX_ORIG = 416
N_IN = 2848
GLA_CHUNK = 128
S5_CHUNK = 256
ROW_TILE = 256
VMEM_LIMIT = 56 * 1024 * 1024

ADAM_LR, ADAM_B1, ADAM_B2, ADAM_EPS, ADAM_WD, ADAM_STEP = 0.001, 0.9, 0.999, 1e-08, 0.01, 10


def _cparams(**kw):
    return pltpu.CompilerParams(vmem_limit_bytes=VMEM_LIMIT, **kw)


def _dg(a, b, ca, cb, precision=None):
    return lax.dot_general(a, b, (((ca,), (cb,)), ((), ())), precision=precision, preferred_element_type=F32)


def hdot(a, b):
    return _dg(a, b, 1, 0, HIGHEST)


def hdot_nt(a, b):
    return _dg(a, b, 1, 1, HIGHEST)


def hdot_tn(a, b):
    return _dg(a, b, 0, 0, HIGHEST)


def mdot(a, b):
    return _dg(a, b, 1, 0, HIGH)


def mdot_nt(a, b):
    return _dg(a, b, 1, 1, HIGH)


def mdot_tn(a, b):
    return _dg(a, b, 0, 0, HIGH)


def b_nn(a, b):
    return _dg(a.astype(BF16), b.astype(BF16), 1, 0)


def b_nt(a, b):
    return _dg(a.astype(BF16), b.astype(BF16), 1, 1)


def b_tn(a, b):
    return _dg(a.astype(BF16), b.astype(BF16), 0, 0)


@jax.custom_vjp
def bdot(a, b):
    return b_nn(a, b)


def _bdot_fwd(a, b):
    return b_nn(a, b), (a, b)


def _bdot_bwd(res, ct):
    a, b = res
    return b_nt(ct, b).astype(a.dtype), b_tn(a, ct).astype(b.dtype)


bdot.defvjp(_bdot_fwd, _bdot_bwd)


def _log_sigmoid(z):
    return jnp.minimum(z, 0.0) - jnp.log(1.0 + jnp.exp(-jnp.abs(z)))


def _silu(z):
    return z * jax.nn.sigmoid(z)


def _gelu(z):
    return 0.5 * z * (1.0 + jnp.tanh(math.sqrt(2.0 / math.pi) * (z + 0.044715 * (z * z * z))))


def _cat(vals):
    return vals[0] if len(vals) == 1 else jnp.concatenate(vals, axis=-1)


def mm_nn(a_parts, b, name, tm=ROW_TILE, tn=1024):
    t = a_parts[0].shape[0]
    k, n = b.shape
    na = len(a_parts)
    tn = min(tn, n)

    def body(*refs):
        a = _cat([r[...].astype(BF16) for r in refs[:na]])
        refs[na + 1][...] = _dg(a, refs[na][...].astype(BF16), 1, 0)

    return pl.pallas_call(
        body, name=name, grid=(n // tn, t // tm),
        in_specs=[pl.BlockSpec((tm, p.shape[1]), lambda j, i: (i, 0)) for p in a_parts]
        + [pl.BlockSpec((k, tn), lambda j, i: (0, j))],
        out_specs=pl.BlockSpec((tm, tn), lambda j, i: (i, j)),
        out_shape=jax.ShapeDtypeStruct((t, n), F32),
        compiler_params=_cparams(dimension_semantics=("arbitrary", "arbitrary")),
    )(*a_parts, b)


def mm_nn_cols(a, b, start, widths, name, tm=ROW_TILE):
    t, k = a.shape
    tn = 1024
    assert start % tn == 0 and sum(widths) <= tn

    def body(a_ref, b_ref, *o_refs):
        r = _dg(a_ref[...].astype(BF16), b_ref[...].astype(BF16), 1, 0)
        off = 0
        for o_ref, w in zip(o_refs, widths):
            o_ref[...] = r[:, off:off + w]
            off += w

    return pl.pallas_call(
        body, name=name, grid=(t // tm,),
        in_specs=[pl.BlockSpec((tm, k), lambda i: (i, 0)), pl.BlockSpec((k, tn), lambda i: (0, start // tn))],
        out_specs=[pl.BlockSpec((tm, w), lambda i: (i, 0)) for w in widths],
        out_shape=[jax.ShapeDtypeStruct((t, w), F32) for w in widths],
        compiler_params=_cparams(dimension_semantics=("arbitrary",)),
    )(a, b)


def mm_nt(a_parts, b, name, tm=ROW_TILE):
    t = a_parts[0].shape[0]
    n, k = b.shape
    na = len(a_parts)

    def body(*refs):
        a = _cat([r[...].astype(BF16) for r in refs[:na]])
        refs[na + 1][...] = _dg(a, refs[na][...].astype(BF16), 1, 1)

    return pl.pallas_call(
        body, name=name, grid=(t // tm,),
        in_specs=[pl.BlockSpec((tm, p.shape[1]), lambda i: (i, 0)) for p in a_parts]
        + [pl.BlockSpec((n, k), lambda i: (0, 0))],
        out_specs=pl.BlockSpec((tm, n), lambda i: (i, 0)),
        out_shape=jax.ShapeDtypeStruct((t, n), F32),
        compiler_params=_cparams(dimension_semantics=("arbitrary",)),
    )(*a_parts, b)


def mm_tn(a, b_parts, name, tm=ROW_TILE, tn=1024, out_dtype=F32):
    t, k = a.shape
    widths = [p.shape[1] for p in b_parts]
    n = sum(widths)
    assert n % tn == 0
    groups, cur, acc = [], [], 0
    for idx, w in enumerate(widths):
        cur.append(idx)
        acc += w
        if acc == tn:
            groups.append(cur)
            cur, acc = [], 0
        assert acc < tn
    assert not cur
    outs = []
    for gi, grp in enumerate(groups):
        parts = [b_parts[i] for i in grp]
        npart = len(parts)
        nsteps = t // tm

        def body(*refs, npart=npart, nsteps=nsteps):
            a_v = refs[0][...].astype(BF16)
            b_v = _cat([r[...].astype(BF16) for r in refs[1:1 + npart]])
            o_ref, acc_ref = refs[1 + npart], refs[2 + npart]
            r = _dg(a_v, b_v, 0, 0)

            @pl.when(pl.program_id(0) == 0)
            def _():
                acc_ref[...] = r

            @pl.when(pl.program_id(0) != 0)
            def _():
                acc_ref[...] += r

            @pl.when(pl.program_id(0) == nsteps - 1)
            def _():
                o_ref[...] = acc_ref[...].astype(o_ref.dtype)

        outs.append(pl.pallas_call(
            body, name=f"{name}_{gi}", grid=(nsteps,),
            in_specs=[pl.BlockSpec((tm, k), lambda i: (i, 0))]
            + [pl.BlockSpec((tm, p.shape[1]), lambda i: (i, 0)) for p in parts],
            out_specs=pl.BlockSpec((k, tn), lambda i: (0, 0)),
            out_shape=jax.ShapeDtypeStruct((k, tn), out_dtype),
            scratch_shapes=[pltpu.VMEM((k, tn), F32)],
            compiler_params=_cparams(dimension_semantics=("arbitrary",)),
        )(a, *parts))
    return outs


def _seg_of(i, nct):
    return jnp.where(i < nct, 1, 0)


def rowwise_fwd(fn, name, rows, segs, globs, out_widths, tile, nct):
    t = rows[0].shape[0]
    nr, ns, ng = len(rows), len(segs), len(globs)

    def body(*refs):
        vals = [r[...] for r in refs[:nr]] + [r[0] for r in refs[nr:nr + ns]] + [r[...] for r in refs[nr + ns:nr + ns + ng]]
        outs = fn(*vals)
        for o_ref, o in zip(refs[nr + ns + ng:], outs):
            o_ref[...] = o

    return pl.pallas_call(
        body, name=name, grid=(t // tile,),
        in_specs=[pl.BlockSpec((tile, r.shape[1]), lambda i: (i, 0)) for r in rows]
        + [pl.BlockSpec((1, 1, s.shape[2]), lambda i: (_seg_of(i, nct), 0, 0)) for s in segs]
        + [pl.BlockSpec(g.shape, lambda i: (0, 0)) for g in globs],
        out_specs=[pl.BlockSpec((tile, w), lambda i: (i, 0)) for w in out_widths],
        out_shape=[jax.ShapeDtypeStruct((t, w), F32) for w in out_widths],
        compiler_params=_cparams(dimension_semantics=("arbitrary",)),
    )(*rows, *segs, *globs)


def rowwise_bwd(fn, name, rows, segs, globs, cts, tile, nct, row_diff, glob_diff):
    t = rows[0].shape[0]
    nr, ns, ng, nc = len(rows), len(segs), len(globs), len(cts)
    d_rows = [i for i in range(nr) if row_diff[i]]
    d_globs = [i for i in range(ng) if glob_diff[i]]

    def body(*refs):
        in_refs, out_refs = refs[:nr + ns + ng + nc], refs[nr + ns + ng + nc:]
        row_v = [r[...] for r in in_refs[:nr]]
        seg_v = [r[0] for r in in_refs[nr:nr + ns]]
        glob_v = [r[...] for r in in_refs[nr + ns:nr + ns + ng]]
        ct_v = tuple(r[...] for r in in_refs[nr + ns + ng:])

        def wrapped(dr, sv, dg):
            rv = list(row_v)
            for j, i in enumerate(d_rows):
                rv[i] = dr[j]
            gv = list(glob_v)
            for j, i in enumerate(d_globs):
                gv[i] = dg[j]
            return tuple(fn(*rv, *sv, *gv))

        _, vjp = jax.vjp(wrapped, [row_v[i] for i in d_rows], seg_v, [glob_v[i] for i in d_globs])
        c_rows, c_segs, c_globs = vjp(ct_v)
        i = pl.program_id(0)
        k = 0
        for c in c_rows:
            out_refs[k][...] = c
            k += 1
        seg_first = jnp.logical_or(i == 0, i == nct)
        for c in c_segs:
            ref = out_refs[k]
            k += 1

            @pl.when(seg_first)
            def _(ref=ref, c=c):
                ref[0] = c

            @pl.when(jnp.logical_not(seg_first))
            def _(ref=ref, c=c):
                ref[0] += c
        for c in c_globs:
            ref = out_refs[k]
            k += 1

            @pl.when(i == 0)
            def _(ref=ref, c=c):
                ref[...] = c

            @pl.when(i != 0)
            def _(ref=ref, c=c):
                ref[...] += c

    return pl.pallas_call(
        body, name=name, grid=(t // tile,),
        in_specs=[pl.BlockSpec((tile, r.shape[1]), lambda i: (i, 0)) for r in rows]
        + [pl.BlockSpec((1, 1, s.shape[2]), lambda i: (_seg_of(i, nct), 0, 0)) for s in segs]
        + [pl.BlockSpec(g.shape, lambda i: (0, 0)) for g in globs]
        + [pl.BlockSpec((tile, c.shape[1]), lambda i: (i, 0)) for c in cts],
        out_specs=[pl.BlockSpec((tile, rows[i].shape[1]), lambda i: (i, 0)) for i in d_rows]
        + [pl.BlockSpec((1, 1, s.shape[2]), lambda i: (_seg_of(i, nct), 0, 0)) for s in segs]
        + [pl.BlockSpec(globs[i].shape, lambda i: (0, 0)) for i in d_globs],
        out_shape=[jax.ShapeDtypeStruct(rows[i].shape, F32) for i in d_rows]
        + [jax.ShapeDtypeStruct(s.shape, F32) for s in segs]
        + [jax.ShapeDtypeStruct(globs[i].shape, F32) for i in d_globs],
        compiler_params=_cparams(dimension_semantics=("arbitrary",)),
    )(*rows, *segs, *globs, *cts)


def f_pre(x, mod, g_pre):
    shift, scale = mod[:, :D], mod[:, D:2 * D]
    rs = lax.rsqrt(jnp.mean(x * x, axis=-1, keepdims=True) + EPS)
    return ((x * rs) * g_pre * (1.0 + scale) + shift,)


def f_post(x, out, mod, g_post):
    gate = mod[:, 2 * D:]
    rs = lax.rsqrt(jnp.mean(out * out, axis=-1, keepdims=True) + EPS)
    return (x + gate * ((out * rs) * g_post),)


def f_mix(o_gla, o_na, y5, u5, pm, gcols, g_norm, s5_d, w_glu, b_glu, wpool, pool_scale, havg, e4):
    ms = hdot(o_gla * o_gla, havg)
    y_gla = o_gla * lax.rsqrt(ms + EPS) * jnp.sum(hdot(g_norm, e4), axis=0, keepdims=True)
    g = _gelu(u5 * s5_d + y5)
    y_s5 = g * jax.nn.sigmoid(bdot(g, w_glu) + b_glu)
    y_pool = bdot(pm, wpool) * pool_scale
    ycat = jnp.concatenate([y_gla, o_na, y_s5, y_pool], axis=-1)
    return (ycat * _silu(gcols),)


@jax.custom_vjp
def _rot_half16(x):
    lane = lax.broadcasted_iota(jnp.int32, x.shape, 1)
    first = jnp.bitwise_and(lane, 15) < 8
    return jnp.where(first, -pltpu.roll(x, x.shape[1] - 8, 1), pltpu.roll(x, 8, 1))


def _rot_fwd(x):
    return _rot_half16(x), None


def _rot_bwd(_, ct):
    return (-_rot_half16(ct),)


_rot_half16.defvjp(_rot_fwd, _rot_bwd)


def f_gla_prep(pk, pg, pq, cos, sin, wg, bg):
    z = bdot(pg, wg) + bg
    lg = _log_sigmoid(z) * (1.0 / 16.0)
    k_r = pk * cos + _rot_half16(pk) * sin
    q_r = (pq * cos + _rot_half16(pq) * sin) * (32.0 ** -0.5)
    return q_r, k_r, lg[:, :128], lg[:, 128:]


def _gla_consts(rev):
    c = GLA_CHUNK
    i = np.arange(c)
    inc = (i[None, :] >= i[:, None]) if rev else (i[None, :] <= i[:, None])
    mq = np.stack([(np.arange(128) // 32 == h) for h in range(4)]).astype(np.float32).reshape(4, 1, 128)
    mv = np.stack([(np.arange(256) // 64 == h) for h in range(4)]).astype(np.float32).reshape(4, 1, 256)
    bdt = (np.arange(256)[:, None] // 64 == np.arange(128)[None, :] // 32).astype(np.float32)
    inc = inc.astype(np.float32)
    return jnp.asarray(inc), jnp.asarray(inc.T.copy()), jnp.asarray(mq), jnp.asarray(mv), jnp.asarray(bdt)


def _stack_heads(x, m_ref):
    return jnp.concatenate([x * m_ref[h] for h in range(4)], axis=0)


def _tile4(m):
    return jnp.concatenate([m, m, m, m], axis=0)


def _fold_heads(r4, m_ref):
    r = r4.shape[0] // 4
    out = m_ref[0] * r4[0:r]
    for h in range(1, 4):
        out = out + m_ref[h] * r4[h * r:(h + 1) * r]
    return out


def _gla_chunk_of(s, n_ctx_chunks, n_chunks, rev):
    if not rev:
        return s
    return jnp.where(s < n_ctx_chunks, n_ctx_chunks - 1 - s, n_ctx_chunks + n_chunks - 1 - s)


def gla_scan_fwd(q, k, v, lg, acc, n_ctx_rows, rev, name, comm=None):
    t = q.shape[0]
    nch, ncc = t // GLA_CHUNK, n_ctx_rows // GLA_CHUNK
    inc, inc_t, mq, mv, bdt = _gla_consts(rev)

    def body(q_ref, k_ref, v_ref, lg_ref, acc_ref, inc_ref, inct_ref, mq_ref, mv_ref, bdt_ref, o_ref, st_ref):
        lmask, lmask_t = inc_ref[...], inct_ref[...]
        bd = bdt_ref[...]

        def step(s, st):
            c = _gla_chunk_of(s, ncc, nch, rev)
            rows = pl.ds(pl.multiple_of(c * GLA_CHUNK, GLA_CHUNK), GLA_CHUNK)
            qc, kc, vc, lgc = q_ref[rows, :], k_ref[rows, :], v_ref[rows, :], lg_ref[rows, :]
            st_ref[c] = st
            b = hdot(lmask, lgc)
            blast = jnp.sum(lgc, axis=0, keepdims=True)
            qe, ke, kd = qc * jnp.exp(b), kc * jnp.exp(-b), kc * jnp.exp(blast - b)
            ke4, v4 = _stack_heads(ke, mq_ref), _stack_heads(vc, mv_ref)
            at = _tile4(lmask_t) * b_nt(ke4, qe)
            o_ref[rows, :] = acc_ref[rows, :] + b_nt(qe, st) + b_tn(at, v4)
            return st * jnp.exp(blast) + bd * mdot_tn(vc, kd)

        lax.fori_loop(0, nch, step, jnp.zeros((256, 128), F32))

    return _call_with_exchange(body, name, [q, k, v, lg, acc, inc, inc_t, mq, mv, bdt],
                               [jax.ShapeDtypeStruct((t, 256), F32), jax.ShapeDtypeStruct((nch, 256, 128), F32)], comm)


def gla_scan_bwd(q, k, v, lg, st, do, acc, n_ctx_rows, rev, name, comm=None):
    t = q.shape[0]
    nch, ncc = t // GLA_CHUNK, n_ctx_rows // GLA_CHUNK
    inc, inc_t, mq, mv, bdt = _gla_consts(rev)

    def body(q_ref, k_ref, v_ref, lg_ref, st_ref, do_ref, aq_ref, ak_ref, av_ref, inc_ref, inct_ref, mq_ref, mv_ref, bdt_ref,
             dq_ref, dk_ref, dv_ref, dlg_ref):
        lmask, lmask_t = inc_ref[...], inct_ref[...]
        bd = bdt_ref[...]

        def step(j, carry):
            dst, gsum = carry
            s = nch - 1 - j
            c = _gla_chunk_of(s, ncc, nch, rev)
            rows = pl.ds(pl.multiple_of(c * GLA_CHUNK, GLA_CHUNK), GLA_CHUNK)
            qc, kc, vc, lgc, doc = q_ref[rows, :], k_ref[rows, :], v_ref[rows, :], lg_ref[rows, :], do_ref[rows, :]
            stc = st_ref[c]
            b = hdot(lmask, lgc)
            blast = jnp.sum(lgc, axis=0, keepdims=True)
            eb, enb, edb = jnp.exp(b), jnp.exp(-b), jnp.exp(blast - b)
            qe, ke, kd = qc * eb, kc * enb, kc * edb
            ke4, v4 = _stack_heads(ke, mq_ref), _stack_heads(vc, mv_ref)
            lm4 = _tile4(lmask_t)
            at = lm4 * b_nt(ke4, qe)
            dat = lm4 * mdot_nt(v4, doc)
            dqe = mdot(doc, stc) + mdot_tn(dat, ke4)
            dke = _fold_heads(mdot(dat, qe), mq_ref)
            dv = b_nt(kd, dst) + _fold_heads(b_nn(at, doc), mv_ref)
            dkd = mdot(vc, dst)
            dq = dqe * eb
            dk = dke * enb + dkd * edb
            g = qc * dq - kc * dk
            dlg_ref[rows, :] = hdot_tn(lmask, g) + gsum
            dq_ref[rows, :] = aq_ref[rows, :] + dq
            dk_ref[rows, :] = ak_ref[rows, :] + dk
            dv_ref[rows, :] = av_ref[rows, :] + dv
            dst_new = dst * jnp.exp(blast) + bd * mdot_tn(doc, qe)
            return dst_new, gsum + jnp.sum(g, axis=0, keepdims=True)

        lax.fori_loop(0, nch, step, (jnp.zeros((256, 128), F32), jnp.zeros((1, 128), F32)))

    return _call_with_exchange(body, name, [q, k, v, lg, st, do, *acc, inc, inc_t, mq, mv, bdt],
                               [jax.ShapeDtypeStruct((t, 128), F32), jax.ShapeDtypeStruct((t, 128), F32),
                                jax.ShapeDtypeStruct((t, 256), F32), jax.ShapeDtypeStruct((t, 128), F32)], comm)


def whole_fwd(fn, name, args, out_shapes):
    def body(*refs):
        outs = fn(*[r[...] for r in refs[:len(args)]])
        for o_ref, o in zip(refs[len(args):], outs):
            o_ref[...] = o

    vm = pl.BlockSpec(memory_space=pltpu.VMEM)
    return pl.pallas_call(
        body, name=name, in_specs=[vm] * len(args), out_specs=[vm] * len(out_shapes),
        out_shape=[jax.ShapeDtypeStruct(s, F32) for s in out_shapes], compiler_params=_cparams(),
    )(*args)


def whole_bwd(fn, name, args, cts, diff):
    d_idx = [i for i in range(len(args)) if diff[i]]

    def body(*refs):
        vals = [r[...] for r in refs[:len(args)]]
        ct_v = tuple(r[...] for r in refs[len(args):len(args) + len(cts)])

        def wrapped(dv):
            av = list(vals)
            for j, i in enumerate(d_idx):
                av[i] = dv[j]
            return tuple(fn(*av))

        _, vjp = jax.vjp(wrapped, [vals[i] for i in d_idx])
        (c_args,) = vjp(ct_v)
        for o_ref, c in zip(refs[len(args) + len(cts):], c_args):
            o_ref[...] = c

    vm = pl.BlockSpec(memory_space=pltpu.VMEM)
    return pl.pallas_call(
        body, name=name, in_specs=[vm] * (len(args) + len(cts)), out_specs=[vm] * len(d_idx),
        out_shape=[jax.ShapeDtypeStruct(args[i].shape, F32) for i in d_idx], compiler_params=_cparams(),
    )(*args, *cts)


def _s5_consts():
    e_rep = (np.arange(256)[:, None] // 16 == np.arange(16)[None, :]).astype(np.float32)
    e_tile = (np.arange(64)[:, None] == np.arange(1024)[None, :] % 64).astype(np.float32)
    gmask = (np.arange(16)[:, None] == np.arange(1024)[None, :] // 64).astype(np.float32)
    bdm = (np.arange(256)[:, None] // 16 == np.arange(1024)[None, :] // 64).astype(np.float32)
    return jnp.asarray(e_rep), jnp.asarray(e_tile), jnp.asarray(gmask), jnp.asarray(bdm)


def f_s5_params(lam_re, lam_im, log_dt, bt_re, bt_im, ct_re, ct_im, e_rep, e_tile, gmask, bdm):
    dt = jnp.exp(log_dt)
    mag = jnp.exp(lam_re * dt)
    ang = lam_im * dt
    lb_re, lb_im = mag * jnp.cos(ang), mag * jnp.sin(ang)
    num_re, num_im = lb_re - 1.0, lb_im
    den = lam_re * lam_re + lam_im * lam_im
    coef_re = (num_re * lam_re + num_im * lam_im) / den
    coef_im = (num_im * lam_re - num_re * lam_im) / den
    cr, ci = hdot(e_rep, coef_re), hdot(e_rep, coef_im)
    bbt_re = cr * bt_re - ci * bt_im
    bbt_im = cr * bt_im + ci * bt_re
    a_re = jnp.sum(hdot(lb_re, e_tile) * gmask, axis=0, keepdims=True)
    a_im = jnp.sum(hdot(lb_im, e_tile) * gmask, axis=0, keepdims=True)
    return (a_re, a_im, hdot(bbt_re, e_tile) * bdm, hdot(bbt_im, e_tile) * bdm,
            hdot(ct_re, e_tile) * bdm, hdot(ct_im, e_tile) * bdm)


def _s5_doubling(xr, xi, pr, pi, pos, n, steps, rev):
    rows = xr.shape[0]
    for s in steps:
        if rev:
            keep = pos < (n - s)
            sr, si = pltpu.roll(xr, rows - s, 0), pltpu.roll(xi, rows - s, 0)
        else:
            keep = pos >= s
            sr, si = pltpu.roll(xr, s, 0), pltpu.roll(xi, s, 0)
        sr, si = jnp.where(keep, sr, 0.0), jnp.where(keep, si, 0.0)
        xr, xi = xr + pr * sr - pi * si, xi + pr * si + pi * sr
        pr, pi = pr * pr - pi * pi, 2.0 * pr * pi
    return xr, xi, pr, pi


SUBLANES = 8


def _s5_scan(xr, xi, a_re, a_im, rev, chunk, scr):
    xs_r, xs_i, yp_r, yp_i = scr
    ng = chunk // SUBLANES
    x3r, x3i = xr.reshape(ng, SUBLANES, 1024), xi.reshape(ng, SUBLANES, 1024)
    sub = lax.broadcasted_iota(jnp.int32, (SUBLANES, 1024), 0)
    a8r, a8i = a_re, a_im
    for s in (1, 2, 4):
        keep = sub < (SUBLANES - s) if rev else sub >= s
        mr, mi = jnp.where(keep, a8r, 0.0)[None], jnp.where(keep, a8i, 0.0)[None]
        shift = SUBLANES - s if rev else s
        sr, si = pltpu.roll(x3r, shift, 1), pltpu.roll(x3i, shift, 1)
        x3r, x3i = x3r + mr * sr - mi * si, x3i + mr * si + mi * sr
        a8r, a8i = a8r * a8r - a8i * a8i, 2.0 * a8r * a8i
    xr, xi = x3r.reshape(chunk, 1024), x3i.reshape(chunk, 1024)
    nblk = 1024 // 128
    for j in range(nblk):
        xs_r[j] = xr[:, 128 * j:128 * (j + 1)]
        xs_i[j] = xi[:, 128 * j:128 * (j + 1)]
    edge = pl.ds(0 if rev else SUBLANES - 1, ng, stride=SUBLANES)
    gr = jnp.concatenate([xs_r[j, edge, :] for j in range(nblk)], axis=-1)
    gi = jnp.concatenate([xs_i[j, edge, :] for j in range(nblk)], axis=-1)
    grow = lax.broadcasted_iota(jnp.int32, (ng, 1024), 0)
    steps = tuple(1 << k for k in range((ng - 1).bit_length()))
    gr, gi, _, _ = _s5_doubling(gr, gi, a8r, a8i, grow, ng, steps, rev)
    if rev:
        yp_r[...] = jnp.where(grow < ng - 1, pltpu.roll(gr, ng - 1, 0), 0.0)
        yp_i[...] = jnp.where(grow < ng - 1, pltpu.roll(gi, ng - 1, 0), 0.0)
    else:
        yp_r[...] = jnp.where(grow >= 1, pltpu.roll(gr, 1, 0), 0.0)
        yp_i[...] = jnp.where(grow >= 1, pltpu.roll(gi, 1, 0), 0.0)
    sub = lax.broadcasted_iota(jnp.int32, (SUBLANES, 1024), 0)
    tr, ti = jnp.zeros((SUBLANES, 1024), F32), jnp.zeros((SUBLANES, 1024), F32)
    cr, ci = a_re, a_im
    for n in range(1, SUBLANES + 1):
        r = SUBLANES - n if rev else n - 1
        tr, ti = jnp.where(sub == r, cr, tr), jnp.where(sub == r, ci, ti)
        cr, ci = cr * a_re - ci * a_im, cr * a_im + ci * a_re
    for j in range(nblk):
        lanes = slice(128 * j, 128 * (j + 1))
        tr_j, ti_j = tr[:, lanes], ti[:, lanes]
        for g in range(ng):
            rows = slice(g * SUBLANES, (g + 1) * SUBLANES)
            er, ei = yp_r[g:g + 1, lanes], yp_i[g:g + 1, lanes]
            xs_r[j, rows, :] = xs_r[j, rows, :] + tr_j * er - ti_j * ei
            xs_i[j, rows, :] = xs_i[j, rows, :] + tr_j * ei + ti_j * er
    return (jnp.concatenate([xs_r[j] for j in range(nblk)], axis=-1),
            jnp.concatenate([xs_i[j] for j in range(nblk)], axis=-1))


def _s5_scratch(chunk):
    return [pltpu.VMEM((8, chunk, 128), F32), pltpu.VMEM((8, chunk, 128), F32),
            pltpu.VMEM((chunk // SUBLANES, 1024), F32), pltpu.VMEM((chunk // SUBLANES, 1024), F32)]


def _s5_chunk_states(u_c, x0r, x0i, a_re, a_im, bb_re, bb_im, rev, chunk, scr):
    row = lax.broadcasted_iota(jnp.int32, (chunk, 1024), 0)
    first = row == (chunk - 1 if rev else 0)
    inj_r = a_re * x0r - a_im * x0i
    inj_i = a_re * x0i + a_im * x0r
    xr = b_nn(u_c, bb_re) + jnp.where(first, inj_r, 0.0)
    xi = b_nn(u_c, bb_im) + jnp.where(first, inj_i, 0.0)
    return _s5_scan(xr, xi, a_re, a_im, rev, chunk, scr)


def _row_pick(x, idx):
    row = lax.broadcasted_iota(jnp.int32, x.shape, 0)
    return jnp.sum(jnp.where(row == idx, x, 0.0), axis=0, keepdims=True)


def s5_scan_fwd(u, acc, a_re, a_im, bb_re, bb_im, cc_re, cc_im, n_ctx_rows, chunk, rev, name):
    t = u.shape[0]
    nch, ncc = t // chunk, n_ctx_rows // chunk

    def body(u_ref, acc_ref, ar_ref, ai_ref, br_ref, bi_ref, cr_ref, ci_ref, y_ref, x0r_ref, x0i_ref, *scr):
        a_r, a_i = ar_ref[...], ai_ref[...]

        def step(s, carry):
            x0r, x0i = carry
            c = _gla_chunk_of(s, ncc, nch, rev)
            rows = pl.ds(pl.multiple_of(c * chunk, chunk), chunk)
            x0r_ref[c] = x0r
            x0i_ref[c] = x0i
            xr, xi = _s5_chunk_states(u_ref[rows, :], x0r, x0i, a_r, a_i, br_ref[...], bi_ref[...], rev, chunk, scr)
            y_ref[rows, :] = acc_ref[rows, :] + b_nt(xr, cr_ref[...]) - b_nt(xi, ci_ref[...])
            last = 0 if rev else chunk - 1
            return _row_pick(xr, last), _row_pick(xi, last)

        lax.fori_loop(0, nch, step, (jnp.zeros((1, 1024), F32), jnp.zeros((1, 1024), F32)))

    vm = pl.BlockSpec(memory_space=pltpu.VMEM)
    return pl.pallas_call(
        body, name=name, in_specs=[vm] * 8, out_specs=[vm] * 3,
        out_shape=[jax.ShapeDtypeStruct((t, 256), F32), jax.ShapeDtypeStruct((nch, 1, 1024), F32),
                   jax.ShapeDtypeStruct((nch, 1, 1024), F32)],
        scratch_shapes=_s5_scratch(chunk), compiler_params=_cparams(),
    )(u, acc, a_re, a_im, bb_re, bb_im, cc_re, cc_im)


def s5_scan_bwd(u, dy, du_acc, x0r, x0i, a_re, a_im, bb_re, bb_im, cc_re, cc_im, n_ctx_rows, chunk, rev, name):
    t = u.shape[0]
    nch, ncc = t // chunk, n_ctx_rows // chunk

    def body(u_ref, dy_ref, dua_ref, x0r_ref, x0i_ref, ar_ref, ai_ref, br_ref, bi_ref, cr_ref, ci_ref,
             du_ref, dar_ref, dai_ref, dbr_ref, dbi_ref, dcr_ref, dci_ref, *scr):
        a_r, a_i = ar_ref[...], ai_ref[...]
        for ref in (dbr_ref, dbi_ref, dcr_ref, dci_ref):
            ref[...] = jnp.zeros_like(ref)
        row = lax.broadcasted_iota(jnp.int32, (chunk, 1024), 0)
        first_idx, last_idx = (chunk - 1, 0) if rev else (0, chunk - 1)

        def step(j, carry):
            lcr, lci, dar, dai = carry
            s = nch - 1 - j
            c = _gla_chunk_of(s, ncc, nch, rev)
            rows = pl.ds(pl.multiple_of(c * chunk, chunk), chunk)
            u_c, dy_c = u_ref[rows, :], dy_ref[rows, :]
            x0r_c, x0i_c = x0r_ref[c], x0i_ref[c]
            xr, xi = _s5_chunk_states(u_c, x0r_c, x0i_c, a_r, a_i, br_ref[...], bi_ref[...], rev, chunk, scr[:4])
            dcr_ref[...] += b_tn(dy_c, xr)
            dci_ref[...] -= b_tn(dy_c, xi)
            inj_r = a_r * lcr + a_i * lci
            inj_i = a_r * lci - a_i * lcr
            is_last = row == last_idx
            lr = b_nn(dy_c, cr_ref[...]) + jnp.where(is_last, inj_r, 0.0)
            li = -b_nn(dy_c, ci_ref[...]) + jnp.where(is_last, inj_i, 0.0)
            lr, li = _s5_scan(lr, li, a_r, -a_i, not rev, chunk, scr[4:])
            du_ref[rows, :] = dua_ref[rows, :] + b_nt(lr, br_ref[...]) + b_nt(li, bi_ref[...])
            dbr_ref[...] += b_tn(u_c, lr)
            dbi_ref[...] += b_tn(u_c, li)
            if rev:
                pr, pi = pltpu.roll(xr, chunk - 1, 0), pltpu.roll(xi, chunk - 1, 0)
            else:
                pr, pi = pltpu.roll(xr, 1, 0), pltpu.roll(xi, 1, 0)
            is_first = row == first_idx
            pr, pi = jnp.where(is_first, x0r_c, pr), jnp.where(is_first, x0i_c, pi)
            dar = dar + jnp.sum(lr * pr + li * pi, axis=0, keepdims=True)
            dai = dai + jnp.sum(li * pr - lr * pi, axis=0, keepdims=True)
            return _row_pick(lr, first_idx), _row_pick(li, first_idx), dar, dai

        z = jnp.zeros((1, 1024), F32)
        _, _, dar, dai = lax.fori_loop(0, nch, step, (z, z, z, z))
        dar_ref[...] = dar
        dai_ref[...] = dai

    vm = pl.BlockSpec(memory_space=pltpu.VMEM)
    big = jax.ShapeDtypeStruct((256, 1024), F32)
    vec = jax.ShapeDtypeStruct((1, 1024), F32)
    return pl.pallas_call(
        body, name=name, in_specs=[vm] * 11, out_specs=[vm] * 7,
        out_shape=[jax.ShapeDtypeStruct((t, 256), F32), vec, vec, big, big, big, big],
        scratch_shapes=_s5_scratch(chunk) + _s5_scratch(chunk), compiler_params=_cparams(),
    )(u, dy, du_acc, x0r, x0i, a_re, a_im, bb_re, bb_im, cc_re, cc_im)


POOL_HALO = 8


def pool_apply(u_pad, n, transpose, name, tile=ROW_TILE):
    tile = min(tile, n)
    ext = tile + 2 * POOL_HALO

    def body(u_ref, o_ref):
        lax.fori_loop(0, n // tile, functools.partial(step, u_ref, o_ref), 0)

    def step(u_ref, o_ref, i, carry):
        val = u_ref[pl.ds(pl.multiple_of(i * tile, tile), ext), :]
        lane = lax.broadcasted_iota(jnp.int32, (ext, 256), 1)
        half = jnp.left_shift(1, jnp.right_shift(lane, 6))
        trow = lax.broadcasted_iota(jnp.int32, (ext, 256), 0) + (i * tile - POOL_HALO)
        cnt = jnp.minimum(trow + half, n) - jnp.maximum(trow - half, 0)
        inv = 1.0 / jnp.maximum(cnt, 1).astype(F32)
        src = val * inv if transpose else val
        acc = jnp.zeros((tile, 256), F32)
        for d in range(-POOL_HALO, POOL_HALO):
            in_win = jnp.logical_and(d >= -half, d <= half - 1)[POOL_HALO:POOL_HALO + tile]
            shift = d if transpose else -d
            rolled = pltpu.roll(src, shift % ext, 0)[POOL_HALO:POOL_HALO + tile]
            acc = acc + jnp.where(in_win, rolled, 0.0)
        centre = val[POOL_HALO:POOL_HALO + tile]
        if not transpose:
            acc = acc * inv[POOL_HALO:POOL_HALO + tile]
        o_ref[pl.ds(pl.multiple_of(i * tile, tile), tile), :] = acc - centre
        return carry

    vm = pl.BlockSpec(memory_space=pltpu.VMEM)
    return pl.pallas_call(
        body, name=name, in_specs=[vm], out_specs=vm,
        out_shape=jax.ShapeDtypeStruct((n, 256), F32), compiler_params=_cparams(),
    )(u_pad)


NA_SCALE = 64.0 ** -0.5
NEG = -1e30


def _call_with_exchange(compute, name, args, out_shapes, comm):
    vm = pl.BlockSpec(memory_space=pltpu.VMEM)
    n_in, n_out = len(args), len(out_shapes)
    if comm is None:
        outs = pl.pallas_call(compute, name=name, in_specs=[vm] * n_in, out_specs=[vm] * n_out, out_shape=out_shapes,
                              compiler_params=_cparams())(*args)
        return outs, None
    arrays, scatter = comm
    n = len(arrays)

    def body(*refs):
        c_in = refs[n_in:n_in + n]
        c_out = refs[n_in + n + n_out:n_in + 2 * n + n_out]
        finish = _exchange_issue(c_in, c_out, scatter, *refs[n_in + 2 * n + n_out:])
        compute(*refs[:n_in], *refs[n_in + n:n_in + n + n_out])
        finish()

    hbm = pl.BlockSpec(memory_space=pl.ANY)
    outs = pl.pallas_call(
        body, name=name, in_specs=[vm] * n_in + [hbm] * n, out_specs=[vm] * n_out + [hbm] * n,
        out_shape=list(out_shapes) + _exchange_out_shapes(arrays, scatter), scratch_shapes=_exchange_sems(n),
        compiler_params=_cparams(has_side_effects=True),
    )(*args, *arrays)
    return outs[:n_out], outs[n_out:]


def _na_head_masks():
    return jnp.asarray(np.stack([(np.arange(256) // 64 == h) for h in range(4)]).astype(np.float32).reshape(4, 1, 256))


def _na_window(r, rows):
    start = jnp.clip(r - 4, 0, rows - 8)
    return start, start - r + 7


def _na_probs(qh, kw, kc, bias):
    s_c = b_nt(qh, kc)
    m = jnp.max(s_c, axis=-1, keepdims=True)
    if kw is not None:
        s_w = b_nt(qh, kw) + bias
        m = jnp.maximum(m, jnp.max(s_w, axis=-1, keepdims=True))
        p_w = jnp.exp(s_w - m)
    p_c = jnp.exp(s_c - m)
    l = jnp.sum(p_c, axis=-1, keepdims=True)
    if kw is not None:
        l = l + jnp.sum(p_w, axis=-1, keepdims=True)
        return p_w / l, p_c / l
    return None, p_c / l


def na_fwd(q, k, v, bias8, n_ctx_rows, name, comm=None):
    t = q.shape[0]
    m_ctx = n_ctx_rows
    rows = (t - m_ctx) // GRID_W
    hm = _na_head_masks()

    def body(q_ref, k_ref, v_ref, b_ref, hm_ref, o_ref):
        kc, vc = k_ref[0:m_ctx, :], v_ref[0:m_ctx, :]

        def ctx_step(i, _):
            rs = pl.ds(pl.multiple_of(i * 64, 64), 64)
            q4 = _stack_heads(q_ref[rs, :] * NA_SCALE, hm_ref)
            _, p_c = _na_probs(q4, None, kc, None)
            o_ref[rs, :] = _fold_heads(b_nn(p_c, vc), hm_ref)
            return 0

        lax.fori_loop(0, m_ctx // 64, ctx_step, 0)

        def lat_step(r, _):
            start, off = _na_window(r, rows)
            rs = pl.ds(pl.multiple_of(m_ctx + r * 64, 64), 64)
            ws = pl.ds(pl.multiple_of(m_ctx + start * 64, 64), 512)
            q4 = _stack_heads(q_ref[rs, :] * NA_SCALE, hm_ref)
            kw, vw = k_ref[ws, :], v_ref[ws, :]
            p_w, p_c = _na_probs(q4, kw, kc, b_ref[off])
            o_ref[rs, :] = _fold_heads(b_nn(p_w, vw) + b_nn(p_c, vc), hm_ref)
            return 0

        lax.fori_loop(0, rows, lat_step, 0)

    (o,), received = _call_with_exchange(body, name, [q, k, v, bias8, hm], [jax.ShapeDtypeStruct((t, 256), F32)], comm)
    return o if comm is None else (o, received)


def na_bwd(q, k, v, do, bias8, n_ctx_rows, name, comm=None):
    t = q.shape[0]
    m_ctx = n_ctx_rows
    rows = (t - m_ctx) // GRID_W
    hm = _na_head_masks()

    def body(q_ref, k_ref, v_ref, do_ref, b_ref, hm_ref, dq_ref, dk_ref, dv_ref, db_ref):
        kc, vc = k_ref[0:m_ctx, :], v_ref[0:m_ctx, :]
        dk_ref[...] = jnp.zeros_like(dk_ref)
        dv_ref[...] = jnp.zeros_like(dv_ref)
        db_ref[...] = jnp.zeros_like(db_ref)

        def head_terms(qh, doh, kw, vw, bias):
            p_w, p_c = _na_probs(qh, kw, kc, bias)
            dp_c = b_nt(doh, vc)
            delta = jnp.sum(p_c * dp_c, axis=-1, keepdims=True)
            if kw is not None:
                dp_w = b_nt(doh, vw)
                delta = delta + jnp.sum(p_w * dp_w, axis=-1, keepdims=True)
                ds_w = p_w * (dp_w - delta)
            else:
                ds_w = None
            ds_c = p_c * (dp_c - delta)
            return p_w, p_c, ds_w, ds_c

        def ctx_step(i, carry):
            dkc, dvc = carry
            rs = pl.ds(pl.multiple_of(i * 64, 64), 64)
            q4, do4 = _stack_heads(q_ref[rs, :] * NA_SCALE, hm_ref), _stack_heads(do_ref[rs, :], hm_ref)
            _, p_c, _, ds_c = head_terms(q4, do4, None, None, None)
            dq_ref[rs, :] = _fold_heads(b_nn(ds_c, kc), hm_ref) * NA_SCALE
            return dkc + b_tn(ds_c, q4), dvc + b_tn(p_c, do4)

        zc = jnp.zeros((m_ctx, 256), F32)
        carry = lax.fori_loop(0, m_ctx // 64, ctx_step, (zc, zc))

        def lat_step(r, carry):
            dkc, dvc = carry
            start, off = _na_window(r, rows)
            rs = pl.ds(pl.multiple_of(m_ctx + r * 64, 64), 64)
            ws = pl.ds(pl.multiple_of(m_ctx + start * 64, 64), 512)
            q4, do4 = _stack_heads(q_ref[rs, :] * NA_SCALE, hm_ref), _stack_heads(do_ref[rs, :], hm_ref)
            kw, vw = k_ref[ws, :], v_ref[ws, :]
            p_w, p_c, ds_w, ds_c = head_terms(q4, do4, kw, vw, b_ref[off])
            dq_ref[rs, :] = _fold_heads(b_nn(ds_w, kw) + b_nn(ds_c, kc), hm_ref) * NA_SCALE
            dk_ref[ws, :] += b_tn(ds_w, q4)
            dv_ref[ws, :] += b_tn(p_w, do4)
            db_ref[off] += ds_w
            return dkc + b_tn(ds_c, q4), dvc + b_tn(p_c, do4)

        dkc, dvc = lax.fori_loop(0, rows, lat_step, carry)
        dk_ref[0:m_ctx, :] = dkc
        dv_ref[0:m_ctx, :] = dvc

    row = jax.ShapeDtypeStruct((t, 256), F32)
    return _call_with_exchange(body, name, [q, k, v, do, bias8, hm], [row, row, row, jax.ShapeDtypeStruct(bias8.shape, F32)], comm)


def _na_toeplitz():
    col = np.arange(GRID_W)
    dd = (col[None, :] - col[:, None] + 15).reshape(-1)
    tt = np.zeros((GRID_W * GRID_W, 128), np.float32)
    ok = (dd >= 0) & (dd <= 30)
    tt[np.arange(GRID_W * GRID_W)[ok], dd[ok]] = 1.0
    return tt


def _na_bias8(rpb, name):
    col = np.arange(GRID_W)
    cs = np.clip(col - 8, 0, GRID_W - 16)
    col_mask = (col[None, :] >= cs[:, None]) & (col[None, :] < cs[:, None] + 16)
    rpb2 = jnp.pad(rpb.reshape(60, 31), ((0, 4), (0, 97)))
    (toe,) = whole_fwd(lambda r_, t_: (hdot_nt(r_, t_),), name, [rpb2, jnp.asarray(_na_toeplitz())], [(64, GRID_W * GRID_W)])
    toe = toe[:60].reshape(4, 15, GRID_W, GRID_W)
    b = jnp.stack([toe[:, off:off + 8] for off in range(8)], axis=1)
    b = jnp.where(jnp.asarray(col_mask)[None, None, None], b, NEG)
    return b.transpose(1, 0, 3, 2, 4).reshape(8, 4 * GRID_W, 8 * GRID_W)


def _na_rpb_grad(dbias8, name):
    tt = _na_toeplitz()
    sel = np.zeros((64, 256), np.float32)
    for h in range(4):
        for off in range(8):
            for i in range(8):
                sel[h * 15 + off + i, h * 64 + off * 8 + i] = 1.0
    a2 = dbias8.reshape(8, 4, GRID_W, 8, GRID_W).transpose(1, 0, 3, 2, 4).reshape(256, GRID_W * GRID_W)
    (out,) = whole_fwd(lambda a, t_, s_: (hdot(s_, hdot(a, t_)),), name, [a2, jnp.asarray(tt), jnp.asarray(sel)], [(64, 128)])
    return out[:60, :31].reshape(4, 15, 31)


def f_mod(cs, b_mod, w_mod):
    s = _silu(cs)
    return bdot(s, w_mod) + b_mod, s


def loss_and_grad(z, tgt, n_ctx_rows, name, tile=ROW_TILE):
    t, d = z.shape
    tile = min(tile, n_ctx_rows)
    nct = n_ctx_rows // tile

    def body(z_ref, t_ref, dz_ref, loss_ref):
        i = pl.program_id(0)

        @pl.when(i == 0)
        def _():
            loss_ref[...] = jnp.zeros_like(loss_ref)

        @pl.when(i < nct)
        def _():
            dz_ref[...] = jnp.zeros_like(dz_ref)

        @pl.when(i >= nct)
        def _():
            diff = z_ref[...] - t_ref[...]
            dz_ref[...] = diff * (1.0 / d)
            loss_ref[...] += 0.5 * jnp.sum(jnp.sum(diff * diff, axis=-1, keepdims=True) * (1.0 / d), axis=0, keepdims=True)

    dz, loss = pl.pallas_call(
        body, name=name, grid=(t // tile,),
        in_specs=[pl.BlockSpec((tile, d), lambda i: (i, 0)),
                  pl.BlockSpec((tile, d), lambda i: (jnp.maximum(i - nct, 0), 0))],
        out_specs=[pl.BlockSpec((tile, d), lambda i: (i, 0)), pl.BlockSpec((8, 128), lambda i: (0, 0))],
        out_shape=[jax.ShapeDtypeStruct((t, d), F32), jax.ShapeDtypeStruct((8, 128), F32)],
        compiler_params=_cparams(dimension_semantics=("arbitrary",)),
    )(z, tgt)
    return loss[0, 0], dz


def adamw(parts, w, m, v, name, tile=256):
    npart, r, c = parts.shape
    tile = min(tile, r)
    assert r % tile == 0
    c1 = 1.0 / (1.0 - ADAM_B1 ** ADAM_STEP)
    c2 = 1.0 / (1.0 - ADAM_B2 ** ADAM_STEP)

    def body(p_ref, w_ref, m_ref, v_ref, g_ref, d_ref, nm_ref, nv_ref):
        g = p_ref[0].astype(F32)
        for i in range(1, npart):
            g = g + p_ref[i].astype(F32)
        nm = ADAM_B1 * m_ref[...] + (1.0 - ADAM_B1) * g
        nv = ADAM_B2 * v_ref[...] + (1.0 - ADAM_B2) * (g * g)
        g_ref[...] = g
        nm_ref[...] = nm
        nv_ref[...] = nv
        d_ref[...] = -ADAM_LR * ((nm * c1) / (jnp.sqrt(nv * c2) + ADAM_EPS) + ADAM_WD * w_ref[...])

    blk = pl.BlockSpec((tile, c), lambda i: (i, 0))
    return pl.pallas_call(
        body, name=name, grid=(r // tile,),
        in_specs=[pl.BlockSpec((npart, tile, c), lambda i: (0, i, 0)), blk, blk, blk],
        out_specs=[blk] * 4, out_shape=[jax.ShapeDtypeStruct((r, c), F32)] * 4,
        compiler_params=_cparams(dimension_semantics=("arbitrary",)),
    )(parts, w, m, v)


def _peer(x, y, c, k):
    return (1 - x if k & 4 else x, 1 - y if k & 2 else y, 1 - c if k & 1 else c)


def _exchange_out_shapes(arrays, scatter):
    return [jax.ShapeDtypeStruct(a.shape if s else (N_DEV,) + a.shape, a.dtype) for a, s in zip(arrays, scatter)]


def _exchange_sems(n):
    return [pltpu.SemaphoreType.DMA((n, N_DEV - 1)), pltpu.SemaphoreType.DMA((n, N_DEV - 1)), pltpu.SemaphoreType.DMA((n,))]


def _exchange_issue(ins, outs, scatter, send_sems, recv_sems, local_sems):
    n = len(ins)
    x, y, c = lax.axis_index("x"), lax.axis_index("y"), lax.axis_index("c")
    me = 4 * x + 2 * y + c

    def index_of(p):
        return 4 * p[0] + 2 * p[1] + p[2]

    local = []
    for a in range(n):
        src_me = ins[a].at[me] if scatter[a] else ins[a]
        loc = pltpu.make_async_copy(src_me, outs[a].at[me], local_sems.at[a])
        loc.start()
        local.append(loc)
    for k in range(1, N_DEV):
        peer = _peer(x, y, c, k)
        for a in range(n):
            src = ins[a].at[index_of(peer)] if scatter[a] else ins[a]
            pltpu.make_async_remote_copy(
                src_ref=src, dst_ref=outs[a].at[me], send_sem=send_sems.at[a, k - 1], recv_sem=recv_sems.at[a, k - 1],
                device_id=peer, device_id_type=pl.DeviceIdType.MESH).start()

    def finish():
        for k in range(1, N_DEV):
            peer = _peer(x, y, c, k)
            for a in range(n):
                src = ins[a].at[index_of(peer)] if scatter[a] else ins[a]
                cp = pltpu.make_async_remote_copy(
                    src_ref=src, dst_ref=outs[a].at[index_of(peer)], send_sem=send_sems.at[a, k - 1],
                    recv_sem=recv_sems.at[a, k - 1], device_id=peer, device_id_type=pl.DeviceIdType.MESH)
                cp.wait_send()
                cp.wait_recv()
        for loc in local:
            loc.wait()

    return finish


def exchange(arrays, scatter, name):
    n = len(arrays)

    def body(*refs):
        _exchange_issue(refs[:n], refs[n:2 * n], scatter, *refs[2 * n:])()

    hbm = pl.BlockSpec(memory_space=pl.ANY)
    return pl.pallas_call(
        body, name=name, in_specs=[hbm] * n, out_specs=[hbm] * n, out_shape=_exchange_out_shapes(arrays, scatter),
        scratch_shapes=_exchange_sems(n), compiler_params=pltpu.CompilerParams(has_side_effects=True),
    )(*arrays)


def _rope_tables(n_lat, n_ctx):
    tok = np.arange(n_lat)
    freqs = 10000.0 ** (-np.arange(0, 16, 2, dtype=np.float32) / 16.0)

    def table(pos):
        ang = pos.astype(np.float32)[:, None] * freqs[None, :]
        ang = np.concatenate([ang, ang], axis=-1)
        return np.cos(ang), np.sin(ang)

    cr, sr = table(tok // GRID_W)
    cc, sc = table(tok % GRID_W)
    cos = np.tile(np.concatenate([cr, cc], axis=-1), (1, 4))
    sin = np.tile(np.concatenate([sr, sc], axis=-1), (1, 4))
    cos = np.concatenate([np.ones((n_ctx, 128), np.float32), cos], axis=0)
    sin = np.concatenate([np.zeros((n_ctx, 128), np.float32), sin], axis=0)
    return jnp.asarray(cos, F32), jnp.asarray(sin, F32)


def _pad_w_in(w):
    z = lambda n: jnp.zeros((w.shape[0], n), w.dtype)
    return jnp.concatenate([w[:, 1824:2848], w[:, 128:384], w[:, 416:672], w[:, 672:928], w[:, 928:1184], w[:, 1312:1568],
                            w[:, 1568:1824], w[:, 0:128], w[:, 384:416], z(96), w[:, 1184:1312], z(128)], axis=1)


def _unpad_w_in(wp):
    return jnp.concatenate([wp[:, C_GK:C_GK + 128], wp[:, C_GV:C_GV + 256], wp[:, C_GG:C_GG + 32], wp[:, C_NK:C_NK + 256],
                            wp[:, C_NV:C_NV + 256], wp[:, C_SU:C_SU + 256], wp[:, C_GQ:C_GQ + 128], wp[:, C_NQ:C_NQ + 256],
                            wp[:, C_PU:C_PU + 256], wp[:, C_GT:C_GT + 1024]], axis=1)


def _pad_rows(u):
    return jnp.pad(u, ((POOL_HALO, POOL_HALO), (0, 0)))


def _block_diag4(w):
    out = jnp.zeros((256, 256), w.dtype)
    for i in range(4):
        out = lax.dynamic_update_slice(out, w[i], (64 * i, 64 * i))
    return out


def _layer_params(p, big, l):
    e_rep, e_tile, gmask, bdm = _s5_consts()
    wg = jnp.zeros((128, 256), F32)
    wg = lax.dynamic_update_slice(wg, p["gla_w_gate"][l, 0], (0, 0))
    wg = lax.dynamic_update_slice(wg, p["gla_w_gate"][l, 1], (16, 128))
    s5 = []
    for d in range(2):
        s5.append([p["s5_lam_re"][l, d], p["s5_lam_im"][l, d], p["s5_log_dt"][l, d].reshape(16, 1),
                   p["s5_b_re"][l, d].transpose(0, 2, 1).reshape(256, 64), p["s5_b_im"][l, d].transpose(0, 2, 1).reshape(256, 64),
                   p["s5_c_re"][l, d].reshape(256, 64), p["s5_c_im"][l, d].reshape(256, 64), e_rep, e_tile, gmask, bdm])
    havg = jnp.asarray((np.arange(256)[:, None] // 64 == np.arange(256)[None, :] // 64).astype(np.float32) / 64.0)
    e4 = jnp.asarray((np.arange(64)[:, None] == np.arange(256)[None, :] % 64).astype(np.float32))
    return dict(
        g_pre=p["g_pre"][l].reshape(1, D), g_post=p["g_post"][l].reshape(1, D), b_mod=p["b_mod"][l].reshape(1, 3 * D),
        w_mod=big["w_mod"], w_in=_pad_w_in(big["w_in"]), w_out=big["w_out"],
        wg=wg, bg=p["gla_b_gate"][l].reshape(1, 256), g_norm=jnp.pad(p["gla_g_norm"][l].reshape(1, 64), ((0, 7), (0, 0))),
        bias8=_na_bias8(p["na_rpb"][l], f"na_bias_l{l}"), s5=s5, s5_d=p["s5_d"][l].reshape(1, 256), w_glu=big["s5_w_glu"].astype(F32),
        b_glu=p["s5_b_glu"][l].reshape(1, 256), wpool=_block_diag4(p["pool_w"][l]), pool_scale=p["pool_scale"][l].reshape(1, 256),
        havg=havg, e4=e4)


def _cols(pz, start, width):
    return pz[:, start:start + width]


def _layer_fwd(z, modseg, lp, cos, sin, m_ctx, tile, s5_chunk, l, comm=None):
    t = z.shape[0]
    nct = m_ctx // tile
    nm = lambda s: f"{s}_l{l}"
    (h,) = rowwise_fwd(f_pre, nm("pre"), [z], [modseg], [lp["g_pre"]], [D], tile, nct)
    (gt,) = mm_nn_cols(h, lp["w_in"], C_GT, [1024], nm("in_proj_a"), tm=tile)
    pv, nk, nv, su = mm_nn_cols(h, lp["w_in"], C_GV, [256] * 4, nm("in_proj_b"), tm=tile)
    nq, pu, pk, pg, pq = mm_nn_cols(h, lp["w_in"], C_NQ, [256, 256, 128, 128, 128], nm("in_proj_c"), tm=tile)
    q_r, k_r, lgf, lgb = rowwise_fwd(f_gla_prep, nm("gla_prep"), [pk, pg, pq, cos, sin], [], [lp["wg"], lp["bg"]], [128] * 4, tile, nct)
    part = (lambda idx: None) if comm is None else (lambda idx: ([comm[i] for i in idx], [False] * len(idx)))
    (o1, st_f), got_mod = gla_scan_fwd(q_r, k_r, pv, lgf, jnp.zeros((t, 256), F32), m_ctx, False, nm("gla_f"), part([0]))
    (o_gla, st_b), got_out = gla_scan_fwd(q_r, k_r, pv, lgb, o1, m_ctx, True, nm("gla_r"), part([2, 3]))
    received = None
    if comm is None:
        o_na = na_fwd(nq, nk, nv, lp["bias8"], m_ctx, nm("na"))
    else:
        o_na, got_in = na_fwd(nq, nk, nv, lp["bias8"], m_ctx, nm("na"), part([1]))
        received = [got_mod[0], got_in[0], got_out[0], got_out[1]]
    s5p = [whole_fwd(f_s5_params, nm(f"s5_par{d}"), lp["s5"][d], [(1, 1024)] * 2 + [(256, 1024)] * 4) for d in range(2)]
    y1, x0r_f, x0i_f = s5_scan_fwd(su, jnp.zeros((t, 256), F32), *s5p[0], m_ctx, s5_chunk, False, nm("s5_f"))
    y5, x0r_b, x0i_b = s5_scan_fwd(su, y1, *s5p[1], m_ctx, s5_chunk, True, nm("s5_r"))
    pm = jnp.concatenate([pool_apply(_pad_rows(pu[:m_ctx]), m_ctx, False, nm("pool_c")),
                          pool_apply(_pad_rows(pu[m_ctx:]), t - m_ctx, False, nm("pool_x"))], axis=0)
    mix_rows = [o_gla, o_na, y5, su, pm, gt]
    mix_globs = [lp["g_norm"], lp["s5_d"], lp["w_glu"], lp["b_glu"], lp["wpool"], lp["pool_scale"], lp["havg"], lp["e4"]]
    (yg,) = rowwise_fwd(f_mix, nm("mix"), mix_rows, [], mix_globs, [D], tile, nct)
    out = mm_nn([yg], lp["w_out"], nm("out_proj"), tm=tile)
    (z_new,) = rowwise_fwd(f_post, nm("post"), [z, out], [modseg], [lp["g_post"]], [D], tile, nct)
    saved = dict(z=z, h=h, pv=pv, nk=nk, nv=nv, su=su, nq=nq, pk=pk, pg=pg, pq=pq, q_r=q_r, k_r=k_r, lgf=lgf, lgb=lgb,
                 st_f=st_f, st_b=st_b, s5p=s5p, x0f=(x0r_f, x0i_f), x0b=(x0r_b, x0i_b), mix_rows=mix_rows, mix_globs=mix_globs,
                 yg=yg, out=out)
    return z_new, saved, received


def _f_pre_res(x, mod, g_pre):
    return f_pre(x, mod, g_pre)[0], x


def _layer_bwd(dz_new, sv, modseg, lp, cos, sin, m_ctx, tile, s5_chunk, l, comm=None, gdt=F32):
    t = dz_new.shape[0]
    nct = m_ctx // tile
    nm = lambda s: f"{s}_l{l}"
    g = {}
    dz_res, dout, dmod_post, g["g_post"] = rowwise_bwd(f_post, nm("post_b"), [sv["z"], sv["out"]], [modseg], [lp["g_post"]],
                                                       [dz_new], tile, nct, [True, True], [True])
    dyg = mm_nt([dout], lp["w_out"], nm("out_proj_dx"), tm=tile)
    dw_tile = t // 4 if t % 32 == 0 else tile
    (g["w_out"],) = mm_tn(sv["yg"], [dout], nm("out_proj_dw"), tm=dw_tile, out_dtype=gdt)
    res = rowwise_bwd(f_mix, nm("mix_b"), sv["mix_rows"], [], sv["mix_globs"], [dyg], tile, nct, [True] * 6, [True] * 6 + [False] * 2)
    do_gla, do_na, dy5, dsu_a, dpm, dgt = res[:6]
    g["g_norm"], g["s5_d"], g["w_glu"], g["b_glu"], g["wpool"], g["pool_scale"] = res[6:]
    dpu = jnp.concatenate([pool_apply(_pad_rows(dpm[:m_ctx]), m_ctx, True, nm("pool_c_b")),
                           pool_apply(_pad_rows(dpm[m_ctx:]), t - m_ctx, True, nm("pool_x_b"))], axis=0)
    r_b = s5_scan_bwd(sv["su"], dy5, dsu_a, *sv["x0b"], *sv["s5p"][1], m_ctx, s5_chunk, True, nm("s5_r_b"))
    r_f = s5_scan_bwd(sv["su"], dy5, r_b[0], *sv["x0f"], *sv["s5p"][0], m_ctx, s5_chunk, False, nm("s5_f_b"))
    dsu = r_f[0]
    g["s5"] = [whole_bwd(f_s5_params, nm(f"s5_par{d}_b"), lp["s5"][d], list(r[1:]), [True] * 7 + [False] * 4)
               for d, r in ((0, r_f), (1, r_b))]
    part = (lambda idx: None) if comm is None else (lambda idx: ([comm[i] for i in idx], [True] * len(idx)))
    (dnq, dnk, dnv, dbias8), got_in = na_bwd(sv["nq"], sv["nk"], sv["nv"], do_na, lp["bias8"], m_ctx, nm("na_b"), part([1]))
    g["rpb"] = _na_rpb_grad(dbias8, nm("na_rpb_b"))
    zq, zv = jnp.zeros((t, 128), F32), jnp.zeros((t, 256), F32)
    (dq1, dk1, dv1, dlgb), got_mod = gla_scan_bwd(sv["q_r"], sv["k_r"], sv["pv"], sv["lgb"], sv["st_b"], do_gla, (zq, zq, zv), m_ctx, True,
                                                  nm("gla_r_b"), part([0]))
    (dq_r, dk_r, dpv, dlgf), got_out = gla_scan_bwd(sv["q_r"], sv["k_r"], sv["pv"], sv["lgf"], sv["st_f"], do_gla, (dq1, dk1, dv1), m_ctx, False,
                                                    nm("gla_f_b"), part([2, 3]))
    received = None if comm is None else [got_mod[0], got_in[0], got_out[0], got_out[1]]
    dpk, dpg, dpq, g["wg"], g["bg"] = rowwise_bwd(f_gla_prep, nm("gla_prep_b"), [sv["pk"], sv["pg"], sv["pq"], cos, sin], [],
                                                  [lp["wg"], lp["bg"]], [dq_r, dk_r, dlgf, dlgb], tile, nct,
                                                  [True, True, True, False, False], [True, True])
    parts = [dgt, dpv, dnk, dnv, dsu, dnq, dpu, dpk, dpg, dpq, jnp.zeros((t, 128), F32)]
    dh = mm_nt(parts, lp["w_in"], nm("in_proj_dx"), tm=tile)
    g["w_in"] = _unpad_w_in(jnp.concatenate(mm_tn(sv["h"], parts, nm("in_proj_dw"), tm=dw_tile, out_dtype=gdt), axis=1))
    dz, dmod_pre, g["g_pre"] = rowwise_bwd(_f_pre_res, nm("pre_b"), [sv["z"]], [modseg], [lp["g_pre"]], [dh, dz_res], tile, nct, [True], [True])
    return dz, dmod_pre, dmod_post, g, received


def _f_mod_sum(cs, b_mod, w_mod):
    mod, _ = f_mod(cs, b_mod, w_mod)
    return mod, cs


def local_step(x, c, ctx, tgt, p, shards=None, tile=ROW_TILE, s5_chunk=S5_CHUNK):
    n_lat, m_ctx = x.shape[0], ctx.shape[0]
    n_layers = p["g_pre"].shape[0]
    z = jnp.concatenate([ctx, x], axis=0)
    cos, sin = _rope_tables(n_lat, m_ctx)
    cs = jnp.concatenate([c.reshape(1, D), p["c_ctx"].reshape(1, D), jnp.zeros((6, D), F32)], axis=0)
    gather = [False] * len(_SHARDED)
    lps, mods, silus, saves = [], [], [], []
    got = exchange(shards[0], gather, "gather_weights_l0") if shards is not None else None
    for l in range(n_layers):
        if shards is None:
            big = {n: p[n][l] for n in _SHARDED}
        else:
            big = {n: _gathered(g, _BY_COLS[n]) for n, g in zip(_SHARDED, got)}
        lp = _layer_params(p, big, l)
        mod8, s8 = whole_fwd(f_mod, f"mod_l{l}", [cs, lp["b_mod"], lp["w_mod"]], [(8, 3 * D), (8, D)])
        modseg = mod8[:2].reshape(2, 1, 3 * D)
        comm = shards[l + 1] if shards is not None and l + 1 < n_layers else None
        z, sv, got = _layer_fwd(z, modseg, lp, cos, sin, m_ctx, tile, s5_chunk, l, comm)
        lps.append(lp); mods.append(modseg); silus.append(s8); saves.append(sv)
    loss, dz = loss_and_grad(z, tgt, m_ctx, "loss", tile)
    grads, received = [None] * n_layers, [None] * n_layers
    gdt = F32 if shards is None else BF16
    dcs = jnp.zeros((8, D), F32)
    pending = None
    for l in reversed(range(n_layers)):
        lp = lps[l]
        dz, dmod_pre, dmod_post, g, got = _layer_bwd(dz, saves[l], mods[l], lp, cos, sin, m_ctx, tile, s5_chunk, l, pending, gdt)
        if pending is not None:
            received[l + 1] = got
        dmod = jnp.concatenate([dmod_pre.reshape(2, 3 * D)[:, :2 * D], dmod_post.reshape(2, 3 * D)[:, 2 * D:]], axis=1)
        dmod8 = jnp.pad(dmod, ((0, 6), (0, 0)))
        dcs, g["b_mod"] = whole_bwd(_f_mod_sum, f"mod_b_l{l}", [cs, lp["b_mod"], lp["w_mod"]], [dmod8, dcs], [True, True, False])
        g["w_mod"] = jnp.concatenate(mm_tn(silus[l], [dmod8[:, :D], dmod8[:, D:2 * D], dmod8[:, 2 * D:]], f"mod_dw_l{l}", tm=8, out_dtype=gdt), axis=1)
        grads[l] = g
        if shards is not None:
            pending = _layer_sends(g)
    if shards is not None:
        received[0] = exchange(pending + [_small_sends(dcs[1], grads)], [True] * len(_SHARDED) + [False], "exchange_grads_l0")
    return loss, dz[m_ctx:], dcs[1], grads, received


_WEIGHTS = ["c_ctx", "w_mod", "b_mod", "g_pre", "g_post", "w_in", "w_out", "gla_w_gate", "gla_b_gate", "gla_g_norm", "na_rpb",
            "s5_lam_re", "s5_lam_im", "s5_log_dt", "s5_b_re", "s5_b_im", "s5_c_re", "s5_c_im", "s5_d", "s5_w_glu", "s5_b_glu",
            "pool_w", "pool_scale"]
_INPUTS = ["x", "c", "ctx"] + _WEIGHTS + ["loss_target"] + ["m_" + n for n in _WEIGHTS] + ["v_" + n for n in _WEIGHTS]
_SHARDED = ["w_mod", "w_in", "w_out", "s5_w_glu"]
_BY_COLS = {"w_mod": True, "w_in": True, "w_out": False, "s5_w_glu": False}
_SMALL = [n for n in _WEIGHTS if n not in _SHARDED]
_SMALL_PER_LAYER = [n for n in _SMALL if n != "c_ctx"]
_PACK_ROWS = 256


def _pack_plan(like):
    tiled = [i for i, a in enumerate(like) if a.size % 1024 == 0]
    loose = [i for i, a in enumerate(like) if a.size % 1024 != 0]
    tail = -(-sum(like[i].size for i in loose) // 1024) * 8
    rows = sum(like[i].size // 128 for i in tiled) + tail
    return tiled, loose, tail, -(-rows // _PACK_ROWS) * _PACK_ROWS - rows


def _pack(arrs):
    tiled, loose, tail, fill = _pack_plan(arrs)
    dt = arrs[0].dtype
    flat = jnp.concatenate([arrs[i].reshape(-1) for i in loose])
    flat = jnp.pad(flat, (0, tail * 128 - flat.shape[0])).reshape(tail, 128)
    return jnp.concatenate([arrs[i].reshape(-1, 128) for i in tiled] + [flat, jnp.zeros((fill, 128), dt)], axis=0)


def _unpack(packed, like):
    tiled, loose, tail, _ = _pack_plan(like)
    out, row = [None] * len(like), 0
    for i in tiled:
        n = like[i].size // 128
        out[i] = packed[row:row + n].reshape(like[i].shape)
        row += n
    flat, pos = packed[row:row + tail].reshape(-1), 0
    for i in loose:
        out[i] = flat[pos:pos + like[i].size].reshape(like[i].shape)
        pos += like[i].size
    return out


def _gathered(g, cols):
    if cols:
        return g.transpose(1, 0, 2).reshape(g.shape[1], N_DEV * g.shape[2])
    return g.reshape(N_DEV * g.shape[1], g.shape[2])


def _slabs(w, cols):
    r, c = w.shape
    if cols:
        return w.reshape(r, N_DEV, c // N_DEV).transpose(1, 0, 2)
    return w.reshape(N_DEV, r // N_DEV, c)


def _layer_small(g):
    s5 = lambda i, f: jnp.stack([f(g["s5"][d][i]) for d in range(2)])
    return {
        "b_mod": g["b_mod"].reshape(3 * D), "g_pre": g["g_pre"].reshape(D), "g_post": g["g_post"].reshape(D),
        "gla_w_gate": jnp.stack([g["wg"][0:16, 0:128], g["wg"][16:32, 128:256]]),
        "gla_b_gate": g["bg"].reshape(2, 128), "gla_g_norm": g["g_norm"][0], "na_rpb": g["rpb"],
        "s5_lam_re": s5(0, lambda a: a), "s5_lam_im": s5(1, lambda a: a), "s5_log_dt": s5(2, lambda a: a.reshape(16)),
        "s5_b_re": s5(3, lambda a: a.reshape(16, 16, 64).transpose(0, 2, 1)),
        "s5_b_im": s5(4, lambda a: a.reshape(16, 16, 64).transpose(0, 2, 1)),
        "s5_c_re": s5(5, lambda a: a.reshape(16, 16, 64)), "s5_c_im": s5(6, lambda a: a.reshape(16, 16, 64)),
        "s5_d": g["s5_d"].reshape(256), "s5_b_glu": g["b_glu"].reshape(256),
        "pool_w": jnp.stack([g["wpool"][64 * i:64 * i + 64, 64 * i:64 * i + 64] for i in range(4)]),
        "pool_scale": g["pool_scale"].reshape(256),
    }


def _layer_sends(g):
    big = {"w_mod": g["w_mod"], "w_in": g["w_in"], "w_out": g["w_out"], "s5_w_glu": g["w_glu"]}
    return [_slabs(big[n], _BY_COLS[n]).astype(BF16) for n in _SHARDED]


def _small_sends(d_c_ctx, grads):
    per_layer = [_layer_small(g) for g in grads]
    full = {n: jnp.stack([s[n] for s in per_layer]) for n in _SMALL_PER_LAYER}
    full["c_ctx"] = d_c_ctx
    return _pack([full[n] for n in _SMALL]).astype(BF16)


def kernel(x, c, ctx, c_ctx, w_mod, b_mod, g_pre, g_post, w_in, w_out, gla_w_gate, gla_b_gate, gla_g_norm, na_rpb, s5_lam_re, s5_lam_im, s5_log_dt, s5_b_re, s5_b_im, s5_c_re, s5_c_im, s5_d, s5_w_glu, s5_b_glu, pool_w, pool_scale, loss_target, m_c_ctx, m_w_mod, m_b_mod, m_g_pre, m_g_post, m_w_in, m_w_out, m_gla_w_gate, m_gla_b_gate, m_gla_g_norm, m_na_rpb, m_s5_lam_re, m_s5_lam_im, m_s5_log_dt, m_s5_b_re, m_s5_b_im, m_s5_c_re, m_s5_c_im, m_s5_d, m_s5_w_glu, m_s5_b_glu, m_pool_w, m_pool_scale, v_c_ctx, v_w_mod, v_b_mod, v_g_pre, v_g_post, v_w_in, v_w_out, v_gla_w_gate, v_gla_b_gate, v_gla_g_norm, v_na_rpb, v_s5_lam_re, v_s5_lam_im, v_s5_log_dt, v_s5_b_re, v_s5_b_im, v_s5_c_re, v_s5_c_im, v_s5_d, v_s5_w_glu, v_s5_b_glu, v_pool_w, v_pool_scale):
    given = dict(zip(_INPUTS, (x, c, ctx, c_ctx, w_mod, b_mod, g_pre, g_post, w_in, w_out, gla_w_gate, gla_b_gate, gla_g_norm, na_rpb, s5_lam_re, s5_lam_im, s5_log_dt, s5_b_re, s5_b_im, s5_c_re, s5_c_im, s5_d, s5_w_glu, s5_b_glu, pool_w, pool_scale, loss_target, m_c_ctx, m_w_mod, m_b_mod, m_g_pre, m_g_post, m_w_in, m_w_out, m_gla_w_gate, m_gla_b_gate, m_gla_g_norm, m_na_rpb, m_s5_lam_re, m_s5_lam_im, m_s5_log_dt, m_s5_b_re, m_s5_b_im, m_s5_c_re, m_s5_c_im, m_s5_d, m_s5_w_glu, m_s5_b_glu, m_pool_w, m_pool_scale, v_c_ctx, v_w_mod, v_b_mod, v_g_pre, v_g_post, v_w_in, v_w_out, v_gla_w_gate, v_gla_b_gate, v_gla_g_norm, v_na_rpb, v_s5_lam_re, v_s5_lam_im, v_s5_log_dt, v_s5_b_re, v_s5_b_im, v_s5_c_re, v_s5_c_im, v_s5_d, v_s5_w_glu, v_s5_b_glu, v_pool_w, v_pool_scale)))
    n_layers = w_in.shape[0]
    shards = [[given[n][l].astype(BF16) for n in _SHARDED] for l in range(n_layers)]
    p = {n: given[n] for n in _SMALL}
    loss, grad_x, _, _, received = local_step(x[0], c, ctx[0], loss_target[0], p, shards)
    final = {}
    for n in _SHARDED:
        per_layer = [adamw(received[l][_SHARDED.index(n)], given[n][l], given["m_" + n][l], given["v_" + n][l], f"adamw_{n}_l{l}")
                     for l in range(n_layers)]
        final[n] = [jnp.stack([res[kind] for res in per_layer]) for kind in range(4)]
    like = [given[n] for n in _SMALL]
    res = adamw(received[0][-1], _pack(like), _pack([given["m_" + n] for n in _SMALL]), _pack([given["v_" + n] for n in _SMALL]),
                "adamw_small")
    unpacked = [_unpack(packed, like) for packed in res]
    for i, n in enumerate(_SMALL):
        final[n] = [unpacked[kind][i] for kind in range(4)]
    loss = lax.psum(loss, ("x", "y", "c"))
    return (loss, grad_x[None], *[final[n][0] for n in _WEIGHTS], *[final[n][1] for n in _WEIGHTS],
            *[final[n][2] for n in _WEIGHTS], *[final[n][3] for n in _WEIGHTS])
```

```python
import functools
import math

import numpy as np
import jax
import jax.numpy as jnp
from jax import lax
from jax.experimental import pallas as pl
from jax.experimental.pallas import tpu as pltpu

F32 = jnp.float32
BF16 = jnp.bfloat16
HIGHEST = lax.Precision.HIGHEST
HIGH = lax.Precision.HIGH

D = 1024
GRID_W = 64
EPS = 1e-6
N_DEV = 8
C_GT, C_GV, C_NK, C_NV, C_SU, C_NQ, C_PU, C_GK, C_GG, C_GQ, C_END = 0, 1024, 1280, 1536, 1792, 2048, 2304, 2560, 2688, 2816, 2944
PW = 3072
N_CTX_ORIG = 416
N_IN = 2848
GLA_CHUNK = 128
S5_CHUNK = 256
ROW_TILE = 256
VMEM_LIMIT = 56 * 1024 * 1024

ADAM_LR, ADAM_B1, ADAM_B2, ADAM_EPS, ADAM_WD, ADAM_STEP = 0.001, 0.9, 0.999, 1e-08, 0.01, 10


def _cparams(**kw):
    return pltpu.CompilerParams(vmem_limit_bytes=VMEM_LIMIT, **kw)


def _dg(a, b, ca, cb, precision=None):
    return lax.dot_general(a, b, (((ca,), (cb,)), ((), ())), precision=precision, preferred_element_type=F32)


def hdot(a, b):
    return _dg(a, b, 1, 0, HIGHEST)


def hdot_nt(a, b):
    return _dg(a, b, 1, 1, HIGHEST)


def hdot_tn(a, b):
    return _dg(a, b, 0, 0, HIGHEST)


def mdot(a, b):
    return _dg(a, b, 1, 0, HIGH)


def mdot_nt(a, b):
    return _dg(a, b, 1, 1, HIGH)


def mdot_tn(a, b):
    return _dg(a, b, 0, 0, HIGH)


def b_nn(a, b):
    return _dg(a.astype(BF16), b.astype(BF16), 1, 0)


def b_nt(a, b):
    return _dg(a.astype(BF16), b.astype(BF16), 1, 1)


def b_tn(a, b):
    return _dg(a.astype(BF16), b.astype(BF16), 0, 0)


@jax.custom_vjp
def bdot(a, b):
    return b_nn(a, b)


def _bdot_fwd(a, b):
    return b_nn(a, b), (a, b)


def _bdot_bwd(res, ct):
    a, b = res
    return b_nt(ct, b).astype(a.dtype), b_tn(a, ct).astype(b.dtype)


bdot.defvjp(_bdot_fwd, _bdot_bwd)


def _log_sigmoid(z):
    return jnp.minimum(z, 0.0) - jnp.log(1.0 + jnp.exp(-jnp.abs(z)))


def _silu(z):
    return z * jax.nn.sigmoid(z)


def _gelu(z):
    return 0.5 * z * (1.0 + jnp.tanh(math.sqrt(2.0 / math.pi) * (z + 0.044715 * (z * z * z))))


def _cat(vals):
    return vals[0] if len(vals) == 1 else jnp.concatenate(vals, axis=-1)


def mm_nn(a_parts, b, name, tm=ROW_TILE, tn=1024):
    t = a_parts[0].shape[0]
    k, n = b.shape
    na = len(a_parts)
    tn = min(tn, n)

    def body(*refs):
        a = _cat([r[...].astype(BF16) for r in refs[:na]])
        refs[na + 1][...] = _dg(a, refs[na][...].astype(BF16), 1, 0)

    return pl.pallas_call(
        body, name=name, grid=(n // tn, t // tm),
        in_specs=[pl.BlockSpec((tm, p.shape[1]), lambda j, i: (i, 0)) for p in a_parts]
        + [pl.BlockSpec((k, tn), lambda j, i: (0, j))],
        out_specs=pl.BlockSpec((tm, tn), lambda j, i: (i, j)),
        out_shape=jax.ShapeDtypeStruct((t, n), F32),
        compiler_params=_cparams(dimension_semantics=("arbitrary", "arbitrary")),
    )(*a_parts, b)


def mm_nn_cols(a, b, start, widths, name, tm=ROW_TILE):
    t, k = a.shape
    tn = 1024
    assert start % tn == 0 and sum(widths) <= tn

    def body(a_ref, b_ref, *o_refs):
        r = _dg(a_ref[...].astype(BF16), b_ref[...].astype(BF16), 1, 0)
        off = 0
        for o_ref, w in zip(o_refs, widths):
            o_ref[...] = r[:, off:off + w]
            off += w

    return pl.pallas_call(
        body, name=name, grid=(t // tm,),
        in_specs=[pl.BlockSpec((tm, k), lambda i: (i, 0)), pl.BlockSpec((k, tn), lambda i: (0, start // tn))],
        out_specs=[pl.BlockSpec((tm, w), lambda i: (i, 0)) for w in widths],
        out_shape=[jax.ShapeDtypeStruct((t, w), F32) for w in widths],
        compiler_params=_cparams(dimension_semantics=("arbitrary",)),
    )(a, b)


def mm_nt(a_parts, b, name, tm=ROW_TILE):
    t = a_parts[0].shape[0]
    n, k = b.shape
    na = len(a_parts)

    def body(*refs):
        a = _cat([r[...].astype(BF16) for r in refs[:na]])
        refs[na + 1][...] = _dg(a, refs[na][...].astype(BF16), 1, 1)

    return pl.pallas_call(
        body, name=name, grid=(t // tm,),
        in_specs=[pl.BlockSpec((tm, p.shape[1]), lambda i: (i, 0)) for p in a_parts]
        + [pl.BlockSpec((n, k), lambda i: (0, 0))],
        out_specs=pl.BlockSpec((tm, n), lambda i: (i, 0)),
        out_shape=jax.ShapeDtypeStruct((t, n), F32),
        compiler_params=_cparams(dimension_semantics=("arbitrary",)),
    )(*a_parts, b)


def mm_tn(a, b_parts, name, tm=ROW_TILE, tn=1024, out_dtype=F32):
    t, k = a.shape
    widths = [p.shape[1] for p in b_parts]
    n = sum(widths)
    assert n % tn == 0
    groups, cur, acc = [], [], 0
    for idx, w in enumerate(widths):
        cur.append(idx)
        acc += w
        if acc == tn:
            groups.append(cur)
            cur, acc = [], 0
        assert acc < tn
    assert not cur
    outs = []
    for gi, grp in enumerate(groups):
        parts = [b_parts[i] for i in grp]
        npart = len(parts)
        nsteps = t // tm

        def body(*refs, npart=npart, nsteps=nsteps):
            a_v = refs[0][...].astype(BF16)
            b_v = _cat([r[...].astype(BF16) for r in refs[1:1 + npart]])
            o_ref, acc_ref = refs[1 + npart], refs[2 + npart]
            r = _dg(a_v, b_v, 0, 0)

            @pl.when(pl.program_id(0) == 0)
            def _():
                acc_ref[...] = r

            @pl.when(pl.program_id(0) != 0)
            def _():
                acc_ref[...] += r

            @pl.when(pl.program_id(0) == nsteps - 1)
            def _():
                o_ref[...] = acc_ref[...].astype(o_ref.dtype)

        outs.append(pl.pallas_call(
            body, name=f"{name}_{gi}", grid=(nsteps,),
            in_specs=[pl.BlockSpec((tm, k), lambda i: (i, 0))]
            + [pl.BlockSpec((tm, p.shape[1]), lambda i: (i, 0)) for p in parts],
            out_specs=pl.BlockSpec((k, tn), lambda i: (0, 0)),
            out_shape=jax.ShapeDtypeStruct((k, tn), out_dtype),
            scratch_shapes=[pltpu.VMEM((k, tn), F32)],
            compiler_params=_cparams(dimension_semantics=("arbitrary",)),
        )(a, *parts))
    return outs


def _seg_of(i, nct):
    return jnp.where(i < nct, 1, 0)


def rowwise_fwd(fn, name, rows, segs, globs, out_widths, tile, nct):
    t = rows[0].shape[0]
    nr, ns, ng = len(rows), len(segs), len(globs)

    def body(*refs):
        vals = [r[...] for r in refs[:nr]] + [r[0] for r in refs[nr:nr + ns]] + [r[...] for r in refs[nr + ns:nr + ns + ng]]
        outs = fn(*vals)
        for o_ref, o in zip(refs[nr + ns + ng:], outs):
            o_ref[...] = o

    return pl.pallas_call(
        body, name=name, grid=(t // tile,),
        in_specs=[pl.BlockSpec((tile, r.shape[1]), lambda i: (i, 0)) for r in rows]
        + [pl.BlockSpec((1, 1, s.shape[2]), lambda i: (_seg_of(i, nct), 0, 0)) for s in segs]
        + [pl.BlockSpec(g.shape, lambda i: (0, 0)) for g in globs],
        out_specs=[pl.BlockSpec((tile, w), lambda i: (i, 0)) for w in out_widths],
        out_shape=[jax.ShapeDtypeStruct((t, w), F32) for w in out_widths],
        compiler_params=_cparams(dimension_semantics=("arbitrary",)),
    )(*rows, *segs, *globs)


def rowwise_bwd(fn, name, rows, segs, globs, cts, tile, nct, row_diff, glob_diff):
    t = rows[0].shape[0]
    nr, ns, ng, nc = len(rows), len(segs), len(globs), len(cts)
    d_rows = [i for i in range(nr) if row_diff[i]]
    d_globs = [i for i in range(ng) if glob_diff[i]]

    def body(*refs):
        in_refs, out_refs = refs[:nr + ns + ng + nc], refs[nr + ns + ng + nc:]
        row_v = [r[...] for r in in_refs[:nr]]
        seg_v = [r[0] for r in in_refs[nr:nr + ns]]
        glob_v = [r[...] for r in in_refs[nr + ns:nr + ns + ng]]
        ct_v = tuple(r[...] for r in in_refs[nr + ns + ng:])

        def wrapped(dr, sv, dg):
            rv = list(row_v)
            for j, i in enumerate(d_rows):
                rv[i] = dr[j]
            gv = list(glob_v)
            for j, i in enumerate(d_globs):
                gv[i] = dg[j]
            return tuple(fn(*rv, *sv, *gv))

        _, vjp = jax.vjp(wrapped, [row_v[i] for i in d_rows], seg_v, [glob_v[i] for i in d_globs])
        c_rows, c_segs, c_globs = vjp(ct_v)
        i = pl.program_id(0)
        k = 0
        for c in c_rows:
            out_refs[k][...] = c
            k += 1
        seg_first = jnp.logical_or(i == 0, i == nct)
        for c in c_segs:
            ref = out_refs[k]
            k += 1

            @pl.when(seg_first)
            def _(ref=ref, c=c):
                ref[0] = c

            @pl.when(jnp.logical_not(seg_first))
            def _(ref=ref, c=c):
                ref[0] += c
        for c in c_globs:
            ref = out_refs[k]
            k += 1

            @pl.when(i == 0)
            def _(ref=ref, c=c):
                ref[...] = c

            @pl.when(i != 0)
            def _(ref=ref, c=c):
                ref[...] += c

    return pl.pallas_call(
        body, name=name, grid=(t // tile,),
        in_specs=[pl.BlockSpec((tile, r.shape[1]), lambda i: (i, 0)) for r in rows]
        + [pl.BlockSpec((1, 1, s.shape[2]), lambda i: (_seg_of(i, nct), 0, 0)) for s in segs]
        + [pl.BlockSpec(g.shape, lambda i: (0, 0)) for g in globs]
        + [pl.BlockSpec((tile, c.shape[1]), lambda i: (i, 0)) for c in cts],
        out_specs=[pl.BlockSpec((tile, rows[i].shape[1]), lambda i: (i, 0)) for i in d_rows]
        + [pl.BlockSpec((1, 1, s.shape[2]), lambda i: (_seg_of(i, nct), 0, 0)) for s in segs]
        + [pl.BlockSpec(globs[i].shape, lambda i: (0, 0)) for i in d_globs],
        out_shape=[jax.ShapeDtypeStruct(rows[i].shape, F32) for i in d_rows]
        + [jax.ShapeDtypeStruct(s.shape, F32) for s in segs]
        + [jax.ShapeDtypeStruct(globs[i].shape, F32) for i in d_globs],
        compiler_params=_cparams(dimension_semantics=("arbitrary",)),
    )(*rows, *segs, *globs, *cts)


def f_pre(x, mod, g_pre):
    shift, scale = mod[:, :D], mod[:, D:2 * D]
    rs = lax.rsqrt(jnp.mean(x * x, axis=-1, keepdims=True) + EPS)
    return ((x * rs) * g_pre * (1.0 + scale) + shift,)


def f_post(x, out, mod, g_post):
    gate = mod[:, 2 * D:]
    rs = lax.rsqrt(jnp.mean(out * out, axis=-1, keepdims=True) + EPS)
    return (x + gate * ((out * rs) * g_post),)


def f_mix(o_gla, o_na, y5, u5, pm, gcols, g_norm, s5_d, w_glu, b_glu, wpool, pool_scale, havg, e4):
    ms = mdot(o_gla * o_gla, havg)
    y_gla = o_gla * lax.rsqrt(ms + EPS) * jnp.sum(hdot(g_norm, e4), axis=0, keepdims=True)
    g = _gelu(u5 * s5_d + y5)
    y_s5 = g * jax.nn.sigmoid(bdot(g, w_glu) + b_glu)
    y_pool = bdot(pm, wpool) * pool_scale
    ycat = jnp.concatenate([y_gla, o_na, y_s5, y_pool], axis=-1)
    return (ycat * _silu(gcols),)


@jax.custom_vjp
def _rot_half16(x):
    lane = lax.broadcasted_iota(jnp.int32, x.shape, 1)
    first = jnp.bitwise_and(lane, 15) < 8
    return jnp.where(first, -pltpu.roll(x, x.shape[1] - 8, 1), pltpu.roll(x, 8, 1))


def _rot_fwd(x):
    return _rot_half16(x), None


def _rot_bwd(_, ct):
    return (-_rot_half16(ct),)


_rot_half16.defvjp(_rot_fwd, _rot_bwd)


def f_gla_prep(pk, pg, pq, cos, sin, wg, bg):
    z = bdot(pg, wg) + bg
    lg = _log_sigmoid(z) * (1.0 / 16.0)
    k_r = pk * cos + _rot_half16(pk) * sin
    q_r = (pq * cos + _rot_half16(pq) * sin) * (32.0 ** -0.5)
    return q_r, k_r, lg[:, :128], lg[:, 128:]


def _gla_consts(rev):
    c = GLA_CHUNK
    i = np.arange(c)
    inc = (i[None, :] >= i[:, None]) if rev else (i[None, :] <= i[:, None])
    mq = np.stack([(np.arange(128) // 32 == h) for h in range(4)]).astype(np.float32).reshape(4, 1, 128)
    mv = np.stack([(np.arange(256) // 64 == h) for h in range(4)]).astype(np.float32).reshape(4, 1, 256)
    bdt = (np.arange(256)[:, None] // 64 == np.arange(128)[None, :] // 32).astype(np.float32)
    inc = inc.astype(np.float32)
    return jnp.asarray(inc), jnp.asarray(inc.T.copy()), jnp.asarray(mq), jnp.asarray(mv), jnp.asarray(bdt)


def _stack_heads(x, m_ref):
    return jnp.concatenate([x * m_ref[h] for h in range(4)], axis=0)


def _tile4(m):
    return jnp.concatenate([m, m, m, m], axis=0)


def _fold_heads(r4, m_ref):
    r = r4.shape[0] // 4
    out = m_ref[0] * r4[0:r]
    for h in range(1, 4):
        out = out + m_ref[h] * r4[h * r:(h + 1) * r]
    return out


def _gla_chunk_of(s, n_ctx_chunks, n_chunks, rev):
    if not rev:
        return s
    return jnp.where(s < n_ctx_chunks, n_ctx_chunks - 1 - s, n_ctx_chunks + n_chunks - 1 - s)


def gla_scan_fwd(q, k, v, lg, acc, n_ctx_rows, rev, name, comm=None):
    t = q.shape[0]
    nch, ncc = t // GLA_CHUNK, n_ctx_rows // GLA_CHUNK
    inc, inc_t, mq, mv, bdt = _gla_consts(rev)

    def body(q_ref, k_ref, v_ref, lg_ref, acc_ref, inc_ref, inct_ref, mq_ref, mv_ref, bdt_ref, o_ref, st_ref):
        lmask, lmask_t = inc_ref[...], inct_ref[...]
        bd = bdt_ref[...]

        def step(s, st):
            c = _gla_chunk_of(s, ncc, nch, rev)
            rows = pl.ds(pl.multiple_of(c * GLA_CHUNK, GLA_CHUNK), GLA_CHUNK)
            qc, kc, vc, lgc = q_ref[rows, :], k_ref[rows, :], v_ref[rows, :], lg_ref[rows, :]
            st_ref[c] = st
            b = hdot(lmask, lgc)
            blast = jnp.sum(lgc, axis=0, keepdims=True)
            qe, ke, kd = qc * jnp.exp(b), kc * jnp.exp(-b), kc * jnp.exp(blast - b)
            ke4, v4 = _stack_heads(ke, mq_ref), _stack_heads(vc, mv_ref)
            at = _tile4(lmask_t) * b_nt(ke4, qe)
            o_ref[rows, :] = acc_ref[rows, :] + b_nt(qe, st) + b_tn(at, v4)
            return st * jnp.exp(blast) + bd * mdot_tn(vc, kd)

        lax.fori_loop(0, nch, step, jnp.zeros((256, 128), F32))

    return _call_with_exchange(body, name, [q, k, v, lg, acc, inc, inc_t, mq, mv, bdt],
                               [jax.ShapeDtypeStruct((t, 256), F32), jax.ShapeDtypeStruct((nch, 256, 128), F32)], comm)


def gla_scan_bwd(q, k, v, lg, st, do, acc, n_ctx_rows, rev, name, comm=None):
    t = q.shape[0]
    nch, ncc = t // GLA_CHUNK, n_ctx_rows // GLA_CHUNK
    inc, inc_t, mq, mv, bdt = _gla_consts(rev)

    def body(q_ref, k_ref, v_ref, lg_ref, st_ref, do_ref, aq_ref, ak_ref, av_ref, inc_ref, inct_ref, mq_ref, mv_ref, bdt_ref,
             dq_ref, dk_ref, dv_ref, dlg_ref):
        lmask, lmask_t = inc_ref[...], inct_ref[...]
        bd = bdt_ref[...]

        def step(j, carry):
            dst, gsum = carry
            s = nch - 1 - j
            c = _gla_chunk_of(s, ncc, nch, rev)
            rows = pl.ds(pl.multiple_of(c * GLA_CHUNK, GLA_CHUNK), GLA_CHUNK)
            qc, kc, vc, lgc, doc = q_ref[rows, :], k_ref[rows, :], v_ref[rows, :], lg_ref[rows, :], do_ref[rows, :]
            stc = st_ref[c]
            b = hdot(lmask, lgc)
            blast = jnp.sum(lgc, axis=0, keepdims=True)
            eb, enb, edb = jnp.exp(b), jnp.exp(-b), jnp.exp(blast - b)
            qe, ke, kd = qc * eb, kc * enb, kc * edb
            ke4, v4 = _stack_heads(ke, mq_ref), _stack_heads(vc, mv_ref)
            lm4 = _tile4(lmask_t)
            at = lm4 * b_nt(ke4, qe)
            dat = lm4 * mdot_nt(v4, doc)
            dqe = mdot(doc, stc) + mdot_tn(dat, ke4)
            dke = _fold_heads(mdot(dat, qe), mq_ref)
            dv = b_nt(kd, dst) + _fold_heads(b_nn(at, doc), mv_ref)
            dkd = mdot(vc, dst)
            dq = dqe * eb
            dk = dke * enb + dkd * edb
            g = qc * dq - kc * dk
            dlg_ref[rows, :] = hdot_tn(lmask, g) + gsum
            dq_ref[rows, :] = aq_ref[rows, :] + dq
            dk_ref[rows, :] = ak_ref[rows, :] + dk
            dv_ref[rows, :] = av_ref[rows, :] + dv
            dst_new = dst * jnp.exp(blast) + bd * mdot_tn(doc, qe)
            return dst_new, gsum + jnp.sum(g, axis=0, keepdims=True)

        lax.fori_loop(0, nch, step, (jnp.zeros((256, 128), F32), jnp.zeros((1, 128), F32)))

    return _call_with_exchange(body, name, [q, k, v, lg, st, do, *acc, inc, inc_t, mq, mv, bdt],
                               [jax.ShapeDtypeStruct((t, 128), F32), jax.ShapeDtypeStruct((t, 128), F32),
                                jax.ShapeDtypeStruct((t, 256), F32), jax.ShapeDtypeStruct((t, 128), F32)], comm)


def whole_fwd(fn, name, args, out_shapes):
    def body(*refs):
        outs = fn(*[r[...] for r in refs[:len(args)]])
        for o_ref, o in zip(refs[len(args):], outs):
            o_ref[...] = o

    vm = pl.BlockSpec(memory_space=pltpu.VMEM)
    return pl.pallas_call(
        body, name=name, in_specs=[vm] * len(args), out_specs=[vm] * len(out_shapes),
        out_shape=[jax.ShapeDtypeStruct(s, F32) for s in out_shapes], compiler_params=_cparams(),
    )(*args)


def whole_bwd(fn, name, args, cts, diff):
    d_idx = [i for i in range(len(args)) if diff[i]]

    def body(*refs):
        vals = [r[...] for r in refs[:len(args)]]
        ct_v = tuple(r[...] for r in refs[len(args):len(args) + len(cts)])

        def wrapped(dv):
            av = list(vals)
            for j, i in enumerate(d_idx):
                av[i] = dv[j]
            return tuple(fn(*av))

        _, vjp = jax.vjp(wrapped, [vals[i] for i in d_idx])
        (c_args,) = vjp(ct_v)
        for o_ref, c in zip(refs[len(args) + len(cts):], c_args):
            o_ref[...] = c

    vm = pl.BlockSpec(memory_space=pltpu.VMEM)
    return pl.pallas_call(
        body, name=name, in_specs=[vm] * (len(args) + len(cts)), out_specs=[vm] * len(d_idx),
        out_shape=[jax.ShapeDtypeStruct(args[i].shape, F32) for i in d_idx], compiler_params=_cparams(),
    )(*args, *cts)


def _s5_consts():
    e_rep = (np.arange(256)[:, None] // 16 == np.arange(16)[None, :]).astype(np.float32)
    e_tile = (np.arange(64)[:, None] == np.arange(1024)[None, :] % 64).astype(np.float32)
    gmask = (np.arange(16)[:, None] == np.arange(1024)[None, :] // 64).astype(np.float32)
    bdm = (np.arange(256)[:, None] // 16 == np.arange(1024)[None, :] // 64).astype(np.float32)
    return jnp.asarray(e_rep), jnp.asarray(e_tile), jnp.asarray(gmask), jnp.asarray(bdm)


def f_s5_params(lam_re, lam_im, log_dt, bt_re, bt_im, ct_re, ct_im, e_rep, e_tile, gmask, bdm):
    dt = jnp.exp(log_dt)
    mag = jnp.exp(lam_re * dt)
    ang = lam_im * dt
    lb_re, lb_im = mag * jnp.cos(ang), mag * jnp.sin(ang)
    num_re, num_im = lb_re - 1.0, lb_im
    den = lam_re * lam_re + lam_im * lam_im
    coef_re = (num_re * lam_re + num_im * lam_im) / den
    coef_im = (num_im * lam_re - num_re * lam_im) / den
    cr, ci = hdot(e_rep, coef_re), hdot(e_rep, coef_im)
    bbt_re = cr * bt_re - ci * bt_im
    bbt_im = cr * bt_im + ci * bt_re
    a_re = jnp.sum(hdot(lb_re, e_tile) * gmask, axis=0, keepdims=True)
    a_im = jnp.sum(hdot(lb_im, e_tile) * gmask, axis=0, keepdims=True)
    return (a_re, a_im, hdot(bbt_re, e_tile) * bdm, hdot(bbt_im, e_tile) * bdm,
            hdot(ct_re, e_tile) * bdm, hdot(ct_im, e_tile) * bdm)


def _s5_doubling(xr, xi, pr, pi, pos, n, steps, rev):
    rows = xr.shape[0]
    for s in steps:
        if rev:
            keep = pos < (n - s)
            sr, si = pltpu.roll(xr, rows - s, 0), pltpu.roll(xi, rows - s, 0)
        else:
            keep = pos >= s
            sr, si = pltpu.roll(xr, s, 0), pltpu.roll(xi, s, 0)
        sr, si = jnp.where(keep, sr, 0.0), jnp.where(keep, si, 0.0)
        xr, xi = xr + pr * sr - pi * si, xi + pr * si + pi * sr
        pr, pi = pr * pr - pi * pi, 2.0 * pr * pi
    return xr, xi, pr, pi


SUBLANES = 8


def _s5_scan(xr, xi, a_re, a_im, rev, chunk, scr):
    xs_r, xs_i, yp_r, yp_i = scr
    ng = chunk // SUBLANES
    x3r, x3i = xr.reshape(ng, SUBLANES, 1024), xi.reshape(ng, SUBLANES, 1024)
    sub = lax.broadcasted_iota(jnp.int32, (SUBLANES, 1024), 0)
    a8r, a8i = a_re, a_im
    for s in (1, 2, 4):
        keep = sub < (SUBLANES - s) if rev else sub >= s
        mr, mi = jnp.where(keep, a8r, 0.0)[None], jnp.where(keep, a8i, 0.0)[None]
        shift = SUBLANES - s if rev else s
        sr, si = pltpu.roll(x3r, shift, 1), pltpu.roll(x3i, shift, 1)
        x3r, x3i = x3r + mr * sr - mi * si, x3i + mr * si + mi * sr
        a8r, a8i = a8r * a8r - a8i * a8i, 2.0 * a8r * a8i
    xr, xi = x3r.reshape(chunk, 1024), x3i.reshape(chunk, 1024)
    nblk = 1024 // 128
    for j in range(nblk):
        xs_r[j] = xr[:, 128 * j:128 * (j + 1)]
        xs_i[j] = xi[:, 128 * j:128 * (j + 1)]
    edge = pl.ds(0 if rev else SUBLANES - 1, ng, stride=SUBLANES)
    gr = jnp.concatenate([xs_r[j, edge, :] for j in range(nblk)], axis=-1)
    gi = jnp.concatenate([xs_i[j, edge, :] for j in range(nblk)], axis=-1)
    grow = lax.broadcasted_iota(jnp.int32, (ng, 1024), 0)
    steps = tuple(1 << k for k in range((ng - 1).bit_length()))
    gr, gi, _, _ = _s5_doubling(gr, gi, a8r, a8i, grow, ng, steps, rev)
    if rev:
        yp_r[...] = jnp.where(grow < ng - 1, pltpu.roll(gr, ng - 1, 0), 0.0)
        yp_i[...] = jnp.where(grow < ng - 1, pltpu.roll(gi, ng - 1, 0), 0.0)
    else:
        yp_r[...] = jnp.where(grow >= 1, pltpu.roll(gr, 1, 0), 0.0)
        yp_i[...] = jnp.where(grow >= 1, pltpu.roll(gi, 1, 0), 0.0)
    sub = lax.broadcasted_iota(jnp.int32, (SUBLANES, 1024), 0)
    tr, ti = jnp.zeros((SUBLANES, 1024), F32), jnp.zeros((SUBLANES, 1024), F32)
    cr, ci = a_re, a_im
    for n in range(1, SUBLANES + 1):
        r = SUBLANES - n if rev else n - 1
        tr, ti = jnp.where(sub == r, cr, tr), jnp.where(sub == r, ci, ti)
        cr, ci = cr * a_re - ci * a_im, cr * a_im + ci * a_re
    for j in range(nblk):
        lanes = slice(128 * j, 128 * (j + 1))
        tr_j, ti_j = tr[:, lanes], ti[:, lanes]
        for g in range(ng):
            rows = slice(g * SUBLANES, (g + 1) * SUBLANES)
            er, ei = yp_r[g:g + 1, lanes], yp_i[g:g + 1, lanes]
            xs_r[j, rows, :] = xs_r[j, rows, :] + tr_j * er - ti_j * ei
            xs_i[j, rows, :] = xs_i[j, rows, :] + tr_j * ei + ti_j * er
    return (jnp.concatenate([xs_r[j] for j in range(nblk)], axis=-1),
            jnp.concatenate([xs_i[j] for j in range(nblk)], axis=-1))


def _s5_scratch(chunk):
    return [pltpu.VMEM((8, chunk, 128), F32), pltpu.VMEM((8, chunk, 128), F32),
            pltpu.VMEM((chunk // SUBLANES, 1024), F32), pltpu.VMEM((chunk // SUBLANES, 1024), F32)]


def _s5_chunk_states(u_c, x0r, x0i, a_re, a_im, bb_re, bb_im, rev, chunk, scr):
    row = lax.broadcasted_iota(jnp.int32, (chunk, 1024), 0)
    first = row == (chunk - 1 if rev else 0)
    inj_r = a_re * x0r - a_im * x0i
    inj_i = a_re * x0i + a_im * x0r
    xr = b_nn(u_c, bb_re) + jnp.where(first, inj_r, 0.0)
    xi = b_nn(u_c, bb_im) + jnp.where(first, inj_i, 0.0)
    return _s5_scan(xr, xi, a_re, a_im, rev, chunk, scr)


def _row_pick(x, idx):
    row = lax.broadcasted_iota(jnp.int32, x.shape, 0)
    return jnp.sum(jnp.where(row == idx, x, 0.0), axis=0, keepdims=True)


def s5_scan_fwd(u, acc, a_re, a_im, bb_re, bb_im, cc_re, cc_im, n_ctx_rows, chunk, rev, name):
    t = u.shape[0]
    nch, ncc = t // chunk, n_ctx_rows // chunk

    def body(u_ref, acc_ref, ar_ref, ai_ref, br_ref, bi_ref, cr_ref, ci_ref, y_ref, x0r_ref, x0i_ref, xsr_ref, xsi_ref, *scr):
        a_r, a_i = ar_ref[...], ai_ref[...]

        def step(s, carry):
            x0r, x0i = carry
            c = _gla_chunk_of(s, ncc, nch, rev)
            rows = pl.ds(pl.multiple_of(c * chunk, chunk), chunk)
            x0r_ref[c] = x0r
            x0i_ref[c] = x0i
            xr, xi = _s5_chunk_states(u_ref[rows, :], x0r, x0i, a_r, a_i, br_ref[...], bi_ref[...], rev, chunk, scr)
            y_ref[rows, :] = acc_ref[rows, :] + b_nt(xr, cr_ref[...]) - b_nt(xi, ci_ref[...])
            xsr_ref[rows, :] = xr.astype(BF16)
            xsi_ref[rows, :] = xi.astype(BF16)
            last = 0 if rev else chunk - 1
            return _row_pick(xr, last), _row_pick(xi, last)

        lax.fori_loop(0, nch, step, (jnp.zeros((1, 1024), F32), jnp.zeros((1, 1024), F32)))

    vm = pl.BlockSpec(memory_space=pltpu.VMEM)
    return pl.pallas_call(
        body, name=name, in_specs=[vm] * 8, out_specs=[vm] * 5,
        out_shape=[jax.ShapeDtypeStruct((t, 256), F32), jax.ShapeDtypeStruct((nch, 1, 1024), F32),
                   jax.ShapeDtypeStruct((nch, 1, 1024), F32), jax.ShapeDtypeStruct((t, 1024), BF16),
                   jax.ShapeDtypeStruct((t, 1024), BF16)],
        scratch_shapes=_s5_scratch(chunk), compiler_params=_cparams(),
    )(u, acc, a_re, a_im, bb_re, bb_im, cc_re, cc_im)


def s5_scan_bwd(u, dy, du_acc, x0r, x0i, xsr, xsi, a_re, a_im, bb_re, bb_im, cc_re, cc_im, n_ctx_rows, chunk, rev, name):
    t = u.shape[0]
    nch, ncc = t // chunk, n_ctx_rows // chunk

    def body(u_ref, dy_ref, dua_ref, x0r_ref, x0i_ref, xsr_ref, xsi_ref, ar_ref, ai_ref, br_ref, bi_ref, cr_ref, ci_ref,
             du_ref, dar_ref, dai_ref, dbr_ref, dbi_ref, dcr_ref, dci_ref, *scr):
        a_r, a_i = ar_ref[...], ai_ref[...]
        for ref in (dbr_ref, dbi_ref, dcr_ref, dci_ref):
            ref[...] = jnp.zeros_like(ref)
        row = lax.broadcasted_iota(jnp.int32, (chunk, 1024), 0)
        first_idx, last_idx = (chunk - 1, 0) if rev else (0, chunk - 1)

        def step(j, carry):
            lcr, lci, dar, dai = carry
            s = nch - 1 - j
            c = _gla_chunk_of(s, ncc, nch, rev)
            rows = pl.ds(pl.multiple_of(c * chunk, chunk), chunk)
            u_c, dy_c = u_ref[rows, :], dy_ref[rows, :]
            x0r_c, x0i_c = x0r_ref[c], x0i_ref[c]
            xr, xi = xsr_ref[rows, :].astype(F32), xsi_ref[rows, :].astype(F32)
            dcr_ref[...] += b_tn(dy_c, xr)
            dci_ref[...] -= b_tn(dy_c, xi)
            inj_r = a_r * lcr + a_i * lci
            inj_i = a_r * lci - a_i * lcr
            is_last = row == last_idx
            lr = b_nn(dy_c, cr_ref[...]) + jnp.where(is_last, inj_r, 0.0)
            li = -b_nn(dy_c, ci_ref[...]) + jnp.where(is_last, inj_i, 0.0)
            lr, li = _s5_scan(lr, li, a_r, -a_i, not rev, chunk, scr)
            du_ref[rows, :] = dua_ref[rows, :] + b_nt(lr, br_ref[...]) + b_nt(li, bi_ref[...])
            dbr_ref[...] += b_tn(u_c, lr)
            dbi_ref[...] += b_tn(u_c, li)
            if rev:
                pr, pi = pltpu.roll(xr, chunk - 1, 0), pltpu.roll(xi, chunk - 1, 0)
            else:
                pr, pi = pltpu.roll(xr, 1, 0), pltpu.roll(xi, 1, 0)
            is_first = row == first_idx
            pr, pi = jnp.where(is_first, x0r_c, pr), jnp.where(is_first, x0i_c, pi)
            dar = dar + jnp.sum(lr * pr + li * pi, axis=0, keepdims=True)
            dai = dai + jnp.sum(li * pr - lr * pi, axis=0, keepdims=True)
            return _row_pick(lr, first_idx), _row_pick(li, first_idx), dar, dai

        z = jnp.zeros((1, 1024), F32)
        _, _, dar, dai = lax.fori_loop(0, nch, step, (z, z, z, z))
        dar_ref[...] = dar
        dai_ref[...] = dai

    vm = pl.BlockSpec(memory_space=pltpu.VMEM)
    big = jax.ShapeDtypeStruct((256, 1024), F32)
    vec = jax.ShapeDtypeStruct((1, 1024), F32)
    return pl.pallas_call(
        body, name=name, in_specs=[vm] * 13, out_specs=[vm] * 7,
        out_shape=[jax.ShapeDtypeStruct((t, 256), F32), vec, vec, big, big, big, big],
        scratch_shapes=_s5_scratch(chunk), compiler_params=_cparams(),
    )(u, dy, du_acc, x0r, x0i, xsr, xsi, a_re, a_im, bb_re, bb_im, cc_re, cc_im)


POOL_HALO = 8


def pool_apply(u_pad, n, transpose, name, tile=ROW_TILE):
    tile = min(tile, n)
    ext = tile + 2 * POOL_HALO
    trel = np.arange(ext)[None, :] - POOL_HALO - np.arange(tile)[:, None]
    if transpose:
        trel = -trel
    band4 = np.concatenate([((trel >= -(1 << w)) & (trel <= (1 << w) - 1)) for w in range(4)], axis=0).astype(np.float32)

    def body(u_ref, band_ref, lm_ref, o_ref):
        lax.fori_loop(0, n // tile, functools.partial(step, u_ref, band_ref, lm_ref, o_ref), 0)

    def step(u_ref, band_ref, lm_ref, o_ref, i, carry):
        val = u_ref[pl.ds(pl.multiple_of(i * tile, tile), ext), :]
        lane = lax.broadcasted_iota(jnp.int32, (ext, 256), 1)
        half = jnp.left_shift(1, jnp.right_shift(lane, 6))
        trow = lax.broadcasted_iota(jnp.int32, (ext, 256), 0) + (i * tile - POOL_HALO)
        cnt = jnp.minimum(trow + half, n) - jnp.maximum(trow - half, 0)
        inv = 1.0 / jnp.maximum(cnt, 1).astype(F32)
        src = val * inv if transpose else val
        acc = _fold_heads(mdot(band_ref[...], src), lm_ref)
        centre = val[POOL_HALO:POOL_HALO + tile]
        if not transpose:
            acc = acc * inv[POOL_HALO:POOL_HALO + tile]
        o_ref[pl.ds(pl.multiple_of(i * tile, tile), tile), :] = acc - centre
        return carry

    vm = pl.BlockSpec(memory_space=pltpu.VMEM)
    return pl.pallas_call(
        body, name=name, in_specs=[vm] * 3, out_specs=vm,
        out_shape=jax.ShapeDtypeStruct((n, 256), F32), compiler_params=_cparams(),
    )(u_pad, jnp.asarray(band4), _na_head_masks())


NA_SCALE = 64.0 ** -0.5
NEG = -1e30


def _call_with_exchange(compute, name, args, out_shapes, comm):
    vm = pl.BlockSpec(memory_space=pltpu.VMEM)
    n_in, n_out = len(args), len(out_shapes)
    if comm is None:
        outs = pl.pallas_call(compute, name=name, in_specs=[vm] * n_in, out_specs=[vm] * n_out, out_shape=out_shapes,
                              compiler_params=_cparams())(*args)
        return outs, None
    arrays, scatter = comm
    n = len(arrays)

    def body(*refs):
        c_in = refs[n_in:n_in + n]
        c_out = refs[n_in + n + n_out:n_in + 2 * n + n_out]
        finish = _exchange_issue(c_in, c_out, scatter, *refs[n_in + 2 * n + n_out:])
        compute(*refs[:n_in], *refs[n_in + n:n_in + n + n_out])
        finish()

    hbm = pl.BlockSpec(memory_space=pl.ANY)
    outs = pl.pallas_call(
        body, name=name, in_specs=[vm] * n_in + [hbm] * n, out_specs=[vm] * n_out + [hbm] * n,
        out_shape=list(out_shapes) + _exchange_out_shapes(arrays, scatter), scratch_shapes=_exchange_sems(n),
        compiler_params=_cparams(has_side_effects=True),
    )(*args, *arrays)
    return outs[:n_out], outs[n_out:]


def _na_head_masks():
    return jnp.asarray(np.stack([(np.arange(256) // 64 == h) for h in range(4)]).astype(np.float32).reshape(4, 1, 256))


def _na_window(r, rows):
    start = jnp.clip(r - 4, 0, rows - 8)
    return start, start - r + 7


def _na_probs(qh, kw, kc, bias):
    s_c = b_nt(qh, kc)
    m = jnp.max(s_c, axis=-1, keepdims=True)
    if kw is not None:
        s_w = b_nt(qh, kw) + bias
        m = jnp.maximum(m, jnp.max(s_w, axis=-1, keepdims=True))
        p_w = jnp.exp(s_w - m)
    p_c = jnp.exp(s_c - m)
    l = jnp.sum(p_c, axis=-1, keepdims=True)
    if kw is not None:
        l = l + jnp.sum(p_w, axis=-1, keepdims=True)
        return p_w / l, p_c / l
    return None, p_c / l


def na_fwd(q, k, v, bias8, n_ctx_rows, name, comm=None):
    t = q.shape[0]
    m_ctx = n_ctx_rows
    rows = (t - m_ctx) // GRID_W
    hm = _na_head_masks()

    def body(q_ref, k_ref, v_ref, b_ref, hm_ref, o_ref):
        kc, vc = k_ref[0:m_ctx, :], v_ref[0:m_ctx, :]

        def ctx_step(i, _):
            rs = pl.ds(pl.multiple_of(i * 64, 64), 64)
            q4 = _stack_heads(q_ref[rs, :] * NA_SCALE, hm_ref)
            _, p_c = _na_probs(q4, None, kc, None)
            o_ref[rs, :] = _fold_heads(b_nn(p_c, vc), hm_ref)
            return 0

        lax.fori_loop(0, m_ctx // 64, ctx_step, 0)

        def lat_step(r, _):
            start, off = _na_window(r, rows)
            rs = pl.ds(pl.multiple_of(m_ctx + r * 64, 64), 64)
            ws = pl.ds(pl.multiple_of(m_ctx + start * 64, 64), 512)
            q4 = _stack_heads(q_ref[rs, :] * NA_SCALE, hm_ref)
            kw, vw = k_ref[ws, :], v_ref[ws, :]
            p_w, p_c = _na_probs(q4, kw, kc, b_ref[off])
            o_ref[rs, :] = _fold_heads(b_nn(p_w, vw) + b_nn(p_c, vc), hm_ref)
            return 0

        lax.fori_loop(0, rows, lat_step, 0)

    (o,), received = _call_with_exchange(body, name, [q, k, v, bias8, hm], [jax.ShapeDtypeStruct((t, 256), F32)], comm)
    return o if comm is None else (o, received)


def na_bwd(q, k, v, do, bias8, n_ctx_rows, name, comm=None):
    t = q.shape[0]
    m_ctx = n_ctx_rows
    rows = (t - m_ctx) // GRID_W
    hm = _na_head_masks()

    def body(q_ref, k_ref, v_ref, do_ref, b_ref, hm_ref, dq_ref, dk_ref, dv_ref, db_ref):
        kc, vc = k_ref[0:m_ctx, :], v_ref[0:m_ctx, :]
        dk_ref[...] = jnp.zeros_like(dk_ref)
        dv_ref[...] = jnp.zeros_like(dv_ref)
        db_ref[...] = jnp.zeros_like(db_ref)

        def head_terms(qh, doh, kw, vw, bias):
            p_w, p_c = _na_probs(qh, kw, kc, bias)
            dp_c = b_nt(doh, vc)
            delta = jnp.sum(p_c * dp_c, axis=-1, keepdims=True)
            if kw is not None:
                dp_w = b_nt(doh, vw)
                delta = delta + jnp.sum(p_w * dp_w, axis=-1, keepdims=True)
                ds_w = p_w * (dp_w - delta)
            else:
                ds_w = None
            ds_c = p_c * (dp_c - delta)
            return p_w, p_c, ds_w, ds_c

        def ctx_step(i, carry):
            dkc, dvc = carry
            rs = pl.ds(pl.multiple_of(i * 64, 64), 64)
            q4, do4 = _stack_heads(q_ref[rs, :] * NA_SCALE, hm_ref), _stack_heads(do_ref[rs, :], hm_ref)
            _, p_c, _, ds_c = head_terms(q4, do4, None, None, None)
            dq_ref[rs, :] = _fold_heads(b_nn(ds_c, kc), hm_ref) * NA_SCALE
            return dkc + b_tn(ds_c, q4), dvc + b_tn(p_c, do4)

        zc = jnp.zeros((m_ctx, 256), F32)
        carry = lax.fori_loop(0, m_ctx // 64, ctx_step, (zc, zc))

        def lat_step(r, carry):
            dkc, dvc = carry
            start, off = _na_window(r, rows)
            rs = pl.ds(pl.multiple_of(m_ctx + r * 64, 64), 64)
            ws = pl.ds(pl.multiple_of(m_ctx + start * 64, 64), 512)
            q4, do4 = _stack_heads(q_ref[rs, :] * NA_SCALE, hm_ref), _stack_heads(do_ref[rs, :], hm_ref)
            kw, vw = k_ref[ws, :], v_ref[ws, :]
            p_w, p_c, ds_w, ds_c = head_terms(q4, do4, kw, vw, b_ref[off])
            dq_ref[rs, :] = _fold_heads(b_nn(ds_w, kw) + b_nn(ds_c, kc), hm_ref) * NA_SCALE
            dk_ref[ws, :] += b_tn(ds_w, q4)
            dv_ref[ws, :] += b_tn(p_w, do4)
            db_ref[off] += ds_w
            return dkc + b_tn(ds_c, q4), dvc + b_tn(p_c, do4)

        dkc, dvc = lax.fori_loop(0, rows, lat_step, carry)
        dk_ref[0:m_ctx, :] = dkc
        dv_ref[0:m_ctx, :] = dvc

    row = jax.ShapeDtypeStruct((t, 256), F32)
    return _call_with_exchange(body, name, [q, k, v, do, bias8, hm], [row, row, row, jax.ShapeDtypeStruct(bias8.shape, F32)], comm)


def _na_toeplitz():
    col = np.arange(GRID_W)
    dd = (col[None, :] - col[:, None] + 15).reshape(-1)
    tt = np.zeros((GRID_W * GRID_W, 128), np.float32)
    ok = (dd >= 0) & (dd <= 30)
    tt[np.arange(GRID_W * GRID_W)[ok], dd[ok]] = 1.0
    return tt


def _na_bias8(rpb, name):
    col = np.arange(GRID_W)
    cs = np.clip(col - 8, 0, GRID_W - 16)
    col_mask = (col[None, :] >= cs[:, None]) & (col[None, :] < cs[:, None] + 16)
    rpb2 = jnp.pad(rpb.reshape(60, 31), ((0, 4), (0, 97)))
    (toe,) = whole_fwd(lambda r_, t_: (hdot_nt(r_, t_),), name, [rpb2, jnp.asarray(_na_toeplitz())], [(64, GRID_W * GRID_W)])
    toe = toe[:60].reshape(4, 15, GRID_W, GRID_W)
    b = jnp.stack([toe[:, off:off + 8] for off in range(8)], axis=1)
    b = jnp.where(jnp.asarray(col_mask)[None, None, None], b, NEG)
    return b.transpose(1, 0, 3, 2, 4).reshape(8, 4 * GRID_W, 8 * GRID_W)


def _na_rpb_grad(dbias8, name):
    tt = _na_toeplitz()
    sel = np.zeros((64, 256), np.float32)
    for h in range(4):
        for off in range(8):
            for i in range(8):
                sel[h * 15 + off + i, h * 64 + off * 8 + i] = 1.0
    a2 = dbias8.reshape(8, 4, GRID_W, 8, GRID_W).transpose(1, 0, 3, 2, 4).reshape(256, GRID_W * GRID_W)
    (out,) = whole_fwd(lambda a, t_, s_: (hdot(s_, hdot(a, t_)),), name, [a2, jnp.asarray(tt), jnp.asarray(sel)], [(64, 128)])
    return out[:60, :31].reshape(4, 15, 31)


def f_mod(cs, b_mod, w_mod):
    s = _silu(cs)
    return bdot(s, w_mod) + b_mod, s


def loss_and_grad(z, tgt, n_ctx_rows, name, tile=ROW_TILE):
    t, d = z.shape
    tile = min(tile, n_ctx_rows)
    nct = n_ctx_rows // tile

    def body(z_ref, t_ref, dz_ref, loss_ref):
        i = pl.program_id(0)

        @pl.when(i == 0)
        def _():
            loss_ref[...] = jnp.zeros_like(loss_ref)

        @pl.when(i < nct)
        def _():
            dz_ref[...] = jnp.zeros_like(dz_ref)

        @pl.when(i >= nct)
        def _():
            diff = z_ref[...] - t_ref[...]
            dz_ref[...] = diff * (1.0 / d)
            loss_ref[...] += 0.5 * jnp.sum(jnp.sum(diff * diff, axis=-1, keepdims=True) * (1.0 / d), axis=0, keepdims=True)

    dz, loss = pl.pallas_call(
        body, name=name, grid=(t // tile,),
        in_specs=[pl.BlockSpec((tile, d), lambda i: (i, 0)),
                  pl.BlockSpec((tile, d), lambda i: (jnp.maximum(i - nct, 0), 0))],
        out_specs=[pl.BlockSpec((tile, d), lambda i: (i, 0)), pl.BlockSpec((8, 128), lambda i: (0, 0))],
        out_shape=[jax.ShapeDtypeStruct((t, d), F32), jax.ShapeDtypeStruct((8, 128), F32)],
        compiler_params=_cparams(dimension_semantics=("arbitrary",)),
    )(z, tgt)
    return loss[0, 0], dz


def adamw(parts, w, m, v, name, tile=256):
    npart, r, c = parts.shape
    tile = min(tile, r)
    assert r % tile == 0
    c1 = 1.0 / (1.0 - ADAM_B1 ** ADAM_STEP)
    c2 = 1.0 / (1.0 - ADAM_B2 ** ADAM_STEP)

    def body(p_ref, w_ref, m_ref, v_ref, g_ref, d_ref, nm_ref, nv_ref):
        g = p_ref[0].astype(F32)
        for i in range(1, npart):
            g = g + p_ref[i].astype(F32)
        nm = ADAM_B1 * m_ref[...] + (1.0 - ADAM_B1) * g
        nv = ADAM_B2 * v_ref[...] + (1.0 - ADAM_B2) * (g * g)
        g_ref[...] = g
        nm_ref[...] = nm
        nv_ref[...] = nv
        d_ref[...] = -ADAM_LR * ((nm * c1) / (jnp.sqrt(nv * c2) + ADAM_EPS) + ADAM_WD * w_ref[...])

    blk = pl.BlockSpec((tile, c), lambda i: (i, 0))
    return pl.pallas_call(
        body, name=name, grid=(r // tile,),
        in_specs=[pl.BlockSpec((npart, tile, c), lambda i: (0, i, 0)), blk, blk, blk],
        out_specs=[blk] * 4, out_shape=[jax.ShapeDtypeStruct((r, c), F32)] * 4,
        compiler_params=_cparams(dimension_semantics=("arbitrary",)),
    )(parts, w, m, v)


def _peer(x, y, c, k):
    return (1 - x if k & 4 else x, 1 - y if k & 2 else y, 1 - c if k & 1 else c)


def _exchange_out_shapes(arrays, scatter):
    return [jax.ShapeDtypeStruct(a.shape if s else (N_DEV,) + a.shape, a.dtype) for a, s in zip(arrays, scatter)]


def _exchange_sems(n):
    return [pltpu.SemaphoreType.DMA((n, N_DEV - 1)), pltpu.SemaphoreType.DMA((n, N_DEV - 1)), pltpu.SemaphoreType.DMA((n,))]


def _exchange_issue(ins, outs, scatter, send_sems, recv_sems, local_sems):
    n = len(ins)
    x, y, c = lax.axis_index("x"), lax.axis_index("y"), lax.axis_index("c")
    me = 4 * x + 2 * y + c

    def index_of(p):
        return 4 * p[0] + 2 * p[1] + p[2]

    local = []
    for a in range(n):
        src_me = ins[a].at[me] if scatter[a] else ins[a]
        loc = pltpu.make_async_copy(src_me, outs[a].at[me], local_sems.at[a])
        loc.start()
        local.append(loc)
    for k in range(1, N_DEV):
        peer = _peer(x, y, c, k)
        for a in range(n):
            src = ins[a].at[index_of(peer)] if scatter[a] else ins[a]
            pltpu.make_async_remote_copy(
                src_ref=src, dst_ref=outs[a].at[me], send_sem=send_sems.at[a, k - 1], recv_sem=recv_sems.at[a, k - 1],
                device_id=peer, device_id_type=pl.DeviceIdType.MESH).start()

    def finish():
        for k in range(1, N_DEV):
            peer = _peer(x, y, c, k)
            for a in range(n):
                src = ins[a].at[index_of(peer)] if scatter[a] else ins[a]
                cp = pltpu.make_async_remote_copy(
                    src_ref=src, dst_ref=outs[a].at[index_of(peer)], send_sem=send_sems.at[a, k - 1],
                    recv_sem=recv_sems.at[a, k - 1], device_id=peer, device_id_type=pl.DeviceIdType.MESH)
                cp.wait_send()
                cp.wait_recv()
        for loc in local:
            loc.wait()

    return finish


def exchange(arrays, scatter, name):
    n = len(arrays)

    def body(*refs):
        _exchange_issue(refs[:n], refs[n:2 * n], scatter, *refs[2 * n:])()

    hbm = pl.BlockSpec(memory_space=pl.ANY)
    return pl.pallas_call(
        body, name=name, in_specs=[hbm] * n, out_specs=[hbm] * n, out_shape=_exchange_out_shapes(arrays, scatter),
        scratch_shapes=_exchange_sems(n), compiler_params=pltpu.CompilerParams(has_side_effects=True),
    )(*arrays)


def _rope_tables(n_lat, n_ctx):
    tok = np.arange(n_lat)
    freqs = 10000.0 ** (-np.arange(0, 16, 2, dtype=np.float32) / 16.0)

    def table(pos):
        ang = pos.astype(np.float32)[:, None] * freqs[None, :]
        ang = np.concatenate([ang, ang], axis=-1)
        return np.cos(ang), np.sin(ang)

    cr, sr = table(tok // GRID_W)
    cc, sc = table(tok % GRID_W)
    cos = np.tile(np.concatenate([cr, cc], axis=-1), (1, 4))
    sin = np.tile(np.concatenate([sr, sc], axis=-1), (1, 4))
    cos = np.concatenate([np.ones((n_ctx, 128), np.float32), cos], axis=0)
    sin = np.concatenate([np.zeros((n_ctx, 128), np.float32), sin], axis=0)
    return jnp.asarray(cos, F32), jnp.asarray(sin, F32)


def _pad_w_in(w):
    z = lambda n: jnp.zeros((w.shape[0], n), w.dtype)
    return jnp.concatenate([w[:, 1824:2848], w[:, 128:384], w[:, 416:672], w[:, 672:928], w[:, 928:1184], w[:, 1312:1568],
                            w[:, 1568:1824], w[:, 0:128], w[:, 384:416], z(96), w[:, 1184:1312], z(128)], axis=1)


def _unpad_w_in(wp):
    return jnp.concatenate([wp[:, C_GK:C_GK + 128], wp[:, C_GV:C_GV + 256], wp[:, C_GG:C_GG + 32], wp[:, C_NK:C_NK + 256],
                            wp[:, C_NV:C_NV + 256], wp[:, C_SU:C_SU + 256], wp[:, C_GQ:C_GQ + 128], wp[:, C_NQ:C_NQ + 256],
                            wp[:, C_PU:C_PU + 256], wp[:, C_GT:C_GT + 1024]], axis=1)


def _pad_rows(u):
    return jnp.pad(u, ((POOL_HALO, POOL_HALO), (0, 0)))


def _block_diag4(w):
    out = jnp.zeros((256, 256), w.dtype)
    for i in range(4):
        out = lax.dynamic_update_slice(out, w[i], (64 * i, 64 * i))
    return out


def _layer_params(p, big, l):
    e_rep, e_tile, gmask, bdm = _s5_consts()
    wg = jnp.zeros((128, 256), F32)
    wg = lax.dynamic_update_slice(wg, p["gla_w_gate"][l, 0], (0, 0))
    wg = lax.dynamic_update_slice(wg, p["gla_w_gate"][l, 1], (16, 128))
    s5 = []
    for d in range(2):
        s5.append([p["s5_lam_re"][l, d], p["s5_lam_im"][l, d], p["s5_log_dt"][l, d].reshape(16, 1),
                   p["s5_b_re"][l, d].transpose(0, 2, 1).reshape(256, 64), p["s5_b_im"][l, d].transpose(0, 2, 1).reshape(256, 64),
                   p["s5_c_re"][l, d].reshape(256, 64), p["s5_c_im"][l, d].reshape(256, 64), e_rep, e_tile, gmask, bdm])
    havg = jnp.asarray((np.arange(256)[:, None] // 64 == np.arange(256)[None, :] // 64).astype(np.float32) / 64.0)
    e4 = jnp.asarray((np.arange(64)[:, None] == np.arange(256)[None, :] % 64).astype(np.float32))
    return dict(
        g_pre=p["g_pre"][l].reshape(1, D), g_post=p["g_post"][l].reshape(1, D), b_mod=p["b_mod"][l].reshape(1, 3 * D),
        w_mod=big["w_mod"], w_in=_pad_w_in(big["w_in"]), w_out=big["w_out"],
        wg=wg, bg=p["gla_b_gate"][l].reshape(1, 256), g_norm=jnp.pad(p["gla_g_norm"][l].reshape(1, 64), ((0, 7), (0, 0))),
        bias8=_na_bias8(p["na_rpb"][l], f"na_bias_l{l}"), s5=s5, s5_d=p["s5_d"][l].reshape(1, 256), w_glu=big["s5_w_glu"].astype(F32),
        b_glu=p["s5_b_glu"][l].reshape(1, 256), wpool=_block_diag4(p["pool_w"][l]), pool_scale=p["pool_scale"][l].reshape(1, 256),
        havg=havg, e4=e4)


def _cols(pz, start, width):
    return pz[:, start:start + width]


def _layer_fwd(z, modseg, lp, cos, sin, m_ctx, tile, s5_chunk, l, comm=None):
    t = z.shape[0]
    nct = m_ctx // tile
    nm = lambda s: f"{s}_l{l}"
    (h,) = rowwise_fwd(f_pre, nm("pre"), [z], [modseg], [lp["g_pre"]], [D], tile, nct)
    (gt,) = mm_nn_cols(h, lp["w_in"], C_GT, [1024], nm("in_proj_a"), tm=tile)
    pv, nk, nv, su = mm_nn_cols(h, lp["w_in"], C_GV, [256] * 4, nm("in_proj_b"), tm=tile)
    nq, pu, pk, pg, pq = mm_nn_cols(h, lp["w_in"], C_NQ, [256, 256, 128, 128, 128], nm("in_proj_c"), tm=tile)
    q_r, k_r, lgf, lgb = rowwise_fwd(f_gla_prep, nm("gla_prep"), [pk, pg, pq, cos, sin], [], [lp["wg"], lp["bg"]], [128] * 4, tile, nct)
    part = (lambda idx: None) if comm is None else (lambda idx: ([comm[i] for i in idx], [False] * len(idx)))
    (o1, st_f), got_mod = gla_scan_fwd(q_r, k_r, pv, lgf, jnp.zeros((t, 256), F32), m_ctx, False, nm("gla_f"), part([0]))
    (o_gla, st_b), got_out = gla_scan_fwd(q_r, k_r, pv, lgb, o1, m_ctx, True, nm("gla_r"), part([2, 3]))
    received = None
    if comm is None:
        o_na = na_fwd(nq, nk, nv, lp["bias8"], m_ctx, nm("na"))
    else:
        o_na, got_in = na_fwd(nq, nk, nv, lp["bias8"], m_ctx, nm("na"), part([1]))
        received = [got_mod[0], got_in[0], got_out[0], got_out[1]]
    s5p = [whole_fwd(f_s5_params, nm(f"s5_par{d}"), lp["s5"][d], [(1, 1024)] * 2 + [(256, 1024)] * 4) for d in range(2)]
    y1, *states_f = s5_scan_fwd(su, jnp.zeros((t, 256), F32), *s5p[0], m_ctx, s5_chunk, False, nm("s5_f"))
    y5, *states_b = s5_scan_fwd(su, y1, *s5p[1], m_ctx, s5_chunk, True, nm("s5_r"))
    pm = jnp.concatenate([pool_apply(_pad_rows(pu[:m_ctx]), m_ctx, False, nm("pool_c")),
                          pool_apply(_pad_rows(pu[m_ctx:]), t - m_ctx, False, nm("pool_x"))], axis=0)
    mix_rows = [o_gla, o_na, y5, su, pm, gt]
    mix_globs = [lp["g_norm"], lp["s5_d"], lp["w_glu"], lp["b_glu"], lp["wpool"], lp["pool_scale"], lp["havg"], lp["e4"]]
    (yg,) = rowwise_fwd(f_mix, nm("mix"), mix_rows, [], mix_globs, [D], tile, nct)
    out = mm_nn([yg], lp["w_out"], nm("out_proj"), tm=tile)
    (z_new,) = rowwise_fwd(f_post, nm("post"), [z, out], [modseg], [lp["g_post"]], [D], tile, nct)
    saved = dict(z=z, h=h, pv=pv, nk=nk, nv=nv, su=su, nq=nq, pk=pk, pg=pg, pq=pq, q_r=q_r, k_r=k_r, lgf=lgf, lgb=lgb,
                 st_f=st_f, st_b=st_b, s5p=s5p, x0f=tuple(states_f), x0b=tuple(states_b), mix_rows=mix_rows, mix_globs=mix_globs,
                 yg=yg, out=out)
    return z_new, saved, received


def _f_pre_res(x, mod, g_pre):
    return f_pre(x, mod, g_pre)[0], x


def _layer_bwd(dz_new, sv, modseg, lp, cos, sin, m_ctx, tile, s5_chunk, l, comm=None, gdt=F32):
    t = dz_new.shape[0]
    nct = m_ctx // tile
    nm = lambda s: f"{s}_l{l}"
    g = {}
    dz_res, dout, dmod_post, g["g_post"] = rowwise_bwd(f_post, nm("post_b"), [sv["z"], sv["out"]], [modseg], [lp["g_post"]],
                                                       [dz_new], tile, nct, [True, True], [True])
    dyg = mm_nt([dout], lp["w_out"], nm("out_proj_dx"), tm=tile)
    dw_tile = t // 4 if t % 32 == 0 else tile
    (g["w_out"],) = mm_tn(sv["yg"], [dout], nm("out_proj_dw"), tm=dw_tile, out_dtype=gdt)
    res = rowwise_bwd(f_mix, nm("mix_b"), sv["mix_rows"], [], sv["mix_globs"], [dyg], tile, nct, [True] * 6, [True] * 6 + [False] * 2)
    do_gla, do_na, dy5, dsu_a, dpm, dgt = res[:6]
    g["g_norm"], g["s5_d"], g["w_glu"], g["b_glu"], g["wpool"], g["pool_scale"] = res[6:]
    dpu = jnp.concatenate([pool_apply(_pad_rows(dpm[:m_ctx]), m_ctx, True, nm("pool_c_b")),
                           pool_apply(_pad_rows(dpm[m_ctx:]), t - m_ctx, True, nm("pool_x_b"))], axis=0)
    r_b = s5_scan_bwd(sv["su"], dy5, dsu_a, *sv["x0b"], *sv["s5p"][1], m_ctx, s5_chunk, True, nm("s5_r_b"))
    r_f = s5_scan_bwd(sv["su"], dy5, r_b[0], *sv["x0f"], *sv["s5p"][0], m_ctx, s5_chunk, False, nm("s5_f_b"))
    dsu = r_f[0]
    g["s5"] = [whole_bwd(f_s5_params, nm(f"s5_par{d}_b"), lp["s5"][d], list(r[1:]), [True] * 7 + [False] * 4)
               for d, r in ((0, r_f), (1, r_b))]
    part = (lambda idx: None) if comm is None else (lambda idx: ([comm[i] for i in idx], [True] * len(idx)))
    (dnq, dnk, dnv, dbias8), got_in = na_bwd(sv["nq"], sv["nk"], sv["nv"], do_na, lp["bias8"], m_ctx, nm("na_b"), part([1]))
    g["rpb"] = _na_rpb_grad(dbias8, nm("na_rpb_b"))
    zq, zv = jnp.zeros((t, 128), F32), jnp.zeros((t, 256), F32)
    (dq1, dk1, dv1, dlgb), got_mod = gla_scan_bwd(sv["q_r"], sv["k_r"], sv["pv"], sv["lgb"], sv["st_b"], do_gla, (zq, zq, zv), m_ctx, True,
                                                  nm("gla_r_b"), part([0]))
    (dq_r, dk_r, dpv, dlgf), got_out = gla_scan_bwd(sv["q_r"], sv["k_r"], sv["pv"], sv["lgf"], sv["st_f"], do_gla, (dq1, dk1, dv1), m_ctx, False,
                                                    nm("gla_f_b"), part([2, 3]))
    received = None if comm is None else [got_mod[0], got_in[0], got_out[0], got_out[1]]
    dpk, dpg, dpq, g["wg"], g["bg"] = rowwise_bwd(f_gla_prep, nm("gla_prep_b"), [sv["pk"], sv["pg"], sv["pq"], cos, sin], [],
                                                  [lp["wg"], lp["bg"]], [dq_r, dk_r, dlgf, dlgb], tile, nct,
                                                  [True, True, True, False, False], [True, True])
    parts = [dgt, dpv, dnk, dnv, dsu, dnq, dpu, dpk, dpg, dpq, jnp.zeros((t, 128), F32)]
    dh = mm_nt(parts, lp["w_in"], nm("in_proj_dx"), tm=tile)
    g["w_in"] = _unpad_w_in(jnp.concatenate(mm_tn(sv["h"], parts, nm("in_proj_dw"), tm=dw_tile, out_dtype=gdt), axis=1))
    dz, dmod_pre, g["g_pre"] = rowwise_bwd(_f_pre_res, nm("pre_b"), [sv["z"]], [modseg], [lp["g_pre"]], [dh, dz_res], tile, nct, [True], [True])
    return dz, dmod_pre, dmod_post, g, received


def _f_mod_sum(cs, b_mod, w_mod):
    mod, _ = f_mod(cs, b_mod, w_mod)
    return mod, cs


def local_step(x, c, ctx, tgt, p, shards=None, tile=ROW_TILE, s5_chunk=S5_CHUNK):
    n_lat, m_ctx = x.shape[0], ctx.shape[0]
    n_layers = p["g_pre"].shape[0]
    z = jnp.concatenate([ctx, x], axis=0)
    cos, sin = _rope_tables(n_lat, m_ctx)
    cs = jnp.concatenate([c.reshape(1, D), p["c_ctx"].reshape(1, D), jnp.zeros((6, D), F32)], axis=0)
    gather = [False] * len(_SHARDED)
    lps, mods, silus, saves = [], [], [], []
    got = exchange(shards[0], gather, "gather_weights_l0") if shards is not None else None
    for l in range(n_layers):
        if shards is None:
            big = {n: p[n][l] for n in _SHARDED}
        else:
            big = {n: _gathered(g, _BY_COLS[n]) for n, g in zip(_SHARDED, got)}
        lp = _layer_params(p, big, l)
        mod8, s8 = whole_fwd(f_mod, f"mod_l{l}", [cs, lp["b_mod"], lp["w_mod"]], [(8, 3 * D), (8, D)])
        modseg = mod8[:2].reshape(2, 1, 3 * D)
        comm = shards[l + 1] if shards is not None and l + 1 < n_layers else None
        z, sv, got = _layer_fwd(z, modseg, lp, cos, sin, m_ctx, tile, s5_chunk, l, comm)
        lps.append(lp); mods.append(modseg); silus.append(s8); saves.append(sv)
    loss, dz = loss_and_grad(z, tgt, m_ctx, "loss", tile)
    grads, received = [None] * n_layers, [None] * n_layers
    gdt = F32 if shards is None else BF16
    dcs = jnp.zeros((8, D), F32)
    pending = None
    for l in reversed(range(n_layers)):
        lp = lps[l]
        dz, dmod_pre, dmod_post, g, got = _layer_bwd(dz, saves[l], mods[l], lp, cos, sin, m_ctx, tile, s5_chunk, l, pending, gdt)
        if pending is not None:
            received[l + 1] = got
        dmod = jnp.concatenate([dmod_pre.reshape(2, 3 * D)[:, :2 * D], dmod_post.reshape(2, 3 * D)[:, 2 * D:]], axis=1)
        dmod8 = jnp.pad(dmod, ((0, 6), (0, 0)))
        dcs, g["b_mod"] = whole_bwd(_f_mod_sum, f"mod_b_l{l}", [cs, lp["b_mod"], lp["w_mod"]], [dmod8, dcs], [True, True, False])
        g["w_mod"] = jnp.concatenate(mm_tn(silus[l], [dmod8[:, :D], dmod8[:, D:2 * D], dmod8[:, 2 * D:]], f"mod_dw_l{l}", tm=8, out_dtype=gdt), axis=1)
        grads[l] = g
        if shards is not None:
            pending = _layer_sends(g)
    if shards is not None:
        received[0] = exchange(pending + [_small_sends(dcs[1], grads)], [True] * len(_SHARDED) + [False], "exchange_grads_l0")
    return loss, dz[m_ctx:], dcs[1], grads, received


_WEIGHTS = ["c_ctx", "w_mod", "b_mod", "g_pre", "g_post", "w_in", "w_out", "gla_w_gate", "gla_b_gate", "gla_g_norm", "na_rpb",
            "s5_lam_re", "s5_lam_im", "s5_log_dt", "s5_b_re", "s5_b_im", "s5_c_re", "s5_c_im", "s5_d", "s5_w_glu", "s5_b_glu",
            "pool_w", "pool_scale"]
_INPUTS = ["x", "c", "ctx"] + _WEIGHTS + ["loss_target"] + ["m_" + n for n in _WEIGHTS] + ["v_" + n for n in _WEIGHTS]
_SHARDED = ["w_mod", "w_in", "w_out", "s5_w_glu"]
_BY_COLS = {"w_mod": True, "w_in": True, "w_out": False, "s5_w_glu": False}
_SMALL = [n for n in _WEIGHTS if n not in _SHARDED]
_SMALL_PER_LAYER = [n for n in _SMALL if n != "c_ctx"]
_PACK_ROWS = 256


def _pack_plan(like):
    tiled = [i for i, a in enumerate(like) if a.size % 1024 == 0]
    loose = [i for i, a in enumerate(like) if a.size % 1024 != 0]
    tail = -(-sum(like[i].size for i in loose) // 1024) * 8
    rows = sum(like[i].size // 128 for i in tiled) + tail
    return tiled, loose, tail, -(-rows // _PACK_ROWS) * _PACK_ROWS - rows


def _pack(arrs):
    tiled, loose, tail, fill = _pack_plan(arrs)
    dt = arrs[0].dtype
    flat = jnp.concatenate([arrs[i].reshape(-1) for i in loose])
    flat = jnp.pad(flat, (0, tail * 128 - flat.shape[0])).reshape(tail, 128)
    return jnp.concatenate([arrs[i].reshape(-1, 128) for i in tiled] + [flat, jnp.zeros((fill, 128), dt)], axis=0)


def _unpack(packed, like):
    tiled, loose, tail, _ = _pack_plan(like)
    out, row = [None] * len(like), 0
    for i in tiled:
        n = like[i].size // 128
        out[i] = packed[row:row + n].reshape(like[i].shape)
        row += n
    flat, pos = packed[row:row + tail].reshape(-1), 0
    for i in loose:
        out[i] = flat[pos:pos + like[i].size].reshape(like[i].shape)
        pos += like[i].size
    return out


def _gathered(g, cols):
    if cols:
        return g.transpose(1, 0, 2).reshape(g.shape[1], N_DEV * g.shape[2])
    return g.reshape(N_DEV * g.shape[1], g.shape[2])


def _slabs(w, cols):
    r, c = w.shape
    if cols:
        return w.reshape(r, N_DEV, c // N_DEV).transpose(1, 0, 2)
    return w.reshape(N_DEV, r // N_DEV, c)


def _layer_small(g):
    s5 = lambda i, f: jnp.stack([f(g["s5"][d][i]) for d in range(2)])
    return {
        "b_mod": g["b_mod"].reshape(3 * D), "g_pre": g["g_pre"].reshape(D), "g_post": g["g_post"].reshape(D),
        "gla_w_gate": jnp.stack([g["wg"][0:16, 0:128], g["wg"][16:32, 128:256]]),
        "gla_b_gate": g["bg"].reshape(2, 128), "gla_g_norm": g["g_norm"][0], "na_rpb": g["rpb"],
        "s5_lam_re": s5(0, lambda a: a), "s5_lam_im": s5(1, lambda a: a), "s5_log_dt": s5(2, lambda a: a.reshape(16)),
        "s5_b_re": s5(3, lambda a: a.reshape(16, 16, 64).transpose(0, 2, 1)),
        "s5_b_im": s5(4, lambda a: a.reshape(16, 16, 64).transpose(0, 2, 1)),
        "s5_c_re": s5(5, lambda a: a.reshape(16, 16, 64)), "s5_c_im": s5(6, lambda a: a.reshape(16, 16, 64)),
        "s5_d": g["s5_d"].reshape(256), "s5_b_glu": g["b_glu"].reshape(256),
        "pool_w": jnp.stack([g["wpool"][64 * i:64 * i + 64, 64 * i:64 * i + 64] for i in range(4)]),
        "pool_scale": g["pool_scale"].reshape(256),
    }


def _layer_sends(g):
    big = {"w_mod": g["w_mod"], "w_in": g["w_in"], "w_out": g["w_out"], "s5_w_glu": g["w_glu"]}
    return [_slabs(big[n], _BY_COLS[n]).astype(BF16) for n in _SHARDED]


def _small_sends(d_c_ctx, grads):
    per_layer = [_layer_small(g) for g in grads]
    full = {n: jnp.stack([s[n] for s in per_layer]) for n in _SMALL_PER_LAYER}
    full["c_ctx"] = d_c_ctx
    return _pack([full[n] for n in _SMALL]).astype(BF16)


def kernel(x, c, ctx, c_ctx, w_mod, b_mod, g_pre, g_post, w_in, w_out, gla_w_gate, gla_b_gate, gla_g_norm, na_rpb, s5_lam_re, s5_lam_im, s5_log_dt, s5_b_re, s5_b_im, s5_c_re, s5_c_im, s5_d, s5_w_glu, s5_b_glu, pool_w, pool_scale, loss_target, m_c_ctx, m_w_mod, m_b_mod, m_g_pre, m_g_post, m_w_in, m_w_out, m_gla_w_gate, m_gla_b_gate, m_gla_g_norm, m_na_rpb, m_s5_lam_re, m_s5_lam_im, m_s5_log_dt, m_s5_b_re, m_s5_b_im, m_s5_c_re, m_s5_c_im, m_s5_d, m_s5_w_glu, m_s5_b_glu, m_pool_w, m_pool_scale, v_c_ctx, v_w_mod, v_b_mod, v_g_pre, v_g_post, v_w_in, v_w_out, v_gla_w_gate, v_gla_b_gate, v_gla_g_norm, v_na_rpb, v_s5_lam_re, v_s5_lam_im, v_s5_log_dt, v_s5_b_re, v_s5_b_im, v_s5_c_re, v_s5_c_im, v_s5_d, v_s5_w_glu, v_s5_b_glu, v_pool_w, v_pool_scale):
    given = dict(zip(_INPUTS, (x, c, ctx, c_ctx, w_mod, b_mod, g_pre, g_post, w_in, w_out, gla_w_gate, gla_b_gate, gla_g_norm, na_rpb, s5_lam_re, s5_lam_im, s5_log_dt, s5_b_re, s5_b_im, s5_c_re, s5_c_im, s5_d, s5_w_glu, s5_b_glu, pool_w, pool_scale, loss_target, m_c_ctx, m_w_mod, m_b_mod, m_g_pre, m_g_post, m_w_in, m_w_out, m_gla_w_gate, m_gla_b_gate, m_gla_g_norm, m_na_rpb, m_s5_lam_re, m_s5_lam_im, m_s5_log_dt, m_s5_b_re, m_s5_b_im, m_s5_c_re, m_s5_c_im, m_s5_d, m_s5_w_glu, m_s5_b_glu, m_pool_w, m_pool_scale, v_c_ctx, v_w_mod, v_b_mod, v_g_pre, v_g_post, v_w_in, v_w_out, v_gla_w_gate, v_gla_b_gate, v_gla_g_norm, v_na_rpb, v_s5_lam_re, v_s5_lam_im, v_s5_log_dt, v_s5_b_re, v_s5_b_im, v_s5_c_re, v_s5_c_im, v_s5_d, v_s5_w_glu, v_s5_b_glu, v_pool_w, v_pool_scale)))
    n_layers = w_in.shape[0]
    shards = [[given[n][l].astype(BF16) for n in _SHARDED] for l in range(n_layers)]
    p = {n: given[n] for n in _SMALL}
    loss, grad_x, _, _, received = local_step(x[0], c, ctx[0], loss_target[0], p, shards)
    final = {}
    for n in _SHARDED:
        per_layer = [adamw(received[l][_SHARDED.index(n)], given[n][l], given["m_" + n][l], given["v_" + n][l], f"adamw_{n}_l{l}")
                     for l in range(n_layers)]
        final[n] = [jnp.stack([res[kind] for res in per_layer]) for kind in range(4)]
    like = [given[n] for n in _SMALL]
    res = adamw(received[0][-1], _pack(like), _pack([given["m_" + n] for n in _SMALL]), _pack([given["v_" + n] for n in _SMALL]),
                "adamw_small")
    unpacked = [_unpack(packed, like) for packed in res]
    for i, n in enumerate(_SMALL):
        final[n] = [unpacked[kind][i] for kind in range(4)]
    loss = lax.psum(loss, ("x", "y", "c"))
    return (loss, grad_x[None], *[final[n][0] for n in _WEIGHTS], *[final[n][1] for n in _WEIGHTS],
            *[final[n][2] for n in _WEIGHTS], *[final[n][3] for n in _WEIGHTS])
```

```python
import functools
import math

import numpy as np
import jax
import jax.numpy as jnp
from jax import lax
from jax.experimental import pallas as pl
from jax.experimental.pallas import tpu as pltpu

F32 = jnp.float32
BF16 = jnp.bfloat16
HIGHEST = lax.Precision.HIGHEST
HIGH = lax.Precision.HIGH

D = 1024
GRID_W = 64
EPS = 1e-6
N_DEV = 8
C_GT, C_GV, C_NK, C_NV, C_SU, C_NQ, C_PU, C_GK, C_GG, C_GQ, C_END = 0, 1024, 1280, 1536, 1792, 2048, 2304, 2560, 2688, 2816, 2944
PW = 3072
N_CTX_ORIG = 416
N_IN = 2848
GLA_CHUNK = 128
S5_CHUNK = 256
ROW_TILE = 256
VMEM_LIMIT = 56 * 1024 * 1024

ADAM_LR, ADAM_B1, ADAM_B2, ADAM_EPS, ADAM_WD, ADAM_STEP = 0.001, 0.9, 0.999, 1e-08, 0.01, 10


def _cparams(**kw):
    return pltpu.CompilerParams(vmem_limit_bytes=VMEM_LIMIT, **kw)


def _dg(a, b, ca, cb, precision=None):
    return lax.dot_general(a, b, (((ca,), (cb,)), ((), ())), precision=precision, preferred_element_type=F32)


def hdot(a, b):
    return _dg(a, b, 1, 0, HIGHEST)


def hdot_nt(a, b):
    return _dg(a, b, 1, 1, HIGHEST)


def hdot_tn(a, b):
    return _dg(a, b, 0, 0, HIGHEST)


def mdot(a, b):
    return _dg(a, b, 1, 0, HIGH)


def mdot_nt(a, b):
    return _dg(a, b, 1, 1, HIGH)


def mdot_tn(a, b):
    return _dg(a, b, 0, 0, HIGH)


def b_nn(a, b):
    return _dg(a.astype(BF16), b.astype(BF16), 1, 0)


def b_nt(a, b):
    return _dg(a.astype(BF16), b.astype(BF16), 1, 1)


def b_tn(a, b):
    return _dg(a.astype(BF16), b.astype(BF16), 0, 0)


@jax.custom_vjp
def bdot(a, b):
    return b_nn(a, b)


def _bdot_fwd(a, b):
    return b_nn(a, b), (a, b)


def _bdot_bwd(res, ct):
    a, b = res
    return b_nt(ct, b).astype(a.dtype), b_tn(a, ct).astype(b.dtype)


bdot.defvjp(_bdot_fwd, _bdot_bwd)


def _log_sigmoid(z):
    return jnp.minimum(z, 0.0) - jnp.log(1.0 + jnp.exp(-jnp.abs(z)))


def _silu(z):
    return z * jax.nn.sigmoid(z)


def _gelu(z):
    return 0.5 * z * (1.0 + jnp.tanh(math.sqrt(2.0 / math.pi) * (z + 0.044715 * (z * z * z))))


def _cat(vals):
    return vals[0] if len(vals) == 1 else jnp.concatenate(vals, axis=-1)


def mm_nn(a_parts, b, name, tm=ROW_TILE, tn=1024):
    t = a_parts[0].shape[0]
    k, n = b.shape
    na = len(a_parts)
    tn = min(tn, n)

    def body(*refs):
        a = _cat([r[...].astype(BF16) for r in refs[:na]])
        refs[na + 1][...] = _dg(a, refs[na][...].astype(BF16), 1, 0)

    return pl.pallas_call(
        body, name=name, grid=(n // tn, t // tm),
        in_specs=[pl.BlockSpec((tm, p.shape[1]), lambda j, i: (i, 0)) for p in a_parts]
        + [pl.BlockSpec((k, tn), lambda j, i: (0, j))],
        out_specs=pl.BlockSpec((tm, tn), lambda j, i: (i, j)),
        out_shape=jax.ShapeDtypeStruct((t, n), F32),
        compiler_params=_cparams(dimension_semantics=("arbitrary", "arbitrary")),
    )(*a_parts, b)


def mm_nn_cols(a, b, start, widths, name, tm=ROW_TILE):
    t, k = a.shape
    tn = 1024
    assert start % tn == 0 and sum(widths) <= tn

    def body(a_ref, b_ref, *o_refs):
        r = _dg(a_ref[...].astype(BF16), b_ref[...].astype(BF16), 1, 0)
        off = 0
        for o_ref, w in zip(o_refs, widths):
            o_ref[...] = r[:, off:off + w]
            off += w

    return pl.pallas_call(
        body, name=name, grid=(t // tm,),
        in_specs=[pl.BlockSpec((tm, k), lambda i: (i, 0)), pl.BlockSpec((k, tn), lambda i: (0, start // tn))],
        out_specs=[pl.BlockSpec((tm, w), lambda i: (i, 0)) for w in widths],
        out_shape=[jax.ShapeDtypeStruct((t, w), F32) for w in widths],
        compiler_params=_cparams(dimension_semantics=("arbitrary",)),
    )(a, b)


def mm_nt(a_parts, b, name, tm=ROW_TILE):
    t = a_parts[0].shape[0]
    n, k = b.shape
    na = len(a_parts)

    def body(*refs):
        a = _cat([r[...].astype(BF16) for r in refs[:na]])
        refs[na + 1][...] = _dg(a, refs[na][...].astype(BF16), 1, 1)

    return pl.pallas_call(
        body, name=name, grid=(t // tm,),
        in_specs=[pl.BlockSpec((tm, p.shape[1]), lambda i: (i, 0)) for p in a_parts]
        + [pl.BlockSpec((n, k), lambda i: (0, 0))],
        out_specs=pl.BlockSpec((tm, n), lambda i: (i, 0)),
        out_shape=jax.ShapeDtypeStruct((t, n), F32),
        compiler_params=_cparams(dimension_semantics=("arbitrary",)),
    )(*a_parts, b)


def mm_tn(a, b_parts, name, tm=ROW_TILE, tn=1024, out_dtype=F32):
    t, k = a.shape
    widths = [p.shape[1] for p in b_parts]
    n = sum(widths)
    assert n % tn == 0
    groups, cur, acc = [], [], 0
    for idx, w in enumerate(widths):
        cur.append(idx)
        acc += w
        if acc == tn:
            groups.append(cur)
            cur, acc = [], 0
        assert acc < tn
    assert not cur
    outs = []
    for gi, grp in enumerate(groups):
        parts = [b_parts[i] for i in grp]
        npart = len(parts)
        nsteps = t // tm

        def body(*refs, npart=npart, nsteps=nsteps):
            a_v = refs[0][...].astype(BF16)
            b_v = _cat([r[...].astype(BF16) for r in refs[1:1 + npart]])
            o_ref, acc_ref = refs[1 + npart], refs[2 + npart]
            r = _dg(a_v, b_v, 0, 0)

            @pl.when(pl.program_id(0) == 0)
            def _():
                acc_ref[...] = r

            @pl.when(pl.program_id(0) != 0)
            def _():
                acc_ref[...] += r

            @pl.when(pl.program_id(0) == nsteps - 1)
            def _():
                o_ref[...] = acc_ref[...].astype(o_ref.dtype)

        outs.append(pl.pallas_call(
            body, name=f"{name}_{gi}", grid=(nsteps,),
            in_specs=[pl.BlockSpec((tm, k), lambda i: (i, 0))]
            + [pl.BlockSpec((tm, p.shape[1]), lambda i: (i, 0)) for p in parts],
            out_specs=pl.BlockSpec((k, tn), lambda i: (0, 0)),
            out_shape=jax.ShapeDtypeStruct((k, tn), out_dtype),
            scratch_shapes=[pltpu.VMEM((k, tn), F32)],
            compiler_params=_cparams(dimension_semantics=("arbitrary",)),
        )(a, *parts))
    return outs


def _seg_of(i, nct):
    return jnp.where(i < nct, 1, 0)


def rowwise_fwd(fn, name, rows, segs, globs, out_widths, tile, nct):
    t = rows[0].shape[0]
    nr, ns, ng = len(rows), len(segs), len(globs)

    def body(*refs):
        vals = [r[...] for r in refs[:nr]] + [r[0] for r in refs[nr:nr + ns]] + [r[...] for r in refs[nr + ns:nr + ns + ng]]
        outs = fn(*vals)
        for o_ref, o in zip(refs[nr + ns + ng:], outs):
            o_ref[...] = o

    return pl.pallas_call(
        body, name=name, grid=(t // tile,),
        in_specs=[pl.BlockSpec((tile, r.shape[1]), lambda i: (i, 0)) for r in rows]
        + [pl.BlockSpec((1, 1, s.shape[2]), lambda i: (_seg_of(i, nct), 0, 0)) for s in segs]
        + [pl.BlockSpec(g.shape, lambda i: (0, 0)) for g in globs],
        out_specs=[pl.BlockSpec((tile, w), lambda i: (i, 0)) for w in out_widths],
        out_shape=[jax.ShapeDtypeStruct((t, w), F32) for w in out_widths],
        compiler_params=_cparams(dimension_semantics=("arbitrary",)),
    )(*rows, *segs, *globs)


def rowwise_bwd(fn, name, rows, segs, globs, cts, tile, nct, row_diff, glob_diff):
    t = rows[0].shape[0]
    nr, ns, ng, nc = len(rows), len(segs), len(globs), len(cts)
    d_rows = [i for i in range(nr) if row_diff[i]]
    d_globs = [i for i in range(ng) if glob_diff[i]]

    def body(*refs):
        in_refs, out_refs = refs[:nr + ns + ng + nc], refs[nr + ns + ng + nc:]
        row_v = [r[...] for r in in_refs[:nr]]
        seg_v = [r[0] for r in in_refs[nr:nr + ns]]
        glob_v = [r[...] for r in in_refs[nr + ns:nr + ns + ng]]
        ct_v = tuple(r[...] for r in in_refs[nr + ns + ng:])

        def wrapped(dr, sv, dg):
            rv = list(row_v)
            for j, i in enumerate(d_rows):
                rv[i] = dr[j]
            gv = list(glob_v)
            for j, i in enumerate(d_globs):
                gv[i] = dg[j]
            return tuple(fn(*rv, *sv, *gv))

        _, vjp = jax.vjp(wrapped, [row_v[i] for i in d_rows], seg_v, [glob_v[i] for i in d_globs])
        c_rows, c_segs, c_globs = vjp(ct_v)
        i = pl.program_id(0)
        k = 0
        for c in c_rows:
            out_refs[k][...] = c
            k += 1
        seg_first = jnp.logical_or(i == 0, i == nct)
        for c in c_segs:
            ref = out_refs[k]
            k += 1

            @pl.when(seg_first)
            def _(ref=ref, c=c):
                ref[0] = c

            @pl.when(jnp.logical_not(seg_first))
            def _(ref=ref, c=c):
                ref[0] += c
        for c in c_globs:
            ref = out_refs[k]
            k += 1

            @pl.when(i == 0)
            def _(ref=ref, c=c):
                ref[...] = c

            @pl.when(i != 0)
            def _(ref=ref, c=c):
                ref[...] += c

    return pl.pallas_call(
        body, name=name, grid=(t // tile,),
        in_specs=[pl.BlockSpec((tile, r.shape[1]), lambda i: (i, 0)) for r in rows]
        + [pl.BlockSpec((1, 1, s.shape[2]), lambda i: (_seg_of(i, nct), 0, 0)) for s in segs]
        + [pl.BlockSpec(g.shape, lambda i: (0, 0)) for g in globs]
        + [pl.BlockSpec((tile, c.shape[1]), lambda i: (i, 0)) for c in cts],
        out_specs=[pl.BlockSpec((tile, rows[i].shape[1]), lambda i: (i, 0)) for i in d_rows]
        + [pl.BlockSpec((1, 1, s.shape[2]), lambda i: (_seg_of(i, nct), 0, 0)) for s in segs]
        + [pl.BlockSpec(globs[i].shape, lambda i: (0, 0)) for i in d_globs],
        out_shape=[jax.ShapeDtypeStruct(rows[i].shape, F32) for i in d_rows]
        + [jax.ShapeDtypeStruct(s.shape, F32) for s in segs]
        + [jax.ShapeDtypeStruct(globs[i].shape, F32) for i in d_globs],
        compiler_params=_cparams(dimension_semantics=("arbitrary",)),
    )(*rows, *segs, *globs, *cts)


def f_pre(x, mod, g_pre):
    shift, scale = mod[:, :D], mod[:, D:2 * D]
    rs = lax.rsqrt(jnp.mean(x * x, axis=-1, keepdims=True) + EPS)
    return ((x * rs) * g_pre * (1.0 + scale) + shift,)


def f_post(x, out, mod, g_post):
    gate = mod[:, 2 * D:]
    rs = lax.rsqrt(jnp.mean(out * out, axis=-1, keepdims=True) + EPS)
    return (x + gate * ((out * rs) * g_post),)


def f_mix(o_gla, o_na, y5, u5, pm, gcols, g_norm, s5_d, w_glu, b_glu, wpool, pool_scale, havg, e4):
    ms = mdot(o_gla * o_gla, havg)
    y_gla = o_gla * lax.rsqrt(ms + EPS) * jnp.sum(hdot(g_norm, e4), axis=0, keepdims=True)
    g = _gelu(u5 * s5_d + y5)
    y_s5 = g * jax.nn.sigmoid(bdot(g, w_glu) + b_glu)
    y_pool = bdot(pm, wpool) * pool_scale
    ycat = jnp.concatenate([y_gla, o_na, y_s5, y_pool], axis=-1)
    return (ycat * _silu(gcols),)


@jax.custom_vjp
def _rot_half16(x):
    lane = lax.broadcasted_iota(jnp.int32, x.shape, 1)
    first = jnp.bitwise_and(lane, 15) < 8
    return jnp.where(first, -pltpu.roll(x, x.shape[1] - 8, 1), pltpu.roll(x, 8, 1))


def _rot_fwd(x):
    return _rot_half16(x), None


def _rot_bwd(_, ct):
    return (-_rot_half16(ct),)


_rot_half16.defvjp(_rot_fwd, _rot_bwd)


def f_gla_prep(pk, pg, pq, cos, sin, wg, bg):
    z = bdot(pg, wg) + bg
    lg = _log_sigmoid(z) * (1.0 / 16.0)
    k_r = pk * cos + _rot_half16(pk) * sin
    q_r = (pq * cos + _rot_half16(pq) * sin) * (32.0 ** -0.5)
    return q_r, k_r, lg[:, :128], lg[:, 128:]


def _gla_consts(rev):
    c = GLA_CHUNK
    i = np.arange(c)
    inc = (i[None, :] >= i[:, None]) if rev else (i[None, :] <= i[:, None])
    mq = np.stack([(np.arange(128) // 32 == h) for h in range(4)]).astype(np.float32).reshape(4, 1, 128)
    mv = np.stack([(np.arange(256) // 64 == h) for h in range(4)]).astype(np.float32).reshape(4, 1, 256)
    bdt = (np.arange(256)[:, None] // 64 == np.arange(128)[None, :] // 32).astype(np.float32)
    inc = inc.astype(np.float32)
    return jnp.asarray(inc), jnp.asarray(inc.T.copy()), jnp.asarray(mq), jnp.asarray(mv), jnp.asarray(bdt)


def _stack_heads(x, m_ref):
    return jnp.concatenate([x * m_ref[h] for h in range(4)], axis=0)


def _tile4(m):
    return jnp.concatenate([m, m, m, m], axis=0)


def _fold_heads(r4, m_ref):
    r = r4.shape[0] // 4
    out = m_ref[0] * r4[0:r]
    for h in range(1, 4):
        out = out + m_ref[h] * r4[h * r:(h + 1) * r]
    return out


def _gla_chunk_of(s, n_ctx_chunks, n_chunks, rev):
    if not rev:
        return s
    return jnp.where(s < n_ctx_chunks, n_ctx_chunks - 1 - s, n_ctx_chunks + n_chunks - 1 - s)


def gla_scan_fwd(q, k, v, lg, acc, n_ctx_rows, rev, name, comm=None):
    t = q.shape[0]
    nch, ncc = t // GLA_CHUNK, n_ctx_rows // GLA_CHUNK
    inc, inc_t, mq, mv, bdt = _gla_consts(rev)

    def body(q_ref, k_ref, v_ref, lg_ref, acc_ref, inc_ref, inct_ref, mq_ref, mv_ref, bdt_ref, o_ref, st_ref):
        lmask, lmask_t = inc_ref[...], inct_ref[...]
        bd = bdt_ref[...]

        def step(s, st):
            c = _gla_chunk_of(s, ncc, nch, rev)
            rows = pl.ds(pl.multiple_of(c * GLA_CHUNK, GLA_CHUNK), GLA_CHUNK)
            qc, kc, vc, lgc = q_ref[rows, :], k_ref[rows, :], v_ref[rows, :], lg_ref[rows, :]
            st_ref[c] = st
            b = hdot(lmask, lgc)
            blast = jnp.sum(lgc, axis=0, keepdims=True)
            qe, ke, kd = qc * jnp.exp(b), kc * jnp.exp(-b), kc * jnp.exp(blast - b)
            ke4, v4 = _stack_heads(ke, mq_ref), _stack_heads(vc, mv_ref)
            at = _tile4(lmask_t) * b_nt(ke4, qe)
            o_ref[rows, :] = acc_ref[rows, :] + b_nt(qe, st) + b_tn(at, v4)
            return st * jnp.exp(blast) + bd * mdot_tn(vc, kd)

        lax.fori_loop(0, nch, step, jnp.zeros((256, 128), F32))

    return _call_with_exchange(body, name, [q, k, v, lg, acc, inc, inc_t, mq, mv, bdt],
                               [jax.ShapeDtypeStruct((t, 256), F32), jax.ShapeDtypeStruct((nch, 256, 128), F32)], comm)


def gla_scan_bwd(q, k, v, lg, st, do, acc, n_ctx_rows, rev, name, comm=None):
    t = q.shape[0]
    nch, ncc = t // GLA_CHUNK, n_ctx_rows // GLA_CHUNK
    inc, inc_t, mq, mv, bdt = _gla_consts(rev)

    def body(q_ref, k_ref, v_ref, lg_ref, st_ref, do_ref, aq_ref, ak_ref, av_ref, inc_ref, inct_ref, mq_ref, mv_ref, bdt_ref,
             dq_ref, dk_ref, dv_ref, dlg_ref):
        lmask, lmask_t = inc_ref[...], inct_ref[...]
        bd = bdt_ref[...]

        def step(j, carry):
            dst, gsum = carry
            s = nch - 1 - j
            c = _gla_chunk_of(s, ncc, nch, rev)
            rows = pl.ds(pl.multiple_of(c * GLA_CHUNK, GLA_CHUNK), GLA_CHUNK)
            qc, kc, vc, lgc, doc = q_ref[rows, :], k_ref[rows, :], v_ref[rows, :], lg_ref[rows, :], do_ref[rows, :]
            stc = st_ref[c]
            b = hdot(lmask, lgc)
            blast = jnp.sum(lgc, axis=0, keepdims=True)
            eb, enb, edb = jnp.exp(b), jnp.exp(-b), jnp.exp(blast - b)
            qe, ke, kd = qc * eb, kc * enb, kc * edb
            ke4, v4 = _stack_heads(ke, mq_ref), _stack_heads(vc, mv_ref)
            lm4 = _tile4(lmask_t)
            at = lm4 * b_nt(ke4, qe)
            dat = lm4 * mdot_nt(v4, doc)
            dqe = mdot(doc, stc) + mdot_tn(dat, ke4)
            dke = _fold_heads(mdot(dat, qe), mq_ref)
            dv = b_nt(kd, dst) + _fold_heads(b_nn(at, doc), mv_ref)
            dkd = mdot(vc, dst)
            dq = dqe * eb
            dk = dke * enb + dkd * edb
            g = qc * dq - kc * dk
            dlg_ref[rows, :] = hdot_tn(lmask, g) + gsum
            dq_ref[rows, :] = aq_ref[rows, :] + dq
            dk_ref[rows, :] = ak_ref[rows, :] + dk
            dv_ref[rows, :] = av_ref[rows, :] + dv
            dst_new = dst * jnp.exp(blast) + bd * mdot_tn(doc, qe)
            return dst_new, gsum + jnp.sum(g, axis=0, keepdims=True)

        lax.fori_loop(0, nch, step, (jnp.zeros((256, 128), F32), jnp.zeros((1, 128), F32)))

    return _call_with_exchange(body, name, [q, k, v, lg, st, do, *acc, inc, inc_t, mq, mv, bdt],
                               [jax.ShapeDtypeStruct((t, 128), F32), jax.ShapeDtypeStruct((t, 128), F32),
                                jax.ShapeDtypeStruct((t, 256), F32), jax.ShapeDtypeStruct((t, 128), F32)], comm)


def whole_fwd(fn, name, args, out_shapes):
    def body(*refs):
        outs = fn(*[r[...] for r in refs[:len(args)]])
        for o_ref, o in zip(refs[len(args):], outs):
            o_ref[...] = o

    vm = pl.BlockSpec(memory_space=pltpu.VMEM)
    return pl.pallas_call(
        body, name=name, in_specs=[vm] * len(args), out_specs=[vm] * len(out_shapes),
        out_shape=[jax.ShapeDtypeStruct(s, F32) for s in out_shapes], compiler_params=_cparams(),
    )(*args)


def whole_bwd(fn, name, args, cts, diff):
    d_idx = [i for i in range(len(args)) if diff[i]]

    def body(*refs):
        vals = [r[...] for r in refs[:len(args)]]
        ct_v = tuple(r[...] for r in refs[len(args):len(args) + len(cts)])

        def wrapped(dv):
            av = list(vals)
            for j, i in enumerate(d_idx):
                av[i] = dv[j]
            return tuple(fn(*av))

        _, vjp = jax.vjp(wrapped, [vals[i] for i in d_idx])
        (c_args,) = vjp(ct_v)
        for o_ref, c in zip(refs[len(args) + len(cts):], c_args):
            o_ref[...] = c

    vm = pl.BlockSpec(memory_space=pltpu.VMEM)
    return pl.pallas_call(
        body, name=name, in_specs=[vm] * (len(args) + len(cts)), out_specs=[vm] * len(d_idx),
        out_shape=[jax.ShapeDtypeStruct(args[i].shape, F32) for i in d_idx], compiler_params=_cparams(),
    )(*args, *cts)


def _s5_consts():
    e_rep = (np.arange(256)[:, None] // 16 == np.arange(16)[None, :]).astype(np.float32)
    e_tile = (np.arange(64)[:, None] == np.arange(1024)[None, :] % 64).astype(np.float32)
    gmask = (np.arange(16)[:, None] == np.arange(1024)[None, :] // 64).astype(np.float32)
    bdm = (np.arange(256)[:, None] // 16 == np.arange(1024)[None, :] // 64).astype(np.float32)
    return jnp.asarray(e_rep), jnp.asarray(e_tile), jnp.asarray(gmask), jnp.asarray(bdm)


def f_s5_params(lam_re, lam_im, log_dt, bt_re, bt_im, ct_re, ct_im, e_rep, e_tile, gmask, bdm):
    dt = jnp.exp(log_dt)
    mag = jnp.exp(lam_re * dt)
    ang = lam_im * dt
    lb_re, lb_im = mag * jnp.cos(ang), mag * jnp.sin(ang)
    num_re, num_im = lb_re - 1.0, lb_im
    den = lam_re * lam_re + lam_im * lam_im
    coef_re = (num_re * lam_re + num_im * lam_im) / den
    coef_im = (num_im * lam_re - num_re * lam_im) / den
    cr, ci = hdot(e_rep, coef_re), hdot(e_rep, coef_im)
    bbt_re = cr * bt_re - ci * bt_im
    bbt_im = cr * bt_im + ci * bt_re
    a_re = jnp.sum(hdot(lb_re, e_tile) * gmask, axis=0, keepdims=True)
    a_im = jnp.sum(hdot(lb_im, e_tile) * gmask, axis=0, keepdims=True)
    return (a_re, a_im, hdot(bbt_re, e_tile) * bdm, hdot(bbt_im, e_tile) * bdm,
            hdot(ct_re, e_tile) * bdm, hdot(ct_im, e_tile) * bdm)


def _s5_doubling(xr, xi, pr, pi, pos, n, steps, rev):
    rows = xr.shape[0]
    for s in steps:
        if rev:
            keep = pos < (n - s)
            sr, si = pltpu.roll(xr, rows - s, 0), pltpu.roll(xi, rows - s, 0)
        else:
            keep = pos >= s
            sr, si = pltpu.roll(xr, s, 0), pltpu.roll(xi, s, 0)
        sr, si = jnp.where(keep, sr, 0.0), jnp.where(keep, si, 0.0)
        xr, xi = xr + pr * sr - pi * si, xi + pr * si + pi * sr
        pr, pi = pr * pr - pi * pi, 2.0 * pr * pi
    return xr, xi, pr, pi


SUBLANES = 8


def _s5_scan(xr, xi, a_re, a_im, rev, chunk, scr):
    xs_r, xs_i, yp_r, yp_i = scr
    ng = chunk // SUBLANES
    x3r, x3i = xr.reshape(ng, SUBLANES, 1024), xi.reshape(ng, SUBLANES, 1024)
    sub = lax.broadcasted_iota(jnp.int32, (SUBLANES, 1024), 0)
    a8r, a8i = a_re, a_im
    for s in (1, 2, 4):
        keep = sub < (SUBLANES - s) if rev else sub >= s
        mr, mi = jnp.where(keep, a8r, 0.0)[None], jnp.where(keep, a8i, 0.0)[None]
        shift = SUBLANES - s if rev else s
        sr, si = pltpu.roll(x3r, shift, 1), pltpu.roll(x3i, shift, 1)
        x3r, x3i = x3r + mr * sr - mi * si, x3i + mr * si + mi * sr
        a8r, a8i = a8r * a8r - a8i * a8i, 2.0 * a8r * a8i
    xr, xi = x3r.reshape(chunk, 1024), x3i.reshape(chunk, 1024)
    nblk = 1024 // 128
    for j in range(nblk):
        xs_r[j] = xr[:, 128 * j:128 * (j + 1)]
        xs_i[j] = xi[:, 128 * j:128 * (j + 1)]
    edge = pl.ds(0 if rev else SUBLANES - 1, ng, stride=SUBLANES)
    gr = jnp.concatenate([xs_r[j, edge, :] for j in range(nblk)], axis=-1)
    gi = jnp.concatenate([xs_i[j, edge, :] for j in range(nblk)], axis=-1)
    grow = lax.broadcasted_iota(jnp.int32, (ng, 1024), 0)
    steps = tuple(1 << k for k in range((ng - 1).bit_length()))
    gr, gi, _, _ = _s5_doubling(gr, gi, a8r, a8i, grow, ng, steps, rev)
    if rev:
        yp_r[...] = jnp.where(grow < ng - 1, pltpu.roll(gr, ng - 1, 0), 0.0)
        yp_i[...] = jnp.where(grow < ng - 1, pltpu.roll(gi, ng - 1, 0), 0.0)
    else:
        yp_r[...] = jnp.where(grow >= 1, pltpu.roll(gr, 1, 0), 0.0)
        yp_i[...] = jnp.where(grow >= 1, pltpu.roll(gi, 1, 0), 0.0)
    sub = lax.broadcasted_iota(jnp.int32, (SUBLANES, 1024), 0)
    tr, ti = jnp.zeros((SUBLANES, 1024), F32), jnp.zeros((SUBLANES, 1024), F32)
    cr, ci = a_re, a_im
    for n in range(1, SUBLANES + 1):
        r = SUBLANES - n if rev else n - 1
        tr, ti = jnp.where(sub == r, cr, tr), jnp.where(sub == r, ci, ti)
        cr, ci = cr * a_re - ci * a_im, cr * a_im + ci * a_re
    for j in range(nblk):
        lanes = slice(128 * j, 128 * (j + 1))
        tr_j, ti_j = tr[:, lanes], ti[:, lanes]
        for g in range(ng):
            rows = slice(g * SUBLANES, (g + 1) * SUBLANES)
            er, ei = yp_r[g:g + 1, lanes], yp_i[g:g + 1, lanes]
            xs_r[j, rows, :] = xs_r[j, rows, :] + tr_j * er - ti_j * ei
            xs_i[j, rows, :] = xs_i[j, rows, :] + tr_j * ei + ti_j * er
    return (jnp.concatenate([xs_r[j] for j in range(nblk)], axis=-1),
            jnp.concatenate([xs_i[j] for j in range(nblk)], axis=-1))


def _s5_scratch(chunk):
    return [pltpu.VMEM((8, chunk, 128), F32), pltpu.VMEM((8, chunk, 128), F32),
            pltpu.VMEM((chunk // SUBLANES, 1024), F32), pltpu.VMEM((chunk // SUBLANES, 1024), F32)]


def _s5_chunk_states(u_c, x0r, x0i, a_re, a_im, bb_re, bb_im, rev, chunk, scr):
    row = lax.broadcasted_iota(jnp.int32, (chunk, 1024), 0)
    first = row == (chunk - 1 if rev else 0)
    inj_r = a_re * x0r - a_im * x0i
    inj_i = a_re * x0i + a_im * x0r
    xr = b_nn(u_c, bb_re) + jnp.where(first, inj_r, 0.0)
    xi = b_nn(u_c, bb_im) + jnp.where(first, inj_i, 0.0)
    return _s5_scan(xr, xi, a_re, a_im, rev, chunk, scr)


def _row_pick(x, idx):
    row = lax.broadcasted_iota(jnp.int32, x.shape, 0)
    return jnp.sum(jnp.where(row == idx, x, 0.0), axis=0, keepdims=True)


def s5_scan_fwd(u, acc, a_re, a_im, bb_re, bb_im, cc_re, cc_im, n_ctx_rows, chunk, rev, name, comm=None):
    t = u.shape[0]
    nch, ncc = t // chunk, n_ctx_rows // chunk

    def body(u_ref, acc_ref, ar_ref, ai_ref, br_ref, bi_ref, cr_ref, ci_ref, y_ref, x0r_ref, x0i_ref, xsr_ref, xsi_ref, *scr):
        a_r, a_i = ar_ref[...], ai_ref[...]

        def step(s, carry):
            x0r, x0i = carry
            c = _gla_chunk_of(s, ncc, nch, rev)
            rows = pl.ds(pl.multiple_of(c * chunk, chunk), chunk)
            x0r_ref[c] = x0r
            x0i_ref[c] = x0i
            xr, xi = _s5_chunk_states(u_ref[rows, :], x0r, x0i, a_r, a_i, br_ref[...], bi_ref[...], rev, chunk, scr)
            y_ref[rows, :] = acc_ref[rows, :] + b_nt(xr, cr_ref[...]) - b_nt(xi, ci_ref[...])
            xsr_ref[rows, :] = xr.astype(BF16)
            xsi_ref[rows, :] = xi.astype(BF16)
            last = 0 if rev else chunk - 1
            return _row_pick(xr, last), _row_pick(xi, last)

        lax.fori_loop(0, nch, step, (jnp.zeros((1, 1024), F32), jnp.zeros((1, 1024), F32)))

    return _call_with_exchange(
        body, name, [u, acc, a_re, a_im, bb_re, bb_im, cc_re, cc_im],
        [jax.ShapeDtypeStruct((t, 256), F32), jax.ShapeDtypeStruct((nch, 1, 1024), F32),
         jax.ShapeDtypeStruct((nch, 1, 1024), F32), jax.ShapeDtypeStruct((t, 1024), BF16),
         jax.ShapeDtypeStruct((t, 1024), BF16)], comm, _s5_scratch(chunk))


def s5_scan_bwd(u, dy, du_acc, x0r, x0i, xsr, xsi, a_re, a_im, bb_re, bb_im, cc_re, cc_im, n_ctx_rows, chunk, rev, name):
    t = u.shape[0]
    nch, ncc = t // chunk, n_ctx_rows // chunk

    def body(u_ref, dy_ref, dua_ref, x0r_ref, x0i_ref, xsr_ref, xsi_ref, ar_ref, ai_ref, br_ref, bi_ref, cr_ref, ci_ref,
             du_ref, dar_ref, dai_ref, dbr_ref, dbi_ref, dcr_ref, dci_ref, *scr):
        a_r, a_i = ar_ref[...], ai_ref[...]
        for ref in (dbr_ref, dbi_ref, dcr_ref, dci_ref):
            ref[...] = jnp.zeros_like(ref)
        row = lax.broadcasted_iota(jnp.int32, (chunk, 1024), 0)
        first_idx, last_idx = (chunk - 1, 0) if rev else (0, chunk - 1)

        def step(j, carry):
            lcr, lci, dar, dai = carry
            s = nch - 1 - j
            c = _gla_chunk_of(s, ncc, nch, rev)
            rows = pl.ds(pl.multiple_of(c * chunk, chunk), chunk)
            u_c, dy_c = u_ref[rows, :], dy_ref[rows, :]
            x0r_c, x0i_c = x0r_ref[c], x0i_ref[c]
            xr, xi = xsr_ref[rows, :].astype(F32), xsi_ref[rows, :].astype(F32)
            dcr_ref[...] += b_tn(dy_c, xr)
            dci_ref[...] -= b_tn(dy_c, xi)
            inj_r = a_r * lcr + a_i * lci
            inj_i = a_r * lci - a_i * lcr
            is_last = row == last_idx
            lr = b_nn(dy_c, cr_ref[...]) + jnp.where(is_last, inj_r, 0.0)
            li = -b_nn(dy_c, ci_ref[...]) + jnp.where(is_last, inj_i, 0.0)
            lr, li = _s5_scan(lr, li, a_r, -a_i, not rev, chunk, scr)
            du_ref[rows, :] = dua_ref[rows, :] + b_nt(lr, br_ref[...]) + b_nt(li, bi_ref[...])
            dbr_ref[...] += b_tn(u_c, lr)
            dbi_ref[...] += b_tn(u_c, li)
            if rev:
                pr, pi = pltpu.roll(xr, chunk - 1, 0), pltpu.roll(xi, chunk - 1, 0)
            else:
                pr, pi = pltpu.roll(xr, 1, 0), pltpu.roll(xi, 1, 0)
            is_first = row == first_idx
            pr, pi = jnp.where(is_first, x0r_c, pr), jnp.where(is_first, x0i_c, pi)
            dar = dar + jnp.sum(lr * pr + li * pi, axis=0, keepdims=True)
            dai = dai + jnp.sum(li * pr - lr * pi, axis=0, keepdims=True)
            return _row_pick(lr, first_idx), _row_pick(li, first_idx), dar, dai

        z = jnp.zeros((1, 1024), F32)
        _, _, dar, dai = lax.fori_loop(0, nch, step, (z, z, z, z))
        dar_ref[...] = dar
        dai_ref[...] = dai

    vm = pl.BlockSpec(memory_space=pltpu.VMEM)
    big = jax.ShapeDtypeStruct((256, 1024), F32)
    vec = jax.ShapeDtypeStruct((1, 1024), F32)
    return pl.pallas_call(
        body, name=name, in_specs=[vm] * 13, out_specs=[vm] * 7,
        out_shape=[jax.ShapeDtypeStruct((t, 256), F32), vec, vec, big, big, big, big],
        scratch_shapes=_s5_scratch(chunk), compiler_params=_cparams(),
    )(u, dy, du_acc, x0r, x0i, xsr, xsi, a_re, a_im, bb_re, bb_im, cc_re, cc_im)


POOL_HALO = 8


def pool_apply(u_pad, n, transpose, name, tile=ROW_TILE):
    tile = min(tile, n)
    ext = tile + 2 * POOL_HALO
    trel = np.arange(ext)[None, :] - POOL_HALO - np.arange(tile)[:, None]
    if transpose:
        trel = -trel
    band4 = np.concatenate([((trel >= -(1 << w)) & (trel <= (1 << w) - 1)) for w in range(4)], axis=0).astype(np.float32)

    def body(u_ref, band_ref, lm_ref, o_ref):
        lax.fori_loop(0, n // tile, functools.partial(step, u_ref, band_ref, lm_ref, o_ref), 0)

    def step(u_ref, band_ref, lm_ref, o_ref, i, carry):
        val = u_ref[pl.ds(pl.multiple_of(i * tile, tile), ext), :]
        lane = lax.broadcasted_iota(jnp.int32, (ext, 256), 1)
        half = jnp.left_shift(1, jnp.right_shift(lane, 6))
        trow = lax.broadcasted_iota(jnp.int32, (ext, 256), 0) + (i * tile - POOL_HALO)
        cnt = jnp.minimum(trow + half, n) - jnp.maximum(trow - half, 0)
        inv = 1.0 / jnp.maximum(cnt, 1).astype(F32)
        src = val * inv if transpose else val
        acc = _fold_heads(mdot(band_ref[...], src), lm_ref)
        centre = val[POOL_HALO:POOL_HALO + tile]
        if not transpose:
            acc = acc * inv[POOL_HALO:POOL_HALO + tile]
        o_ref[pl.ds(pl.multiple_of(i * tile, tile), tile), :] = acc - centre
        return carry

    vm = pl.BlockSpec(memory_space=pltpu.VMEM)
    return pl.pallas_call(
        body, name=name, in_specs=[vm] * 3, out_specs=vm,
        out_shape=jax.ShapeDtypeStruct((n, 256), F32), compiler_params=_cparams(),
    )(u_pad, jnp.asarray(band4), _na_head_masks())


NA_SCALE = 64.0 ** -0.5
NEG = -1e30


def _call_with_exchange(compute, name, args, out_shapes, comm, scratch=()):
    vm = pl.BlockSpec(memory_space=pltpu.VMEM)
    n_in, n_out = len(args), len(out_shapes)
    if comm is None:
        outs = pl.pallas_call(compute, name=name, in_specs=[vm] * n_in, out_specs=[vm] * n_out, out_shape=out_shapes,
                              scratch_shapes=list(scratch), compiler_params=_cparams())(*args)
        return outs, None
    arrays, scatter = comm
    n = len(arrays)

    def body(*refs):
        c_in = refs[n_in:n_in + n]
        c_out = refs[n_in + n + n_out:n_in + 2 * n + n_out]
        scr = refs[n_in + 2 * n + n_out:n_in + 2 * n + n_out + len(scratch)]
        finish = _exchange_issue(c_in, c_out, scatter, *refs[n_in + 2 * n + n_out + len(scratch):])
        compute(*refs[:n_in], *refs[n_in + n:n_in + n + n_out], *scr)
        finish()

    hbm = pl.BlockSpec(memory_space=pl.ANY)
    outs = pl.pallas_call(
        body, name=name, in_specs=[vm] * n_in + [hbm] * n, out_specs=[vm] * n_out + [hbm] * n,
        out_shape=list(out_shapes) + _exchange_out_shapes(arrays, scatter), scratch_shapes=list(scratch) + _exchange_sems(n),
        compiler_params=_cparams(has_side_effects=True),
    )(*args, *arrays)
    return outs[:n_out], outs[n_out:]


def _na_head_masks():
    return jnp.asarray(np.stack([(np.arange(256) // 64 == h) for h in range(4)]).astype(np.float32).reshape(4, 1, 256))


def _na_window(r, rows):
    start = jnp.clip(r - 4, 0, rows - 8)
    return start, start - r + 7


def _na_probs(qh, kw, kc, bias):
    s_c = b_nt(qh, kc)
    m = jnp.max(s_c, axis=-1, keepdims=True)
    if kw is not None:
        s_w = b_nt(qh, kw) + bias
        m = jnp.maximum(m, jnp.max(s_w, axis=-1, keepdims=True))
        p_w = jnp.exp(s_w - m)
    p_c = jnp.exp(s_c - m)
    l = jnp.sum(p_c, axis=-1, keepdims=True)
    if kw is not None:
        l = l + jnp.sum(p_w, axis=-1, keepdims=True)
        return p_w / l, p_c / l
    return None, p_c / l


def na_fwd(q, k, v, bias8, n_ctx_rows, name, comm=None):
    t = q.shape[0]
    m_ctx = n_ctx_rows
    rows = (t - m_ctx) // GRID_W
    hm = _na_head_masks()

    def body(q_ref, k_ref, v_ref, b_ref, hm_ref, o_ref):
        kc, vc = k_ref[0:m_ctx, :], v_ref[0:m_ctx, :]

        def ctx_step(i, _):
            rs = pl.ds(pl.multiple_of(i * 64, 64), 64)
            q4 = _stack_heads(q_ref[rs, :] * NA_SCALE, hm_ref)
            _, p_c = _na_probs(q4, None, kc, None)
            o_ref[rs, :] = _fold_heads(b_nn(p_c, vc), hm_ref)
            return 0

        lax.fori_loop(0, m_ctx // 64, ctx_step, 0)

        def lat_step(r, _):
            start, off = _na_window(r, rows)
            rs = pl.ds(pl.multiple_of(m_ctx + r * 64, 64), 64)
            ws = pl.ds(pl.multiple_of(m_ctx + start * 64, 64), 512)
            q4 = _stack_heads(q_ref[rs, :] * NA_SCALE, hm_ref)
            kw, vw = k_ref[ws, :], v_ref[ws, :]
            p_w, p_c = _na_probs(q4, kw, kc, b_ref[off])
            o_ref[rs, :] = _fold_heads(b_nn(p_w, vw) + b_nn(p_c, vc), hm_ref)
            return 0

        lax.fori_loop(0, rows, lat_step, 0)

    (o,), received = _call_with_exchange(body, name, [q, k, v, bias8, hm], [jax.ShapeDtypeStruct((t, 256), F32)], comm)
    return o if comm is None else (o, received)


def na_bwd(q, k, v, do, bias8, n_ctx_rows, name, comm=None):
    t = q.shape[0]
    m_ctx = n_ctx_rows
    rows = (t - m_ctx) // GRID_W
    hm = _na_head_masks()

    def body(q_ref, k_ref, v_ref, do_ref, b_ref, hm_ref, dq_ref, dk_ref, dv_ref, db_ref):
        kc, vc = k_ref[0:m_ctx, :], v_ref[0:m_ctx, :]
        dk_ref[...] = jnp.zeros_like(dk_ref)
        dv_ref[...] = jnp.zeros_like(dv_ref)
        db_ref[...] = jnp.zeros_like(db_ref)

        def head_terms(qh, doh, kw, vw, bias):
            p_w, p_c = _na_probs(qh, kw, kc, bias)
            dp_c = b_nt(doh, vc)
            delta = jnp.sum(p_c * dp_c, axis=-1, keepdims=True)
            if kw is not None:
                dp_w = b_nt(doh, vw)
                delta = delta + jnp.sum(p_w * dp_w, axis=-1, keepdims=True)
                ds_w = p_w * (dp_w - delta)
            else:
                ds_w = None
            ds_c = p_c * (dp_c - delta)
            return p_w, p_c, ds_w, ds_c

        def ctx_step(i, carry):
            dkc, dvc = carry
            rs = pl.ds(pl.multiple_of(i * 64, 64), 64)
            q4, do4 = _stack_heads(q_ref[rs, :] * NA_SCALE, hm_ref), _stack_heads(do_ref[rs, :], hm_ref)
            _, p_c, _, ds_c = head_terms(q4, do4, None, None, None)
            dq_ref[rs, :] = _fold_heads(b_nn(ds_c, kc), hm_ref) * NA_SCALE
            return dkc + b_tn(ds_c, q4), dvc + b_tn(p_c, do4)

        zc = jnp.zeros((m_ctx, 256), F32)
        carry = lax.fori_loop(0, m_ctx // 64, ctx_step, (zc, zc))

        def lat_step(r, carry):
            dkc, dvc = carry
            start, off = _na_window(r, rows)
            rs = pl.ds(pl.multiple_of(m_ctx + r * 64, 64), 64)
            ws = pl.ds(pl.multiple_of(m_ctx + start * 64, 64), 512)
            q4, do4 = _stack_heads(q_ref[rs, :] * NA_SCALE, hm_ref), _stack_heads(do_ref[rs, :], hm_ref)
            kw, vw = k_ref[ws, :], v_ref[ws, :]
            p_w, p_c, ds_w, ds_c = head_terms(q4, do4, kw, vw, b_ref[off])
            dq_ref[rs, :] = _fold_heads(b_nn(ds_w, kw) + b_nn(ds_c, kc), hm_ref) * NA_SCALE
            dk_ref[ws, :] += b_tn(ds_w, q4)
            dv_ref[ws, :] += b_tn(p_w, do4)
            db_ref[off] += ds_w
            return dkc + b_tn(ds_c, q4), dvc + b_tn(p_c, do4)

        dkc, dvc = lax.fori_loop(0, rows, lat_step, carry)
        dk_ref[0:m_ctx, :] = dkc
        dv_ref[0:m_ctx, :] = dvc

    row = jax.ShapeDtypeStruct((t, 256), F32)
    return _call_with_exchange(body, name, [q, k, v, do, bias8, hm], [row, row, row, jax.ShapeDtypeStruct(bias8.shape, F32)], comm)


def _na_toeplitz():
    col = np.arange(GRID_W)
    dd = (col[None, :] - col[:, None] + 15).reshape(-1)
    tt = np.zeros((GRID_W * GRID_W, 128), np.float32)
    ok = (dd >= 0) & (dd <= 30)
    tt[np.arange(GRID_W * GRID_W)[ok], dd[ok]] = 1.0
    return tt


def _na_bias8(rpb, name):
    col = np.arange(GRID_W)
    cs = np.clip(col - 8, 0, GRID_W - 16)
    col_mask = (col[None, :] >= cs[:, None]) & (col[None, :] < cs[:, None] + 16)
    rpb2 = jnp.pad(rpb.reshape(60, 31), ((0, 4), (0, 97)))
    (toe,) = whole_fwd(lambda r_, t_: (hdot_nt(r_, t_),), name, [rpb2, jnp.asarray(_na_toeplitz())], [(64, GRID_W * GRID_W)])
    toe = toe[:60].reshape(4, 15, GRID_W, GRID_W)
    b = jnp.stack([toe[:, off:off + 8] for off in range(8)], axis=1)
    b = jnp.where(jnp.asarray(col_mask)[None, None, None], b, NEG)
    return b.transpose(1, 0, 3, 2, 4).reshape(8, 4 * GRID_W, 8 * GRID_W)


def _na_rpb_grad(dbias8, name):
    tt = _na_toeplitz()
    sel = np.zeros((64, 256), np.float32)
    for h in range(4):
        for off in range(8):
            for i in range(8):
                sel[h * 15 + off + i, h * 64 + off * 8 + i] = 1.0
    a2 = dbias8.reshape(8, 4, GRID_W, 8, GRID_W).transpose(1, 0, 3, 2, 4).reshape(256, GRID_W * GRID_W)
    (out,) = whole_fwd(lambda a, t_, s_: (hdot(s_, hdot(a, t_)),), name, [a2, jnp.asarray(tt), jnp.asarray(sel)], [(64, 128)])
    return out[:60, :31].reshape(4, 15, 31)


def f_mod(cs, b_mod, w_mod):
    s = _silu(cs)
    return bdot(s, w_mod) + b_mod, s


def loss_and_grad(z, tgt, n_ctx_rows, name, tile=ROW_TILE):
    t, d = z.shape
    tile = min(tile, n_ctx_rows)
    nct = n_ctx_rows // tile

    def body(z_ref, t_ref, dz_ref, loss_ref):
        i = pl.program_id(0)

        @pl.when(i == 0)
        def _():
            loss_ref[...] = jnp.zeros_like(loss_ref)

        @pl.when(i < nct)
        def _():
            dz_ref[...] = jnp.zeros_like(dz_ref)

        @pl.when(i >= nct)
        def _():
            diff = z_ref[...] - t_ref[...]
            dz_ref[...] = diff * (1.0 / d)
            loss_ref[...] += 0.5 * jnp.sum(jnp.sum(diff * diff, axis=-1, keepdims=True) * (1.0 / d), axis=0, keepdims=True)

    dz, loss = pl.pallas_call(
        body, name=name, grid=(t // tile,),
        in_specs=[pl.BlockSpec((tile, d), lambda i: (i, 0)),
                  pl.BlockSpec((tile, d), lambda i: (jnp.maximum(i - nct, 0), 0))],
        out_specs=[pl.BlockSpec((tile, d), lambda i: (i, 0)), pl.BlockSpec((8, 128), lambda i: (0, 0))],
        out_shape=[jax.ShapeDtypeStruct((t, d), F32), jax.ShapeDtypeStruct((8, 128), F32)],
        compiler_params=_cparams(dimension_semantics=("arbitrary",)),
    )(z, tgt)
    return loss[0, 0], dz


def adamw(parts, w, m, v, name, tile=256):
    npart, r, c = parts.shape
    tile = min(tile, r)
    assert r % tile == 0
    c1 = 1.0 / (1.0 - ADAM_B1 ** ADAM_STEP)
    c2 = 1.0 / (1.0 - ADAM_B2 ** ADAM_STEP)

    def body(p_ref, w_ref, m_ref, v_ref, g_ref, d_ref, nm_ref, nv_ref):
        g = p_ref[0].astype(F32)
        for i in range(1, npart):
            g = g + p_ref[i].astype(F32)
        nm = ADAM_B1 * m_ref[...] + (1.0 - ADAM_B1) * g
        nv = ADAM_B2 * v_ref[...] + (1.0 - ADAM_B2) * (g * g)
        g_ref[...] = g
        nm_ref[...] = nm
        nv_ref[...] = nv
        d_ref[...] = -ADAM_LR * ((nm * c1) / (jnp.sqrt(nv * c2) + ADAM_EPS) + ADAM_WD * w_ref[...])

    blk = pl.BlockSpec((tile, c), lambda i: (i, 0))
    return pl.pallas_call(
        body, name=name, grid=(r // tile,),
        in_specs=[pl.BlockSpec((npart, tile, c), lambda i: (0, i, 0)), blk, blk, blk],
        out_specs=[blk] * 4, out_shape=[jax.ShapeDtypeStruct((r, c), F32)] * 4,
        compiler_params=_cparams(dimension_semantics=("arbitrary",)),
    )(parts, w, m, v)


def _peer(x, y, c, k):
    return (1 - x if k & 4 else x, 1 - y if k & 2 else y, 1 - c if k & 1 else c)


def _exchange_out_shapes(arrays, scatter):
    return [jax.ShapeDtypeStruct(a.shape if s else (N_DEV,) + a.shape, a.dtype) for a, s in zip(arrays, scatter)]


def _exchange_sems(n):
    return [pltpu.SemaphoreType.DMA((n, N_DEV - 1)), pltpu.SemaphoreType.DMA((n, N_DEV - 1)), pltpu.SemaphoreType.DMA((n,))]


def _exchange_issue(ins, outs, scatter, send_sems, recv_sems, local_sems):
    n = len(ins)
    x, y, c = lax.axis_index("x"), lax.axis_index("y"), lax.axis_index("c")
    me = 4 * x + 2 * y + c

    def index_of(p):
        return 4 * p[0] + 2 * p[1] + p[2]

    local = []
    for a in range(n):
        src_me = ins[a].at[me] if scatter[a] else ins[a]
        loc = pltpu.make_async_copy(src_me, outs[a].at[me], local_sems.at[a])
        loc.start()
        local.append(loc)
    for k in range(1, N_DEV):
        peer = _peer(x, y, c, k)
        for a in range(n):
            src = ins[a].at[index_of(peer)] if scatter[a] else ins[a]
            pltpu.make_async_remote_copy(
                src_ref=src, dst_ref=outs[a].at[me], send_sem=send_sems.at[a, k - 1], recv_sem=recv_sems.at[a, k - 1],
                device_id=peer, device_id_type=pl.DeviceIdType.MESH).start()

    def finish():
        for k in range(1, N_DEV):
            peer = _peer(x, y, c, k)
            for a in range(n):
                src = ins[a].at[index_of(peer)] if scatter[a] else ins[a]
                cp = pltpu.make_async_remote_copy(
                    src_ref=src, dst_ref=outs[a].at[index_of(peer)], send_sem=send_sems.at[a, k - 1],
                    recv_sem=recv_sems.at[a, k - 1], device_id=peer, device_id_type=pl.DeviceIdType.MESH)
                cp.wait_send()
                cp.wait_recv()
        for loc in local:
            loc.wait()

    return finish


def exchange(arrays, scatter, name):
    n = len(arrays)

    def body(*refs):
        _exchange_issue(refs[:n], refs[n:2 * n], scatter, *refs[2 * n:])()

    hbm = pl.BlockSpec(memory_space=pl.ANY)
    return pl.pallas_call(
        body, name=name, in_specs=[hbm] * n, out_specs=[hbm] * n, out_shape=_exchange_out_shapes(arrays, scatter),
        scratch_shapes=_exchange_sems(n), compiler_params=pltpu.CompilerParams(has_side_effects=True),
    )(*arrays)


def _rope_tables(n_lat, n_ctx):
    tok = np.arange(n_lat)
    freqs = 10000.0 ** (-np.arange(0, 16, 2, dtype=np.float32) / 16.0)

    def table(pos):
        ang = pos.astype(np.float32)[:, None] * freqs[None, :]
        ang = np.concatenate([ang, ang], axis=-1)
        return np.cos(ang), np.sin(ang)

    cr, sr = table(tok // GRID_W)
    cc, sc = table(tok % GRID_W)
    cos = np.tile(np.concatenate([cr, cc], axis=-1), (1, 4))
    sin = np.tile(np.concatenate([sr, sc], axis=-1), (1, 4))
    cos = np.concatenate([np.ones((n_ctx, 128), np.float32), cos], axis=0)
    sin = np.concatenate([np.zeros((n_ctx, 128), np.float32), sin], axis=0)
    return jnp.asarray(cos, F32), jnp.asarray(sin, F32)


def _pad_w_in(w):
    z = lambda n: jnp.zeros((w.shape[0], n), w.dtype)
    return jnp.concatenate([w[:, 1824:2848], w[:, 128:384], w[:, 416:672], w[:, 672:928], w[:, 928:1184], w[:, 1312:1568],
                            w[:, 1568:1824], w[:, 0:128], w[:, 384:416], z(96), w[:, 1184:1312], z(128)], axis=1)


def _unpad_w_in(wp):
    return jnp.concatenate([wp[:, C_GK:C_GK + 128], wp[:, C_GV:C_GV + 256], wp[:, C_GG:C_GG + 32], wp[:, C_NK:C_NK + 256],
                            wp[:, C_NV:C_NV + 256], wp[:, C_SU:C_SU + 256], wp[:, C_GQ:C_GQ + 128], wp[:, C_NQ:C_NQ + 256],
                            wp[:, C_PU:C_PU + 256], wp[:, C_GT:C_GT + 1024]], axis=1)


def _pad_rows(u):
    return jnp.pad(u, ((POOL_HALO, POOL_HALO), (0, 0)))


def _block_diag4(w):
    out = jnp.zeros((256, 256), w.dtype)
    for i in range(4):
        out = lax.dynamic_update_slice(out, w[i], (64 * i, 64 * i))
    return out


def _layer_params(p, big, l):
    e_rep, e_tile, gmask, bdm = _s5_consts()
    wg = jnp.zeros((128, 256), F32)
    wg = lax.dynamic_update_slice(wg, p["gla_w_gate"][l, 0], (0, 0))
    wg = lax.dynamic_update_slice(wg, p["gla_w_gate"][l, 1], (16, 128))
    s5 = []
    for d in range(2):
        s5.append([p["s5_lam_re"][l, d], p["s5_lam_im"][l, d], p["s5_log_dt"][l, d].reshape(16, 1),
                   p["s5_b_re"][l, d].transpose(0, 2, 1).reshape(256, 64), p["s5_b_im"][l, d].transpose(0, 2, 1).reshape(256, 64),
                   p["s5_c_re"][l, d].reshape(256, 64), p["s5_c_im"][l, d].reshape(256, 64), e_rep, e_tile, gmask, bdm])
    havg = jnp.asarray((np.arange(256)[:, None] // 64 == np.arange(256)[None, :] // 64).astype(np.float32) / 64.0)
    e4 = jnp.asarray((np.arange(64)[:, None] == np.arange(256)[None, :] % 64).astype(np.float32))
    return dict(
        g_pre=p["g_pre"][l].reshape(1, D), g_post=p["g_post"][l].reshape(1, D), b_mod=p["b_mod"][l].reshape(1, 3 * D),
        w_mod=big["w_mod"], w_in=_pad_w_in(big["w_in"]), w_out=big["w_out"],
        wg=wg, bg=p["gla_b_gate"][l].reshape(1, 256), g_norm=jnp.pad(p["gla_g_norm"][l].reshape(1, 64), ((0, 7), (0, 0))),
        bias8=_na_bias8(p["na_rpb"][l], f"na_bias_l{l}"), s5=s5, s5_d=p["s5_d"][l].reshape(1, 256), w_glu=big["s5_w_glu"].astype(F32),
        b_glu=p["s5_b_glu"][l].reshape(1, 256), wpool=_block_diag4(p["pool_w"][l]), pool_scale=p["pool_scale"][l].reshape(1, 256),
        havg=havg, e4=e4)


def _matmul_tile(t, tile, steps):
    return t // steps if t % (8 * steps) == 0 else tile


def _cols(pz, start, width):
    return pz[:, start:start + width]


def _layer_fwd(z, modseg, lp, cos, sin, m_ctx, tile, s5_chunk, l, comm=None):
    t = z.shape[0]
    nct = m_ctx // tile
    nm = lambda s: f"{s}_l{l}"
    (h,) = rowwise_fwd(f_pre, nm("pre"), [z], [modseg], [lp["g_pre"]], [D], tile, nct)
    mm_tile = _matmul_tile(t, tile, 4)
    (gt,) = mm_nn_cols(h, lp["w_in"], C_GT, [1024], nm("in_proj_a"), tm=mm_tile)
    pv, nk, nv, su = mm_nn_cols(h, lp["w_in"], C_GV, [256] * 4, nm("in_proj_b"), tm=mm_tile)
    nq, pu, pk, pg, pq = mm_nn_cols(h, lp["w_in"], C_NQ, [256, 256, 128, 128, 128], nm("in_proj_c"), tm=mm_tile)
    q_r, k_r, lgf, lgb = rowwise_fwd(f_gla_prep, nm("gla_prep"), [pk, pg, pq, cos, sin], [], [lp["wg"], lp["bg"]], [128] * 4, tile, nct)
    half = None if comm is None else comm[0].shape[0] // 2
    spread = None if comm is None else [comm[0][:half], comm[1], comm[2], comm[3], comm[0][half:]]
    part = (lambda idx: None) if comm is None else (lambda idx: ([spread[i] for i in idx], [False] * len(idx)))
    (o1, st_f), _ = gla_scan_fwd(q_r, k_r, pv, lgf, jnp.zeros((t, 256), F32), m_ctx, False, nm("gla_f"))
    (o_gla, st_b), got_out = gla_scan_fwd(q_r, k_r, pv, lgb, o1, m_ctx, True, nm("gla_r"), part([2, 3]))
    received = None
    if comm is None:
        o_na = na_fwd(nq, nk, nv, lp["bias8"], m_ctx, nm("na"))
    else:
        o_na, got_in = na_fwd(nq, nk, nv, lp["bias8"], m_ctx, nm("na"), part([1]))
    s5p = [whole_fwd(f_s5_params, nm(f"s5_par{d}"), lp["s5"][d], [(1, 1024)] * 2 + [(256, 1024)] * 4) for d in range(2)]
    (y1, *states_f), got_mod_a = s5_scan_fwd(su, jnp.zeros((t, 256), F32), *s5p[0], m_ctx, s5_chunk, False, nm("s5_f"), part([0]))
    (y5, *states_b), got_mod_b = s5_scan_fwd(su, y1, *s5p[1], m_ctx, s5_chunk, True, nm("s5_r"), part([4]))
    if comm is not None:
        received = [jnp.concatenate([got_mod_a[0], got_mod_b[0]], axis=1), got_in[0], got_out[0], got_out[1]]
    pm = jnp.concatenate([pool_apply(_pad_rows(pu[:m_ctx]), m_ctx, False, nm("pool_c")),
                          pool_apply(_pad_rows(pu[m_ctx:]), t - m_ctx, False, nm("pool_x"))], axis=0)
    mix_rows = [o_gla, o_na, y5, su, pm, gt]
    mix_globs = [lp["g_norm"], lp["s5_d"], lp["w_glu"], lp["b_glu"], lp["wpool"], lp["pool_scale"], lp["havg"], lp["e4"]]
    (yg,) = rowwise_fwd(f_mix, nm("mix"), mix_rows, [], mix_globs, [D], tile, nct)
    out = mm_nn([yg], lp["w_out"], nm("out_proj"), tm=mm_tile)
    (z_new,) = rowwise_fwd(f_post, nm("post"), [z, out], [modseg], [lp["g_post"]], [D], tile, nct)
    saved = dict(z=z, h=h, pv=pv, nk=nk, nv=nv, su=su, nq=nq, pk=pk, pg=pg, pq=pq, q_r=q_r, k_r=k_r, lgf=lgf, lgb=lgb,
                 st_f=st_f, st_b=st_b, s5p=s5p, x0f=tuple(states_f), x0b=tuple(states_b), mix_rows=mix_rows, mix_globs=mix_globs,
                 yg=yg, out=out)
    return z_new, saved, received


def _f_pre_res(x, mod, g_pre):
    return f_pre(x, mod, g_pre)[0], x


def _layer_bwd(dz_new, sv, modseg, lp, cos, sin, m_ctx, tile, s5_chunk, l, comm=None, gdt=F32):
    t = dz_new.shape[0]
    nct = m_ctx // tile
    nm = lambda s: f"{s}_l{l}"
    g = {}
    dz_res, dout, dmod_post, g["g_post"] = rowwise_bwd(f_post, nm("post_b"), [sv["z"], sv["out"]], [modseg], [lp["g_post"]],
                                                       [dz_new], tile, nct, [True, True], [True])
    dyg = mm_nt([dout], lp["w_out"], nm("out_proj_dx"), tm=_matmul_tile(t, tile, 4))
    dw_tile = _matmul_tile(t, tile, 4)
    (g["w_out"],) = mm_tn(sv["yg"], [dout], nm("out_proj_dw"), tm=dw_tile, out_dtype=gdt)
    res = rowwise_bwd(f_mix, nm("mix_b"), sv["mix_rows"], [], sv["mix_globs"], [dyg], tile, nct, [True] * 6, [True] * 6 + [False] * 2)
    do_gla, do_na, dy5, dsu_a, dpm, dgt = res[:6]
    g["g_norm"], g["s5_d"], g["w_glu"], g["b_glu"], g["wpool"], g["pool_scale"] = res[6:]
    dpu = jnp.concatenate([pool_apply(_pad_rows(dpm[:m_ctx]), m_ctx, True, nm("pool_c_b")),
                           pool_apply(_pad_rows(dpm[m_ctx:]), t - m_ctx, True, nm("pool_x_b"))], axis=0)
    r_b = s5_scan_bwd(sv["su"], dy5, dsu_a, *sv["x0b"], *sv["s5p"][1], m_ctx, s5_chunk, True, nm("s5_r_b"))
    r_f = s5_scan_bwd(sv["su"], dy5, r_b[0], *sv["x0f"], *sv["s5p"][0], m_ctx, s5_chunk, False, nm("s5_f_b"))
    dsu = r_f[0]
    g["s5"] = [whole_bwd(f_s5_params, nm(f"s5_par{d}_b"), lp["s5"][d], list(r[1:]), [True] * 7 + [False] * 4)
               for d, r in ((0, r_f), (1, r_b))]
    part = (lambda idx: None) if comm is None else (lambda idx: ([comm[i] for i in idx], [True] * len(idx)))
    (dnq, dnk, dnv, dbias8), got_in = na_bwd(sv["nq"], sv["nk"], sv["nv"], do_na, lp["bias8"], m_ctx, nm("na_b"), part([1]))
    g["rpb"] = _na_rpb_grad(dbias8, nm("na_rpb_b"))
    zq, zv = jnp.zeros((t, 128), F32), jnp.zeros((t, 256), F32)
    (dq1, dk1, dv1, dlgb), got_mod = gla_scan_bwd(sv["q_r"], sv["k_r"], sv["pv"], sv["lgb"], sv["st_b"], do_gla, (zq, zq, zv), m_ctx, True,
                                                  nm("gla_r_b"), part([0]))
    (dq_r, dk_r, dpv, dlgf), got_out = gla_scan_bwd(sv["q_r"], sv["k_r"], sv["pv"], sv["lgf"], sv["st_f"], do_gla, (dq1, dk1, dv1), m_ctx, False,
                                                    nm("gla_f_b"), part([2, 3]))
    received = None if comm is None else [got_mod[0], got_in[0], got_out[0], got_out[1]]
    dpk, dpg, dpq, g["wg"], g["bg"] = rowwise_bwd(f_gla_prep, nm("gla_prep_b"), [sv["pk"], sv["pg"], sv["pq"], cos, sin], [],
                                                  [lp["wg"], lp["bg"]], [dq_r, dk_r, dlgf, dlgb], tile, nct,
                                                  [True, True, True, False, False], [True, True])
    parts = [dgt, dpv, dnk, dnv, dsu, dnq, dpu, dpk, dpg, dpq, jnp.zeros((t, 128), F32)]
    dh = mm_nt(parts, lp["w_in"], nm("in_proj_dx"), tm=_matmul_tile(t, tile, 8))
    g["w_in"] = _unpad_w_in(jnp.concatenate(mm_tn(sv["h"], parts, nm("in_proj_dw"), tm=dw_tile, out_dtype=gdt), axis=1))
    dz, dmod_pre, g["g_pre"] = rowwise_bwd(_f_pre_res, nm("pre_b"), [sv["z"]], [modseg], [lp["g_pre"]], [dh, dz_res], tile, nct, [True], [True])
    return dz, dmod_pre, dmod_post, g, received


def _f_mod_sum(cs, b_mod, w_mod):
    mod, _ = f_mod(cs, b_mod, w_mod)
    return mod, cs


def local_step(x, c, ctx, tgt, p, shards=None, tile=ROW_TILE, s5_chunk=S5_CHUNK):
    n_lat, m_ctx = x.shape[0], ctx.shape[0]
    n_layers = p["g_pre"].shape[0]
    z = jnp.concatenate([ctx, x], axis=0)
    cos, sin = _rope_tables(n_lat, m_ctx)
    cs = jnp.concatenate([c.reshape(1, D), p["c_ctx"].reshape(1, D), jnp.zeros((6, D), F32)], axis=0)
    gather = [False] * len(_SHARDED)
    lps, mods, silus, saves = [], [], [], []
    got = exchange(shards[0], gather, "gather_weights_l0") if shards is not None else None
    for l in range(n_layers):
        if shards is None:
            big = {n: p[n][l] for n in _SHARDED}
        else:
            big = {n: _gathered(g, _BY_COLS[n]) for n, g in zip(_SHARDED, got)}
        lp = _layer_params(p, big, l)
        mod8, s8 = whole_fwd(f_mod, f"mod_l{l}", [cs, lp["b_mod"], lp["w_mod"]], [(8, 3 * D), (8, D)])
        modseg = mod8[:2].reshape(2, 1, 3 * D)
        comm = shards[l + 1] if shards is not None and l + 1 < n_layers else None
        z, sv, got = _layer_fwd(z, modseg, lp, cos, sin, m_ctx, tile, s5_chunk, l, comm)
        lps.append(lp); mods.append(modseg); silus.append(s8); saves.append(sv)
    loss, dz = loss_and_grad(z, tgt, m_ctx, "loss", tile)
    grads, received = [None] * n_layers, [None] * n_layers
    gdt = F32 if shards is None else BF16
    dcs = jnp.zeros((8, D), F32)
    pending = None
    for l in reversed(range(n_layers)):
        lp = lps[l]
        dz, dmod_pre, dmod_post, g, got = _layer_bwd(dz, saves[l], mods[l], lp, cos, sin, m_ctx, tile, s5_chunk, l, pending, gdt)
        if pending is not None:
            received[l + 1] = got
        dmod = jnp.concatenate([dmod_pre.reshape(2, 3 * D)[:, :2 * D], dmod_post.reshape(2, 3 * D)[:, 2 * D:]], axis=1)
        dmod8 = jnp.pad(dmod, ((0, 6), (0, 0)))
        dcs, g["b_mod"] = whole_bwd(_f_mod_sum, f"mod_b_l{l}", [cs, lp["b_mod"], lp["w_mod"]], [dmod8, dcs], [True, True, False])
        g["w_mod"] = jnp.concatenate(mm_tn(silus[l], [dmod8[:, :D], dmod8[:, D:2 * D], dmod8[:, 2 * D:]], f"mod_dw_l{l}", tm=8, out_dtype=gdt), axis=1)
        grads[l] = g
        if shards is not None:
            pending = _layer_sends(g)
    if shards is not None:
        received[0] = exchange(pending + [_small_sends(dcs[1], grads)], [True] * len(_SHARDED) + [False], "exchange_grads_l0")
    return loss, dz[m_ctx:], dcs[1], grads, received


_WEIGHTS = ["c_ctx", "w_mod", "b_mod", "g_pre", "g_post", "w_in", "w_out", "gla_w_gate", "gla_b_gate", "gla_g_norm", "na_rpb",
            "s5_lam_re", "s5_lam_im", "s5_log_dt", "s5_b_re", "s5_b_im", "s5_c_re", "s5_c_im", "s5_d", "s5_w_glu", "s5_b_glu",
            "pool_w", "pool_scale"]
_INPUTS = ["x", "c", "ctx"] + _WEIGHTS + ["loss_target"] + ["m_" + n for n in _WEIGHTS] + ["v_" + n for n in _WEIGHTS]
_SHARDED = ["w_mod", "w_in", "w_out", "s5_w_glu"]
_BY_COLS = {"w_mod": True, "w_in": True, "w_out": False, "s5_w_glu": False}
_SMALL = [n for n in _WEIGHTS if n not in _SHARDED]
_SMALL_PER_LAYER = [n for n in _SMALL if n != "c_ctx"]
_PACK_ROWS = 256


def _pack_plan(like):
    tiled = [i for i, a in enumerate(like) if a.size % 1024 == 0]
    loose = [i for i, a in enumerate(like) if a.size % 1024 != 0]
    tail = -(-sum(like[i].size for i in loose) // 1024) * 8
    rows = sum(like[i].size // 128 for i in tiled) + tail
    return tiled, loose, tail, -(-rows // _PACK_ROWS) * _PACK_ROWS - rows


def _pack(arrs):
    tiled, loose, tail, fill = _pack_plan(arrs)
    dt = arrs[0].dtype
    flat = jnp.concatenate([arrs[i].reshape(-1) for i in loose])
    flat = jnp.pad(flat, (0, tail * 128 - flat.shape[0])).reshape(tail, 128)
    return jnp.concatenate([arrs[i].reshape(-1, 128) for i in tiled] + [flat, jnp.zeros((fill, 128), dt)], axis=0)


def _unpack(packed, like):
    tiled, loose, tail, _ = _pack_plan(like)
    out, row = [None] * len(like), 0
    for i in tiled:
        n = like[i].size // 128
        out[i] = packed[row:row + n].reshape(like[i].shape)
        row += n
    flat, pos = packed[row:row + tail].reshape(-1), 0
    for i in loose:
        out[i] = flat[pos:pos + like[i].size].reshape(like[i].shape)
        pos += like[i].size
    return out


def _gathered(g, cols):
    if cols:
        return g.transpose(1, 0, 2).reshape(g.shape[1], N_DEV * g.shape[2])
    return g.reshape(N_DEV * g.shape[1], g.shape[2])


def _slabs(w, cols):
    r, c = w.shape
    if cols:
        return w.reshape(r, N_DEV, c // N_DEV).transpose(1, 0, 2)
    return w.reshape(N_DEV, r // N_DEV, c)


def _layer_small(g):
    s5 = lambda i, f: jnp.stack([f(g["s5"][d][i]) for d in range(2)])
    return {
        "b_mod": g["b_mod"].reshape(3 * D), "g_pre": g["g_pre"].reshape(D), "g_post": g["g_post"].reshape(D),
        "gla_w_gate": jnp.stack([g["wg"][0:16, 0:128], g["wg"][16:32, 128:256]]),
        "gla_b_gate": g["bg"].reshape(2, 128), "gla_g_norm": g["g_norm"][0], "na_rpb": g["rpb"],
        "s5_lam_re": s5(0, lambda a: a), "s5_lam_im": s5(1, lambda a: a), "s5_log_dt": s5(2, lambda a: a.reshape(16)),
        "s5_b_re": s5(3, lambda a: a.reshape(16, 16, 64).transpose(0, 2, 1)),
        "s5_b_im": s5(4, lambda a: a.reshape(16, 16, 64).transpose(0, 2, 1)),
        "s5_c_re": s5(5, lambda a: a.reshape(16, 16, 64)), "s5_c_im": s5(6, lambda a: a.reshape(16, 16, 64)),
        "s5_d": g["s5_d"].reshape(256), "s5_b_glu": g["b_glu"].reshape(256),
        "pool_w": jnp.stack([g["wpool"][64 * i:64 * i + 64, 64 * i:64 * i + 64] for i in range(4)]),
        "pool_scale": g["pool_scale"].reshape(256),
    }


def _layer_sends(g):
    big = {"w_mod": g["w_mod"], "w_in": g["w_in"], "w_out": g["w_out"], "s5_w_glu": g["w_glu"]}
    return [_slabs(big[n], _BY_COLS[n]).astype(BF16) for n in _SHARDED]


def _small_sends(d_c_ctx, grads):
    per_layer = [_layer_small(g) for g in grads]
    full = {n: jnp.stack([s[n] for s in per_layer]) for n in _SMALL_PER_LAYER}
    full["c_ctx"] = d_c_ctx
    return _pack([full[n] for n in _SMALL]).astype(BF16)


def kernel(x, c, ctx, c_ctx, w_mod, b_mod, g_pre, g_post, w_in, w_out, gla_w_gate, gla_b_gate, gla_g_norm, na_rpb, s5_lam_re, s5_lam_im, s5_log_dt, s5_b_re, s5_b_im, s5_c_re, s5_c_im, s5_d, s5_w_glu, s5_b_glu, pool_w, pool_scale, loss_target, m_c_ctx, m_w_mod, m_b_mod, m_g_pre, m_g_post, m_w_in, m_w_out, m_gla_w_gate, m_gla_b_gate, m_gla_g_norm, m_na_rpb, m_s5_lam_re, m_s5_lam_im, m_s5_log_dt, m_s5_b_re, m_s5_b_im, m_s5_c_re, m_s5_c_im, m_s5_d, m_s5_w_glu, m_s5_b_glu, m_pool_w, m_pool_scale, v_c_ctx, v_w_mod, v_b_mod, v_g_pre, v_g_post, v_w_in, v_w_out, v_gla_w_gate, v_gla_b_gate, v_gla_g_norm, v_na_rpb, v_s5_lam_re, v_s5_lam_im, v_s5_log_dt, v_s5_b_re, v_s5_b_im, v_s5_c_re, v_s5_c_im, v_s5_d, v_s5_w_glu, v_s5_b_glu, v_pool_w, v_pool_scale):
    given = dict(zip(_INPUTS, (x, c, ctx, c_ctx, w_mod, b_mod, g_pre, g_post, w_in, w_out, gla_w_gate, gla_b_gate, gla_g_norm, na_rpb, s5_lam_re, s5_lam_im, s5_log_dt, s5_b_re, s5_b_im, s5_c_re, s5_c_im, s5_d, s5_w_glu, s5_b_glu, pool_w, pool_scale, loss_target, m_c_ctx, m_w_mod, m_b_mod, m_g_pre, m_g_post, m_w_in, m_w_out, m_gla_w_gate, m_gla_b_gate, m_gla_g_norm, m_na_rpb, m_s5_lam_re, m_s5_lam_im, m_s5_log_dt, m_s5_b_re, m_s5_b_im, m_s5_c_re, m_s5_c_im, m_s5_d, m_s5_w_glu, m_s5_b_glu, m_pool_w, m_pool_scale, v_c_ctx, v_w_mod, v_b_mod, v_g_pre, v_g_post, v_w_in, v_w_out, v_gla_w_gate, v_gla_b_gate, v_gla_g_norm, v_na_rpb, v_s5_lam_re, v_s5_lam_im, v_s5_log_dt, v_s5_b_re, v_s5_b_im, v_s5_c_re, v_s5_c_im, v_s5_d, v_s5_w_glu, v_s5_b_glu, v_pool_w, v_pool_scale)))
    n_layers = w_in.shape[0]
    shards = [[given[n][l].astype(BF16) for n in _SHARDED] for l in range(n_layers)]
    p = {n: given[n] for n in _SMALL}
    loss, grad_x, _, _, received = local_step(x[0], c, ctx[0], loss_target[0], p, shards)
    final = {}
    for n in _SHARDED:
        per_layer = [adamw(received[l][_SHARDED.index(n)], given[n][l], given["m_" + n][l], given["v_" + n][l], f"adamw_{n}_l{l}")
                     for l in range(n_layers)]
        final[n] = [jnp.stack([res[kind] for res in per_layer]) for kind in range(4)]
    like = [given[n] for n in _SMALL]
    res = adamw(received[0][-1], _pack(like), _pack([given["m_" + n] for n in _SMALL]), _pack([given["v_" + n] for n in _SMALL]),
                "adamw_small")
    unpacked = [_unpack(packed, like) for packed in res]
    for i, n in enumerate(_SMALL):
        final[n] = [unpacked[kind][i] for kind in range(4)]
    loss = lax.psum(loss, ("x", "y", "c"))
    return (loss, grad_x[None], *[final[n][0] for n in _WEIGHTS], *[final[n][1] for n in _WEIGHTS],
            *[final[n][2] for n in _WEIGHTS], *[final[n][3] for n in _WEIGHTS])
```

```python
import functools
import math

import numpy as np
import jax
import jax.numpy as jnp
from jax import lax
from jax.experimental import pallas as pl
from jax.experimental.pallas import tpu as pltpu

F32 = jnp.float32
BF16 = jnp.bfloat16
HIGHEST = lax.Precision.HIGHEST
HIGH = lax.Precision.HIGH

D = 1024
GRID_W = 64
EPS = 1e-6
N_DEV = 8
C_GT, C_GV, C_NK, C_NV, C_SU, C_NQ, C_PU, C_GK, C_GG, C_GQ, C_END = 0, 1024, 1280, 1536, 1792, 2048, 2304, 2560, 2688, 2816, 2944
PW = 3072
N_CTX_ORIG = 416
N_IN = 2848
GLA_CHUNK = 128
S5_CHUNK = 256
ROW_TILE = 256
VMEM_LIMIT = 56 * 1024 * 1024

ADAM_LR, ADAM_B1, ADAM_B2, ADAM_EPS, ADAM_WD, ADAM_STEP = 0.001, 0.9, 0.999, 1e-08, 0.01, 10


def _cparams(**kw):
    return pltpu.CompilerParams(vmem_limit_bytes=VMEM_LIMIT, **kw)


def _dg(a, b, ca, cb, precision=None):
    return lax.dot_general(a, b, (((ca,), (cb,)), ((), ())), precision=precision, preferred_element_type=F32)


def hdot(a, b):
    return _dg(a, b, 1, 0, HIGHEST)


def hdot_nt(a, b):
    return _dg(a, b, 1, 1, HIGHEST)


def hdot_tn(a, b):
    return _dg(a, b, 0, 0, HIGHEST)


def mdot(a, b):
    return _dg(a, b, 1, 0, HIGH)


def mdot_nt(a, b):
    return _dg(a, b, 1, 1, HIGH)


def mdot_tn(a, b):
    return _dg(a, b, 0, 0, HIGH)


def b_nn(a, b):
    return _dg(a.astype(BF16), b.astype(BF16), 1, 0)


def b_nt(a, b):
    return _dg(a.astype(BF16), b.astype(BF16), 1, 1)


def b_tn(a, b):
    return _dg(a.astype(BF16), b.astype(BF16), 0, 0)


@jax.custom_vjp
def bdot(a, b):
    return b_nn(a, b)


def _bdot_fwd(a, b):
    return b_nn(a, b), (a, b)


def _bdot_bwd(res, ct):
    a, b = res
    return b_nt(ct, b).astype(a.dtype), b_tn(a, ct).astype(b.dtype)


bdot.defvjp(_bdot_fwd, _bdot_bwd)


def _log_sigmoid(z):
    return jnp.minimum(z, 0.0) - jnp.log(1.0 + jnp.exp(-jnp.abs(z)))


def _silu(z):
    return z * jax.nn.sigmoid(z)


def _gelu(z):
    return 0.5 * z * (1.0 + jnp.tanh(math.sqrt(2.0 / math.pi) * (z + 0.044715 * (z * z * z))))


def _cat(vals):
    return vals[0] if len(vals) == 1 else jnp.concatenate(vals, axis=-1)


def mm_nn(a_parts, b, name, tm=ROW_TILE, tn=1024):
    t = a_parts[0].shape[0]
    k, n = b.shape
    na = len(a_parts)
    tn = min(tn, n)

    def body(*refs):
        a = _cat([r[...].astype(BF16) for r in refs[:na]])
        refs[na + 1][...] = _dg(a, refs[na][...].astype(BF16), 1, 0)

    return pl.pallas_call(
        body, name=name, grid=(n // tn, t // tm),
        in_specs=[pl.BlockSpec((tm, p.shape[1]), lambda j, i: (i, 0)) for p in a_parts]
        + [pl.BlockSpec((k, tn), lambda j, i: (0, j))],
        out_specs=pl.BlockSpec((tm, tn), lambda j, i: (i, j)),
        out_shape=jax.ShapeDtypeStruct((t, n), F32),
        compiler_params=_cparams(dimension_semantics=("arbitrary", "arbitrary")),
    )(*a_parts, b)


def mm_nn_cols(a, b, start, widths, name, tm=ROW_TILE):
    t, k = a.shape
    tn = 1024
    assert start % tn == 0 and sum(widths) <= tn

    def body(a_ref, b_ref, *o_refs):
        r = _dg(a_ref[...].astype(BF16), b_ref[...].astype(BF16), 1, 0)
        off = 0
        for o_ref, w in zip(o_refs, widths):
            o_ref[...] = r[:, off:off + w]
            off += w

    return pl.pallas_call(
        body, name=name, grid=(t // tm,),
        in_specs=[pl.BlockSpec((tm, k), lambda i: (i, 0)), pl.BlockSpec((k, tn), lambda i: (0, start // tn))],
        out_specs=[pl.BlockSpec((tm, w), lambda i: (i, 0)) for w in widths],
        out_shape=[jax.ShapeDtypeStruct((t, w), F32) for w in widths],
        compiler_params=_cparams(dimension_semantics=("arbitrary",)),
    )(a, b)


def mm_nt(a_parts, b, name, tm=ROW_TILE):
    t = a_parts[0].shape[0]
    n, k = b.shape
    na = len(a_parts)

    def body(*refs):
        a = _cat([r[...].astype(BF16) for r in refs[:na]])
        refs[na + 1][...] = _dg(a, refs[na][...].astype(BF16), 1, 1)

    return pl.pallas_call(
        body, name=name, grid=(t // tm,),
        in_specs=[pl.BlockSpec((tm, p.shape[1]), lambda i: (i, 0)) for p in a_parts]
        + [pl.BlockSpec((n, k), lambda i: (0, 0))],
        out_specs=pl.BlockSpec((tm, n), lambda i: (i, 0)),
        out_shape=jax.ShapeDtypeStruct((t, n), F32),
        compiler_params=_cparams(dimension_semantics=("arbitrary",)),
    )(*a_parts, b)


def mm_tn(a, b_parts, name, tm=ROW_TILE, tn=1024, out_dtype=F32):
    t, k = a.shape
    widths = [p.shape[1] for p in b_parts]
    n = sum(widths)
    assert n % tn == 0
    groups, cur, acc = [], [], 0
    for idx, w in enumerate(widths):
        cur.append(idx)
        acc += w
        if acc == tn:
            groups.append(cur)
            cur, acc = [], 0
        assert acc < tn
    assert not cur
    outs = []
    for gi, grp in enumerate(groups):
        parts = [b_parts[i] for i in grp]
        npart = len(parts)
        nsteps = t // tm

        def body(*refs, npart=npart, nsteps=nsteps):
            a_v = refs[0][...].astype(BF16)
            b_v = _cat([r[...].astype(BF16) for r in refs[1:1 + npart]])
            o_ref, acc_ref = refs[1 + npart], refs[2 + npart]
            r = _dg(a_v, b_v, 0, 0)

            @pl.when(pl.program_id(0) == 0)
            def _():
                acc_ref[...] = r

            @pl.when(pl.program_id(0) != 0)
            def _():
                acc_ref[...] += r

            @pl.when(pl.program_id(0) == nsteps - 1)
            def _():
                o_ref[...] = acc_ref[...].astype(o_ref.dtype)

        outs.append(pl.pallas_call(
            body, name=f"{name}_{gi}", grid=(nsteps,),
            in_specs=[pl.BlockSpec((tm, k), lambda i: (i, 0))]
            + [pl.BlockSpec((tm, p.shape[1]), lambda i: (i, 0)) for p in parts],
            out_specs=pl.BlockSpec((k, tn), lambda i: (0, 0)),
            out_shape=jax.ShapeDtypeStruct((k, tn), out_dtype),
            scratch_shapes=[pltpu.VMEM((k, tn), F32)],
            compiler_params=_cparams(dimension_semantics=("arbitrary",)),
        )(a, *parts))
    return outs


def _seg_of(i, nct):
    return jnp.where(i < nct, 1, 0)


def rowwise_fwd(fn, name, rows, segs, globs, out_widths, tile, nct):
    t = rows[0].shape[0]
    nr, ns, ng = len(rows), len(segs), len(globs)

    def body(*refs):
        vals = [r[...] for r in refs[:nr]] + [r[0] for r in refs[nr:nr + ns]] + [r[...] for r in refs[nr + ns:nr + ns + ng]]
        outs = fn(*vals)
        for o_ref, o in zip(refs[nr + ns + ng:], outs):
            o_ref[...] = o

    return pl.pallas_call(
        body, name=name, grid=(t // tile,),
        in_specs=[pl.BlockSpec((tile, r.shape[1]), lambda i: (i, 0)) for r in rows]
        + [pl.BlockSpec((1, 1, s.shape[2]), lambda i: (_seg_of(i, nct), 0, 0)) for s in segs]
        + [pl.BlockSpec(g.shape, lambda i: (0, 0)) for g in globs],
        out_specs=[pl.BlockSpec((tile, w), lambda i: (i, 0)) for w in out_widths],
        out_shape=[jax.ShapeDtypeStruct((t, w), F32) for w in out_widths],
        compiler_params=_cparams(dimension_semantics=("arbitrary",)),
    )(*rows, *segs, *globs)


def rowwise_bwd(fn, name, rows, segs, globs, cts, tile, nct, row_diff, glob_diff):
    t = rows[0].shape[0]
    nr, ns, ng, nc = len(rows), len(segs), len(globs), len(cts)
    d_rows = [i for i in range(nr) if row_diff[i]]
    d_globs = [i for i in range(ng) if glob_diff[i]]

    def body(*refs):
        in_refs, out_refs = refs[:nr + ns + ng + nc], refs[nr + ns + ng + nc:]
        row_v = [r[...] for r in in_refs[:nr]]
        seg_v = [r[0] for r in in_refs[nr:nr + ns]]
        glob_v = [r[...] for r in in_refs[nr + ns:nr + ns + ng]]
        ct_v = tuple(r[...] for r in in_refs[nr + ns + ng:])

        def wrapped(dr, sv, dg):
            rv = list(row_v)
            for j, i in enumerate(d_rows):
                rv[i] = dr[j]
            gv = list(glob_v)
            for j, i in enumerate(d_globs):
                gv[i] = dg[j]
            return tuple(fn(*rv, *sv, *gv))

        _, vjp = jax.vjp(wrapped, [row_v[i] for i in d_rows], seg_v, [glob_v[i] for i in d_globs])
        c_rows, c_segs, c_globs = vjp(ct_v)
        i = pl.program_id(0)
        k = 0
        for c in c_rows:
            out_refs[k][...] = c
            k += 1
        seg_first = jnp.logical_or(i == 0, i == nct)
        for c in c_segs:
            ref = out_refs[k]
            k += 1

            @pl.when(seg_first)
            def _(ref=ref, c=c):
                ref[0] = c

            @pl.when(jnp.logical_not(seg_first))
            def _(ref=ref, c=c):
                ref[0] += c
        for c in c_globs:
            ref = out_refs[k]
            k += 1

            @pl.when(i == 0)
            def _(ref=ref, c=c):
                ref[...] = c

            @pl.when(i != 0)
            def _(ref=ref, c=c):
                ref[...] += c

    return pl.pallas_call(
        body, name=name, grid=(t // tile,),
        in_specs=[pl.BlockSpec((tile, r.shape[1]), lambda i: (i, 0)) for r in rows]
        + [pl.BlockSpec((1, 1, s.shape[2]), lambda i: (_seg_of(i, nct), 0, 0)) for s in segs]
        + [pl.BlockSpec(g.shape, lambda i: (0, 0)) for g in globs]
        + [pl.BlockSpec((tile, c.shape[1]), lambda i: (i, 0)) for c in cts],
        out_specs=[pl.BlockSpec((tile, rows[i].shape[1]), lambda i: (i, 0)) for i in d_rows]
        + [pl.BlockSpec((1, 1, s.shape[2]), lambda i: (_seg_of(i, nct), 0, 0)) for s in segs]
        + [pl.BlockSpec(globs[i].shape, lambda i: (0, 0)) for i in d_globs],
        out_shape=[jax.ShapeDtypeStruct(rows[i].shape, F32) for i in d_rows]
        + [jax.ShapeDtypeStruct(s.shape, F32) for s in segs]
        + [jax.ShapeDtypeStruct(globs[i].shape, F32) for i in d_globs],
        compiler_params=_cparams(dimension_semantics=("arbitrary",)),
    )(*rows, *segs, *globs, *cts)


def f_pre(x, mod, g_pre):
    shift, scale = mod[:, :D], mod[:, D:2 * D]
    rs = lax.rsqrt(jnp.mean(x * x, axis=-1, keepdims=True) + EPS)
    return ((x * rs) * g_pre * (1.0 + scale) + shift,)


def f_post(x, out, mod, g_post):
    gate = mod[:, 2 * D:]
    rs = lax.rsqrt(jnp.mean(out * out, axis=-1, keepdims=True) + EPS)
    return (x + gate * ((out * rs) * g_post),)


def f_mix(o_gla, o_na, y5, u5, pm, gcols, g_norm, s5_d, w_glu, b_glu, wpool, pool_scale, havg, e4):
    ms = mdot(o_gla * o_gla, havg)
    y_gla = o_gla * lax.rsqrt(ms + EPS) * jnp.sum(hdot(g_norm, e4), axis=0, keepdims=True)
    g = _gelu(u5 * s5_d + y5)
    y_s5 = g * jax.nn.sigmoid(bdot(g, w_glu) + b_glu)
    y_pool = bdot(pm, wpool) * pool_scale
    ycat = jnp.concatenate([y_gla, o_na, y_s5, y_pool], axis=-1)
    return (ycat * _silu(gcols),)


@jax.custom_vjp
def _rot_half16(x):
    lane = lax.broadcasted_iota(jnp.int32, x.shape, 1)
    first = jnp.bitwise_and(lane, 15) < 8
    return jnp.where(first, -pltpu.roll(x, x.shape[1] - 8, 1), pltpu.roll(x, 8, 1))


def _rot_fwd(x):
    return _rot_half16(x), None


def _rot_bwd(_, ct):
    return (-_rot_half16(ct),)


_rot_half16.defvjp(_rot_fwd, _rot_bwd)


def f_gla_prep(pk, pg, pq, cos, sin, wg, bg):
    z = bdot(pg, wg) + bg
    lg = _log_sigmoid(z) * (1.0 / 16.0)
    k_r = pk * cos + _rot_half16(pk) * sin
    q_r = (pq * cos + _rot_half16(pq) * sin) * (32.0 ** -0.5)
    return q_r, k_r, lg[:, :128], lg[:, 128:]


def _gla_consts(rev):
    c = GLA_CHUNK
    i = np.arange(c)
    inc = (i[None, :] >= i[:, None]) if rev else (i[None, :] <= i[:, None])
    mq = np.stack([(np.arange(128) // 32 == h) for h in range(4)]).astype(np.float32).reshape(4, 1, 128)
    mv = np.stack([(np.arange(256) // 64 == h) for h in range(4)]).astype(np.float32).reshape(4, 1, 256)
    bdt = (np.arange(256)[:, None] // 64 == np.arange(128)[None, :] // 32).astype(np.float32)
    inc = inc.astype(np.float32)
    return jnp.asarray(inc), jnp.asarray(inc.T.copy()), jnp.asarray(mq), jnp.asarray(mv), jnp.asarray(bdt)


def _stack_heads(x, m_ref):
    return jnp.concatenate([x * m_ref[h] for h in range(4)], axis=0)


def _tile4(m):
    return jnp.concatenate([m, m, m, m], axis=0)


def _fold_heads(r4, m_ref):
    r = r4.shape[0] // 4
    out = m_ref[0] * r4[0:r]
    for h in range(1, 4):
        out = out + m_ref[h] * r4[h * r:(h + 1) * r]
    return out


def _gla_chunk_of(s, n_ctx_chunks, n_chunks, rev):
    if not rev:
        return s
    return jnp.where(s < n_ctx_chunks, n_ctx_chunks - 1 - s, n_ctx_chunks + n_chunks - 1 - s)


def gla_scan_fwd(q, k, v, lg, acc, n_ctx_rows, rev, name, comm=None):
    t = q.shape[0]
    nch, ncc = t // GLA_CHUNK, n_ctx_rows // GLA_CHUNK
    inc, inc_t, mq, mv, bdt = _gla_consts(rev)

    def body(q_ref, k_ref, v_ref, lg_ref, acc_ref, inc_ref, inct_ref, mq_ref, mv_ref, bdt_ref, o_ref, st_ref):
        lmask, lmask_t = inc_ref[...], inct_ref[...]
        bd = bdt_ref[...]

        def step(s, st):
            c = _gla_chunk_of(s, ncc, nch, rev)
            rows = pl.ds(pl.multiple_of(c * GLA_CHUNK, GLA_CHUNK), GLA_CHUNK)
            qc, kc, vc, lgc = q_ref[rows, :], k_ref[rows, :], v_ref[rows, :], lg_ref[rows, :]
            st_ref[c] = st
            b = hdot(lmask, lgc)
            blast = jnp.sum(lgc, axis=0, keepdims=True)
            qe, ke, kd = qc * jnp.exp(b), kc * jnp.exp(-b), kc * jnp.exp(blast - b)
            ke4, v4 = _stack_heads(ke, mq_ref), _stack_heads(vc, mv_ref)
            at = _tile4(lmask_t) * b_nt(ke4, qe)
            o_ref[rows, :] = acc_ref[rows, :] + b_nt(qe, st) + b_tn(at, v4)
            return st * jnp.exp(blast) + bd * mdot_tn(vc, kd)

        lax.fori_loop(0, nch, step, jnp.zeros((256, 128), F32))

    return _call_with_exchange(body, name, [q, k, v, lg, acc, inc, inc_t, mq, mv, bdt],
                               [jax.ShapeDtypeStruct((t, 256), F32), jax.ShapeDtypeStruct((nch, 256, 128), F32)], comm)


def gla_scan_bwd(q, k, v, lg, st, do, acc, n_ctx_rows, rev, name, comm=None):
    t = q.shape[0]
    nch, ncc = t // GLA_CHUNK, n_ctx_rows // GLA_CHUNK
    inc, inc_t, mq, mv, bdt = _gla_consts(rev)

    def body(q_ref, k_ref, v_ref, lg_ref, st_ref, do_ref, aq_ref, ak_ref, av_ref, inc_ref, inct_ref, mq_ref, mv_ref, bdt_ref,
             dq_ref, dk_ref, dv_ref, dlg_ref):
        lmask, lmask_t = inc_ref[...], inct_ref[...]
        bd = bdt_ref[...]

        def step(j, carry):
            dst, gsum = carry
            s = nch - 1 - j
            c = _gla_chunk_of(s, ncc, nch, rev)
            rows = pl.ds(pl.multiple_of(c * GLA_CHUNK, GLA_CHUNK), GLA_CHUNK)
            qc, kc, vc, lgc, doc = q_ref[rows, :], k_ref[rows, :], v_ref[rows, :], lg_ref[rows, :], do_ref[rows, :]
            stc = st_ref[c]
            b = hdot(lmask, lgc)
            blast = jnp.sum(lgc, axis=0, keepdims=True)
            eb, enb, edb = jnp.exp(b), jnp.exp(-b), jnp.exp(blast - b)
            qe, ke, kd = qc * eb, kc * enb, kc * edb
            ke4, v4 = _stack_heads(ke, mq_ref), _stack_heads(vc, mv_ref)
            lm4 = _tile4(lmask_t)
            at = lm4 * b_nt(ke4, qe)
            dat = lm4 * mdot_nt(v4, doc)
            dqe = mdot(doc, stc) + mdot_tn(dat, ke4)
            dke = _fold_heads(mdot(dat, qe), mq_ref)
            dv = b_nt(kd, dst) + _fold_heads(b_nn(at, doc), mv_ref)
            dkd = mdot(vc, dst)
            dq = dqe * eb
            dk = dke * enb + dkd * edb
            g = qc * dq - kc * dk
            dlg_ref[rows, :] = hdot_tn(lmask, g) + gsum
            dq_ref[rows, :] = aq_ref[rows, :] + dq
            dk_ref[rows, :] = ak_ref[rows, :] + dk
            dv_ref[rows, :] = av_ref[rows, :] + dv
            dst_new = dst * jnp.exp(blast) + bd * mdot_tn(doc, qe)
            return dst_new, gsum + jnp.sum(g, axis=0, keepdims=True)

        lax.fori_loop(0, nch, step, (jnp.zeros((256, 128), F32), jnp.zeros((1, 128), F32)))

    return _call_with_exchange(body, name, [q, k, v, lg, st, do, *acc, inc, inc_t, mq, mv, bdt],
                               [jax.ShapeDtypeStruct((t, 128), F32), jax.ShapeDtypeStruct((t, 128), F32),
                                jax.ShapeDtypeStruct((t, 256), F32), jax.ShapeDtypeStruct((t, 128), F32)], comm)


def whole_fwd(fn, name, args, out_shapes):
    def body(*refs):
        outs = fn(*[r[...] for r in refs[:len(args)]])
        for o_ref, o in zip(refs[len(args):], outs):
            o_ref[...] = o

    vm = pl.BlockSpec(memory_space=pltpu.VMEM)
    return pl.pallas_call(
        body, name=name, in_specs=[vm] * len(args), out_specs=[vm] * len(out_shapes),
        out_shape=[jax.ShapeDtypeStruct(s, F32) for s in out_shapes], compiler_params=_cparams(),
    )(*args)


def whole_bwd(fn, name, args, cts, diff):
    d_idx = [i for i in range(len(args)) if diff[i]]

    def body(*refs):
        vals = [r[...] for r in refs[:len(args)]]
        ct_v = tuple(r[...] for r in refs[len(args):len(args) + len(cts)])

        def wrapped(dv):
            av = list(vals)
            for j, i in enumerate(d_idx):
                av[i] = dv[j]
            return tuple(fn(*av))

        _, vjp = jax.vjp(wrapped, [vals[i] for i in d_idx])
        (c_args,) = vjp(ct_v)
        for o_ref, c in zip(refs[len(args) + len(cts):], c_args):
            o_ref[...] = c

    vm = pl.BlockSpec(memory_space=pltpu.VMEM)
    return pl.pallas_call(
        body, name=name, in_specs=[vm] * (len(args) + len(cts)), out_specs=[vm] * len(d_idx),
        out_shape=[jax.ShapeDtypeStruct(args[i].shape, F32) for i in d_idx], compiler_params=_cparams(),
    )(*args, *cts)


def _s5_consts():
    e_rep = (np.arange(256)[:, None] // 16 == np.arange(16)[None, :]).astype(np.float32)
    e_tile = (np.arange(64)[:, None] == np.arange(1024)[None, :] % 64).astype(np.float32)
    gmask = (np.arange(16)[:, None] == np.arange(1024)[None, :] // 64).astype(np.float32)
    bdm = (np.arange(256)[:, None] // 16 == np.arange(1024)[None, :] // 64).astype(np.float32)
    return jnp.asarray(e_rep), jnp.asarray(e_tile), jnp.asarray(gmask), jnp.asarray(bdm)


def f_s5_params(lam_re, lam_im, log_dt, bt_re, bt_im, ct_re, ct_im, e_rep, e_tile, gmask, bdm):
    dt = jnp.exp(log_dt)
    mag = jnp.exp(lam_re * dt)
    ang = lam_im * dt
    lb_re, lb_im = mag * jnp.cos(ang), mag * jnp.sin(ang)
    num_re, num_im = lb_re - 1.0, lb_im
    den = lam_re * lam_re + lam_im * lam_im
    coef_re = (num_re * lam_re + num_im * lam_im) / den
    coef_im = (num_im * lam_re - num_re * lam_im) / den
    cr, ci = hdot(e_rep, coef_re), hdot(e_rep, coef_im)
    bbt_re = cr * bt_re - ci * bt_im
    bbt_im = cr * bt_im + ci * bt_re
    a_re = jnp.sum(hdot(lb_re, e_tile) * gmask, axis=0, keepdims=True)
    a_im = jnp.sum(hdot(lb_im, e_tile) * gmask, axis=0, keepdims=True)
    return (a_re, a_im, hdot(bbt_re, e_tile) * bdm, hdot(bbt_im, e_tile) * bdm,
            hdot(ct_re, e_tile) * bdm, hdot(ct_im, e_tile) * bdm)


def _s5_doubling(xr, xi, pr, pi, pos, n, steps, rev):
    rows = xr.shape[0]
    for s in steps:
        if rev:
            keep = pos < (n - s)
            sr, si = pltpu.roll(xr, rows - s, 0), pltpu.roll(xi, rows - s, 0)
        else:
            keep = pos >= s
            sr, si = pltpu.roll(xr, s, 0), pltpu.roll(xi, s, 0)
        sr, si = jnp.where(keep, sr, 0.0), jnp.where(keep, si, 0.0)
        xr, xi = xr + pr * sr - pi * si, xi + pr * si + pi * sr
        pr, pi = pr * pr - pi * pi, 2.0 * pr * pi
    return xr, xi, pr, pi


SUBLANES = 8


def _s5_scan(xr, xi, a_re, a_im, rev, chunk, scr):
    xs_r, xs_i, yp_r, yp_i = scr
    ng = chunk // SUBLANES
    x3r, x3i = xr.reshape(ng, SUBLANES, 1024), xi.reshape(ng, SUBLANES, 1024)
    sub = lax.broadcasted_iota(jnp.int32, (SUBLANES, 1024), 0)
    a8r, a8i = a_re, a_im
    for s in (1, 2, 4):
        keep = sub < (SUBLANES - s) if rev else sub >= s
        mr, mi = jnp.where(keep, a8r, 0.0)[None], jnp.where(keep, a8i, 0.0)[None]
        shift = SUBLANES - s if rev else s
        sr, si = pltpu.roll(x3r, shift, 1), pltpu.roll(x3i, shift, 1)
        x3r, x3i = x3r + mr * sr - mi * si, x3i + mr * si + mi * sr
        a8r, a8i = a8r * a8r - a8i * a8i, 2.0 * a8r * a8i
    xr, xi = x3r.reshape(chunk, 1024), x3i.reshape(chunk, 1024)
    nblk = 1024 // 128
    for j in range(nblk):
        xs_r[j] = xr[:, 128 * j:128 * (j + 1)]
        xs_i[j] = xi[:, 128 * j:128 * (j + 1)]
    edge = pl.ds(0 if rev else SUBLANES - 1, ng, stride=SUBLANES)
    gr = jnp.concatenate([xs_r[j, edge, :] for j in range(nblk)], axis=-1)
    gi = jnp.concatenate([xs_i[j, edge, :] for j in range(nblk)], axis=-1)
    grow = lax.broadcasted_iota(jnp.int32, (ng, 1024), 0)
    steps = tuple(1 << k for k in range((ng - 1).bit_length()))
    gr, gi, _, _ = _s5_doubling(gr, gi, a8r, a8i, grow, ng, steps, rev)
    if rev:
        yp_r[...] = jnp.where(grow < ng - 1, pltpu.roll(gr, ng - 1, 0), 0.0)
        yp_i[...] = jnp.where(grow < ng - 1, pltpu.roll(gi, ng - 1, 0), 0.0)
    else:
        yp_r[...] = jnp.where(grow >= 1, pltpu.roll(gr, 1, 0), 0.0)
        yp_i[...] = jnp.where(grow >= 1, pltpu.roll(gi, 1, 0), 0.0)
    sub = lax.broadcasted_iota(jnp.int32, (SUBLANES, 1024), 0)
    tr, ti = jnp.zeros((SUBLANES, 1024), F32), jnp.zeros((SUBLANES, 1024), F32)
    cr, ci = a_re, a_im
    for n in range(1, SUBLANES + 1):
        r = SUBLANES - n if rev else n - 1
        tr, ti = jnp.where(sub == r, cr, tr), jnp.where(sub == r, ci, ti)
        cr, ci = cr * a_re - ci * a_im, cr * a_im + ci * a_re
    for j in range(nblk):
        lanes = slice(128 * j, 128 * (j + 1))
        tr_j, ti_j = tr[:, lanes], ti[:, lanes]
        for g in range(ng):
            rows = slice(g * SUBLANES, (g + 1) * SUBLANES)
            er, ei = yp_r[g:g + 1, lanes], yp_i[g:g + 1, lanes]
            xs_r[j, rows, :] = xs_r[j, rows, :] + tr_j * er - ti_j * ei
            xs_i[j, rows, :] = xs_i[j, rows, :] + tr_j * ei + ti_j * er
    return (jnp.concatenate([xs_r[j] for j in range(nblk)], axis=-1),
            jnp.concatenate([xs_i[j] for j in range(nblk)], axis=-1))


def _s5_scratch(chunk):
    return [pltpu.VMEM((8, chunk, 128), F32), pltpu.VMEM((8, chunk, 128), F32),
            pltpu.VMEM((chunk // SUBLANES, 1024), F32), pltpu.VMEM((chunk // SUBLANES, 1024), F32)]


def _s5_chunk_states(u_c, x0r, x0i, a_re, a_im, bb_re, bb_im, rev, chunk, scr):
    row = lax.broadcasted_iota(jnp.int32, (chunk, 1024), 0)
    first = row == (chunk - 1 if rev else 0)
    inj_r = a_re * x0r - a_im * x0i
    inj_i = a_re * x0i + a_im * x0r
    xr = b_nn(u_c, bb_re) + jnp.where(first, inj_r, 0.0)
    xi = b_nn(u_c, bb_im) + jnp.where(first, inj_i, 0.0)
    return _s5_scan(xr, xi, a_re, a_im, rev, chunk, scr)


def _row_pick(x, idx):
    row = lax.broadcasted_iota(jnp.int32, x.shape, 0)
    return jnp.sum(jnp.where(row == idx, x, 0.0), axis=0, keepdims=True)


def s5_scan_fwd(u, acc, a_re, a_im, bb_re, bb_im, cc_re, cc_im, n_ctx_rows, chunk, rev, name, comm=None):
    t = u.shape[0]
    nch, ncc = t // chunk, n_ctx_rows // chunk

    def body(u_ref, acc_ref, ar_ref, ai_ref, br_ref, bi_ref, cr_ref, ci_ref, y_ref, x0r_ref, x0i_ref, xsr_ref, xsi_ref, *scr):
        a_r, a_i = ar_ref[...], ai_ref[...]

        def step(s, carry):
            x0r, x0i = carry
            c = _gla_chunk_of(s, ncc, nch, rev)
            rows = pl.ds(pl.multiple_of(c * chunk, chunk), chunk)
            x0r_ref[c] = x0r
            x0i_ref[c] = x0i
            xr, xi = _s5_chunk_states(u_ref[rows, :], x0r, x0i, a_r, a_i, br_ref[...], bi_ref[...], rev, chunk, scr)
            y_ref[rows, :] = acc_ref[rows, :] + b_nt(xr, cr_ref[...]) - b_nt(xi, ci_ref[...])
            xsr_ref[rows, :] = xr.astype(BF16)
            xsi_ref[rows, :] = xi.astype(BF16)
            last = 0 if rev else chunk - 1
            return _row_pick(xr, last), _row_pick(xi, last)

        lax.fori_loop(0, nch, step, (jnp.zeros((1, 1024), F32), jnp.zeros((1, 1024), F32)))

    return _call_with_exchange(
        body, name, [u, acc, a_re, a_im, bb_re, bb_im, cc_re, cc_im],
        [jax.ShapeDtypeStruct((t, 256), F32), jax.ShapeDtypeStruct((nch, 1, 1024), F32),
         jax.ShapeDtypeStruct((nch, 1, 1024), F32), jax.ShapeDtypeStruct((t, 1024), BF16),
         jax.ShapeDtypeStruct((t, 1024), BF16)], comm, _s5_scratch(chunk))


def s5_scan_bwd(u, dy, du_acc, x0r, x0i, xsr, xsi, a_re, a_im, bb_re, bb_im, cc_re, cc_im, n_ctx_rows, chunk, rev, name):
    t = u.shape[0]
    nch, ncc = t // chunk, n_ctx_rows // chunk

    def body(u_ref, dy_ref, dua_ref, x0r_ref, x0i_ref, xsr_ref, xsi_ref, ar_ref, ai_ref, br_ref, bi_ref, cr_ref, ci_ref,
             du_ref, dar_ref, dai_ref, dbr_ref, dbi_ref, dcr_ref, dci_ref, *scr):
        a_r, a_i = ar_ref[...], ai_ref[...]
        for ref in (dbr_ref, dbi_ref, dcr_ref, dci_ref):
            ref[...] = jnp.zeros_like(ref)
        row = lax.broadcasted_iota(jnp.int32, (chunk, 1024), 0)
        first_idx, last_idx = (chunk - 1, 0) if rev else (0, chunk - 1)

        def step(j, carry):
            lcr, lci, dar, dai = carry
            s = nch - 1 - j
            c = _gla_chunk_of(s, ncc, nch, rev)
            rows = pl.ds(pl.multiple_of(c * chunk, chunk), chunk)
            u_c, dy_c = u_ref[rows, :], dy_ref[rows, :]
            x0r_c, x0i_c = x0r_ref[c], x0i_ref[c]
            xr, xi = xsr_ref[rows, :].astype(F32), xsi_ref[rows, :].astype(F32)
            dcr_ref[...] += b_tn(dy_c, xr)
            dci_ref[...] -= b_tn(dy_c, xi)
            inj_r = a_r * lcr + a_i * lci
            inj_i = a_r * lci - a_i * lcr
            is_last = row == last_idx
            lr = b_nn(dy_c, cr_ref[...]) + jnp.where(is_last, inj_r, 0.0)
            li = -b_nn(dy_c, ci_ref[...]) + jnp.where(is_last, inj_i, 0.0)
            lr, li = _s5_scan(lr, li, a_r, -a_i, not rev, chunk, scr)
            du_ref[rows, :] = dua_ref[rows, :] + b_nt(lr, br_ref[...]) + b_nt(li, bi_ref[...])
            dbr_ref[...] += b_tn(u_c, lr)
            dbi_ref[...] += b_tn(u_c, li)
            if rev:
                pr, pi = pltpu.roll(xr, chunk - 1, 0), pltpu.roll(xi, chunk - 1, 0)
            else:
                pr, pi = pltpu.roll(xr, 1, 0), pltpu.roll(xi, 1, 0)
            is_first = row == first_idx
            pr, pi = jnp.where(is_first, x0r_c, pr), jnp.where(is_first, x0i_c, pi)
            dar = dar + jnp.sum(lr * pr + li * pi, axis=0, keepdims=True)
            dai = dai + jnp.sum(li * pr - lr * pi, axis=0, keepdims=True)
            return _row_pick(lr, first_idx), _row_pick(li, first_idx), dar, dai

        z = jnp.zeros((1, 1024), F32)
        _, _, dar, dai = lax.fori_loop(0, nch, step, (z, z, z, z))
        dar_ref[...] = dar
        dai_ref[...] = dai

    vm = pl.BlockSpec(memory_space=pltpu.VMEM)
    big = jax.ShapeDtypeStruct((256, 1024), F32)
    vec = jax.ShapeDtypeStruct((1, 1024), F32)
    return pl.pallas_call(
        body, name=name, in_specs=[vm] * 13, out_specs=[vm] * 7,
        out_shape=[jax.ShapeDtypeStruct((t, 256), F32), vec, vec, big, big, big, big],
        scratch_shapes=_s5_scratch(chunk), compiler_params=_cparams(),
    )(u, dy, du_acc, x0r, x0i, xsr, xsi, a_re, a_im, bb_re, bb_im, cc_re, cc_im)


POOL_HALO = 8


def pool_apply(u_pad, n, transpose, name, tile=ROW_TILE):
    tile = min(tile, n)
    ext = tile + 2 * POOL_HALO
    trel = np.arange(ext)[None, :] - POOL_HALO - np.arange(tile)[:, None]
    if transpose:
        trel = -trel
    band4 = np.concatenate([((trel >= -(1 << w)) & (trel <= (1 << w) - 1)) for w in range(4)], axis=0).astype(np.float32)

    def body(u_ref, band_ref, lm_ref, o_ref):
        lax.fori_loop(0, n // tile, functools.partial(step, u_ref, band_ref, lm_ref, o_ref), 0)

    def step(u_ref, band_ref, lm_ref, o_ref, i, carry):
        val = u_ref[pl.ds(pl.multiple_of(i * tile, tile), ext), :]
        lane = lax.broadcasted_iota(jnp.int32, (ext, 256), 1)
        half = jnp.left_shift(1, jnp.right_shift(lane, 6))
        trow = lax.broadcasted_iota(jnp.int32, (ext, 256), 0) + (i * tile - POOL_HALO)
        cnt = jnp.minimum(trow + half, n) - jnp.maximum(trow - half, 0)
        inv = 1.0 / jnp.maximum(cnt, 1).astype(F32)
        src = val * inv if transpose else val
        acc = _fold_heads(mdot(band_ref[...], src), lm_ref)
        centre = val[POOL_HALO:POOL_HALO + tile]
        if not transpose:
            acc = acc * inv[POOL_HALO:POOL_HALO + tile]
        o_ref[pl.ds(pl.multiple_of(i * tile, tile), tile), :] = acc - centre
        return carry

    vm = pl.BlockSpec(memory_space=pltpu.VMEM)
    return pl.pallas_call(
        body, name=name, in_specs=[vm] * 3, out_specs=vm,
        out_shape=jax.ShapeDtypeStruct((n, 256), F32), compiler_params=_cparams(),
    )(u_pad, jnp.asarray(band4), _na_head_masks())


NA_SCALE = 64.0 ** -0.5
NEG = -1e30


def _call_with_exchange(compute, name, args, out_shapes, comm, scratch=()):
    vm = pl.BlockSpec(memory_space=pltpu.VMEM)
    n_in, n_out = len(args), len(out_shapes)
    if comm is None:
        outs = pl.pallas_call(compute, name=name, in_specs=[vm] * n_in, out_specs=[vm] * n_out, out_shape=out_shapes,
                              scratch_shapes=list(scratch), compiler_params=_cparams())(*args)
        return outs, None
    arrays, scatter = comm
    n = len(arrays)

    def body(*refs):
        c_in = refs[n_in:n_in + n]
        c_out = refs[n_in + n + n_out:n_in + 2 * n + n_out]
        scr = refs[n_in + 2 * n + n_out:n_in + 2 * n + n_out + len(scratch)]
        finish = _exchange_issue(c_in, c_out, scatter, *refs[n_in + 2 * n + n_out + len(scratch):])
        compute(*refs[:n_in], *refs[n_in + n:n_in + n + n_out], *scr)
        finish()

    hbm = pl.BlockSpec(memory_space=pl.ANY)
    outs = pl.pallas_call(
        body, name=name, in_specs=[vm] * n_in + [hbm] * n, out_specs=[vm] * n_out + [hbm] * n,
        out_shape=list(out_shapes) + _exchange_out_shapes(arrays, scatter), scratch_shapes=list(scratch) + _exchange_sems(n),
        compiler_params=_cparams(has_side_effects=True),
    )(*args, *arrays)
    return outs[:n_out], outs[n_out:]


def _na_head_masks():
    return jnp.asarray(np.stack([(np.arange(256) // 64 == h) for h in range(4)]).astype(np.float32).reshape(4, 1, 256))


def _na_window(r, rows):
    start = jnp.clip(r - 4, 0, rows - 8)
    return start, start - r + 7


def _na_probs(qh, kw, kc, bias):
    s_c = b_nt(qh, kc)
    m = jnp.max(s_c, axis=-1, keepdims=True)
    if kw is not None:
        s_w = b_nt(qh, kw) + bias
        m = jnp.maximum(m, jnp.max(s_w, axis=-1, keepdims=True))
        p_w = jnp.exp(s_w - m)
    p_c = jnp.exp(s_c - m)
    l = jnp.sum(p_c, axis=-1, keepdims=True)
    if kw is not None:
        l = l + jnp.sum(p_w, axis=-1, keepdims=True)
        return p_w / l, p_c / l
    return None, p_c / l


def na_fwd(q, k, v, bias8, n_ctx_rows, name, comm=None):
    t = q.shape[0]
    m_ctx = n_ctx_rows
    rows = (t - m_ctx) // GRID_W
    hm = _na_head_masks()

    def body(q_ref, k_ref, v_ref, b_ref, hm_ref, o_ref):
        kc, vc = k_ref[0:m_ctx, :], v_ref[0:m_ctx, :]

        def ctx_step(i, _):
            rs = pl.ds(pl.multiple_of(i * 64, 64), 64)
            q4 = _stack_heads(q_ref[rs, :] * NA_SCALE, hm_ref)
            _, p_c = _na_probs(q4, None, kc, None)
            o_ref[rs, :] = _fold_heads(b_nn(p_c, vc), hm_ref)
            return 0

        lax.fori_loop(0, m_ctx // 64, ctx_step, 0)

        def lat_step(r, _):
            start, off = _na_window(r, rows)
            rs = pl.ds(pl.multiple_of(m_ctx + r * 64, 64), 64)
            ws = pl.ds(pl.multiple_of(m_ctx + start * 64, 64), 512)
            q4 = _stack_heads(q_ref[rs, :] * NA_SCALE, hm_ref)
            kw, vw = k_ref[ws, :], v_ref[ws, :]
            p_w, p_c = _na_probs(q4, kw, kc, b_ref[off])
            o_ref[rs, :] = _fold_heads(b_nn(p_w, vw) + b_nn(p_c, vc), hm_ref)
            return 0

        lax.fori_loop(0, rows, lat_step, 0)

    (o,), received = _call_with_exchange(body, name, [q, k, v, bias8, hm], [jax.ShapeDtypeStruct((t, 256), F32)], comm)
    return o if comm is None else (o, received)


def na_bwd(q, k, v, do, bias8, n_ctx_rows, name, comm=None):
    t = q.shape[0]
    m_ctx = n_ctx_rows
    rows = (t - m_ctx) // GRID_W
    hm = _na_head_masks()

    def body(q_ref, k_ref, v_ref, do_ref, b_ref, hm_ref, dq_ref, dk_ref, dv_ref, db_ref):
        kc, vc = k_ref[0:m_ctx, :], v_ref[0:m_ctx, :]
        dk_ref[...] = jnp.zeros_like(dk_ref)
        dv_ref[...] = jnp.zeros_like(dv_ref)
        db_ref[...] = jnp.zeros_like(db_ref)

        def head_terms(qh, doh, kw, vw, bias):
            p_w, p_c = _na_probs(qh, kw, kc, bias)
            dp_c = b_nt(doh, vc)
            delta = jnp.sum(p_c * dp_c, axis=-1, keepdims=True)
            if kw is not None:
                dp_w = b_nt(doh, vw)
                delta = delta + jnp.sum(p_w * dp_w, axis=-1, keepdims=True)
                ds_w = p_w * (dp_w - delta)
            else:
                ds_w = None
            ds_c = p_c * (dp_c - delta)
            return p_w, p_c, ds_w, ds_c

        def ctx_step(i, carry):
            dkc, dvc = carry
            rs = pl.ds(pl.multiple_of(i * 64, 64), 64)
            q4, do4 = _stack_heads(q_ref[rs, :] * NA_SCALE, hm_ref), _stack_heads(do_ref[rs, :], hm_ref)
            _, p_c, _, ds_c = head_terms(q4, do4, None, None, None)
            dq_ref[rs, :] = _fold_heads(b_nn(ds_c, kc), hm_ref) * NA_SCALE
            return dkc + b_tn(ds_c, q4), dvc + b_tn(p_c, do4)

        zc = jnp.zeros((m_ctx, 256), F32)
        carry = lax.fori_loop(0, m_ctx // 64, ctx_step, (zc, zc))

        def lat_step(r, carry):
            dkc, dvc = carry
            start, off = _na_window(r, rows)
            rs = pl.ds(pl.multiple_of(m_ctx + r * 64, 64), 64)
            ws = pl.ds(pl.multiple_of(m_ctx + start * 64, 64), 512)
            q4, do4 = _stack_heads(q_ref[rs, :] * NA_SCALE, hm_ref), _stack_heads(do_ref[rs, :], hm_ref)
            kw, vw = k_ref[ws, :], v_ref[ws, :]
            p_w, p_c, ds_w, ds_c = head_terms(q4, do4, kw, vw, b_ref[off])
            dq_ref[rs, :] = _fold_heads(b_nn(ds_w, kw) + b_nn(ds_c, kc), hm_ref) * NA_SCALE
            dk_ref[ws, :] += b_tn(ds_w, q4)
            dv_ref[ws, :] += b_tn(p_w, do4)
            db_ref[off] += ds_w
            return dkc + b_tn(ds_c, q4), dvc + b_tn(p_c, do4)

        dkc, dvc = lax.fori_loop(0, rows, lat_step, carry)
        dk_ref[0:m_ctx, :] = dkc
        dv_ref[0:m_ctx, :] = dvc

    row = jax.ShapeDtypeStruct((t, 256), F32)
    return _call_with_exchange(body, name, [q, k, v, do, bias8, hm], [row, row, row, jax.ShapeDtypeStruct(bias8.shape, F32)], comm)


def _na_toeplitz():
    col = np.arange(GRID_W)
    dd = (col[None, :] - col[:, None] + 15).reshape(-1)
    tt = np.zeros((GRID_W * GRID_W, 128), np.float32)
    ok = (dd >= 0) & (dd <= 30)
    tt[np.arange(GRID_W * GRID_W)[ok], dd[ok]] = 1.0
    return tt


def _na_bias8(rpb, name):
    col = np.arange(GRID_W)
    cs = np.clip(col - 8, 0, GRID_W - 16)
    col_mask = (col[None, :] >= cs[:, None]) & (col[None, :] < cs[:, None] + 16)
    rpb2 = jnp.pad(rpb.reshape(60, 31), ((0, 4), (0, 97)))
    (toe,) = whole_fwd(lambda r_, t_: (hdot_nt(r_, t_),), name, [rpb2, jnp.asarray(_na_toeplitz())], [(64, GRID_W * GRID_W)])
    toe = toe[:60].reshape(4, 15, GRID_W, GRID_W)
    b = jnp.stack([toe[:, off:off + 8] for off in range(8)], axis=1)
    b = jnp.where(jnp.asarray(col_mask)[None, None, None], b, NEG)
    return b.transpose(1, 0, 3, 2, 4).reshape(8, 4 * GRID_W, 8 * GRID_W)


def _na_rpb_grad(dbias8, name):
    tt = _na_toeplitz()
    sel = np.zeros((64, 256), np.float32)
    for h in range(4):
        for off in range(8):
            for i in range(8):
                sel[h * 15 + off + i, h * 64 + off * 8 + i] = 1.0
    a2 = dbias8.reshape(8, 4, GRID_W, 8, GRID_W).transpose(1, 0, 3, 2, 4).reshape(256, GRID_W * GRID_W)
    (out,) = whole_fwd(lambda a, t_, s_: (hdot(s_, hdot(a, t_)),), name, [a2, jnp.asarray(tt), jnp.asarray(sel)], [(64, 128)])
    return out[:60, :31].reshape(4, 15, 31)


def f_mod(cs, b_mod, w_mod):
    s = _silu(cs)
    return bdot(s, w_mod) + b_mod, s


def loss_and_grad(z, tgt, n_ctx_rows, name, tile=ROW_TILE):
    t, d = z.shape
    tile = min(tile, n_ctx_rows)
    nct = n_ctx_rows // tile

    def body(z_ref, t_ref, dz_ref, loss_ref):
        i = pl.program_id(0)

        @pl.when(i == 0)
        def _():
            loss_ref[...] = jnp.zeros_like(loss_ref)

        @pl.when(i < nct)
        def _():
            dz_ref[...] = jnp.zeros_like(dz_ref)

        @pl.when(i >= nct)
        def _():
            diff = z_ref[...] - t_ref[...]
            dz_ref[...] = diff * (1.0 / d)
            loss_ref[...] += 0.5 * jnp.sum(jnp.sum(diff * diff, axis=-1, keepdims=True) * (1.0 / d), axis=0, keepdims=True)

    dz, loss = pl.pallas_call(
        body, name=name, grid=(t // tile,),
        in_specs=[pl.BlockSpec((tile, d), lambda i: (i, 0)),
                  pl.BlockSpec((tile, d), lambda i: (jnp.maximum(i - nct, 0), 0))],
        out_specs=[pl.BlockSpec((tile, d), lambda i: (i, 0)), pl.BlockSpec((8, 128), lambda i: (0, 0))],
        out_shape=[jax.ShapeDtypeStruct((t, d), F32), jax.ShapeDtypeStruct((8, 128), F32)],
        compiler_params=_cparams(dimension_semantics=("arbitrary",)),
    )(z, tgt)
    return loss[0, 0], dz


def adamw(parts, w, m, v, name, tile=256):
    npart, r, c = parts.shape
    tile = min(tile, r)
    assert r % tile == 0
    c1 = 1.0 / (1.0 - ADAM_B1 ** ADAM_STEP)
    c2 = 1.0 / (1.0 - ADAM_B2 ** ADAM_STEP)

    def body(p_ref, w_ref, m_ref, v_ref, g_ref, d_ref, nm_ref, nv_ref):
        g = p_ref[0].astype(F32)
        for i in range(1, npart):
            g = g + p_ref[i].astype(F32)
        nm = ADAM_B1 * m_ref[...] + (1.0 - ADAM_B1) * g
        nv = ADAM_B2 * v_ref[...] + (1.0 - ADAM_B2) * (g * g)
        g_ref[...] = g
        nm_ref[...] = nm
        nv_ref[...] = nv
        d_ref[...] = -ADAM_LR * ((nm * c1) / (jnp.sqrt(nv * c2) + ADAM_EPS) + ADAM_WD * w_ref[...])

    blk = pl.BlockSpec((tile, c), lambda i: (i, 0))
    return pl.pallas_call(
        body, name=name, grid=(r // tile,),
        in_specs=[pl.BlockSpec((npart, tile, c), lambda i: (0, i, 0)), blk, blk, blk],
        out_specs=[blk] * 4, out_shape=[jax.ShapeDtypeStruct((r, c), F32)] * 4,
        compiler_params=_cparams(dimension_semantics=("arbitrary",)),
    )(parts, w, m, v)


def _peer(x, y, c, k):
    return (1 - x if k & 4 else x, 1 - y if k & 2 else y, 1 - c if k & 1 else c)


def _exchange_out_shapes(arrays, scatter):
    return [jax.ShapeDtypeStruct(a.shape if s else (N_DEV,) + a.shape, a.dtype) for a, s in zip(arrays, scatter)]


def _exchange_sems(n):
    return [pltpu.SemaphoreType.DMA((n, N_DEV - 1)), pltpu.SemaphoreType.DMA((n, N_DEV - 1)), pltpu.SemaphoreType.DMA((n,))]


def _exchange_issue(ins, outs, scatter, send_sems, recv_sems, local_sems):
    n = len(ins)
    x, y, c = lax.axis_index("x"), lax.axis_index("y"), lax.axis_index("c")
    me = 4 * x + 2 * y + c

    def index_of(p):
        return 4 * p[0] + 2 * p[1] + p[2]

    local = []
    for a in range(n):
        src_me = ins[a].at[me] if scatter[a] else ins[a]
        loc = pltpu.make_async_copy(src_me, outs[a].at[me], local_sems.at[a])
        loc.start()
        local.append(loc)
    for k in range(1, N_DEV):
        peer = _peer(x, y, c, k)
        for a in range(n):
            src = ins[a].at[index_of(peer)] if scatter[a] else ins[a]
            pltpu.make_async_remote_copy(
                src_ref=src, dst_ref=outs[a].at[me], send_sem=send_sems.at[a, k - 1], recv_sem=recv_sems.at[a, k - 1],
                device_id=peer, device_id_type=pl.DeviceIdType.MESH).start()

    def finish():
        for k in range(1, N_DEV):
            peer = _peer(x, y, c, k)
            for a in range(n):
                src = ins[a].at[index_of(peer)] if scatter[a] else ins[a]
                cp = pltpu.make_async_remote_copy(
                    src_ref=src, dst_ref=outs[a].at[index_of(peer)], send_sem=send_sems.at[a, k - 1],
                    recv_sem=recv_sems.at[a, k - 1], device_id=peer, device_id_type=pl.DeviceIdType.MESH)
                cp.wait_send()
                cp.wait_recv()
        for loc in local:
            loc.wait()

    return finish


def exchange(arrays, scatter, name):
    n = len(arrays)

    def body(*refs):
        _exchange_issue(refs[:n], refs[n:2 * n], scatter, *refs[2 * n:])()

    hbm = pl.BlockSpec(memory_space=pl.ANY)
    return pl.pallas_call(
        body, name=name, in_specs=[hbm] * n, out_specs=[hbm] * n, out_shape=_exchange_out_shapes(arrays, scatter),
        scratch_shapes=_exchange_sems(n), compiler_params=pltpu.CompilerParams(has_side_effects=True),
    )(*arrays)


def _rope_tables(n_lat, n_ctx):
    tok = np.arange(n_lat)
    freqs = 10000.0 ** (-np.arange(0, 16, 2, dtype=np.float32) / 16.0)

    def table(pos):
        ang = pos.astype(np.float32)[:, None] * freqs[None, :]
        ang = np.concatenate([ang, ang], axis=-1)
        return np.cos(ang), np.sin(ang)

    cr, sr = table(tok // GRID_W)
    cc, sc = table(tok % GRID_W)
    cos = np.tile(np.concatenate([cr, cc], axis=-1), (1, 4))
    sin = np.tile(np.concatenate([sr, sc], axis=-1), (1, 4))
    cos = np.concatenate([np.ones((n_ctx, 128), np.float32), cos], axis=0)
    sin = np.concatenate([np.zeros((n_ctx, 128), np.float32), sin], axis=0)
    return jnp.asarray(cos, F32), jnp.asarray(sin, F32)


def _pad_w_in(w):
    z = lambda n: jnp.zeros((w.shape[0], n), w.dtype)
    return jnp.concatenate([w[:, 1824:2848], w[:, 128:384], w[:, 416:672], w[:, 672:928], w[:, 928:1184], w[:, 1312:1568],
                            w[:, 1568:1824], w[:, 0:128], w[:, 384:416], z(96), w[:, 1184:1312], z(128)], axis=1)


def _unpad_w_in(wp):
    return jnp.concatenate([wp[:, C_GK:C_GK + 128], wp[:, C_GV:C_GV + 256], wp[:, C_GG:C_GG + 32], wp[:, C_NK:C_NK + 256],
                            wp[:, C_NV:C_NV + 256], wp[:, C_SU:C_SU + 256], wp[:, C_GQ:C_GQ + 128], wp[:, C_NQ:C_NQ + 256],
                            wp[:, C_PU:C_PU + 256], wp[:, C_GT:C_GT + 1024]], axis=1)


def _pad_rows(u):
    return jnp.pad(u, ((POOL_HALO, POOL_HALO), (0, 0)))


def _block_diag4(w):
    out = jnp.zeros((256, 256), w.dtype)
    for i in range(4):
        out = lax.dynamic_update_slice(out, w[i], (64 * i, 64 * i))
    return out


def _layer_params(p, big, l):
    e_rep, e_tile, gmask, bdm = _s5_consts()
    wg = jnp.zeros((128, 256), F32)
    wg = lax.dynamic_update_slice(wg, p["gla_w_gate"][l, 0], (0, 0))
    wg = lax.dynamic_update_slice(wg, p["gla_w_gate"][l, 1], (16, 128))
    s5 = []
    for d in range(2):
        s5.append([p["s5_lam_re"][l, d], p["s5_lam_im"][l, d], p["s5_log_dt"][l, d].reshape(16, 1),
                   p["s5_b_re"][l, d].transpose(0, 2, 1).reshape(256, 64), p["s5_b_im"][l, d].transpose(0, 2, 1).reshape(256, 64),
                   p["s5_c_re"][l, d].reshape(256, 64), p["s5_c_im"][l, d].reshape(256, 64), e_rep, e_tile, gmask, bdm])
    havg = jnp.asarray((np.arange(256)[:, None] // 64 == np.arange(256)[None, :] // 64).astype(np.float32) / 64.0)
    e4 = jnp.asarray((np.arange(64)[:, None] == np.arange(256)[None, :] % 64).astype(np.float32))
    return dict(
        g_pre=p["g_pre"][l].reshape(1, D), g_post=p["g_post"][l].reshape(1, D), b_mod=p["b_mod"][l].reshape(1, 3 * D),
        w_mod=big["w_mod"], w_in=_pad_w_in(big["w_in"]), w_out=big["w_out"],
        wg=wg, bg=p["gla_b_gate"][l].reshape(1, 256), g_norm=jnp.pad(p["gla_g_norm"][l].reshape(1, 64), ((0, 7), (0, 0))),
        bias8=_na_bias8(p["na_rpb"][l], f"na_bias_l{l}"), s5=s5, s5_d=p["s5_d"][l].reshape(1, 256), w_glu=big["s5_w_glu"].astype(F32),
        b_glu=p["s5_b_glu"][l].reshape(1, 256), wpool=_block_diag4(p["pool_w"][l]), pool_scale=p["pool_scale"][l].reshape(1, 256),
        havg=havg, e4=e4)


def _matmul_tile(t, tile, steps):
    return t // steps if t % (8 * steps) == 0 else tile


def _cols(pz, start, width):
    return pz[:, start:start + width]


def _layer_fwd(z, modseg, lp, cos, sin, m_ctx, tile, s5_chunk, l, comm=None):
    t = z.shape[0]
    nct = m_ctx // tile
    nm = lambda s: f"{s}_l{l}"
    (h,) = rowwise_fwd(f_pre, nm("pre"), [z], [modseg], [lp["g_pre"]], [D], tile, nct)
    mm_tile = _matmul_tile(t, tile, 4)
    (gt,) = mm_nn_cols(h, lp["w_in"], C_GT, [1024], nm("in_proj_a"), tm=mm_tile)
    pv, nk, nv, su = mm_nn_cols(h, lp["w_in"], C_GV, [256] * 4, nm("in_proj_b"), tm=mm_tile)
    nq, pu, pk, pg, pq = mm_nn_cols(h, lp["w_in"], C_NQ, [256, 256, 128, 128, 128], nm("in_proj_c"), tm=mm_tile)
    q_r, k_r, lgf, lgb = rowwise_fwd(f_gla_prep, nm("gla_prep"), [pk, pg, pq, cos, sin], [], [lp["wg"], lp["bg"]], [128] * 4, tile, nct)
    half = None if comm is None else comm[0].shape[0] // 2
    spread = None if comm is None else [comm[0][:half], comm[1], comm[2], comm[3], comm[0][half:]]
    part = (lambda idx: None) if comm is None else (lambda idx: ([spread[i] for i in idx], [False] * len(idx)))
    (o1, st_f), _ = gla_scan_fwd(q_r, k_r, pv, lgf, jnp.zeros((t, 256), F32), m_ctx, False, nm("gla_f"))
    (o_gla, st_b), got_out = gla_scan_fwd(q_r, k_r, pv, lgb, o1, m_ctx, True, nm("gla_r"), part([2, 3]))
    received = None
    if comm is None:
        o_na = na_fwd(nq, nk, nv, lp["bias8"], m_ctx, nm("na"))
    else:
        o_na, got_in = na_fwd(nq, nk, nv, lp["bias8"], m_ctx, nm("na"), part([1]))
    s5p = [whole_fwd(f_s5_params, nm(f"s5_par{d}"), lp["s5"][d], [(1, 1024)] * 2 + [(256, 1024)] * 4) for d in range(2)]
    (y1, *states_f), got_mod_a = s5_scan_fwd(su, jnp.zeros((t, 256), F32), *s5p[0], m_ctx, s5_chunk, False, nm("s5_f"), part([0]))
    (y5, *states_b), got_mod_b = s5_scan_fwd(su, y1, *s5p[1], m_ctx, s5_chunk, True, nm("s5_r"), part([4]))
    if comm is not None:
        received = [jnp.concatenate([got_mod_a[0], got_mod_b[0]], axis=1), got_in[0], got_out[0], got_out[1]]
    pm = jnp.concatenate([pool_apply(_pad_rows(pu[:m_ctx]), m_ctx, False, nm("pool_c")),
                          pool_apply(_pad_rows(pu[m_ctx:]), t - m_ctx, False, nm("pool_x"))], axis=0)
    mix_rows = [o_gla, o_na, y5, su, pm, gt]
    mix_globs = [lp["g_norm"], lp["s5_d"], lp["w_glu"], lp["b_glu"], lp["wpool"], lp["pool_scale"], lp["havg"], lp["e4"]]
    (yg,) = rowwise_fwd(f_mix, nm("mix"), mix_rows, [], mix_globs, [D], tile, nct)
    out = mm_nn([yg], lp["w_out"], nm("out_proj"), tm=mm_tile)
    (z_new,) = rowwise_fwd(f_post, nm("post"), [z, out], [modseg], [lp["g_post"]], [D], tile, nct)
    saved = dict(z=z, h=h, pv=pv, nk=nk, nv=nv, su=su, nq=nq, pk=pk, pg=pg, pq=pq, q_r=q_r, k_r=k_r, lgf=lgf, lgb=lgb,
                 st_f=st_f, st_b=st_b, s5p=s5p, x0f=tuple(states_f), x0b=tuple(states_b), mix_rows=mix_rows, mix_globs=mix_globs,
                 yg=yg, out=out)
    return z_new, saved, received


def _f_pre_res(x, mod, g_pre):
    return f_pre(x, mod, g_pre)[0], x


def _layer_bwd(dz_new, sv, modseg, lp, cos, sin, m_ctx, tile, s5_chunk, l, comm=None, gdt=F32):
    t = dz_new.shape[0]
    nct = m_ctx // tile
    nm = lambda s: f"{s}_l{l}"
    g = {}
    dz_res, dout, dmod_post, g["g_post"] = rowwise_bwd(f_post, nm("post_b"), [sv["z"], sv["out"]], [modseg], [lp["g_post"]],
                                                       [dz_new], tile, nct, [True, True], [True])
    dyg = mm_nt([dout], lp["w_out"], nm("out_proj_dx"), tm=_matmul_tile(t, tile, 4))
    dw_tile = _matmul_tile(t, tile, 4)
    (g["w_out"],) = mm_tn(sv["yg"], [dout], nm("out_proj_dw"), tm=dw_tile, out_dtype=gdt)
    res = rowwise_bwd(f_mix, nm("mix_b"), sv["mix_rows"], [], sv["mix_globs"], [dyg], tile, nct, [True] * 6, [True] * 6 + [False] * 2)
    do_gla, do_na, dy5, dsu_a, dpm, dgt = res[:6]
    g["g_norm"], g["s5_d"], g["w_glu"], g["b_glu"], g["wpool"], g["pool_scale"] = res[6:]
    dpu = jnp.concatenate([pool_apply(_pad_rows(dpm[:m_ctx]), m_ctx, True, nm("pool_c_b")),
                           pool_apply(_pad_rows(dpm[m_ctx:]), t - m_ctx, True, nm("pool_x_b"))], axis=0)
    r_b = s5_scan_bwd(sv["su"], dy5, dsu_a, *sv["x0b"], *sv["s5p"][1], m_ctx, s5_chunk, True, nm("s5_r_b"))
    r_f = s5_scan_bwd(sv["su"], dy5, r_b[0], *sv["x0f"], *sv["s5p"][0], m_ctx, s5_chunk, False, nm("s5_f_b"))
    dsu = r_f[0]
    g["s5"] = [whole_bwd(f_s5_params, nm(f"s5_par{d}_b"), lp["s5"][d], list(r[1:]), [True] * 7 + [False] * 4)
               for d, r in ((0, r_f), (1, r_b))]
    part = (lambda idx: None) if comm is None else (lambda idx: ([comm[i] for i in idx], [True] * len(idx)))
    (dnq, dnk, dnv, dbias8), got_in = na_bwd(sv["nq"], sv["nk"], sv["nv"], do_na, lp["bias8"], m_ctx, nm("na_b"), part([0]))
    g["rpb"] = _na_rpb_grad(dbias8, nm("na_rpb_b"))
    zq, zv = jnp.zeros((t, 128), F32), jnp.zeros((t, 256), F32)
    (dq1, dk1, dv1, dlgb), _ = gla_scan_bwd(sv["q_r"], sv["k_r"], sv["pv"], sv["lgb"], sv["st_b"], do_gla, (zq, zq, zv), m_ctx, True,
                                            nm("gla_r_b"))
    (dq_r, dk_r, dpv, dlgf), got_out = gla_scan_bwd(sv["q_r"], sv["k_r"], sv["pv"], sv["lgf"], sv["st_f"], do_gla, (dq1, dk1, dv1), m_ctx, False,
                                                    nm("gla_f_b"), part([1, 2]))
    received = None if comm is None else [got_in[0], got_out[0], got_out[1]]
    dpk, dpg, dpq, g["wg"], g["bg"] = rowwise_bwd(f_gla_prep, nm("gla_prep_b"), [sv["pk"], sv["pg"], sv["pq"], cos, sin], [],
                                                  [lp["wg"], lp["bg"]], [dq_r, dk_r, dlgf, dlgb], tile, nct,
                                                  [True, True, True, False, False], [True, True])
    parts = [dgt, dpv, dnk, dnv, dsu, dnq, dpu, dpk, dpg, dpq, jnp.zeros((t, 128), F32)]
    dh = mm_nt(parts, lp["w_in"], nm("in_proj_dx"), tm=_matmul_tile(t, tile, 8))
    g["w_in"] = _unpad_w_in(jnp.concatenate(mm_tn(sv["h"], parts, nm("in_proj_dw"), tm=dw_tile, out_dtype=gdt), axis=1))
    dz, dmod_pre, g["g_pre"] = rowwise_bwd(_f_pre_res, nm("pre_b"), [sv["z"]], [modseg], [lp["g_pre"]], [dh, dz_res], tile, nct, [True], [True])
    return dz, dmod_pre, dmod_post, g, received


def _f_mod_sum(cs, b_mod, w_mod):
    mod, _ = f_mod(cs, b_mod, w_mod)
    return mod, cs


def local_step(x, c, ctx, tgt, p, shards=None, tile=ROW_TILE, s5_chunk=S5_CHUNK):
    n_lat, m_ctx = x.shape[0], ctx.shape[0]
    n_layers = p["g_pre"].shape[0]
    z = jnp.concatenate([ctx, x], axis=0)
    cos, sin = _rope_tables(n_lat, m_ctx)
    cs = jnp.concatenate([c.reshape(1, D), p["c_ctx"].reshape(1, D), jnp.zeros((6, D), F32)], axis=0)
    gather = [False] * len(_SHARDED)
    lps, mods, silus, saves = [], [], [], []
    got = exchange(shards[0], gather, "gather_weights_l0") if shards is not None else None
    for l in range(n_layers):
        if shards is None:
            big = {n: p[n][l] for n in _SHARDED}
        else:
            big = {n: _gathered(g, _BY_COLS[n]) for n, g in zip(_SHARDED, got)}
        lp = _layer_params(p, big, l)
        mod8, s8 = whole_fwd(f_mod, f"mod_l{l}", [cs, lp["b_mod"], lp["w_mod"]], [(8, 3 * D), (8, D)])
        modseg = mod8[:2].reshape(2, 1, 3 * D)
        comm = shards[l + 1] if shards is not None and l + 1 < n_layers else None
        z, sv, got = _layer_fwd(z, modseg, lp, cos, sin, m_ctx, tile, s5_chunk, l, comm)
        lps.append(lp); mods.append(modseg); silus.append(s8); saves.append(sv)
    loss, dz = loss_and_grad(z, tgt, m_ctx, "loss", tile)
    grads, received = [None] * n_layers, [None] * n_layers
    gdt = F32 if shards is None else BF16
    dcs = jnp.zeros((8, D), F32)
    pending = None
    for l in reversed(range(n_layers)):
        lp = lps[l]
        dz, dmod_pre, dmod_post, g, got = _layer_bwd(dz, saves[l], mods[l], lp, cos, sin, m_ctx, tile, s5_chunk, l, pending, gdt)
        if pending is not None:
            received[l + 1] = got
        dmod = jnp.concatenate([dmod_pre.reshape(2, 3 * D)[:, :2 * D], dmod_post.reshape(2, 3 * D)[:, 2 * D:]], axis=1)
        dmod8 = jnp.pad(dmod, ((0, 6), (0, 0)))
        dcs, g["b_mod"] = whole_bwd(_f_mod_sum, f"mod_b_l{l}", [cs, lp["b_mod"], lp["w_mod"]], [dmod8, dcs], [True, True, False])
        if shards is None:
            g["w_mod"] = jnp.concatenate(mm_tn(silus[l], [dmod8[:, :D], dmod8[:, D:2 * D], dmod8[:, 2 * D:]], f"mod_dw_l{l}", tm=8), axis=1)
        else:
            g["mod_s"], g["mod_d"] = silus[l][:2], dmod
        grads[l] = g
        if shards is not None:
            pending = _layer_sends(g)
    if shards is not None:
        received[0] = exchange(pending + [_small_sends(dcs[1], grads)], [True] * len(_GRAD_SHARDED) + [False], "exchange_grads_l0")
    return loss, dz[m_ctx:], dcs[1], grads, received


_WEIGHTS = ["c_ctx", "w_mod", "b_mod", "g_pre", "g_post", "w_in", "w_out", "gla_w_gate", "gla_b_gate", "gla_g_norm", "na_rpb",
            "s5_lam_re", "s5_lam_im", "s5_log_dt", "s5_b_re", "s5_b_im", "s5_c_re", "s5_c_im", "s5_d", "s5_w_glu", "s5_b_glu",
            "pool_w", "pool_scale"]
_INPUTS = ["x", "c", "ctx"] + _WEIGHTS + ["loss_target"] + ["m_" + n for n in _WEIGHTS] + ["v_" + n for n in _WEIGHTS]
_SHARDED = ["w_mod", "w_in", "w_out", "s5_w_glu"]
_BY_COLS = {"w_mod": True, "w_in": True, "w_out": False, "s5_w_glu": False}
_GRAD_SHARDED = ["w_in", "w_out", "s5_w_glu"]
_SMALL = [n for n in _WEIGHTS if n not in _SHARDED]
_SMALL_PER_LAYER = [n for n in _SMALL if n != "c_ctx"]
_PACK_ROWS = 256


def _pack_plan(like):
    tiled = [i for i, a in enumerate(like) if a.size % 1024 == 0]
    loose = [i for i, a in enumerate(like) if a.size % 1024 != 0]
    tail = -(-sum(like[i].size for i in loose) // 1024) * 8
    rows = sum(like[i].size // 128 for i in tiled) + tail
    return tiled, loose, tail, -(-rows // _PACK_ROWS) * _PACK_ROWS - rows


def _pack_rows(like, index):
    tiled, _, _, _ = _pack_plan(like)
    row = 0
    for i in tiled:
        n = like[i].size // 128
        if i == index:
            return row, row + n
        row += n
    raise ValueError("not a tile-aligned entry")


def _pack(arrs):
    tiled, loose, tail, fill = _pack_plan(arrs)
    dt = arrs[0].dtype
    flat = jnp.concatenate([arrs[i].reshape(-1) for i in loose])
    flat = jnp.pad(flat, (0, tail * 128 - flat.shape[0])).reshape(tail, 128)
    return jnp.concatenate([arrs[i].reshape(-1, 128) for i in tiled] + [flat, jnp.zeros((fill, 128), dt)], axis=0)


def _unpack(packed, like):
    tiled, loose, tail, _ = _pack_plan(like)
    out, row = [None] * len(like), 0
    for i in tiled:
        n = like[i].size // 128
        out[i] = packed[row:row + n].reshape(like[i].shape)
        row += n
    flat, pos = packed[row:row + tail].reshape(-1), 0
    for i in loose:
        out[i] = flat[pos:pos + like[i].size].reshape(like[i].shape)
        pos += like[i].size
    return out


def _gathered(g, cols):
    if cols:
        return g.transpose(1, 0, 2).reshape(g.shape[1], N_DEV * g.shape[2])
    return g.reshape(N_DEV * g.shape[1], g.shape[2])


def _slabs(w, cols):
    r, c = w.shape
    if cols:
        return w.reshape(r, N_DEV, c // N_DEV).transpose(1, 0, 2)
    return w.reshape(N_DEV, r // N_DEV, c)


def _layer_small(g):
    s5 = lambda i, f: jnp.stack([f(g["s5"][d][i]) for d in range(2)])
    return {
        "b_mod": g["b_mod"].reshape(3 * D), "g_pre": g["g_pre"].reshape(D), "g_post": g["g_post"].reshape(D),
        "gla_w_gate": jnp.stack([g["wg"][0:16, 0:128], g["wg"][16:32, 128:256]]),
        "gla_b_gate": g["bg"].reshape(2, 128), "gla_g_norm": g["g_norm"][0], "na_rpb": g["rpb"],
        "s5_lam_re": s5(0, lambda a: a), "s5_lam_im": s5(1, lambda a: a), "s5_log_dt": s5(2, lambda a: a.reshape(16)),
        "s5_b_re": s5(3, lambda a: a.reshape(16, 16, 64).transpose(0, 2, 1)),
        "s5_b_im": s5(4, lambda a: a.reshape(16, 16, 64).transpose(0, 2, 1)),
        "s5_c_re": s5(5, lambda a: a.reshape(16, 16, 64)), "s5_c_im": s5(6, lambda a: a.reshape(16, 16, 64)),
        "s5_d": g["s5_d"].reshape(256), "s5_b_glu": g["b_glu"].reshape(256),
        "pool_w": jnp.stack([g["wpool"][64 * i:64 * i + 64, 64 * i:64 * i + 64] for i in range(4)]),
        "pool_scale": g["pool_scale"].reshape(256),
    }


def _layer_sends(g):
    big = {"w_in": g["w_in"], "w_out": g["w_out"], "s5_w_glu": g["w_glu"]}
    return [_slabs(big[n], _BY_COLS[n]).astype(BF16) for n in _GRAD_SHARDED]


def _small_sends(d_c_ctx, grads):
    per_layer = [_layer_small(g) for g in grads]
    full = {n: jnp.stack([s[n] for s in per_layer]) for n in _SMALL_PER_LAYER}
    full["c_ctx"] = d_c_ctx
    factors = [jnp.stack([g["mod_s"] for g in grads]), jnp.stack([g["mod_d"] for g in grads])]
    return _pack([full[n] for n in _SMALL] + factors).astype(BF16)


def kernel(x, c, ctx, c_ctx, w_mod, b_mod, g_pre, g_post, w_in, w_out, gla_w_gate, gla_b_gate, gla_g_norm, na_rpb, s5_lam_re, s5_lam_im, s5_log_dt, s5_b_re, s5_b_im, s5_c_re, s5_c_im, s5_d, s5_w_glu, s5_b_glu, pool_w, pool_scale, loss_target, m_c_ctx, m_w_mod, m_b_mod, m_g_pre, m_g_post, m_w_in, m_w_out, m_gla_w_gate, m_gla_b_gate, m_gla_g_norm, m_na_rpb, m_s5_lam_re, m_s5_lam_im, m_s5_log_dt, m_s5_b_re, m_s5_b_im, m_s5_c_re, m_s5_c_im, m_s5_d, m_s5_w_glu, m_s5_b_glu, m_pool_w, m_pool_scale, v_c_ctx, v_w_mod, v_b_mod, v_g_pre, v_g_post, v_w_in, v_w_out, v_gla_w_gate, v_gla_b_gate, v_gla_g_norm, v_na_rpb, v_s5_lam_re, v_s5_lam_im, v_s5_log_dt, v_s5_b_re, v_s5_b_im, v_s5_c_re, v_s5_c_im, v_s5_d, v_s5_w_glu, v_s5_b_glu, v_pool_w, v_pool_scale):
    given = dict(zip(_INPUTS, (x, c, ctx, c_ctx, w_mod, b_mod, g_pre, g_post, w_in, w_out, gla_w_gate, gla_b_gate, gla_g_norm, na_rpb, s5_lam_re, s5_lam_im, s5_log_dt, s5_b_re, s5_b_im, s5_c_re, s5_c_im, s5_d, s5_w_glu, s5_b_glu, pool_w, pool_scale, loss_target, m_c_ctx, m_w_mod, m_b_mod, m_g_pre, m_g_post, m_w_in, m_w_out, m_gla_w_gate, m_gla_b_gate, m_gla_g_norm, m_na_rpb, m_s5_lam_re, m_s5_lam_im, m_s5_log_dt, m_s5_b_re, m_s5_b_im, m_s5_c_re, m_s5_c_im, m_s5_d, m_s5_w_glu, m_s5_b_glu, m_pool_w, m_pool_scale, v_c_ctx, v_w_mod, v_b_mod, v_g_pre, v_g_post, v_w_in, v_w_out, v_gla_w_gate, v_gla_b_gate, v_gla_g_norm, v_na_rpb, v_s5_lam_re, v_s5_lam_im, v_s5_log_dt, v_s5_b_re, v_s5_b_im, v_s5_c_re, v_s5_c_im, v_s5_d, v_s5_w_glu, v_s5_b_glu, v_pool_w, v_pool_scale)))
    n_layers = w_in.shape[0]
    shards = [[given[n][l].astype(BF16) for n in _SHARDED] for l in range(n_layers)]
    p = {n: given[n] for n in _SMALL}
    loss, grad_x, _, _, received = local_step(x[0], c, ctx[0], loss_target[0], p, shards)
    final = {}
    for n in _GRAD_SHARDED:
        per_layer = [adamw(received[l][_GRAD_SHARDED.index(n)], given[n][l], given["m_" + n][l], given["v_" + n][l], f"adamw_{n}_l{l}")
                     for l in range(n_layers)]
        final[n] = [jnp.stack([res[kind] for res in per_layer]) for kind in range(4)]
    factor_like = [jnp.zeros((n_layers, 2, D), F32), jnp.zeros((n_layers, 2, 3 * D), F32)]
    like = [given[n] for n in _SMALL] + factor_like
    small_recv = received[0][-1]
    rows_s, rows_d = _pack_rows(like, len(_SMALL)), _pack_rows(like, len(_SMALL) + 1)
    fac_s = small_recv[:, rows_s[0]:rows_s[1]].reshape(N_DEV, n_layers, 2, D)
    fac_d = small_recv[:, rows_d[0]:rows_d[1]].reshape(N_DEV, n_layers, 2, 3 * D)
    me = 4 * lax.axis_index("x") + 2 * lax.axis_index("y") + lax.axis_index("c")
    cols = w_mod.shape[2]
    per_layer = []
    for l in range(n_layers):
        s_all = fac_s[:, l].reshape(2 * N_DEV, D)
        d_mine = lax.dynamic_slice_in_dim(fac_d[:, l].reshape(2 * N_DEV, 3 * D), me * cols, cols, axis=1)
        (g_mod,) = mm_tn(s_all, [d_mine], f"mod_dw_l{l}", tm=2 * N_DEV, tn=cols)
        per_layer.append(adamw(g_mod[None], given["w_mod"][l], given["m_w_mod"][l], given["v_w_mod"][l], f"adamw_w_mod_l{l}"))
    final["w_mod"] = [jnp.stack([res[kind] for res in per_layer]) for kind in range(4)]
    res = adamw(small_recv, _pack(like), _pack([given["m_" + n] for n in _SMALL] + factor_like),
                _pack([given["v_" + n] for n in _SMALL] + factor_like), "adamw_small")
    unpacked = [_unpack(packed, like) for packed in res]
    for i, n in enumerate(_SMALL):
        final[n] = [unpacked[kind][i] for kind in range(4)]
    loss = lax.psum(loss, ("x", "y", "c"))
    return (loss, grad_x[None], *[final[n][0] for n in _WEIGHTS], *[final[n][1] for n in _WEIGHTS],
            *[final[n][2] for n in _WEIGHTS], *[final[n][3] for n in _WEIGHTS])
```

```python
import functools
import math

import numpy as np
import jax
import jax.numpy as jnp
from jax import lax
from jax.experimental import pallas as pl
from jax.experimental.pallas import tpu as pltpu

F32 = jnp.float32
BF16 = jnp.bfloat16
HIGHEST = lax.Precision.HIGHEST
HIGH = lax.Precision.HIGH

D = 1024
GRID_W = 64
EPS = 1e-6
N_DEV = 8
C_GT, C_GV, C_NK, C_NV, C_SU, C_NQ, C_PU, C_GK, C_GG, C_GQ, C_END = 0, 1024, 1280, 1536, 1792, 2048, 2304, 2560, 2688, 2816, 2944
PW = 3072
N_CTX_ORIG = 416
N_IN = 2848
GLA_CHUNK = 128
S5_CHUNK = 256
ROW_TILE = 256
VMEM_LIMIT = 56 * 1024 * 1024

ADAM_LR, ADAM_B1, ADAM_B2, ADAM_EPS, ADAM_WD, ADAM_STEP = 0.001, 0.9, 0.999, 1e-08, 0.01, 10


def _cparams(**kw):
    return pltpu.CompilerParams(vmem_limit_bytes=VMEM_LIMIT, **kw)


def _dg(a, b, ca, cb, precision=None):
    return lax.dot_general(a, b, (((ca,), (cb,)), ((), ())), precision=precision, preferred_element_type=F32)


def hdot(a, b):
    return _dg(a, b, 1, 0, HIGHEST)


def hdot_nt(a, b):
    return _dg(a, b, 1, 1, HIGHEST)


def hdot_tn(a, b):
    return _dg(a, b, 0, 0, HIGHEST)


def mdot(a, b):
    return _dg(a, b, 1, 0, HIGH)


def mdot_nt(a, b):
    return _dg(a, b, 1, 1, HIGH)


def mdot_tn(a, b):
    return _dg(a, b, 0, 0, HIGH)


def b_nn(a, b):
    return _dg(a.astype(BF16), b.astype(BF16), 1, 0)


def b_nt(a, b):
    return _dg(a.astype(BF16), b.astype(BF16), 1, 1)


def b_tn(a, b):
    return _dg(a.astype(BF16), b.astype(BF16), 0, 0)


@jax.custom_vjp
def bdot(a, b):
    return b_nn(a, b)


def _bdot_fwd(a, b):
    return b_nn(a, b), (a, b)


def _bdot_bwd(res, ct):
    a, b = res
    return b_nt(ct, b).astype(a.dtype), b_tn(a, ct).astype(b.dtype)


bdot.defvjp(_bdot_fwd, _bdot_bwd)


def _log_sigmoid(z):
    return jnp.minimum(z, 0.0) - jnp.log(1.0 + jnp.exp(-jnp.abs(z)))


def _silu(z):
    return z * jax.nn.sigmoid(z)


def _gelu(z):
    return 0.5 * z * (1.0 + jnp.tanh(math.sqrt(2.0 / math.pi) * (z + 0.044715 * (z * z * z))))


def _cat(vals):
    return vals[0] if len(vals) == 1 else jnp.concatenate(vals, axis=-1)


def mm_nn(a_parts, b, name, tm=ROW_TILE, tn=1024):
    t = a_parts[0].shape[0]
    k, n = b.shape
    na = len(a_parts)
    tn = min(tn, n)

    def body(*refs):
        a = _cat([r[...].astype(BF16) for r in refs[:na]])
        refs[na + 1][...] = _dg(a, refs[na][...].astype(BF16), 1, 0)

    return pl.pallas_call(
        body, name=name, grid=(n // tn, t // tm),
        in_specs=[pl.BlockSpec((tm, p.shape[1]), lambda j, i: (i, 0)) for p in a_parts]
        + [pl.BlockSpec((k, tn), lambda j, i: (0, j))],
        out_specs=pl.BlockSpec((tm, tn), lambda j, i: (i, j)),
        out_shape=jax.ShapeDtypeStruct((t, n), F32),
        compiler_params=_cparams(dimension_semantics=("arbitrary", "arbitrary")),
    )(*a_parts, b)


def mm_nn_cols(a, b, start, widths, name, tm=ROW_TILE):
    t, k = a.shape
    tn = 1024
    assert start % tn == 0 and sum(widths) <= tn

    def body(a_ref, b_ref, *o_refs):
        r = _dg(a_ref[...].astype(BF16), b_ref[...].astype(BF16), 1, 0)
        off = 0
        for o_ref, w in zip(o_refs, widths):
            o_ref[...] = r[:, off:off + w]
            off += w

    return pl.pallas_call(
        body, name=name, grid=(t // tm,),
        in_specs=[pl.BlockSpec((tm, k), lambda i: (i, 0)), pl.BlockSpec((k, tn), lambda i: (0, start // tn))],
        out_specs=[pl.BlockSpec((tm, w), lambda i: (i, 0)) for w in widths],
        out_shape=[jax.ShapeDtypeStruct((t, w), F32) for w in widths],
        compiler_params=_cparams(dimension_semantics=("arbitrary",)),
    )(a, b)


def mm_nt(a_parts, b, name, tm=ROW_TILE):
    t = a_parts[0].shape[0]
    n, k = b.shape
    na = len(a_parts)

    def body(*refs):
        a = _cat([r[...].astype(BF16) for r in refs[:na]])
        refs[na + 1][...] = _dg(a, refs[na][...].astype(BF16), 1, 1)

    return pl.pallas_call(
        body, name=name, grid=(t // tm,),
        in_specs=[pl.BlockSpec((tm, p.shape[1]), lambda i: (i, 0)) for p in a_parts]
        + [pl.BlockSpec((n, k), lambda i: (0, 0))],
        out_specs=pl.BlockSpec((tm, n), lambda i: (i, 0)),
        out_shape=jax.ShapeDtypeStruct((t, n), F32),
        compiler_params=_cparams(dimension_semantics=("arbitrary",)),
    )(*a_parts, b)


def mm_tn(a, b_parts, name, tm=ROW_TILE, tn=1024, out_dtype=F32):
    t, k = a.shape
    widths = [p.shape[1] for p in b_parts]
    n = sum(widths)
    assert n % tn == 0
    groups, cur, acc = [], [], 0
    for idx, w in enumerate(widths):
        cur.append(idx)
        acc += w
        if acc == tn:
            groups.append(cur)
            cur, acc = [], 0
        assert acc < tn
    assert not cur
    outs = []
    for gi, grp in enumerate(groups):
        parts = [b_parts[i] for i in grp]
        npart = len(parts)
        nsteps = t // tm

        def body(*refs, npart=npart, nsteps=nsteps):
            a_v = refs[0][...].astype(BF16)
            b_v = _cat([r[...].astype(BF16) for r in refs[1:1 + npart]])
            o_ref, acc_ref = refs[1 + npart], refs[2 + npart]
            r = _dg(a_v, b_v, 0, 0)

            @pl.when(pl.program_id(0) == 0)
            def _():
                acc_ref[...] = r

            @pl.when(pl.program_id(0) != 0)
            def _():
                acc_ref[...] += r

            @pl.when(pl.program_id(0) == nsteps - 1)
            def _():
                o_ref[...] = acc_ref[...].astype(o_ref.dtype)

        outs.append(pl.pallas_call(
            body, name=f"{name}_{gi}", grid=(nsteps,),
            in_specs=[pl.BlockSpec((tm, k), lambda i: (i, 0))]
            + [pl.BlockSpec((tm, p.shape[1]), lambda i: (i, 0)) for p in parts],
            out_specs=pl.BlockSpec((k, tn), lambda i: (0, 0)),
            out_shape=jax.ShapeDtypeStruct((k, tn), out_dtype),
            scratch_shapes=[pltpu.VMEM((k, tn), F32)],
            compiler_params=_cparams(dimension_semantics=("arbitrary",)),
        )(a, *parts))
    return outs


def _seg_of(i, nct):
    return jnp.where(i < nct, 1, 0)


def rowwise_fwd(fn, name, rows, segs, globs, out_widths, tile, nct):
    t = rows[0].shape[0]
    nr, ns, ng = len(rows), len(segs), len(globs)

    def body(*refs):
        vals = [r[...] for r in refs[:nr]] + [r[0] for r in refs[nr:nr + ns]] + [r[...] for r in refs[nr + ns:nr + ns + ng]]
        outs = fn(*vals)
        for o_ref, o in zip(refs[nr + ns + ng:], outs):
            o_ref[...] = o

    return pl.pallas_call(
        body, name=name, grid=(t // tile,),
        in_specs=[pl.BlockSpec((tile, r.shape[1]), lambda i: (i, 0)) for r in rows]
        + [pl.BlockSpec((1, 1, s.shape[2]), lambda i: (_seg_of(i, nct), 0, 0)) for s in segs]
        + [pl.BlockSpec(g.shape, lambda i: (0, 0)) for g in globs],
        out_specs=[pl.BlockSpec((tile, w), lambda i: (i, 0)) for w in out_widths],
        out_shape=[jax.ShapeDtypeStruct((t, w), F32) for w in out_widths],
        compiler_params=_cparams(dimension_semantics=("arbitrary",)),
    )(*rows, *segs, *globs)


def rowwise_bwd(fn, name, rows, segs, globs, cts, tile, nct, row_diff, glob_diff):
    t = rows[0].shape[0]
    nr, ns, ng, nc = len(rows), len(segs), len(globs), len(cts)
    d_rows = [i for i in range(nr) if row_diff[i]]
    d_globs = [i for i in range(ng) if glob_diff[i]]

    def body(*refs):
        in_refs, out_refs = refs[:nr + ns + ng + nc], refs[nr + ns + ng + nc:]
        row_v = [r[...] for r in in_refs[:nr]]
        seg_v = [r[0] for r in in_refs[nr:nr + ns]]
        glob_v = [r[...] for r in in_refs[nr + ns:nr + ns + ng]]
        ct_v = tuple(r[...] for r in in_refs[nr + ns + ng:])

        def wrapped(dr, sv, dg):
            rv = list(row_v)
            for j, i in enumerate(d_rows):
                rv[i] = dr[j]
            gv = list(glob_v)
            for j, i in enumerate(d_globs):
                gv[i] = dg[j]
            return tuple(fn(*rv, *sv, *gv))

        _, vjp = jax.vjp(wrapped, [row_v[i] for i in d_rows], seg_v, [glob_v[i] for i in d_globs])
        c_rows, c_segs, c_globs = vjp(ct_v)
        i = pl.program_id(0)
        k = 0
        for c in c_rows:
            out_refs[k][...] = c
            k += 1
        seg_first = jnp.logical_or(i == 0, i == nct)
        for c in c_segs:
            ref = out_refs[k]
            k += 1

            @pl.when(seg_first)
            def _(ref=ref, c=c):
                ref[0] = c

            @pl.when(jnp.logical_not(seg_first))
            def _(ref=ref, c=c):
                ref[0] += c
        for c in c_globs:
            ref = out_refs[k]
            k += 1

            @pl.when(i == 0)
            def _(ref=ref, c=c):
                ref[...] = c

            @pl.when(i != 0)
            def _(ref=ref, c=c):
                ref[...] += c

    return pl.pallas_call(
        body, name=name, grid=(t // tile,),
        in_specs=[pl.BlockSpec((tile, r.shape[1]), lambda i: (i, 0)) for r in rows]
        + [pl.BlockSpec((1, 1, s.shape[2]), lambda i: (_seg_of(i, nct), 0, 0)) for s in segs]
        + [pl.BlockSpec(g.shape, lambda i: (0, 0)) for g in globs]
        + [pl.BlockSpec((tile, c.shape[1]), lambda i: (i, 0)) for c in cts],
        out_specs=[pl.BlockSpec((tile, rows[i].shape[1]), lambda i: (i, 0)) for i in d_rows]
        + [pl.BlockSpec((1, 1, s.shape[2]), lambda i: (_seg_of(i, nct), 0, 0)) for s in segs]
        + [pl.BlockSpec(globs[i].shape, lambda i: (0, 0)) for i in d_globs],
        out_shape=[jax.ShapeDtypeStruct(rows[i].shape, F32) for i in d_rows]
        + [jax.ShapeDtypeStruct(s.shape, F32) for s in segs]
        + [jax.ShapeDtypeStruct(globs[i].shape, F32) for i in d_globs],
        compiler_params=_cparams(dimension_semantics=("arbitrary",)),
    )(*rows, *segs, *globs, *cts)


def f_pre(x, mod, g_pre):
    shift, scale = mod[:, :D], mod[:, D:2 * D]
    rs = lax.rsqrt(jnp.mean(x * x, axis=-1, keepdims=True) + EPS)
    return ((x * rs) * g_pre * (1.0 + scale) + shift,)


def f_post(x, out, mod, g_post):
    gate = mod[:, 2 * D:]
    rs = lax.rsqrt(jnp.mean(out * out, axis=-1, keepdims=True) + EPS)
    return (x + gate * ((out * rs) * g_post),)


def f_mix(o_gla, o_na, y5, u5, pm, gcols, g_norm, s5_d, w_glu, b_glu, wpool, pool_scale, havg, e4):
    ms = mdot(o_gla * o_gla, havg)
    y_gla = o_gla * lax.rsqrt(ms + EPS) * jnp.sum(hdot(g_norm, e4), axis=0, keepdims=True)
    g = _gelu(u5 * s5_d + y5)
    y_s5 = g * jax.nn.sigmoid(bdot(g, w_glu) + b_glu)
    y_pool = bdot(pm, wpool) * pool_scale
    ycat = jnp.concatenate([y_gla, o_na, y_s5, y_pool], axis=-1)
    return (ycat * _silu(gcols),)


@jax.custom_vjp
def _rot_half16(x):
    lane = lax.broadcasted_iota(jnp.int32, x.shape, 1)
    first = jnp.bitwise_and(lane, 15) < 8
    return jnp.where(first, -pltpu.roll(x, x.shape[1] - 8, 1), pltpu.roll(x, 8, 1))


def _rot_fwd(x):
    return _rot_half16(x), None


def _rot_bwd(_, ct):
    return (-_rot_half16(ct),)


_rot_half16.defvjp(_rot_fwd, _rot_bwd)


def f_gla_prep(pk, pg, pq, cos, sin, wg, bg):
    z = bdot(pg, wg) + bg
    lg = _log_sigmoid(z) * (1.0 / 16.0)
    k_r = pk * cos + _rot_half16(pk) * sin
    q_r = (pq * cos + _rot_half16(pq) * sin) * (32.0 ** -0.5)
    return q_r, k_r, lg[:, :128], lg[:, 128:]


def _gla_consts(rev):
    c = GLA_CHUNK
    i = np.arange(c)
    inc = (i[None, :] >= i[:, None]) if rev else (i[None, :] <= i[:, None])
    mq = np.stack([(np.arange(128) // 32 == h) for h in range(4)]).astype(np.float32).reshape(4, 1, 128)
    mv = np.stack([(np.arange(256) // 64 == h) for h in range(4)]).astype(np.float32).reshape(4, 1, 256)
    bdt = (np.arange(256)[:, None] // 64 == np.arange(128)[None, :] // 32).astype(np.float32)
    inc = inc.astype(np.float32)
    return jnp.asarray(inc), jnp.asarray(inc.T.copy()), jnp.asarray(mq), jnp.asarray(mv), jnp.asarray(bdt)


def _stack_heads(x, m_ref):
    return jnp.concatenate([x * m_ref[h] for h in range(4)], axis=0)


def _tile4(m):
    return jnp.concatenate([m, m, m, m], axis=0)


def _fold_heads(r4, m_ref):
    r = r4.shape[0] // 4
    out = m_ref[0] * r4[0:r]
    for h in range(1, 4):
        out = out + m_ref[h] * r4[h * r:(h + 1) * r]
    return out


def _gla_chunk_of(s, n_ctx_chunks, n_chunks, rev):
    if not rev:
        return s
    return jnp.where(s < n_ctx_chunks, n_ctx_chunks - 1 - s, n_ctx_chunks + n_chunks - 1 - s)


def gla_scan_fwd(q, k, v, lg, acc, n_ctx_rows, rev, name, comm=None):
    t = q.shape[0]
    nch, ncc = t // GLA_CHUNK, n_ctx_rows // GLA_CHUNK
    inc, inc_t, mq, mv, bdt = _gla_consts(rev)

    def body(q_ref, k_ref, v_ref, lg_ref, acc_ref, inc_ref, inct_ref, mq_ref, mv_ref, bdt_ref, o_ref, st_ref):
        lmask, lmask_t = inc_ref[...], inct_ref[...]
        bd = bdt_ref[...]

        def step(s, st):
            c = _gla_chunk_of(s, ncc, nch, rev)
            rows = pl.ds(pl.multiple_of(c * GLA_CHUNK, GLA_CHUNK), GLA_CHUNK)
            qc, kc, vc, lgc = q_ref[rows, :], k_ref[rows, :], v_ref[rows, :], lg_ref[rows, :]
            st_ref[c] = st
            b = hdot(lmask, lgc)
            blast = jnp.sum(lgc, axis=0, keepdims=True)
            qe, ke, kd = qc * jnp.exp(b), kc * jnp.exp(-b), kc * jnp.exp(blast - b)
            ke4, v4 = _stack_heads(ke, mq_ref), _stack_heads(vc, mv_ref)
            at = _tile4(lmask_t) * b_nt(ke4, qe)
            o_ref[rows, :] = acc_ref[rows, :] + b_nt(qe, st) + b_tn(at, v4)
            return st * jnp.exp(blast) + bd * mdot_tn(vc, kd)

        lax.fori_loop(0, nch, step, jnp.zeros((256, 128), F32))

    return _call_with_exchange(body, name, [q, k, v, lg, acc, inc, inc_t, mq, mv, bdt],
                               [jax.ShapeDtypeStruct((t, 256), F32), jax.ShapeDtypeStruct((nch, 256, 128), F32)], comm)


def gla_scan_bwd(q, k, v, lg, st, do, acc, n_ctx_rows, rev, name, comm=None):
    t = q.shape[0]
    nch, ncc = t // GLA_CHUNK, n_ctx_rows // GLA_CHUNK
    inc, inc_t, mq, mv, bdt = _gla_consts(rev)

    def body(q_ref, k_ref, v_ref, lg_ref, st_ref, do_ref, aq_ref, ak_ref, av_ref, inc_ref, inct_ref, mq_ref, mv_ref, bdt_ref,
             dq_ref, dk_ref, dv_ref, dlg_ref):
        lmask, lmask_t = inc_ref[...], inct_ref[...]
        bd = bdt_ref[...]

        def step(j, carry):
            dst, gsum = carry
            s = nch - 1 - j
            c = _gla_chunk_of(s, ncc, nch, rev)
            rows = pl.ds(pl.multiple_of(c * GLA_CHUNK, GLA_CHUNK), GLA_CHUNK)
            qc, kc, vc, lgc, doc = q_ref[rows, :], k_ref[rows, :], v_ref[rows, :], lg_ref[rows, :], do_ref[rows, :]
            stc = st_ref[c]
            b = hdot(lmask, lgc)
            blast = jnp.sum(lgc, axis=0, keepdims=True)
            eb, enb, edb = jnp.exp(b), jnp.exp(-b), jnp.exp(blast - b)
            qe, ke, kd = qc * eb, kc * enb, kc * edb
            ke4, v4 = _stack_heads(ke, mq_ref), _stack_heads(vc, mv_ref)
            lm4 = _tile4(lmask_t)
            at = lm4 * b_nt(ke4, qe)
            dat = lm4 * mdot_nt(v4, doc)
            dqe = mdot(doc, stc) + mdot_tn(dat, ke4)
            dke = _fold_heads(mdot(dat, qe), mq_ref)
            dv = b_nt(kd, dst) + _fold_heads(b_nn(at, doc), mv_ref)
            dkd = mdot(vc, dst)
            dq = dqe * eb
            dk = dke * enb + dkd * edb
            g = qc * dq - kc * dk
            dlg_ref[rows, :] = hdot_tn(lmask, g) + gsum
            dq_ref[rows, :] = aq_ref[rows, :] + dq
            dk_ref[rows, :] = ak_ref[rows, :] + dk
            dv_ref[rows, :] = av_ref[rows, :] + dv
            dst_new = dst * jnp.exp(blast) + bd * mdot_tn(doc, qe)
            return dst_new, gsum + jnp.sum(g, axis=0, keepdims=True)

        lax.fori_loop(0, nch, step, (jnp.zeros((256, 128), F32), jnp.zeros((1, 128), F32)))

    return _call_with_exchange(body, name, [q, k, v, lg, st, do, *acc, inc, inc_t, mq, mv, bdt],
                               [jax.ShapeDtypeStruct((t, 128), F32), jax.ShapeDtypeStruct((t, 128), F32),
                                jax.ShapeDtypeStruct((t, 256), F32), jax.ShapeDtypeStruct((t, 128), F32)], comm)


def whole_fwd(fn, name, args, out_shapes):
    def body(*refs):
        outs = fn(*[r[...] for r in refs[:len(args)]])
        for o_ref, o in zip(refs[len(args):], outs):
            o_ref[...] = o

    vm = pl.BlockSpec(memory_space=pltpu.VMEM)
    return pl.pallas_call(
        body, name=name, in_specs=[vm] * len(args), out_specs=[vm] * len(out_shapes),
        out_shape=[jax.ShapeDtypeStruct(s, F32) for s in out_shapes], compiler_params=_cparams(),
    )(*args)


def whole_bwd(fn, name, args, cts, diff):
    d_idx = [i for i in range(len(args)) if diff[i]]

    def body(*refs):
        vals = [r[...] for r in refs[:len(args)]]
        ct_v = tuple(r[...] for r in refs[len(args):len(args) + len(cts)])

        def wrapped(dv):
            av = list(vals)
            for j, i in enumerate(d_idx):
                av[i] = dv[j]
            return tuple(fn(*av))

        _, vjp = jax.vjp(wrapped, [vals[i] for i in d_idx])
        (c_args,) = vjp(ct_v)
        for o_ref, c in zip(refs[len(args) + len(cts):], c_args):
            o_ref[...] = c

    vm = pl.BlockSpec(memory_space=pltpu.VMEM)
    return pl.pallas_call(
        body, name=name, in_specs=[vm] * (len(args) + len(cts)), out_specs=[vm] * len(d_idx),
        out_shape=[jax.ShapeDtypeStruct(args[i].shape, F32) for i in d_idx], compiler_params=_cparams(),
    )(*args, *cts)


def _s5_consts():
    e_rep = (np.arange(256)[:, None] // 16 == np.arange(16)[None, :]).astype(np.float32)
    e_tile = (np.arange(64)[:, None] == np.arange(1024)[None, :] % 64).astype(np.float32)
    gmask = (np.arange(16)[:, None] == np.arange(1024)[None, :] // 64).astype(np.float32)
    bdm = (np.arange(256)[:, None] // 16 == np.arange(1024)[None, :] // 64).astype(np.float32)
    return jnp.asarray(e_rep), jnp.asarray(e_tile), jnp.asarray(gmask), jnp.asarray(bdm)


def f_s5_params(lam_re, lam_im, log_dt, bt_re, bt_im, ct_re, ct_im, e_rep, e_tile, gmask, bdm):
    dt = jnp.exp(log_dt)
    mag = jnp.exp(lam_re * dt)
    ang = lam_im * dt
    lb_re, lb_im = mag * jnp.cos(ang), mag * jnp.sin(ang)
    num_re, num_im = lb_re - 1.0, lb_im
    den = lam_re * lam_re + lam_im * lam_im
    coef_re = (num_re * lam_re + num_im * lam_im) / den
    coef_im = (num_im * lam_re - num_re * lam_im) / den
    cr, ci = hdot(e_rep, coef_re), hdot(e_rep, coef_im)
    bbt_re = cr * bt_re - ci * bt_im
    bbt_im = cr * bt_im + ci * bt_re
    a_re = jnp.sum(hdot(lb_re, e_tile) * gmask, axis=0, keepdims=True)
    a_im = jnp.sum(hdot(lb_im, e_tile) * gmask, axis=0, keepdims=True)
    return (a_re, a_im, hdot(bbt_re, e_tile) * bdm, hdot(bbt_im, e_tile) * bdm,
            hdot(ct_re, e_tile) * bdm, hdot(ct_im, e_tile) * bdm)


def _s5_doubling(xr, xi, pr, pi, pos, n, steps, rev):
    rows = xr.shape[0]
    for s in steps:
        if rev:
            keep = pos < (n - s)
            sr, si = pltpu.roll(xr, rows - s, 0), pltpu.roll(xi, rows - s, 0)
        else:
            keep = pos >= s
            sr, si = pltpu.roll(xr, s, 0), pltpu.roll(xi, s, 0)
        sr, si = jnp.where(keep, sr, 0.0), jnp.where(keep, si, 0.0)
        xr, xi = xr + pr * sr - pi * si, xi + pr * si + pi * sr
        pr, pi = pr * pr - pi * pi, 2.0 * pr * pi
    return xr, xi, pr, pi


SUBLANES = 8


def _s5_scan(xr, xi, a_re, a_im, rev, chunk, scr):
    xs_r, xs_i, yp_r, yp_i = scr
    ng = chunk // SUBLANES
    x3r, x3i = xr.reshape(ng, SUBLANES, 1024), xi.reshape(ng, SUBLANES, 1024)
    sub = lax.broadcasted_iota(jnp.int32, (SUBLANES, 1024), 0)
    a8r, a8i = a_re, a_im
    for s in (1, 2, 4):
        keep = sub < (SUBLANES - s) if rev else sub >= s
        mr, mi = jnp.where(keep, a8r, 0.0)[None], jnp.where(keep, a8i, 0.0)[None]
        shift = SUBLANES - s if rev else s
        sr, si = pltpu.roll(x3r, shift, 1), pltpu.roll(x3i, shift, 1)
        x3r, x3i = x3r + mr * sr - mi * si, x3i + mr * si + mi * sr
        a8r, a8i = a8r * a8r - a8i * a8i, 2.0 * a8r * a8i
    xr, xi = x3r.reshape(chunk, 1024), x3i.reshape(chunk, 1024)
    nblk = 1024 // 128
    for j in range(nblk):
        xs_r[j] = xr[:, 128 * j:128 * (j + 1)]
        xs_i[j] = xi[:, 128 * j:128 * (j + 1)]
    edge = pl.ds(0 if rev else SUBLANES - 1, ng, stride=SUBLANES)
    gr = jnp.concatenate([xs_r[j, edge, :] for j in range(nblk)], axis=-1)
    gi = jnp.concatenate([xs_i[j, edge, :] for j in range(nblk)], axis=-1)
    grow = lax.broadcasted_iota(jnp.int32, (ng, 1024), 0)
    steps = tuple(1 << k for k in range((ng - 1).bit_length()))
    gr, gi, _, _ = _s5_doubling(gr, gi, a8r, a8i, grow, ng, steps, rev)
    if rev:
        yp_r[...] = jnp.where(grow < ng - 1, pltpu.roll(gr, ng - 1, 0), 0.0)
        yp_i[...] = jnp.where(grow < ng - 1, pltpu.roll(gi, ng - 1, 0), 0.0)
    else:
        yp_r[...] = jnp.where(grow >= 1, pltpu.roll(gr, 1, 0), 0.0)
        yp_i[...] = jnp.where(grow >= 1, pltpu.roll(gi, 1, 0), 0.0)
    sub = lax.broadcasted_iota(jnp.int32, (SUBLANES, 1024), 0)
    tr, ti = jnp.zeros((SUBLANES, 1024), F32), jnp.zeros((SUBLANES, 1024), F32)
    cr, ci = a_re, a_im
    for n in range(1, SUBLANES + 1):
        r = SUBLANES - n if rev else n - 1
        tr, ti = jnp.where(sub == r, cr, tr), jnp.where(sub == r, ci, ti)
        cr, ci = cr * a_re - ci * a_im, cr * a_im + ci * a_re
    for j in range(nblk):
        lanes = slice(128 * j, 128 * (j + 1))
        tr_j, ti_j = tr[:, lanes], ti[:, lanes]
        for g in range(ng):
            rows = slice(g * SUBLANES, (g + 1) * SUBLANES)
            er, ei = yp_r[g:g + 1, lanes], yp_i[g:g + 1, lanes]
            xs_r[j, rows, :] = xs_r[j, rows, :] + tr_j * er - ti_j * ei
            xs_i[j, rows, :] = xs_i[j, rows, :] + tr_j * ei + ti_j * er
    return (jnp.concatenate([xs_r[j] for j in range(nblk)], axis=-1),
            jnp.concatenate([xs_i[j] for j in range(nblk)], axis=-1))


def _s5_scratch(chunk):
    return [pltpu.VMEM((8, chunk, 128), F32), pltpu.VMEM((8, chunk, 128), F32),
            pltpu.VMEM((chunk // SUBLANES, 1024), F32), pltpu.VMEM((chunk // SUBLANES, 1024), F32)]


def _s5_chunk_states(u_c, x0r, x0i, a_re, a_im, bb_re, bb_im, rev, chunk, scr):
    row = lax.broadcasted_iota(jnp.int32, (chunk, 1024), 0)
    first = row == (chunk - 1 if rev else 0)
    inj_r = a_re * x0r - a_im * x0i
    inj_i = a_re * x0i + a_im * x0r
    xr = b_nn(u_c, bb_re) + jnp.where(first, inj_r, 0.0)
    xi = b_nn(u_c, bb_im) + jnp.where(first, inj_i, 0.0)
    return _s5_scan(xr, xi, a_re, a_im, rev, chunk, scr)


def _row_pick(x, idx):
    row = lax.broadcasted_iota(jnp.int32, x.shape, 0)
    return jnp.sum(jnp.where(row == idx, x, 0.0), axis=0, keepdims=True)


def s5_scan_fwd(u, acc, a_re, a_im, bb_re, bb_im, cc_re, cc_im, n_ctx_rows, chunk, rev, name, comm=None):
    t = u.shape[0]
    nch, ncc = t // chunk, n_ctx_rows // chunk

    def body(u_ref, acc_ref, ar_ref, ai_ref, br_ref, bi_ref, cr_ref, ci_ref, y_ref, x0r_ref, x0i_ref, xsr_ref, xsi_ref, *scr):
        a_r, a_i = ar_ref[...], ai_ref[...]

        def step(s, carry):
            x0r, x0i = carry
            c = _gla_chunk_of(s, ncc, nch, rev)
            rows = pl.ds(pl.multiple_of(c * chunk, chunk), chunk)
            x0r_ref[c] = x0r
            x0i_ref[c] = x0i
            xr, xi = _s5_chunk_states(u_ref[rows, :], x0r, x0i, a_r, a_i, br_ref[...], bi_ref[...], rev, chunk, scr)
            y_ref[rows, :] = acc_ref[rows, :] + b_nt(xr, cr_ref[...]) - b_nt(xi, ci_ref[...])
            xsr_ref[rows, :] = xr.astype(BF16)
            xsi_ref[rows, :] = xi.astype(BF16)
            last = 0 if rev else chunk - 1
            return _row_pick(xr, last), _row_pick(xi, last)

        lax.fori_loop(0, nch, step, (jnp.zeros((1, 1024), F32), jnp.zeros((1, 1024), F32)))

    return _call_with_exchange(
        body, name, [u, acc, a_re, a_im, bb_re, bb_im, cc_re, cc_im],
        [jax.ShapeDtypeStruct((t, 256), F32), jax.ShapeDtypeStruct((nch, 1, 1024), F32),
         jax.ShapeDtypeStruct((nch, 1, 1024), F32), jax.ShapeDtypeStruct((t, 1024), BF16),
         jax.ShapeDtypeStruct((t, 1024), BF16)], comm, _s5_scratch(chunk))


def s5_scan_bwd(u, dy, du_acc, x0r, x0i, xsr, xsi, a_re, a_im, bb_re, bb_im, cc_re, cc_im, n_ctx_rows, chunk, rev, name):
    t = u.shape[0]
    nch, ncc = t // chunk, n_ctx_rows // chunk

    def body(u_ref, dy_ref, dua_ref, x0r_ref, x0i_ref, xsr_ref, xsi_ref, ar_ref, ai_ref, br_ref, bi_ref, cr_ref, ci_ref,
             du_ref, dar_ref, dai_ref, dbr_ref, dbi_ref, dcr_ref, dci_ref, *scr):
        a_r, a_i = ar_ref[...], ai_ref[...]
        for ref in (dbr_ref, dbi_ref, dcr_ref, dci_ref):
            ref[...] = jnp.zeros_like(ref)
        row = lax.broadcasted_iota(jnp.int32, (chunk, 1024), 0)
        first_idx, last_idx = (chunk - 1, 0) if rev else (0, chunk - 1)

        def step(j, carry):
            lcr, lci, dar, dai = carry
            s = nch - 1 - j
            c = _gla_chunk_of(s, ncc, nch, rev)
            rows = pl.ds(pl.multiple_of(c * chunk, chunk), chunk)
            u_c, dy_c = u_ref[rows, :], dy_ref[rows, :]
            x0r_c, x0i_c = x0r_ref[c], x0i_ref[c]
            xr, xi = xsr_ref[rows, :].astype(F32), xsi_ref[rows, :].astype(F32)
            dcr_ref[...] += b_tn(dy_c, xr)
            dci_ref[...] -= b_tn(dy_c, xi)
            inj_r = a_r * lcr + a_i * lci
            inj_i = a_r * lci - a_i * lcr
            is_last = row == last_idx
            lr = b_nn(dy_c, cr_ref[...]) + jnp.where(is_last, inj_r, 0.0)
            li = -b_nn(dy_c, ci_ref[...]) + jnp.where(is_last, inj_i, 0.0)
            lr, li = _s5_scan(lr, li, a_r, -a_i, not rev, chunk, scr)
            du_ref[rows, :] = dua_ref[rows, :] + b_nt(lr, br_ref[...]) + b_nt(li, bi_ref[...])
            dbr_ref[...] += b_tn(u_c, lr)
            dbi_ref[...] += b_tn(u_c, li)
            if rev:
                pr, pi = pltpu.roll(xr, chunk - 1, 0), pltpu.roll(xi, chunk - 1, 0)
            else:
                pr, pi = pltpu.roll(xr, 1, 0), pltpu.roll(xi, 1, 0)
            is_first = row == first_idx
            pr, pi = jnp.where(is_first, x0r_c, pr), jnp.where(is_first, x0i_c, pi)
            dar = dar + jnp.sum(lr * pr + li * pi, axis=0, keepdims=True)
            dai = dai + jnp.sum(li * pr - lr * pi, axis=0, keepdims=True)
            return _row_pick(lr, first_idx), _row_pick(li, first_idx), dar, dai

        z = jnp.zeros((1, 1024), F32)
        _, _, dar, dai = lax.fori_loop(0, nch, step, (z, z, z, z))
        dar_ref[...] = dar
        dai_ref[...] = dai

    vm = pl.BlockSpec(memory_space=pltpu.VMEM)
    big = jax.ShapeDtypeStruct((256, 1024), F32)
    vec = jax.ShapeDtypeStruct((1, 1024), F32)
    return pl.pallas_call(
        body, name=name, in_specs=[vm] * 13, out_specs=[vm] * 7,
        out_shape=[jax.ShapeDtypeStruct((t, 256), F32), vec, vec, big, big, big, big],
        scratch_shapes=_s5_scratch(chunk), compiler_params=_cparams(),
    )(u, dy, du_acc, x0r, x0i, xsr, xsi, a_re, a_im, bb_re, bb_im, cc_re, cc_im)


POOL_HALO = 8


def pool_apply(u_pad, n, transpose, name, tile=ROW_TILE):
    tile = min(tile, n)
    ext = tile + 2 * POOL_HALO
    trel = np.arange(ext)[None, :] - POOL_HALO - np.arange(tile)[:, None]
    if transpose:
        trel = -trel
    band4 = np.concatenate([((trel >= -(1 << w)) & (trel <= (1 << w) - 1)) for w in range(4)], axis=0).astype(np.float32)

    def body(u_ref, band_ref, lm_ref, o_ref):
        lax.fori_loop(0, n // tile, functools.partial(step, u_ref, band_ref, lm_ref, o_ref), 0)

    def step(u_ref, band_ref, lm_ref, o_ref, i, carry):
        val = u_ref[pl.ds(pl.multiple_of(i * tile, tile), ext), :]
        lane = lax.broadcasted_iota(jnp.int32, (ext, 256), 1)
        half = jnp.left_shift(1, jnp.right_shift(lane, 6))
        trow = lax.broadcasted_iota(jnp.int32, (ext, 256), 0) + (i * tile - POOL_HALO)
        cnt = jnp.minimum(trow + half, n) - jnp.maximum(trow - half, 0)
        inv = 1.0 / jnp.maximum(cnt, 1).astype(F32)
        src = val * inv if transpose else val
        acc = _fold_heads(mdot(band_ref[...], src), lm_ref)
        centre = val[POOL_HALO:POOL_HALO + tile]
        if not transpose:
            acc = acc * inv[POOL_HALO:POOL_HALO + tile]
        o_ref[pl.ds(pl.multiple_of(i * tile, tile), tile), :] = acc - centre
        return carry

    vm = pl.BlockSpec(memory_space=pltpu.VMEM)
    return pl.pallas_call(
        body, name=name, in_specs=[vm] * 3, out_specs=vm,
        out_shape=jax.ShapeDtypeStruct((n, 256), F32), compiler_params=_cparams(),
    )(u_pad, jnp.asarray(band4), _na_head_masks())


NA_SCALE = 64.0 ** -0.5
NEG = -1e30


def _call_with_exchange(compute, name, args, out_shapes, comm, scratch=()):
    vm = pl.BlockSpec(memory_space=pltpu.VMEM)
    n_in, n_out = len(args), len(out_shapes)
    if comm is None:
        outs = pl.pallas_call(compute, name=name, in_specs=[vm] * n_in, out_specs=[vm] * n_out, out_shape=out_shapes,
                              scratch_shapes=list(scratch), compiler_params=_cparams())(*args)
        return outs, None
    arrays, scatter = comm
    n = len(arrays)

    def body(*refs):
        c_in = refs[n_in:n_in + n]
        c_out = refs[n_in + n + n_out:n_in + 2 * n + n_out]
        scr = refs[n_in + 2 * n + n_out:n_in + 2 * n + n_out + len(scratch)]
        finish = _exchange_issue(c_in, c_out, scatter, *refs[n_in + 2 * n + n_out + len(scratch):])
        compute(*refs[:n_in], *refs[n_in + n:n_in + n + n_out], *scr)
        finish()

    hbm = pl.BlockSpec(memory_space=pl.ANY)
    outs = pl.pallas_call(
        body, name=name, in_specs=[vm] * n_in + [hbm] * n, out_specs=[vm] * n_out + [hbm] * n,
        out_shape=list(out_shapes) + _exchange_out_shapes(arrays, scatter), scratch_shapes=list(scratch) + _exchange_sems(n),
        compiler_params=_cparams(has_side_effects=True),
    )(*args, *arrays)
    return outs[:n_out], outs[n_out:]


def _na_head_masks():
    return jnp.asarray(np.stack([(np.arange(256) // 64 == h) for h in range(4)]).astype(np.float32).reshape(4, 1, 256))


def _na_window(r, rows):
    start = jnp.clip(r - 4, 0, rows - 8)
    return start, start - r + 7


def _na_probs(qh, kw, kc, bias):
    s_c = b_nt(qh, kc)
    m = jnp.max(s_c, axis=-1, keepdims=True)
    if kw is not None:
        s_w = b_nt(qh, kw) + bias
        m = jnp.maximum(m, jnp.max(s_w, axis=-1, keepdims=True))
        p_w = jnp.exp(s_w - m)
    p_c = jnp.exp(s_c - m)
    l = jnp.sum(p_c, axis=-1, keepdims=True)
    if kw is not None:
        l = l + jnp.sum(p_w, axis=-1, keepdims=True)
        return p_w / l, p_c / l
    return None, p_c / l


def na_fwd(q, k, v, bias8, n_ctx_rows, name, comm=None):
    t = q.shape[0]
    m_ctx = n_ctx_rows
    rows = (t - m_ctx) // GRID_W
    hm = _na_head_masks()

    def body(q_ref, k_ref, v_ref, b_ref, hm_ref, o_ref):
        kc, vc = k_ref[0:m_ctx, :], v_ref[0:m_ctx, :]

        def ctx_step(i, _):
            rs = pl.ds(pl.multiple_of(i * 64, 64), 64)
            q4 = _stack_heads(q_ref[rs, :] * NA_SCALE, hm_ref)
            _, p_c = _na_probs(q4, None, kc, None)
            o_ref[rs, :] = _fold_heads(b_nn(p_c, vc), hm_ref)
            return 0

        lax.fori_loop(0, m_ctx // 64, ctx_step, 0)

        def lat_step(r, _):
            start, off = _na_window(r, rows)
            rs = pl.ds(pl.multiple_of(m_ctx + r * 64, 64), 64)
            ws = pl.ds(pl.multiple_of(m_ctx + start * 64, 64), 512)
            q4 = _stack_heads(q_ref[rs, :] * NA_SCALE, hm_ref)
            kw, vw = k_ref[ws, :], v_ref[ws, :]
            p_w, p_c = _na_probs(q4, kw, kc, b_ref[off])
            o_ref[rs, :] = _fold_heads(b_nn(p_w, vw) + b_nn(p_c, vc), hm_ref)
            return 0

        lax.fori_loop(0, rows, lat_step, 0)

    (o,), received = _call_with_exchange(body, name, [q, k, v, bias8, hm], [jax.ShapeDtypeStruct((t, 256), F32)], comm)
    return o if comm is None else (o, received)


def na_bwd(q, k, v, do, bias8, n_ctx_rows, name, comm=None):
    t = q.shape[0]
    m_ctx = n_ctx_rows
    rows = (t - m_ctx) // GRID_W
    hm = _na_head_masks()

    def body(q_ref, k_ref, v_ref, do_ref, b_ref, hm_ref, dq_ref, dk_ref, dv_ref, db_ref):
        kc, vc = k_ref[0:m_ctx, :], v_ref[0:m_ctx, :]
        dk_ref[...] = jnp.zeros_like(dk_ref)
        dv_ref[...] = jnp.zeros_like(dv_ref)
        db_ref[...] = jnp.zeros_like(db_ref)

        def head_terms(qh, doh, kw, vw, bias):
            p_w, p_c = _na_probs(qh, kw, kc, bias)
            dp_c = b_nt(doh, vc)
            delta = jnp.sum(p_c * dp_c, axis=-1, keepdims=True)
            if kw is not None:
                dp_w = b_nt(doh, vw)
                delta = delta + jnp.sum(p_w * dp_w, axis=-1, keepdims=True)
                ds_w = p_w * (dp_w - delta)
            else:
                ds_w = None
            ds_c = p_c * (dp_c - delta)
            return p_w, p_c, ds_w, ds_c

        def ctx_step(i, carry):
            dkc, dvc = carry
            rs = pl.ds(pl.multiple_of(i * 64, 64), 64)
            q4, do4 = _stack_heads(q_ref[rs, :] * NA_SCALE, hm_ref), _stack_heads(do_ref[rs, :], hm_ref)
            _, p_c, _, ds_c = head_terms(q4, do4, None, None, None)
            dq_ref[rs, :] = _fold_heads(b_nn(ds_c, kc), hm_ref) * NA_SCALE
            return dkc + b_tn(ds_c, q4), dvc + b_tn(p_c, do4)

        zc = jnp.zeros((m_ctx, 256), F32)
        carry = lax.fori_loop(0, m_ctx // 64, ctx_step, (zc, zc))

        def lat_step(r, carry):
            dkc, dvc = carry
            start, off = _na_window(r, rows)
            rs = pl.ds(pl.multiple_of(m_ctx + r * 64, 64), 64)
            ws = pl.ds(pl.multiple_of(m_ctx + start * 64, 64), 512)
            q4, do4 = _stack_heads(q_ref[rs, :] * NA_SCALE, hm_ref), _stack_heads(do_ref[rs, :], hm_ref)
            kw, vw = k_ref[ws, :], v_ref[ws, :]
            p_w, p_c, ds_w, ds_c = head_terms(q4, do4, kw, vw, b_ref[off])
            dq_ref[rs, :] = _fold_heads(b_nn(ds_w, kw) + b_nn(ds_c, kc), hm_ref) * NA_SCALE
            dk_ref[ws, :] += b_tn(ds_w, q4)
            dv_ref[ws, :] += b_tn(p_w, do4)
            db_ref[off] += ds_w
            return dkc + b_tn(ds_c, q4), dvc + b_tn(p_c, do4)

        dkc, dvc = lax.fori_loop(0, rows, lat_step, carry)
        dk_ref[0:m_ctx, :] = dkc
        dv_ref[0:m_ctx, :] = dvc

    row = jax.ShapeDtypeStruct((t, 256), F32)
    return _call_with_exchange(body, name, [q, k, v, do, bias8, hm], [row, row, row, jax.ShapeDtypeStruct(bias8.shape, F32)], comm)


def _na_toeplitz():
    col = np.arange(GRID_W)
    dd = (col[None, :] - col[:, None] + 15).reshape(-1)
    tt = np.zeros((GRID_W * GRID_W, 128), np.float32)
    ok = (dd >= 0) & (dd <= 30)
    tt[np.arange(GRID_W * GRID_W)[ok], dd[ok]] = 1.0
    return tt


def _na_bias8(rpb, name):
    col = np.arange(GRID_W)
    cs = np.clip(col - 8, 0, GRID_W - 16)
    col_mask = (col[None, :] >= cs[:, None]) & (col[None, :] < cs[:, None] + 16)
    rpb2 = jnp.pad(rpb.reshape(60, 31), ((0, 4), (0, 97)))
    (toe,) = whole_fwd(lambda r_, t_: (hdot_nt(r_, t_),), name, [rpb2, jnp.asarray(_na_toeplitz())], [(64, GRID_W * GRID_W)])
    toe = toe[:60].reshape(4, 15, GRID_W, GRID_W)
    b = jnp.stack([toe[:, off:off + 8] for off in range(8)], axis=1)
    b = jnp.where(jnp.asarray(col_mask)[None, None, None], b, NEG)
    return b.transpose(1, 0, 3, 2, 4).reshape(8, 4 * GRID_W, 8 * GRID_W)


def _na_rpb_grad(dbias8, name):
    tt = _na_toeplitz()
    sel = np.zeros((64, 256), np.float32)
    for h in range(4):
        for off in range(8):
            for i in range(8):
                sel[h * 15 + off + i, h * 64 + off * 8 + i] = 1.0
    a2 = dbias8.reshape(8, 4, GRID_W, 8, GRID_W).transpose(1, 0, 3, 2, 4).reshape(256, GRID_W * GRID_W)
    (out,) = whole_fwd(lambda a, t_, s_: (hdot(s_, hdot(a, t_)),), name, [a2, jnp.asarray(tt), jnp.asarray(sel)], [(64, 128)])
    return out[:60, :31].reshape(4, 15, 31)


def f_mod(cs, b_mod, w_mod):
    s = _silu(cs)
    return bdot(s, w_mod) + b_mod, s


def loss_and_grad(z, tgt, n_ctx_rows, name, tile=ROW_TILE):
    t, d = z.shape
    tile = min(tile, n_ctx_rows)
    nct = n_ctx_rows // tile

    def body(z_ref, t_ref, dz_ref, loss_ref):
        i = pl.program_id(0)

        @pl.when(i == 0)
        def _():
            loss_ref[...] = jnp.zeros_like(loss_ref)

        @pl.when(i < nct)
        def _():
            dz_ref[...] = jnp.zeros_like(dz_ref)

        @pl.when(i >= nct)
        def _():
            diff = z_ref[...] - t_ref[...]
            dz_ref[...] = diff * (1.0 / d)
            loss_ref[...] += 0.5 * jnp.sum(jnp.sum(diff * diff, axis=-1, keepdims=True) * (1.0 / d), axis=0, keepdims=True)

    dz, loss = pl.pallas_call(
        body, name=name, grid=(t // tile,),
        in_specs=[pl.BlockSpec((tile, d), lambda i: (i, 0)),
                  pl.BlockSpec((tile, d), lambda i: (jnp.maximum(i - nct, 0), 0))],
        out_specs=[pl.BlockSpec((tile, d), lambda i: (i, 0)), pl.BlockSpec((8, 128), lambda i: (0, 0))],
        out_shape=[jax.ShapeDtypeStruct((t, d), F32), jax.ShapeDtypeStruct((8, 128), F32)],
        compiler_params=_cparams(dimension_semantics=("arbitrary",)),
    )(z, tgt)
    return loss[0, 0], dz


def adamw(parts, w, m, v, name, tile=256):
    npart, r, c = parts.shape
    tile = min(tile, r)
    assert r % tile == 0
    c1 = 1.0 / (1.0 - ADAM_B1 ** ADAM_STEP)
    c2 = 1.0 / (1.0 - ADAM_B2 ** ADAM_STEP)

    def body(p_ref, w_ref, m_ref, v_ref, g_ref, d_ref, nm_ref, nv_ref):
        g = p_ref[0].astype(F32)
        for i in range(1, npart):
            g = g + p_ref[i].astype(F32)
        nm = ADAM_B1 * m_ref[...] + (1.0 - ADAM_B1) * g
        nv = ADAM_B2 * v_ref[...] + (1.0 - ADAM_B2) * (g * g)
        g_ref[...] = g
        nm_ref[...] = nm
        nv_ref[...] = nv
        d_ref[...] = -ADAM_LR * ((nm * c1) / (jnp.sqrt(nv * c2) + ADAM_EPS) + ADAM_WD * w_ref[...])

    blk = pl.BlockSpec((tile, c), lambda i: (i, 0))
    return pl.pallas_call(
        body, name=name, grid=(r // tile,),
        in_specs=[pl.BlockSpec((npart, tile, c), lambda i: (0, i, 0)), blk, blk, blk],
        out_specs=[blk] * 4, out_shape=[jax.ShapeDtypeStruct((r, c), F32)] * 4,
        compiler_params=_cparams(dimension_semantics=("arbitrary",)),
    )(parts, w, m, v)


def _peer(x, y, c, k):
    return (1 - x if k & 4 else x, 1 - y if k & 2 else y, 1 - c if k & 1 else c)


def _exchange_out_shapes(arrays, scatter):
    return [jax.ShapeDtypeStruct(a.shape if s else (N_DEV,) + a.shape, a.dtype) for a, s in zip(arrays, scatter)]


def _exchange_sems(n):
    return [pltpu.SemaphoreType.DMA((n, N_DEV - 1)), pltpu.SemaphoreType.DMA((n, N_DEV - 1)), pltpu.SemaphoreType.DMA((n,))]


def _exchange_issue(ins, outs, scatter, send_sems, recv_sems, local_sems):
    n = len(ins)
    x, y, c = lax.axis_index("x"), lax.axis_index("y"), lax.axis_index("c")
    me = 4 * x + 2 * y + c

    def index_of(p):
        return 4 * p[0] + 2 * p[1] + p[2]

    local = []
    for a in range(n):
        src_me = ins[a].at[me] if scatter[a] else ins[a]
        loc = pltpu.make_async_copy(src_me, outs[a].at[me], local_sems.at[a])
        loc.start()
        local.append(loc)
    for k in range(1, N_DEV):
        peer = _peer(x, y, c, k)
        for a in range(n):
            src = ins[a].at[index_of(peer)] if scatter[a] else ins[a]
            pltpu.make_async_remote_copy(
                src_ref=src, dst_ref=outs[a].at[me], send_sem=send_sems.at[a, k - 1], recv_sem=recv_sems.at[a, k - 1],
                device_id=peer, device_id_type=pl.DeviceIdType.MESH).start()

    def finish():
        for k in range(1, N_DEV):
            peer = _peer(x, y, c, k)
            for a in range(n):
                src = ins[a].at[index_of(peer)] if scatter[a] else ins[a]
                cp = pltpu.make_async_remote_copy(
                    src_ref=src, dst_ref=outs[a].at[index_of(peer)], send_sem=send_sems.at[a, k - 1],
                    recv_sem=recv_sems.at[a, k - 1], device_id=peer, device_id_type=pl.DeviceIdType.MESH)
                cp.wait_send()
                cp.wait_recv()
        for loc in local:
            loc.wait()

    return finish


def exchange(arrays, scatter, name):
    n = len(arrays)

    def body(*refs):
        _exchange_issue(refs[:n], refs[n:2 * n], scatter, *refs[2 * n:])()

    hbm = pl.BlockSpec(memory_space=pl.ANY)
    return pl.pallas_call(
        body, name=name, in_specs=[hbm] * n, out_specs=[hbm] * n, out_shape=_exchange_out_shapes(arrays, scatter),
        scratch_shapes=_exchange_sems(n), compiler_params=pltpu.CompilerParams(has_side_effects=True),
    )(*arrays)


def _rope_tables(n_lat, n_ctx):
    tok = np.arange(n_lat)
    freqs = 10000.0 ** (-np.arange(0, 16, 2, dtype=np.float32) / 16.0)

    def table(pos):
        ang = pos.astype(np.float32)[:, None] * freqs[None, :]
        ang = np.concatenate([ang, ang], axis=-1)
        return np.cos(ang), np.sin(ang)

    cr, sr = table(tok // GRID_W)
    cc, sc = table(tok % GRID_W)
    cos = np.tile(np.concatenate([cr, cc], axis=-1), (1, 4))
    sin = np.tile(np.concatenate([sr, sc], axis=-1), (1, 4))
    cos = np.concatenate([np.ones((n_ctx, 128), np.float32), cos], axis=0)
    sin = np.concatenate([np.zeros((n_ctx, 128), np.float32), sin], axis=0)
    return jnp.asarray(cos, F32), jnp.asarray(sin, F32)


def _pad_w_in(w):
    z = lambda n: jnp.zeros((w.shape[0], n), w.dtype)
    return jnp.concatenate([w[:, 1824:2848], w[:, 128:384], w[:, 416:672], w[:, 672:928], w[:, 928:1184], w[:, 1312:1568],
                            w[:, 1568:1824], w[:, 0:128], w[:, 384:416], z(96), w[:, 1184:1312], z(128)], axis=1)


def _unpad_w_in(wp):
    return jnp.concatenate([wp[:, C_GK:C_GK + 128], wp[:, C_GV:C_GV + 256], wp[:, C_GG:C_GG + 32], wp[:, C_NK:C_NK + 256],
                            wp[:, C_NV:C_NV + 256], wp[:, C_SU:C_SU + 256], wp[:, C_GQ:C_GQ + 128], wp[:, C_NQ:C_NQ + 256],
                            wp[:, C_PU:C_PU + 256], wp[:, C_GT:C_GT + 1024]], axis=1)


def _pad_rows(u):
    return jnp.pad(u, ((POOL_HALO, POOL_HALO), (0, 0)))


def _block_diag4(w):
    out = jnp.zeros((256, 256), w.dtype)
    for i in range(4):
        out = lax.dynamic_update_slice(out, w[i], (64 * i, 64 * i))
    return out


def _layer_params(p, big, l):
    e_rep, e_tile, gmask, bdm = _s5_consts()
    wg = jnp.zeros((128, 256), F32)
    wg = lax.dynamic_update_slice(wg, p["gla_w_gate"][l, 0], (0, 0))
    wg = lax.dynamic_update_slice(wg, p["gla_w_gate"][l, 1], (16, 128))
    s5 = []
    for d in range(2):
        s5.append([p["s5_lam_re"][l, d], p["s5_lam_im"][l, d], p["s5_log_dt"][l, d].reshape(16, 1),
                   p["s5_b_re"][l, d].transpose(0, 2, 1).reshape(256, 64), p["s5_b_im"][l, d].transpose(0, 2, 1).reshape(256, 64),
                   p["s5_c_re"][l, d].reshape(256, 64), p["s5_c_im"][l, d].reshape(256, 64), e_rep, e_tile, gmask, bdm])
    havg = jnp.asarray((np.arange(256)[:, None] // 64 == np.arange(256)[None, :] // 64).astype(np.float32) / 64.0)
    e4 = jnp.asarray((np.arange(64)[:, None] == np.arange(256)[None, :] % 64).astype(np.float32))
    return dict(
        g_pre=p["g_pre"][l].reshape(1, D), g_post=p["g_post"][l].reshape(1, D), b_mod=p["b_mod"][l].reshape(1, 3 * D),
        w_mod=big["w_mod"], w_in=_pad_w_in(big["w_in"]), w_out=big.get("w_out"),
        wg=wg, bg=p["gla_b_gate"][l].reshape(1, 256), g_norm=jnp.pad(p["gla_g_norm"][l].reshape(1, 64), ((0, 7), (0, 0))),
        bias8=_na_bias8(p["na_rpb"][l], f"na_bias_l{l}"), s5=s5, s5_d=p["s5_d"][l].reshape(1, 256), w_glu=None if big.get("s5_w_glu") is None else big["s5_w_glu"].astype(F32),
        b_glu=p["s5_b_glu"][l].reshape(1, 256), wpool=_block_diag4(p["pool_w"][l]), pool_scale=p["pool_scale"][l].reshape(1, 256),
        havg=havg, e4=e4)


def _matmul_tile(t, tile, steps):
    return t // steps if t % (8 * steps) == 0 else tile


def _cols(pz, start, width):
    return pz[:, start:start + width]


def _layer_fwd(z, modseg, lp, cos, sin, m_ctx, tile, s5_chunk, l, comm=None, late=None):
    t = z.shape[0]
    nct = m_ctx // tile
    nm = lambda s: f"{s}_l{l}"
    (h,) = rowwise_fwd(f_pre, nm("pre"), [z], [modseg], [lp["g_pre"]], [D], tile, nct)
    mm_tile = _matmul_tile(t, tile, 4)
    (gt,) = mm_nn_cols(h, lp["w_in"], C_GT, [1024], nm("in_proj_a"), tm=mm_tile)
    pv, nk, nv, su = mm_nn_cols(h, lp["w_in"], C_GV, [256] * 4, nm("in_proj_b"), tm=mm_tile)
    nq, pu, pk, pg, pq = mm_nn_cols(h, lp["w_in"], C_NQ, [256, 256, 128, 128, 128], nm("in_proj_c"), tm=mm_tile)
    q_r, k_r, lgf, lgb = rowwise_fwd(f_gla_prep, nm("gla_prep"), [pk, pg, pq, cos, sin], [], [lp["wg"], lp["bg"]], [128] * 4, tile, nct)
    half = None if comm is None else comm[0].shape[0] // 2
    spread = None if comm is None else [comm[0][:half], comm[1], comm[2], comm[3], comm[0][half:]]
    part = (lambda idx: None) if comm is None else (lambda idx: ([spread[i] for i in idx], [False] * len(idx)))
    (o1, st_f), got_late = gla_scan_fwd(q_r, k_r, pv, lgf, jnp.zeros((t, 256), F32), m_ctx, False, nm("gla_f"),
                                        None if late is None else (list(late), [False, False]))
    if late is not None:
        lp["w_out"], lp["w_glu"] = _gathered(got_late[0], False), _gathered(got_late[1], False).astype(F32)
    (o_gla, st_b), got_out = gla_scan_fwd(q_r, k_r, pv, lgb, o1, m_ctx, True, nm("gla_r"), part([2, 3]))
    received = None
    if comm is None:
        o_na = na_fwd(nq, nk, nv, lp["bias8"], m_ctx, nm("na"))
    else:
        o_na, got_in = na_fwd(nq, nk, nv, lp["bias8"], m_ctx, nm("na"), part([1]))
    s5p = [whole_fwd(f_s5_params, nm(f"s5_par{d}"), lp["s5"][d], [(1, 1024)] * 2 + [(256, 1024)] * 4) for d in range(2)]
    (y1, *states_f), got_mod_a = s5_scan_fwd(su, jnp.zeros((t, 256), F32), *s5p[0], m_ctx, s5_chunk, False, nm("s5_f"), part([0]))
    (y5, *states_b), got_mod_b = s5_scan_fwd(su, y1, *s5p[1], m_ctx, s5_chunk, True, nm("s5_r"), part([4]))
    if comm is not None:
        received = [jnp.concatenate([got_mod_a[0], got_mod_b[0]], axis=1), got_in[0], got_out[0], got_out[1]]
    pm = jnp.concatenate([pool_apply(_pad_rows(pu[:m_ctx]), m_ctx, False, nm("pool_c")),
                          pool_apply(_pad_rows(pu[m_ctx:]), t - m_ctx, False, nm("pool_x"))], axis=0)
    mix_rows = [o_gla, o_na, y5, su, pm, gt]
    mix_globs = [lp["g_norm"], lp["s5_d"], lp["w_glu"], lp["b_glu"], lp["wpool"], lp["pool_scale"], lp["havg"], lp["e4"]]
    (yg,) = rowwise_fwd(f_mix, nm("mix"), mix_rows, [], mix_globs, [D], tile, nct)
    out = mm_nn([yg], lp["w_out"], nm("out_proj"), tm=mm_tile)
    (z_new,) = rowwise_fwd(f_post, nm("post"), [z, out], [modseg], [lp["g_post"]], [D], tile, nct)
    saved = dict(z=z, h=h, pv=pv, nk=nk, nv=nv, su=su, nq=nq, pk=pk, pg=pg, pq=pq, q_r=q_r, k_r=k_r, lgf=lgf, lgb=lgb,
                 st_f=st_f, st_b=st_b, s5p=s5p, x0f=tuple(states_f), x0b=tuple(states_b), mix_rows=mix_rows, mix_globs=mix_globs,
                 yg=yg, out=out)
    return z_new, saved, received


def _f_pre_res(x, mod, g_pre):
    return f_pre(x, mod, g_pre)[0], x


def _layer_bwd(dz_new, sv, modseg, lp, cos, sin, m_ctx, tile, s5_chunk, l, comm=None, gdt=F32, send_early=False):
    t = dz_new.shape[0]
    nct = m_ctx // tile
    nm = lambda s: f"{s}_l{l}"
    g = {}
    dz_res, dout, dmod_post, g["g_post"] = rowwise_bwd(f_post, nm("post_b"), [sv["z"], sv["out"]], [modseg], [lp["g_post"]],
                                                       [dz_new], tile, nct, [True, True], [True])
    dyg = mm_nt([dout], lp["w_out"], nm("out_proj_dx"), tm=_matmul_tile(t, tile, 4))
    dw_tile = _matmul_tile(t, tile, 4)
    (g["w_out"],) = mm_tn(sv["yg"], [dout], nm("out_proj_dw"), tm=dw_tile, out_dtype=gdt)
    res = rowwise_bwd(f_mix, nm("mix_b"), sv["mix_rows"], [], sv["mix_globs"], [dyg], tile, nct, [True] * 6, [True] * 6 + [False] * 2)
    do_gla, do_na, dy5, dsu_a, dpm, dgt = res[:6]
    g["g_norm"], g["s5_d"], g["w_glu"], g["b_glu"], g["wpool"], g["pool_scale"] = res[6:]
    dpu = jnp.concatenate([pool_apply(_pad_rows(dpm[:m_ctx]), m_ctx, True, nm("pool_c_b")),
                           pool_apply(_pad_rows(dpm[m_ctx:]), t - m_ctx, True, nm("pool_x_b"))], axis=0)
    r_b = s5_scan_bwd(sv["su"], dy5, dsu_a, *sv["x0b"], *sv["s5p"][1], m_ctx, s5_chunk, True, nm("s5_r_b"))
    r_f = s5_scan_bwd(sv["su"], dy5, r_b[0], *sv["x0f"], *sv["s5p"][0], m_ctx, s5_chunk, False, nm("s5_f_b"))
    dsu = r_f[0]
    g["s5"] = [whole_bwd(f_s5_params, nm(f"s5_par{d}_b"), lp["s5"][d], list(r[1:]), [True] * 7 + [False] * 4)
               for d, r in ((0, r_f), (1, r_b))]
    part = (lambda idx: None) if comm is None else (lambda idx: ([comm[i] for i in idx], [True] * len(idx)))
    (dnq, dnk, dnv, dbias8), got_in = na_bwd(sv["nq"], sv["nk"], sv["nv"], do_na, lp["bias8"], m_ctx, nm("na_b"), part([0]))
    g["rpb"] = _na_rpb_grad(dbias8, nm("na_rpb_b"))
    zq, zv = jnp.zeros((t, 128), F32), jnp.zeros((t, 256), F32)
    early = ([_slabs(g["w_out"], False).astype(BF16), _slabs(g["w_glu"], False).astype(BF16)], [True, True]) if send_early else None
    (dq1, dk1, dv1, dlgb), g["early"] = gla_scan_bwd(sv["q_r"], sv["k_r"], sv["pv"], sv["lgb"], sv["st_b"], do_gla, (zq, zq, zv), m_ctx, True,
                                                     nm("gla_r_b"), early)
    (dq_r, dk_r, dpv, dlgf), got_out = gla_scan_bwd(sv["q_r"], sv["k_r"], sv["pv"], sv["lgf"], sv["st_f"], do_gla, (dq1, dk1, dv1), m_ctx, False,
                                                    nm("gla_f_b"), part([1, 2]))
    received = None if comm is None else [got_in[0], got_out[0], got_out[1]]
    dpk, dpg, dpq, g["wg"], g["bg"] = rowwise_bwd(f_gla_prep, nm("gla_prep_b"), [sv["pk"], sv["pg"], sv["pq"], cos, sin], [],
                                                  [lp["wg"], lp["bg"]], [dq_r, dk_r, dlgf, dlgb], tile, nct,
                                                  [True, True, True, False, False], [True, True])
    parts = [dgt, dpv, dnk, dnv, dsu, dnq, dpu, dpk, dpg, dpq, jnp.zeros((t, 128), F32)]
    dh = mm_nt(parts, lp["w_in"], nm("in_proj_dx"), tm=_matmul_tile(t, tile, 8))
    g["w_in"] = _unpad_w_in(jnp.concatenate(mm_tn(sv["h"], parts, nm("in_proj_dw"), tm=dw_tile, out_dtype=gdt), axis=1))
    dz, dmod_pre, g["g_pre"] = rowwise_bwd(_f_pre_res, nm("pre_b"), [sv["z"]], [modseg], [lp["g_pre"]], [dh, dz_res], tile, nct, [True], [True])
    return dz, dmod_pre, dmod_post, g, received


def _f_mod_sum(cs, b_mod, w_mod):
    mod, _ = f_mod(cs, b_mod, w_mod)
    return mod, cs


def local_step(x, c, ctx, tgt, p, shards=None, tile=ROW_TILE, s5_chunk=S5_CHUNK):
    n_lat, m_ctx = x.shape[0], ctx.shape[0]
    n_layers = p["g_pre"].shape[0]
    z = jnp.concatenate([ctx, x], axis=0)
    cos, sin = _rope_tables(n_lat, m_ctx)
    cs = jnp.concatenate([c.reshape(1, D), p["c_ctx"].reshape(1, D), jnp.zeros((6, D), F32)], axis=0)
    gather = [False] * len(_SHARDED)
    lps, mods, silus, saves = [], [], [], []
    got = exchange(shards[0][:2], [False, False], "gather_weights_l0") if shards is not None else None
    for l in range(n_layers):
        if shards is None:
            big = {n: p[n][l] for n in _SHARDED}
        else:
            big = {n: _gathered(g, _BY_COLS[n]) for n, g in zip(_SHARDED, got)}
        lp = _layer_params(p, big, l)
        mod8, s8 = whole_fwd(f_mod, f"mod_l{l}", [cs, lp["b_mod"], lp["w_mod"]], [(8, 3 * D), (8, D)])
        modseg = mod8[:2].reshape(2, 1, 3 * D)
        comm = shards[l + 1] if shards is not None and l + 1 < n_layers else None
        late = shards[0][2:] if shards is not None and l == 0 else None
        z, sv, got = _layer_fwd(z, modseg, lp, cos, sin, m_ctx, tile, s5_chunk, l, comm, late)
        lps.append(lp); mods.append(modseg); silus.append(s8); saves.append(sv)
    loss, dz = loss_and_grad(z, tgt, m_ctx, "loss", tile)
    grads, received = [None] * n_layers, [None] * n_layers
    gdt = F32 if shards is None else BF16
    dcs = jnp.zeros((8, D), F32)
    pending = None
    for l in reversed(range(n_layers)):
        lp = lps[l]
        dz, dmod_pre, dmod_post, g, got = _layer_bwd(dz, saves[l], mods[l], lp, cos, sin, m_ctx, tile, s5_chunk, l, pending, gdt,
                                                     send_early=shards is not None and l == 0)
        if pending is not None:
            received[l + 1] = got
        dmod = jnp.concatenate([dmod_pre.reshape(2, 3 * D)[:, :2 * D], dmod_post.reshape(2, 3 * D)[:, 2 * D:]], axis=1)
        dmod8 = jnp.pad(dmod, ((0, 6), (0, 0)))
        dcs, g["b_mod"] = whole_bwd(_f_mod_sum, f"mod_b_l{l}", [cs, lp["b_mod"], lp["w_mod"]], [dmod8, dcs], [True, True, False])
        if shards is None:
            g["w_mod"] = jnp.concatenate(mm_tn(silus[l], [dmod8[:, :D], dmod8[:, D:2 * D], dmod8[:, 2 * D:]], f"mod_dw_l{l}", tm=8), axis=1)
        else:
            g["mod_s"], g["mod_d"] = silus[l][:2], dmod
        grads[l] = g
        if shards is not None:
            pending = _layer_sends(g)
    if shards is not None:
        got_in, got_small = exchange([pending[0], _small_sends(dcs[1], grads)], [True, False], "exchange_grads_l0")
        received[0] = [got_in] + list(grads[0]["early"]) + [got_small]
    return loss, dz[m_ctx:], dcs[1], grads, received


_WEIGHTS = ["c_ctx", "w_mod", "b_mod", "g_pre", "g_post", "w_in", "w_out", "gla_w_gate", "gla_b_gate", "gla_g_norm", "na_rpb",
            "s5_lam_re", "s5_lam_im", "s5_log_dt", "s5_b_re", "s5_b_im", "s5_c_re", "s5_c_im", "s5_d", "s5_w_glu", "s5_b_glu",
            "pool_w", "pool_scale"]
_INPUTS = ["x", "c", "ctx"] + _WEIGHTS + ["loss_target"] + ["m_" + n for n in _WEIGHTS] + ["v_" + n for n in _WEIGHTS]
_SHARDED = ["w_mod", "w_in", "w_out", "s5_w_glu"]
_BY_COLS = {"w_mod": True, "w_in": True, "w_out": False, "s5_w_glu": False}
_GRAD_SHARDED = ["w_in", "w_out", "s5_w_glu"]
_SMALL = [n for n in _WEIGHTS if n not in _SHARDED]
_SMALL_PER_LAYER = [n for n in _SMALL if n != "c_ctx"]
_PACK_ROWS = 256


def _pack_plan(like):
    tiled = [i for i, a in enumerate(like) if a.size % 1024 == 0]
    loose = [i for i, a in enumerate(like) if a.size % 1024 != 0]
    tail = -(-sum(like[i].size for i in loose) // 1024) * 8
    rows = sum(like[i].size // 128 for i in tiled) + tail
    return tiled, loose, tail, -(-rows // _PACK_ROWS) * _PACK_ROWS - rows


def _pack_rows(like, index):
    tiled, _, _, _ = _pack_plan(like)
    row = 0
    for i in tiled:
        n = like[i].size // 128
        if i == index:
            return row, row + n
        row += n
    raise ValueError("not a tile-aligned entry")


def _pack(arrs):
    tiled, loose, tail, fill = _pack_plan(arrs)
    dt = arrs[0].dtype
    flat = jnp.concatenate([arrs[i].reshape(-1) for i in loose])
    flat = jnp.pad(flat, (0, tail * 128 - flat.shape[0])).reshape(tail, 128)
    return jnp.concatenate([arrs[i].reshape(-1, 128) for i in tiled] + [flat, jnp.zeros((fill, 128), dt)], axis=0)


def _unpack(packed, like):
    tiled, loose, tail, _ = _pack_plan(like)
    out, row = [None] * len(like), 0
    for i in tiled:
        n = like[i].size // 128
        out[i] = packed[row:row + n].reshape(like[i].shape)
        row += n
    flat, pos = packed[row:row + tail].reshape(-1), 0
    for i in loose:
        out[i] = flat[pos:pos + like[i].size].reshape(like[i].shape)
        pos += like[i].size
    return out


def _gathered(g, cols):
    if cols:
        return g.transpose(1, 0, 2).reshape(g.shape[1], N_DEV * g.shape[2])
    return g.reshape(N_DEV * g.shape[1], g.shape[2])


def _slabs(w, cols):
    r, c = w.shape
    if cols:
        return w.reshape(r, N_DEV, c // N_DEV).transpose(1, 0, 2)
    return w.reshape(N_DEV, r // N_DEV, c)


def _layer_small(g):
    s5 = lambda i, f: jnp.stack([f(g["s5"][d][i]) for d in range(2)])
    return {
        "b_mod": g["b_mod"].reshape(3 * D), "g_pre": g["g_pre"].reshape(D), "g_post": g["g_post"].reshape(D),
        "gla_w_gate": jnp.stack([g["wg"][0:16, 0:128], g["wg"][16:32, 128:256]]),
        "gla_b_gate": g["bg"].reshape(2, 128), "gla_g_norm": g["g_norm"][0], "na_rpb": g["rpb"],
        "s5_lam_re": s5(0, lambda a: a), "s5_lam_im": s5(1, lambda a: a), "s5_log_dt": s5(2, lambda a: a.reshape(16)),
        "s5_b_re": s5(3, lambda a: a.reshape(16, 16, 64).transpose(0, 2, 1)),
        "s5_b_im": s5(4, lambda a: a.reshape(16, 16, 64).transpose(0, 2, 1)),
        "s5_c_re": s5(5, lambda a: a.reshape(16, 16, 64)), "s5_c_im": s5(6, lambda a: a.reshape(16, 16, 64)),
        "s5_d": g["s5_d"].reshape(256), "s5_b_glu": g["b_glu"].reshape(256),
        "pool_w": jnp.stack([g["wpool"][64 * i:64 * i + 64, 64 * i:64 * i + 64] for i in range(4)]),
        "pool_scale": g["pool_scale"].reshape(256),
    }


def _layer_sends(g):
    big = {"w_in": g["w_in"], "w_out": g["w_out"], "s5_w_glu": g["w_glu"]}
    return [_slabs(big[n], _BY_COLS[n]).astype(BF16) for n in _GRAD_SHARDED]


def _small_sends(d_c_ctx, grads):
    per_layer = [_layer_small(g) for g in grads]
    full = {n: jnp.stack([s[n] for s in per_layer]) for n in _SMALL_PER_LAYER}
    full["c_ctx"] = d_c_ctx
    factors = [jnp.stack([g["mod_s"] for g in grads]), jnp.stack([g["mod_d"] for g in grads])]
    return _pack([full[n] for n in _SMALL] + factors).astype(BF16)


def kernel(x, c, ctx, c_ctx, w_mod, b_mod, g_pre, g_post, w_in, w_out, gla_w_gate, gla_b_gate, gla_g_norm, na_rpb, s5_lam_re, s5_lam_im, s5_log_dt, s5_b_re, s5_b_im, s5_c_re, s5_c_im, s5_d, s5_w_glu, s5_b_glu, pool_w, pool_scale, loss_target, m_c_ctx, m_w_mod, m_b_mod, m_g_pre, m_g_post, m_w_in, m_w_out, m_gla_w_gate, m_gla_b_gate, m_gla_g_norm, m_na_rpb, m_s5_lam_re, m_s5_lam_im, m_s5_log_dt, m_s5_b_re, m_s5_b_im, m_s5_c_re, m_s5_c_im, m_s5_d, m_s5_w_glu, m_s5_b_glu, m_pool_w, m_pool_scale, v_c_ctx, v_w_mod, v_b_mod, v_g_pre, v_g_post, v_w_in, v_w_out, v_gla_w_gate, v_gla_b_gate, v_gla_g_norm, v_na_rpb, v_s5_lam_re, v_s5_lam_im, v_s5_log_dt, v_s5_b_re, v_s5_b_im, v_s5_c_re, v_s5_c_im, v_s5_d, v_s5_w_glu, v_s5_b_glu, v_pool_w, v_pool_scale):
    given = dict(zip(_INPUTS, (x, c, ctx, c_ctx, w_mod, b_mod, g_pre, g_post, w_in, w_out, gla_w_gate, gla_b_gate, gla_g_norm, na_rpb, s5_lam_re, s5_lam_im, s5_log_dt, s5_b_re, s5_b_im, s5_c_re, s5_c_im, s5_d, s5_w_glu, s5_b_glu, pool_w, pool_scale, loss_target, m_c_ctx, m_w_mod, m_b_mod, m_g_pre, m_g_post, m_w_in, m_w_out, m_gla_w_gate, m_gla_b_gate, m_gla_g_norm, m_na_rpb, m_s5_lam_re, m_s5_lam_im, m_s5_log_dt, m_s5_b_re, m_s5_b_im, m_s5_c_re, m_s5_c_im, m_s5_d, m_s5_w_glu, m_s5_b_glu, m_pool_w, m_pool_scale, v_c_ctx, v_w_mod, v_b_mod, v_g_pre, v_g_post, v_w_in, v_w_out, v_gla_w_gate, v_gla_b_gate, v_gla_g_norm, v_na_rpb, v_s5_lam_re, v_s5_lam_im, v_s5_log_dt, v_s5_b_re, v_s5_b_im, v_s5_c_re, v_s5_c_im, v_s5_d, v_s5_w_glu, v_s5_b_glu, v_pool_w, v_pool_scale)))
    n_layers = w_in.shape[0]
    shards = [[given[n][l].astype(BF16) for n in _SHARDED] for l in range(n_layers)]
    p = {n: given[n] for n in _SMALL}
    loss, grad_x, _, _, received = local_step(x[0], c, ctx[0], loss_target[0], p, shards)
    final = {}
    for n in _GRAD_SHARDED:
        per_layer = [adamw(received[l][_GRAD_SHARDED.index(n)], given[n][l], given["m_" + n][l], given["v_" + n][l], f"adamw_{n}_l{l}")
                     for l in range(n_layers)]
        final[n] = [jnp.stack([res[kind] for res in per_layer]) for kind in range(4)]
    factor_like = [jnp.zeros((n_layers, 2, D), F32), jnp.zeros((n_layers, 2, 3 * D), F32)]
    like = [given[n] for n in _SMALL] + factor_like
    small_recv = received[0][-1]
    rows_s, rows_d = _pack_rows(like, len(_SMALL)), _pack_rows(like, len(_SMALL) + 1)
    fac_s = small_recv[:, rows_s[0]:rows_s[1]].reshape(N_DEV, n_layers, 2, D)
    fac_d = small_recv[:, rows_d[0]:rows_d[1]].reshape(N_DEV, n_layers, 2, 3 * D)
    me = 4 * lax.axis_index("x") + 2 * lax.axis_index("y") + lax.axis_index("c")
    cols = w_mod.shape[2]
    per_layer = []
    for l in range(n_layers):
        s_all = fac_s[:, l].reshape(2 * N_DEV, D)
        d_mine = lax.dynamic_slice_in_dim(fac_d[:, l].reshape(2 * N_DEV, 3 * D), me * cols, cols, axis=1)
        (g_mod,) = mm_tn(s_all, [d_mine], f"mod_dw_l{l}", tm=2 * N_DEV, tn=cols)
        per_layer.append(adamw(g_mod[None], given["w_mod"][l], given["m_w_mod"][l], given["v_w_mod"][l], f"adamw_w_mod_l{l}"))
    final["w_mod"] = [jnp.stack([res[kind] for res in per_layer]) for kind in range(4)]
    res = adamw(small_recv, _pack(like), _pack([given["m_" + n] for n in _SMALL] + factor_like),
                _pack([given["v_" + n] for n in _SMALL] + factor_like), "adamw_small")
    unpacked = [_unpack(packed, like) for packed in res]
    for i, n in enumerate(_SMALL):
        final[n] = [unpacked[kind][i] for kind in range(4)]
    loss = lax.psum(loss, ("x", "y", "c"))
    return (loss, grad_x[None], *[final[n][0] for n in _WEIGHTS], *[final[n][1] for n in _WEIGHTS],
            *[final[n][2] for n in _WEIGHTS], *[final[n][3] for n in _WEIGHTS])
```

```python
import functools
import math

import numpy as np
import jax
import jax.numpy as jnp
from jax import lax
from jax.experimental import pallas as pl
from jax.experimental.pallas import tpu as pltpu

F32 = jnp.float32
BF16 = jnp.bfloat16
HIGHEST = lax.Precision.HIGHEST
HIGH = lax.Precision.HIGH

D = 1024
GRID_W = 64
EPS = 1e-6
N_DEV = 8
C_GT, C_GV, C_NK, C_NV, C_SU, C_NQ, C_PU, C_GK, C_GG, C_GQ, C_END = 0, 1024, 1280, 1536, 1792, 2048, 2304, 2560, 2688, 2816, 2944
PW = 3072
N_CTX_ORIG = 416
N_IN = 2848
GLA_CHUNK = 128
S5_CHUNK = 256
ROW_TILE = 256
VMEM_LIMIT = 56 * 1024 * 1024

ADAM_LR, ADAM_B1, ADAM_B2, ADAM_EPS, ADAM_WD, ADAM_STEP = 0.001, 0.9, 0.999, 1e-08, 0.01, 10


def _cparams(**kw):
    return pltpu.CompilerParams(vmem_limit_bytes=VMEM_LIMIT, **kw)


def _dg(a, b, ca, cb, precision=None):
    return lax.dot_general(a, b, (((ca,), (cb,)), ((), ())), precision=precision, preferred_element_type=F32)


def hdot(a, b):
    return _dg(a, b, 1, 0, HIGHEST)


def hdot_nt(a, b):
    return _dg(a, b, 1, 1, HIGHEST)


def hdot_tn(a, b):
    return _dg(a, b, 0, 0, HIGHEST)


def mdot(a, b):
    return _dg(a, b, 1, 0, HIGH)


def mdot_nt(a, b):
    return _dg(a, b, 1, 1, HIGH)


def mdot_tn(a, b):
    return _dg(a, b, 0, 0, HIGH)


def b_nn(a, b):
    return _dg(a.astype(BF16), b.astype(BF16), 1, 0)


def b_nt(a, b):
    return _dg(a.astype(BF16), b.astype(BF16), 1, 1)


def b_tn(a, b):
    return _dg(a.astype(BF16), b.astype(BF16), 0, 0)


@jax.custom_vjp
def bdot(a, b):
    return b_nn(a, b)


def _bdot_fwd(a, b):
    return b_nn(a, b), (a, b)


def _bdot_bwd(res, ct):
    a, b = res
    return b_nt(ct, b).astype(a.dtype), b_tn(a, ct).astype(b.dtype)


bdot.defvjp(_bdot_fwd, _bdot_bwd)


def _log_sigmoid(z):
    return jnp.minimum(z, 0.0) - jnp.log(1.0 + jnp.exp(-jnp.abs(z)))


def _silu(z):
    return z * jax.nn.sigmoid(z)


def _gelu(z):
    return 0.5 * z * (1.0 + jnp.tanh(math.sqrt(2.0 / math.pi) * (z + 0.044715 * (z * z * z))))


def _cat(vals):
    return vals[0] if len(vals) == 1 else jnp.concatenate(vals, axis=-1)


def mm_nn(a_parts, b, name, tm=ROW_TILE, tn=1024):
    t = a_parts[0].shape[0]
    k, n = b.shape
    na = len(a_parts)
    tn = min(tn, n)

    def body(*refs):
        a = _cat([r[...].astype(BF16) for r in refs[:na]])
        refs[na + 1][...] = _dg(a, refs[na][...].astype(BF16), 1, 0)

    return pl.pallas_call(
        body, name=name, grid=(n // tn, t // tm),
        in_specs=[pl.BlockSpec((tm, p.shape[1]), lambda j, i: (i, 0)) for p in a_parts]
        + [pl.BlockSpec((k, tn), lambda j, i: (0, j))],
        out_specs=pl.BlockSpec((tm, tn), lambda j, i: (i, j)),
        out_shape=jax.ShapeDtypeStruct((t, n), F32),
        compiler_params=_cparams(dimension_semantics=("arbitrary", "arbitrary")),
    )(*a_parts, b)


def mm_nn_cols(a, b, start, widths, name, tm=ROW_TILE):
    t, k = a.shape
    tn = 1024
    assert start % tn == 0 and sum(widths) <= tn

    def body(a_ref, b_ref, *o_refs):
        r = _dg(a_ref[...].astype(BF16), b_ref[...].astype(BF16), 1, 0)
        off = 0
        for o_ref, w in zip(o_refs, widths):
            o_ref[...] = r[:, off:off + w]
            off += w

    return pl.pallas_call(
        body, name=name, grid=(t // tm,),
        in_specs=[pl.BlockSpec((tm, k), lambda i: (i, 0)), pl.BlockSpec((k, tn), lambda i: (0, start // tn))],
        out_specs=[pl.BlockSpec((tm, w), lambda i: (i, 0)) for w in widths],
        out_shape=[jax.ShapeDtypeStruct((t, w), F32) for w in widths],
        compiler_params=_cparams(dimension_semantics=("arbitrary",)),
    )(a, b)


def mm_nt(a_parts, b, name, tm=ROW_TILE):
    t = a_parts[0].shape[0]
    n, k = b.shape
    na = len(a_parts)

    def body(*refs):
        a = _cat([r[...].astype(BF16) for r in refs[:na]])
        refs[na + 1][...] = _dg(a, refs[na][...].astype(BF16), 1, 1)

    return pl.pallas_call(
        body, name=name, grid=(t // tm,),
        in_specs=[pl.BlockSpec((tm, p.shape[1]), lambda i: (i, 0)) for p in a_parts]
        + [pl.BlockSpec((n, k), lambda i: (0, 0))],
        out_specs=pl.BlockSpec((tm, n), lambda i: (i, 0)),
        out_shape=jax.ShapeDtypeStruct((t, n), F32),
        compiler_params=_cparams(dimension_semantics=("arbitrary",)),
    )(*a_parts, b)


def mm_tn(a, b_parts, name, tm=ROW_TILE, tn=1024, out_dtype=F32):
    t, k = a.shape
    widths = [p.shape[1] for p in b_parts]
    n = sum(widths)
    assert n % tn == 0
    groups, cur, acc = [], [], 0
    for idx, w in enumerate(widths):
        cur.append(idx)
        acc += w
        if acc == tn:
            groups.append(cur)
            cur, acc = [], 0
        assert acc < tn
    assert not cur
    outs = []
    for gi, grp in enumerate(groups):
        parts = [b_parts[i] for i in grp]
        npart = len(parts)
        nsteps = t // tm

        def body(*refs, npart=npart, nsteps=nsteps):
            a_v = refs[0][...].astype(BF16)
            b_v = _cat([r[...].astype(BF16) for r in refs[1:1 + npart]])
            o_ref, acc_ref = refs[1 + npart], refs[2 + npart]
            r = _dg(a_v, b_v, 0, 0)

            @pl.when(pl.program_id(0) == 0)
            def _():
                acc_ref[...] = r

            @pl.when(pl.program_id(0) != 0)
            def _():
                acc_ref[...] += r

            @pl.when(pl.program_id(0) == nsteps - 1)
            def _():
                o_ref[...] = acc_ref[...].astype(o_ref.dtype)

        outs.append(pl.pallas_call(
            body, name=f"{name}_{gi}", grid=(nsteps,),
            in_specs=[pl.BlockSpec((tm, k), lambda i: (i, 0))]
            + [pl.BlockSpec((tm, p.shape[1]), lambda i: (i, 0)) for p in parts],
            out_specs=pl.BlockSpec((k, tn), lambda i: (0, 0)),
            out_shape=jax.ShapeDtypeStruct((k, tn), out_dtype),
            scratch_shapes=[pltpu.VMEM((k, tn), F32)],
            compiler_params=_cparams(dimension_semantics=("arbitrary",)),
        )(a, *parts))
    return outs


def _seg_of(i, nct):
    return jnp.where(i < nct, 1, 0)


def rowwise_fwd(fn, name, rows, segs, globs, out_widths, tile, nct):
    t = rows[0].shape[0]
    nr, ns, ng = len(rows), len(segs), len(globs)

    def body(*refs):
        vals = [r[...] for r in refs[:nr]] + [r[0] for r in refs[nr:nr + ns]] + [r[...] for r in refs[nr + ns:nr + ns + ng]]
        outs = fn(*vals)
        for o_ref, o in zip(refs[nr + ns + ng:], outs):
            o_ref[...] = o

    return pl.pallas_call(
        body, name=name, grid=(t // tile,),
        in_specs=[pl.BlockSpec((tile, r.shape[1]), lambda i: (i, 0)) for r in rows]
        + [pl.BlockSpec((1, 1, s.shape[2]), lambda i: (_seg_of(i, nct), 0, 0)) for s in segs]
        + [pl.BlockSpec(g.shape, lambda i: (0, 0)) for g in globs],
        out_specs=[pl.BlockSpec((tile, w), lambda i: (i, 0)) for w in out_widths],
        out_shape=[jax.ShapeDtypeStruct((t, w), F32) for w in out_widths],
        compiler_params=_cparams(dimension_semantics=("arbitrary",)),
    )(*rows, *segs, *globs)


def rowwise_bwd(fn, name, rows, segs, globs, cts, tile, nct, row_diff, glob_diff):
    t = rows[0].shape[0]
    nr, ns, ng, nc = len(rows), len(segs), len(globs), len(cts)
    d_rows = [i for i in range(nr) if row_diff[i]]
    d_globs = [i for i in range(ng) if glob_diff[i]]

    def body(*refs):
        in_refs, out_refs = refs[:nr + ns + ng + nc], refs[nr + ns + ng + nc:]
        row_v = [r[...] for r in in_refs[:nr]]
        seg_v = [r[0] for r in in_refs[nr:nr + ns]]
        glob_v = [r[...] for r in in_refs[nr + ns:nr + ns + ng]]
        ct_v = tuple(r[...] for r in in_refs[nr + ns + ng:])

        def wrapped(dr, sv, dg):
            rv = list(row_v)
            for j, i in enumerate(d_rows):
                rv[i] = dr[j]
            gv = list(glob_v)
            for j, i in enumerate(d_globs):
                gv[i] = dg[j]
            return tuple(fn(*rv, *sv, *gv))

        _, vjp = jax.vjp(wrapped, [row_v[i] for i in d_rows], seg_v, [glob_v[i] for i in d_globs])
        c_rows, c_segs, c_globs = vjp(ct_v)
        i = pl.program_id(0)
        k = 0
        for c in c_rows:
            out_refs[k][...] = c
            k += 1
        seg_first = jnp.logical_or(i == 0, i == nct)
        for c in c_segs:
            ref = out_refs[k]
            k += 1

            @pl.when(seg_first)
            def _(ref=ref, c=c):
                ref[0] = c

            @pl.when(jnp.logical_not(seg_first))
            def _(ref=ref, c=c):
                ref[0] += c
        for c in c_globs:
            ref = out_refs[k]
            k += 1

            @pl.when(i == 0)
            def _(ref=ref, c=c):
                ref[...] = c

            @pl.when(i != 0)
            def _(ref=ref, c=c):
                ref[...] += c

    return pl.pallas_call(
        body, name=name, grid=(t // tile,),
        in_specs=[pl.BlockSpec((tile, r.shape[1]), lambda i: (i, 0)) for r in rows]
        + [pl.BlockSpec((1, 1, s.shape[2]), lambda i: (_seg_of(i, nct), 0, 0)) for s in segs]
        + [pl.BlockSpec(g.shape, lambda i: (0, 0)) for g in globs]
        + [pl.BlockSpec((tile, c.shape[1]), lambda i: (i, 0)) for c in cts],
        out_specs=[pl.BlockSpec((tile, rows[i].shape[1]), lambda i: (i, 0)) for i in d_rows]
        + [pl.BlockSpec((1, 1, s.shape[2]), lambda i: (_seg_of(i, nct), 0, 0)) for s in segs]
        + [pl.BlockSpec(globs[i].shape, lambda i: (0, 0)) for i in d_globs],
        out_shape=[jax.ShapeDtypeStruct(rows[i].shape, F32) for i in d_rows]
        + [jax.ShapeDtypeStruct(s.shape, F32) for s in segs]
        + [jax.ShapeDtypeStruct(globs[i].shape, F32) for i in d_globs],
        compiler_params=_cparams(dimension_semantics=("arbitrary",)),
    )(*rows, *segs, *globs, *cts)


def f_pre(x, mod, g_pre):
    shift, scale = mod[:, :D], mod[:, D:2 * D]
    rs = lax.rsqrt(jnp.mean(x * x, axis=-1, keepdims=True) + EPS)
    return ((x * rs) * g_pre * (1.0 + scale) + shift,)


def f_post(x, out, mod, g_post):
    gate = mod[:, 2 * D:]
    rs = lax.rsqrt(jnp.mean(out * out, axis=-1, keepdims=True) + EPS)
    return (x + gate * ((out * rs) * g_post),)


def f_mix(o_gla, o_na, y5, u5, pm, gcols, g_norm, s5_d, w_glu, b_glu, wpool, pool_scale, havg, e4):
    ms = mdot(o_gla * o_gla, havg)
    y_gla = o_gla * lax.rsqrt(ms + EPS) * jnp.sum(hdot(g_norm, e4), axis=0, keepdims=True)
    g = _gelu(u5 * s5_d + y5)
    y_s5 = g * jax.nn.sigmoid(bdot(g, w_glu) + b_glu)
    y_pool = bdot(pm, wpool) * pool_scale
    ycat = jnp.concatenate([y_gla, o_na, y_s5, y_pool], axis=-1)
    return (ycat * _silu(gcols),)


@jax.custom_vjp
def _rot_half16(x):
    lane = lax.broadcasted_iota(jnp.int32, x.shape, 1)
    first = jnp.bitwise_and(lane, 15) < 8
    return jnp.where(first, -pltpu.roll(x, x.shape[1] - 8, 1), pltpu.roll(x, 8, 1))


def _rot_fwd(x):
    return _rot_half16(x), None


def _rot_bwd(_, ct):
    return (-_rot_half16(ct),)


_rot_half16.defvjp(_rot_fwd, _rot_bwd)


def f_gla_prep(pk, pg, pq, cos, sin, wg, bg):
    z = bdot(pg, wg) + bg
    lg = _log_sigmoid(z) * (1.0 / 16.0)
    k_r = pk * cos + _rot_half16(pk) * sin
    q_r = (pq * cos + _rot_half16(pq) * sin) * (32.0 ** -0.5)
    return q_r, k_r, lg[:, :128], lg[:, 128:]


def _gla_consts(rev):
    c = GLA_CHUNK
    i = np.arange(c)
    inc = (i[None, :] >= i[:, None]) if rev else (i[None, :] <= i[:, None])
    mq = np.stack([(np.arange(128) // 32 == h) for h in range(4)]).astype(np.float32).reshape(4, 1, 128)
    mv = np.stack([(np.arange(256) // 64 == h) for h in range(4)]).astype(np.float32).reshape(4, 1, 256)
    bdt = (np.arange(256)[:, None] // 64 == np.arange(128)[None, :] // 32).astype(np.float32)
    inc = inc.astype(np.float32)
    return jnp.asarray(inc), jnp.asarray(inc.T.copy()), jnp.asarray(mq), jnp.asarray(mv), jnp.asarray(bdt)


def _stack_heads(x, m_ref):
    return jnp.concatenate([x * m_ref[h] for h in range(4)], axis=0)


def _tile4(m):
    return jnp.concatenate([m, m, m, m], axis=0)


def _fold_heads(r4, m_ref):
    r = r4.shape[0] // 4
    out = m_ref[0] * r4[0:r]
    for h in range(1, 4):
        out = out + m_ref[h] * r4[h * r:(h + 1) * r]
    return out


def _gla_chunk_of(s, n_ctx_chunks, n_chunks, rev):
    if not rev:
        return s
    return jnp.where(s < n_ctx_chunks, n_ctx_chunks - 1 - s, n_ctx_chunks + n_chunks - 1 - s)


def gla_scan_fwd(q, k, v, lg, acc, n_ctx_rows, rev, name, comm=None):
    t = q.shape[0]
    nch, ncc = t // GLA_CHUNK, n_ctx_rows // GLA_CHUNK
    inc, inc_t, mq, mv, bdt = _gla_consts(rev)

    def body(q_ref, k_ref, v_ref, lg_ref, acc_ref, inc_ref, inct_ref, mq_ref, mv_ref, bdt_ref, o_ref, st_ref):
        lmask, lmask_t = inc_ref[...], inct_ref[...]
        bd = bdt_ref[...]

        def step(s, st):
            c = _gla_chunk_of(s, ncc, nch, rev)
            rows = pl.ds(pl.multiple_of(c * GLA_CHUNK, GLA_CHUNK), GLA_CHUNK)
            qc, kc, vc, lgc = q_ref[rows, :], k_ref[rows, :], v_ref[rows, :], lg_ref[rows, :]
            st_ref[c] = st
            b = hdot(lmask, lgc)
            blast = jnp.sum(lgc, axis=0, keepdims=True)
            qe, ke, kd = qc * jnp.exp(b), kc * jnp.exp(-b), kc * jnp.exp(blast - b)
            ke4, v4 = _stack_heads(ke, mq_ref), _stack_heads(vc, mv_ref)
            at = _tile4(lmask_t) * b_nt(ke4, qe)
            o_ref[rows, :] = acc_ref[rows, :] + b_nt(qe, st) + b_tn(at, v4)
            return st * jnp.exp(blast) + bd * mdot_tn(vc, kd)

        lax.fori_loop(0, nch, step, jnp.zeros((256, 128), F32))

    return _call_with_exchange(body, name, [q, k, v, lg, acc, inc, inc_t, mq, mv, bdt],
                               [jax.ShapeDtypeStruct((t, 256), F32), jax.ShapeDtypeStruct((nch, 256, 128), F32)], comm)


def gla_scan_bwd(q, k, v, lg, st, do, acc, n_ctx_rows, rev, name, comm=None):
    t = q.shape[0]
    nch, ncc = t // GLA_CHUNK, n_ctx_rows // GLA_CHUNK
    inc, inc_t, mq, mv, bdt = _gla_consts(rev)

    def body(q_ref, k_ref, v_ref, lg_ref, st_ref, do_ref, aq_ref, ak_ref, av_ref, inc_ref, inct_ref, mq_ref, mv_ref, bdt_ref,
             dq_ref, dk_ref, dv_ref, dlg_ref):
        lmask, lmask_t = inc_ref[...], inct_ref[...]
        bd = bdt_ref[...]

        def step(j, carry):
            dst, gsum = carry
            s = nch - 1 - j
            c = _gla_chunk_of(s, ncc, nch, rev)
            rows = pl.ds(pl.multiple_of(c * GLA_CHUNK, GLA_CHUNK), GLA_CHUNK)
            qc, kc, vc, lgc, doc = q_ref[rows, :], k_ref[rows, :], v_ref[rows, :], lg_ref[rows, :], do_ref[rows, :]
            stc = st_ref[c]
            b = hdot(lmask, lgc)
            blast = jnp.sum(lgc, axis=0, keepdims=True)
            eb, enb, edb = jnp.exp(b), jnp.exp(-b), jnp.exp(blast - b)
            qe, ke, kd = qc * eb, kc * enb, kc * edb
            ke4, v4 = _stack_heads(ke, mq_ref), _stack_heads(vc, mv_ref)
            lm4 = _tile4(lmask_t)
            at = lm4 * b_nt(ke4, qe)
            dat = lm4 * mdot_nt(v4, doc)
            dqe = mdot(doc, stc) + mdot_tn(dat, ke4)
            dke = _fold_heads(mdot(dat, qe), mq_ref)
            dv = b_nt(kd, dst) + _fold_heads(b_nn(at, doc), mv_ref)
            dkd = mdot(vc, dst)
            dq = dqe * eb
            dk = dke * enb + dkd * edb
            g = qc * dq - kc * dk
            dlg_ref[rows, :] = hdot_tn(lmask, g) + gsum
            dq_ref[rows, :] = aq_ref[rows, :] + dq
            dk_ref[rows, :] = ak_ref[rows, :] + dk
            dv_ref[rows, :] = av_ref[rows, :] + dv
            dst_new = dst * jnp.exp(blast) + bd * mdot_tn(doc, qe)
            return dst_new, gsum + jnp.sum(g, axis=0, keepdims=True)

        lax.fori_loop(0, nch, step, (jnp.zeros((256, 128), F32), jnp.zeros((1, 128), F32)))

    return _call_with_exchange(body, name, [q, k, v, lg, st, do, *acc, inc, inc_t, mq, mv, bdt],
                               [jax.ShapeDtypeStruct((t, 128), F32), jax.ShapeDtypeStruct((t, 128), F32),
                                jax.ShapeDtypeStruct((t, 256), F32), jax.ShapeDtypeStruct((t, 128), F32)], comm)


def whole_fwd(fn, name, args, out_shapes):
    def body(*refs):
        outs = fn(*[r[...] for r in refs[:len(args)]])
        for o_ref, o in zip(refs[len(args):], outs):
            o_ref[...] = o

    vm = pl.BlockSpec(memory_space=pltpu.VMEM)
    return pl.pallas_call(
        body, name=name, in_specs=[vm] * len(args), out_specs=[vm] * len(out_shapes),
        out_shape=[jax.ShapeDtypeStruct(s, F32) for s in out_shapes], compiler_params=_cparams(),
    )(*args)


def whole_bwd(fn, name, args, cts, diff):
    d_idx = [i for i in range(len(args)) if diff[i]]

    def body(*refs):
        vals = [r[...] for r in refs[:len(args)]]
        ct_v = tuple(r[...] for r in refs[len(args):len(args) + len(cts)])

        def wrapped(dv):
            av = list(vals)
            for j, i in enumerate(d_idx):
                av[i] = dv[j]
            return tuple(fn(*av))

        _, vjp = jax.vjp(wrapped, [vals[i] for i in d_idx])
        (c_args,) = vjp(ct_v)
        for o_ref, c in zip(refs[len(args) + len(cts):], c_args):
            o_ref[...] = c

    vm = pl.BlockSpec(memory_space=pltpu.VMEM)
    return pl.pallas_call(
        body, name=name, in_specs=[vm] * (len(args) + len(cts)), out_specs=[vm] * len(d_idx),
        out_shape=[jax.ShapeDtypeStruct(args[i].shape, F32) for i in d_idx], compiler_params=_cparams(),
    )(*args, *cts)


def _s5_consts():
    e_rep = (np.arange(256)[:, None] // 16 == np.arange(16)[None, :]).astype(np.float32)
    e_tile = (np.arange(64)[:, None] == np.arange(1024)[None, :] % 64).astype(np.float32)
    gmask = (np.arange(16)[:, None] == np.arange(1024)[None, :] // 64).astype(np.float32)
    bdm = (np.arange(256)[:, None] // 16 == np.arange(1024)[None, :] // 64).astype(np.float32)
    return jnp.asarray(e_rep), jnp.asarray(e_tile), jnp.asarray(gmask), jnp.asarray(bdm)


def f_s5_params(lam_re, lam_im, log_dt, bt_re, bt_im, ct_re, ct_im, e_rep, e_tile, gmask, bdm):
    dt = jnp.exp(log_dt)
    mag = jnp.exp(lam_re * dt)
    ang = lam_im * dt
    lb_re, lb_im = mag * jnp.cos(ang), mag * jnp.sin(ang)
    num_re, num_im = lb_re - 1.0, lb_im
    den = lam_re * lam_re + lam_im * lam_im
    coef_re = (num_re * lam_re + num_im * lam_im) / den
    coef_im = (num_im * lam_re - num_re * lam_im) / den
    cr, ci = hdot(e_rep, coef_re), hdot(e_rep, coef_im)
    bbt_re = cr * bt_re - ci * bt_im
    bbt_im = cr * bt_im + ci * bt_re
    a_re = jnp.sum(hdot(lb_re, e_tile) * gmask, axis=0, keepdims=True)
    a_im = jnp.sum(hdot(lb_im, e_tile) * gmask, axis=0, keepdims=True)
    return (a_re, a_im, hdot(bbt_re, e_tile) * bdm, hdot(bbt_im, e_tile) * bdm,
            hdot(ct_re, e_tile) * bdm, hdot(ct_im, e_tile) * bdm)


def _s5_doubling(xr, xi, pr, pi, pos, n, steps, rev):
    rows = xr.shape[0]
    for s in steps:
        if rev:
            keep = pos < (n - s)
            sr, si = pltpu.roll(xr, rows - s, 0), pltpu.roll(xi, rows - s, 0)
        else:
            keep = pos >= s
            sr, si = pltpu.roll(xr, s, 0), pltpu.roll(xi, s, 0)
        sr, si = jnp.where(keep, sr, 0.0), jnp.where(keep, si, 0.0)
        xr, xi = xr + pr * sr - pi * si, xi + pr * si + pi * sr
        pr, pi = pr * pr - pi * pi, 2.0 * pr * pi
    return xr, xi, pr, pi


SUBLANES = 8


def _s5_scan(xr, xi, a_re, a_im, rev, chunk, scr):
    xs_r, xs_i, yp_r, yp_i = scr
    ng = chunk // SUBLANES
    x3r, x3i = xr.reshape(ng, SUBLANES, 1024), xi.reshape(ng, SUBLANES, 1024)
    sub = lax.broadcasted_iota(jnp.int32, (SUBLANES, 1024), 0)
    a8r, a8i = a_re, a_im
    for s in (1, 2, 4):
        keep = sub < (SUBLANES - s) if rev else sub >= s
        mr, mi = jnp.where(keep, a8r, 0.0)[None], jnp.where(keep, a8i, 0.0)[None]
        shift = SUBLANES - s if rev else s
        sr, si = pltpu.roll(x3r, shift, 1), pltpu.roll(x3i, shift, 1)
        x3r, x3i = x3r + mr * sr - mi * si, x3i + mr * si + mi * sr
        a8r, a8i = a8r * a8r - a8i * a8i, 2.0 * a8r * a8i
    xr, xi = x3r.reshape(chunk, 1024), x3i.reshape(chunk, 1024)
    nblk = 1024 // 128
    for j in range(nblk):
        xs_r[j] = xr[:, 128 * j:128 * (j + 1)]
        xs_i[j] = xi[:, 128 * j:128 * (j + 1)]
    edge = pl.ds(0 if rev else SUBLANES - 1, ng, stride=SUBLANES)
    gr = jnp.concatenate([xs_r[j, edge, :] for j in range(nblk)], axis=-1)
    gi = jnp.concatenate([xs_i[j, edge, :] for j in range(nblk)], axis=-1)
    grow = lax.broadcasted_iota(jnp.int32, (ng, 1024), 0)
    steps = tuple(1 << k for k in range((ng - 1).bit_length()))
    gr, gi, _, _ = _s5_doubling(gr, gi, a8r, a8i, grow, ng, steps, rev)
    if rev:
        yp_r[...] = jnp.where(grow < ng - 1, pltpu.roll(gr, ng - 1, 0), 0.0)
        yp_i[...] = jnp.where(grow < ng - 1, pltpu.roll(gi, ng - 1, 0), 0.0)
    else:
        yp_r[...] = jnp.where(grow >= 1, pltpu.roll(gr, 1, 0), 0.0)
        yp_i[...] = jnp.where(grow >= 1, pltpu.roll(gi, 1, 0), 0.0)
    sub = lax.broadcasted_iota(jnp.int32, (SUBLANES, 1024), 0)
    tr, ti = jnp.zeros((SUBLANES, 1024), F32), jnp.zeros((SUBLANES, 1024), F32)
    cr, ci = a_re, a_im
    for n in range(1, SUBLANES + 1):
        r = SUBLANES - n if rev else n - 1
        tr, ti = jnp.where(sub == r, cr, tr), jnp.where(sub == r, ci, ti)
        cr, ci = cr * a_re - ci * a_im, cr * a_im + ci * a_re
    for j in range(nblk):
        lanes = slice(128 * j, 128 * (j + 1))
        tr_j, ti_j = tr[:, lanes], ti[:, lanes]
        for g in range(ng):
            rows = slice(g * SUBLANES, (g + 1) * SUBLANES)
            er, ei = yp_r[g:g + 1, lanes], yp_i[g:g + 1, lanes]
            xs_r[j, rows, :] = xs_r[j, rows, :] + tr_j * er - ti_j * ei
            xs_i[j, rows, :] = xs_i[j, rows, :] + tr_j * ei + ti_j * er
    return (jnp.concatenate([xs_r[j] for j in range(nblk)], axis=-1),
            jnp.concatenate([xs_i[j] for j in range(nblk)], axis=-1))


def _s5_scratch(chunk):
    return [pltpu.VMEM((8, chunk, 128), F32), pltpu.VMEM((8, chunk, 128), F32),
            pltpu.VMEM((chunk // SUBLANES, 1024), F32), pltpu.VMEM((chunk // SUBLANES, 1024), F32)]


def _s5_chunk_states(u_c, x0r, x0i, a_re, a_im, bb_re, bb_im, rev, chunk, scr):
    row = lax.broadcasted_iota(jnp.int32, (chunk, 1024), 0)
    first = row == (chunk - 1 if rev else 0)
    inj_r = a_re * x0r - a_im * x0i
    inj_i = a_re * x0i + a_im * x0r
    xr = b_nn(u_c, bb_re) + jnp.where(first, inj_r, 0.0)
    xi = b_nn(u_c, bb_im) + jnp.where(first, inj_i, 0.0)
    return _s5_scan(xr, xi, a_re, a_im, rev, chunk, scr)


def _row_pick(x, idx):
    row = lax.broadcasted_iota(jnp.int32, x.shape, 0)
    return jnp.sum(jnp.where(row == idx, x, 0.0), axis=0, keepdims=True)


def s5_scan_fwd(u, acc, a_re, a_im, bb_re, bb_im, cc_re, cc_im, n_ctx_rows, chunk, rev, name, comm=None):
    t = u.shape[0]
    nch, ncc = t // chunk, n_ctx_rows // chunk

    def body(u_ref, acc_ref, ar_ref, ai_ref, br_ref, bi_ref, cr_ref, ci_ref, y_ref, x0r_ref, x0i_ref, xsr_ref, xsi_ref, *scr):
        a_r, a_i = ar_ref[...], ai_ref[...]

        def step(s, carry):
            x0r, x0i = carry
            c = _gla_chunk_of(s, ncc, nch, rev)
            rows = pl.ds(pl.multiple_of(c * chunk, chunk), chunk)
            x0r_ref[c] = x0r
            x0i_ref[c] = x0i
            xr, xi = _s5_chunk_states(u_ref[rows, :], x0r, x0i, a_r, a_i, br_ref[...], bi_ref[...], rev, chunk, scr)
            y_ref[rows, :] = acc_ref[rows, :] + b_nt(xr, cr_ref[...]) - b_nt(xi, ci_ref[...])
            xsr_ref[rows, :] = xr.astype(BF16)
            xsi_ref[rows, :] = xi.astype(BF16)
            last = 0 if rev else chunk - 1
            return _row_pick(xr, last), _row_pick(xi, last)

        lax.fori_loop(0, nch, step, (jnp.zeros((1, 1024), F32), jnp.zeros((1, 1024), F32)))

    return _call_with_exchange(
        body, name, [u, acc, a_re, a_im, bb_re, bb_im, cc_re, cc_im],
        [jax.ShapeDtypeStruct((t, 256), F32), jax.ShapeDtypeStruct((nch, 1, 1024), F32),
         jax.ShapeDtypeStruct((nch, 1, 1024), F32), jax.ShapeDtypeStruct((t, 1024), BF16),
         jax.ShapeDtypeStruct((t, 1024), BF16)], comm, _s5_scratch(chunk))


def s5_scan_bwd(u, dy, du_acc, x0r, x0i, xsr, xsi, a_re, a_im, bb_re, bb_im, cc_re, cc_im, n_ctx_rows, chunk, rev, name):
    t = u.shape[0]
    nch, ncc = t // chunk, n_ctx_rows // chunk

    def body(u_ref, dy_ref, dua_ref, x0r_ref, x0i_ref, xsr_ref, xsi_ref, ar_ref, ai_ref, br_ref, bi_ref, cr_ref, ci_ref,
             du_ref, dar_ref, dai_ref, dbr_ref, dbi_ref, dcr_ref, dci_ref, *scr):
        a_r, a_i = ar_ref[...], ai_ref[...]
        for ref in (dbr_ref, dbi_ref, dcr_ref, dci_ref):
            ref[...] = jnp.zeros_like(ref)
        row = lax.broadcasted_iota(jnp.int32, (chunk, 1024), 0)
        first_idx, last_idx = (chunk - 1, 0) if rev else (0, chunk - 1)

        def step(j, carry):
            lcr, lci, dar, dai = carry
            s = nch - 1 - j
            c = _gla_chunk_of(s, ncc, nch, rev)
            rows = pl.ds(pl.multiple_of(c * chunk, chunk), chunk)
            u_c, dy_c = u_ref[rows, :], dy_ref[rows, :]
            x0r_c, x0i_c = x0r_ref[c], x0i_ref[c]
            xr, xi = xsr_ref[rows, :].astype(F32), xsi_ref[rows, :].astype(F32)
            dcr_ref[...] += b_tn(dy_c, xr)
            dci_ref[...] -= b_tn(dy_c, xi)
            inj_r = a_r * lcr + a_i * lci
            inj_i = a_r * lci - a_i * lcr
            is_last = row == last_idx
            lr = b_nn(dy_c, cr_ref[...]) + jnp.where(is_last, inj_r, 0.0)
            li = -b_nn(dy_c, ci_ref[...]) + jnp.where(is_last, inj_i, 0.0)
            lr, li = _s5_scan(lr, li, a_r, -a_i, not rev, chunk, scr)
            du_ref[rows, :] = dua_ref[rows, :] + b_nt(lr, br_ref[...]) + b_nt(li, bi_ref[...])
            dbr_ref[...] += b_tn(u_c, lr)
            dbi_ref[...] += b_tn(u_c, li)
            if rev:
                pr, pi = pltpu.roll(xr, chunk - 1, 0), pltpu.roll(xi, chunk - 1, 0)
            else:
                pr, pi = pltpu.roll(xr, 1, 0), pltpu.roll(xi, 1, 0)
            is_first = row == first_idx
            pr, pi = jnp.where(is_first, x0r_c, pr), jnp.where(is_first, x0i_c, pi)
            dar = dar + jnp.sum(lr * pr + li * pi, axis=0, keepdims=True)
            dai = dai + jnp.sum(li * pr - lr * pi, axis=0, keepdims=True)
            return _row_pick(lr, first_idx), _row_pick(li, first_idx), dar, dai

        z = jnp.zeros((1, 1024), F32)
        _, _, dar, dai = lax.fori_loop(0, nch, step, (z, z, z, z))
        dar_ref[...] = dar
        dai_ref[...] = dai

    vm = pl.BlockSpec(memory_space=pltpu.VMEM)
    big = jax.ShapeDtypeStruct((256, 1024), F32)
    vec = jax.ShapeDtypeStruct((1, 1024), F32)
    return pl.pallas_call(
        body, name=name, in_specs=[vm] * 13, out_specs=[vm] * 7,
        out_shape=[jax.ShapeDtypeStruct((t, 256), F32), vec, vec, big, big, big, big],
        scratch_shapes=_s5_scratch(chunk), compiler_params=_cparams(),
    )(u, dy, du_acc, x0r, x0i, xsr, xsi, a_re, a_im, bb_re, bb_im, cc_re, cc_im)


POOL_HALO = 8


def pool_apply(u_pad, n, transpose, name, tile=ROW_TILE):
    tile = min(tile, n)
    ext = tile + 2 * POOL_HALO
    trel = np.arange(ext)[None, :] - POOL_HALO - np.arange(tile)[:, None]
    if transpose:
        trel = -trel
    band4 = np.concatenate([((trel >= -(1 << w)) & (trel <= (1 << w) - 1)) for w in range(4)], axis=0).astype(np.float32)

    def body(u_ref, band_ref, lm_ref, o_ref):
        lax.fori_loop(0, n // tile, functools.partial(step, u_ref, band_ref, lm_ref, o_ref), 0)

    def step(u_ref, band_ref, lm_ref, o_ref, i, carry):
        val = u_ref[pl.ds(pl.multiple_of(i * tile, tile), ext), :]
        lane = lax.broadcasted_iota(jnp.int32, (ext, 256), 1)
        half = jnp.left_shift(1, jnp.right_shift(lane, 6))
        trow = lax.broadcasted_iota(jnp.int32, (ext, 256), 0) + (i * tile - POOL_HALO)
        cnt = jnp.minimum(trow + half, n) - jnp.maximum(trow - half, 0)
        inv = 1.0 / jnp.maximum(cnt, 1).astype(F32)
        src = val * inv if transpose else val
        acc = _fold_heads(mdot(band_ref[...], src), lm_ref)
        centre = val[POOL_HALO:POOL_HALO + tile]
        if not transpose:
            acc = acc * inv[POOL_HALO:POOL_HALO + tile]
        o_ref[pl.ds(pl.multiple_of(i * tile, tile), tile), :] = acc - centre
        return carry

    vm = pl.BlockSpec(memory_space=pltpu.VMEM)
    return pl.pallas_call(
        body, name=name, in_specs=[vm] * 3, out_specs=vm,
        out_shape=jax.ShapeDtypeStruct((n, 256), F32), compiler_params=_cparams(),
    )(u_pad, jnp.asarray(band4), _na_head_masks())


NA_SCALE = 64.0 ** -0.5
NEG = -1e30


def _call_with_exchange(compute, name, args, out_shapes, comm, scratch=()):
    vm = pl.BlockSpec(memory_space=pltpu.VMEM)
    n_in, n_out = len(args), len(out_shapes)
    if comm is None:
        outs = pl.pallas_call(compute, name=name, in_specs=[vm] * n_in, out_specs=[vm] * n_out, out_shape=out_shapes,
                              scratch_shapes=list(scratch), compiler_params=_cparams())(*args)
        return outs, None
    arrays, scatter = comm
    n = len(arrays)

    def body(*refs):
        c_in = refs[n_in:n_in + n]
        c_out = refs[n_in + n + n_out:n_in + 2 * n + n_out]
        scr = refs[n_in + 2 * n + n_out:n_in + 2 * n + n_out + len(scratch)]
        finish = _exchange_issue(c_in, c_out, scatter, *refs[n_in + 2 * n + n_out + len(scratch):])
        compute(*refs[:n_in], *refs[n_in + n:n_in + n + n_out], *scr)
        finish()

    hbm = pl.BlockSpec(memory_space=pl.ANY)
    outs = pl.pallas_call(
        body, name=name, in_specs=[vm] * n_in + [hbm] * n, out_specs=[vm] * n_out + [hbm] * n,
        out_shape=list(out_shapes) + _exchange_out_shapes(arrays, scatter), scratch_shapes=list(scratch) + _exchange_sems(n),
        compiler_params=_cparams(has_side_effects=True),
    )(*args, *arrays)
    return outs[:n_out], outs[n_out:]


def _na_head_masks():
    return jnp.asarray(np.stack([(np.arange(256) // 64 == h) for h in range(4)]).astype(np.float32).reshape(4, 1, 256))


def _na_window(r, rows):
    start = jnp.clip(r - 4, 0, rows - 8)
    return start, start - r + 7


def _na_probs(qh, kw, kc, bias):
    s_c = b_nt(qh, kc)
    m = jnp.max(s_c, axis=-1, keepdims=True)
    if kw is not None:
        s_w = b_nt(qh, kw) + bias
        m = jnp.maximum(m, jnp.max(s_w, axis=-1, keepdims=True))
        p_w = jnp.exp(s_w - m)
    p_c = jnp.exp(s_c - m)
    l = jnp.sum(p_c, axis=-1, keepdims=True)
    if kw is not None:
        l = l + jnp.sum(p_w, axis=-1, keepdims=True)
        return p_w / l, p_c / l
    return None, p_c / l


def na_fwd(q, k, v, bias8, n_ctx_rows, name, comm=None):
    t = q.shape[0]
    m_ctx = n_ctx_rows
    rows = (t - m_ctx) // GRID_W
    hm = _na_head_masks()

    def body(q_ref, k_ref, v_ref, b_ref, hm_ref, o_ref):
        kc, vc = k_ref[0:m_ctx, :], v_ref[0:m_ctx, :]

        def ctx_step(i, _):
            rs = pl.ds(pl.multiple_of(i * 64, 64), 64)
            q4 = _stack_heads(q_ref[rs, :] * NA_SCALE, hm_ref)
            _, p_c = _na_probs(q4, None, kc, None)
            o_ref[rs, :] = _fold_heads(b_nn(p_c, vc), hm_ref)
            return 0

        lax.fori_loop(0, m_ctx // 64, ctx_step, 0)

        def lat_step(r, _):
            start, off = _na_window(r, rows)
            rs = pl.ds(pl.multiple_of(m_ctx + r * 64, 64), 64)
            ws = pl.ds(pl.multiple_of(m_ctx + start * 64, 64), 512)
            q4 = _stack_heads(q_ref[rs, :] * NA_SCALE, hm_ref)
            kw, vw = k_ref[ws, :], v_ref[ws, :]
            p_w, p_c = _na_probs(q4, kw, kc, b_ref[off])
            o_ref[rs, :] = _fold_heads(b_nn(p_w, vw) + b_nn(p_c, vc), hm_ref)
            return 0

        lax.fori_loop(0, rows, lat_step, 0)

    (o,), received = _call_with_exchange(body, name, [q, k, v, bias8, hm], [jax.ShapeDtypeStruct((t, 256), F32)], comm)
    return o if comm is None else (o, received)


def na_bwd(q, k, v, do, bias8, n_ctx_rows, name, comm=None):
    t = q.shape[0]
    m_ctx = n_ctx_rows
    rows = (t - m_ctx) // GRID_W
    hm = _na_head_masks()

    def body(q_ref, k_ref, v_ref, do_ref, b_ref, hm_ref, dq_ref, dk_ref, dv_ref, db_ref):
        kc, vc = k_ref[0:m_ctx, :], v_ref[0:m_ctx, :]
        dk_ref[...] = jnp.zeros_like(dk_ref)
        dv_ref[...] = jnp.zeros_like(dv_ref)
        db_ref[...] = jnp.zeros_like(db_ref)

        def head_terms(qh, doh, kw, vw, bias):
            p_w, p_c = _na_probs(qh, kw, kc, bias)
            dp_c = b_nt(doh, vc)
            delta = jnp.sum(p_c * dp_c, axis=-1, keepdims=True)
            if kw is not None:
                dp_w = b_nt(doh, vw)
                delta = delta + jnp.sum(p_w * dp_w, axis=-1, keepdims=True)
                ds_w = p_w * (dp_w - delta)
            else:
                ds_w = None
            ds_c = p_c * (dp_c - delta)
            return p_w, p_c, ds_w, ds_c

        def ctx_step(i, carry):
            dkc, dvc = carry
            rs = pl.ds(pl.multiple_of(i * 64, 64), 64)
            q4, do4 = _stack_heads(q_ref[rs, :] * NA_SCALE, hm_ref), _stack_heads(do_ref[rs, :], hm_ref)
            _, p_c, _, ds_c = head_terms(q4, do4, None, None, None)
            dq_ref[rs, :] = _fold_heads(b_nn(ds_c, kc), hm_ref) * NA_SCALE
            return dkc + b_tn(ds_c, q4), dvc + b_tn(p_c, do4)

        zc = jnp.zeros((m_ctx, 256), F32)
        carry = lax.fori_loop(0, m_ctx // 64, ctx_step, (zc, zc))

        def lat_step(r, carry):
            dkc, dvc = carry
            start, off = _na_window(r, rows)
            rs = pl.ds(pl.multiple_of(m_ctx + r * 64, 64), 64)
            ws = pl.ds(pl.multiple_of(m_ctx + start * 64, 64), 512)
            q4, do4 = _stack_heads(q_ref[rs, :] * NA_SCALE, hm_ref), _stack_heads(do_ref[rs, :], hm_ref)
            kw, vw = k_ref[ws, :], v_ref[ws, :]
            p_w, p_c, ds_w, ds_c = head_terms(q4, do4, kw, vw, b_ref[off])
            dq_ref[rs, :] = _fold_heads(b_nn(ds_w, kw) + b_nn(ds_c, kc), hm_ref) * NA_SCALE
            dk_ref[ws, :] += b_tn(ds_w, q4)
            dv_ref[ws, :] += b_tn(p_w, do4)
            db_ref[off] += ds_w
            return dkc + b_tn(ds_c, q4), dvc + b_tn(p_c, do4)

        dkc, dvc = lax.fori_loop(0, rows, lat_step, carry)
        dk_ref[0:m_ctx, :] = dkc
        dv_ref[0:m_ctx, :] = dvc

    row = jax.ShapeDtypeStruct((t, 256), F32)
    return _call_with_exchange(body, name, [q, k, v, do, bias8, hm], [row, row, row, jax.ShapeDtypeStruct(bias8.shape, F32)], comm)


def _na_toeplitz():
    col = np.arange(GRID_W)
    dd = (col[None, :] - col[:, None] + 15).reshape(-1)
    tt = np.zeros((GRID_W * GRID_W, 128), np.float32)
    ok = (dd >= 0) & (dd <= 30)
    tt[np.arange(GRID_W * GRID_W)[ok], dd[ok]] = 1.0
    return tt


def _na_bias8(rpb, name):
    col = np.arange(GRID_W)
    cs = np.clip(col - 8, 0, GRID_W - 16)
    col_mask = (col[None, :] >= cs[:, None]) & (col[None, :] < cs[:, None] + 16)
    rpb2 = jnp.pad(rpb.reshape(60, 31), ((0, 4), (0, 97)))
    (toe,) = whole_fwd(lambda r_, t_: (hdot_nt(r_, t_),), name, [rpb2, jnp.asarray(_na_toeplitz())], [(64, GRID_W * GRID_W)])
    toe = toe[:60].reshape(4, 15, GRID_W, GRID_W)
    b = jnp.stack([toe[:, off:off + 8] for off in range(8)], axis=1)
    b = jnp.where(jnp.asarray(col_mask)[None, None, None], b, NEG)
    return b.transpose(1, 0, 3, 2, 4).reshape(8, 4 * GRID_W, 8 * GRID_W)


def _na_rpb_grad(dbias8, name):
    tt = _na_toeplitz()
    sel = np.zeros((64, 256), np.float32)
    for h in range(4):
        for off in range(8):
            for i in range(8):
                sel[h * 15 + off + i, h * 64 + off * 8 + i] = 1.0
    a2 = dbias8.reshape(8, 4, GRID_W, 8, GRID_W).transpose(1, 0, 3, 2, 4).reshape(256, GRID_W * GRID_W)
    (out,) = whole_fwd(lambda a, t_, s_: (hdot(s_, hdot(a, t_)),), name, [a2, jnp.asarray(tt), jnp.asarray(sel)], [(64, 128)])
    return out[:60, :31].reshape(4, 15, 31)


def f_mod(cs, b_mod, w_mod):
    s = _silu(cs)
    return bdot(s, w_mod) + b_mod, s


def loss_and_grad(z, tgt, n_ctx_rows, name, tile=ROW_TILE):
    t, d = z.shape
    tile = min(tile, n_ctx_rows)
    nct = n_ctx_rows // tile

    def body(z_ref, t_ref, dz_ref, loss_ref):
        i = pl.program_id(0)

        @pl.when(i == 0)
        def _():
            loss_ref[...] = jnp.zeros_like(loss_ref)

        @pl.when(i < nct)
        def _():
            dz_ref[...] = jnp.zeros_like(dz_ref)

        @pl.when(i >= nct)
        def _():
            diff = z_ref[...] - t_ref[...]
            dz_ref[...] = diff * (1.0 / d)
            loss_ref[...] += 0.5 * jnp.sum(jnp.sum(diff * diff, axis=-1, keepdims=True) * (1.0 / d), axis=0, keepdims=True)

    dz, loss = pl.pallas_call(
        body, name=name, grid=(t // tile,),
        in_specs=[pl.BlockSpec((tile, d), lambda i: (i, 0)),
                  pl.BlockSpec((tile, d), lambda i: (jnp.maximum(i - nct, 0), 0))],
        out_specs=[pl.BlockSpec((tile, d), lambda i: (i, 0)), pl.BlockSpec((8, 128), lambda i: (0, 0))],
        out_shape=[jax.ShapeDtypeStruct((t, d), F32), jax.ShapeDtypeStruct((8, 128), F32)],
        compiler_params=_cparams(dimension_semantics=("arbitrary",)),
    )(z, tgt)
    return loss[0, 0], dz


def adamw(parts, w, m, v, name, tile=256):
    npart, r, c = parts.shape
    tile = min(tile, r)
    assert r % tile == 0
    c1 = 1.0 / (1.0 - ADAM_B1 ** ADAM_STEP)
    c2 = 1.0 / (1.0 - ADAM_B2 ** ADAM_STEP)

    def body(p_ref, w_ref, m_ref, v_ref, g_ref, d_ref, nm_ref, nv_ref):
        g = p_ref[0].astype(F32)
        for i in range(1, npart):
            g = g + p_ref[i].astype(F32)
        nm = ADAM_B1 * m_ref[...] + (1.0 - ADAM_B1) * g
        nv = ADAM_B2 * v_ref[...] + (1.0 - ADAM_B2) * (g * g)
        g_ref[...] = g
        nm_ref[...] = nm
        nv_ref[...] = nv
        d_ref[...] = -ADAM_LR * ((nm * c1) / (jnp.sqrt(nv * c2) + ADAM_EPS) + ADAM_WD * w_ref[...])

    blk = pl.BlockSpec((tile, c), lambda i: (i, 0))
    return pl.pallas_call(
        body, name=name, grid=(r // tile,),
        in_specs=[pl.BlockSpec((npart, tile, c), lambda i: (0, i, 0)), blk, blk, blk],
        out_specs=[blk] * 4, out_shape=[jax.ShapeDtypeStruct((r, c), F32)] * 4,
        compiler_params=_cparams(dimension_semantics=("arbitrary",)),
    )(parts, w, m, v)


def _peer(x, y, c, k):
    return (1 - x if k & 4 else x, 1 - y if k & 2 else y, 1 - c if k & 1 else c)


def _exchange_out_shapes(arrays, scatter):
    return [jax.ShapeDtypeStruct(a.shape if s else (N_DEV,) + a.shape, a.dtype) for a, s in zip(arrays, scatter)]


def _exchange_sems(n):
    return [pltpu.SemaphoreType.DMA((n, N_DEV - 1)), pltpu.SemaphoreType.DMA((n, N_DEV - 1)), pltpu.SemaphoreType.DMA((n,))]


def _exchange_issue(ins, outs, scatter, send_sems, recv_sems, local_sems):
    n = len(ins)
    x, y, c = lax.axis_index("x"), lax.axis_index("y"), lax.axis_index("c")
    me = 4 * x + 2 * y + c

    def index_of(p):
        return 4 * p[0] + 2 * p[1] + p[2]

    local = []
    for a in range(n):
        src_me = ins[a].at[me] if scatter[a] else ins[a]
        loc = pltpu.make_async_copy(src_me, outs[a].at[me], local_sems.at[a])
        loc.start()
        local.append(loc)
    for k in range(1, N_DEV):
        peer = _peer(x, y, c, k)
        for a in range(n):
            src = ins[a].at[index_of(peer)] if scatter[a] else ins[a]
            pltpu.make_async_remote_copy(
                src_ref=src, dst_ref=outs[a].at[me], send_sem=send_sems.at[a, k - 1], recv_sem=recv_sems.at[a, k - 1],
                device_id=peer, device_id_type=pl.DeviceIdType.MESH).start()

    def finish():
        for k in range(1, N_DEV):
            peer = _peer(x, y, c, k)
            for a in range(n):
                src = ins[a].at[index_of(peer)] if scatter[a] else ins[a]
                cp = pltpu.make_async_remote_copy(
                    src_ref=src, dst_ref=outs[a].at[index_of(peer)], send_sem=send_sems.at[a, k - 1],
                    recv_sem=recv_sems.at[a, k - 1], device_id=peer, device_id_type=pl.DeviceIdType.MESH)
                cp.wait_send()
                cp.wait_recv()
        for loc in local:
            loc.wait()

    return finish


def exchange(arrays, scatter, name):
    n = len(arrays)

    def body(*refs):
        _exchange_issue(refs[:n], refs[n:2 * n], scatter, *refs[2 * n:])()

    hbm = pl.BlockSpec(memory_space=pl.ANY)
    return pl.pallas_call(
        body, name=name, in_specs=[hbm] * n, out_specs=[hbm] * n, out_shape=_exchange_out_shapes(arrays, scatter),
        scratch_shapes=_exchange_sems(n), compiler_params=pltpu.CompilerParams(has_side_effects=True),
    )(*arrays)


def _rope_tables(n_lat, n_ctx):
    tok = np.arange(n_lat)
    freqs = 10000.0 ** (-np.arange(0, 16, 2, dtype=np.float32) / 16.0)

    def table(pos):
        ang = pos.astype(np.float32)[:, None] * freqs[None, :]
        ang = np.concatenate([ang, ang], axis=-1)
        return np.cos(ang), np.sin(ang)

    cr, sr = table(tok // GRID_W)
    cc, sc = table(tok % GRID_W)
    cos = np.tile(np.concatenate([cr, cc], axis=-1), (1, 4))
    sin = np.tile(np.concatenate([sr, sc], axis=-1), (1, 4))
    cos = np.concatenate([np.ones((n_ctx, 128), np.float32), cos], axis=0)
    sin = np.concatenate([np.zeros((n_ctx, 128), np.float32), sin], axis=0)
    return jnp.asarray(cos, F32), jnp.asarray(sin, F32)


def _pad_w_in(w):
    z = lambda n: jnp.zeros((w.shape[0], n), w.dtype)
    return jnp.concatenate([w[:, 1824:2848], w[:, 128:384], w[:, 416:672], w[:, 672:928], w[:, 928:1184], w[:, 1312:1568],
                            w[:, 1568:1824], w[:, 0:128], w[:, 384:416], z(96), w[:, 1184:1312], z(128)], axis=1)


def _unpad_w_in(wp):
    return jnp.concatenate([wp[:, C_GK:C_GK + 128], wp[:, C_GV:C_GV + 256], wp[:, C_GG:C_GG + 32], wp[:, C_NK:C_NK + 256],
                            wp[:, C_NV:C_NV + 256], wp[:, C_SU:C_SU + 256], wp[:, C_GQ:C_GQ + 128], wp[:, C_NQ:C_NQ + 256],
                            wp[:, C_PU:C_PU + 256], wp[:, C_GT:C_GT + 1024]], axis=1)


_W_IN_SEGS = [(0, 128, C_GK), (128, 256, C_GV), (384, 32, C_GG), (416, 256, C_NK), (672, 256, C_NV), (928, 256, C_SU),
              (1184, 128, C_GQ), (1312, 256, C_NQ), (1568, 256, C_PU), (1824, 1024, C_GT)]
W_IN_SHARD = N_IN // N_DEV


def _pad_w_in_blocks(blocks):
    pieces = []
    for orig, width, padded in _W_IN_SEGS:
        col = orig
        while col < orig + width:
            dev, lo = divmod(col, W_IN_SHARD)
            n = min(W_IN_SHARD - lo, orig + width - col)
            pieces.append((padded + col - orig, blocks[dev][:, lo:lo + n]))
            col += n
    pieces.sort(key=lambda p: p[0])
    out, at = [], 0
    for start, piece in pieces:
        if start > at:
            out.append(jnp.zeros((blocks.shape[1], start - at), blocks.dtype))
        out.append(piece)
        at = start + piece.shape[1]
    out.append(jnp.zeros((blocks.shape[1], PW - at), blocks.dtype))
    return jnp.concatenate(out, axis=1)


def _w_in_slabs(wp_blocks):
    slabs = []
    for dev in range(N_DEV):
        first, pieces = dev * W_IN_SHARD, []
        for orig, width, padded in _W_IN_SEGS:
            lo, hi = max(orig, first), min(orig + width, first + W_IN_SHARD)
            if lo < hi:
                a = padded + lo - orig
                blk, off = divmod(a, 1024)
                assert off + (hi - lo) <= 1024
                pieces.append(wp_blocks[blk][:, off:off + hi - lo])
        slabs.append(jnp.concatenate(pieces, axis=1))
    return jnp.stack(slabs)


def _pad_rows(u):
    return jnp.pad(u, ((POOL_HALO, POOL_HALO), (0, 0)))


def _block_diag4(w):
    out = jnp.zeros((256, 256), w.dtype)
    for i in range(4):
        out = lax.dynamic_update_slice(out, w[i], (64 * i, 64 * i))
    return out


def _layer_params(p, big, l):
    e_rep, e_tile, gmask, bdm = _s5_consts()
    wg = jnp.zeros((128, 256), F32)
    wg = lax.dynamic_update_slice(wg, p["gla_w_gate"][l, 0], (0, 0))
    wg = lax.dynamic_update_slice(wg, p["gla_w_gate"][l, 1], (16, 128))
    s5 = []
    for d in range(2):
        s5.append([p["s5_lam_re"][l, d], p["s5_lam_im"][l, d], p["s5_log_dt"][l, d].reshape(16, 1),
                   p["s5_b_re"][l, d].transpose(0, 2, 1).reshape(256, 64), p["s5_b_im"][l, d].transpose(0, 2, 1).reshape(256, 64),
                   p["s5_c_re"][l, d].reshape(256, 64), p["s5_c_im"][l, d].reshape(256, 64), e_rep, e_tile, gmask, bdm])
    havg = jnp.asarray((np.arange(256)[:, None] // 64 == np.arange(256)[None, :] // 64).astype(np.float32) / 64.0)
    e4 = jnp.asarray((np.arange(64)[:, None] == np.arange(256)[None, :] % 64).astype(np.float32))
    return dict(
        g_pre=p["g_pre"][l].reshape(1, D), g_post=p["g_post"][l].reshape(1, D), b_mod=p["b_mod"][l].reshape(1, 3 * D),
        w_mod=big["w_mod"], w_in=_pad_w_in_blocks(big["w_in_blocks"]) if "w_in_blocks" in big else _pad_w_in(big["w_in"]), w_out=big.get("w_out"),
        wg=wg, bg=p["gla_b_gate"][l].reshape(1, 256), g_norm=jnp.pad(p["gla_g_norm"][l].reshape(1, 64), ((0, 7), (0, 0))),
        bias8=_na_bias8(p["na_rpb"][l], f"na_bias_l{l}"), s5=s5, s5_d=p["s5_d"][l].reshape(1, 256), w_glu=None if big.get("s5_w_glu") is None else big["s5_w_glu"].astype(F32),
        b_glu=p["s5_b_glu"][l].reshape(1, 256), wpool=_block_diag4(p["pool_w"][l]), pool_scale=p["pool_scale"][l].reshape(1, 256),
        havg=havg, e4=e4)


def _matmul_tile(t, tile, steps):
    return t // steps if t % (8 * steps) == 0 else tile


def _cols(pz, start, width):
    return pz[:, start:start + width]


def _layer_fwd(z, modseg, lp, cos, sin, m_ctx, tile, s5_chunk, l, comm=None, late=None):
    t = z.shape[0]
    nct = m_ctx // tile
    nm = lambda s: f"{s}_l{l}"
    (h,) = rowwise_fwd(f_pre, nm("pre"), [z], [modseg], [lp["g_pre"]], [D], tile, nct)
    mm_tile = _matmul_tile(t, tile, 4)
    (gt,) = mm_nn_cols(h, lp["w_in"], C_GT, [1024], nm("in_proj_a"), tm=mm_tile)
    pv, nk, nv, su = mm_nn_cols(h, lp["w_in"], C_GV, [256] * 4, nm("in_proj_b"), tm=mm_tile)
    nq, pu, pk, pg, pq = mm_nn_cols(h, lp["w_in"], C_NQ, [256, 256, 128, 128, 128], nm("in_proj_c"), tm=mm_tile)
    q_r, k_r, lgf, lgb = rowwise_fwd(f_gla_prep, nm("gla_prep"), [pk, pg, pq, cos, sin], [], [lp["wg"], lp["bg"]], [128] * 4, tile, nct)
    half = None if comm is None else comm[0].shape[0] // 2
    spread = None if comm is None else [comm[0][:half], comm[1], comm[2], comm[3], comm[0][half:]]
    part = (lambda idx: None) if comm is None else (lambda idx: ([spread[i] for i in idx], [False] * len(idx)))
    (o1, st_f), got_late = gla_scan_fwd(q_r, k_r, pv, lgf, jnp.zeros((t, 256), F32), m_ctx, False, nm("gla_f"),
                                        None if late is None else (list(late), [False, False]))
    if late is not None:
        lp["w_out"], lp["w_glu"] = _gathered(got_late[0], False), _gathered(got_late[1], False).astype(F32)
    (o_gla, st_b), got_out = gla_scan_fwd(q_r, k_r, pv, lgb, o1, m_ctx, True, nm("gla_r"), part([2, 3]))
    received = None
    if comm is None:
        o_na = na_fwd(nq, nk, nv, lp["bias8"], m_ctx, nm("na"))
    else:
        o_na, got_in = na_fwd(nq, nk, nv, lp["bias8"], m_ctx, nm("na"), part([1]))
    s5p = [whole_fwd(f_s5_params, nm(f"s5_par{d}"), lp["s5"][d], [(1, 1024)] * 2 + [(256, 1024)] * 4) for d in range(2)]
    (y1, *states_f), got_mod_a = s5_scan_fwd(su, jnp.zeros((t, 256), F32), *s5p[0], m_ctx, s5_chunk, False, nm("s5_f"), part([0]))
    (y5, *states_b), got_mod_b = s5_scan_fwd(su, y1, *s5p[1], m_ctx, s5_chunk, True, nm("s5_r"), part([4]))
    if comm is not None:
        received = [jnp.concatenate([got_mod_a[0], got_mod_b[0]], axis=1), got_in[0], got_out[0], got_out[1]]
    pm = jnp.concatenate([pool_apply(_pad_rows(pu[:m_ctx]), m_ctx, False, nm("pool_c")),
                          pool_apply(_pad_rows(pu[m_ctx:]), t - m_ctx, False, nm("pool_x"))], axis=0)
    mix_rows = [o_gla, o_na, y5, su, pm, gt]
    mix_globs = [lp["g_norm"], lp["s5_d"], lp["w_glu"], lp["b_glu"], lp["wpool"], lp["pool_scale"], lp["havg"], lp["e4"]]
    (yg,) = rowwise_fwd(f_mix, nm("mix"), mix_rows, [], mix_globs, [D], tile, nct)
    out = mm_nn([yg], lp["w_out"], nm("out_proj"), tm=mm_tile)
    (z_new,) = rowwise_fwd(f_post, nm("post"), [z, out], [modseg], [lp["g_post"]], [D], tile, nct)
    saved = dict(z=z, h=h, pv=pv, nk=nk, nv=nv, su=su, nq=nq, pk=pk, pg=pg, pq=pq, q_r=q_r, k_r=k_r, lgf=lgf, lgb=lgb,
                 st_f=st_f, st_b=st_b, s5p=s5p, x0f=tuple(states_f), x0b=tuple(states_b), mix_rows=mix_rows, mix_globs=mix_globs,
                 yg=yg, out=out)
    return z_new, saved, received


def _f_pre_res(x, mod, g_pre):
    return f_pre(x, mod, g_pre)[0], x


def _layer_bwd(dz_new, sv, modseg, lp, cos, sin, m_ctx, tile, s5_chunk, l, comm=None, gdt=F32, send_early=False, as_slabs=False):
    t = dz_new.shape[0]
    nct = m_ctx // tile
    nm = lambda s: f"{s}_l{l}"
    g = {}
    dz_res, dout, dmod_post, g["g_post"] = rowwise_bwd(f_post, nm("post_b"), [sv["z"], sv["out"]], [modseg], [lp["g_post"]],
                                                       [dz_new], tile, nct, [True, True], [True])
    dyg = mm_nt([dout], lp["w_out"], nm("out_proj_dx"), tm=_matmul_tile(t, tile, 4))
    dw_tile = _matmul_tile(t, tile, 4)
    (g["w_out"],) = mm_tn(sv["yg"], [dout], nm("out_proj_dw"), tm=dw_tile, out_dtype=gdt)
    res = rowwise_bwd(f_mix, nm("mix_b"), sv["mix_rows"], [], sv["mix_globs"], [dyg], tile, nct, [True] * 6, [True] * 6 + [False] * 2)
    do_gla, do_na, dy5, dsu_a, dpm, dgt = res[:6]
    g["g_norm"], g["s5_d"], g["w_glu"], g["b_glu"], g["wpool"], g["pool_scale"] = res[6:]
    dpu = jnp.concatenate([pool_apply(_pad_rows(dpm[:m_ctx]), m_ctx, True, nm("pool_c_b")),
                           pool_apply(_pad_rows(dpm[m_ctx:]), t - m_ctx, True, nm("pool_x_b"))], axis=0)
    r_b = s5_scan_bwd(sv["su"], dy5, dsu_a, *sv["x0b"], *sv["s5p"][1], m_ctx, s5_chunk, True, nm("s5_r_b"))
    r_f = s5_scan_bwd(sv["su"], dy5, r_b[0], *sv["x0f"], *sv["s5p"][0], m_ctx, s5_chunk, False, nm("s5_f_b"))
    dsu = r_f[0]
    g["s5"] = [whole_bwd(f_s5_params, nm(f"s5_par{d}_b"), lp["s5"][d], list(r[1:]), [True] * 7 + [False] * 4)
               for d, r in ((0, r_f), (1, r_b))]
    part = (lambda idx: None) if comm is None else (lambda idx: ([comm[i] for i in idx], [True] * len(idx)))
    (dnq, dnk, dnv, dbias8), got_in = na_bwd(sv["nq"], sv["nk"], sv["nv"], do_na, lp["bias8"], m_ctx, nm("na_b"), part([0]))
    g["rpb"] = _na_rpb_grad(dbias8, nm("na_rpb_b"))
    zq, zv = jnp.zeros((t, 128), F32), jnp.zeros((t, 256), F32)
    early = ([_slabs(g["w_out"], False).astype(BF16), _slabs(g["w_glu"], False).astype(BF16)], [True, True]) if send_early else None
    (dq1, dk1, dv1, dlgb), g["early"] = gla_scan_bwd(sv["q_r"], sv["k_r"], sv["pv"], sv["lgb"], sv["st_b"], do_gla, (zq, zq, zv), m_ctx, True,
                                                     nm("gla_r_b"), early)
    (dq_r, dk_r, dpv, dlgf), got_out = gla_scan_bwd(sv["q_r"], sv["k_r"], sv["pv"], sv["lgf"], sv["st_f"], do_gla, (dq1, dk1, dv1), m_ctx, False,
                                                    nm("gla_f_b"), part([1, 2]))
    received = None if comm is None else [got_in[0], got_out[0], got_out[1]]
    dpk, dpg, dpq, g["wg"], g["bg"] = rowwise_bwd(f_gla_prep, nm("gla_prep_b"), [sv["pk"], sv["pg"], sv["pq"], cos, sin], [],
                                                  [lp["wg"], lp["bg"]], [dq_r, dk_r, dlgf, dlgb], tile, nct,
                                                  [True, True, True, False, False], [True, True])
    parts = [dgt, dpv, dnk, dnv, dsu, dnq, dpu, dpk, dpg, dpq, jnp.zeros((t, 128), F32)]
    dh = mm_nt(parts, lp["w_in"], nm("in_proj_dx"), tm=_matmul_tile(t, tile, 8))
    dw_blocks = mm_tn(sv["h"], parts, nm("in_proj_dw"), tm=dw_tile, out_dtype=gdt)
    if as_slabs:
        g["w_in_slabs"] = _w_in_slabs(dw_blocks)
    else:
        g["w_in"] = _unpad_w_in(jnp.concatenate(dw_blocks, axis=1))
    dz, dmod_pre, g["g_pre"] = rowwise_bwd(_f_pre_res, nm("pre_b"), [sv["z"]], [modseg], [lp["g_pre"]], [dh, dz_res], tile, nct, [True], [True])
    return dz, dmod_pre, dmod_post, g, received


def _f_mod_sum(cs, b_mod, w_mod):
    mod, _ = f_mod(cs, b_mod, w_mod)
    return mod, cs


def local_step(x, c, ctx, tgt, p, shards=None, tile=ROW_TILE, s5_chunk=S5_CHUNK):
    n_lat, m_ctx = x.shape[0], ctx.shape[0]
    n_layers = p["g_pre"].shape[0]
    z = jnp.concatenate([ctx, x], axis=0)
    cos, sin = _rope_tables(n_lat, m_ctx)
    cs = jnp.concatenate([c.reshape(1, D), p["c_ctx"].reshape(1, D), jnp.zeros((6, D), F32)], axis=0)
    gather = [False] * len(_SHARDED)
    lps, mods, silus, saves = [], [], [], []
    got = exchange(shards[0][:2], [False, False], "gather_weights_l0") if shards is not None else None
    for l in range(n_layers):
        if shards is None:
            big = {n: p[n][l] for n in _SHARDED}
        else:
            big = {n: _gathered(g, _BY_COLS[n]) for n, g in zip(_SHARDED, got) if n != "w_in"}
            big["w_in_blocks"] = got[_SHARDED.index("w_in")]
        lp = _layer_params(p, big, l)
        mod8, s8 = whole_fwd(f_mod, f"mod_l{l}", [cs, lp["b_mod"], lp["w_mod"]], [(8, 3 * D), (8, D)])
        modseg = mod8[:2].reshape(2, 1, 3 * D)
        comm = shards[l + 1] if shards is not None and l + 1 < n_layers else None
        late = shards[0][2:] if shards is not None and l == 0 else None
        z, sv, got = _layer_fwd(z, modseg, lp, cos, sin, m_ctx, tile, s5_chunk, l, comm, late)
        lps.append(lp); mods.append(modseg); silus.append(s8); saves.append(sv)
    loss, dz = loss_and_grad(z, tgt, m_ctx, "loss", tile)
    grads, received = [None] * n_layers, [None] * n_layers
    gdt = F32 if shards is None else BF16
    dcs = jnp.zeros((8, D), F32)
    pending = None
    for l in reversed(range(n_layers)):
        lp = lps[l]
        dz, dmod_pre, dmod_post, g, got = _layer_bwd(dz, saves[l], mods[l], lp, cos, sin, m_ctx, tile, s5_chunk, l, pending, gdt,
                                                     send_early=shards is not None and l == 0, as_slabs=shards is not None)
        if pending is not None:
            received[l + 1] = got
        dmod = jnp.concatenate([dmod_pre.reshape(2, 3 * D)[:, :2 * D], dmod_post.reshape(2, 3 * D)[:, 2 * D:]], axis=1)
        dmod8 = jnp.pad(dmod, ((0, 6), (0, 0)))
        dcs, g["b_mod"] = whole_bwd(_f_mod_sum, f"mod_b_l{l}", [cs, lp["b_mod"], lp["w_mod"]], [dmod8, dcs], [True, True, False])
        if shards is None:
            g["w_mod"] = jnp.concatenate(mm_tn(silus[l], [dmod8[:, :D], dmod8[:, D:2 * D], dmod8[:, 2 * D:]], f"mod_dw_l{l}", tm=8), axis=1)
        else:
            g["mod_s"], g["mod_d"] = silus[l][:2], dmod
        grads[l] = g
        if shards is not None:
            pending = _layer_sends(g)
    if shards is not None:
        got_in, got_small = exchange([pending[0], _small_sends(dcs[1], grads)], [True, False], "exchange_grads_l0")
        received[0] = [got_in] + list(grads[0]["early"]) + [got_small]
    return loss, dz[m_ctx:], dcs[1], grads, received


_WEIGHTS = ["c_ctx", "w_mod", "b_mod", "g_pre", "g_post", "w_in", "w_out", "gla_w_gate", "gla_b_gate", "gla_g_norm", "na_rpb",
            "s5_lam_re", "s5_lam_im", "s5_log_dt", "s5_b_re", "s5_b_im", "s5_c_re", "s5_c_im", "s5_d", "s5_w_glu", "s5_b_glu",
            "pool_w", "pool_scale"]
_INPUTS = ["x", "c", "ctx"] + _WEIGHTS + ["loss_target"] + ["m_" + n for n in _WEIGHTS] + ["v_" + n for n in _WEIGHTS]
_SHARDED = ["w_mod", "w_in", "w_out", "s5_w_glu"]
_BY_COLS = {"w_mod": True, "w_in": True, "w_out": False, "s5_w_glu": False}
_GRAD_SHARDED = ["w_in", "w_out", "s5_w_glu"]
_SMALL = [n for n in _WEIGHTS if n not in _SHARDED]
_SMALL_PER_LAYER = [n for n in _SMALL if n != "c_ctx"]
_PACK_ROWS = 256


def _pack_plan(like):
    tiled = [i for i, a in enumerate(like) if a.size % 1024 == 0]
    loose = [i for i, a in enumerate(like) if a.size % 1024 != 0]
    tail = -(-sum(like[i].size for i in loose) // 1024) * 8
    rows = sum(like[i].size // 128 for i in tiled) + tail
    return tiled, loose, tail, -(-rows // _PACK_ROWS) * _PACK_ROWS - rows


def _pack_rows(like, index):
    tiled, _, _, _ = _pack_plan(like)
    row = 0
    for i in tiled:
        n = like[i].size // 128
        if i == index:
            return row, row + n
        row += n
    raise ValueError("not a tile-aligned entry")


def _pack(arrs):
    tiled, loose, tail, fill = _pack_plan(arrs)
    dt = arrs[0].dtype
    flat = jnp.concatenate([arrs[i].reshape(-1) for i in loose])
    flat = jnp.pad(flat, (0, tail * 128 - flat.shape[0])).reshape(tail, 128)
    return jnp.concatenate([arrs[i].reshape(-1, 128) for i in tiled] + [flat, jnp.zeros((fill, 128), dt)], axis=0)


def _unpack(packed, like):
    tiled, loose, tail, _ = _pack_plan(like)
    out, row = [None] * len(like), 0
    for i in tiled:
        n = like[i].size // 128
        out[i] = packed[row:row + n].reshape(like[i].shape)
        row += n
    flat, pos = packed[row:row + tail].reshape(-1), 0
    for i in loose:
        out[i] = flat[pos:pos + like[i].size].reshape(like[i].shape)
        pos += like[i].size
    return out


def _gathered(g, cols):
    if cols:
        return g.transpose(1, 0, 2).reshape(g.shape[1], N_DEV * g.shape[2])
    return g.reshape(N_DEV * g.shape[1], g.shape[2])


def _slabs(w, cols):
    r, c = w.shape
    if cols:
        return w.reshape(r, N_DEV, c // N_DEV).transpose(1, 0, 2)
    return w.reshape(N_DEV, r // N_DEV, c)


def _layer_small(g):
    s5 = lambda i, f: jnp.stack([f(g["s5"][d][i]) for d in range(2)])
    return {
        "b_mod": g["b_mod"].reshape(3 * D), "g_pre": g["g_pre"].reshape(D), "g_post": g["g_post"].reshape(D),
        "gla_w_gate": jnp.stack([g["wg"][0:16, 0:128], g["wg"][16:32, 128:256]]),
        "gla_b_gate": g["bg"].reshape(2, 128), "gla_g_norm": g["g_norm"][0], "na_rpb": g["rpb"],
        "s5_lam_re": s5(0, lambda a: a), "s5_lam_im": s5(1, lambda a: a), "s5_log_dt": s5(2, lambda a: a.reshape(16)),
        "s5_b_re": s5(3, lambda a: a.reshape(16, 16, 64).transpose(0, 2, 1)),
        "s5_b_im": s5(4, lambda a: a.reshape(16, 16, 64).transpose(0, 2, 1)),
        "s5_c_re": s5(5, lambda a: a.reshape(16, 16, 64)), "s5_c_im": s5(6, lambda a: a.reshape(16, 16, 64)),
        "s5_d": g["s5_d"].reshape(256), "s5_b_glu": g["b_glu"].reshape(256),
        "pool_w": jnp.stack([g["wpool"][64 * i:64 * i + 64, 64 * i:64 * i + 64] for i in range(4)]),
        "pool_scale": g["pool_scale"].reshape(256),
    }


def _layer_sends(g):
    return [g["w_in_slabs"].astype(BF16), _slabs(g["w_out"], False).astype(BF16), _slabs(g["w_glu"], False).astype(BF16)]


def _small_sends(d_c_ctx, grads):
    per_layer = [_layer_small(g) for g in grads]
    full = {n: jnp.stack([s[n] for s in per_layer]) for n in _SMALL_PER_LAYER}
    full["c_ctx"] = d_c_ctx
    factors = [jnp.stack([g["mod_s"] for g in grads]), jnp.stack([g["mod_d"] for g in grads])]
    return _pack([full[n] for n in _SMALL] + factors).astype(BF16)


def kernel(x, c, ctx, c_ctx, w_mod, b_mod, g_pre, g_post, w_in, w_out, gla_w_gate, gla_b_gate, gla_g_norm, na_rpb, s5_lam_re, s5_lam_im, s5_log_dt, s5_b_re, s5_b_im, s5_c_re, s5_c_im, s5_d, s5_w_glu, s5_b_glu, pool_w, pool_scale, loss_target, m_c_ctx, m_w_mod, m_b_mod, m_g_pre, m_g_post, m_w_in, m_w_out, m_gla_w_gate, m_gla_b_gate, m_gla_g_norm, m_na_rpb, m_s5_lam_re, m_s5_lam_im, m_s5_log_dt, m_s5_b_re, m_s5_b_im, m_s5_c_re, m_s5_c_im, m_s5_d, m_s5_w_glu, m_s5_b_glu, m_pool_w, m_pool_scale, v_c_ctx, v_w_mod, v_b_mod, v_g_pre, v_g_post, v_w_in, v_w_out, v_gla_w_gate, v_gla_b_gate, v_gla_g_norm, v_na_rpb, v_s5_lam_re, v_s5_lam_im, v_s5_log_dt, v_s5_b_re, v_s5_b_im, v_s5_c_re, v_s5_c_im, v_s5_d, v_s5_w_glu, v_s5_b_glu, v_pool_w, v_pool_scale):
    given = dict(zip(_INPUTS, (x, c, ctx, c_ctx, w_mod, b_mod, g_pre, g_post, w_in, w_out, gla_w_gate, gla_b_gate, gla_g_norm, na_rpb, s5_lam_re, s5_lam_im, s5_log_dt, s5_b_re, s5_b_im, s5_c_re, s5_c_im, s5_d, s5_w_glu, s5_b_glu, pool_w, pool_scale, loss_target, m_c_ctx, m_w_mod, m_b_mod, m_g_pre, m_g_post, m_w_in, m_w_out, m_gla_w_gate, m_gla_b_gate, m_gla_g_norm, m_na_rpb, m_s5_lam_re, m_s5_lam_im, m_s5_log_dt, m_s5_b_re, m_s5_b_im, m_s5_c_re, m_s5_c_im, m_s5_d, m_s5_w_glu, m_s5_b_glu, m_pool_w, m_pool_scale, v_c_ctx, v_w_mod, v_b_mod, v_g_pre, v_g_post, v_w_in, v_w_out, v_gla_w_gate, v_gla_b_gate, v_gla_g_norm, v_na_rpb, v_s5_lam_re, v_s5_lam_im, v_s5_log_dt, v_s5_b_re, v_s5_b_im, v_s5_c_re, v_s5_c_im, v_s5_d, v_s5_w_glu, v_s5_b_glu, v_pool_w, v_pool_scale)))
    n_layers = w_in.shape[0]
    shards = [[given[n][l].astype(BF16) for n in _SHARDED] for l in range(n_layers)]
    p = {n: given[n] for n in _SMALL}
    loss, grad_x, _, _, received = local_step(x[0], c, ctx[0], loss_target[0], p, shards)
    final = {}
    for n in _GRAD_SHARDED:
        per_layer = [adamw(received[l][_GRAD_SHARDED.index(n)], given[n][l], given["m_" + n][l], given["v_" + n][l], f"adamw_{n}_l{l}")
                     for l in range(n_layers)]
        final[n] = [jnp.stack([res[kind] for res in per_layer]) for kind in range(4)]
    factor_like = [jnp.zeros((n_layers, 2, D), F32), jnp.zeros((n_layers, 2, 3 * D), F32)]
    like = [given[n] for n in _SMALL] + factor_like
    small_recv = received[0][-1]
    rows_s, rows_d = _pack_rows(like, len(_SMALL)), _pack_rows(like, len(_SMALL) + 1)
    fac_s = small_recv[:, rows_s[0]:rows_s[1]].reshape(N_DEV, n_layers, 2, D)
    fac_d = small_recv[:, rows_d[0]:rows_d[1]].reshape(N_DEV, n_layers, 2, 3 * D)
    me = 4 * lax.axis_index("x") + 2 * lax.axis_index("y") + lax.axis_index("c")
    cols = w_mod.shape[2]
    per_layer = []
    for l in range(n_layers):
        s_all = fac_s[:, l].reshape(2 * N_DEV, D)
        d_mine = lax.dynamic_slice_in_dim(fac_d[:, l].reshape(2 * N_DEV, 3 * D), me * cols, cols, axis=1)
        (g_mod,) = mm_tn(s_all, [d_mine], f"mod_dw_l{l}", tm=2 * N_DEV, tn=cols)
        per_layer.append(adamw(g_mod[None], given["w_mod"][l], given["m_w_mod"][l], given["v_w_mod"][l], f"adamw_w_mod_l{l}"))
    final["w_mod"] = [jnp.stack([res[kind] for res in per_layer]) for kind in range(4)]
    res = adamw(small_recv, _pack(like), _pack([given["m_" + n] for n in _SMALL] + factor_like),
                _pack([given["v_" + n] for n in _SMALL] + factor_like), "adamw_small")
    unpacked = [_unpack(packed, like) for packed in res]
    for i, n in enumerate(_SMALL):
        final[n] = [unpacked[kind][i] for kind in range(4)]
    loss = lax.psum(loss, ("x", "y", "c"))
    return (loss, grad_x[None], *[final[n][0] for n in _WEIGHTS], *[final[n][1] for n in _WEIGHTS],
            *[final[n][2] for n in _WEIGHTS], *[final[n][3] for n in _WEIGHTS])
```

```python
import functools
import math

import numpy as np
import jax
import jax.numpy as jnp
from jax import lax
from jax.experimental import pallas as pl
from jax.experimental.pallas import tpu as pltpu

F32 = jnp.float32
BF16 = jnp.bfloat16
HIGHEST = lax.Precision.HIGHEST
HIGH = lax.Precision.HIGH

D = 1024
GRID_W = 64
EPS = 1e-6
N_DEV = 8
C_GT, C_GV, C_NK, C_NV, C_SU, C_NQ, C_PU, C_GK, C_GG, C_GQ, C_END = 0, 1024, 1280, 1536, 1792, 2048, 2304, 2560, 2688, 2816, 2944
PW = 3072
N_CTX_ORIG = 416
N_IN = 2848
GLA_CHUNK = 128
S5_CHUNK = 256
ROW_TILE = 256
VMEM_LIMIT = 56 * 1024 * 1024

ADAM_LR, ADAM_B1, ADAM_B2, ADAM_EPS, ADAM_WD, ADAM_STEP = 0.001, 0.9, 0.999, 1e-08, 0.01, 10


def _cparams(**kw):
    return pltpu.CompilerParams(vmem_limit_bytes=VMEM_LIMIT, **kw)


def _dg(a, b, ca, cb, precision=None):
    return lax.dot_general(a, b, (((ca,), (cb,)), ((), ())), precision=precision, preferred_element_type=F32)


def hdot(a, b):
    return _dg(a, b, 1, 0, HIGHEST)


def hdot_nt(a, b):
    return _dg(a, b, 1, 1, HIGHEST)


def hdot_tn(a, b):
    return _dg(a, b, 0, 0, HIGHEST)


def mdot(a, b):
    return _dg(a, b, 1, 0, HIGH)


def mdot_nt(a, b):
    return _dg(a, b, 1, 1, HIGH)


def mdot_tn(a, b):
    return _dg(a, b, 0, 0, HIGH)


def b_nn(a, b):
    return _dg(a.astype(BF16), b.astype(BF16), 1, 0)


def b_nt(a, b):
    return _dg(a.astype(BF16), b.astype(BF16), 1, 1)


def b_tn(a, b):
    return _dg(a.astype(BF16), b.astype(BF16), 0, 0)


@jax.custom_vjp
def bdot(a, b):
    return b_nn(a, b)


def _bdot_fwd(a, b):
    return b_nn(a, b), (a, b)


def _bdot_bwd(res, ct):
    a, b = res
    return b_nt(ct, b).astype(a.dtype), b_tn(a, ct).astype(b.dtype)


bdot.defvjp(_bdot_fwd, _bdot_bwd)


def _log_sigmoid(z):
    return jnp.minimum(z, 0.0) - jnp.log(1.0 + jnp.exp(-jnp.abs(z)))


def _silu(z):
    return z * jax.nn.sigmoid(z)


def _gelu(z):
    return 0.5 * z * (1.0 + jnp.tanh(math.sqrt(2.0 / math.pi) * (z + 0.044715 * (z * z * z))))


def _cat(vals):
    return vals[0] if len(vals) == 1 else jnp.concatenate(vals, axis=-1)


def mm_nn(a_parts, b, name, tm=ROW_TILE, tn=1024):
    t = a_parts[0].shape[0]
    k, n = b.shape
    na = len(a_parts)
    tn = min(tn, n)

    def body(*refs):
        a = _cat([r[...].astype(BF16) for r in refs[:na]])
        refs[na + 1][...] = _dg(a, refs[na][...].astype(BF16), 1, 0)

    return pl.pallas_call(
        body, name=name, grid=(n // tn, t // tm),
        in_specs=[pl.BlockSpec((tm, p.shape[1]), lambda j, i: (i, 0)) for p in a_parts]
        + [pl.BlockSpec((k, tn), lambda j, i: (0, j))],
        out_specs=pl.BlockSpec((tm, tn), lambda j, i: (i, j)),
        out_shape=jax.ShapeDtypeStruct((t, n), F32),
        compiler_params=_cparams(dimension_semantics=("arbitrary", "arbitrary")),
    )(*a_parts, b)


def mm_nn_cols(a, b, start, widths, name, tm=ROW_TILE):
    t, k = a.shape
    tn = 1024
    assert start % tn == 0 and sum(widths) <= tn

    def body(a_ref, b_ref, *o_refs):
        r = _dg(a_ref[...].astype(BF16), b_ref[...].astype(BF16), 1, 0)
        off = 0
        for o_ref, w in zip(o_refs, widths):
            o_ref[...] = r[:, off:off + w]
            off += w

    return pl.pallas_call(
        body, name=name, grid=(t // tm,),
        in_specs=[pl.BlockSpec((tm, k), lambda i: (i, 0)), pl.BlockSpec((k, tn), lambda i: (0, start // tn))],
        out_specs=[pl.BlockSpec((tm, w), lambda i: (i, 0)) for w in widths],
        out_shape=[jax.ShapeDtypeStruct((t, w), F32) for w in widths],
        compiler_params=_cparams(dimension_semantics=("arbitrary",)),
    )(a, b)


def mm_nt(a_parts, b, name, tm=ROW_TILE):
    t = a_parts[0].shape[0]
    n, k = b.shape
    na = len(a_parts)

    def body(*refs):
        a = _cat([r[...].astype(BF16) for r in refs[:na]])
        refs[na + 1][...] = _dg(a, refs[na][...].astype(BF16), 1, 1)

    return pl.pallas_call(
        body, name=name, grid=(t // tm,),
        in_specs=[pl.BlockSpec((tm, p.shape[1]), lambda i: (i, 0)) for p in a_parts]
        + [pl.BlockSpec((n, k), lambda i: (0, 0))],
        out_specs=pl.BlockSpec((tm, n), lambda i: (i, 0)),
        out_shape=jax.ShapeDtypeStruct((t, n), F32),
        compiler_params=_cparams(dimension_semantics=("arbitrary",)),
    )(*a_parts, b)


def mm_tn(a, b_parts, name, tm=ROW_TILE, tn=1024, out_dtype=F32):
    t, k = a.shape
    widths = [p.shape[1] for p in b_parts]
    n = sum(widths)
    assert n % tn == 0
    groups, cur, acc = [], [], 0
    for idx, w in enumerate(widths):
        cur.append(idx)
        acc += w
        if acc == tn:
            groups.append(cur)
            cur, acc = [], 0
        assert acc < tn
    assert not cur
    outs = []
    for gi, grp in enumerate(groups):
        parts = [b_parts[i] for i in grp]
        npart = len(parts)
        nsteps = t // tm

        def body(*refs, npart=npart, nsteps=nsteps):
            a_v = refs[0][...].astype(BF16)
            b_v = _cat([r[...].astype(BF16) for r in refs[1:1 + npart]])
            o_ref, acc_ref = refs[1 + npart], refs[2 + npart]
            r = _dg(a_v, b_v, 0, 0)

            @pl.when(pl.program_id(0) == 0)
            def _():
                acc_ref[...] = r

            @pl.when(pl.program_id(0) != 0)
            def _():
                acc_ref[...] += r

            @pl.when(pl.program_id(0) == nsteps - 1)
            def _():
                o_ref[...] = acc_ref[...].astype(o_ref.dtype)

        outs.append(pl.pallas_call(
            body, name=f"{name}_{gi}", grid=(nsteps,),
            in_specs=[pl.BlockSpec((tm, k), lambda i: (i, 0))]
            + [pl.BlockSpec((tm, p.shape[1]), lambda i: (i, 0)) for p in parts],
            out_specs=pl.BlockSpec((k, tn), lambda i: (0, 0)),
            out_shape=jax.ShapeDtypeStruct((k, tn), out_dtype),
            scratch_shapes=[pltpu.VMEM((k, tn), F32)],
            compiler_params=_cparams(dimension_semantics=("arbitrary",)),
        )(a, *parts))
    return outs


def _seg_of(i, nct):
    return jnp.where(i < nct, 1, 0)


def rowwise_fwd(fn, name, rows, segs, globs, out_widths, tile, nct):
    t = rows[0].shape[0]
    nr, ns, ng = len(rows), len(segs), len(globs)

    def body(*refs):
        vals = [r[...] for r in refs[:nr]] + [r[0] for r in refs[nr:nr + ns]] + [r[...] for r in refs[nr + ns:nr + ns + ng]]
        outs = fn(*vals)
        for o_ref, o in zip(refs[nr + ns + ng:], outs):
            o_ref[...] = o

    return pl.pallas_call(
        body, name=name, grid=(t // tile,),
        in_specs=[pl.BlockSpec((tile, r.shape[1]), lambda i: (i, 0)) for r in rows]
        + [pl.BlockSpec((1, 1, s.shape[2]), lambda i: (_seg_of(i, nct), 0, 0)) for s in segs]
        + [pl.BlockSpec(g.shape, lambda i: (0, 0)) for g in globs],
        out_specs=[pl.BlockSpec((tile, w), lambda i: (i, 0)) for w in out_widths],
        out_shape=[jax.ShapeDtypeStruct((t, w), F32) for w in out_widths],
        compiler_params=_cparams(dimension_semantics=("arbitrary",)),
    )(*rows, *segs, *globs)


def rowwise_bwd(fn, name, rows, segs, globs, cts, tile, nct, row_diff, glob_diff):
    t = rows[0].shape[0]
    nr, ns, ng, nc = len(rows), len(segs), len(globs), len(cts)
    d_rows = [i for i in range(nr) if row_diff[i]]
    d_globs = [i for i in range(ng) if glob_diff[i]]

    def body(*refs):
        in_refs, out_refs = refs[:nr + ns + ng + nc], refs[nr + ns + ng + nc:]
        row_v = [r[...] for r in in_refs[:nr]]
        seg_v = [r[0] for r in in_refs[nr:nr + ns]]
        glob_v = [r[...] for r in in_refs[nr + ns:nr + ns + ng]]
        ct_v = tuple(r[...] for r in in_refs[nr + ns + ng:])

        def wrapped(dr, sv, dg):
            rv = list(row_v)
            for j, i in enumerate(d_rows):
                rv[i] = dr[j]
            gv = list(glob_v)
            for j, i in enumerate(d_globs):
                gv[i] = dg[j]
            return tuple(fn(*rv, *sv, *gv))

        _, vjp = jax.vjp(wrapped, [row_v[i] for i in d_rows], seg_v, [glob_v[i] for i in d_globs])
        c_rows, c_segs, c_globs = vjp(ct_v)
        i = pl.program_id(0)
        k = 0
        for c in c_rows:
            out_refs[k][...] = c
            k += 1
        seg_first = jnp.logical_or(i == 0, i == nct)
        for c in c_segs:
            ref = out_refs[k]
            k += 1

            @pl.when(seg_first)
            def _(ref=ref, c=c):
                ref[0] = c

            @pl.when(jnp.logical_not(seg_first))
            def _(ref=ref, c=c):
                ref[0] += c
        for c in c_globs:
            ref = out_refs[k]
            k += 1

            @pl.when(i == 0)
            def _(ref=ref, c=c):
                ref[...] = c

            @pl.when(i != 0)
            def _(ref=ref, c=c):
                ref[...] += c

    return pl.pallas_call(
        body, name=name, grid=(t // tile,),
        in_specs=[pl.BlockSpec((tile, r.shape[1]), lambda i: (i, 0)) for r in rows]
        + [pl.BlockSpec((1, 1, s.shape[2]), lambda i: (_seg_of(i, nct), 0, 0)) for s in segs]
        + [pl.BlockSpec(g.shape, lambda i: (0, 0)) for g in globs]
        + [pl.BlockSpec((tile, c.shape[1]), lambda i: (i, 0)) for c in cts],
        out_specs=[pl.BlockSpec((tile, rows[i].shape[1]), lambda i: (i, 0)) for i in d_rows]
        + [pl.BlockSpec((1, 1, s.shape[2]), lambda i: (_seg_of(i, nct), 0, 0)) for s in segs]
        + [pl.BlockSpec(globs[i].shape, lambda i: (0, 0)) for i in d_globs],
        out_shape=[jax.ShapeDtypeStruct(rows[i].shape, F32) for i in d_rows]
        + [jax.ShapeDtypeStruct(s.shape, F32) for s in segs]
        + [jax.ShapeDtypeStruct(globs[i].shape, F32) for i in d_globs],
        compiler_params=_cparams(dimension_semantics=("arbitrary",)),
    )(*rows, *segs, *globs, *cts)


def f_pre(x, mod, g_pre):
    shift, scale = mod[:, :D], mod[:, D:2 * D]
    rs = lax.rsqrt(jnp.mean(x * x, axis=-1, keepdims=True) + EPS)
    return ((x * rs) * g_pre * (1.0 + scale) + shift,)


def f_post(x, out, mod, g_post):
    gate = mod[:, 2 * D:]
    rs = lax.rsqrt(jnp.mean(out * out, axis=-1, keepdims=True) + EPS)
    return (x + gate * ((out * rs) * g_post),)


def f_mix(o_gla, o_na, y5, u5, pm, gcols, g_norm, s5_d, w_glu, b_glu, wpool, pool_scale, havg, e4):
    ms = mdot(o_gla * o_gla, havg)
    y_gla = o_gla * lax.rsqrt(ms + EPS) * jnp.sum(hdot(g_norm, e4), axis=0, keepdims=True)
    g = _gelu(u5 * s5_d + y5)
    y_s5 = g * jax.nn.sigmoid(bdot(g, w_glu) + b_glu)
    y_pool = bdot(pm, wpool) * pool_scale
    ycat = jnp.concatenate([y_gla, o_na, y_s5, y_pool], axis=-1)
    return (ycat * _silu(gcols),)


@jax.custom_vjp
def _rot_half16(x):
    lane = lax.broadcasted_iota(jnp.int32, x.shape, 1)
    first = jnp.bitwise_and(lane, 15) < 8
    return jnp.where(first, -pltpu.roll(x, x.shape[1] - 8, 1), pltpu.roll(x, 8, 1))


def _rot_fwd(x):
    return _rot_half16(x), None


def _rot_bwd(_, ct):
    return (-_rot_half16(ct),)


_rot_half16.defvjp(_rot_fwd, _rot_bwd)


def f_gla_prep(pk, pg, pq, cos, sin, wg, bg):
    z = bdot(pg, wg) + bg
    lg = _log_sigmoid(z) * (1.0 / 16.0)
    k_r = pk * cos + _rot_half16(pk) * sin
    q_r = (pq * cos + _rot_half16(pq) * sin) * (32.0 ** -0.5)
    return q_r, k_r, lg[:, :128], lg[:, 128:]


def _gla_consts(rev):
    c = GLA_CHUNK
    i = np.arange(c)
    inc = (i[None, :] >= i[:, None]) if rev else (i[None, :] <= i[:, None])
    mq = np.stack([(np.arange(128) // 32 == h) for h in range(4)]).astype(np.float32).reshape(4, 1, 128)
    mv = np.stack([(np.arange(256) // 64 == h) for h in range(4)]).astype(np.float32).reshape(4, 1, 256)
    bdt = (np.arange(256)[:, None] // 64 == np.arange(128)[None, :] // 32).astype(np.float32)
    inc = inc.astype(np.float32)
    return jnp.asarray(inc), jnp.asarray(inc.T.copy()), jnp.asarray(mq), jnp.asarray(mv), jnp.asarray(bdt)


def _stack_heads(x, m_ref):
    return jnp.concatenate([x * m_ref[h] for h in range(4)], axis=0)


def _tile4(m):
    return jnp.concatenate([m, m, m, m], axis=0)


def _fold_heads(r4, m_ref):
    r = r4.shape[0] // 4
    out = m_ref[0] * r4[0:r]
    for h in range(1, 4):
        out = out + m_ref[h] * r4[h * r:(h + 1) * r]
    return out


def _gla_chunk_of(s, n_ctx_chunks, n_chunks, rev):
    if not rev:
        return s
    return jnp.where(s < n_ctx_chunks, n_ctx_chunks - 1 - s, n_ctx_chunks + n_chunks - 1 - s)


def gla_scan_fwd(q, k, v, lg, acc, n_ctx_rows, rev, name, comm=None):
    t = q.shape[0]
    nch, ncc = t // GLA_CHUNK, n_ctx_rows // GLA_CHUNK
    inc, inc_t, mq, mv, bdt = _gla_consts(rev)

    def body(q_ref, k_ref, v_ref, lg_ref, acc_ref, inc_ref, inct_ref, mq_ref, mv_ref, bdt_ref, o_ref, st_ref):
        lmask, lmask_t = inc_ref[...], inct_ref[...]
        bd = bdt_ref[...]

        def step(s, st):
            c = _gla_chunk_of(s, ncc, nch, rev)
            rows = pl.ds(pl.multiple_of(c * GLA_CHUNK, GLA_CHUNK), GLA_CHUNK)
            qc, kc, vc, lgc = q_ref[rows, :], k_ref[rows, :], v_ref[rows, :], lg_ref[rows, :]
            st_ref[c] = st
            b = hdot(lmask, lgc)
            blast = jnp.sum(lgc, axis=0, keepdims=True)
            qe, ke, kd = qc * jnp.exp(b), kc * jnp.exp(-b), kc * jnp.exp(blast - b)
            ke4, v4 = _stack_heads(ke, mq_ref), _stack_heads(vc, mv_ref)
            at = _tile4(lmask_t) * b_nt(ke4, qe)
            o_ref[rows, :] = acc_ref[rows, :] + b_nt(qe, st) + b_tn(at, v4)
            return st * jnp.exp(blast) + bd * mdot_tn(vc, kd)

        lax.fori_loop(0, nch, step, jnp.zeros((256, 128), F32))

    return _call_with_exchange(body, name, [q, k, v, lg, acc, inc, inc_t, mq, mv, bdt],
                               [jax.ShapeDtypeStruct((t, 256), F32), jax.ShapeDtypeStruct((nch, 256, 128), F32)], comm)


def gla_scan_bwd(q, k, v, lg, st, do, acc, n_ctx_rows, rev, name, comm=None):
    t = q.shape[0]
    nch, ncc = t // GLA_CHUNK, n_ctx_rows // GLA_CHUNK
    inc, inc_t, mq, mv, bdt = _gla_consts(rev)

    def body(q_ref, k_ref, v_ref, lg_ref, st_ref, do_ref, aq_ref, ak_ref, av_ref, inc_ref, inct_ref, mq_ref, mv_ref, bdt_ref,
             dq_ref, dk_ref, dv_ref, dlg_ref):
        lmask, lmask_t = inc_ref[...], inct_ref[...]
        bd = bdt_ref[...]

        def step(j, carry):
            dst, gsum = carry
            s = nch - 1 - j
            c = _gla_chunk_of(s, ncc, nch, rev)
            rows = pl.ds(pl.multiple_of(c * GLA_CHUNK, GLA_CHUNK), GLA_CHUNK)
            qc, kc, vc, lgc, doc = q_ref[rows, :], k_ref[rows, :], v_ref[rows, :], lg_ref[rows, :], do_ref[rows, :]
            stc = st_ref[c]
            b = hdot(lmask, lgc)
            blast = jnp.sum(lgc, axis=0, keepdims=True)
            eb, enb, edb = jnp.exp(b), jnp.exp(-b), jnp.exp(blast - b)
            qe, ke, kd = qc * eb, kc * enb, kc * edb
            ke4, v4 = _stack_heads(ke, mq_ref), _stack_heads(vc, mv_ref)
            lm4 = _tile4(lmask_t)
            at = lm4 * b_nt(ke4, qe)
            dat = lm4 * mdot_nt(v4, doc)
            dqe = mdot(doc, stc) + mdot_tn(dat, ke4)
            dke = _fold_heads(mdot(dat, qe), mq_ref)
            dv = b_nt(kd, dst) + _fold_heads(b_nn(at, doc), mv_ref)
            dkd = mdot(vc, dst)
            dq = dqe * eb
            dk = dke * enb + dkd * edb
            g = qc * dq - kc * dk
            dlg_ref[rows, :] = hdot_tn(lmask, g) + gsum
            dq_ref[rows, :] = aq_ref[rows, :] + dq
            dk_ref[rows, :] = ak_ref[rows, :] + dk
            dv_ref[rows, :] = av_ref[rows, :] + dv
            dst_new = dst * jnp.exp(blast) + bd * mdot_tn(doc, qe)
            return dst_new, gsum + jnp.sum(g, axis=0, keepdims=True)

        lax.fori_loop(0, nch, step, (jnp.zeros((256, 128), F32), jnp.zeros((1, 128), F32)))

    return _call_with_exchange(body, name, [q, k, v, lg, st, do, *acc, inc, inc_t, mq, mv, bdt],
                               [jax.ShapeDtypeStruct((t, 128), F32), jax.ShapeDtypeStruct((t, 128), F32),
                                jax.ShapeDtypeStruct((t, 256), F32), jax.ShapeDtypeStruct((t, 128), F32)], comm)


def whole_fwd(fn, name, args, out_shapes):
    def body(*refs):
        outs = fn(*[r[...] for r in refs[:len(args)]])
        for o_ref, o in zip(refs[len(args):], outs):
            o_ref[...] = o

    vm = pl.BlockSpec(memory_space=pltpu.VMEM)
    return pl.pallas_call(
        body, name=name, in_specs=[vm] * len(args), out_specs=[vm] * len(out_shapes),
        out_shape=[jax.ShapeDtypeStruct(s, F32) for s in out_shapes], compiler_params=_cparams(),
    )(*args)


def whole_bwd(fn, name, args, cts, diff):
    d_idx = [i for i in range(len(args)) if diff[i]]

    def body(*refs):
        vals = [r[...] for r in refs[:len(args)]]
        ct_v = tuple(r[...] for r in refs[len(args):len(args) + len(cts)])

        def wrapped(dv):
            av = list(vals)
            for j, i in enumerate(d_idx):
                av[i] = dv[j]
            return tuple(fn(*av))

        _, vjp = jax.vjp(wrapped, [vals[i] for i in d_idx])
        (c_args,) = vjp(ct_v)
        for o_ref, c in zip(refs[len(args) + len(cts):], c_args):
            o_ref[...] = c

    vm = pl.BlockSpec(memory_space=pltpu.VMEM)
    return pl.pallas_call(
        body, name=name, in_specs=[vm] * (len(args) + len(cts)), out_specs=[vm] * len(d_idx),
        out_shape=[jax.ShapeDtypeStruct(args[i].shape, F32) for i in d_idx], compiler_params=_cparams(),
    )(*args, *cts)


def _s5_consts():
    e_rep = (np.arange(256)[:, None] // 16 == np.arange(16)[None, :]).astype(np.float32)
    e_tile = (np.arange(64)[:, None] == np.arange(1024)[None, :] % 64).astype(np.float32)
    gmask = (np.arange(16)[:, None] == np.arange(1024)[None, :] // 64).astype(np.float32)
    bdm = (np.arange(256)[:, None] // 16 == np.arange(1024)[None, :] // 64).astype(np.float32)
    return jnp.asarray(e_rep), jnp.asarray(e_tile), jnp.asarray(gmask), jnp.asarray(bdm)


def f_s5_params(lam_re, lam_im, log_dt, bt_re, bt_im, ct_re, ct_im, e_rep, e_tile, gmask, bdm):
    dt = jnp.exp(log_dt)
    mag = jnp.exp(lam_re * dt)
    ang = lam_im * dt
    lb_re, lb_im = mag * jnp.cos(ang), mag * jnp.sin(ang)
    num_re, num_im = lb_re - 1.0, lb_im
    den = lam_re * lam_re + lam_im * lam_im
    coef_re = (num_re * lam_re + num_im * lam_im) / den
    coef_im = (num_im * lam_re - num_re * lam_im) / den
    cr, ci = hdot(e_rep, coef_re), hdot(e_rep, coef_im)
    bbt_re = cr * bt_re - ci * bt_im
    bbt_im = cr * bt_im + ci * bt_re
    a_re = jnp.sum(hdot(lb_re, e_tile) * gmask, axis=0, keepdims=True)
    a_im = jnp.sum(hdot(lb_im, e_tile) * gmask, axis=0, keepdims=True)
    return (a_re, a_im, hdot(bbt_re, e_tile) * bdm, hdot(bbt_im, e_tile) * bdm,
            hdot(ct_re, e_tile) * bdm, hdot(ct_im, e_tile) * bdm)


def _s5_doubling(xr, xi, pr, pi, pos, n, steps, rev):
    rows = xr.shape[0]
    for s in steps:
        if rev:
            keep = pos < (n - s)
            sr, si = pltpu.roll(xr, rows - s, 0), pltpu.roll(xi, rows - s, 0)
        else:
            keep = pos >= s
            sr, si = pltpu.roll(xr, s, 0), pltpu.roll(xi, s, 0)
        sr, si = jnp.where(keep, sr, 0.0), jnp.where(keep, si, 0.0)
        xr, xi = xr + pr * sr - pi * si, xi + pr * si + pi * sr
        pr, pi = pr * pr - pi * pi, 2.0 * pr * pi
    return xr, xi, pr, pi


SUBLANES = 8


def _s5_scan(xr, xi, a_re, a_im, rev, chunk, scr):
    xs_r, xs_i, yp_r, yp_i = scr
    ng = chunk // SUBLANES
    x3r, x3i = xr.reshape(ng, SUBLANES, 1024), xi.reshape(ng, SUBLANES, 1024)
    sub = lax.broadcasted_iota(jnp.int32, (SUBLANES, 1024), 0)
    a8r, a8i = a_re, a_im
    for s in (1, 2, 4):
        keep = sub < (SUBLANES - s) if rev else sub >= s
        mr, mi = jnp.where(keep, a8r, 0.0)[None], jnp.where(keep, a8i, 0.0)[None]
        shift = SUBLANES - s if rev else s
        sr, si = pltpu.roll(x3r, shift, 1), pltpu.roll(x3i, shift, 1)
        x3r, x3i = x3r + mr * sr - mi * si, x3i + mr * si + mi * sr
        a8r, a8i = a8r * a8r - a8i * a8i, 2.0 * a8r * a8i
    xr, xi = x3r.reshape(chunk, 1024), x3i.reshape(chunk, 1024)
    nblk = 1024 // 128
    for j in range(nblk):
        xs_r[j] = xr[:, 128 * j:128 * (j + 1)]
        xs_i[j] = xi[:, 128 * j:128 * (j + 1)]
    edge = pl.ds(0 if rev else SUBLANES - 1, ng, stride=SUBLANES)
    gr = jnp.concatenate([xs_r[j, edge, :] for j in range(nblk)], axis=-1)
    gi = jnp.concatenate([xs_i[j, edge, :] for j in range(nblk)], axis=-1)
    grow = lax.broadcasted_iota(jnp.int32, (ng, 1024), 0)
    steps = tuple(1 << k for k in range((ng - 1).bit_length()))
    gr, gi, _, _ = _s5_doubling(gr, gi, a8r, a8i, grow, ng, steps, rev)
    if rev:
        yp_r[...] = jnp.where(grow < ng - 1, pltpu.roll(gr, ng - 1, 0), 0.0)
        yp_i[...] = jnp.where(grow < ng - 1, pltpu.roll(gi, ng - 1, 0), 0.0)
    else:
        yp_r[...] = jnp.where(grow >= 1, pltpu.roll(gr, 1, 0), 0.0)
        yp_i[...] = jnp.where(grow >= 1, pltpu.roll(gi, 1, 0), 0.0)
    sub = lax.broadcasted_iota(jnp.int32, (SUBLANES, 1024), 0)
    tr, ti = jnp.zeros((SUBLANES, 1024), F32), jnp.zeros((SUBLANES, 1024), F32)
    cr, ci = a_re, a_im
    for n in range(1, SUBLANES + 1):
        r = SUBLANES - n if rev else n - 1
        tr, ti = jnp.where(sub == r, cr, tr), jnp.where(sub == r, ci, ti)
        cr, ci = cr * a_re - ci * a_im, cr * a_im + ci * a_re
    for j in range(nblk):
        lanes = slice(128 * j, 128 * (j + 1))
        tr_j, ti_j = tr[:, lanes], ti[:, lanes]
        for g in range(ng):
            rows = slice(g * SUBLANES, (g + 1) * SUBLANES)
            er, ei = yp_r[g:g + 1, lanes], yp_i[g:g + 1, lanes]
            xs_r[j, rows, :] = xs_r[j, rows, :] + tr_j * er - ti_j * ei
            xs_i[j, rows, :] = xs_i[j, rows, :] + tr_j * ei + ti_j * er
    return (jnp.concatenate([xs_r[j] for j in range(nblk)], axis=-1),
            jnp.concatenate([xs_i[j] for j in range(nblk)], axis=-1))


def _s5_scratch(chunk):
    return [pltpu.VMEM((8, chunk, 128), F32), pltpu.VMEM((8, chunk, 128), F32),
            pltpu.VMEM((chunk // SUBLANES, 1024), F32), pltpu.VMEM((chunk // SUBLANES, 1024), F32)]


def _s5_chunk_states(u_c, x0r, x0i, a_re, a_im, bb_re, bb_im, rev, chunk, scr):
    row = lax.broadcasted_iota(jnp.int32, (chunk, 1024), 0)
    first = row == (chunk - 1 if rev else 0)
    inj_r = a_re * x0r - a_im * x0i
    inj_i = a_re * x0i + a_im * x0r
    xr = b_nn(u_c, bb_re) + jnp.where(first, inj_r, 0.0)
    xi = b_nn(u_c, bb_im) + jnp.where(first, inj_i, 0.0)
    return _s5_scan(xr, xi, a_re, a_im, rev, chunk, scr)


def _row_pick(x, idx):
    row = lax.broadcasted_iota(jnp.int32, x.shape, 0)
    return jnp.sum(jnp.where(row == idx, x, 0.0), axis=0, keepdims=True)


def s5_scan_fwd(u, acc, a_re, a_im, bb_re, bb_im, cc_re, cc_im, n_ctx_rows, chunk, rev, name, comm=None):
    t = u.shape[0]
    nch, ncc = t // chunk, n_ctx_rows // chunk

    def body(u_ref, acc_ref, ar_ref, ai_ref, br_ref, bi_ref, cr_ref, ci_ref, y_ref, x0r_ref, x0i_ref, xsr_ref, xsi_ref, *scr):
        a_r, a_i = ar_ref[...], ai_ref[...]

        def step(s, carry):
            x0r, x0i = carry
            c = _gla_chunk_of(s, ncc, nch, rev)
            rows = pl.ds(pl.multiple_of(c * chunk, chunk), chunk)
            x0r_ref[c] = x0r
            x0i_ref[c] = x0i
            xr, xi = _s5_chunk_states(u_ref[rows, :], x0r, x0i, a_r, a_i, br_ref[...], bi_ref[...], rev, chunk, scr)
            y_ref[rows, :] = acc_ref[rows, :] + b_nt(xr, cr_ref[...]) - b_nt(xi, ci_ref[...])
            xsr_ref[rows, :] = xr.astype(BF16)
            xsi_ref[rows, :] = xi.astype(BF16)
            last = 0 if rev else chunk - 1
            return _row_pick(xr, last), _row_pick(xi, last)

        lax.fori_loop(0, nch, step, (jnp.zeros((1, 1024), F32), jnp.zeros((1, 1024), F32)))

    return _call_with_exchange(
        body, name, [u, acc, a_re, a_im, bb_re, bb_im, cc_re, cc_im],
        [jax.ShapeDtypeStruct((t, 256), F32), jax.ShapeDtypeStruct((nch, 1, 1024), F32),
         jax.ShapeDtypeStruct((nch, 1, 1024), F32), jax.ShapeDtypeStruct((t, 1024), BF16),
         jax.ShapeDtypeStruct((t, 1024), BF16)], comm, _s5_scratch(chunk))


def s5_scan_bwd(u, dy, du_acc, x0r, x0i, xsr, xsi, a_re, a_im, bb_re, bb_im, cc_re, cc_im, n_ctx_rows, chunk, rev, name):
    t = u.shape[0]
    nch, ncc = t // chunk, n_ctx_rows // chunk

    def body(u_ref, dy_ref, dua_ref, x0r_ref, x0i_ref, xsr_ref, xsi_ref, ar_ref, ai_ref, br_ref, bi_ref, cr_ref, ci_ref,
             du_ref, dar_ref, dai_ref, dbr_ref, dbi_ref, dcr_ref, dci_ref, *scr):
        a_r, a_i = ar_ref[...], ai_ref[...]
        for ref in (dbr_ref, dbi_ref, dcr_ref, dci_ref):
            ref[...] = jnp.zeros_like(ref)
        row = lax.broadcasted_iota(jnp.int32, (chunk, 1024), 0)
        first_idx, last_idx = (chunk - 1, 0) if rev else (0, chunk - 1)

        def step(j, carry):
            lcr, lci, dar, dai = carry
            s = nch - 1 - j
            c = _gla_chunk_of(s, ncc, nch, rev)
            rows = pl.ds(pl.multiple_of(c * chunk, chunk), chunk)
            u_c, dy_c = u_ref[rows, :], dy_ref[rows, :]
            x0r_c, x0i_c = x0r_ref[c], x0i_ref[c]
            xr, xi = xsr_ref[rows, :].astype(F32), xsi_ref[rows, :].astype(F32)
            dcr_ref[...] += b_tn(dy_c, xr)
            dci_ref[...] -= b_tn(dy_c, xi)
            inj_r = a_r * lcr + a_i * lci
            inj_i = a_r * lci - a_i * lcr
            is_last = row == last_idx
            lr = b_nn(dy_c, cr_ref[...]) + jnp.where(is_last, inj_r, 0.0)
            li = -b_nn(dy_c, ci_ref[...]) + jnp.where(is_last, inj_i, 0.0)
            lr, li = _s5_scan(lr, li, a_r, -a_i, not rev, chunk, scr)
            du_ref[rows, :] = dua_ref[rows, :] + b_nt(lr, br_ref[...]) + b_nt(li, bi_ref[...])
            dbr_ref[...] += b_tn(u_c, lr)
            dbi_ref[...] += b_tn(u_c, li)
            if rev:
                pr, pi = pltpu.roll(xr, chunk - 1, 0), pltpu.roll(xi, chunk - 1, 0)
            else:
                pr, pi = pltpu.roll(xr, 1, 0), pltpu.roll(xi, 1, 0)
            is_first = row == first_idx
            pr, pi = jnp.where(is_first, x0r_c, pr), jnp.where(is_first, x0i_c, pi)
            dar = dar + jnp.sum(lr * pr + li * pi, axis=0, keepdims=True)
            dai = dai + jnp.sum(li * pr - lr * pi, axis=0, keepdims=True)
            return _row_pick(lr, first_idx), _row_pick(li, first_idx), dar, dai

        z = jnp.zeros((1, 1024), F32)
        _, _, dar, dai = lax.fori_loop(0, nch, step, (z, z, z, z))
        dar_ref[...] = dar
        dai_ref[...] = dai

    vm = pl.BlockSpec(memory_space=pltpu.VMEM)
    big = jax.ShapeDtypeStruct((256, 1024), F32)
    vec = jax.ShapeDtypeStruct((1, 1024), F32)
    return pl.pallas_call(
        body, name=name, in_specs=[vm] * 13, out_specs=[vm] * 7,
        out_shape=[jax.ShapeDtypeStruct((t, 256), F32), vec, vec, big, big, big, big],
        scratch_shapes=_s5_scratch(chunk), compiler_params=_cparams(),
    )(u, dy, du_acc, x0r, x0i, xsr, xsi, a_re, a_im, bb_re, bb_im, cc_re, cc_im)


POOL_HALO = 8


def pool_apply(u_pad, n, transpose, name, tile=ROW_TILE):
    tile = min(tile, n)
    ext = tile + 2 * POOL_HALO
    trel = np.arange(ext)[None, :] - POOL_HALO - np.arange(tile)[:, None]
    if transpose:
        trel = -trel
    band4 = np.concatenate([((trel >= -(1 << w)) & (trel <= (1 << w) - 1)) for w in range(4)], axis=0).astype(np.float32)

    def body(u_ref, band_ref, lm_ref, o_ref):
        lax.fori_loop(0, n // tile, functools.partial(step, u_ref, band_ref, lm_ref, o_ref), 0)

    def step(u_ref, band_ref, lm_ref, o_ref, i, carry):
        val = u_ref[pl.ds(pl.multiple_of(i * tile, tile), ext), :]
        lane = lax.broadcasted_iota(jnp.int32, (ext, 256), 1)
        half = jnp.left_shift(1, jnp.right_shift(lane, 6))
        trow = lax.broadcasted_iota(jnp.int32, (ext, 256), 0) + (i * tile - POOL_HALO)
        cnt = jnp.minimum(trow + half, n) - jnp.maximum(trow - half, 0)
        inv = 1.0 / jnp.maximum(cnt, 1).astype(F32)
        src = val * inv if transpose else val
        acc = _fold_heads(mdot(band_ref[...], src), lm_ref)
        centre = val[POOL_HALO:POOL_HALO + tile]
        if not transpose:
            acc = acc * inv[POOL_HALO:POOL_HALO + tile]
        o_ref[pl.ds(pl.multiple_of(i * tile, tile), tile), :] = acc - centre
        return carry

    vm = pl.BlockSpec(memory_space=pltpu.VMEM)
    return pl.pallas_call(
        body, name=name, in_specs=[vm] * 3, out_specs=vm,
        out_shape=jax.ShapeDtypeStruct((n, 256), F32), compiler_params=_cparams(),
    )(u_pad, jnp.asarray(band4), _na_head_masks())


NA_SCALE = 64.0 ** -0.5
NEG = -1e30


def _call_with_exchange(compute, name, args, out_shapes, comm, scratch=()):
    vm = pl.BlockSpec(memory_space=pltpu.VMEM)
    n_in, n_out = len(args), len(out_shapes)
    if comm is None:
        outs = pl.pallas_call(compute, name=name, in_specs=[vm] * n_in, out_specs=[vm] * n_out, out_shape=out_shapes,
                              scratch_shapes=list(scratch), compiler_params=_cparams())(*args)
        return outs, None
    arrays, scatter = comm
    n = len(arrays)

    def body(*refs):
        c_in = refs[n_in:n_in + n]
        c_out = refs[n_in + n + n_out:n_in + 2 * n + n_out]
        scr = refs[n_in + 2 * n + n_out:n_in + 2 * n + n_out + len(scratch)]
        finish = _exchange_issue(c_in, c_out, scatter, *refs[n_in + 2 * n + n_out + len(scratch):])
        compute(*refs[:n_in], *refs[n_in + n:n_in + n + n_out], *scr)
        finish()

    hbm = pl.BlockSpec(memory_space=pl.ANY)
    outs = pl.pallas_call(
        body, name=name, in_specs=[vm] * n_in + [hbm] * n, out_specs=[vm] * n_out + [hbm] * n,
        out_shape=list(out_shapes) + _exchange_out_shapes(arrays, scatter), scratch_shapes=list(scratch) + _exchange_sems(n),
        compiler_params=_cparams(has_side_effects=True),
    )(*args, *arrays)
    return outs[:n_out], outs[n_out:]


def _na_head_masks():
    return jnp.asarray(np.stack([(np.arange(256) // 64 == h) for h in range(4)]).astype(np.float32).reshape(4, 1, 256))


def _na_window(r, rows):
    start = jnp.clip(r - 4, 0, rows - 8)
    return start, start - r + 7


def _na_probs(qh, kw, kc, bias):
    s_c = b_nt(qh, kc)
    m = jnp.max(s_c, axis=-1, keepdims=True)
    if kw is not None:
        s_w = b_nt(qh, kw) + bias
        m = jnp.maximum(m, jnp.max(s_w, axis=-1, keepdims=True))
        p_w = jnp.exp(s_w - m)
    p_c = jnp.exp(s_c - m)
    l = jnp.sum(p_c, axis=-1, keepdims=True)
    if kw is not None:
        l = l + jnp.sum(p_w, axis=-1, keepdims=True)
        return p_w / l, p_c / l
    return None, p_c / l


def na_fwd(q, k, v, bias8, n_ctx_rows, name, comm=None):
    t = q.shape[0]
    m_ctx = n_ctx_rows
    rows = (t - m_ctx) // GRID_W
    hm = _na_head_masks()

    def body(q_ref, k_ref, v_ref, b_ref, hm_ref, o_ref):
        kc, vc = k_ref[0:m_ctx, :], v_ref[0:m_ctx, :]

        def ctx_step(i, _):
            rs = pl.ds(pl.multiple_of(i * 64, 64), 64)
            q4 = _stack_heads(q_ref[rs, :] * NA_SCALE, hm_ref)
            _, p_c = _na_probs(q4, None, kc, None)
            o_ref[rs, :] = _fold_heads(b_nn(p_c, vc), hm_ref)
            return 0

        lax.fori_loop(0, m_ctx // 64, ctx_step, 0)

        def lat_step(r, _):
            start, off = _na_window(r, rows)
            rs = pl.ds(pl.multiple_of(m_ctx + r * 64, 64), 64)
            ws = pl.ds(pl.multiple_of(m_ctx + start * 64, 64), 512)
            q4 = _stack_heads(q_ref[rs, :] * NA_SCALE, hm_ref)
            kw, vw = k_ref[ws, :], v_ref[ws, :]
            p_w, p_c = _na_probs(q4, kw, kc, b_ref[off])
            o_ref[rs, :] = _fold_heads(b_nn(p_w, vw) + b_nn(p_c, vc), hm_ref)
            return 0

        lax.fori_loop(0, rows, lat_step, 0)

    (o,), received = _call_with_exchange(body, name, [q, k, v, bias8, hm], [jax.ShapeDtypeStruct((t, 256), F32)], comm)
    return o if comm is None else (o, received)


def na_bwd(q, k, v, do, bias8, n_ctx_rows, name, comm=None):
    t = q.shape[0]
    m_ctx = n_ctx_rows
    rows = (t - m_ctx) // GRID_W
    hm = _na_head_masks()

    def body(q_ref, k_ref, v_ref, do_ref, b_ref, hm_ref, dq_ref, dk_ref, dv_ref, db_ref):
        kc, vc = k_ref[0:m_ctx, :], v_ref[0:m_ctx, :]
        dk_ref[...] = jnp.zeros_like(dk_ref)
        dv_ref[...] = jnp.zeros_like(dv_ref)
        db_ref[...] = jnp.zeros_like(db_ref)

        def head_terms(qh, doh, kw, vw, bias):
            p_w, p_c = _na_probs(qh, kw, kc, bias)
            dp_c = b_nt(doh, vc)
            delta = jnp.sum(p_c * dp_c, axis=-1, keepdims=True)
            if kw is not None:
                dp_w = b_nt(doh, vw)
                delta = delta + jnp.sum(p_w * dp_w, axis=-1, keepdims=True)
                ds_w = p_w * (dp_w - delta)
            else:
                ds_w = None
            ds_c = p_c * (dp_c - delta)
            return p_w, p_c, ds_w, ds_c

        def ctx_step(i, carry):
            dkc, dvc = carry
            rs = pl.ds(pl.multiple_of(i * 64, 64), 64)
            q4, do4 = _stack_heads(q_ref[rs, :] * NA_SCALE, hm_ref), _stack_heads(do_ref[rs, :], hm_ref)
            _, p_c, _, ds_c = head_terms(q4, do4, None, None, None)
            dq_ref[rs, :] = _fold_heads(b_nn(ds_c, kc), hm_ref) * NA_SCALE
            return dkc + b_tn(ds_c, q4), dvc + b_tn(p_c, do4)

        zc = jnp.zeros((m_ctx, 256), F32)
        carry = lax.fori_loop(0, m_ctx // 64, ctx_step, (zc, zc))

        def lat_step(r, carry):
            dkc, dvc = carry
            start, off = _na_window(r, rows)
            rs = pl.ds(pl.multiple_of(m_ctx + r * 64, 64), 64)
            ws = pl.ds(pl.multiple_of(m_ctx + start * 64, 64), 512)
            q4, do4 = _stack_heads(q_ref[rs, :] * NA_SCALE, hm_ref), _stack_heads(do_ref[rs, :], hm_ref)
            kw, vw = k_ref[ws, :], v_ref[ws, :]
            p_w, p_c, ds_w, ds_c = head_terms(q4, do4, kw, vw, b_ref[off])
            dq_ref[rs, :] = _fold_heads(b_nn(ds_w, kw) + b_nn(ds_c, kc), hm_ref) * NA_SCALE
            dk_ref[ws, :] += b_tn(ds_w, q4)
            dv_ref[ws, :] += b_tn(p_w, do4)
            db_ref[off] += ds_w
            return dkc + b_tn(ds_c, q4), dvc + b_tn(p_c, do4)

        dkc, dvc = lax.fori_loop(0, rows, lat_step, carry)
        dk_ref[0:m_ctx, :] = dkc
        dv_ref[0:m_ctx, :] = dvc

    row = jax.ShapeDtypeStruct((t, 256), F32)
    return _call_with_exchange(body, name, [q, k, v, do, bias8, hm], [row, row, row, jax.ShapeDtypeStruct(bias8.shape, F32)], comm)


def _na_toeplitz():
    col = np.arange(GRID_W)
    dd = (col[None, :] - col[:, None] + 15).reshape(-1)
    tt = np.zeros((GRID_W * GRID_W, 128), np.float32)
    ok = (dd >= 0) & (dd <= 30)
    tt[np.arange(GRID_W * GRID_W)[ok], dd[ok]] = 1.0
    return tt


def _na_bias8(rpb, name):
    col = np.arange(GRID_W)
    cs = np.clip(col - 8, 0, GRID_W - 16)
    col_mask = (col[None, :] >= cs[:, None]) & (col[None, :] < cs[:, None] + 16)
    rpb2 = jnp.pad(rpb.reshape(60, 31), ((0, 4), (0, 97)))
    (toe,) = whole_fwd(lambda r_, t_: (hdot_nt(r_, t_),), name, [rpb2, jnp.asarray(_na_toeplitz())], [(64, GRID_W * GRID_W)])
    toe = toe[:60].reshape(4, 15, GRID_W, GRID_W)
    b = jnp.stack([toe[:, off:off + 8] for off in range(8)], axis=1)
    b = jnp.where(jnp.asarray(col_mask)[None, None, None], b, NEG)
    return b.transpose(1, 0, 3, 2, 4).reshape(8, 4 * GRID_W, 8 * GRID_W)


def _na_rpb_grad(dbias8, name):
    tt = _na_toeplitz()
    sel = np.zeros((64, 256), np.float32)
    for h in range(4):
        for off in range(8):
            for i in range(8):
                sel[h * 15 + off + i, h * 64 + off * 8 + i] = 1.0
    a2 = dbias8.reshape(8, 4, GRID_W, 8, GRID_W).transpose(1, 0, 3, 2, 4).reshape(256, GRID_W * GRID_W)
    (out,) = whole_fwd(lambda a, t_, s_: (hdot(s_, hdot(a, t_)),), name, [a2, jnp.asarray(tt), jnp.asarray(sel)], [(64, 128)])
    return out[:60, :31].reshape(4, 15, 31)


def f_mod(cs, b_mod, w_mod):
    s = _silu(cs)
    return bdot(s, w_mod) + b_mod, s


def loss_and_grad(z, tgt, n_ctx_rows, name, tile=ROW_TILE):
    t, d = z.shape
    tile = min(tile, n_ctx_rows)
    nct = n_ctx_rows // tile

    def body(z_ref, t_ref, dz_ref, loss_ref):
        i = pl.program_id(0)

        @pl.when(i == 0)
        def _():
            loss_ref[...] = jnp.zeros_like(loss_ref)

        @pl.when(i < nct)
        def _():
            dz_ref[...] = jnp.zeros_like(dz_ref)

        @pl.when(i >= nct)
        def _():
            diff = z_ref[...] - t_ref[...]
            dz_ref[...] = diff * (1.0 / d)
            loss_ref[...] += 0.5 * jnp.sum(jnp.sum(diff * diff, axis=-1, keepdims=True) * (1.0 / d), axis=0, keepdims=True)

    dz, loss = pl.pallas_call(
        body, name=name, grid=(t // tile,),
        in_specs=[pl.BlockSpec((tile, d), lambda i: (i, 0)),
                  pl.BlockSpec((tile, d), lambda i: (jnp.maximum(i - nct, 0), 0))],
        out_specs=[pl.BlockSpec((tile, d), lambda i: (i, 0)), pl.BlockSpec((8, 128), lambda i: (0, 0))],
        out_shape=[jax.ShapeDtypeStruct((t, d), F32), jax.ShapeDtypeStruct((8, 128), F32)],
        compiler_params=_cparams(dimension_semantics=("arbitrary",)),
    )(z, tgt)
    return loss[0, 0], dz


def adamw(parts, w, m, v, name, tile=256):
    npart, r, c = parts.shape
    tile = min(tile, r)
    assert r % tile == 0
    c1 = 1.0 / (1.0 - ADAM_B1 ** ADAM_STEP)
    c2 = 1.0 / (1.0 - ADAM_B2 ** ADAM_STEP)

    def body(p_ref, w_ref, m_ref, v_ref, g_ref, d_ref, nm_ref, nv_ref):
        g = p_ref[0].astype(F32)
        for i in range(1, npart):
            g = g + p_ref[i].astype(F32)
        nm = ADAM_B1 * m_ref[...] + (1.0 - ADAM_B1) * g
        nv = ADAM_B2 * v_ref[...] + (1.0 - ADAM_B2) * (g * g)
        g_ref[...] = g
        nm_ref[...] = nm
        nv_ref[...] = nv
        d_ref[...] = -ADAM_LR * ((nm * c1) / (jnp.sqrt(nv * c2) + ADAM_EPS) + ADAM_WD * w_ref[...])

    blk = pl.BlockSpec((tile, c), lambda i: (i, 0))
    return pl.pallas_call(
        body, name=name, grid=(r // tile,),
        in_specs=[pl.BlockSpec((npart, tile, c), lambda i: (0, i, 0)), blk, blk, blk],
        out_specs=[blk] * 4, out_shape=[jax.ShapeDtypeStruct((r, c), F32)] * 4,
        compiler_params=_cparams(dimension_semantics=("arbitrary",)),
    )(parts, w, m, v)


def _peer(x, y, c, k):
    return (1 - x if k & 4 else x, 1 - y if k & 2 else y, 1 - c if k & 1 else c)


def _exchange_out_shapes(arrays, scatter):
    return [jax.ShapeDtypeStruct(a.shape if s else (N_DEV,) + a.shape, a.dtype) for a, s in zip(arrays, scatter)]


def _exchange_sems(n):
    return [pltpu.SemaphoreType.DMA((n, N_DEV - 1)), pltpu.SemaphoreType.DMA((n, N_DEV - 1)), pltpu.SemaphoreType.DMA((n,))]


def _exchange_issue(ins, outs, scatter, send_sems, recv_sems, local_sems):
    n = len(ins)
    x, y, c = lax.axis_index("x"), lax.axis_index("y"), lax.axis_index("c")
    me = 4 * x + 2 * y + c

    def index_of(p):
        return 4 * p[0] + 2 * p[1] + p[2]

    local = []
    for a in range(n):
        src_me = ins[a].at[me] if scatter[a] else ins[a]
        loc = pltpu.make_async_copy(src_me, outs[a].at[me], local_sems.at[a])
        loc.start()
        local.append(loc)
    for k in range(1, N_DEV):
        peer = _peer(x, y, c, k)
        for a in range(n):
            src = ins[a].at[index_of(peer)] if scatter[a] else ins[a]
            pltpu.make_async_remote_copy(
                src_ref=src, dst_ref=outs[a].at[me], send_sem=send_sems.at[a, k - 1], recv_sem=recv_sems.at[a, k - 1],
                device_id=peer, device_id_type=pl.DeviceIdType.MESH).start()

    def finish():
        for k in range(1, N_DEV):
            peer = _peer(x, y, c, k)
            for a in range(n):
                src = ins[a].at[index_of(peer)] if scatter[a] else ins[a]
                cp = pltpu.make_async_remote_copy(
                    src_ref=src, dst_ref=outs[a].at[index_of(peer)], send_sem=send_sems.at[a, k - 1],
                    recv_sem=recv_sems.at[a, k - 1], device_id=peer, device_id_type=pl.DeviceIdType.MESH)
                cp.wait_send()
                cp.wait_recv()
        for loc in local:
            loc.wait()

    return finish


def gather_two_level(arrays, name):
    n = len(arrays)

    def body(*refs):
        ins, outs = refs[:n], refs[n:2 * n]
        send_sems, recv_sems, local_sems = refs[2 * n:]
        x, y, c = lax.axis_index("x"), lax.axis_index("y"), lax.axis_index("c")
        sibling = (x, y, 1 - c)
        chips = [(1 - x, y), (x, 1 - y), (1 - x, 1 - y)]

        def slot(a, p):
            return outs[a].at[4 * p[0] + 2 * p[1] + p[2]]

        def copy(a, k, src, block, to):
            return pltpu.make_async_remote_copy(src_ref=src, dst_ref=slot(a, block), send_sem=send_sems.at[a, k],
                                                recv_sem=recv_sems.at[a, k], device_id=to, device_id_type=pl.DeviceIdType.MESH)

        me = (x, y, c)
        started, local = [], []
        for a in range(n):
            loc = pltpu.make_async_copy(ins[a], slot(a, me), local_sems.at[a])
            loc.start()
            local.append(loc)
            first = [copy(a, 0, ins[a], me, sibling)] + [copy(a, 1 + j, ins[a], me, (*chip, c)) for j, chip in enumerate(chips)]
            for cp in first:
                cp.start()
            started += first
        for j, chip in enumerate(chips):
            for a in range(n):
                copy(a, 1 + j, ins[a], (*chip, c), me).wait_recv()
                fwd = copy(a, 4 + j, slot(a, (*chip, c)), (*chip, c), sibling)
                fwd.start()
                started.append(fwd)
        for a in range(n):
            copy(a, 0, ins[a], sibling, me).wait_recv()
            for j, chip in enumerate(chips):
                copy(a, 4 + j, ins[a], (*chip, 1 - c), me).wait_recv()
        for cp in started:
            cp.wait_send()
        for loc in local:
            loc.wait()

    hbm = pl.BlockSpec(memory_space=pl.ANY)
    return pl.pallas_call(
        body, name=name, in_specs=[hbm] * n, out_specs=[hbm] * n, out_shape=_exchange_out_shapes(arrays, [False] * n),
        scratch_shapes=_exchange_sems(n), compiler_params=pltpu.CompilerParams(has_side_effects=True),
    )(*arrays)


def exchange(arrays, scatter, name):
    n = len(arrays)

    def body(*refs):
        _exchange_issue(refs[:n], refs[n:2 * n], scatter, *refs[2 * n:])()

    hbm = pl.BlockSpec(memory_space=pl.ANY)
    return pl.pallas_call(
        body, name=name, in_specs=[hbm] * n, out_specs=[hbm] * n, out_shape=_exchange_out_shapes(arrays, scatter),
        scratch_shapes=_exchange_sems(n), compiler_params=pltpu.CompilerParams(has_side_effects=True),
    )(*arrays)


def _rope_tables(n_lat, n_ctx):
    tok = np.arange(n_lat)
    freqs = 10000.0 ** (-np.arange(0, 16, 2, dtype=np.float32) / 16.0)

    def table(pos):
        ang = pos.astype(np.float32)[:, None] * freqs[None, :]
        ang = np.concatenate([ang, ang], axis=-1)
        return np.cos(ang), np.sin(ang)

    cr, sr = table(tok // GRID_W)
    cc, sc = table(tok % GRID_W)
    cos = np.tile(np.concatenate([cr, cc], axis=-1), (1, 4))
    sin = np.tile(np.concatenate([sr, sc], axis=-1), (1, 4))
    cos = np.concatenate([np.ones((n_ctx, 128), np.float32), cos], axis=0)
    sin = np.concatenate([np.zeros((n_ctx, 128), np.float32), sin], axis=0)
    return jnp.asarray(cos, F32), jnp.asarray(sin, F32)


def _pad_w_in(w):
    z = lambda n: jnp.zeros((w.shape[0], n), w.dtype)
    return jnp.concatenate([w[:, 1824:2848], w[:, 128:384], w[:, 416:672], w[:, 672:928], w[:, 928:1184], w[:, 1312:1568],
                            w[:, 1568:1824], w[:, 0:128], w[:, 384:416], z(96), w[:, 1184:1312], z(128)], axis=1)


def _unpad_w_in(wp):
    return jnp.concatenate([wp[:, C_GK:C_GK + 128], wp[:, C_GV:C_GV + 256], wp[:, C_GG:C_GG + 32], wp[:, C_NK:C_NK + 256],
                            wp[:, C_NV:C_NV + 256], wp[:, C_SU:C_SU + 256], wp[:, C_GQ:C_GQ + 128], wp[:, C_NQ:C_NQ + 256],
                            wp[:, C_PU:C_PU + 256], wp[:, C_GT:C_GT + 1024]], axis=1)


_W_IN_SEGS = [(0, 128, C_GK), (128, 256, C_GV), (384, 32, C_GG), (416, 256, C_NK), (672, 256, C_NV), (928, 256, C_SU),
              (1184, 128, C_GQ), (1312, 256, C_NQ), (1568, 256, C_PU), (1824, 1024, C_GT)]
W_IN_SHARD = N_IN // N_DEV


def _pad_w_in_blocks(blocks):
    pieces = []
    for orig, width, padded in _W_IN_SEGS:
        col = orig
        while col < orig + width:
            dev, lo = divmod(col, W_IN_SHARD)
            n = min(W_IN_SHARD - lo, orig + width - col)
            pieces.append((padded + col - orig, blocks[dev][:, lo:lo + n]))
            col += n
    pieces.sort(key=lambda p: p[0])
    out, at = [], 0
    for start, piece in pieces:
        if start > at:
            out.append(jnp.zeros((blocks.shape[1], start - at), blocks.dtype))
        out.append(piece)
        at = start + piece.shape[1]
    out.append(jnp.zeros((blocks.shape[1], PW - at), blocks.dtype))
    return jnp.concatenate(out, axis=1)


def _w_in_slabs(wp_blocks):
    slabs = []
    for dev in range(N_DEV):
        first, pieces = dev * W_IN_SHARD, []
        for orig, width, padded in _W_IN_SEGS:
            lo, hi = max(orig, first), min(orig + width, first + W_IN_SHARD)
            if lo < hi:
                a = padded + lo - orig
                blk, off = divmod(a, 1024)
                assert off + (hi - lo) <= 1024
                pieces.append(wp_blocks[blk][:, off:off + hi - lo])
        slabs.append(jnp.concatenate(pieces, axis=1))
    return jnp.stack(slabs)


def _pad_rows(u):
    return jnp.pad(u, ((POOL_HALO, POOL_HALO), (0, 0)))


def _block_diag4(w):
    out = jnp.zeros((256, 256), w.dtype)
    for i in range(4):
        out = lax.dynamic_update_slice(out, w[i], (64 * i, 64 * i))
    return out


def _layer_params(p, big, l):
    e_rep, e_tile, gmask, bdm = _s5_consts()
    wg = jnp.zeros((128, 256), F32)
    wg = lax.dynamic_update_slice(wg, p["gla_w_gate"][l, 0], (0, 0))
    wg = lax.dynamic_update_slice(wg, p["gla_w_gate"][l, 1], (16, 128))
    s5 = []
    for d in range(2):
        s5.append([p["s5_lam_re"][l, d], p["s5_lam_im"][l, d], p["s5_log_dt"][l, d].reshape(16, 1),
                   p["s5_b_re"][l, d].transpose(0, 2, 1).reshape(256, 64), p["s5_b_im"][l, d].transpose(0, 2, 1).reshape(256, 64),
                   p["s5_c_re"][l, d].reshape(256, 64), p["s5_c_im"][l, d].reshape(256, 64), e_rep, e_tile, gmask, bdm])
    havg = jnp.asarray((np.arange(256)[:, None] // 64 == np.arange(256)[None, :] // 64).astype(np.float32) / 64.0)
    e4 = jnp.asarray((np.arange(64)[:, None] == np.arange(256)[None, :] % 64).astype(np.float32))
    return dict(
        g_pre=p["g_pre"][l].reshape(1, D), g_post=p["g_post"][l].reshape(1, D), b_mod=p["b_mod"][l].reshape(1, 3 * D),
        w_mod=big["w_mod"], w_in=_pad_w_in_blocks(big["w_in_blocks"]) if "w_in_blocks" in big else _pad_w_in(big["w_in"]), w_out=big.get("w_out"),
        wg=wg, bg=p["gla_b_gate"][l].reshape(1, 256), g_norm=jnp.pad(p["gla_g_norm"][l].reshape(1, 64), ((0, 7), (0, 0))),
        bias8=_na_bias8(p["na_rpb"][l], f"na_bias_l{l}"), s5=s5, s5_d=p["s5_d"][l].reshape(1, 256), w_glu=None if big.get("s5_w_glu") is None else big["s5_w_glu"].astype(F32),
        b_glu=p["s5_b_glu"][l].reshape(1, 256), wpool=_block_diag4(p["pool_w"][l]), pool_scale=p["pool_scale"][l].reshape(1, 256),
        havg=havg, e4=e4)


def _matmul_tile(t, tile, steps):
    return t // steps if t % (8 * steps) == 0 else tile


def _cols(pz, start, width):
    return pz[:, start:start + width]


def _layer_fwd(z, modseg, lp, cos, sin, m_ctx, tile, s5_chunk, l, comm=None, late=None):
    t = z.shape[0]
    nct = m_ctx // tile
    nm = lambda s: f"{s}_l{l}"
    (h,) = rowwise_fwd(f_pre, nm("pre"), [z], [modseg], [lp["g_pre"]], [D], tile, nct)
    mm_tile = _matmul_tile(t, tile, 4)
    (gt,) = mm_nn_cols(h, lp["w_in"], C_GT, [1024], nm("in_proj_a"), tm=mm_tile)
    pv, nk, nv, su = mm_nn_cols(h, lp["w_in"], C_GV, [256] * 4, nm("in_proj_b"), tm=mm_tile)
    nq, pu, pk, pg, pq = mm_nn_cols(h, lp["w_in"], C_NQ, [256, 256, 128, 128, 128], nm("in_proj_c"), tm=mm_tile)
    q_r, k_r, lgf, lgb = rowwise_fwd(f_gla_prep, nm("gla_prep"), [pk, pg, pq, cos, sin], [], [lp["wg"], lp["bg"]], [128] * 4, tile, nct)
    half = None if comm is None else comm[0].shape[0] // 2
    spread = None if comm is None else [comm[0][:half], comm[1], comm[2], comm[3], comm[0][half:]]
    part = (lambda idx: None) if comm is None else (lambda idx: ([spread[i] for i in idx], [False] * len(idx)))
    (o1, st_f), got_late = gla_scan_fwd(q_r, k_r, pv, lgf, jnp.zeros((t, 256), F32), m_ctx, False, nm("gla_f"),
                                        None if late is None else (list(late), [False, False]))
    if late is not None:
        lp["w_out"], lp["w_glu"] = _gathered(got_late[0], False), _gathered(got_late[1], False).astype(F32)
    (o_gla, st_b), got_out = gla_scan_fwd(q_r, k_r, pv, lgb, o1, m_ctx, True, nm("gla_r"), part([2, 3]))
    received = None
    if comm is None:
        o_na = na_fwd(nq, nk, nv, lp["bias8"], m_ctx, nm("na"))
    else:
        o_na, got_in = na_fwd(nq, nk, nv, lp["bias8"], m_ctx, nm("na"), part([1]))
    s5p = [whole_fwd(f_s5_params, nm(f"s5_par{d}"), lp["s5"][d], [(1, 1024)] * 2 + [(256, 1024)] * 4) for d in range(2)]
    (y1, *states_f), got_mod_a = s5_scan_fwd(su, jnp.zeros((t, 256), F32), *s5p[0], m_ctx, s5_chunk, False, nm("s5_f"), part([0]))
    (y5, *states_b), got_mod_b = s5_scan_fwd(su, y1, *s5p[1], m_ctx, s5_chunk, True, nm("s5_r"), part([4]))
    if comm is not None:
        received = [jnp.concatenate([got_mod_a[0], got_mod_b[0]], axis=1), got_in[0], got_out[0], got_out[1]]
    pm = jnp.concatenate([pool_apply(_pad_rows(pu[:m_ctx]), m_ctx, False, nm("pool_c")),
                          pool_apply(_pad_rows(pu[m_ctx:]), t - m_ctx, False, nm("pool_x"))], axis=0)
    mix_rows = [o_gla, o_na, y5, su, pm, gt]
    mix_globs = [lp["g_norm"], lp["s5_d"], lp["w_glu"], lp["b_glu"], lp["wpool"], lp["pool_scale"], lp["havg"], lp["e4"]]
    (yg,) = rowwise_fwd(f_mix, nm("mix"), mix_rows, [], mix_globs, [D], tile, nct)
    out = mm_nn([yg], lp["w_out"], nm("out_proj"), tm=mm_tile)
    (z_new,) = rowwise_fwd(f_post, nm("post"), [z, out], [modseg], [lp["g_post"]], [D], tile, nct)
    saved = dict(z=z, h=h, pv=pv, nk=nk, nv=nv, su=su, nq=nq, pk=pk, pg=pg, pq=pq, q_r=q_r, k_r=k_r, lgf=lgf, lgb=lgb,
                 st_f=st_f, st_b=st_b, s5p=s5p, x0f=tuple(states_f), x0b=tuple(states_b), mix_rows=mix_rows, mix_globs=mix_globs,
                 yg=yg, out=out)
    return z_new, saved, received


def _f_pre_res(x, mod, g_pre):
    return f_pre(x, mod, g_pre)[0], x


def _layer_bwd(dz_new, sv, modseg, lp, cos, sin, m_ctx, tile, s5_chunk, l, comm=None, gdt=F32, send_early=False, as_slabs=False):
    t = dz_new.shape[0]
    nct = m_ctx // tile
    nm = lambda s: f"{s}_l{l}"
    g = {}
    dz_res, dout, dmod_post, g["g_post"] = rowwise_bwd(f_post, nm("post_b"), [sv["z"], sv["out"]], [modseg], [lp["g_post"]],
                                                       [dz_new], tile, nct, [True, True], [True])
    dyg = mm_nt([dout], lp["w_out"], nm("out_proj_dx"), tm=_matmul_tile(t, tile, 4))
    dw_tile = _matmul_tile(t, tile, 4)
    (g["w_out"],) = mm_tn(sv["yg"], [dout], nm("out_proj_dw"), tm=dw_tile, out_dtype=gdt)
    res = rowwise_bwd(f_mix, nm("mix_b"), sv["mix_rows"], [], sv["mix_globs"], [dyg], tile, nct, [True] * 6, [True] * 6 + [False] * 2)
    do_gla, do_na, dy5, dsu_a, dpm, dgt = res[:6]
    g["g_norm"], g["s5_d"], g["w_glu"], g["b_glu"], g["wpool"], g["pool_scale"] = res[6:]
    dpu = jnp.concatenate([pool_apply(_pad_rows(dpm[:m_ctx]), m_ctx, True, nm("pool_c_b")),
                           pool_apply(_pad_rows(dpm[m_ctx:]), t - m_ctx, True, nm("pool_x_b"))], axis=0)
    r_b = s5_scan_bwd(sv["su"], dy5, dsu_a, *sv["x0b"], *sv["s5p"][1], m_ctx, s5_chunk, True, nm("s5_r_b"))
    r_f = s5_scan_bwd(sv["su"], dy5, r_b[0], *sv["x0f"], *sv["s5p"][0], m_ctx, s5_chunk, False, nm("s5_f_b"))
    dsu = r_f[0]
    g["s5"] = [whole_bwd(f_s5_params, nm(f"s5_par{d}_b"), lp["s5"][d], list(r[1:]), [True] * 7 + [False] * 4)
               for d, r in ((0, r_f), (1, r_b))]
    part = (lambda idx: None) if comm is None else (lambda idx: ([comm[i] for i in idx], [True] * len(idx)))
    (dnq, dnk, dnv, dbias8), got_in = na_bwd(sv["nq"], sv["nk"], sv["nv"], do_na, lp["bias8"], m_ctx, nm("na_b"), part([0]))
    g["rpb"] = _na_rpb_grad(dbias8, nm("na_rpb_b"))
    zq, zv = jnp.zeros((t, 128), F32), jnp.zeros((t, 256), F32)
    early = ([_slabs(g["w_out"], False).astype(BF16), _slabs(g["w_glu"], False).astype(BF16)], [True, True]) if send_early else None
    (dq1, dk1, dv1, dlgb), g["early"] = gla_scan_bwd(sv["q_r"], sv["k_r"], sv["pv"], sv["lgb"], sv["st_b"], do_gla, (zq, zq, zv), m_ctx, True,
                                                     nm("gla_r_b"), early)
    (dq_r, dk_r, dpv, dlgf), got_out = gla_scan_bwd(sv["q_r"], sv["k_r"], sv["pv"], sv["lgf"], sv["st_f"], do_gla, (dq1, dk1, dv1), m_ctx, False,
                                                    nm("gla_f_b"), part([1, 2]))
    received = None if comm is None else [got_in[0], got_out[0], got_out[1]]
    dpk, dpg, dpq, g["wg"], g["bg"] = rowwise_bwd(f_gla_prep, nm("gla_prep_b"), [sv["pk"], sv["pg"], sv["pq"], cos, sin], [],
                                                  [lp["wg"], lp["bg"]], [dq_r, dk_r, dlgf, dlgb], tile, nct,
                                                  [True, True, True, False, False], [True, True])
    parts = [dgt, dpv, dnk, dnv, dsu, dnq, dpu, dpk, dpg, dpq, jnp.zeros((t, 128), F32)]
    dh = mm_nt(parts, lp["w_in"], nm("in_proj_dx"), tm=_matmul_tile(t, tile, 8))
    dw_blocks = mm_tn(sv["h"], parts, nm("in_proj_dw"), tm=dw_tile, out_dtype=gdt)
    if as_slabs:
        g["w_in_slabs"] = _w_in_slabs(dw_blocks)
    else:
        g["w_in"] = _unpad_w_in(jnp.concatenate(dw_blocks, axis=1))
    dz, dmod_pre, g["g_pre"] = rowwise_bwd(_f_pre_res, nm("pre_b"), [sv["z"]], [modseg], [lp["g_pre"]], [dh, dz_res], tile, nct, [True], [True])
    return dz, dmod_pre, dmod_post, g, received


def _f_mod_sum(cs, b_mod, w_mod):
    mod, _ = f_mod(cs, b_mod, w_mod)
    return mod, cs


def local_step(x, c, ctx, tgt, p, shards=None, tile=ROW_TILE, s5_chunk=S5_CHUNK):
    n_lat, m_ctx = x.shape[0], ctx.shape[0]
    n_layers = p["g_pre"].shape[0]
    z = jnp.concatenate([ctx, x], axis=0)
    cos, sin = _rope_tables(n_lat, m_ctx)
    cs = jnp.concatenate([c.reshape(1, D), p["c_ctx"].reshape(1, D), jnp.zeros((6, D), F32)], axis=0)
    gather = [False] * len(_SHARDED)
    lps, mods, silus, saves = [], [], [], []
    got = gather_two_level(shards[0][:2], "gather_weights_l0") if shards is not None else None
    for l in range(n_layers):
        if shards is None:
            big = {n: p[n][l] for n in _SHARDED}
        else:
            big = {n: _gathered(g, _BY_COLS[n]) for n, g in zip(_SHARDED, got) if n != "w_in"}
            big["w_in_blocks"] = got[_SHARDED.index("w_in")]
        lp = _layer_params(p, big, l)
        mod8, s8 = whole_fwd(f_mod, f"mod_l{l}", [cs, lp["b_mod"], lp["w_mod"]], [(8, 3 * D), (8, D)])
        modseg = mod8[:2].reshape(2, 1, 3 * D)
        comm = shards[l + 1] if shards is not None and l + 1 < n_layers else None
        late = shards[0][2:] if shards is not None and l == 0 else None
        z, sv, got = _layer_fwd(z, modseg, lp, cos, sin, m_ctx, tile, s5_chunk, l, comm, late)
        lps.append(lp); mods.append(modseg); silus.append(s8); saves.append(sv)
    loss, dz = loss_and_grad(z, tgt, m_ctx, "loss", tile)
    grads, received = [None] * n_layers, [None] * n_layers
    gdt = F32 if shards is None else BF16
    dcs = jnp.zeros((8, D), F32)
    pending = None
    for l in reversed(range(n_layers)):
        lp = lps[l]
        dz, dmod_pre, dmod_post, g, got = _layer_bwd(dz, saves[l], mods[l], lp, cos, sin, m_ctx, tile, s5_chunk, l, pending, gdt,
                                                     send_early=shards is not None and l == 0, as_slabs=shards is not None)
        if pending is not None:
            received[l + 1] = got
        dmod = jnp.concatenate([dmod_pre.reshape(2, 3 * D)[:, :2 * D], dmod_post.reshape(2, 3 * D)[:, 2 * D:]], axis=1)
        dmod8 = jnp.pad(dmod, ((0, 6), (0, 0)))
        dcs, g["b_mod"] = whole_bwd(_f_mod_sum, f"mod_b_l{l}", [cs, lp["b_mod"], lp["w_mod"]], [dmod8, dcs], [True, True, False])
        if shards is None:
            g["w_mod"] = jnp.concatenate(mm_tn(silus[l], [dmod8[:, :D], dmod8[:, D:2 * D], dmod8[:, 2 * D:]], f"mod_dw_l{l}", tm=8), axis=1)
        else:
            g["mod_s"], g["mod_d"] = silus[l][:2], dmod
        grads[l] = g
        if shards is not None:
            pending = _layer_sends(g)
    if shards is not None:
        got_in, got_small = exchange([pending[0], _small_sends(dcs[1], grads)], [True, False], "exchange_grads_l0")
        received[0] = [got_in] + list(grads[0]["early"]) + [got_small]
    return loss, dz[m_ctx:], dcs[1], grads, received


_WEIGHTS = ["c_ctx", "w_mod", "b_mod", "g_pre", "g_post", "w_in", "w_out", "gla_w_gate", "gla_b_gate", "gla_g_norm", "na_rpb",
            "s5_lam_re", "s5_lam_im", "s5_log_dt", "s5_b_re", "s5_b_im", "s5_c_re", "s5_c_im", "s5_d", "s5_w_glu", "s5_b_glu",
            "pool_w", "pool_scale"]
_INPUTS = ["x", "c", "ctx"] + _WEIGHTS + ["loss_target"] + ["m_" + n for n in _WEIGHTS] + ["v_" + n for n in _WEIGHTS]
_SHARDED = ["w_mod", "w_in", "w_out", "s5_w_glu"]
_BY_COLS = {"w_mod": True, "w_in": True, "w_out": False, "s5_w_glu": False}
_GRAD_SHARDED = ["w_in", "w_out", "s5_w_glu"]
_SMALL = [n for n in _WEIGHTS if n not in _SHARDED]
_SMALL_PER_LAYER = [n for n in _SMALL if n != "c_ctx"]
_PACK_ROWS = 256


def _pack_plan(like):
    tiled = [i for i, a in enumerate(like) if a.size % 1024 == 0]
    loose = [i for i, a in enumerate(like) if a.size % 1024 != 0]
    tail = -(-sum(like[i].size for i in loose) // 1024) * 8
    rows = sum(like[i].size // 128 for i in tiled) + tail
    return tiled, loose, tail, -(-rows // _PACK_ROWS) * _PACK_ROWS - rows


def _pack_rows(like, index):
    tiled, _, _, _ = _pack_plan(like)
    row = 0
    for i in tiled:
        n = like[i].size // 128
        if i == index:
            return row, row + n
        row += n
    raise ValueError("not a tile-aligned entry")


def _pack(arrs):
    tiled, loose, tail, fill = _pack_plan(arrs)
    dt = arrs[0].dtype
    flat = jnp.concatenate([arrs[i].reshape(-1) for i in loose])
    flat = jnp.pad(flat, (0, tail * 128 - flat.shape[0])).reshape(tail, 128)
    return jnp.concatenate([arrs[i].reshape(-1, 128) for i in tiled] + [flat, jnp.zeros((fill, 128), dt)], axis=0)


def _unpack(packed, like):
    tiled, loose, tail, _ = _pack_plan(like)
    out, row = [None] * len(like), 0
    for i in tiled:
        n = like[i].size // 128
        out[i] = packed[row:row + n].reshape(like[i].shape)
        row += n
    flat, pos = packed[row:row + tail].reshape(-1), 0
    for i in loose:
        out[i] = flat[pos:pos + like[i].size].reshape(like[i].shape)
        pos += like[i].size
    return out


def _gathered(g, cols):
    if cols:
        return g.transpose(1, 0, 2).reshape(g.shape[1], N_DEV * g.shape[2])
    return g.reshape(N_DEV * g.shape[1], g.shape[2])


def _slabs(w, cols):
    r, c = w.shape
    if cols:
        return w.reshape(r, N_DEV, c // N_DEV).transpose(1, 0, 2)
    return w.reshape(N_DEV, r // N_DEV, c)


def _layer_small(g):
    s5 = lambda i, f: jnp.stack([f(g["s5"][d][i]) for d in range(2)])
    return {
        "b_mod": g["b_mod"].reshape(3 * D), "g_pre": g["g_pre"].reshape(D), "g_post": g["g_post"].reshape(D),
        "gla_w_gate": jnp.stack([g["wg"][0:16, 0:128], g["wg"][16:32, 128:256]]),
        "gla_b_gate": g["bg"].reshape(2, 128), "gla_g_norm": g["g_norm"][0], "na_rpb": g["rpb"],
        "s5_lam_re": s5(0, lambda a: a), "s5_lam_im": s5(1, lambda a: a), "s5_log_dt": s5(2, lambda a: a.reshape(16)),
        "s5_b_re": s5(3, lambda a: a.reshape(16, 16, 64).transpose(0, 2, 1)),
        "s5_b_im": s5(4, lambda a: a.reshape(16, 16, 64).transpose(0, 2, 1)),
        "s5_c_re": s5(5, lambda a: a.reshape(16, 16, 64)), "s5_c_im": s5(6, lambda a: a.reshape(16, 16, 64)),
        "s5_d": g["s5_d"].reshape(256), "s5_b_glu": g["b_glu"].reshape(256),
        "pool_w": jnp.stack([g["wpool"][64 * i:64 * i + 64, 64 * i:64 * i + 64] for i in range(4)]),
        "pool_scale": g["pool_scale"].reshape(256),
    }


def _layer_sends(g):
    return [g["w_in_slabs"].astype(BF16), _slabs(g["w_out"], False).astype(BF16), _slabs(g["w_glu"], False).astype(BF16)]


def _small_sends(d_c_ctx, grads):
    per_layer = [_layer_small(g) for g in grads]
    full = {n: jnp.stack([s[n] for s in per_layer]) for n in _SMALL_PER_LAYER}
    full["c_ctx"] = d_c_ctx
    factors = [jnp.stack([g["mod_s"] for g in grads]), jnp.stack([g["mod_d"] for g in grads])]
    return _pack([full[n] for n in _SMALL] + factors).astype(BF16)


def kernel(x, c, ctx, c_ctx, w_mod, b_mod, g_pre, g_post, w_in, w_out, gla_w_gate, gla_b_gate, gla_g_norm, na_rpb, s5_lam_re, s5_lam_im, s5_log_dt, s5_b_re, s5_b_im, s5_c_re, s5_c_im, s5_d, s5_w_glu, s5_b_glu, pool_w, pool_scale, loss_target, m_c_ctx, m_w_mod, m_b_mod, m_g_pre, m_g_post, m_w_in, m_w_out, m_gla_w_gate, m_gla_b_gate, m_gla_g_norm, m_na_rpb, m_s5_lam_re, m_s5_lam_im, m_s5_log_dt, m_s5_b_re, m_s5_b_im, m_s5_c_re, m_s5_c_im, m_s5_d, m_s5_w_glu, m_s5_b_glu, m_pool_w, m_pool_scale, v_c_ctx, v_w_mod, v_b_mod, v_g_pre, v_g_post, v_w_in, v_w_out, v_gla_w_gate, v_gla_b_gate, v_gla_g_norm, v_na_rpb, v_s5_lam_re, v_s5_lam_im, v_s5_log_dt, v_s5_b_re, v_s5_b_im, v_s5_c_re, v_s5_c_im, v_s5_d, v_s5_w_glu, v_s5_b_glu, v_pool_w, v_pool_scale):
    given = dict(zip(_INPUTS, (x, c, ctx, c_ctx, w_mod, b_mod, g_pre, g_post, w_in, w_out, gla_w_gate, gla_b_gate, gla_g_norm, na_rpb, s5_lam_re, s5_lam_im, s5_log_dt, s5_b_re, s5_b_im, s5_c_re, s5_c_im, s5_d, s5_w_glu, s5_b_glu, pool_w, pool_scale, loss_target, m_c_ctx, m_w_mod, m_b_mod, m_g_pre, m_g_post, m_w_in, m_w_out, m_gla_w_gate, m_gla_b_gate, m_gla_g_norm, m_na_rpb, m_s5_lam_re, m_s5_lam_im, m_s5_log_dt, m_s5_b_re, m_s5_b_im, m_s5_c_re, m_s5_c_im, m_s5_d, m_s5_w_glu, m_s5_b_glu, m_pool_w, m_pool_scale, v_c_ctx, v_w_mod, v_b_mod, v_g_pre, v_g_post, v_w_in, v_w_out, v_gla_w_gate, v_gla_b_gate, v_gla_g_norm, v_na_rpb, v_s5_lam_re, v_s5_lam_im, v_s5_log_dt, v_s5_b_re, v_s5_b_im, v_s5_c_re, v_s5_c_im, v_s5_d, v_s5_w_glu, v_s5_b_glu, v_pool_w, v_pool_scale)))
    n_layers = w_in.shape[0]
    shards = [[given[n][l].astype(BF16) for n in _SHARDED] for l in range(n_layers)]
    p = {n: given[n] for n in _SMALL}
    loss, grad_x, _, _, received = local_step(x[0], c, ctx[0], loss_target[0], p, shards)
    final = {}
    for n in _GRAD_SHARDED:
        per_layer = [adamw(received[l][_GRAD_SHARDED.index(n)], given[n][l], given["m_" + n][l], given["v_" + n][l], f"adamw_{n}_l{l}")
                     for l in range(n_layers)]
        final[n] = [jnp.stack([res[kind] for res in per_layer]) for kind in range(4)]
    factor_like = [jnp.zeros((n_layers, 2, D), F32), jnp.zeros((n_layers, 2, 3 * D), F32)]
    like = [given[n] for n in _SMALL] + factor_like
    small_recv = received[0][-1]
    rows_s, rows_d = _pack_rows(like, len(_SMALL)), _pack_rows(like, len(_SMALL) + 1)
    fac_s = small_recv[:, rows_s[0]:rows_s[1]].reshape(N_DEV, n_layers, 2, D)
    fac_d = small_recv[:, rows_d[0]:rows_d[1]].reshape(N_DEV, n_layers, 2, 3 * D)
    me = 4 * lax.axis_index("x") + 2 * lax.axis_index("y") + lax.axis_index("c")
    cols = w_mod.shape[2]
    per_layer = []
    for l in range(n_layers):
        s_all = fac_s[:, l].reshape(2 * N_DEV, D)
        d_mine = lax.dynamic_slice_in_dim(fac_d[:, l].reshape(2 * N_DEV, 3 * D), me * cols, cols, axis=1)
        (g_mod,) = mm_tn(s_all, [d_mine], f"mod_dw_l{l}", tm=2 * N_DEV, tn=cols)
        per_layer.append(adamw(g_mod[None], given["w_mod"][l], given["m_w_mod"][l], given["v_w_mod"][l], f"adamw_w_mod_l{l}"))
    final["w_mod"] = [jnp.stack([res[kind] for res in per_layer]) for kind in range(4)]
    res = adamw(small_recv, _pack(like), _pack([given["m_" + n] for n in _SMALL] + factor_like),
                _pack([given["v_" + n] for n in _SMALL] + factor_like), "adamw_small")
    unpacked = [_unpack(packed, like) for packed in res]
    for i, n in enumerate(_SMALL):
        final[n] = [unpacked[kind][i] for kind in range(4)]
    loss = lax.psum(loss, ("x", "y", "c"))
    return (loss, grad_x[None], *[final[n][0] for n in _WEIGHTS], *[final[n][1] for n in _WEIGHTS],
            *[final[n][2] for n in _WEIGHTS], *[final[n][3] for n in _WEIGHTS])
```

```python
import functools
import math

import numpy as np
import jax
import jax.numpy as jnp
from jax import lax
from jax.experimental import pallas as pl
from jax.experimental.pallas import tpu as pltpu

F32 = jnp.float32
BF16 = jnp.bfloat16
HIGHEST = lax.Precision.HIGHEST
HIGH = lax.Precision.HIGH

D = 1024
GRID_W = 64
EPS = 1e-6
N_DEV = 8
C_GT, C_GV, C_NK, C_NV, C_SU, C_NQ, C_PU, C_GK, C_GG, C_GQ, C_END = 0, 1024, 1280, 1536, 1792, 2048, 2304, 2560, 2688, 2816, 2944
PW = 3072
N_CTX_ORIG = 416
N_IN = 2848
GLA_CHUNK = 128
S5_CHUNK = 256
ROW_TILE = 256
VMEM_LIMIT = 56 * 1024 * 1024

ADAM_LR, ADAM_B1, ADAM_B2, ADAM_EPS, ADAM_WD, ADAM_STEP = 0.001, 0.9, 0.999, 1e-08, 0.01, 10


def _cparams(**kw):
    return pltpu.CompilerParams(vmem_limit_bytes=VMEM_LIMIT, **kw)


def _dg(a, b, ca, cb, precision=None):
    return lax.dot_general(a, b, (((ca,), (cb,)), ((), ())), precision=precision, preferred_element_type=F32)


def hdot(a, b):
    return _dg(a, b, 1, 0, HIGHEST)


def hdot_nt(a, b):
    return _dg(a, b, 1, 1, HIGHEST)


def hdot_tn(a, b):
    return _dg(a, b, 0, 0, HIGHEST)


def mdot(a, b):
    return _dg(a, b, 1, 0, HIGH)


def mdot_nt(a, b):
    return _dg(a, b, 1, 1, HIGH)


def mdot_tn(a, b):
    return _dg(a, b, 0, 0, HIGH)


def b_nn(a, b):
    return _dg(a.astype(BF16), b.astype(BF16), 1, 0)


def b_nt(a, b):
    return _dg(a.astype(BF16), b.astype(BF16), 1, 1)


def b_tn(a, b):
    return _dg(a.astype(BF16), b.astype(BF16), 0, 0)


@jax.custom_vjp
def bdot(a, b):
    return b_nn(a, b)


def _bdot_fwd(a, b):
    return b_nn(a, b), (a, b)


def _bdot_bwd(res, ct):
    a, b = res
    return b_nt(ct, b).astype(a.dtype), b_tn(a, ct).astype(b.dtype)


bdot.defvjp(_bdot_fwd, _bdot_bwd)


def _log_sigmoid(z):
    return jnp.minimum(z, 0.0) - jnp.log(1.0 + jnp.exp(-jnp.abs(z)))


def _silu(z):
    return z * jax.nn.sigmoid(z)


def _gelu(z):
    return 0.5 * z * (1.0 + jnp.tanh(math.sqrt(2.0 / math.pi) * (z + 0.044715 * (z * z * z))))


def _cat(vals):
    return vals[0] if len(vals) == 1 else jnp.concatenate(vals, axis=-1)


def mm_nn(a_parts, b, name, tm=ROW_TILE, tn=1024):
    t = a_parts[0].shape[0]
    k, n = b.shape
    na = len(a_parts)
    tn = min(tn, n)

    def body(*refs):
        a = _cat([r[...].astype(BF16) for r in refs[:na]])
        refs[na + 1][...] = _dg(a, refs[na][...].astype(BF16), 1, 0)

    return pl.pallas_call(
        body, name=name, grid=(n // tn, t // tm),
        in_specs=[pl.BlockSpec((tm, p.shape[1]), lambda j, i: (i, 0)) for p in a_parts]
        + [pl.BlockSpec((k, tn), lambda j, i: (0, j))],
        out_specs=pl.BlockSpec((tm, tn), lambda j, i: (i, j)),
        out_shape=jax.ShapeDtypeStruct((t, n), F32),
        compiler_params=_cparams(dimension_semantics=("arbitrary", "arbitrary")),
    )(*a_parts, b)


def mm_nn_cols(a, b, start, widths, name, tm=ROW_TILE):
    t, k = a.shape
    tn = 1024
    assert start % tn == 0 and sum(widths) <= tn

    def body(a_ref, b_ref, *o_refs):
        r = _dg(a_ref[...].astype(BF16), b_ref[...].astype(BF16), 1, 0)
        off = 0
        for o_ref, w in zip(o_refs, widths):
            o_ref[...] = r[:, off:off + w]
            off += w

    return pl.pallas_call(
        body, name=name, grid=(t // tm,),
        in_specs=[pl.BlockSpec((tm, k), lambda i: (i, 0)), pl.BlockSpec((k, tn), lambda i: (0, start // tn))],
        out_specs=[pl.BlockSpec((tm, w), lambda i: (i, 0)) for w in widths],
        out_shape=[jax.ShapeDtypeStruct((t, w), F32) for w in widths],
        compiler_params=_cparams(dimension_semantics=("arbitrary",)),
    )(a, b)


def mm_nt(a_parts, b, name, tm=ROW_TILE):
    t = a_parts[0].shape[0]
    n, k = b.shape
    na = len(a_parts)

    def body(*refs):
        a = _cat([r[...].astype(BF16) for r in refs[:na]])
        refs[na + 1][...] = _dg(a, refs[na][...].astype(BF16), 1, 1)

    return pl.pallas_call(
        body, name=name, grid=(t // tm,),
        in_specs=[pl.BlockSpec((tm, p.shape[1]), lambda i: (i, 0)) for p in a_parts]
        + [pl.BlockSpec((n, k), lambda i: (0, 0))],
        out_specs=pl.BlockSpec((tm, n), lambda i: (i, 0)),
        out_shape=jax.ShapeDtypeStruct((t, n), F32),
        compiler_params=_cparams(dimension_semantics=("arbitrary",)),
    )(*a_parts, b)


def mm_tn(a, b_parts, name, tm=ROW_TILE, tn=1024, out_dtype=F32):
    t, k = a.shape
    widths = [p.shape[1] for p in b_parts]
    n = sum(widths)
    assert n % tn == 0
    groups, cur, acc = [], [], 0
    for idx, w in enumerate(widths):
        cur.append(idx)
        acc += w
        if acc == tn:
            groups.append(cur)
            cur, acc = [], 0
        assert acc < tn
    assert not cur
    outs = []
    for gi, grp in enumerate(groups):
        parts = [b_parts[i] for i in grp]
        npart = len(parts)
        nsteps = t // tm

        def body(*refs, npart=npart, nsteps=nsteps):
            a_v = refs[0][...].astype(BF16)
            b_v = _cat([r[...].astype(BF16) for r in refs[1:1 + npart]])
            o_ref, acc_ref = refs[1 + npart], refs[2 + npart]
            r = _dg(a_v, b_v, 0, 0)

            @pl.when(pl.program_id(0) == 0)
            def _():
                acc_ref[...] = r

            @pl.when(pl.program_id(0) != 0)
            def _():
                acc_ref[...] += r

            @pl.when(pl.program_id(0) == nsteps - 1)
            def _():
                o_ref[...] = acc_ref[...].astype(o_ref.dtype)

        outs.append(pl.pallas_call(
            body, name=f"{name}_{gi}", grid=(nsteps,),
            in_specs=[pl.BlockSpec((tm, k), lambda i: (i, 0))]
            + [pl.BlockSpec((tm, p.shape[1]), lambda i: (i, 0)) for p in parts],
            out_specs=pl.BlockSpec((k, tn), lambda i: (0, 0)),
            out_shape=jax.ShapeDtypeStruct((k, tn), out_dtype),
            scratch_shapes=[pltpu.VMEM((k, tn), F32)],
            compiler_params=_cparams(dimension_semantics=("arbitrary",)),
        )(a, *parts))
    return outs


def _seg_of(i, nct):
    return jnp.where(i < nct, 1, 0)


def rowwise_fwd(fn, name, rows, segs, globs, out_widths, tile, nct):
    t = rows[0].shape[0]
    nr, ns, ng = len(rows), len(segs), len(globs)

    def body(*refs):
        vals = [r[...] for r in refs[:nr]] + [r[0] for r in refs[nr:nr + ns]] + [r[...] for r in refs[nr + ns:nr + ns + ng]]
        outs = fn(*vals)
        for o_ref, o in zip(refs[nr + ns + ng:], outs):
            o_ref[...] = o

    return pl.pallas_call(
        body, name=name, grid=(t // tile,),
        in_specs=[pl.BlockSpec((tile, r.shape[1]), lambda i: (i, 0)) for r in rows]
        + [pl.BlockSpec((1, 1, s.shape[2]), lambda i: (_seg_of(i, nct), 0, 0)) for s in segs]
        + [pl.BlockSpec(g.shape, lambda i: (0, 0)) for g in globs],
        out_specs=[pl.BlockSpec((tile, w), lambda i: (i, 0)) for w in out_widths],
        out_shape=[jax.ShapeDtypeStruct((t, w), F32) for w in out_widths],
        compiler_params=_cparams(dimension_semantics=("arbitrary",)),
    )(*rows, *segs, *globs)


def rowwise_bwd(fn, name, rows, segs, globs, cts, tile, nct, row_diff, glob_diff):
    t = rows[0].shape[0]
    nr, ns, ng, nc = len(rows), len(segs), len(globs), len(cts)
    d_rows = [i for i in range(nr) if row_diff[i]]
    d_globs = [i for i in range(ng) if glob_diff[i]]

    def body(*refs):
        in_refs, out_refs = refs[:nr + ns + ng + nc], refs[nr + ns + ng + nc:]
        row_v = [r[...] for r in in_refs[:nr]]
        seg_v = [r[0] for r in in_refs[nr:nr + ns]]
        glob_v = [r[...] for r in in_refs[nr + ns:nr + ns + ng]]
        ct_v = tuple(r[...] for r in in_refs[nr + ns + ng:])

        def wrapped(dr, sv, dg):
            rv = list(row_v)
            for j, i in enumerate(d_rows):
                rv[i] = dr[j]
            gv = list(glob_v)
            for j, i in enumerate(d_globs):
                gv[i] = dg[j]
            return tuple(fn(*rv, *sv, *gv))

        _, vjp = jax.vjp(wrapped, [row_v[i] for i in d_rows], seg_v, [glob_v[i] for i in d_globs])
        c_rows, c_segs, c_globs = vjp(ct_v)
        i = pl.program_id(0)
        k = 0
        for c in c_rows:
            out_refs[k][...] = c
            k += 1
        seg_first = jnp.logical_or(i == 0, i == nct)
        for c in c_segs:
            ref = out_refs[k]
            k += 1

            @pl.when(seg_first)
            def _(ref=ref, c=c):
                ref[0] = c

            @pl.when(jnp.logical_not(seg_first))
            def _(ref=ref, c=c):
                ref[0] += c
        for c in c_globs:
            ref = out_refs[k]
            k += 1

            @pl.when(i == 0)
            def _(ref=ref, c=c):
                ref[...] = c

            @pl.when(i != 0)
            def _(ref=ref, c=c):
                ref[...] += c

    return pl.pallas_call(
        body, name=name, grid=(t // tile,),
        in_specs=[pl.BlockSpec((tile, r.shape[1]), lambda i: (i, 0)) for r in rows]
        + [pl.BlockSpec((1, 1, s.shape[2]), lambda i: (_seg_of(i, nct), 0, 0)) for s in segs]
        + [pl.BlockSpec(g.shape, lambda i: (0, 0)) for g in globs]
        + [pl.BlockSpec((tile, c.shape[1]), lambda i: (i, 0)) for c in cts],
        out_specs=[pl.BlockSpec((tile, rows[i].shape[1]), lambda i: (i, 0)) for i in d_rows]
        + [pl.BlockSpec((1, 1, s.shape[2]), lambda i: (_seg_of(i, nct), 0, 0)) for s in segs]
        + [pl.BlockSpec(globs[i].shape, lambda i: (0, 0)) for i in d_globs],
        out_shape=[jax.ShapeDtypeStruct(rows[i].shape, F32) for i in d_rows]
        + [jax.ShapeDtypeStruct(s.shape, F32) for s in segs]
        + [jax.ShapeDtypeStruct(globs[i].shape, F32) for i in d_globs],
        compiler_params=_cparams(dimension_semantics=("arbitrary",)),
    )(*rows, *segs, *globs, *cts)


def f_pre(x, mod, g_pre):
    shift, scale = mod[:, :D], mod[:, D:2 * D]
    rs = lax.rsqrt(jnp.mean(x * x, axis=-1, keepdims=True) + EPS)
    return ((x * rs) * g_pre * (1.0 + scale) + shift,)


def f_post(x, out, mod, g_post):
    gate = mod[:, 2 * D:]
    rs = lax.rsqrt(jnp.mean(out * out, axis=-1, keepdims=True) + EPS)
    return (x + gate * ((out * rs) * g_post),)


def f_mix(o_gla, o_na, y5, u5, pm, gcols, g_norm, s5_d, w_glu, b_glu, wpool, pool_scale, havg, e4):
    ms = mdot(o_gla * o_gla, havg)
    y_gla = o_gla * lax.rsqrt(ms + EPS) * jnp.sum(hdot(g_norm, e4), axis=0, keepdims=True)
    g = _gelu(u5 * s5_d + y5)
    y_s5 = g * jax.nn.sigmoid(bdot(g, w_glu) + b_glu)
    y_pool = bdot(pm, wpool) * pool_scale
    ycat = jnp.concatenate([y_gla, o_na, y_s5, y_pool], axis=-1)
    return (ycat * _silu(gcols),)


@jax.custom_vjp
def _rot_half16(x):
    lane = lax.broadcasted_iota(jnp.int32, x.shape, 1)
    first = jnp.bitwise_and(lane, 15) < 8
    return jnp.where(first, -pltpu.roll(x, x.shape[1] - 8, 1), pltpu.roll(x, 8, 1))


def _rot_fwd(x):
    return _rot_half16(x), None


def _rot_bwd(_, ct):
    return (-_rot_half16(ct),)


_rot_half16.defvjp(_rot_fwd, _rot_bwd)


def f_gla_prep(pk, pg, pq, cos, sin, wg, bg):
    z = bdot(pg, wg) + bg
    lg = _log_sigmoid(z) * (1.0 / 16.0)
    k_r = pk * cos + _rot_half16(pk) * sin
    q_r = (pq * cos + _rot_half16(pq) * sin) * (32.0 ** -0.5)
    return q_r, k_r, lg[:, :128], lg[:, 128:]


def _gla_consts(rev):
    c = GLA_CHUNK
    i = np.arange(c)
    inc = (i[None, :] >= i[:, None]) if rev else (i[None, :] <= i[:, None])
    mq = np.stack([(np.arange(128) // 32 == h) for h in range(4)]).astype(np.float32).reshape(4, 1, 128)
    mv = np.stack([(np.arange(256) // 64 == h) for h in range(4)]).astype(np.float32).reshape(4, 1, 256)
    bdt = (np.arange(256)[:, None] // 64 == np.arange(128)[None, :] // 32).astype(np.float32)
    inc = inc.astype(np.float32)
    return jnp.asarray(inc), jnp.asarray(inc.T.copy()), jnp.asarray(mq), jnp.asarray(mv), jnp.asarray(bdt)


def _stack_heads(x, m_ref):
    return jnp.concatenate([x * m_ref[h] for h in range(4)], axis=0)


def _tile4(m):
    return jnp.concatenate([m, m, m, m], axis=0)


def _fold_heads(r4, m_ref):
    r = r4.shape[0] // 4
    out = m_ref[0] * r4[0:r]
    for h in range(1, 4):
        out = out + m_ref[h] * r4[h * r:(h + 1) * r]
    return out


def _gla_chunk_of(s, n_ctx_chunks, n_chunks, rev):
    if not rev:
        return s
    return jnp.where(s < n_ctx_chunks, n_ctx_chunks - 1 - s, n_ctx_chunks + n_chunks - 1 - s)


def gla_scan_fwd(q, k, v, lg, acc, n_ctx_rows, rev, name, comm=None):
    t = q.shape[0]
    nch, ncc = t // GLA_CHUNK, n_ctx_rows // GLA_CHUNK
    inc, inc_t, mq, mv, bdt = _gla_consts(rev)

    def body(q_ref, k_ref, v_ref, lg_ref, acc_ref, inc_ref, inct_ref, mq_ref, mv_ref, bdt_ref, o_ref, st_ref):
        lmask, lmask_t = inc_ref[...], inct_ref[...]
        bd = bdt_ref[...]

        def step(s, st):
            c = _gla_chunk_of(s, ncc, nch, rev)
            rows = pl.ds(pl.multiple_of(c * GLA_CHUNK, GLA_CHUNK), GLA_CHUNK)
            qc, kc, vc, lgc = q_ref[rows, :], k_ref[rows, :], v_ref[rows, :], lg_ref[rows, :]
            st_ref[c] = st
            b = hdot(lmask, lgc)
            blast = jnp.sum(lgc, axis=0, keepdims=True)
            qe, ke, kd = qc * jnp.exp(b), kc * jnp.exp(-b), kc * jnp.exp(blast - b)
            ke4, v4 = _stack_heads(ke, mq_ref), _stack_heads(vc, mv_ref)
            at = _tile4(lmask_t) * b_nt(ke4, qe)
            o_ref[rows, :] = acc_ref[rows, :] + b_nt(qe, st) + b_tn(at, v4)
            return st * jnp.exp(blast) + bd * mdot_tn(vc, kd)

        lax.fori_loop(0, nch, step, jnp.zeros((256, 128), F32))

    return _call_with_exchange(body, name, [q, k, v, lg, acc, inc, inc_t, mq, mv, bdt],
                               [jax.ShapeDtypeStruct((t, 256), F32), jax.ShapeDtypeStruct((nch, 256, 128), F32)], comm)


def gla_scan_bwd(q, k, v, lg, st, do, acc, n_ctx_rows, rev, name, comm=None):
    t = q.shape[0]
    nch, ncc = t // GLA_CHUNK, n_ctx_rows // GLA_CHUNK
    inc, inc_t, mq, mv, bdt = _gla_consts(rev)

    def body(q_ref, k_ref, v_ref, lg_ref, st_ref, do_ref, aq_ref, ak_ref, av_ref, inc_ref, inct_ref, mq_ref, mv_ref, bdt_ref,
             dq_ref, dk_ref, dv_ref, dlg_ref):
        lmask, lmask_t = inc_ref[...], inct_ref[...]
        bd = bdt_ref[...]

        def step(j, carry):
            dst, gsum = carry
            s = nch - 1 - j
            c = _gla_chunk_of(s, ncc, nch, rev)
            rows = pl.ds(pl.multiple_of(c * GLA_CHUNK, GLA_CHUNK), GLA_CHUNK)
            qc, kc, vc, lgc, doc = q_ref[rows, :], k_ref[rows, :], v_ref[rows, :], lg_ref[rows, :], do_ref[rows, :]
            stc = st_ref[c]
            b = hdot(lmask, lgc)
            blast = jnp.sum(lgc, axis=0, keepdims=True)
            eb, enb, edb = jnp.exp(b), jnp.exp(-b), jnp.exp(blast - b)
            qe, ke, kd = qc * eb, kc * enb, kc * edb
            ke4, v4 = _stack_heads(ke, mq_ref), _stack_heads(vc, mv_ref)
            lm4 = _tile4(lmask_t)
            at = lm4 * b_nt(ke4, qe)
            dat = lm4 * mdot_nt(v4, doc)
            dqe = mdot(doc, stc) + mdot_tn(dat, ke4)
            dke = _fold_heads(mdot(dat, qe), mq_ref)
            dv = b_nt(kd, dst) + _fold_heads(b_nn(at, doc), mv_ref)
            dkd = mdot(vc, dst)
            dq = dqe * eb
            dk = dke * enb + dkd * edb
            g = qc * dq - kc * dk
            dlg_ref[rows, :] = hdot_tn(lmask, g) + gsum
            dq_ref[rows, :] = aq_ref[rows, :] + dq
            dk_ref[rows, :] = ak_ref[rows, :] + dk
            dv_ref[rows, :] = av_ref[rows, :] + dv
            dst_new = dst * jnp.exp(blast) + bd * mdot_tn(doc, qe)
            return dst_new, gsum + jnp.sum(g, axis=0, keepdims=True)

        lax.fori_loop(0, nch, step, (jnp.zeros((256, 128), F32), jnp.zeros((1, 128), F32)))

    return _call_with_exchange(body, name, [q, k, v, lg, st, do, *acc, inc, inc_t, mq, mv, bdt],
                               [jax.ShapeDtypeStruct((t, 128), F32), jax.ShapeDtypeStruct((t, 128), F32),
                                jax.ShapeDtypeStruct((t, 256), F32), jax.ShapeDtypeStruct((t, 128), F32)], comm)


def whole_fwd(fn, name, args, out_shapes):
    def body(*refs):
        outs = fn(*[r[...] for r in refs[:len(args)]])
        for o_ref, o in zip(refs[len(args):], outs):
            o_ref[...] = o

    vm = pl.BlockSpec(memory_space=pltpu.VMEM)
    return pl.pallas_call(
        body, name=name, in_specs=[vm] * len(args), out_specs=[vm] * len(out_shapes),
        out_shape=[jax.ShapeDtypeStruct(s, F32) for s in out_shapes], compiler_params=_cparams(),
    )(*args)


def whole_bwd(fn, name, args, cts, diff):
    d_idx = [i for i in range(len(args)) if diff[i]]

    def body(*refs):
        vals = [r[...] for r in refs[:len(args)]]
        ct_v = tuple(r[...] for r in refs[len(args):len(args) + len(cts)])

        def wrapped(dv):
            av = list(vals)
            for j, i in enumerate(d_idx):
                av[i] = dv[j]
            return tuple(fn(*av))

        _, vjp = jax.vjp(wrapped, [vals[i] for i in d_idx])
        (c_args,) = vjp(ct_v)
        for o_ref, c in zip(refs[len(args) + len(cts):], c_args):
            o_ref[...] = c

    vm = pl.BlockSpec(memory_space=pltpu.VMEM)
    return pl.pallas_call(
        body, name=name, in_specs=[vm] * (len(args) + len(cts)), out_specs=[vm] * len(d_idx),
        out_shape=[jax.ShapeDtypeStruct(args[i].shape, F32) for i in d_idx], compiler_params=_cparams(),
    )(*args, *cts)


def _s5_consts():
    e_rep = (np.arange(256)[:, None] // 16 == np.arange(16)[None, :]).astype(np.float32)
    e_tile = (np.arange(64)[:, None] == np.arange(1024)[None, :] % 64).astype(np.float32)
    gmask = (np.arange(16)[:, None] == np.arange(1024)[None, :] // 64).astype(np.float32)
    bdm = (np.arange(256)[:, None] // 16 == np.arange(1024)[None, :] // 64).astype(np.float32)
    return jnp.asarray(e_rep), jnp.asarray(e_tile), jnp.asarray(gmask), jnp.asarray(bdm)


def f_s5_params(lam_re, lam_im, log_dt, bt_re, bt_im, ct_re, ct_im, e_rep, e_tile, gmask, bdm):
    dt = jnp.exp(log_dt)
    mag = jnp.exp(lam_re * dt)
    ang = lam_im * dt
    lb_re, lb_im = mag * jnp.cos(ang), mag * jnp.sin(ang)
    num_re, num_im = lb_re - 1.0, lb_im
    den = lam_re * lam_re + lam_im * lam_im
    coef_re = (num_re * lam_re + num_im * lam_im) / den
    coef_im = (num_im * lam_re - num_re * lam_im) / den
    cr, ci = hdot(e_rep, coef_re), hdot(e_rep, coef_im)
    bbt_re = cr * bt_re - ci * bt_im
    bbt_im = cr * bt_im + ci * bt_re
    a_re = jnp.sum(hdot(lb_re, e_tile) * gmask, axis=0, keepdims=True)
    a_im = jnp.sum(hdot(lb_im, e_tile) * gmask, axis=0, keepdims=True)
    return (a_re, a_im, hdot(bbt_re, e_tile) * bdm, hdot(bbt_im, e_tile) * bdm,
            hdot(ct_re, e_tile) * bdm, hdot(ct_im, e_tile) * bdm)


def _s5_doubling(xr, xi, pr, pi, pos, n, steps, rev):
    rows = xr.shape[0]
    for s in steps:
        if rev:
            keep = pos < (n - s)
            sr, si = pltpu.roll(xr, rows - s, 0), pltpu.roll(xi, rows - s, 0)
        else:
            keep = pos >= s
            sr, si = pltpu.roll(xr, s, 0), pltpu.roll(xi, s, 0)
        sr, si = jnp.where(keep, sr, 0.0), jnp.where(keep, si, 0.0)
        xr, xi = xr + pr * sr - pi * si, xi + pr * si + pi * sr
        pr, pi = pr * pr - pi * pi, 2.0 * pr * pi
    return xr, xi, pr, pi


SUBLANES = 8


def _s5_scan(xr, xi, a_re, a_im, rev, chunk, scr):
    xs_r, xs_i, yp_r, yp_i = scr
    ng = chunk // SUBLANES
    x3r, x3i = xr.reshape(ng, SUBLANES, 1024), xi.reshape(ng, SUBLANES, 1024)
    sub = lax.broadcasted_iota(jnp.int32, (SUBLANES, 1024), 0)
    a8r, a8i = a_re, a_im
    for s in (1, 2, 4):
        keep = sub < (SUBLANES - s) if rev else sub >= s
        mr, mi = jnp.where(keep, a8r, 0.0)[None], jnp.where(keep, a8i, 0.0)[None]
        shift = SUBLANES - s if rev else s
        sr, si = pltpu.roll(x3r, shift, 1), pltpu.roll(x3i, shift, 1)
        x3r, x3i = x3r + mr * sr - mi * si, x3i + mr * si + mi * sr
        a8r, a8i = a8r * a8r - a8i * a8i, 2.0 * a8r * a8i
    xr, xi = x3r.reshape(chunk, 1024), x3i.reshape(chunk, 1024)
    nblk = 1024 // 128
    for j in range(nblk):
        xs_r[j] = xr[:, 128 * j:128 * (j + 1)]
        xs_i[j] = xi[:, 128 * j:128 * (j + 1)]
    edge = pl.ds(0 if rev else SUBLANES - 1, ng, stride=SUBLANES)
    gr = jnp.concatenate([xs_r[j, edge, :] for j in range(nblk)], axis=-1)
    gi = jnp.concatenate([xs_i[j, edge, :] for j in range(nblk)], axis=-1)
    grow = lax.broadcasted_iota(jnp.int32, (ng, 1024), 0)
    steps = tuple(1 << k for k in range((ng - 1).bit_length()))
    gr, gi, _, _ = _s5_doubling(gr, gi, a8r, a8i, grow, ng, steps, rev)
    if rev:
        yp_r[...] = jnp.where(grow < ng - 1, pltpu.roll(gr, ng - 1, 0), 0.0)
        yp_i[...] = jnp.where(grow < ng - 1, pltpu.roll(gi, ng - 1, 0), 0.0)
    else:
        yp_r[...] = jnp.where(grow >= 1, pltpu.roll(gr, 1, 0), 0.0)
        yp_i[...] = jnp.where(grow >= 1, pltpu.roll(gi, 1, 0), 0.0)
    sub = lax.broadcasted_iota(jnp.int32, (SUBLANES, 1024), 0)
    tr, ti = jnp.zeros((SUBLANES, 1024), F32), jnp.zeros((SUBLANES, 1024), F32)
    cr, ci = a_re, a_im
    for n in range(1, SUBLANES + 1):
        r = SUBLANES - n if rev else n - 1
        tr, ti = jnp.where(sub == r, cr, tr), jnp.where(sub == r, ci, ti)
        cr, ci = cr * a_re - ci * a_im, cr * a_im + ci * a_re
    for j in range(nblk):
        lanes = slice(128 * j, 128 * (j + 1))
        tr_j, ti_j = tr[:, lanes], ti[:, lanes]
        for g in range(ng):
            rows = slice(g * SUBLANES, (g + 1) * SUBLANES)
            er, ei = yp_r[g:g + 1, lanes], yp_i[g:g + 1, lanes]
            xs_r[j, rows, :] = xs_r[j, rows, :] + tr_j * er - ti_j * ei
            xs_i[j, rows, :] = xs_i[j, rows, :] + tr_j * ei + ti_j * er
    return (jnp.concatenate([xs_r[j] for j in range(nblk)], axis=-1),
            jnp.concatenate([xs_i[j] for j in range(nblk)], axis=-1))


def _s5_scratch(chunk):
    return [pltpu.VMEM((8, chunk, 128), F32), pltpu.VMEM((8, chunk, 128), F32),
            pltpu.VMEM((chunk // SUBLANES, 1024), F32), pltpu.VMEM((chunk // SUBLANES, 1024), F32)]


def _s5_chunk_states(u_c, x0r, x0i, a_re, a_im, bb_re, bb_im, rev, chunk, scr):
    row = lax.broadcasted_iota(jnp.int32, (chunk, 1024), 0)
    first = row == (chunk - 1 if rev else 0)
    inj_r = a_re * x0r - a_im * x0i
    inj_i = a_re * x0i + a_im * x0r
    xr = b_nn(u_c, bb_re) + jnp.where(first, inj_r, 0.0)
    xi = b_nn(u_c, bb_im) + jnp.where(first, inj_i, 0.0)
    return _s5_scan(xr, xi, a_re, a_im, rev, chunk, scr)


def _row_pick(x, idx):
    row = lax.broadcasted_iota(jnp.int32, x.shape, 0)
    return jnp.sum(jnp.where(row == idx, x, 0.0), axis=0, keepdims=True)


def s5_scan_fwd(u, acc, a_re, a_im, bb_re, bb_im, cc_re, cc_im, n_ctx_rows, chunk, rev, name, comm=None):
    t = u.shape[0]
    nch, ncc = t // chunk, n_ctx_rows // chunk

    def body(u_ref, acc_ref, ar_ref, ai_ref, br_ref, bi_ref, cr_ref, ci_ref, y_ref, x0r_ref, x0i_ref, xsr_ref, xsi_ref, *scr):
        a_r, a_i = ar_ref[...], ai_ref[...]

        def step(s, carry):
            x0r, x0i = carry
            c = _gla_chunk_of(s, ncc, nch, rev)
            rows = pl.ds(pl.multiple_of(c * chunk, chunk), chunk)
            x0r_ref[c] = x0r
            x0i_ref[c] = x0i
            xr, xi = _s5_chunk_states(u_ref[rows, :], x0r, x0i, a_r, a_i, br_ref[...], bi_ref[...], rev, chunk, scr)
            y_ref[rows, :] = acc_ref[rows, :] + b_nt(xr, cr_ref[...]) - b_nt(xi, ci_ref[...])
            xsr_ref[rows, :] = xr.astype(BF16)
            xsi_ref[rows, :] = xi.astype(BF16)
            last = 0 if rev else chunk - 1
            return _row_pick(xr, last), _row_pick(xi, last)

        lax.fori_loop(0, nch, step, (jnp.zeros((1, 1024), F32), jnp.zeros((1, 1024), F32)))

    return _call_with_exchange(
        body, name, [u, acc, a_re, a_im, bb_re, bb_im, cc_re, cc_im],
        [jax.ShapeDtypeStruct((t, 256), F32), jax.ShapeDtypeStruct((nch, 1, 1024), F32),
         jax.ShapeDtypeStruct((nch, 1, 1024), F32), jax.ShapeDtypeStruct((t, 1024), BF16),
         jax.ShapeDtypeStruct((t, 1024), BF16)], comm, _s5_scratch(chunk))


def s5_scan_bwd(u, dy, du_acc, x0r, x0i, xsr, xsi, a_re, a_im, bb_re, bb_im, cc_re, cc_im, n_ctx_rows, chunk, rev, name):
    t = u.shape[0]
    nch, ncc = t // chunk, n_ctx_rows // chunk

    def body(u_ref, dy_ref, dua_ref, x0r_ref, x0i_ref, xsr_ref, xsi_ref, ar_ref, ai_ref, br_ref, bi_ref, cr_ref, ci_ref,
             du_ref, dar_ref, dai_ref, dbr_ref, dbi_ref, dcr_ref, dci_ref, *scr):
        a_r, a_i = ar_ref[...], ai_ref[...]
        for ref in (dbr_ref, dbi_ref, dcr_ref, dci_ref):
            ref[...] = jnp.zeros_like(ref)
        row = lax.broadcasted_iota(jnp.int32, (chunk, 1024), 0)
        first_idx, last_idx = (chunk - 1, 0) if rev else (0, chunk - 1)

        def step(j, carry):
            lcr, lci, dar, dai = carry
            s = nch - 1 - j
            c = _gla_chunk_of(s, ncc, nch, rev)
            rows = pl.ds(pl.multiple_of(c * chunk, chunk), chunk)
            u_c, dy_c = u_ref[rows, :], dy_ref[rows, :]
            x0r_c, x0i_c = x0r_ref[c], x0i_ref[c]
            xr, xi = xsr_ref[rows, :].astype(F32), xsi_ref[rows, :].astype(F32)
            dcr_ref[...] += b_tn(dy_c, xr)
            dci_ref[...] -= b_tn(dy_c, xi)
            inj_r = a_r * lcr + a_i * lci
            inj_i = a_r * lci - a_i * lcr
            is_last = row == last_idx
            lr = b_nn(dy_c, cr_ref[...]) + jnp.where(is_last, inj_r, 0.0)
            li = -b_nn(dy_c, ci_ref[...]) + jnp.where(is_last, inj_i, 0.0)
            lr, li = _s5_scan(lr, li, a_r, -a_i, not rev, chunk, scr)
            du_ref[rows, :] = dua_ref[rows, :] + b_nt(lr, br_ref[...]) + b_nt(li, bi_ref[...])
            dbr_ref[...] += b_tn(u_c, lr)
            dbi_ref[...] += b_tn(u_c, li)
            if rev:
                pr, pi = pltpu.roll(xr, chunk - 1, 0), pltpu.roll(xi, chunk - 1, 0)
            else:
                pr, pi = pltpu.roll(xr, 1, 0), pltpu.roll(xi, 1, 0)
            is_first = row == first_idx
            pr, pi = jnp.where(is_first, x0r_c, pr), jnp.where(is_first, x0i_c, pi)
            dar = dar + jnp.sum(lr * pr + li * pi, axis=0, keepdims=True)
            dai = dai + jnp.sum(li * pr - lr * pi, axis=0, keepdims=True)
            return _row_pick(lr, first_idx), _row_pick(li, first_idx), dar, dai

        z = jnp.zeros((1, 1024), F32)
        _, _, dar, dai = lax.fori_loop(0, nch, step, (z, z, z, z))
        dar_ref[...] = dar
        dai_ref[...] = dai

    vm = pl.BlockSpec(memory_space=pltpu.VMEM)
    big = jax.ShapeDtypeStruct((256, 1024), F32)
    vec = jax.ShapeDtypeStruct((1, 1024), F32)
    return pl.pallas_call(
        body, name=name, in_specs=[vm] * 13, out_specs=[vm] * 7,
        out_shape=[jax.ShapeDtypeStruct((t, 256), F32), vec, vec, big, big, big, big],
        scratch_shapes=_s5_scratch(chunk), compiler_params=_cparams(),
    )(u, dy, du_acc, x0r, x0i, xsr, xsi, a_re, a_im, bb_re, bb_im, cc_re, cc_im)


POOL_HALO = 8


def pool_apply(u_pad, n, transpose, name, tile=ROW_TILE):
    tile = min(tile, n)
    ext = tile + 2 * POOL_HALO
    trel = np.arange(ext)[None, :] - POOL_HALO - np.arange(tile)[:, None]
    if transpose:
        trel = -trel
    band4 = np.concatenate([((trel >= -(1 << w)) & (trel <= (1 << w) - 1)) for w in range(4)], axis=0).astype(np.float32)

    def body(u_ref, band_ref, lm_ref, o_ref):
        lax.fori_loop(0, n // tile, functools.partial(step, u_ref, band_ref, lm_ref, o_ref), 0)

    def step(u_ref, band_ref, lm_ref, o_ref, i, carry):
        val = u_ref[pl.ds(pl.multiple_of(i * tile, tile), ext), :]
        lane = lax.broadcasted_iota(jnp.int32, (ext, 256), 1)
        half = jnp.left_shift(1, jnp.right_shift(lane, 6))
        trow = lax.broadcasted_iota(jnp.int32, (ext, 256), 0) + (i * tile - POOL_HALO)
        cnt = jnp.minimum(trow + half, n) - jnp.maximum(trow - half, 0)
        inv = 1.0 / jnp.maximum(cnt, 1).astype(F32)
        src = val * inv if transpose else val
        acc = _fold_heads(mdot(band_ref[...], src), lm_ref)
        centre = val[POOL_HALO:POOL_HALO + tile]
        if not transpose:
            acc = acc * inv[POOL_HALO:POOL_HALO + tile]
        o_ref[pl.ds(pl.multiple_of(i * tile, tile), tile), :] = acc - centre
        return carry

    vm = pl.BlockSpec(memory_space=pltpu.VMEM)
    return pl.pallas_call(
        body, name=name, in_specs=[vm] * 3, out_specs=vm,
        out_shape=jax.ShapeDtypeStruct((n, 256), F32), compiler_params=_cparams(),
    )(u_pad, jnp.asarray(band4), _na_head_masks())


NA_SCALE = 64.0 ** -0.5
NEG = -1e30


def _call_with_exchange(compute, name, args, out_shapes, comm, scratch=()):
    vm = pl.BlockSpec(memory_space=pltpu.VMEM)
    n_in, n_out = len(args), len(out_shapes)
    if comm is None:
        outs = pl.pallas_call(compute, name=name, in_specs=[vm] * n_in, out_specs=[vm] * n_out, out_shape=out_shapes,
                              scratch_shapes=list(scratch), compiler_params=_cparams())(*args)
        return outs, None
    arrays, scatter = comm
    n = len(arrays)

    def body(*refs):
        c_in = refs[n_in:n_in + n]
        c_out = refs[n_in + n + n_out:n_in + 2 * n + n_out]
        scr = refs[n_in + 2 * n + n_out:n_in + 2 * n + n_out + len(scratch)]
        finish = _exchange_issue(c_in, c_out, scatter, *refs[n_in + 2 * n + n_out + len(scratch):])
        compute(*refs[:n_in], *refs[n_in + n:n_in + n + n_out], *scr)
        finish()

    hbm = pl.BlockSpec(memory_space=pl.ANY)
    outs = pl.pallas_call(
        body, name=name, in_specs=[vm] * n_in + [hbm] * n, out_specs=[vm] * n_out + [hbm] * n,
        out_shape=list(out_shapes) + _exchange_out_shapes(arrays, scatter), scratch_shapes=list(scratch) + _exchange_sems(n),
        compiler_params=_cparams(has_side_effects=True),
    )(*args, *arrays)
    return outs[:n_out], outs[n_out:]


def _na_head_masks():
    return jnp.asarray(np.stack([(np.arange(256) // 64 == h) for h in range(4)]).astype(np.float32).reshape(4, 1, 256))


def _na_window(r, rows):
    start = jnp.clip(r - 4, 0, rows - 8)
    return start, start - r + 7


def _na_probs(qh, kw, kc, bias):
    s_c = b_nt(qh, kc)
    m = jnp.max(s_c, axis=-1, keepdims=True)
    if kw is not None:
        s_w = b_nt(qh, kw) + bias
        m = jnp.maximum(m, jnp.max(s_w, axis=-1, keepdims=True))
        p_w = jnp.exp(s_w - m)
    p_c = jnp.exp(s_c - m)
    l = jnp.sum(p_c, axis=-1, keepdims=True)
    if kw is not None:
        l = l + jnp.sum(p_w, axis=-1, keepdims=True)
        return p_w / l, p_c / l
    return None, p_c / l


def na_fwd(q, k, v, bias8, n_ctx_rows, name, comm=None):
    t = q.shape[0]
    m_ctx = n_ctx_rows
    rows = (t - m_ctx) // GRID_W
    hm = _na_head_masks()

    def body(q_ref, k_ref, v_ref, b_ref, hm_ref, o_ref):
        kc, vc = k_ref[0:m_ctx, :], v_ref[0:m_ctx, :]

        def ctx_step(i, _):
            rs = pl.ds(pl.multiple_of(i * 64, 64), 64)
            q4 = _stack_heads(q_ref[rs, :] * NA_SCALE, hm_ref)
            _, p_c = _na_probs(q4, None, kc, None)
            o_ref[rs, :] = _fold_heads(b_nn(p_c, vc), hm_ref)
            return 0

        lax.fori_loop(0, m_ctx // 64, ctx_step, 0)

        def lat_step(r, _):
            start, off = _na_window(r, rows)
            rs = pl.ds(pl.multiple_of(m_ctx + r * 64, 64), 64)
            ws = pl.ds(pl.multiple_of(m_ctx + start * 64, 64), 512)
            q4 = _stack_heads(q_ref[rs, :] * NA_SCALE, hm_ref)
            kw, vw = k_ref[ws, :], v_ref[ws, :]
            p_w, p_c = _na_probs(q4, kw, kc, b_ref[off])
            o_ref[rs, :] = _fold_heads(b_nn(p_w, vw) + b_nn(p_c, vc), hm_ref)
            return 0

        lax.fori_loop(0, rows, lat_step, 0)

    (o,), received = _call_with_exchange(body, name, [q, k, v, bias8, hm], [jax.ShapeDtypeStruct((t, 256), F32)], comm)
    return o if comm is None else (o, received)


def na_bwd(q, k, v, do, bias8, n_ctx_rows, name, comm=None):
    t = q.shape[0]
    m_ctx = n_ctx_rows
    rows = (t - m_ctx) // GRID_W
    hm = _na_head_masks()

    def body(q_ref, k_ref, v_ref, do_ref, b_ref, hm_ref, dq_ref, dk_ref, dv_ref, db_ref):
        kc, vc = k_ref[0:m_ctx, :], v_ref[0:m_ctx, :]
        dk_ref[...] = jnp.zeros_like(dk_ref)
        dv_ref[...] = jnp.zeros_like(dv_ref)
        db_ref[...] = jnp.zeros_like(db_ref)

        def head_terms(qh, doh, kw, vw, bias):
            p_w, p_c = _na_probs(qh, kw, kc, bias)
            dp_c = b_nt(doh, vc)
            delta = jnp.sum(p_c * dp_c, axis=-1, keepdims=True)
            if kw is not None:
                dp_w = b_nt(doh, vw)
                delta = delta + jnp.sum(p_w * dp_w, axis=-1, keepdims=True)
                ds_w = p_w * (dp_w - delta)
            else:
                ds_w = None
            ds_c = p_c * (dp_c - delta)
            return p_w, p_c, ds_w, ds_c

        def ctx_step(i, carry):
            dkc, dvc = carry
            rs = pl.ds(pl.multiple_of(i * 64, 64), 64)
            q4, do4 = _stack_heads(q_ref[rs, :] * NA_SCALE, hm_ref), _stack_heads(do_ref[rs, :], hm_ref)
            _, p_c, _, ds_c = head_terms(q4, do4, None, None, None)
            dq_ref[rs, :] = _fold_heads(b_nn(ds_c, kc), hm_ref) * NA_SCALE
            return dkc + b_tn(ds_c, q4), dvc + b_tn(p_c, do4)

        zc = jnp.zeros((m_ctx, 256), F32)
        carry = lax.fori_loop(0, m_ctx // 64, ctx_step, (zc, zc))

        def lat_step(r, carry):
            dkc, dvc = carry
            start, off = _na_window(r, rows)
            rs = pl.ds(pl.multiple_of(m_ctx + r * 64, 64), 64)
            ws = pl.ds(pl.multiple_of(m_ctx + start * 64, 64), 512)
            q4, do4 = _stack_heads(q_ref[rs, :] * NA_SCALE, hm_ref), _stack_heads(do_ref[rs, :], hm_ref)
            kw, vw = k_ref[ws, :], v_ref[ws, :]
            p_w, p_c, ds_w, ds_c = head_terms(q4, do4, kw, vw, b_ref[off])
            dq_ref[rs, :] = _fold_heads(b_nn(ds_w, kw) + b_nn(ds_c, kc), hm_ref) * NA_SCALE
            dk_ref[ws, :] += b_tn(ds_w, q4)
            dv_ref[ws, :] += b_tn(p_w, do4)
            db_ref[off] += ds_w
            return dkc + b_tn(ds_c, q4), dvc + b_tn(p_c, do4)

        dkc, dvc = lax.fori_loop(0, rows, lat_step, carry)
        dk_ref[0:m_ctx, :] = dkc
        dv_ref[0:m_ctx, :] = dvc

    row = jax.ShapeDtypeStruct((t, 256), F32)
    return _call_with_exchange(body, name, [q, k, v, do, bias8, hm], [row, row, row, jax.ShapeDtypeStruct(bias8.shape, F32)], comm)


def _na_toeplitz():
    col = np.arange(GRID_W)
    dd = (col[None, :] - col[:, None] + 15).reshape(-1)
    tt = np.zeros((GRID_W * GRID_W, 128), np.float32)
    ok = (dd >= 0) & (dd <= 30)
    tt[np.arange(GRID_W * GRID_W)[ok], dd[ok]] = 1.0
    return tt


def _na_bias8(rpb, name):
    col = np.arange(GRID_W)
    cs = np.clip(col - 8, 0, GRID_W - 16)
    col_mask = (col[None, :] >= cs[:, None]) & (col[None, :] < cs[:, None] + 16)
    rpb2 = jnp.pad(rpb.reshape(60, 31), ((0, 4), (0, 97)))
    (toe,) = whole_fwd(lambda r_, t_: (hdot_nt(r_, t_),), name, [rpb2, jnp.asarray(_na_toeplitz())], [(64, GRID_W * GRID_W)])
    toe = toe[:60].reshape(4, 15, GRID_W, GRID_W)
    b = jnp.stack([toe[:, off:off + 8] for off in range(8)], axis=1)
    b = jnp.where(jnp.asarray(col_mask)[None, None, None], b, NEG)
    return b.transpose(1, 0, 3, 2, 4).reshape(8, 4 * GRID_W, 8 * GRID_W)


def _na_rpb_grad(dbias8, name):
    tt = _na_toeplitz()
    sel = np.zeros((64, 256), np.float32)
    for h in range(4):
        for off in range(8):
            for i in range(8):
                sel[h * 15 + off + i, h * 64 + off * 8 + i] = 1.0
    a2 = dbias8.reshape(8, 4, GRID_W, 8, GRID_W).transpose(1, 0, 3, 2, 4).reshape(256, GRID_W * GRID_W)
    (out,) = whole_fwd(lambda a, t_, s_: (hdot(s_, hdot(a, t_)),), name, [a2, jnp.asarray(tt), jnp.asarray(sel)], [(64, 128)])
    return out[:60, :31].reshape(4, 15, 31)


def f_mod(cs, b_mod, w_mod):
    s = _silu(cs)
    return bdot(s, w_mod) + b_mod, s


def loss_and_grad(z, tgt, n_ctx_rows, name, tile=ROW_TILE):
    t, d = z.shape
    tile = min(tile, n_ctx_rows)
    nct = n_ctx_rows // tile

    def body(z_ref, t_ref, dz_ref, loss_ref):
        i = pl.program_id(0)

        @pl.when(i == 0)
        def _():
            loss_ref[...] = jnp.zeros_like(loss_ref)

        @pl.when(i < nct)
        def _():
            dz_ref[...] = jnp.zeros_like(dz_ref)

        @pl.when(i >= nct)
        def _():
            diff = z_ref[...] - t_ref[...]
            dz_ref[...] = diff * (1.0 / d)
            loss_ref[...] += 0.5 * jnp.sum(jnp.sum(diff * diff, axis=-1, keepdims=True) * (1.0 / d), axis=0, keepdims=True)

    dz, loss = pl.pallas_call(
        body, name=name, grid=(t // tile,),
        in_specs=[pl.BlockSpec((tile, d), lambda i: (i, 0)),
                  pl.BlockSpec((tile, d), lambda i: (jnp.maximum(i - nct, 0), 0))],
        out_specs=[pl.BlockSpec((tile, d), lambda i: (i, 0)), pl.BlockSpec((8, 128), lambda i: (0, 0))],
        out_shape=[jax.ShapeDtypeStruct((t, d), F32), jax.ShapeDtypeStruct((8, 128), F32)],
        compiler_params=_cparams(dimension_semantics=("arbitrary",)),
    )(z, tgt)
    return loss[0, 0], dz


def adamw(parts, w, m, v, name, tile=256):
    npart, r, c = parts.shape
    tile = min(tile, r)
    assert r % tile == 0
    c1 = 1.0 / (1.0 - ADAM_B1 ** ADAM_STEP)
    c2 = 1.0 / (1.0 - ADAM_B2 ** ADAM_STEP)

    def body(p_ref, w_ref, m_ref, v_ref, g_ref, d_ref, nm_ref, nv_ref):
        g = p_ref[0].astype(F32)
        for i in range(1, npart):
            g = g + p_ref[i].astype(F32)
        nm = ADAM_B1 * m_ref[...] + (1.0 - ADAM_B1) * g
        nv = ADAM_B2 * v_ref[...] + (1.0 - ADAM_B2) * (g * g)
        g_ref[...] = g
        nm_ref[...] = nm
        nv_ref[...] = nv
        d_ref[...] = -ADAM_LR * ((nm * c1) / (jnp.sqrt(nv * c2) + ADAM_EPS) + ADAM_WD * w_ref[...])

    blk = pl.BlockSpec((tile, c), lambda i: (i, 0))
    return pl.pallas_call(
        body, name=name, grid=(r // tile,),
        in_specs=[pl.BlockSpec((npart, tile, c), lambda i: (0, i, 0)), blk, blk, blk],
        out_specs=[blk] * 4, out_shape=[jax.ShapeDtypeStruct((r, c), F32)] * 4,
        compiler_params=_cparams(dimension_semantics=("arbitrary",)),
    )(parts, w, m, v)


def _peer(x, y, c, k):
    return (1 - x if k & 4 else x, 1 - y if k & 2 else y, 1 - c if k & 1 else c)


def _exchange_out_shapes(arrays, scatter):
    return [jax.ShapeDtypeStruct(a.shape if s else (N_DEV,) + a.shape, a.dtype) for a, s in zip(arrays, scatter)]


def _exchange_sems(n):
    return [pltpu.SemaphoreType.DMA((n, N_DEV - 1)), pltpu.SemaphoreType.DMA((n, N_DEV - 1)), pltpu.SemaphoreType.DMA((n,))]


def _exchange_issue(ins, outs, scatter, send_sems, recv_sems, local_sems):
    n = len(ins)
    x, y, c = lax.axis_index("x"), lax.axis_index("y"), lax.axis_index("c")
    me = 4 * x + 2 * y + c

    def index_of(p):
        return 4 * p[0] + 2 * p[1] + p[2]

    local = []
    for a in range(n):
        src_me = ins[a].at[me] if scatter[a] else ins[a]
        loc = pltpu.make_async_copy(src_me, outs[a].at[me], local_sems.at[a])
        loc.start()
        local.append(loc)
    for k in range(1, N_DEV):
        peer = _peer(x, y, c, k)
        for a in range(n):
            src = ins[a].at[index_of(peer)] if scatter[a] else ins[a]
            pltpu.make_async_remote_copy(
                src_ref=src, dst_ref=outs[a].at[me], send_sem=send_sems.at[a, k - 1], recv_sem=recv_sems.at[a, k - 1],
                device_id=peer, device_id_type=pl.DeviceIdType.MESH).start()

    def finish():
        for k in range(1, N_DEV):
            peer = _peer(x, y, c, k)
            for a in range(n):
                src = ins[a].at[index_of(peer)] if scatter[a] else ins[a]
                cp = pltpu.make_async_remote_copy(
                    src_ref=src, dst_ref=outs[a].at[index_of(peer)], send_sem=send_sems.at[a, k - 1],
                    recv_sem=recv_sems.at[a, k - 1], device_id=peer, device_id_type=pl.DeviceIdType.MESH)
                cp.wait_send()
                cp.wait_recv()
        for loc in local:
            loc.wait()

    return finish


def gather_two_level(arrays, name):
    n = len(arrays)

    def body(*refs):
        ins, outs = refs[:n], refs[n:2 * n]
        send_sems, recv_sems, local_sems = refs[2 * n:]
        x, y, c = lax.axis_index("x"), lax.axis_index("y"), lax.axis_index("c")
        sibling = (x, y, 1 - c)
        chips = [(1 - x, y), (x, 1 - y), (1 - x, 1 - y)]

        def slot(a, p):
            return outs[a].at[4 * p[0] + 2 * p[1] + p[2]]

        def copy(a, k, src, block, to):
            return pltpu.make_async_remote_copy(src_ref=src, dst_ref=slot(a, block), send_sem=send_sems.at[a, k],
                                                recv_sem=recv_sems.at[a, k], device_id=to, device_id_type=pl.DeviceIdType.MESH)

        me = (x, y, c)
        started, local = [], []
        for a in range(n):
            loc = pltpu.make_async_copy(ins[a], slot(a, me), local_sems.at[a])
            loc.start()
            local.append(loc)
            first = [copy(a, 0, ins[a], me, sibling)] + [copy(a, 1 + j, ins[a], me, (*chip, c)) for j, chip in enumerate(chips)]
            for cp in first:
                cp.start()
            started += first
        for j, chip in enumerate(chips):
            for a in range(n):
                copy(a, 1 + j, ins[a], (*chip, c), me).wait_recv()
                fwd = copy(a, 4 + j, slot(a, (*chip, c)), (*chip, c), sibling)
                fwd.start()
                started.append(fwd)
        for a in range(n):
            copy(a, 0, ins[a], sibling, me).wait_recv()
            for j, chip in enumerate(chips):
                copy(a, 4 + j, ins[a], (*chip, 1 - c), me).wait_recv()
        for cp in started:
            cp.wait_send()
        for loc in local:
            loc.wait()

    hbm = pl.BlockSpec(memory_space=pl.ANY)
    return pl.pallas_call(
        body, name=name, in_specs=[hbm] * n, out_specs=[hbm] * n, out_shape=_exchange_out_shapes(arrays, [False] * n),
        scratch_shapes=_exchange_sems(n), compiler_params=pltpu.CompilerParams(has_side_effects=True),
    )(*arrays)


def exchange(arrays, scatter, name):
    n = len(arrays)

    def body(*refs):
        _exchange_issue(refs[:n], refs[n:2 * n], scatter, *refs[2 * n:])()

    hbm = pl.BlockSpec(memory_space=pl.ANY)
    return pl.pallas_call(
        body, name=name, in_specs=[hbm] * n, out_specs=[hbm] * n, out_shape=_exchange_out_shapes(arrays, scatter),
        scratch_shapes=_exchange_sems(n), compiler_params=pltpu.CompilerParams(has_side_effects=True),
    )(*arrays)


def _rope_tables(n_lat, n_ctx):
    tok = np.arange(n_lat)
    freqs = 10000.0 ** (-np.arange(0, 16, 2, dtype=np.float32) / 16.0)

    def table(pos):
        ang = pos.astype(np.float32)[:, None] * freqs[None, :]
        ang = np.concatenate([ang, ang], axis=-1)
        return np.cos(ang), np.sin(ang)

    cr, sr = table(tok // GRID_W)
    cc, sc = table(tok % GRID_W)
    cos = np.tile(np.concatenate([cr, cc], axis=-1), (1, 4))
    sin = np.tile(np.concatenate([sr, sc], axis=-1), (1, 4))
    cos = np.concatenate([np.ones((n_ctx, 128), np.float32), cos], axis=0)
    sin = np.concatenate([np.zeros((n_ctx, 128), np.float32), sin], axis=0)
    return jnp.asarray(cos, F32), jnp.asarray(sin, F32)


def _pad_w_in(w):
    z = lambda n: jnp.zeros((w.shape[0], n), w.dtype)
    return jnp.concatenate([w[:, 1824:2848], w[:, 128:384], w[:, 416:672], w[:, 672:928], w[:, 928:1184], w[:, 1312:1568],
                            w[:, 1568:1824], w[:, 0:128], w[:, 384:416], z(96), w[:, 1184:1312], z(128)], axis=1)


def _unpad_w_in(wp):
    return jnp.concatenate([wp[:, C_GK:C_GK + 128], wp[:, C_GV:C_GV + 256], wp[:, C_GG:C_GG + 32], wp[:, C_NK:C_NK + 256],
                            wp[:, C_NV:C_NV + 256], wp[:, C_SU:C_SU + 256], wp[:, C_GQ:C_GQ + 128], wp[:, C_NQ:C_NQ + 256],
                            wp[:, C_PU:C_PU + 256], wp[:, C_GT:C_GT + 1024]], axis=1)


_W_IN_SEGS = [(0, 128, C_GK), (128, 256, C_GV), (384, 32, C_GG), (416, 256, C_NK), (672, 256, C_NV), (928, 256, C_SU),
              (1184, 128, C_GQ), (1312, 256, C_NQ), (1568, 256, C_PU), (1824, 1024, C_GT)]
W_IN_SHARD = N_IN // N_DEV


def _pad_w_in_blocks(blocks):
    pieces = []
    for orig, width, padded in _W_IN_SEGS:
        col = orig
        while col < orig + width:
            dev, lo = divmod(col, W_IN_SHARD)
            n = min(W_IN_SHARD - lo, orig + width - col)
            pieces.append((padded + col - orig, blocks[dev][:, lo:lo + n]))
            col += n
    pieces.sort(key=lambda p: p[0])
    out, at = [], 0
    for start, piece in pieces:
        if start > at:
            out.append(jnp.zeros((blocks.shape[1], start - at), blocks.dtype))
        out.append(piece)
        at = start + piece.shape[1]
    out.append(jnp.zeros((blocks.shape[1], PW - at), blocks.dtype))
    return jnp.concatenate(out, axis=1)


def _w_in_slabs(wp_blocks):
    slabs = []
    for dev in range(N_DEV):
        first, pieces = dev * W_IN_SHARD, []
        for orig, width, padded in _W_IN_SEGS:
            lo, hi = max(orig, first), min(orig + width, first + W_IN_SHARD)
            if lo < hi:
                a = padded + lo - orig
                blk, off = divmod(a, 1024)
                assert off + (hi - lo) <= 1024
                pieces.append(wp_blocks[blk][:, off:off + hi - lo])
        slabs.append(jnp.concatenate(pieces, axis=1))
    return jnp.stack(slabs)


def _pad_rows(u):
    return jnp.pad(u, ((POOL_HALO, POOL_HALO), (0, 0)))


def _block_diag4(w):
    out = jnp.zeros((256, 256), w.dtype)
    for i in range(4):
        out = lax.dynamic_update_slice(out, w[i], (64 * i, 64 * i))
    return out


def _layer_params(p, big, l):
    e_rep, e_tile, gmask, bdm = _s5_consts()
    wg = jnp.zeros((128, 256), F32)
    wg = lax.dynamic_update_slice(wg, p["gla_w_gate"][l, 0], (0, 0))
    wg = lax.dynamic_update_slice(wg, p["gla_w_gate"][l, 1], (16, 128))
    s5 = []
    for d in range(2):
        s5.append([p["s5_lam_re"][l, d], p["s5_lam_im"][l, d], p["s5_log_dt"][l, d].reshape(16, 1),
                   p["s5_b_re"][l, d].transpose(0, 2, 1).reshape(256, 64), p["s5_b_im"][l, d].transpose(0, 2, 1).reshape(256, 64),
                   p["s5_c_re"][l, d].reshape(256, 64), p["s5_c_im"][l, d].reshape(256, 64), e_rep, e_tile, gmask, bdm])
    havg = jnp.asarray((np.arange(256)[:, None] // 64 == np.arange(256)[None, :] // 64).astype(np.float32) / 64.0)
    e4 = jnp.asarray((np.arange(64)[:, None] == np.arange(256)[None, :] % 64).astype(np.float32))
    return dict(
        g_pre=p["g_pre"][l].reshape(1, D), g_post=p["g_post"][l].reshape(1, D), b_mod=p["b_mod"][l].reshape(1, 3 * D),
        w_mod=big["w_mod"], w_in=_pad_w_in_blocks(big["w_in_blocks"]) if "w_in_blocks" in big else _pad_w_in(big["w_in"]), w_out=big.get("w_out"),
        wg=wg, bg=p["gla_b_gate"][l].reshape(1, 256), g_norm=jnp.pad(p["gla_g_norm"][l].reshape(1, 64), ((0, 7), (0, 0))),
        bias8=_na_bias8(p["na_rpb"][l], f"na_bias_l{l}"), s5=s5, s5_d=p["s5_d"][l].reshape(1, 256), w_glu=None if big.get("s5_w_glu") is None else big["s5_w_glu"].astype(F32),
        b_glu=p["s5_b_glu"][l].reshape(1, 256), wpool=_block_diag4(p["pool_w"][l]), pool_scale=p["pool_scale"][l].reshape(1, 256),
        havg=havg, e4=e4)


def _matmul_tile(t, tile, steps):
    return t // steps if t % (8 * steps) == 0 else tile


def _cols(pz, start, width):
    return pz[:, start:start + width]


def _layer_fwd(z, modseg, lp, cos, sin, m_ctx, tile, s5_chunk, l, comm=None, late=None):
    t = z.shape[0]
    nct = m_ctx // tile
    nm = lambda s: f"{s}_l{l}"
    (h,) = rowwise_fwd(f_pre, nm("pre"), [z], [modseg], [lp["g_pre"]], [D], tile, nct)
    mm_tile = _matmul_tile(t, tile, 4)
    (gt,) = mm_nn_cols(h, lp["w_in"], C_GT, [1024], nm("in_proj_a"), tm=mm_tile)
    pv, nk, nv, su = mm_nn_cols(h, lp["w_in"], C_GV, [256] * 4, nm("in_proj_b"), tm=mm_tile)
    nq, pu, pk, pg, pq = mm_nn_cols(h, lp["w_in"], C_NQ, [256, 256, 128, 128, 128], nm("in_proj_c"), tm=mm_tile)
    q_r, k_r, lgf, lgb = rowwise_fwd(f_gla_prep, nm("gla_prep"), [pk, pg, pq, cos, sin], [], [lp["wg"], lp["bg"]], [128] * 4, tile, nct)
    half = None if comm is None else comm[0].shape[0] // 2
    spread = None if comm is None else [comm[0][:half], comm[1], comm[2], comm[3], comm[0][half:]]
    part = (lambda idx: None) if comm is None else (lambda idx: ([spread[i] for i in idx], [False] * len(idx)))
    (o1, st_f), got_late = gla_scan_fwd(q_r, k_r, pv, lgf, jnp.zeros((t, 256), F32), m_ctx, False, nm("gla_f"),
                                        None if late is None else (list(late), [False, False]))
    if late is not None:
        lp["w_out"], lp["w_glu"] = _gathered(got_late[0], False), _gathered(got_late[1], False).astype(F32)
    (o_gla, st_b), got_out = gla_scan_fwd(q_r, k_r, pv, lgb, o1, m_ctx, True, nm("gla_r"), part([2, 3]))
    received = None
    if comm is None:
        o_na = na_fwd(nq, nk, nv, lp["bias8"], m_ctx, nm("na"))
    else:
        o_na, got_in = na_fwd(nq, nk, nv, lp["bias8"], m_ctx, nm("na"), part([1]))
    s5p = [whole_fwd(f_s5_params, nm(f"s5_par{d}"), lp["s5"][d], [(1, 1024)] * 2 + [(256, 1024)] * 4) for d in range(2)]
    (y1, *states_f), got_mod_a = s5_scan_fwd(su, jnp.zeros((t, 256), F32), *s5p[0], m_ctx, s5_chunk, False, nm("s5_f"), part([0]))
    (y5, *states_b), got_mod_b = s5_scan_fwd(su, y1, *s5p[1], m_ctx, s5_chunk, True, nm("s5_r"), part([4]))
    if comm is not None:
        received = [jnp.concatenate([got_mod_a[0], got_mod_b[0]], axis=1), got_in[0], got_out[0], got_out[1]]
    pm = jnp.concatenate([pool_apply(_pad_rows(pu[:m_ctx]), m_ctx, False, nm("pool_c")),
                          pool_apply(_pad_rows(pu[m_ctx:]), t - m_ctx, False, nm("pool_x"))], axis=0)
    mix_rows = [o_gla, o_na, y5, su, pm, gt]
    mix_globs = [lp["g_norm"], lp["s5_d"], lp["w_glu"], lp["b_glu"], lp["wpool"], lp["pool_scale"], lp["havg"], lp["e4"]]
    (yg,) = rowwise_fwd(f_mix, nm("mix"), mix_rows, [], mix_globs, [D], tile, nct)
    out = mm_nn([yg], lp["w_out"], nm("out_proj"), tm=mm_tile)
    (z_new,) = rowwise_fwd(f_post, nm("post"), [z, out], [modseg], [lp["g_post"]], [D], tile, nct)
    saved = dict(z=z, h=h, pv=pv, nk=nk, nv=nv, su=su, nq=nq, pk=pk, pg=pg, pq=pq, q_r=q_r, k_r=k_r, lgf=lgf, lgb=lgb,
                 st_f=st_f, st_b=st_b, s5p=s5p, x0f=tuple(states_f), x0b=tuple(states_b), mix_rows=mix_rows, mix_globs=mix_globs,
                 yg=yg, out=out)
    return z_new, saved, received


def _f_pre_res(x, mod, g_pre):
    return f_pre(x, mod, g_pre)[0], x


def _layer_bwd(dz_new, sv, modseg, lp, cos, sin, m_ctx, tile, s5_chunk, l, comm=None, gdt=F32, send_early=False, as_slabs=False):
    t = dz_new.shape[0]
    nct = m_ctx // tile
    nm = lambda s: f"{s}_l{l}"
    g = {}
    dz_res, dout, dmod_post, g["g_post"] = rowwise_bwd(f_post, nm("post_b"), [sv["z"], sv["out"]], [modseg], [lp["g_post"]],
                                                       [dz_new], tile, nct, [True, True], [True])
    dyg = mm_nt([dout], lp["w_out"], nm("out_proj_dx"), tm=_matmul_tile(t, tile, 4))
    dw_tile = _matmul_tile(t, tile, 4)
    (g["w_out"],) = mm_tn(sv["yg"], [dout], nm("out_proj_dw"), tm=dw_tile, out_dtype=gdt)
    res = rowwise_bwd(f_mix, nm("mix_b"), sv["mix_rows"], [], sv["mix_globs"], [dyg], tile, nct, [True] * 6, [True] * 6 + [False] * 2)
    do_gla, do_na, dy5, dsu_a, dpm, dgt = res[:6]
    g["g_norm"], g["s5_d"], g["w_glu"], g["b_glu"], g["wpool"], g["pool_scale"] = res[6:]
    dpu = jnp.concatenate([pool_apply(_pad_rows(dpm[:m_ctx]), m_ctx, True, nm("pool_c_b")),
                           pool_apply(_pad_rows(dpm[m_ctx:]), t - m_ctx, True, nm("pool_x_b"))], axis=0)
    r_b = s5_scan_bwd(sv["su"], dy5, dsu_a, *sv["x0b"], *sv["s5p"][1], m_ctx, s5_chunk, True, nm("s5_r_b"))
    r_f = s5_scan_bwd(sv["su"], dy5, r_b[0], *sv["x0f"], *sv["s5p"][0], m_ctx, s5_chunk, False, nm("s5_f_b"))
    dsu = r_f[0]
    g["s5"] = [whole_bwd(f_s5_params, nm(f"s5_par{d}_b"), lp["s5"][d], list(r[1:]), [True] * 7 + [False] * 4)
               for d, r in ((0, r_f), (1, r_b))]
    part = (lambda idx: None) if comm is None else (lambda idx: ([comm[i] for i in idx], [True] * len(idx)))
    (dnq, dnk, dnv, dbias8), got_in = na_bwd(sv["nq"], sv["nk"], sv["nv"], do_na, lp["bias8"], m_ctx, nm("na_b"), part([0]))
    g["rpb"] = _na_rpb_grad(dbias8, nm("na_rpb_b"))
    zq, zv = jnp.zeros((t, 128), F32), jnp.zeros((t, 256), F32)
    early = ([_slabs(g["w_out"], False).astype(BF16), _slabs(g["w_glu"], False).astype(BF16)], [True, True]) if send_early else None
    (dq1, dk1, dv1, dlgb), g["early"] = gla_scan_bwd(sv["q_r"], sv["k_r"], sv["pv"], sv["lgb"], sv["st_b"], do_gla, (zq, zq, zv), m_ctx, True,
                                                     nm("gla_r_b"), early)
    (dq_r, dk_r, dpv, dlgf), got_out = gla_scan_bwd(sv["q_r"], sv["k_r"], sv["pv"], sv["lgf"], sv["st_f"], do_gla, (dq1, dk1, dv1), m_ctx, False,
                                                    nm("gla_f_b"), part([1, 2]))
    received = None if comm is None else [got_in[0], got_out[0], got_out[1]]
    dpk, dpg, dpq, g["wg"], g["bg"] = rowwise_bwd(f_gla_prep, nm("gla_prep_b"), [sv["pk"], sv["pg"], sv["pq"], cos, sin], [],
                                                  [lp["wg"], lp["bg"]], [dq_r, dk_r, dlgf, dlgb], tile, nct,
                                                  [True, True, True, False, False], [True, True])
    parts = [dgt, dpv, dnk, dnv, dsu, dnq, dpu, dpk, dpg, dpq, jnp.zeros((t, 128), F32)]
    dh = mm_nt(parts, lp["w_in"], nm("in_proj_dx"), tm=_matmul_tile(t, tile, 8))
    dw_blocks = mm_tn(sv["h"], parts, nm("in_proj_dw"), tm=dw_tile, out_dtype=gdt)
    if as_slabs:
        g["w_in_slabs"] = _w_in_slabs(dw_blocks)
    else:
        g["w_in"] = _unpad_w_in(jnp.concatenate(dw_blocks, axis=1))
    dz, dmod_pre, g["g_pre"] = rowwise_bwd(_f_pre_res, nm("pre_b"), [sv["z"]], [modseg], [lp["g_pre"]], [dh, dz_res], tile, nct, [True], [True])
    return dz, dmod_pre, dmod_post, g, received


def _f_mod_sum(cs, b_mod, w_mod):
    mod, _ = f_mod(cs, b_mod, w_mod)
    return mod, cs


def local_step(x, c, ctx, tgt, p, shards=None, tile=ROW_TILE, s5_chunk=S5_CHUNK):
    n_lat, m_ctx = x.shape[0], ctx.shape[0]
    n_layers = p["g_pre"].shape[0]
    z = jnp.concatenate([ctx, x], axis=0)
    cos, sin = _rope_tables(n_lat, m_ctx)
    cs = jnp.concatenate([c.reshape(1, D), p["c_ctx"].reshape(1, D), jnp.zeros((6, D), F32)], axis=0)
    gather = [False] * len(_SHARDED)
    lps, mods, silus, saves = [], [], [], []
    got = gather_two_level(shards[0][:2], "gather_weights_l0") if shards is not None else None
    for l in range(n_layers):
        if shards is None:
            big = {n: p[n][l] for n in _SHARDED}
        else:
            big = {n: _gathered(g, _BY_COLS[n]) for n, g in zip(_SHARDED, got) if n != "w_in"}
            big["w_in_blocks"] = got[_SHARDED.index("w_in")]
        lp = _layer_params(p, big, l)
        mod8, s8 = whole_fwd(f_mod, f"mod_l{l}", [cs, lp["b_mod"], lp["w_mod"]], [(8, 3 * D), (8, D)])
        modseg = mod8[:2].reshape(2, 1, 3 * D)
        comm = shards[l + 1] if shards is not None and l + 1 < n_layers else None
        late = shards[0][2:] if shards is not None and l == 0 else None
        z, sv, got = _layer_fwd(z, modseg, lp, cos, sin, m_ctx, tile, s5_chunk, l, comm, late)
        lps.append(lp); mods.append(modseg); silus.append(s8); saves.append(sv)
    loss, dz = loss_and_grad(z, tgt, m_ctx, "loss", tile)
    grads, received = [None] * n_layers, [None] * n_layers
    gdt = F32 if shards is None else BF16
    dcs = jnp.zeros((8, D), F32)
    pending = None
    for l in reversed(range(n_layers)):
        lp = lps[l]
        dz, dmod_pre, dmod_post, g, got = _layer_bwd(dz, saves[l], mods[l], lp, cos, sin, m_ctx, tile, s5_chunk, l, pending, gdt,
                                                     send_early=shards is not None and l == 0, as_slabs=shards is not None)
        if pending is not None:
            received[l + 1] = got
        dmod = jnp.concatenate([dmod_pre.reshape(2, 3 * D)[:, :2 * D], dmod_post.reshape(2, 3 * D)[:, 2 * D:]], axis=1)
        dmod8 = jnp.pad(dmod, ((0, 6), (0, 0)))
        dcs, g["b_mod"] = whole_bwd(_f_mod_sum, f"mod_b_l{l}", [cs, lp["b_mod"], lp["w_mod"]], [dmod8, dcs], [True, True, False])
        if shards is None:
            g["w_mod"] = jnp.concatenate(mm_tn(silus[l], [dmod8[:, :D], dmod8[:, D:2 * D], dmod8[:, 2 * D:]], f"mod_dw_l{l}", tm=8), axis=1)
        else:
            g["mod_s"], g["mod_d"] = silus[l][:2], dmod
        grads[l] = g
        if shards is not None:
            pending = _layer_sends(g)
    if shards is not None:
        (got_in,) = exchange([pending[0]], [True], "exchange_grads_l0")
        (got_small,) = gather_two_level([_small_sends(dcs[1], grads)], "gather_small_grads")
        received[0] = [got_in] + list(grads[0]["early"]) + [got_small]
    return loss, dz[m_ctx:], dcs[1], grads, received


_WEIGHTS = ["c_ctx", "w_mod", "b_mod", "g_pre", "g_post", "w_in", "w_out", "gla_w_gate", "gla_b_gate", "gla_g_norm", "na_rpb",
            "s5_lam_re", "s5_lam_im", "s5_log_dt", "s5_b_re", "s5_b_im", "s5_c_re", "s5_c_im", "s5_d", "s5_w_glu", "s5_b_glu",
            "pool_w", "pool_scale"]
_INPUTS = ["x", "c", "ctx"] + _WEIGHTS + ["loss_target"] + ["m_" + n for n in _WEIGHTS] + ["v_" + n for n in _WEIGHTS]
_SHARDED = ["w_mod", "w_in", "w_out", "s5_w_glu"]
_BY_COLS = {"w_mod": True, "w_in": True, "w_out": False, "s5_w_glu": False}
_GRAD_SHARDED = ["w_in", "w_out", "s5_w_glu"]
_SMALL = [n for n in _WEIGHTS if n not in _SHARDED]
_SMALL_PER_LAYER = [n for n in _SMALL if n != "c_ctx"]
_PACK_ROWS = 256


def _pack_plan(like):
    tiled = [i for i, a in enumerate(like) if a.size % 1024 == 0]
    loose = [i for i, a in enumerate(like) if a.size % 1024 != 0]
    tail = -(-sum(like[i].size for i in loose) // 1024) * 8
    rows = sum(like[i].size // 128 for i in tiled) + tail
    return tiled, loose, tail, -(-rows // _PACK_ROWS) * _PACK_ROWS - rows


def _pack_rows(like, index):
    tiled, _, _, _ = _pack_plan(like)
    row = 0
    for i in tiled:
        n = like[i].size // 128
        if i == index:
            return row, row + n
        row += n
    raise ValueError("not a tile-aligned entry")


def _pack(arrs):
    tiled, loose, tail, fill = _pack_plan(arrs)
    dt = arrs[0].dtype
    flat = jnp.concatenate([arrs[i].reshape(-1) for i in loose])
    flat = jnp.pad(flat, (0, tail * 128 - flat.shape[0])).reshape(tail, 128)
    return jnp.concatenate([arrs[i].reshape(-1, 128) for i in tiled] + [flat, jnp.zeros((fill, 128), dt)], axis=0)


def _unpack(packed, like):
    tiled, loose, tail, _ = _pack_plan(like)
    out, row = [None] * len(like), 0
    for i in tiled:
        n = like[i].size // 128
        out[i] = packed[row:row + n].reshape(like[i].shape)
        row += n
    flat, pos = packed[row:row + tail].reshape(-1), 0
    for i in loose:
        out[i] = flat[pos:pos + like[i].size].reshape(like[i].shape)
        pos += like[i].size
    return out


def _gathered(g, cols):
    if cols:
        return g.transpose(1, 0, 2).reshape(g.shape[1], N_DEV * g.shape[2])
    return g.reshape(N_DEV * g.shape[1], g.shape[2])


def _slabs(w, cols):
    r, c = w.shape
    if cols:
        return w.reshape(r, N_DEV, c // N_DEV).transpose(1, 0, 2)
    return w.reshape(N_DEV, r // N_DEV, c)


def _layer_small(g):
    s5 = lambda i, f: jnp.stack([f(g["s5"][d][i]) for d in range(2)])
    return {
        "b_mod": g["b_mod"].reshape(3 * D), "g_pre": g["g_pre"].reshape(D), "g_post": g["g_post"].reshape(D),
        "gla_w_gate": jnp.stack([g["wg"][0:16, 0:128], g["wg"][16:32, 128:256]]),
        "gla_b_gate": g["bg"].reshape(2, 128), "gla_g_norm": g["g_norm"][0], "na_rpb": g["rpb"],
        "s5_lam_re": s5(0, lambda a: a), "s5_lam_im": s5(1, lambda a: a), "s5_log_dt": s5(2, lambda a: a.reshape(16)),
        "s5_b_re": s5(3, lambda a: a.reshape(16, 16, 64).transpose(0, 2, 1)),
        "s5_b_im": s5(4, lambda a: a.reshape(16, 16, 64).transpose(0, 2, 1)),
        "s5_c_re": s5(5, lambda a: a.reshape(16, 16, 64)), "s5_c_im": s5(6, lambda a: a.reshape(16, 16, 64)),
        "s5_d": g["s5_d"].reshape(256), "s5_b_glu": g["b_glu"].reshape(256),
        "pool_w": jnp.stack([g["wpool"][64 * i:64 * i + 64, 64 * i:64 * i + 64] for i in range(4)]),
        "pool_scale": g["pool_scale"].reshape(256),
    }


def _layer_sends(g):
    return [g["w_in_slabs"].astype(BF16), _slabs(g["w_out"], False).astype(BF16), _slabs(g["w_glu"], False).astype(BF16)]


def _small_sends(d_c_ctx, grads):
    per_layer = [_layer_small(g) for g in grads]
    full = {n: jnp.stack([s[n] for s in per_layer]) for n in _SMALL_PER_LAYER}
    full["c_ctx"] = d_c_ctx
    factors = [jnp.stack([g["mod_s"] for g in grads]), jnp.stack([g["mod_d"] for g in grads])]
    return _pack([full[n] for n in _SMALL] + factors).astype(BF16)


def kernel(x, c, ctx, c_ctx, w_mod, b_mod, g_pre, g_post, w_in, w_out, gla_w_gate, gla_b_gate, gla_g_norm, na_rpb, s5_lam_re, s5_lam_im, s5_log_dt, s5_b_re, s5_b_im, s5_c_re, s5_c_im, s5_d, s5_w_glu, s5_b_glu, pool_w, pool_scale, loss_target, m_c_ctx, m_w_mod, m_b_mod, m_g_pre, m_g_post, m_w_in, m_w_out, m_gla_w_gate, m_gla_b_gate, m_gla_g_norm, m_na_rpb, m_s5_lam_re, m_s5_lam_im, m_s5_log_dt, m_s5_b_re, m_s5_b_im, m_s5_c_re, m_s5_c_im, m_s5_d, m_s5_w_glu, m_s5_b_glu, m_pool_w, m_pool_scale, v_c_ctx, v_w_mod, v_b_mod, v_g_pre, v_g_post, v_w_in, v_w_out, v_gla_w_gate, v_gla_b_gate, v_gla_g_norm, v_na_rpb, v_s5_lam_re, v_s5_lam_im, v_s5_log_dt, v_s5_b_re, v_s5_b_im, v_s5_c_re, v_s5_c_im, v_s5_d, v_s5_w_glu, v_s5_b_glu, v_pool_w, v_pool_scale):
    given = dict(zip(_INPUTS, (x, c, ctx, c_ctx, w_mod, b_mod, g_pre, g_post, w_in, w_out, gla_w_gate, gla_b_gate, gla_g_norm, na_rpb, s5_lam_re, s5_lam_im, s5_log_dt, s5_b_re, s5_b_im, s5_c_re, s5_c_im, s5_d, s5_w_glu, s5_b_glu, pool_w, pool_scale, loss_target, m_c_ctx, m_w_mod, m_b_mod, m_g_pre, m_g_post, m_w_in, m_w_out, m_gla_w_gate, m_gla_b_gate, m_gla_g_norm, m_na_rpb, m_s5_lam_re, m_s5_lam_im, m_s5_log_dt, m_s5_b_re, m_s5_b_im, m_s5_c_re, m_s5_c_im, m_s5_d, m_s5_w_glu, m_s5_b_glu, m_pool_w, m_pool_scale, v_c_ctx, v_w_mod, v_b_mod, v_g_pre, v_g_post, v_w_in, v_w_out, v_gla_w_gate, v_gla_b_gate, v_gla_g_norm, v_na_rpb, v_s5_lam_re, v_s5_lam_im, v_s5_log_dt, v_s5_b_re, v_s5_b_im, v_s5_c_re, v_s5_c_im, v_s5_d, v_s5_w_glu, v_s5_b_glu, v_pool_w, v_pool_scale)))
    n_layers = w_in.shape[0]
    shards = [[given[n][l].astype(BF16) for n in _SHARDED] for l in range(n_layers)]
    p = {n: given[n] for n in _SMALL}
    loss, grad_x, _, _, received = local_step(x[0], c, ctx[0], loss_target[0], p, shards)
    final = {}
    for n in _GRAD_SHARDED:
        per_layer = [adamw(received[l][_GRAD_SHARDED.index(n)], given[n][l], given["m_" + n][l], given["v_" + n][l], f"adamw_{n}_l{l}")
                     for l in range(n_layers)]
        final[n] = [jnp.stack([res[kind] for res in per_layer]) for kind in range(4)]
    factor_like = [jnp.zeros((n_layers, 2, D), F32), jnp.zeros((n_layers, 2, 3 * D), F32)]
    like = [given[n] for n in _SMALL] + factor_like
    small_recv = received[0][-1]
    rows_s, rows_d = _pack_rows(like, len(_SMALL)), _pack_rows(like, len(_SMALL) + 1)
    fac_s = small_recv[:, rows_s[0]:rows_s[1]].reshape(N_DEV, n_layers, 2, D)
    fac_d = small_recv[:, rows_d[0]:rows_d[1]].reshape(N_DEV, n_layers, 2, 3 * D)
    me = 4 * lax.axis_index("x") + 2 * lax.axis_index("y") + lax.axis_index("c")
    cols = w_mod.shape[2]
    per_layer = []
    for l in range(n_layers):
        s_all = fac_s[:, l].reshape(2 * N_DEV, D)
        d_mine = lax.dynamic_slice_in_dim(fac_d[:, l].reshape(2 * N_DEV, 3 * D), me * cols, cols, axis=1)
        (g_mod,) = mm_tn(s_all, [d_mine], f"mod_dw_l{l}", tm=2 * N_DEV, tn=cols)
        per_layer.append(adamw(g_mod[None], given["w_mod"][l], given["m_w_mod"][l], given["v_w_mod"][l], f"adamw_w_mod_l{l}"))
    final["w_mod"] = [jnp.stack([res[kind] for res in per_layer]) for kind in range(4)]
    res = adamw(small_recv, _pack(like), _pack([given["m_" + n] for n in _SMALL] + factor_like),
                _pack([given["v_" + n] for n in _SMALL] + factor_like), "adamw_small")
    unpacked = [_unpack(packed, like) for packed in res]
    for i, n in enumerate(_SMALL):
        final[n] = [unpacked[kind][i] for kind in range(4)]
    loss = lax.psum(loss, ("x", "y", "c"))
    return (loss, grad_x[None], *[final[n][0] for n in _WEIGHTS], *[final[n][1] for n in _WEIGHTS],
            *[final[n][2] for n in _WEIGHTS], *[final[n][3] for n in _WEIGHTS])
```

```python
import functools
import math

import numpy as np
import jax
import jax.numpy as jnp
from jax import lax
from jax.experimental import pallas as pl
from jax.experimental.pallas import tpu as pltpu

F32 = jnp.float32
BF16 = jnp.bfloat16
HIGHEST = lax.Precision.HIGHEST
HIGH = lax.Precision.HIGH

D = 1024
GRID_W = 64
EPS = 1e-6
N_DEV = 8
C_GT, C_GV, C_NK, C_NV, C_SU, C_NQ, C_PU, C_GK, C_GG, C_GQ, C_END = 0, 1024, 1280, 1536, 1792, 2048, 2304, 2560, 2688, 2816, 2944
PW = 3072
N_CTX_ORIG = 416
N_IN = 2848
GLA_CHUNK = 128
S5_CHUNK = 256
ROW_TILE = 256
VMEM_LIMIT = 56 * 1024 * 1024

ADAM_LR, ADAM_B1, ADAM_B2, ADAM_EPS, ADAM_WD, ADAM_STEP = 0.001, 0.9, 0.999, 1e-08, 0.01, 10


def _cparams(**kw):
    return pltpu.CompilerParams(vmem_limit_bytes=VMEM_LIMIT, **kw)


def _dg(a, b, ca, cb, precision=None):
    return lax.dot_general(a, b, (((ca,), (cb,)), ((), ())), precision=precision, preferred_element_type=F32)


def hdot(a, b):
    return _dg(a, b, 1, 0, HIGHEST)


def hdot_nt(a, b):
    return _dg(a, b, 1, 1, HIGHEST)


def hdot_tn(a, b):
    return _dg(a, b, 0, 0, HIGHEST)


def mdot(a, b):
    return _dg(a, b, 1, 0, HIGH)


def mdot_nt(a, b):
    return _dg(a, b, 1, 1, HIGH)


def mdot_tn(a, b):
    return _dg(a, b, 0, 0, HIGH)


def b_nn(a, b):
    return _dg(a.astype(BF16), b.astype(BF16), 1, 0)


def b_nt(a, b):
    return _dg(a.astype(BF16), b.astype(BF16), 1, 1)


def b_tn(a, b):
    return _dg(a.astype(BF16), b.astype(BF16), 0, 0)


@jax.custom_vjp
def bdot(a, b):
    return b_nn(a, b)


def _bdot_fwd(a, b):
    return b_nn(a, b), (a, b)


def _bdot_bwd(res, ct):
    a, b = res
    return b_nt(ct, b).astype(a.dtype), b_tn(a, ct).astype(b.dtype)


bdot.defvjp(_bdot_fwd, _bdot_bwd)


def _log_sigmoid(z):
    return jnp.minimum(z, 0.0) - jnp.log(1.0 + jnp.exp(-jnp.abs(z)))


def _silu(z):
    return z * jax.nn.sigmoid(z)


def _gelu(z):
    return 0.5 * z * (1.0 + jnp.tanh(math.sqrt(2.0 / math.pi) * (z + 0.044715 * (z * z * z))))


def _cat(vals):
    return vals[0] if len(vals) == 1 else jnp.concatenate(vals, axis=-1)


def mm_nn(a_parts, b, name, tm=ROW_TILE, tn=1024):
    t = a_parts[0].shape[0]
    k, n = b.shape
    na = len(a_parts)
    tn = min(tn, n)

    def body(*refs):
        a = _cat([r[...].astype(BF16) for r in refs[:na]])
        refs[na + 1][...] = _dg(a, refs[na][...].astype(BF16), 1, 0)

    return pl.pallas_call(
        body, name=name, grid=(n // tn, t // tm),
        in_specs=[pl.BlockSpec((tm, p.shape[1]), lambda j, i: (i, 0)) for p in a_parts]
        + [pl.BlockSpec((k, tn), lambda j, i: (0, j))],
        out_specs=pl.BlockSpec((tm, tn), lambda j, i: (i, j)),
        out_shape=jax.ShapeDtypeStruct((t, n), F32),
        compiler_params=_cparams(dimension_semantics=("arbitrary", "arbitrary")),
    )(*a_parts, b)


def mm_nn_cols(a, b, start, widths, name, tm=ROW_TILE):
    t, k = a.shape
    tn = 1024
    assert start % tn == 0 and sum(widths) <= tn

    def body(a_ref, b_ref, *o_refs):
        r = _dg(a_ref[...].astype(BF16), b_ref[...].astype(BF16), 1, 0)
        off = 0
        for o_ref, w in zip(o_refs, widths):
            o_ref[...] = r[:, off:off + w]
            off += w

    return pl.pallas_call(
        body, name=name, grid=(t // tm,),
        in_specs=[pl.BlockSpec((tm, k), lambda i: (i, 0)), pl.BlockSpec((k, tn), lambda i: (0, start // tn))],
        out_specs=[pl.BlockSpec((tm, w), lambda i: (i, 0)) for w in widths],
        out_shape=[jax.ShapeDtypeStruct((t, w), F32) for w in widths],
        compiler_params=_cparams(dimension_semantics=("arbitrary",)),
    )(a, b)


def mm_nt(a_parts, b, name, tm=ROW_TILE):
    t = a_parts[0].shape[0]
    n, k = b.shape
    na = len(a_parts)

    def body(*refs):
        a = _cat([r[...].astype(BF16) for r in refs[:na]])
        refs[na + 1][...] = _dg(a, refs[na][...].astype(BF16), 1, 1)

    return pl.pallas_call(
        body, name=name, grid=(t // tm,),
        in_specs=[pl.BlockSpec((tm, p.shape[1]), lambda i: (i, 0)) for p in a_parts]
        + [pl.BlockSpec((n, k), lambda i: (0, 0))],
        out_specs=pl.BlockSpec((tm, n), lambda i: (i, 0)),
        out_shape=jax.ShapeDtypeStruct((t, n), F32),
        compiler_params=_cparams(dimension_semantics=("arbitrary",)),
    )(*a_parts, b)


def mm_tn(a, b_parts, name, tm=ROW_TILE, tn=1024, out_dtype=F32):
    t, k = a.shape
    widths = [p.shape[1] for p in b_parts]
    n = sum(widths)
    assert n % tn == 0
    groups, cur, acc = [], [], 0
    for idx, w in enumerate(widths):
        cur.append(idx)
        acc += w
        if acc == tn:
            groups.append(cur)
            cur, acc = [], 0
        assert acc < tn
    assert not cur
    outs = []
    for gi, grp in enumerate(groups):
        parts = [b_parts[i] for i in grp]
        npart = len(parts)
        nsteps = t // tm

        def body(*refs, npart=npart, nsteps=nsteps):
            a_v = refs[0][...].astype(BF16)
            b_v = _cat([r[...].astype(BF16) for r in refs[1:1 + npart]])
            o_ref, acc_ref = refs[1 + npart], refs[2 + npart]
            r = _dg(a_v, b_v, 0, 0)

            @pl.when(pl.program_id(0) == 0)
            def _():
                acc_ref[...] = r

            @pl.when(pl.program_id(0) != 0)
            def _():
                acc_ref[...] += r

            @pl.when(pl.program_id(0) == nsteps - 1)
            def _():
                o_ref[...] = acc_ref[...].astype(o_ref.dtype)

        outs.append(pl.pallas_call(
            body, name=f"{name}_{gi}", grid=(nsteps,),
            in_specs=[pl.BlockSpec((tm, k), lambda i: (i, 0))]
            + [pl.BlockSpec((tm, p.shape[1]), lambda i: (i, 0)) for p in parts],
            out_specs=pl.BlockSpec((k, tn), lambda i: (0, 0)),
            out_shape=jax.ShapeDtypeStruct((k, tn), out_dtype),
            scratch_shapes=[pltpu.VMEM((k, tn), F32)],
            compiler_params=_cparams(dimension_semantics=("arbitrary",)),
        )(a, *parts))
    return outs


def _seg_of(i, nct):
    return jnp.where(i < nct, 1, 0)


def rowwise_fwd(fn, name, rows, segs, globs, out_widths, tile, nct):
    t = rows[0].shape[0]
    nr, ns, ng = len(rows), len(segs), len(globs)

    def body(*refs):
        vals = [r[...] for r in refs[:nr]] + [r[0] for r in refs[nr:nr + ns]] + [r[...] for r in refs[nr + ns:nr + ns + ng]]
        outs = fn(*vals)
        for o_ref, o in zip(refs[nr + ns + ng:], outs):
            o_ref[...] = o

    return pl.pallas_call(
        body, name=name, grid=(t // tile,),
        in_specs=[pl.BlockSpec((tile, r.shape[1]), lambda i: (i, 0)) for r in rows]
        + [pl.BlockSpec((1, 1, s.shape[2]), lambda i: (_seg_of(i, nct), 0, 0)) for s in segs]
        + [pl.BlockSpec(g.shape, lambda i: (0, 0)) for g in globs],
        out_specs=[pl.BlockSpec((tile, w), lambda i: (i, 0)) for w in out_widths],
        out_shape=[jax.ShapeDtypeStruct((t, w), F32) for w in out_widths],
        compiler_params=_cparams(dimension_semantics=("arbitrary",)),
    )(*rows, *segs, *globs)


def rowwise_bwd(fn, name, rows, segs, globs, cts, tile, nct, row_diff, glob_diff):
    t = rows[0].shape[0]
    nr, ns, ng, nc = len(rows), len(segs), len(globs), len(cts)
    d_rows = [i for i in range(nr) if row_diff[i]]
    d_globs = [i for i in range(ng) if glob_diff[i]]

    def body(*refs):
        in_refs, out_refs = refs[:nr + ns + ng + nc], refs[nr + ns + ng + nc:]
        row_v = [r[...] for r in in_refs[:nr]]
        seg_v = [r[0] for r in in_refs[nr:nr + ns]]
        glob_v = [r[...] for r in in_refs[nr + ns:nr + ns + ng]]
        ct_v = tuple(r[...] for r in in_refs[nr + ns + ng:])

        def wrapped(dr, sv, dg):
            rv = list(row_v)
            for j, i in enumerate(d_rows):
                rv[i] = dr[j]
            gv = list(glob_v)
            for j, i in enumerate(d_globs):
                gv[i] = dg[j]
            return tuple(fn(*rv, *sv, *gv))

        _, vjp = jax.vjp(wrapped, [row_v[i] for i in d_rows], seg_v, [glob_v[i] for i in d_globs])
        c_rows, c_segs, c_globs = vjp(ct_v)
        i = pl.program_id(0)
        k = 0
        for c in c_rows:
            out_refs[k][...] = c
            k += 1
        seg_first = jnp.logical_or(i == 0, i == nct)
        for c in c_segs:
            ref = out_refs[k]
            k += 1

            @pl.when(seg_first)
            def _(ref=ref, c=c):
                ref[0] = c

            @pl.when(jnp.logical_not(seg_first))
            def _(ref=ref, c=c):
                ref[0] += c
        for c in c_globs:
            ref = out_refs[k]
            k += 1

            @pl.when(i == 0)
            def _(ref=ref, c=c):
                ref[...] = c

            @pl.when(i != 0)
            def _(ref=ref, c=c):
                ref[...] += c

    return pl.pallas_call(
        body, name=name, grid=(t // tile,),
        in_specs=[pl.BlockSpec((tile, r.shape[1]), lambda i: (i, 0)) for r in rows]
        + [pl.BlockSpec((1, 1, s.shape[2]), lambda i: (_seg_of(i, nct), 0, 0)) for s in segs]
        + [pl.BlockSpec(g.shape, lambda i: (0, 0)) for g in globs]
        + [pl.BlockSpec((tile, c.shape[1]), lambda i: (i, 0)) for c in cts],
        out_specs=[pl.BlockSpec((tile, rows[i].shape[1]), lambda i: (i, 0)) for i in d_rows]
        + [pl.BlockSpec((1, 1, s.shape[2]), lambda i: (_seg_of(i, nct), 0, 0)) for s in segs]
        + [pl.BlockSpec(globs[i].shape, lambda i: (0, 0)) for i in d_globs],
        out_shape=[jax.ShapeDtypeStruct(rows[i].shape, F32) for i in d_rows]
        + [jax.ShapeDtypeStruct(s.shape, F32) for s in segs]
        + [jax.ShapeDtypeStruct(globs[i].shape, F32) for i in d_globs],
        compiler_params=_cparams(dimension_semantics=("arbitrary",)),
    )(*rows, *segs, *globs, *cts)


def f_pre(x, mod, g_pre):
    shift, scale = mod[:, :D], mod[:, D:2 * D]
    rs = lax.rsqrt(jnp.mean(x * x, axis=-1, keepdims=True) + EPS)
    return ((x * rs) * g_pre * (1.0 + scale) + shift,)


def f_post(x, out, mod, g_post):
    gate = mod[:, 2 * D:]
    rs = lax.rsqrt(jnp.mean(out * out, axis=-1, keepdims=True) + EPS)
    return (x + gate * ((out * rs) * g_post),)


def f_mix(o_gla, o_na, y5, u5, pm, gcols, g_norm, s5_d, w_glu, b_glu, wpool, pool_scale, havg, e4):
    ms = mdot(o_gla * o_gla, havg)
    y_gla = o_gla * lax.rsqrt(ms + EPS) * jnp.sum(hdot(g_norm, e4), axis=0, keepdims=True)
    g = _gelu(u5 * s5_d + y5)
    y_s5 = g * jax.nn.sigmoid(bdot(g, w_glu) + b_glu)
    y_pool = bdot(pm, wpool) * pool_scale
    ycat = jnp.concatenate([y_gla, o_na, y_s5, y_pool], axis=-1)
    return (ycat * _silu(gcols),)


@jax.custom_vjp
def _rot_half16(x):
    lane = lax.broadcasted_iota(jnp.int32, x.shape, 1)
    first = jnp.bitwise_and(lane, 15) < 8
    return jnp.where(first, -pltpu.roll(x, x.shape[1] - 8, 1), pltpu.roll(x, 8, 1))


def _rot_fwd(x):
    return _rot_half16(x), None


def _rot_bwd(_, ct):
    return (-_rot_half16(ct),)


_rot_half16.defvjp(_rot_fwd, _rot_bwd)


def f_gla_prep(pk, pg, pq, cos, sin, wg, bg):
    z = bdot(pg, wg) + bg
    lg = _log_sigmoid(z) * (1.0 / 16.0)
    k_r = pk * cos + _rot_half16(pk) * sin
    q_r = (pq * cos + _rot_half16(pq) * sin) * (32.0 ** -0.5)
    return q_r, k_r, lg[:, :128], lg[:, 128:]


def _gla_consts(rev):
    c = GLA_CHUNK
    i = np.arange(c)
    inc = (i[None, :] >= i[:, None]) if rev else (i[None, :] <= i[:, None])
    mq = np.stack([(np.arange(128) // 32 == h) for h in range(4)]).astype(np.float32).reshape(4, 1, 128)
    mv = np.stack([(np.arange(256) // 64 == h) for h in range(4)]).astype(np.float32).reshape(4, 1, 256)
    bdt = (np.arange(256)[:, None] // 64 == np.arange(128)[None, :] // 32).astype(np.float32)
    inc = inc.astype(np.float32)
    return jnp.asarray(inc), jnp.asarray(inc.T.copy()), jnp.asarray(mq), jnp.asarray(mv), jnp.asarray(bdt)


def _stack_heads(x, m_ref):
    return jnp.concatenate([x * m_ref[h] for h in range(4)], axis=0)


def _tile4(m):
    return jnp.concatenate([m, m, m, m], axis=0)


def _fold_heads(r4, m_ref):
    r = r4.shape[0] // 4
    out = m_ref[0] * r4[0:r]
    for h in range(1, 4):
        out = out + m_ref[h] * r4[h * r:(h + 1) * r]
    return out


def _gla_chunk_of(s, n_ctx_chunks, n_chunks, rev):
    if not rev:
        return s
    return jnp.where(s < n_ctx_chunks, n_ctx_chunks - 1 - s, n_ctx_chunks + n_chunks - 1 - s)


def gla_scan_fwd(q, k, v, lg, acc, n_ctx_rows, rev, name, comm=None):
    t = q.shape[0]
    nch, ncc = t // GLA_CHUNK, n_ctx_rows // GLA_CHUNK
    inc, inc_t, mq, mv, bdt = _gla_consts(rev)

    def body(q_ref, k_ref, v_ref, lg_ref, acc_ref, inc_ref, inct_ref, mq_ref, mv_ref, bdt_ref, o_ref, st_ref):
        lmask, lmask_t = inc_ref[...], inct_ref[...]
        bd = bdt_ref[...]

        def step(s, st):
            c = _gla_chunk_of(s, ncc, nch, rev)
            rows = pl.ds(pl.multiple_of(c * GLA_CHUNK, GLA_CHUNK), GLA_CHUNK)
            qc, kc, vc, lgc = q_ref[rows, :], k_ref[rows, :], v_ref[rows, :], lg_ref[rows, :]
            st_ref[c] = st
            b = mdot(lmask, lgc)
            blast = jnp.sum(lgc, axis=0, keepdims=True)
            qe, ke, kd = qc * jnp.exp(b), kc * jnp.exp(-b), kc * jnp.exp(blast - b)
            ke4, v4 = _stack_heads(ke, mq_ref), _stack_heads(vc, mv_ref)
            at = _tile4(lmask_t) * b_nt(ke4, qe)
            o_ref[rows, :] = acc_ref[rows, :] + b_nt(qe, st) + b_tn(at, v4)
            return st * jnp.exp(blast) + bd * mdot_tn(vc, kd)

        lax.fori_loop(0, nch, step, jnp.zeros((256, 128), F32))

    return _call_with_exchange(body, name, [q, k, v, lg, acc, inc, inc_t, mq, mv, bdt],
                               [jax.ShapeDtypeStruct((t, 256), F32), jax.ShapeDtypeStruct((nch, 256, 128), F32)], comm)


def gla_scan_bwd(q, k, v, lg, st, do, acc, n_ctx_rows, rev, name, comm=None):
    t = q.shape[0]
    nch, ncc = t // GLA_CHUNK, n_ctx_rows // GLA_CHUNK
    inc, inc_t, mq, mv, bdt = _gla_consts(rev)

    def body(q_ref, k_ref, v_ref, lg_ref, st_ref, do_ref, aq_ref, ak_ref, av_ref, inc_ref, inct_ref, mq_ref, mv_ref, bdt_ref,
             dq_ref, dk_ref, dv_ref, dlg_ref):
        lmask, lmask_t = inc_ref[...], inct_ref[...]
        bd = bdt_ref[...]

        def step(j, carry):
            dst, gsum = carry
            s = nch - 1 - j
            c = _gla_chunk_of(s, ncc, nch, rev)
            rows = pl.ds(pl.multiple_of(c * GLA_CHUNK, GLA_CHUNK), GLA_CHUNK)
            qc, kc, vc, lgc, doc = q_ref[rows, :], k_ref[rows, :], v_ref[rows, :], lg_ref[rows, :], do_ref[rows, :]
            stc = st_ref[c]
            b = mdot(lmask, lgc)
            blast = jnp.sum(lgc, axis=0, keepdims=True)
            eb, enb, edb = jnp.exp(b), jnp.exp(-b), jnp.exp(blast - b)
            qe, ke, kd = qc * eb, kc * enb, kc * edb
            ke4, v4 = _stack_heads(ke, mq_ref), _stack_heads(vc, mv_ref)
            lm4 = _tile4(lmask_t)
            at = lm4 * b_nt(ke4, qe)
            dat = lm4 * mdot_nt(v4, doc)
            dqe = mdot(doc, stc) + mdot_tn(dat, ke4)
            dke = _fold_heads(mdot(dat, qe), mq_ref)
            dv = b_nt(kd, dst) + _fold_heads(b_nn(at, doc), mv_ref)
            dkd = mdot(vc, dst)
            dq = dqe * eb
            dk = dke * enb + dkd * edb
            g = qc * dq - kc * dk
            dlg_ref[rows, :] = mdot_tn(lmask, g) + gsum
            dq_ref[rows, :] = aq_ref[rows, :] + dq
            dk_ref[rows, :] = ak_ref[rows, :] + dk
            dv_ref[rows, :] = av_ref[rows, :] + dv
            dst_new = dst * jnp.exp(blast) + bd * mdot_tn(doc, qe)
            return dst_new, gsum + jnp.sum(g, axis=0, keepdims=True)

        lax.fori_loop(0, nch, step, (jnp.zeros((256, 128), F32), jnp.zeros((1, 128), F32)))

    return _call_with_exchange(body, name, [q, k, v, lg, st, do, *acc, inc, inc_t, mq, mv, bdt],
                               [jax.ShapeDtypeStruct((t, 128), F32), jax.ShapeDtypeStruct((t, 128), F32),
                                jax.ShapeDtypeStruct((t, 256), F32), jax.ShapeDtypeStruct((t, 128), F32)], comm)


def whole_fwd(fn, name, args, out_shapes):
    def body(*refs):
        outs = fn(*[r[...] for r in refs[:len(args)]])
        for o_ref, o in zip(refs[len(args):], outs):
            o_ref[...] = o

    vm = pl.BlockSpec(memory_space=pltpu.VMEM)
    return pl.pallas_call(
        body, name=name, in_specs=[vm] * len(args), out_specs=[vm] * len(out_shapes),
        out_shape=[jax.ShapeDtypeStruct(s, F32) for s in out_shapes], compiler_params=_cparams(),
    )(*args)


def whole_bwd(fn, name, args, cts, diff):
    d_idx = [i for i in range(len(args)) if diff[i]]

    def body(*refs):
        vals = [r[...] for r in refs[:len(args)]]
        ct_v = tuple(r[...] for r in refs[len(args):len(args) + len(cts)])

        def wrapped(dv):
            av = list(vals)
            for j, i in enumerate(d_idx):
                av[i] = dv[j]
            return tuple(fn(*av))

        _, vjp = jax.vjp(wrapped, [vals[i] for i in d_idx])
        (c_args,) = vjp(ct_v)
        for o_ref, c in zip(refs[len(args) + len(cts):], c_args):
            o_ref[...] = c

    vm = pl.BlockSpec(memory_space=pltpu.VMEM)
    return pl.pallas_call(
        body, name=name, in_specs=[vm] * (len(args) + len(cts)), out_specs=[vm] * len(d_idx),
        out_shape=[jax.ShapeDtypeStruct(args[i].shape, F32) for i in d_idx], compiler_params=_cparams(),
    )(*args, *cts)


def _s5_consts():
    e_rep = (np.arange(256)[:, None] // 16 == np.arange(16)[None, :]).astype(np.float32)
    e_tile = (np.arange(64)[:, None] == np.arange(1024)[None, :] % 64).astype(np.float32)
    gmask = (np.arange(16)[:, None] == np.arange(1024)[None, :] // 64).astype(np.float32)
    bdm = (np.arange(256)[:, None] // 16 == np.arange(1024)[None, :] // 64).astype(np.float32)
    return jnp.asarray(e_rep), jnp.asarray(e_tile), jnp.asarray(gmask), jnp.asarray(bdm)


def f_s5_params(lam_re, lam_im, log_dt, bt_re, bt_im, ct_re, ct_im, e_rep, e_tile, gmask, bdm):
    dt = jnp.exp(log_dt)
    mag = jnp.exp(lam_re * dt)
    ang = lam_im * dt
    lb_re, lb_im = mag * jnp.cos(ang), mag * jnp.sin(ang)
    num_re, num_im = lb_re - 1.0, lb_im
    den = lam_re * lam_re + lam_im * lam_im
    coef_re = (num_re * lam_re + num_im * lam_im) / den
    coef_im = (num_im * lam_re - num_re * lam_im) / den
    cr, ci = hdot(e_rep, coef_re), hdot(e_rep, coef_im)
    bbt_re = cr * bt_re - ci * bt_im
    bbt_im = cr * bt_im + ci * bt_re
    a_re = jnp.sum(hdot(lb_re, e_tile) * gmask, axis=0, keepdims=True)
    a_im = jnp.sum(hdot(lb_im, e_tile) * gmask, axis=0, keepdims=True)
    return (a_re, a_im, hdot(bbt_re, e_tile) * bdm, hdot(bbt_im, e_tile) * bdm,
            hdot(ct_re, e_tile) * bdm, hdot(ct_im, e_tile) * bdm)


def _s5_doubling(xr, xi, pr, pi, pos, n, steps, rev):
    rows = xr.shape[0]
    for s in steps:
        if rev:
            keep = pos < (n - s)
            sr, si = pltpu.roll(xr, rows - s, 0), pltpu.roll(xi, rows - s, 0)
        else:
            keep = pos >= s
            sr, si = pltpu.roll(xr, s, 0), pltpu.roll(xi, s, 0)
        sr, si = jnp.where(keep, sr, 0.0), jnp.where(keep, si, 0.0)
        xr, xi = xr + pr * sr - pi * si, xi + pr * si + pi * sr
        pr, pi = pr * pr - pi * pi, 2.0 * pr * pi
    return xr, xi, pr, pi


SUBLANES = 8


def _s5_scan(xr, xi, a_re, a_im, rev, chunk, scr):
    xs_r, xs_i, yp_r, yp_i = scr
    ng = chunk // SUBLANES
    x3r, x3i = xr.reshape(ng, SUBLANES, 1024), xi.reshape(ng, SUBLANES, 1024)
    sub = lax.broadcasted_iota(jnp.int32, (SUBLANES, 1024), 0)
    a8r, a8i = a_re, a_im
    for s in (1, 2, 4):
        keep = sub < (SUBLANES - s) if rev else sub >= s
        mr, mi = jnp.where(keep, a8r, 0.0)[None], jnp.where(keep, a8i, 0.0)[None]
        shift = SUBLANES - s if rev else s
        sr, si = pltpu.roll(x3r, shift, 1), pltpu.roll(x3i, shift, 1)
        x3r, x3i = x3r + mr * sr - mi * si, x3i + mr * si + mi * sr
        a8r, a8i = a8r * a8r - a8i * a8i, 2.0 * a8r * a8i
    xr, xi = x3r.reshape(chunk, 1024), x3i.reshape(chunk, 1024)
    nblk = 1024 // 128
    for j in range(nblk):
        xs_r[j] = xr[:, 128 * j:128 * (j + 1)]
        xs_i[j] = xi[:, 128 * j:128 * (j + 1)]
    edge = pl.ds(0 if rev else SUBLANES - 1, ng, stride=SUBLANES)
    gr = jnp.concatenate([xs_r[j, edge, :] for j in range(nblk)], axis=-1)
    gi = jnp.concatenate([xs_i[j, edge, :] for j in range(nblk)], axis=-1)
    grow = lax.broadcasted_iota(jnp.int32, (ng, 1024), 0)
    steps = tuple(1 << k for k in range((ng - 1).bit_length()))
    gr, gi, _, _ = _s5_doubling(gr, gi, a8r, a8i, grow, ng, steps, rev)
    if rev:
        yp_r[...] = jnp.where(grow < ng - 1, pltpu.roll(gr, ng - 1, 0), 0.0)
        yp_i[...] = jnp.where(grow < ng - 1, pltpu.roll(gi, ng - 1, 0), 0.0)
    else:
        yp_r[...] = jnp.where(grow >= 1, pltpu.roll(gr, 1, 0), 0.0)
        yp_i[...] = jnp.where(grow >= 1, pltpu.roll(gi, 1, 0), 0.0)
    sub = lax.broadcasted_iota(jnp.int32, (SUBLANES, 1024), 0)
    tr, ti = jnp.zeros((SUBLANES, 1024), F32), jnp.zeros((SUBLANES, 1024), F32)
    cr, ci = a_re, a_im
    for n in range(1, SUBLANES + 1):
        r = SUBLANES - n if rev else n - 1
        tr, ti = jnp.where(sub == r, cr, tr), jnp.where(sub == r, ci, ti)
        cr, ci = cr * a_re - ci * a_im, cr * a_im + ci * a_re
    for j in range(nblk):
        lanes = slice(128 * j, 128 * (j + 1))
        tr_j, ti_j = tr[:, lanes], ti[:, lanes]
        for g in range(ng):
            rows = slice(g * SUBLANES, (g + 1) * SUBLANES)
            er, ei = yp_r[g:g + 1, lanes], yp_i[g:g + 1, lanes]
            xs_r[j, rows, :] = xs_r[j, rows, :] + tr_j * er - ti_j * ei
            xs_i[j, rows, :] = xs_i[j, rows, :] + tr_j * ei + ti_j * er
    return (jnp.concatenate([xs_r[j] for j in range(nblk)], axis=-1),
            jnp.concatenate([xs_i[j] for j in range(nblk)], axis=-1))


def _s5_scratch(chunk):
    return [pltpu.VMEM((8, chunk, 128), F32), pltpu.VMEM((8, chunk, 128), F32),
            pltpu.VMEM((chunk // SUBLANES, 1024), F32), pltpu.VMEM((chunk // SUBLANES, 1024), F32)]


def _s5_chunk_states(u_c, x0r, x0i, a_re, a_im, bb_re, bb_im, rev, chunk, scr):
    row = lax.broadcasted_iota(jnp.int32, (chunk, 1024), 0)
    first = row == (chunk - 1 if rev else 0)
    inj_r = a_re * x0r - a_im * x0i
    inj_i = a_re * x0i + a_im * x0r
    xr = b_nn(u_c, bb_re) + jnp.where(first, inj_r, 0.0)
    xi = b_nn(u_c, bb_im) + jnp.where(first, inj_i, 0.0)
    return _s5_scan(xr, xi, a_re, a_im, rev, chunk, scr)


def _row_pick(x, idx):
    row = lax.broadcasted_iota(jnp.int32, x.shape, 0)
    return jnp.sum(jnp.where(row == idx, x, 0.0), axis=0, keepdims=True)


def s5_scan_fwd(u, acc, a_re, a_im, bb_re, bb_im, cc_re, cc_im, n_ctx_rows, chunk, rev, name, comm=None):
    t = u.shape[0]
    nch, ncc = t // chunk, n_ctx_rows // chunk

    def body(u_ref, acc_ref, ar_ref, ai_ref, br_ref, bi_ref, cr_ref, ci_ref, y_ref, x0r_ref, x0i_ref, xsr_ref, xsi_ref, *scr):
        a_r, a_i = ar_ref[...], ai_ref[...]

        def step(s, carry):
            x0r, x0i = carry
            c = _gla_chunk_of(s, ncc, nch, rev)
            rows = pl.ds(pl.multiple_of(c * chunk, chunk), chunk)
            x0r_ref[c] = x0r
            x0i_ref[c] = x0i
            xr, xi = _s5_chunk_states(u_ref[rows, :], x0r, x0i, a_r, a_i, br_ref[...], bi_ref[...], rev, chunk, scr)
            y_ref[rows, :] = acc_ref[rows, :] + b_nt(xr, cr_ref[...]) - b_nt(xi, ci_ref[...])
            xsr_ref[rows, :] = xr.astype(BF16)
            xsi_ref[rows, :] = xi.astype(BF16)
            last = 0 if rev else chunk - 1
            return _row_pick(xr, last), _row_pick(xi, last)

        lax.fori_loop(0, nch, step, (jnp.zeros((1, 1024), F32), jnp.zeros((1, 1024), F32)))

    return _call_with_exchange(
        body, name, [u, acc, a_re, a_im, bb_re, bb_im, cc_re, cc_im],
        [jax.ShapeDtypeStruct((t, 256), F32), jax.ShapeDtypeStruct((nch, 1, 1024), F32),
         jax.ShapeDtypeStruct((nch, 1, 1024), F32), jax.ShapeDtypeStruct((t, 1024), BF16),
         jax.ShapeDtypeStruct((t, 1024), BF16)], comm, _s5_scratch(chunk))


def s5_scan_bwd(u, dy, du_acc, x0r, x0i, xsr, xsi, a_re, a_im, bb_re, bb_im, cc_re, cc_im, n_ctx_rows, chunk, rev, name):
    t = u.shape[0]
    nch, ncc = t // chunk, n_ctx_rows // chunk

    def body(u_ref, dy_ref, dua_ref, x0r_ref, x0i_ref, xsr_ref, xsi_ref, ar_ref, ai_ref, br_ref, bi_ref, cr_ref, ci_ref,
             du_ref, dar_ref, dai_ref, dbr_ref, dbi_ref, dcr_ref, dci_ref, *scr):
        a_r, a_i = ar_ref[...], ai_ref[...]
        for ref in (dbr_ref, dbi_ref, dcr_ref, dci_ref):
            ref[...] = jnp.zeros_like(ref)
        row = lax.broadcasted_iota(jnp.int32, (chunk, 1024), 0)
        first_idx, last_idx = (chunk - 1, 0) if rev else (0, chunk - 1)

        def step(j, carry):
            lcr, lci, dar, dai = carry
            s = nch - 1 - j
            c = _gla_chunk_of(s, ncc, nch, rev)
            rows = pl.ds(pl.multiple_of(c * chunk, chunk), chunk)
            u_c, dy_c = u_ref[rows, :], dy_ref[rows, :]
            x0r_c, x0i_c = x0r_ref[c], x0i_ref[c]
            xr, xi = xsr_ref[rows, :].astype(F32), xsi_ref[rows, :].astype(F32)
            dcr_ref[...] += b_tn(dy_c, xr)
            dci_ref[...] -= b_tn(dy_c, xi)
            inj_r = a_r * lcr + a_i * lci
            inj_i = a_r * lci - a_i * lcr
            is_last = row == last_idx
            lr = b_nn(dy_c, cr_ref[...]) + jnp.where(is_last, inj_r, 0.0)
            li = -b_nn(dy_c, ci_ref[...]) + jnp.where(is_last, inj_i, 0.0)
            lr, li = _s5_scan(lr, li, a_r, -a_i, not rev, chunk, scr)
            du_ref[rows, :] = dua_ref[rows, :] + b_nt(lr, br_ref[...]) + b_nt(li, bi_ref[...])
            dbr_ref[...] += b_tn(u_c, lr)
            dbi_ref[...] += b_tn(u_c, li)
            if rev:
                pr, pi = pltpu.roll(xr, chunk - 1, 0), pltpu.roll(xi, chunk - 1, 0)
            else:
                pr, pi = pltpu.roll(xr, 1, 0), pltpu.roll(xi, 1, 0)
            is_first = row == first_idx
            pr, pi = jnp.where(is_first, x0r_c, pr), jnp.where(is_first, x0i_c, pi)
            dar = dar + jnp.sum(lr * pr + li * pi, axis=0, keepdims=True)
            dai = dai + jnp.sum(li * pr - lr * pi, axis=0, keepdims=True)
            return _row_pick(lr, first_idx), _row_pick(li, first_idx), dar, dai

        z = jnp.zeros((1, 1024), F32)
        _, _, dar, dai = lax.fori_loop(0, nch, step, (z, z, z, z))
        dar_ref[...] = dar
        dai_ref[...] = dai

    vm = pl.BlockSpec(memory_space=pltpu.VMEM)
    big = jax.ShapeDtypeStruct((256, 1024), F32)
    vec = jax.ShapeDtypeStruct((1, 1024), F32)
    return pl.pallas_call(
        body, name=name, in_specs=[vm] * 13, out_specs=[vm] * 7,
        out_shape=[jax.ShapeDtypeStruct((t, 256), F32), vec, vec, big, big, big, big],
        scratch_shapes=_s5_scratch(chunk), compiler_params=_cparams(),
    )(u, dy, du_acc, x0r, x0i, xsr, xsi, a_re, a_im, bb_re, bb_im, cc_re, cc_im)


POOL_HALO = 8


def pool_apply(u_pad, n, transpose, name, tile=ROW_TILE):
    tile = min(tile, n)
    ext = tile + 2 * POOL_HALO
    trel = np.arange(ext)[None, :] - POOL_HALO - np.arange(tile)[:, None]
    if transpose:
        trel = -trel
    band4 = np.concatenate([((trel >= -(1 << w)) & (trel <= (1 << w) - 1)) for w in range(4)], axis=0).astype(np.float32)

    def body(u_ref, band_ref, lm_ref, o_ref):
        lax.fori_loop(0, n // tile, functools.partial(step, u_ref, band_ref, lm_ref, o_ref), 0)

    def step(u_ref, band_ref, lm_ref, o_ref, i, carry):
        val = u_ref[pl.ds(pl.multiple_of(i * tile, tile), ext), :]
        lane = lax.broadcasted_iota(jnp.int32, (ext, 256), 1)
        half = jnp.left_shift(1, jnp.right_shift(lane, 6))
        trow = lax.broadcasted_iota(jnp.int32, (ext, 256), 0) + (i * tile - POOL_HALO)
        cnt = jnp.minimum(trow + half, n) - jnp.maximum(trow - half, 0)
        inv = 1.0 / jnp.maximum(cnt, 1).astype(F32)
        src = val * inv if transpose else val
        acc = _fold_heads(mdot(band_ref[...], src), lm_ref)
        centre = val[POOL_HALO:POOL_HALO + tile]
        if not transpose:
            acc = acc * inv[POOL_HALO:POOL_HALO + tile]
        o_ref[pl.ds(pl.multiple_of(i * tile, tile), tile), :] = acc - centre
        return carry

    vm = pl.BlockSpec(memory_space=pltpu.VMEM)
    return pl.pallas_call(
        body, name=name, in_specs=[vm] * 3, out_specs=vm,
        out_shape=jax.ShapeDtypeStruct((n, 256), F32), compiler_params=_cparams(),
    )(u_pad, jnp.asarray(band4), _na_head_masks())


NA_SCALE = 64.0 ** -0.5
NEG = -1e30


def _call_with_exchange(compute, name, args, out_shapes, comm, scratch=()):
    vm = pl.BlockSpec(memory_space=pltpu.VMEM)
    n_in, n_out = len(args), len(out_shapes)
    if comm is None:
        outs = pl.pallas_call(compute, name=name, in_specs=[vm] * n_in, out_specs=[vm] * n_out, out_shape=out_shapes,
                              scratch_shapes=list(scratch), compiler_params=_cparams())(*args)
        return outs, None
    arrays, scatter = comm
    n = len(arrays)

    def body(*refs):
        c_in = refs[n_in:n_in + n]
        c_out = refs[n_in + n + n_out:n_in + 2 * n + n_out]
        scr = refs[n_in + 2 * n + n_out:n_in + 2 * n + n_out + len(scratch)]
        finish = _exchange_issue(c_in, c_out, scatter, *refs[n_in + 2 * n + n_out + len(scratch):])
        compute(*refs[:n_in], *refs[n_in + n:n_in + n + n_out], *scr)
        finish()

    hbm = pl.BlockSpec(memory_space=pl.ANY)
    outs = pl.pallas_call(
        body, name=name, in_specs=[vm] * n_in + [hbm] * n, out_specs=[vm] * n_out + [hbm] * n,
        out_shape=list(out_shapes) + _exchange_out_shapes(arrays, scatter), scratch_shapes=list(scratch) + _exchange_sems(n),
        compiler_params=_cparams(has_side_effects=True),
    )(*args, *arrays)
    return outs[:n_out], outs[n_out:]


def _na_head_masks():
    return jnp.asarray(np.stack([(np.arange(256) // 64 == h) for h in range(4)]).astype(np.float32).reshape(4, 1, 256))


def _na_window(r, rows):
    start = jnp.clip(r - 4, 0, rows - 8)
    return start, start - r + 7


def _na_probs(qh, kw, kc, bias):
    s_c = b_nt(qh, kc)
    m = jnp.max(s_c, axis=-1, keepdims=True)
    if kw is not None:
        s_w = b_nt(qh, kw) + bias
        m = jnp.maximum(m, jnp.max(s_w, axis=-1, keepdims=True))
        p_w = jnp.exp(s_w - m)
    p_c = jnp.exp(s_c - m)
    l = jnp.sum(p_c, axis=-1, keepdims=True)
    if kw is not None:
        l = l + jnp.sum(p_w, axis=-1, keepdims=True)
        return p_w / l, p_c / l
    return None, p_c / l


def na_fwd(q, k, v, bias8, n_ctx_rows, name, comm=None):
    t = q.shape[0]
    m_ctx = n_ctx_rows
    rows = (t - m_ctx) // GRID_W
    hm = _na_head_masks()

    def body(q_ref, k_ref, v_ref, b_ref, hm_ref, o_ref):
        kc, vc = k_ref[0:m_ctx, :], v_ref[0:m_ctx, :]

        def ctx_step(i, _):
            rs = pl.ds(pl.multiple_of(i * 64, 64), 64)
            q4 = _stack_heads(q_ref[rs, :] * NA_SCALE, hm_ref)
            _, p_c = _na_probs(q4, None, kc, None)
            o_ref[rs, :] = _fold_heads(b_nn(p_c, vc), hm_ref)
            return 0

        lax.fori_loop(0, m_ctx // 64, ctx_step, 0)

        def lat_step(r, _):
            start, off = _na_window(r, rows)
            rs = pl.ds(pl.multiple_of(m_ctx + r * 64, 64), 64)
            ws = pl.ds(pl.multiple_of(m_ctx + start * 64, 64), 512)
            q4 = _stack_heads(q_ref[rs, :] * NA_SCALE, hm_ref)
            kw, vw = k_ref[ws, :], v_ref[ws, :]
            p_w, p_c = _na_probs(q4, kw, kc, b_ref[off])
            o_ref[rs, :] = _fold_heads(b_nn(p_w, vw) + b_nn(p_c, vc), hm_ref)
            return 0

        lax.fori_loop(0, rows, lat_step, 0)

    (o,), received = _call_with_exchange(body, name, [q, k, v, bias8, hm], [jax.ShapeDtypeStruct((t, 256), F32)], comm)
    return o if comm is None else (o, received)


def na_bwd(q, k, v, do, bias8, n_ctx_rows, name, comm=None):
    t = q.shape[0]
    m_ctx = n_ctx_rows
    rows = (t - m_ctx) // GRID_W
    hm = _na_head_masks()

    def body(q_ref, k_ref, v_ref, do_ref, b_ref, hm_ref, dq_ref, dk_ref, dv_ref, db_ref):
        kc, vc = k_ref[0:m_ctx, :], v_ref[0:m_ctx, :]
        dk_ref[...] = jnp.zeros_like(dk_ref)
        dv_ref[...] = jnp.zeros_like(dv_ref)
        db_ref[...] = jnp.zeros_like(db_ref)

        def head_terms(qh, doh, kw, vw, bias):
            p_w, p_c = _na_probs(qh, kw, kc, bias)
            dp_c = b_nt(doh, vc)
            delta = jnp.sum(p_c * dp_c, axis=-1, keepdims=True)
            if kw is not None:
                dp_w = b_nt(doh, vw)
                delta = delta + jnp.sum(p_w * dp_w, axis=-1, keepdims=True)
                ds_w = p_w * (dp_w - delta)
            else:
                ds_w = None
            ds_c = p_c * (dp_c - delta)
            return p_w, p_c, ds_w, ds_c

        def ctx_step(i, carry):
            dkc, dvc = carry
            rs = pl.ds(pl.multiple_of(i * 64, 64), 64)
            q4, do4 = _stack_heads(q_ref[rs, :] * NA_SCALE, hm_ref), _stack_heads(do_ref[rs, :], hm_ref)
            _, p_c, _, ds_c = head_terms(q4, do4, None, None, None)
            dq_ref[rs, :] = _fold_heads(b_nn(ds_c, kc), hm_ref) * NA_SCALE
            return dkc + b_tn(ds_c, q4), dvc + b_tn(p_c, do4)

        zc = jnp.zeros((m_ctx, 256), F32)
        carry = lax.fori_loop(0, m_ctx // 64, ctx_step, (zc, zc))

        def lat_step(r, carry):
            dkc, dvc = carry
            start, off = _na_window(r, rows)
            rs = pl.ds(pl.multiple_of(m_ctx + r * 64, 64), 64)
            ws = pl.ds(pl.multiple_of(m_ctx + start * 64, 64), 512)
            q4, do4 = _stack_heads(q_ref[rs, :] * NA_SCALE, hm_ref), _stack_heads(do_ref[rs, :], hm_ref)
            kw, vw = k_ref[ws, :], v_ref[ws, :]
            p_w, p_c, ds_w, ds_c = head_terms(q4, do4, kw, vw, b_ref[off])
            dq_ref[rs, :] = _fold_heads(b_nn(ds_w, kw) + b_nn(ds_c, kc), hm_ref) * NA_SCALE
            dk_ref[ws, :] += b_tn(ds_w, q4)
            dv_ref[ws, :] += b_tn(p_w, do4)
            db_ref[off] += ds_w
            return dkc + b_tn(ds_c, q4), dvc + b_tn(p_c, do4)

        dkc, dvc = lax.fori_loop(0, rows, lat_step, carry)
        dk_ref[0:m_ctx, :] = dkc
        dv_ref[0:m_ctx, :] = dvc

    row = jax.ShapeDtypeStruct((t, 256), F32)
    return _call_with_exchange(body, name, [q, k, v, do, bias8, hm], [row, row, row, jax.ShapeDtypeStruct(bias8.shape, F32)], comm)


def _na_toeplitz():
    col = np.arange(GRID_W)
    dd = (col[None, :] - col[:, None] + 15).reshape(-1)
    tt = np.zeros((GRID_W * GRID_W, 128), np.float32)
    ok = (dd >= 0) & (dd <= 30)
    tt[np.arange(GRID_W * GRID_W)[ok], dd[ok]] = 1.0
    return tt


def _na_bias8(rpb, name):
    col = np.arange(GRID_W)
    cs = np.clip(col - 8, 0, GRID_W - 16)
    col_mask = (col[None, :] >= cs[:, None]) & (col[None, :] < cs[:, None] + 16)
    rpb2 = jnp.pad(rpb.reshape(60, 31), ((0, 4), (0, 97)))
    (toe,) = whole_fwd(lambda r_, t_: (hdot_nt(r_, t_),), name, [rpb2, jnp.asarray(_na_toeplitz())], [(64, GRID_W * GRID_W)])
    toe = toe[:60].reshape(4, 15, GRID_W, GRID_W)
    b = jnp.stack([toe[:, off:off + 8] for off in range(8)], axis=1)
    b = jnp.where(jnp.asarray(col_mask)[None, None, None], b, NEG)
    return b.transpose(1, 0, 3, 2, 4).reshape(8, 4 * GRID_W, 8 * GRID_W)


def _na_rpb_grad(dbias8, name):
    tt = _na_toeplitz()
    sel = np.zeros((64, 256), np.float32)
    for h in range(4):
        for off in range(8):
            for i in range(8):
                sel[h * 15 + off + i, h * 64 + off * 8 + i] = 1.0
    a2 = dbias8.reshape(8, 4, GRID_W, 8, GRID_W).transpose(1, 0, 3, 2, 4).reshape(256, GRID_W * GRID_W)
    (out,) = whole_fwd(lambda a, t_, s_: (hdot(s_, hdot(a, t_)),), name, [a2, jnp.asarray(tt), jnp.asarray(sel)], [(64, 128)])
    return out[:60, :31].reshape(4, 15, 31)


def f_mod(cs, b_mod, w_mod):
    s = _silu(cs)
    return bdot(s, w_mod) + b_mod, s


def loss_and_grad(z, tgt, n_ctx_rows, name, tile=ROW_TILE):
    t, d = z.shape
    tile = min(tile, n_ctx_rows)
    nct = n_ctx_rows // tile

    def body(z_ref, t_ref, dz_ref, loss_ref):
        i = pl.program_id(0)

        @pl.when(i == 0)
        def _():
            loss_ref[...] = jnp.zeros_like(loss_ref)

        @pl.when(i < nct)
        def _():
            dz_ref[...] = jnp.zeros_like(dz_ref)

        @pl.when(i >= nct)
        def _():
            diff = z_ref[...] - t_ref[...]
            dz_ref[...] = diff * (1.0 / d)
            loss_ref[...] += 0.5 * jnp.sum(jnp.sum(diff * diff, axis=-1, keepdims=True) * (1.0 / d), axis=0, keepdims=True)

    dz, loss = pl.pallas_call(
        body, name=name, grid=(t // tile,),
        in_specs=[pl.BlockSpec((tile, d), lambda i: (i, 0)),
                  pl.BlockSpec((tile, d), lambda i: (jnp.maximum(i - nct, 0), 0))],
        out_specs=[pl.BlockSpec((tile, d), lambda i: (i, 0)), pl.BlockSpec((8, 128), lambda i: (0, 0))],
        out_shape=[jax.ShapeDtypeStruct((t, d), F32), jax.ShapeDtypeStruct((8, 128), F32)],
        compiler_params=_cparams(dimension_semantics=("arbitrary",)),
    )(z, tgt)
    return loss[0, 0], dz


def adamw(parts, w, m, v, name, tile=256):
    npart, r, c = parts.shape
    tile = min(tile, r)
    assert r % tile == 0
    c1 = 1.0 / (1.0 - ADAM_B1 ** ADAM_STEP)
    c2 = 1.0 / (1.0 - ADAM_B2 ** ADAM_STEP)

    def body(p_ref, w_ref, m_ref, v_ref, g_ref, d_ref, nm_ref, nv_ref):
        g = p_ref[0].astype(F32)
        for i in range(1, npart):
            g = g + p_ref[i].astype(F32)
        nm = ADAM_B1 * m_ref[...] + (1.0 - ADAM_B1) * g
        nv = ADAM_B2 * v_ref[...] + (1.0 - ADAM_B2) * (g * g)
        g_ref[...] = g
        nm_ref[...] = nm
        nv_ref[...] = nv
        d_ref[...] = -ADAM_LR * ((nm * c1) / (jnp.sqrt(nv * c2) + ADAM_EPS) + ADAM_WD * w_ref[...])

    blk = pl.BlockSpec((tile, c), lambda i: (i, 0))
    return pl.pallas_call(
        body, name=name, grid=(r // tile,),
        in_specs=[pl.BlockSpec((npart, tile, c), lambda i: (0, i, 0)), blk, blk, blk],
        out_specs=[blk] * 4, out_shape=[jax.ShapeDtypeStruct((r, c), F32)] * 4,
        compiler_params=_cparams(dimension_semantics=("arbitrary",)),
    )(parts, w, m, v)


def _peer(x, y, c, k):
    return (1 - x if k & 4 else x, 1 - y if k & 2 else y, 1 - c if k & 1 else c)


def _exchange_out_shapes(arrays, scatter):
    return [jax.ShapeDtypeStruct(a.shape if s else (N_DEV,) + a.shape, a.dtype) for a, s in zip(arrays, scatter)]


def _exchange_sems(n):
    return [pltpu.SemaphoreType.DMA((n, N_DEV - 1)), pltpu.SemaphoreType.DMA((n, N_DEV - 1)), pltpu.SemaphoreType.DMA((n,))]


def _exchange_issue(ins, outs, scatter, send_sems, recv_sems, local_sems):
    n = len(ins)
    x, y, c = lax.axis_index("x"), lax.axis_index("y"), lax.axis_index("c")
    me = 4 * x + 2 * y + c

    def index_of(p):
        return 4 * p[0] + 2 * p[1] + p[2]

    local = []
    for a in range(n):
        src_me = ins[a].at[me] if scatter[a] else ins[a]
        loc = pltpu.make_async_copy(src_me, outs[a].at[me], local_sems.at[a])
        loc.start()
        local.append(loc)
    for k in range(1, N_DEV):
        peer = _peer(x, y, c, k)
        for a in range(n):
            src = ins[a].at[index_of(peer)] if scatter[a] else ins[a]
            pltpu.make_async_remote_copy(
                src_ref=src, dst_ref=outs[a].at[me], send_sem=send_sems.at[a, k - 1], recv_sem=recv_sems.at[a, k - 1],
                device_id=peer, device_id_type=pl.DeviceIdType.MESH).start()

    def finish():
        for k in range(1, N_DEV):
            peer = _peer(x, y, c, k)
            for a in range(n):
                src = ins[a].at[index_of(peer)] if scatter[a] else ins[a]
                cp = pltpu.make_async_remote_copy(
                    src_ref=src, dst_ref=outs[a].at[index_of(peer)], send_sem=send_sems.at[a, k - 1],
                    recv_sem=recv_sems.at[a, k - 1], device_id=peer, device_id_type=pl.DeviceIdType.MESH)
                cp.wait_send()
                cp.wait_recv()
        for loc in local:
            loc.wait()

    return finish


def gather_two_level(arrays, name):
    n = len(arrays)

    def body(*refs):
        ins, outs = refs[:n], refs[n:2 * n]
        send_sems, recv_sems, local_sems = refs[2 * n:]
        x, y, c = lax.axis_index("x"), lax.axis_index("y"), lax.axis_index("c")
        sibling = (x, y, 1 - c)
        chips = [(1 - x, y), (x, 1 - y), (1 - x, 1 - y)]

        def slot(a, p):
            return outs[a].at[4 * p[0] + 2 * p[1] + p[2]]

        def copy(a, k, src, block, to):
            return pltpu.make_async_remote_copy(src_ref=src, dst_ref=slot(a, block), send_sem=send_sems.at[a, k],
                                                recv_sem=recv_sems.at[a, k], device_id=to, device_id_type=pl.DeviceIdType.MESH)

        me = (x, y, c)
        started, local = [], []
        for a in range(n):
            loc = pltpu.make_async_copy(ins[a], slot(a, me), local_sems.at[a])
            loc.start()
            local.append(loc)
            first = [copy(a, 0, ins[a], me, sibling)] + [copy(a, 1 + j, ins[a], me, (*chip, c)) for j, chip in enumerate(chips)]
            for cp in first:
                cp.start()
            started += first
        for j, chip in enumerate(chips):
            for a in range(n):
                copy(a, 1 + j, ins[a], (*chip, c), me).wait_recv()
                fwd = copy(a, 4 + j, slot(a, (*chip, c)), (*chip, c), sibling)
                fwd.start()
                started.append(fwd)
        for a in range(n):
            copy(a, 0, ins[a], sibling, me).wait_recv()
            for j, chip in enumerate(chips):
                copy(a, 4 + j, ins[a], (*chip, 1 - c), me).wait_recv()
        for cp in started:
            cp.wait_send()
        for loc in local:
            loc.wait()

    hbm = pl.BlockSpec(memory_space=pl.ANY)
    return pl.pallas_call(
        body, name=name, in_specs=[hbm] * n, out_specs=[hbm] * n, out_shape=_exchange_out_shapes(arrays, [False] * n),
        scratch_shapes=_exchange_sems(n), compiler_params=pltpu.CompilerParams(has_side_effects=True),
    )(*arrays)


def exchange(arrays, scatter, name):
    n = len(arrays)

    def body(*refs):
        _exchange_issue(refs[:n], refs[n:2 * n], scatter, *refs[2 * n:])()

    hbm = pl.BlockSpec(memory_space=pl.ANY)
    return pl.pallas_call(
        body, name=name, in_specs=[hbm] * n, out_specs=[hbm] * n, out_shape=_exchange_out_shapes(arrays, scatter),
        scratch_shapes=_exchange_sems(n), compiler_params=pltpu.CompilerParams(has_side_effects=True),
    )(*arrays)


def _rope_tables(n_lat, n_ctx):
    tok = np.arange(n_lat)
    freqs = 10000.0 ** (-np.arange(0, 16, 2, dtype=np.float32) / 16.0)

    def table(pos):
        ang = pos.astype(np.float32)[:, None] * freqs[None, :]
        ang = np.concatenate([ang, ang], axis=-1)
        return np.cos(ang), np.sin(ang)

    cr, sr = table(tok // GRID_W)
    cc, sc = table(tok % GRID_W)
    cos = np.tile(np.concatenate([cr, cc], axis=-1), (1, 4))
    sin = np.tile(np.concatenate([sr, sc], axis=-1), (1, 4))
    cos = np.concatenate([np.ones((n_ctx, 128), np.float32), cos], axis=0)
    sin = np.concatenate([np.zeros((n_ctx, 128), np.float32), sin], axis=0)
    return jnp.asarray(cos, F32), jnp.asarray(sin, F32)


def _pad_w_in(w):
    z = lambda n: jnp.zeros((w.shape[0], n), w.dtype)
    return jnp.concatenate([w[:, 1824:2848], w[:, 128:384], w[:, 416:672], w[:, 672:928], w[:, 928:1184], w[:, 1312:1568],
                            w[:, 1568:1824], w[:, 0:128], w[:, 384:416], z(96), w[:, 1184:1312], z(128)], axis=1)


def _unpad_w_in(wp):
    return jnp.concatenate([wp[:, C_GK:C_GK + 128], wp[:, C_GV:C_GV + 256], wp[:, C_GG:C_GG + 32], wp[:, C_NK:C_NK + 256],
                            wp[:, C_NV:C_NV + 256], wp[:, C_SU:C_SU + 256], wp[:, C_GQ:C_GQ + 128], wp[:, C_NQ:C_NQ + 256],
                            wp[:, C_PU:C_PU + 256], wp[:, C_GT:C_GT + 1024]], axis=1)


_W_IN_SEGS = [(0, 128, C_GK), (128, 256, C_GV), (384, 32, C_GG), (416, 256, C_NK), (672, 256, C_NV), (928, 256, C_SU),
              (1184, 128, C_GQ), (1312, 256, C_NQ), (1568, 256, C_PU), (1824, 1024, C_GT)]
W_IN_SHARD = N_IN // N_DEV


def _pad_w_in_blocks(blocks):
    pieces = []
    for orig, width, padded in _W_IN_SEGS:
        col = orig
        while col < orig + width:
            dev, lo = divmod(col, W_IN_SHARD)
            n = min(W_IN_SHARD - lo, orig + width - col)
            pieces.append((padded + col - orig, blocks[dev][:, lo:lo + n]))
            col += n
    pieces.sort(key=lambda p: p[0])
    out, at = [], 0
    for start, piece in pieces:
        if start > at:
            out.append(jnp.zeros((blocks.shape[1], start - at), blocks.dtype))
        out.append(piece)
        at = start + piece.shape[1]
    out.append(jnp.zeros((blocks.shape[1], PW - at), blocks.dtype))
    return jnp.concatenate(out, axis=1)


def _w_in_slabs(wp_blocks):
    slabs = []
    for dev in range(N_DEV):
        first, pieces = dev * W_IN_SHARD, []
        for orig, width, padded in _W_IN_SEGS:
            lo, hi = max(orig, first), min(orig + width, first + W_IN_SHARD)
            if lo < hi:
                a = padded + lo - orig
                blk, off = divmod(a, 1024)
                assert off + (hi - lo) <= 1024
                pieces.append(wp_blocks[blk][:, off:off + hi - lo])
        slabs.append(jnp.concatenate(pieces, axis=1))
    return jnp.stack(slabs)


def _pad_rows(u):
    return jnp.pad(u, ((POOL_HALO, POOL_HALO), (0, 0)))


def _block_diag4(w):
    out = jnp.zeros((256, 256), w.dtype)
    for i in range(4):
        out = lax.dynamic_update_slice(out, w[i], (64 * i, 64 * i))
    return out


def _layer_params(p, big, l):
    e_rep, e_tile, gmask, bdm = _s5_consts()
    wg = jnp.zeros((128, 256), F32)
    wg = lax.dynamic_update_slice(wg, p["gla_w_gate"][l, 0], (0, 0))
    wg = lax.dynamic_update_slice(wg, p["gla_w_gate"][l, 1], (16, 128))
    s5 = []
    for d in range(2):
        s5.append([p["s5_lam_re"][l, d], p["s5_lam_im"][l, d], p["s5_log_dt"][l, d].reshape(16, 1),
                   p["s5_b_re"][l, d].transpose(0, 2, 1).reshape(256, 64), p["s5_b_im"][l, d].transpose(0, 2, 1).reshape(256, 64),
                   p["s5_c_re"][l, d].reshape(256, 64), p["s5_c_im"][l, d].reshape(256, 64), e_rep, e_tile, gmask, bdm])
    havg = jnp.asarray((np.arange(256)[:, None] // 64 == np.arange(256)[None, :] // 64).astype(np.float32) / 64.0)
    e4 = jnp.asarray((np.arange(64)[:, None] == np.arange(256)[None, :] % 64).astype(np.float32))
    return dict(
        g_pre=p["g_pre"][l].reshape(1, D), g_post=p["g_post"][l].reshape(1, D), b_mod=p["b_mod"][l].reshape(1, 3 * D),
        w_mod=big["w_mod"], w_in=_pad_w_in_blocks(big["w_in_blocks"]) if "w_in_blocks" in big else _pad_w_in(big["w_in"]), w_out=big.get("w_out"),
        wg=wg, bg=p["gla_b_gate"][l].reshape(1, 256), g_norm=jnp.pad(p["gla_g_norm"][l].reshape(1, 64), ((0, 7), (0, 0))),
        bias8=_na_bias8(p["na_rpb"][l], f"na_bias_l{l}"), s5=s5, s5_d=p["s5_d"][l].reshape(1, 256), w_glu=None if big.get("s5_w_glu") is None else big["s5_w_glu"].astype(F32),
        b_glu=p["s5_b_glu"][l].reshape(1, 256), wpool=_block_diag4(p["pool_w"][l]), pool_scale=p["pool_scale"][l].reshape(1, 256),
        havg=havg, e4=e4)


def _matmul_tile(t, tile, steps):
    return t // steps if t % (8 * steps) == 0 else tile


def _cols(pz, start, width):
    return pz[:, start:start + width]


def _layer_fwd(z, modseg, lp, cos, sin, m_ctx, tile, s5_chunk, l, comm=None, late=None):
    t = z.shape[0]
    nct = m_ctx // tile
    nm = lambda s: f"{s}_l{l}"
    (h,) = rowwise_fwd(f_pre, nm("pre"), [z], [modseg], [lp["g_pre"]], [D], tile, nct)
    mm_tile = _matmul_tile(t, tile, 4)
    (gt,) = mm_nn_cols(h, lp["w_in"], C_GT, [1024], nm("in_proj_a"), tm=mm_tile)
    pv, nk, nv, su = mm_nn_cols(h, lp["w_in"], C_GV, [256] * 4, nm("in_proj_b"), tm=mm_tile)
    nq, pu, pk, pg, pq = mm_nn_cols(h, lp["w_in"], C_NQ, [256, 256, 128, 128, 128], nm("in_proj_c"), tm=mm_tile)
    q_r, k_r, lgf, lgb = rowwise_fwd(f_gla_prep, nm("gla_prep"), [pk, pg, pq, cos, sin], [], [lp["wg"], lp["bg"]], [128] * 4, tile, nct)
    half = None if comm is None else comm[0].shape[0] // 2
    spread = None if comm is None else [comm[0][:half], comm[1], comm[2], comm[3], comm[0][half:]]
    part = (lambda idx: None) if comm is None else (lambda idx: ([spread[i] for i in idx], [False] * len(idx)))
    (o1, st_f), got_late = gla_scan_fwd(q_r, k_r, pv, lgf, jnp.zeros((t, 256), F32), m_ctx, False, nm("gla_f"),
                                        None if late is None else (list(late), [False, False]))
    if late is not None:
        lp["w_out"], lp["w_glu"] = _gathered(got_late[0], False), _gathered(got_late[1], False).astype(F32)
    (o_gla, st_b), got_out = gla_scan_fwd(q_r, k_r, pv, lgb, o1, m_ctx, True, nm("gla_r"), part([2, 3]))
    received = None
    if comm is None:
        o_na = na_fwd(nq, nk, nv, lp["bias8"], m_ctx, nm("na"))
    else:
        o_na, got_in = na_fwd(nq, nk, nv, lp["bias8"], m_ctx, nm("na"), part([1]))
    s5p = [whole_fwd(f_s5_params, nm(f"s5_par{d}"), lp["s5"][d], [(1, 1024)] * 2 + [(256, 1024)] * 4) for d in range(2)]
    (y1, *states_f), got_mod_a = s5_scan_fwd(su, jnp.zeros((t, 256), F32), *s5p[0], m_ctx, s5_chunk, False, nm("s5_f"), part([0]))
    (y5, *states_b), got_mod_b = s5_scan_fwd(su, y1, *s5p[1], m_ctx, s5_chunk, True, nm("s5_r"), part([4]))
    if comm is not None:
        received = [jnp.concatenate([got_mod_a[0], got_mod_b[0]], axis=1), got_in[0], got_out[0], got_out[1]]
    pm = jnp.concatenate([pool_apply(_pad_rows(pu[:m_ctx]), m_ctx, False, nm("pool_c")),
                          pool_apply(_pad_rows(pu[m_ctx:]), t - m_ctx, False, nm("pool_x"))], axis=0)
    mix_rows = [o_gla, o_na, y5, su, pm, gt]
    mix_globs = [lp["g_norm"], lp["s5_d"], lp["w_glu"], lp["b_glu"], lp["wpool"], lp["pool_scale"], lp["havg"], lp["e4"]]
    (yg,) = rowwise_fwd(f_mix, nm("mix"), mix_rows, [], mix_globs, [D], tile, nct)
    out = mm_nn([yg], lp["w_out"], nm("out_proj"), tm=mm_tile)
    (z_new,) = rowwise_fwd(f_post, nm("post"), [z, out], [modseg], [lp["g_post"]], [D], tile, nct)
    saved = dict(z=z, h=h, pv=pv, nk=nk, nv=nv, su=su, nq=nq, pk=pk, pg=pg, pq=pq, q_r=q_r, k_r=k_r, lgf=lgf, lgb=lgb,
                 st_f=st_f, st_b=st_b, s5p=s5p, x0f=tuple(states_f), x0b=tuple(states_b), mix_rows=mix_rows, mix_globs=mix_globs,
                 yg=yg, out=out)
    return z_new, saved, received


def _f_pre_res(x, mod, g_pre):
    return f_pre(x, mod, g_pre)[0], x


def _layer_bwd(dz_new, sv, modseg, lp, cos, sin, m_ctx, tile, s5_chunk, l, comm=None, gdt=F32, send_early=False, as_slabs=False):
    t = dz_new.shape[0]
    nct = m_ctx // tile
    nm = lambda s: f"{s}_l{l}"
    g = {}
    dz_res, dout, dmod_post, g["g_post"] = rowwise_bwd(f_post, nm("post_b"), [sv["z"], sv["out"]], [modseg], [lp["g_post"]],
                                                       [dz_new], tile, nct, [True, True], [True])
    dyg = mm_nt([dout], lp["w_out"], nm("out_proj_dx"), tm=_matmul_tile(t, tile, 4))
    dw_tile = _matmul_tile(t, tile, 4)
    (g["w_out"],) = mm_tn(sv["yg"], [dout], nm("out_proj_dw"), tm=dw_tile, out_dtype=gdt)
    res = rowwise_bwd(f_mix, nm("mix_b"), sv["mix_rows"], [], sv["mix_globs"], [dyg], tile, nct, [True] * 6, [True] * 6 + [False] * 2)
    do_gla, do_na, dy5, dsu_a, dpm, dgt = res[:6]
    g["g_norm"], g["s5_d"], g["w_glu"], g["b_glu"], g["wpool"], g["pool_scale"] = res[6:]
    dpu = jnp.concatenate([pool_apply(_pad_rows(dpm[:m_ctx]), m_ctx, True, nm("pool_c_b")),
                           pool_apply(_pad_rows(dpm[m_ctx:]), t - m_ctx, True, nm("pool_x_b"))], axis=0)
    r_b = s5_scan_bwd(sv["su"], dy5, dsu_a, *sv["x0b"], *sv["s5p"][1], m_ctx, s5_chunk, True, nm("s5_r_b"))
    r_f = s5_scan_bwd(sv["su"], dy5, r_b[0], *sv["x0f"], *sv["s5p"][0], m_ctx, s5_chunk, False, nm("s5_f_b"))
    dsu = r_f[0]
    g["s5"] = [whole_bwd(f_s5_params, nm(f"s5_par{d}_b"), lp["s5"][d], list(r[1:]), [True] * 7 + [False] * 4)
               for d, r in ((0, r_f), (1, r_b))]
    part = (lambda idx: None) if comm is None else (lambda idx: ([comm[i] for i in idx], [True] * len(idx)))
    (dnq, dnk, dnv, dbias8), got_in = na_bwd(sv["nq"], sv["nk"], sv["nv"], do_na, lp["bias8"], m_ctx, nm("na_b"), part([0]))
    g["rpb"] = _na_rpb_grad(dbias8, nm("na_rpb_b"))
    zq, zv = jnp.zeros((t, 128), F32), jnp.zeros((t, 256), F32)
    early = ([_slabs(g["w_out"], False).astype(BF16), _slabs(g["w_glu"], False).astype(BF16)], [True, True]) if send_early else None
    (dq1, dk1, dv1, dlgb), g["early"] = gla_scan_bwd(sv["q_r"], sv["k_r"], sv["pv"], sv["lgb"], sv["st_b"], do_gla, (zq, zq, zv), m_ctx, True,
                                                     nm("gla_r_b"), early)
    (dq_r, dk_r, dpv, dlgf), got_out = gla_scan_bwd(sv["q_r"], sv["k_r"], sv["pv"], sv["lgf"], sv["st_f"], do_gla, (dq1, dk1, dv1), m_ctx, False,
                                                    nm("gla_f_b"), part([1, 2]))
    received = None if comm is None else [got_in[0], got_out[0], got_out[1]]
    dpk, dpg, dpq, g["wg"], g["bg"] = rowwise_bwd(f_gla_prep, nm("gla_prep_b"), [sv["pk"], sv["pg"], sv["pq"], cos, sin], [],
                                                  [lp["wg"], lp["bg"]], [dq_r, dk_r, dlgf, dlgb], tile, nct,
                                                  [True, True, True, False, False], [True, True])
    parts = [dgt, dpv, dnk, dnv, dsu, dnq, dpu, dpk, dpg, dpq, jnp.zeros((t, 128), F32)]
    dh = mm_nt(parts, lp["w_in"], nm("in_proj_dx"), tm=_matmul_tile(t, tile, 8))
    dw_blocks = mm_tn(sv["h"], parts, nm("in_proj_dw"), tm=dw_tile, out_dtype=gdt)
    if as_slabs:
        g["w_in_slabs"] = _w_in_slabs(dw_blocks)
    else:
        g["w_in"] = _unpad_w_in(jnp.concatenate(dw_blocks, axis=1))
    dz, dmod_pre, g["g_pre"] = rowwise_bwd(_f_pre_res, nm("pre_b"), [sv["z"]], [modseg], [lp["g_pre"]], [dh, dz_res], tile, nct, [True], [True])
    return dz, dmod_pre, dmod_post, g, received


def _f_mod_sum(cs, b_mod, w_mod):
    mod, _ = f_mod(cs, b_mod, w_mod)
    return mod, cs


def local_step(x, c, ctx, tgt, p, shards=None, tile=ROW_TILE, s5_chunk=S5_CHUNK):
    n_lat, m_ctx = x.shape[0], ctx.shape[0]
    n_layers = p["g_pre"].shape[0]
    z = jnp.concatenate([ctx, x], axis=0)
    cos, sin = _rope_tables(n_lat, m_ctx)
    cs = jnp.concatenate([c.reshape(1, D), p["c_ctx"].reshape(1, D), jnp.zeros((6, D), F32)], axis=0)
    gather = [False] * len(_SHARDED)
    lps, mods, silus, saves = [], [], [], []
    got = gather_two_level(shards[0][:2], "gather_weights_l0") if shards is not None else None
    for l in range(n_layers):
        if shards is None:
            big = {n: p[n][l] for n in _SHARDED}
        else:
            big = {n: _gathered(g, _BY_COLS[n]) for n, g in zip(_SHARDED, got) if n != "w_in"}
            big["w_in_blocks"] = got[_SHARDED.index("w_in")]
        lp = _layer_params(p, big, l)
        mod8, s8 = whole_fwd(f_mod, f"mod_l{l}", [cs, lp["b_mod"], lp["w_mod"]], [(8, 3 * D), (8, D)])
        modseg = mod8[:2].reshape(2, 1, 3 * D)
        comm = shards[l + 1] if shards is not None and l + 1 < n_layers else None
        late = shards[0][2:] if shards is not None and l == 0 else None
        z, sv, got = _layer_fwd(z, modseg, lp, cos, sin, m_ctx, tile, s5_chunk, l, comm, late)
        lps.append(lp); mods.append(modseg); silus.append(s8); saves.append(sv)
    loss, dz = loss_and_grad(z, tgt, m_ctx, "loss", tile)
    grads, received = [None] * n_layers, [None] * n_layers
    gdt = F32 if shards is None else BF16
    dcs = jnp.zeros((8, D), F32)
    pending = None
    for l in reversed(range(n_layers)):
        lp = lps[l]
        dz, dmod_pre, dmod_post, g, got = _layer_bwd(dz, saves[l], mods[l], lp, cos, sin, m_ctx, tile, s5_chunk, l, pending, gdt,
                                                     send_early=shards is not None and l == 0, as_slabs=shards is not None)
        if pending is not None:
            received[l + 1] = got
        dmod = jnp.concatenate([dmod_pre.reshape(2, 3 * D)[:, :2 * D], dmod_post.reshape(2, 3 * D)[:, 2 * D:]], axis=1)
        dmod8 = jnp.pad(dmod, ((0, 6), (0, 0)))
        dcs, g["b_mod"] = whole_bwd(_f_mod_sum, f"mod_b_l{l}", [cs, lp["b_mod"], lp["w_mod"]], [dmod8, dcs], [True, True, False])
        if shards is None:
            g["w_mod"] = jnp.concatenate(mm_tn(silus[l], [dmod8[:, :D], dmod8[:, D:2 * D], dmod8[:, 2 * D:]], f"mod_dw_l{l}", tm=8), axis=1)
        else:
            g["mod_s"], g["mod_d"] = silus[l][:2], dmod
        grads[l] = g
        if shards is not None:
            pending = _layer_sends(g)
    if shards is not None:
        got_in, got_small = exchange([pending[0], _small_sends(dcs[1], grads)], [True, False], "exchange_grads_l0")
        received[0] = [got_in] + list(grads[0]["early"]) + [got_small]
    return loss, dz[m_ctx:], dcs[1], grads, received


_WEIGHTS = ["c_ctx", "w_mod", "b_mod", "g_pre", "g_post", "w_in", "w_out", "gla_w_gate", "gla_b_gate", "gla_g_norm", "na_rpb",
            "s5_lam_re", "s5_lam_im", "s5_log_dt", "s5_b_re", "s5_b_im", "s5_c_re", "s5_c_im", "s5_d", "s5_w_glu", "s5_b_glu",
            "pool_w", "pool_scale"]
_INPUTS = ["x", "c", "ctx"] + _WEIGHTS + ["loss_target"] + ["m_" + n for n in _WEIGHTS] + ["v_" + n for n in _WEIGHTS]
_SHARDED = ["w_mod", "w_in", "w_out", "s5_w_glu"]
_BY_COLS = {"w_mod": True, "w_in": True, "w_out": False, "s5_w_glu": False}
_GRAD_SHARDED = ["w_in", "w_out", "s5_w_glu"]
_SMALL = [n for n in _WEIGHTS if n not in _SHARDED]
_SMALL_PER_LAYER = [n for n in _SMALL if n != "c_ctx"]
_PACK_ROWS = 256


def _pack_plan(like):
    tiled = [i for i, a in enumerate(like) if a.size % 1024 == 0]
    loose = [i for i, a in enumerate(like) if a.size % 1024 != 0]
    tail = -(-sum(like[i].size for i in loose) // 1024) * 8
    rows = sum(like[i].size // 128 for i in tiled) + tail
    return tiled, loose, tail, -(-rows // _PACK_ROWS) * _PACK_ROWS - rows


def _pack_rows(like, index):
    tiled, _, _, _ = _pack_plan(like)
    row = 0
    for i in tiled:
        n = like[i].size // 128
        if i == index:
            return row, row + n
        row += n
    raise ValueError("not a tile-aligned entry")


def _pack(arrs):
    tiled, loose, tail, fill = _pack_plan(arrs)
    dt = arrs[0].dtype
    flat = jnp.concatenate([arrs[i].reshape(-1) for i in loose])
    flat = jnp.pad(flat, (0, tail * 128 - flat.shape[0])).reshape(tail, 128)
    return jnp.concatenate([arrs[i].reshape(-1, 128) for i in tiled] + [flat, jnp.zeros((fill, 128), dt)], axis=0)


def _unpack(packed, like):
    tiled, loose, tail, _ = _pack_plan(like)
    out, row = [None] * len(like), 0
    for i in tiled:
        n = like[i].size // 128
        out[i] = packed[row:row + n].reshape(like[i].shape)
        row += n
    flat, pos = packed[row:row + tail].reshape(-1), 0
    for i in loose:
        out[i] = flat[pos:pos + like[i].size].reshape(like[i].shape)
        pos += like[i].size
    return out


def _gathered(g, cols):
    if cols:
        return g.transpose(1, 0, 2).reshape(g.shape[1], N_DEV * g.shape[2])
    return g.reshape(N_DEV * g.shape[1], g.shape[2])


def _slabs(w, cols):
    r, c = w.shape
    if cols:
        return w.reshape(r, N_DEV, c // N_DEV).transpose(1, 0, 2)
    return w.reshape(N_DEV, r // N_DEV, c)


def _layer_small(g):
    s5 = lambda i, f: jnp.stack([f(g["s5"][d][i]) for d in range(2)])
    return {
        "b_mod": g["b_mod"].reshape(3 * D), "g_pre": g["g_pre"].reshape(D), "g_post": g["g_post"].reshape(D),
        "gla_w_gate": jnp.stack([g["wg"][0:16, 0:128], g["wg"][16:32, 128:256]]),
        "gla_b_gate": g["bg"].reshape(2, 128), "gla_g_norm": g["g_norm"][0], "na_rpb": g["rpb"],
        "s5_lam_re": s5(0, lambda a: a), "s5_lam_im": s5(1, lambda a: a), "s5_log_dt": s5(2, lambda a: a.reshape(16)),
        "s5_b_re": s5(3, lambda a: a.reshape(16, 16, 64).transpose(0, 2, 1)),
        "s5_b_im": s5(4, lambda a: a.reshape(16, 16, 64).transpose(0, 2, 1)),
        "s5_c_re": s5(5, lambda a: a.reshape(16, 16, 64)), "s5_c_im": s5(6, lambda a: a.reshape(16, 16, 64)),
        "s5_d": g["s5_d"].reshape(256), "s5_b_glu": g["b_glu"].reshape(256),
        "pool_w": jnp.stack([g["wpool"][64 * i:64 * i + 64, 64 * i:64 * i + 64] for i in range(4)]),
        "pool_scale": g["pool_scale"].reshape(256),
    }


def _layer_sends(g):
    return [g["w_in_slabs"].astype(BF16), _slabs(g["w_out"], False).astype(BF16), _slabs(g["w_glu"], False).astype(BF16)]


def _small_sends(d_c_ctx, grads):
    per_layer = [_layer_small(g) for g in grads]
    full = {n: jnp.stack([s[n] for s in per_layer]) for n in _SMALL_PER_LAYER}
    full["c_ctx"] = d_c_ctx
    factors = [jnp.stack([g["mod_s"] for g in grads]), jnp.stack([g["mod_d"] for g in grads])]
    return _pack([full[n] for n in _SMALL] + factors).astype(BF16)


def kernel(x, c, ctx, c_ctx, w_mod, b_mod, g_pre, g_post, w_in, w_out, gla_w_gate, gla_b_gate, gla_g_norm, na_rpb, s5_lam_re, s5_lam_im, s5_log_dt, s5_b_re, s5_b_im, s5_c_re, s5_c_im, s5_d, s5_w_glu, s5_b_glu, pool_w, pool_scale, loss_target, m_c_ctx, m_w_mod, m_b_mod, m_g_pre, m_g_post, m_w_in, m_w_out, m_gla_w_gate, m_gla_b_gate, m_gla_g_norm, m_na_rpb, m_s5_lam_re, m_s5_lam_im, m_s5_log_dt, m_s5_b_re, m_s5_b_im, m_s5_c_re, m_s5_c_im, m_s5_d, m_s5_w_glu, m_s5_b_glu, m_pool_w, m_pool_scale, v_c_ctx, v_w_mod, v_b_mod, v_g_pre, v_g_post, v_w_in, v_w_out, v_gla_w_gate, v_gla_b_gate, v_gla_g_norm, v_na_rpb, v_s5_lam_re, v_s5_lam_im, v_s5_log_dt, v_s5_b_re, v_s5_b_im, v_s5_c_re, v_s5_c_im, v_s5_d, v_s5_w_glu, v_s5_b_glu, v_pool_w, v_pool_scale):
    given = dict(zip(_INPUTS, (x, c, ctx, c_ctx, w_mod, b_mod, g_pre, g_post, w_in, w_out, gla_w_gate, gla_b_gate, gla_g_norm, na_rpb, s5_lam_re, s5_lam_im, s5_log_dt, s5_b_re, s5_b_im, s5_c_re, s5_c_im, s5_d, s5_w_glu, s5_b_glu, pool_w, pool_scale, loss_target, m_c_ctx, m_w_mod, m_b_mod, m_g_pre, m_g_post, m_w_in, m_w_out, m_gla_w_gate, m_gla_b_gate, m_gla_g_norm, m_na_rpb, m_s5_lam_re, m_s5_lam_im, m_s5_log_dt, m_s5_b_re, m_s5_b_im, m_s5_c_re, m_s5_c_im, m_s5_d, m_s5_w_glu, m_s5_b_glu, m_pool_w, m_pool_scale, v_c_ctx, v_w_mod, v_b_mod, v_g_pre, v_g_post, v_w_in, v_w_out, v_gla_w_gate, v_gla_b_gate, v_gla_g_norm, v_na_rpb, v_s5_lam_re, v_s5_lam_im, v_s5_log_dt, v_s5_b_re, v_s5_b_im, v_s5_c_re, v_s5_c_im, v_s5_d, v_s5_w_glu, v_s5_b_glu, v_pool_w, v_pool_scale)))
    n_layers = w_in.shape[0]
    shards = [[given[n][l].astype(BF16) for n in _SHARDED] for l in range(n_layers)]
    p = {n: given[n] for n in _SMALL}
    loss, grad_x, _, _, received = local_step(x[0], c, ctx[0], loss_target[0], p, shards)
    final = {}
    for n in _GRAD_SHARDED:
        per_layer = [adamw(received[l][_GRAD_SHARDED.index(n)], given[n][l], given["m_" + n][l], given["v_" + n][l], f"adamw_{n}_l{l}")
                     for l in range(n_layers)]
        final[n] = [jnp.stack([res[kind] for res in per_layer]) for kind in range(4)]
    factor_like = [jnp.zeros((n_layers, 2, D), F32), jnp.zeros((n_layers, 2, 3 * D), F32)]
    like = [given[n] for n in _SMALL] + factor_like
    small_recv = received[0][-1]
    rows_s, rows_d = _pack_rows(like, len(_SMALL)), _pack_rows(like, len(_SMALL) + 1)
    fac_s = small_recv[:, rows_s[0]:rows_s[1]].reshape(N_DEV, n_layers, 2, D)
    fac_d = small_recv[:, rows_d[0]:rows_d[1]].reshape(N_DEV, n_layers, 2, 3 * D)
    me = 4 * lax.axis_index("x") + 2 * lax.axis_index("y") + lax.axis_index("c")
    cols = w_mod.shape[2]
    per_layer = []
    for l in range(n_layers):
        s_all = fac_s[:, l].reshape(2 * N_DEV, D)
        d_mine = lax.dynamic_slice_in_dim(fac_d[:, l].reshape(2 * N_DEV, 3 * D), me * cols, cols, axis=1)
        (g_mod,) = mm_tn(s_all, [d_mine], f"mod_dw_l{l}", tm=2 * N_DEV, tn=cols)
        per_layer.append(adamw(g_mod[None], given["w_mod"][l], given["m_w_mod"][l], given["v_w_mod"][l], f"adamw_w_mod_l{l}"))
    final["w_mod"] = [jnp.stack([res[kind] for res in per_layer]) for kind in range(4)]
    res = adamw(small_recv, _pack(like), _pack([given["m_" + n] for n in _SMALL] + factor_like),
                _pack([given["v_" + n] for n in _SMALL] + factor_like), "adamw_small")
    unpacked = [_unpack(packed, like) for packed in res]
    for i, n in enumerate(_SMALL):
        final[n] = [unpacked[kind][i] for kind in range(4)]
    loss = lax.psum(loss, ("x", "y", "c"))
    return (loss, grad_x[None], *[final[n][0] for n in _WEIGHTS], *[final[n][1] for n in _WEIGHTS],
            *[final[n][2] for n in _WEIGHTS], *[final[n][3] for n in _WEIGHTS])
```

```python
import functools
import math

import numpy as np
import jax
import jax.numpy as jnp
from jax import lax
from jax.experimental import pallas as pl
from jax.experimental.pallas import tpu as pltpu

F32 = jnp.float32
BF16 = jnp.bfloat16
HIGHEST = lax.Precision.HIGHEST
HIGH = lax.Precision.HIGH

D = 1024
GRID_W = 64
EPS = 1e-6
N_DEV = 8
C_GT, C_GV, C_NK, C_NV, C_SU, C_NQ, C_PU, C_GK, C_GG, C_GQ, C_END = 0, 1024, 1280, 1536, 1792, 2048, 2304, 2560, 2688, 2816, 2944
PW = 3072
N_CTX_ORIG = 416
N_IN = 2848
GLA_CHUNK = 128
S5_CHUNK = 256
ROW_TILE = 256
VMEM_LIMIT = 56 * 1024 * 1024

ADAM_LR, ADAM_B1, ADAM_B2, ADAM_EPS, ADAM_WD, ADAM_STEP = 0.001, 0.9, 0.999, 1e-08, 0.01, 10


def _cparams(**kw):
    return pltpu.CompilerParams(vmem_limit_bytes=VMEM_LIMIT, **kw)


def _dg(a, b, ca, cb, precision=None):
    return lax.dot_general(a, b, (((ca,), (cb,)), ((), ())), precision=precision, preferred_element_type=F32)


def hdot(a, b):
    return _dg(a, b, 1, 0, HIGHEST)


def hdot_nt(a, b):
    return _dg(a, b, 1, 1, HIGHEST)


def hdot_tn(a, b):
    return _dg(a, b, 0, 0, HIGHEST)


def mdot(a, b):
    return _dg(a, b, 1, 0, HIGH)


def mdot_nt(a, b):
    return _dg(a, b, 1, 1, HIGH)


def mdot_tn(a, b):
    return _dg(a, b, 0, 0, HIGH)


def b_nn(a, b):
    return _dg(a.astype(BF16), b.astype(BF16), 1, 0)


def b_nt(a, b):
    return _dg(a.astype(BF16), b.astype(BF16), 1, 1)


def b_tn(a, b):
    return _dg(a.astype(BF16), b.astype(BF16), 0, 0)


@jax.custom_vjp
def bdot(a, b):
    return b_nn(a, b)


def _bdot_fwd(a, b):
    return b_nn(a, b), (a, b)


def _bdot_bwd(res, ct):
    a, b = res
    return b_nt(ct, b).astype(a.dtype), b_tn(a, ct).astype(b.dtype)


bdot.defvjp(_bdot_fwd, _bdot_bwd)


def _log_sigmoid(z):
    return jnp.minimum(z, 0.0) - jnp.log(1.0 + jnp.exp(-jnp.abs(z)))


def _silu(z):
    return z * jax.nn.sigmoid(z)


def _gelu(z):
    return 0.5 * z * (1.0 + jnp.tanh(math.sqrt(2.0 / math.pi) * (z + 0.044715 * (z * z * z))))


def _cat(vals):
    return vals[0] if len(vals) == 1 else jnp.concatenate(vals, axis=-1)


def mm_nn(a_parts, b, name, tm=ROW_TILE, tn=1024):
    t = a_parts[0].shape[0]
    k, n = b.shape
    na = len(a_parts)
    tn = min(tn, n)

    def body(*refs):
        a = _cat([r[...].astype(BF16) for r in refs[:na]])
        refs[na + 1][...] = _dg(a, refs[na][...].astype(BF16), 1, 0)

    return pl.pallas_call(
        body, name=name, grid=(n // tn, t // tm),
        in_specs=[pl.BlockSpec((tm, p.shape[1]), lambda j, i: (i, 0)) for p in a_parts]
        + [pl.BlockSpec((k, tn), lambda j, i: (0, j))],
        out_specs=pl.BlockSpec((tm, tn), lambda j, i: (i, j)),
        out_shape=jax.ShapeDtypeStruct((t, n), F32),
        compiler_params=_cparams(dimension_semantics=("arbitrary", "arbitrary")),
    )(*a_parts, b)


def mm_nn_cols(a, b, start, widths, name, tm=ROW_TILE):
    t, k = a.shape
    tn = 1024
    assert start % tn == 0 and sum(widths) <= tn

    def body(a_ref, b_ref, *o_refs):
        r = _dg(a_ref[...].astype(BF16), b_ref[...].astype(BF16), 1, 0)
        off = 0
        for o_ref, w in zip(o_refs, widths):
            o_ref[...] = r[:, off:off + w]
            off += w

    return pl.pallas_call(
        body, name=name, grid=(t // tm,),
        in_specs=[pl.BlockSpec((tm, k), lambda i: (i, 0)), pl.BlockSpec((k, tn), lambda i: (0, start // tn))],
        out_specs=[pl.BlockSpec((tm, w), lambda i: (i, 0)) for w in widths],
        out_shape=[jax.ShapeDtypeStruct((t, w), F32) for w in widths],
        compiler_params=_cparams(dimension_semantics=("arbitrary",)),
    )(a, b)


def mm_nt(a_parts, b, name, tm=ROW_TILE):
    t = a_parts[0].shape[0]
    n, k = b.shape
    na = len(a_parts)

    def body(*refs):
        a = _cat([r[...].astype(BF16) for r in refs[:na]])
        refs[na + 1][...] = _dg(a, refs[na][...].astype(BF16), 1, 1)

    return pl.pallas_call(
        body, name=name, grid=(t // tm,),
        in_specs=[pl.BlockSpec((tm, p.shape[1]), lambda i: (i, 0)) for p in a_parts]
        + [pl.BlockSpec((n, k), lambda i: (0, 0))],
        out_specs=pl.BlockSpec((tm, n), lambda i: (i, 0)),
        out_shape=jax.ShapeDtypeStruct((t, n), F32),
        compiler_params=_cparams(dimension_semantics=("arbitrary",)),
    )(*a_parts, b)


def mm_tn(a, b_parts, name, tm=ROW_TILE, tn=1024, out_dtype=F32):
    t, k = a.shape
    widths = [p.shape[1] for p in b_parts]
    n = sum(widths)
    assert n % tn == 0
    groups, cur, acc = [], [], 0
    for idx, w in enumerate(widths):
        cur.append(idx)
        acc += w
        if acc == tn:
            groups.append(cur)
            cur, acc = [], 0
        assert acc < tn
    assert not cur
    outs = []
    for gi, grp in enumerate(groups):
        parts = [b_parts[i] for i in grp]
        npart = len(parts)
        nsteps = t // tm

        def body(*refs, npart=npart, nsteps=nsteps):
            a_v = refs[0][...].astype(BF16)
            b_v = _cat([r[...].astype(BF16) for r in refs[1:1 + npart]])
            o_ref, acc_ref = refs[1 + npart], refs[2 + npart]
            r = _dg(a_v, b_v, 0, 0)

            @pl.when(pl.program_id(0) == 0)
            def _():
                acc_ref[...] = r

            @pl.when(pl.program_id(0) != 0)
            def _():
                acc_ref[...] += r

            @pl.when(pl.program_id(0) == nsteps - 1)
            def _():
                o_ref[...] = acc_ref[...].astype(o_ref.dtype)

        outs.append(pl.pallas_call(
            body, name=f"{name}_{gi}", grid=(nsteps,),
            in_specs=[pl.BlockSpec((tm, k), lambda i: (i, 0))]
            + [pl.BlockSpec((tm, p.shape[1]), lambda i: (i, 0)) for p in parts],
            out_specs=pl.BlockSpec((k, tn), lambda i: (0, 0)),
            out_shape=jax.ShapeDtypeStruct((k, tn), out_dtype),
            scratch_shapes=[pltpu.VMEM((k, tn), F32)],
            compiler_params=_cparams(dimension_semantics=("arbitrary",)),
        )(a, *parts))
    return outs


def _seg_of(i, nct):
    return jnp.where(i < nct, 1, 0)


def rowwise_fwd(fn, name, rows, segs, globs, out_widths, tile, nct):
    t = rows[0].shape[0]
    nr, ns, ng = len(rows), len(segs), len(globs)

    def body(*refs):
        vals = [r[...] for r in refs[:nr]] + [r[0] for r in refs[nr:nr + ns]] + [r[...] for r in refs[nr + ns:nr + ns + ng]]
        outs = fn(*vals)
        for o_ref, o in zip(refs[nr + ns + ng:], outs):
            o_ref[...] = o

    return pl.pallas_call(
        body, name=name, grid=(t // tile,),
        in_specs=[pl.BlockSpec((tile, r.shape[1]), lambda i: (i, 0)) for r in rows]
        + [pl.BlockSpec((1, 1, s.shape[2]), lambda i: (_seg_of(i, nct), 0, 0)) for s in segs]
        + [pl.BlockSpec(g.shape, lambda i: (0, 0)) for g in globs],
        out_specs=[pl.BlockSpec((tile, w), lambda i: (i, 0)) for w in out_widths],
        out_shape=[jax.ShapeDtypeStruct((t, w), F32) for w in out_widths],
        compiler_params=_cparams(dimension_semantics=("arbitrary",)),
    )(*rows, *segs, *globs)


def rowwise_bwd(fn, name, rows, segs, globs, cts, tile, nct, row_diff, glob_diff):
    t = rows[0].shape[0]
    nr, ns, ng, nc = len(rows), len(segs), len(globs), len(cts)
    d_rows = [i for i in range(nr) if row_diff[i]]
    d_globs = [i for i in range(ng) if glob_diff[i]]

    def body(*refs):
        in_refs, out_refs = refs[:nr + ns + ng + nc], refs[nr + ns + ng + nc:]
        row_v = [r[...] for r in in_refs[:nr]]
        seg_v = [r[0] for r in in_refs[nr:nr + ns]]
        glob_v = [r[...] for r in in_refs[nr + ns:nr + ns + ng]]
        ct_v = tuple(r[...] for r in in_refs[nr + ns + ng:])

        def wrapped(dr, sv, dg):
            rv = list(row_v)
            for j, i in enumerate(d_rows):
                rv[i] = dr[j]
            gv = list(glob_v)
            for j, i in enumerate(d_globs):
                gv[i] = dg[j]
            return tuple(fn(*rv, *sv, *gv))

        _, vjp = jax.vjp(wrapped, [row_v[i] for i in d_rows], seg_v, [glob_v[i] for i in d_globs])
        c_rows, c_segs, c_globs = vjp(ct_v)
        i = pl.program_id(0)
        k = 0
        for c in c_rows:
            out_refs[k][...] = c
            k += 1
        seg_first = jnp.logical_or(i == 0, i == nct)
        for c in c_segs:
            ref = out_refs[k]
            k += 1

            @pl.when(seg_first)
            def _(ref=ref, c=c):
                ref[0] = c

            @pl.when(jnp.logical_not(seg_first))
            def _(ref=ref, c=c):
                ref[0] += c
        for c in c_globs:
            ref = out_refs[k]
            k += 1

            @pl.when(i == 0)
            def _(ref=ref, c=c):
                ref[...] = c

            @pl.when(i != 0)
            def _(ref=ref, c=c):
                ref[...] += c

    return pl.pallas_call(
        body, name=name, grid=(t // tile,),
        in_specs=[pl.BlockSpec((tile, r.shape[1]), lambda i: (i, 0)) for r in rows]
        + [pl.BlockSpec((1, 1, s.shape[2]), lambda i: (_seg_of(i, nct), 0, 0)) for s in segs]
        + [pl.BlockSpec(g.shape, lambda i: (0, 0)) for g in globs]
        + [pl.BlockSpec((tile, c.shape[1]), lambda i: (i, 0)) for c in cts],
        out_specs=[pl.BlockSpec((tile, rows[i].shape[1]), lambda i: (i, 0)) for i in d_rows]
        + [pl.BlockSpec((1, 1, s.shape[2]), lambda i: (_seg_of(i, nct), 0, 0)) for s in segs]
        + [pl.BlockSpec(globs[i].shape, lambda i: (0, 0)) for i in d_globs],
        out_shape=[jax.ShapeDtypeStruct(rows[i].shape, F32) for i in d_rows]
        + [jax.ShapeDtypeStruct(s.shape, F32) for s in segs]
        + [jax.ShapeDtypeStruct(globs[i].shape, F32) for i in d_globs],
        compiler_params=_cparams(dimension_semantics=("arbitrary",)),
    )(*rows, *segs, *globs, *cts)


def f_pre(x, mod, g_pre):
    shift, scale = mod[:, :D], mod[:, D:2 * D]
    rs = lax.rsqrt(jnp.mean(x * x, axis=-1, keepdims=True) + EPS)
    return ((x * rs) * g_pre * (1.0 + scale) + shift,)


def f_post(x, out, mod, g_post):
    gate = mod[:, 2 * D:]
    rs = lax.rsqrt(jnp.mean(out * out, axis=-1, keepdims=True) + EPS)
    return (x + gate * ((out * rs) * g_post),)


def f_mix(o_gla, o_na, y5, u5, pm, gcols, g_norm, s5_d, w_glu, b_glu, wpool, pool_scale, havg, e4):
    ms = mdot(o_gla * o_gla, havg)
    y_gla = o_gla * lax.rsqrt(ms + EPS) * jnp.sum(hdot(g_norm, e4), axis=0, keepdims=True)
    g = _gelu(u5 * s5_d + y5)
    y_s5 = g * jax.nn.sigmoid(bdot(g, w_glu) + b_glu)
    y_pool = bdot(pm, wpool) * pool_scale
    ycat = jnp.concatenate([y_gla, o_na, y_s5, y_pool], axis=-1)
    return (ycat * _silu(gcols),)


@jax.custom_vjp
def _rot_half16(x):
    lane = lax.broadcasted_iota(jnp.int32, x.shape, 1)
    first = jnp.bitwise_and(lane, 15) < 8
    return jnp.where(first, -pltpu.roll(x, x.shape[1] - 8, 1), pltpu.roll(x, 8, 1))


def _rot_fwd(x):
    return _rot_half16(x), None


def _rot_bwd(_, ct):
    return (-_rot_half16(ct),)


_rot_half16.defvjp(_rot_fwd, _rot_bwd)


def f_gla_prep(pk, pg, pq, cos, sin, wg, bg):
    z = bdot(pg, wg) + bg
    lg = _log_sigmoid(z) * (1.0 / 16.0)
    k_r = pk * cos + _rot_half16(pk) * sin
    q_r = (pq * cos + _rot_half16(pq) * sin) * (32.0 ** -0.5)
    return q_r, k_r, lg[:, :128], lg[:, 128:]


def _gla_consts(rev):
    c = GLA_CHUNK
    i = np.arange(c)
    inc = (i[None, :] >= i[:, None]) if rev else (i[None, :] <= i[:, None])
    mq = np.stack([(np.arange(128) // 32 == h) for h in range(4)]).astype(np.float32).reshape(4, 1, 128)
    mv = np.stack([(np.arange(256) // 64 == h) for h in range(4)]).astype(np.float32).reshape(4, 1, 256)
    bdt = (np.arange(256)[:, None] // 64 == np.arange(128)[None, :] // 32).astype(np.float32)
    inc = inc.astype(np.float32)
    return jnp.asarray(inc), jnp.asarray(inc.T.copy()), jnp.asarray(mq), jnp.asarray(mv), jnp.asarray(bdt)


def _stack_heads(x, m_ref):
    return jnp.concatenate([x * m_ref[h] for h in range(4)], axis=0)


def _tile4(m):
    return jnp.concatenate([m, m, m, m], axis=0)


def _fold_heads(r4, m_ref):
    r = r4.shape[0] // 4
    out = m_ref[0] * r4[0:r]
    for h in range(1, 4):
        out = out + m_ref[h] * r4[h * r:(h + 1) * r]
    return out


def _gla_chunk_of(s, n_ctx_chunks, n_chunks, rev):
    if not rev:
        return s
    return jnp.where(s < n_ctx_chunks, n_ctx_chunks - 1 - s, n_ctx_chunks + n_chunks - 1 - s)


def gla_scan_fwd(q, k, v, lg, acc, n_ctx_rows, rev, name, comm=None):
    t = q.shape[0]
    nch, ncc = t // GLA_CHUNK, n_ctx_rows // GLA_CHUNK
    inc, inc_t, mq, mv, bdt = _gla_consts(rev)

    def body(q_ref, k_ref, v_ref, lg_ref, acc_ref, inc_ref, inct_ref, mq_ref, mv_ref, bdt_ref, o_ref, st_ref):
        lmask, lmask_t = inc_ref[...], inct_ref[...]
        bd = bdt_ref[...]

        def step(s, st):
            c = _gla_chunk_of(s, ncc, nch, rev)
            rows = pl.ds(pl.multiple_of(c * GLA_CHUNK, GLA_CHUNK), GLA_CHUNK)
            qc, kc, vc, lgc = q_ref[rows, :], k_ref[rows, :], v_ref[rows, :], lg_ref[rows, :]
            st_ref[c] = st
            b = mdot(lmask, lgc)
            blast = jnp.sum(lgc, axis=0, keepdims=True)
            qe, ke, kd = qc * jnp.exp(b), kc * jnp.exp(-b), kc * jnp.exp(blast - b)
            ke4, v4 = _stack_heads(ke, mq_ref), _stack_heads(vc, mv_ref)
            a_cols = jnp.concatenate([lmask] * 4, axis=1) * b_nt(qe, ke4)
            o_ref[rows, :] = acc_ref[rows, :] + b_nt(qe, st) + b_nn(a_cols, v4)
            return st * jnp.exp(blast) + bd * mdot_tn(vc, kd)

        lax.fori_loop(0, nch, step, jnp.zeros((256, 128), F32))

    return _call_with_exchange(body, name, [q, k, v, lg, acc, inc, inc_t, mq, mv, bdt],
                               [jax.ShapeDtypeStruct((t, 256), F32), jax.ShapeDtypeStruct((nch, 256, 128), F32)], comm)


def gla_scan_bwd(q, k, v, lg, st, do, acc, n_ctx_rows, rev, name, comm=None):
    t = q.shape[0]
    nch, ncc = t // GLA_CHUNK, n_ctx_rows // GLA_CHUNK
    inc, inc_t, mq, mv, bdt = _gla_consts(rev)

    def body(q_ref, k_ref, v_ref, lg_ref, st_ref, do_ref, aq_ref, ak_ref, av_ref, inc_ref, inct_ref, mq_ref, mv_ref, bdt_ref,
             dq_ref, dk_ref, dv_ref, dlg_ref):
        lmask, lmask_t = inc_ref[...], inct_ref[...]
        bd = bdt_ref[...]

        def step(j, carry):
            dst, gsum = carry
            s = nch - 1 - j
            c = _gla_chunk_of(s, ncc, nch, rev)
            rows = pl.ds(pl.multiple_of(c * GLA_CHUNK, GLA_CHUNK), GLA_CHUNK)
            qc, kc, vc, lgc, doc = q_ref[rows, :], k_ref[rows, :], v_ref[rows, :], lg_ref[rows, :], do_ref[rows, :]
            stc = st_ref[c]
            b = mdot(lmask, lgc)
            blast = jnp.sum(lgc, axis=0, keepdims=True)
            eb, enb, edb = jnp.exp(b), jnp.exp(-b), jnp.exp(blast - b)
            qe, ke, kd = qc * eb, kc * enb, kc * edb
            ke4, v4 = _stack_heads(ke, mq_ref), _stack_heads(vc, mv_ref)
            lm4 = _tile4(lmask_t)
            at = lm4 * b_nt(ke4, qe)
            dat = lm4 * mdot_nt(v4, doc)
            dqe = mdot(doc, stc) + mdot_tn(dat, ke4)
            dke = _fold_heads(mdot(dat, qe), mq_ref)
            dv = b_nt(kd, dst) + _fold_heads(b_nn(at, doc), mv_ref)
            dkd = mdot(vc, dst)
            dq = dqe * eb
            dk = dke * enb + dkd * edb
            g = qc * dq - kc * dk
            dlg_ref[rows, :] = mdot_tn(lmask, g) + gsum
            dq_ref[rows, :] = aq_ref[rows, :] + dq
            dk_ref[rows, :] = ak_ref[rows, :] + dk
            dv_ref[rows, :] = av_ref[rows, :] + dv
            dst_new = dst * jnp.exp(blast) + bd * mdot_tn(doc, qe)
            return dst_new, gsum + jnp.sum(g, axis=0, keepdims=True)

        lax.fori_loop(0, nch, step, (jnp.zeros((256, 128), F32), jnp.zeros((1, 128), F32)))

    return _call_with_exchange(body, name, [q, k, v, lg, st, do, *acc, inc, inc_t, mq, mv, bdt],
                               [jax.ShapeDtypeStruct((t, 128), F32), jax.ShapeDtypeStruct((t, 128), F32),
                                jax.ShapeDtypeStruct((t, 256), F32), jax.ShapeDtypeStruct((t, 128), F32)], comm)


def whole_fwd(fn, name, args, out_shapes):
    def body(*refs):
        outs = fn(*[r[...] for r in refs[:len(args)]])
        for o_ref, o in zip(refs[len(args):], outs):
            o_ref[...] = o

    vm = pl.BlockSpec(memory_space=pltpu.VMEM)
    return pl.pallas_call(
        body, name=name, in_specs=[vm] * len(args), out_specs=[vm] * len(out_shapes),
        out_shape=[jax.ShapeDtypeStruct(s, F32) for s in out_shapes], compiler_params=_cparams(),
    )(*args)


def whole_bwd(fn, name, args, cts, diff):
    d_idx = [i for i in range(len(args)) if diff[i]]

    def body(*refs):
        vals = [r[...] for r in refs[:len(args)]]
        ct_v = tuple(r[...] for r in refs[len(args):len(args) + len(cts)])

        def wrapped(dv):
            av = list(vals)
            for j, i in enumerate(d_idx):
                av[i] = dv[j]
            return tuple(fn(*av))

        _, vjp = jax.vjp(wrapped, [vals[i] for i in d_idx])
        (c_args,) = vjp(ct_v)
        for o_ref, c in zip(refs[len(args) + len(cts):], c_args):
            o_ref[...] = c

    vm = pl.BlockSpec(memory_space=pltpu.VMEM)
    return pl.pallas_call(
        body, name=name, in_specs=[vm] * (len(args) + len(cts)), out_specs=[vm] * len(d_idx),
        out_shape=[jax.ShapeDtypeStruct(args[i].shape, F32) for i in d_idx], compiler_params=_cparams(),
    )(*args, *cts)


def _s5_consts():
    e_rep = (np.arange(256)[:, None] // 16 == np.arange(16)[None, :]).astype(np.float32)
    e_tile = (np.arange(64)[:, None] == np.arange(1024)[None, :] % 64).astype(np.float32)
    gmask = (np.arange(16)[:, None] == np.arange(1024)[None, :] // 64).astype(np.float32)
    bdm = (np.arange(256)[:, None] // 16 == np.arange(1024)[None, :] // 64).astype(np.float32)
    return jnp.asarray(e_rep), jnp.asarray(e_tile), jnp.asarray(gmask), jnp.asarray(bdm)


def f_s5_params(lam_re, lam_im, log_dt, bt_re, bt_im, ct_re, ct_im, e_rep, e_tile, gmask, bdm):
    dt = jnp.exp(log_dt)
    mag = jnp.exp(lam_re * dt)
    ang = lam_im * dt
    lb_re, lb_im = mag * jnp.cos(ang), mag * jnp.sin(ang)
    num_re, num_im = lb_re - 1.0, lb_im
    den = lam_re * lam_re + lam_im * lam_im
    coef_re = (num_re * lam_re + num_im * lam_im) / den
    coef_im = (num_im * lam_re - num_re * lam_im) / den
    cr, ci = hdot(e_rep, coef_re), hdot(e_rep, coef_im)
    bbt_re = cr * bt_re - ci * bt_im
    bbt_im = cr * bt_im + ci * bt_re
    a_re = jnp.sum(hdot(lb_re, e_tile) * gmask, axis=0, keepdims=True)
    a_im = jnp.sum(hdot(lb_im, e_tile) * gmask, axis=0, keepdims=True)
    return (a_re, a_im, hdot(bbt_re, e_tile) * bdm, hdot(bbt_im, e_tile) * bdm,
            hdot(ct_re, e_tile) * bdm, hdot(ct_im, e_tile) * bdm)


def _s5_doubling(xr, xi, pr, pi, pos, n, steps, rev):
    rows = xr.shape[0]
    for s in steps:
        if rev:
            keep = pos < (n - s)
            sr, si = pltpu.roll(xr, rows - s, 0), pltpu.roll(xi, rows - s, 0)
        else:
            keep = pos >= s
            sr, si = pltpu.roll(xr, s, 0), pltpu.roll(xi, s, 0)
        sr, si = jnp.where(keep, sr, 0.0), jnp.where(keep, si, 0.0)
        xr, xi = xr + pr * sr - pi * si, xi + pr * si + pi * sr
        pr, pi = pr * pr - pi * pi, 2.0 * pr * pi
    return xr, xi, pr, pi


SUBLANES = 8


def _s5_scan(xr, xi, a_re, a_im, rev, chunk, scr):
    xs_r, xs_i, yp_r, yp_i = scr
    ng = chunk // SUBLANES
    x3r, x3i = xr.reshape(ng, SUBLANES, 1024), xi.reshape(ng, SUBLANES, 1024)
    sub = lax.broadcasted_iota(jnp.int32, (SUBLANES, 1024), 0)
    a8r, a8i = a_re, a_im
    for s in (1, 2, 4):
        keep = sub < (SUBLANES - s) if rev else sub >= s
        mr, mi = jnp.where(keep, a8r, 0.0)[None], jnp.where(keep, a8i, 0.0)[None]
        shift = SUBLANES - s if rev else s
        sr, si = pltpu.roll(x3r, shift, 1), pltpu.roll(x3i, shift, 1)
        x3r, x3i = x3r + mr * sr - mi * si, x3i + mr * si + mi * sr
        a8r, a8i = a8r * a8r - a8i * a8i, 2.0 * a8r * a8i
    xr, xi = x3r.reshape(chunk, 1024), x3i.reshape(chunk, 1024)
    nblk = 1024 // 128
    for j in range(nblk):
        xs_r[j] = xr[:, 128 * j:128 * (j + 1)]
        xs_i[j] = xi[:, 128 * j:128 * (j + 1)]
    edge = pl.ds(0 if rev else SUBLANES - 1, ng, stride=SUBLANES)
    gr = jnp.concatenate([xs_r[j, edge, :] for j in range(nblk)], axis=-1)
    gi = jnp.concatenate([xs_i[j, edge, :] for j in range(nblk)], axis=-1)
    grow = lax.broadcasted_iota(jnp.int32, (ng, 1024), 0)
    steps = tuple(1 << k for k in range((ng - 1).bit_length()))
    gr, gi, _, _ = _s5_doubling(gr, gi, a8r, a8i, grow, ng, steps, rev)
    if rev:
        yp_r[...] = jnp.where(grow < ng - 1, pltpu.roll(gr, ng - 1, 0), 0.0)
        yp_i[...] = jnp.where(grow < ng - 1, pltpu.roll(gi, ng - 1, 0), 0.0)
    else:
        yp_r[...] = jnp.where(grow >= 1, pltpu.roll(gr, 1, 0), 0.0)
        yp_i[...] = jnp.where(grow >= 1, pltpu.roll(gi, 1, 0), 0.0)
    sub = lax.broadcasted_iota(jnp.int32, (SUBLANES, 1024), 0)
    tr, ti = jnp.zeros((SUBLANES, 1024), F32), jnp.zeros((SUBLANES, 1024), F32)
    cr, ci = a_re, a_im
    for n in range(1, SUBLANES + 1):
        r = SUBLANES - n if rev else n - 1
        tr, ti = jnp.where(sub == r, cr, tr), jnp.where(sub == r, ci, ti)
        cr, ci = cr * a_re - ci * a_im, cr * a_im + ci * a_re
    for j in range(nblk):
        lanes = slice(128 * j, 128 * (j + 1))
        tr_j, ti_j = tr[:, lanes], ti[:, lanes]
        for g in range(ng):
            rows = slice(g * SUBLANES, (g + 1) * SUBLANES)
            er, ei = yp_r[g:g + 1, lanes], yp_i[g:g + 1, lanes]
            xs_r[j, rows, :] = xs_r[j, rows, :] + tr_j * er - ti_j * ei
            xs_i[j, rows, :] = xs_i[j, rows, :] + tr_j * ei + ti_j * er
    return (jnp.concatenate([xs_r[j] for j in range(nblk)], axis=-1),
            jnp.concatenate([xs_i[j] for j in range(nblk)], axis=-1))


def _s5_scratch(chunk):
    return [pltpu.VMEM((8, chunk, 128), F32), pltpu.VMEM((8, chunk, 128), F32),
            pltpu.VMEM((chunk // SUBLANES, 1024), F32), pltpu.VMEM((chunk // SUBLANES, 1024), F32)]


def _s5_chunk_states(u_c, x0r, x0i, a_re, a_im, bb_re, bb_im, rev, chunk, scr):
    row = lax.broadcasted_iota(jnp.int32, (chunk, 1024), 0)
    first = row == (chunk - 1 if rev else 0)
    inj_r = a_re * x0r - a_im * x0i
    inj_i = a_re * x0i + a_im * x0r
    xr = b_nn(u_c, bb_re) + jnp.where(first, inj_r, 0.0)
    xi = b_nn(u_c, bb_im) + jnp.where(first, inj_i, 0.0)
    return _s5_scan(xr, xi, a_re, a_im, rev, chunk, scr)


def _row_pick(x, idx):
    row = lax.broadcasted_iota(jnp.int32, x.shape, 0)
    return jnp.sum(jnp.where(row == idx, x, 0.0), axis=0, keepdims=True)


def s5_scan_fwd(u, acc, a_re, a_im, bb_re, bb_im, cc_re, cc_im, n_ctx_rows, chunk, rev, name, comm=None):
    t = u.shape[0]
    nch, ncc = t // chunk, n_ctx_rows // chunk

    def body(u_ref, acc_ref, ar_ref, ai_ref, br_ref, bi_ref, cr_ref, ci_ref, y_ref, x0r_ref, x0i_ref, xsr_ref, xsi_ref, *scr):
        a_r, a_i = ar_ref[...], ai_ref[...]

        def step(s, carry):
            x0r, x0i = carry
            c = _gla_chunk_of(s, ncc, nch, rev)
            rows = pl.ds(pl.multiple_of(c * chunk, chunk), chunk)
            x0r_ref[c] = x0r
            x0i_ref[c] = x0i
            xr, xi = _s5_chunk_states(u_ref[rows, :], x0r, x0i, a_r, a_i, br_ref[...], bi_ref[...], rev, chunk, scr)
            y_ref[rows, :] = acc_ref[rows, :] + b_nt(xr, cr_ref[...]) - b_nt(xi, ci_ref[...])
            xsr_ref[rows, :] = xr.astype(BF16)
            xsi_ref[rows, :] = xi.astype(BF16)
            last = 0 if rev else chunk - 1
            return _row_pick(xr, last), _row_pick(xi, last)

        lax.fori_loop(0, nch, step, (jnp.zeros((1, 1024), F32), jnp.zeros((1, 1024), F32)))

    return _call_with_exchange(
        body, name, [u, acc, a_re, a_im, bb_re, bb_im, cc_re, cc_im],
        [jax.ShapeDtypeStruct((t, 256), F32), jax.ShapeDtypeStruct((nch, 1, 1024), F32),
         jax.ShapeDtypeStruct((nch, 1, 1024), F32), jax.ShapeDtypeStruct((t, 1024), BF16),
         jax.ShapeDtypeStruct((t, 1024), BF16)], comm, _s5_scratch(chunk))


def s5_scan_bwd(u, dy, du_acc, x0r, x0i, xsr, xsi, a_re, a_im, bb_re, bb_im, cc_re, cc_im, n_ctx_rows, chunk, rev, name):
    t = u.shape[0]
    nch, ncc = t // chunk, n_ctx_rows // chunk

    def body(u_ref, dy_ref, dua_ref, x0r_ref, x0i_ref, xsr_ref, xsi_ref, ar_ref, ai_ref, br_ref, bi_ref, cr_ref, ci_ref,
             du_ref, dar_ref, dai_ref, dbr_ref, dbi_ref, dcr_ref, dci_ref, *scr):
        a_r, a_i = ar_ref[...], ai_ref[...]
        for ref in (dbr_ref, dbi_ref, dcr_ref, dci_ref):
            ref[...] = jnp.zeros_like(ref)
        row = lax.broadcasted_iota(jnp.int32, (chunk, 1024), 0)
        first_idx, last_idx = (chunk - 1, 0) if rev else (0, chunk - 1)

        def step(j, carry):
            lcr, lci, dar, dai = carry
            s = nch - 1 - j
            c = _gla_chunk_of(s, ncc, nch, rev)
            rows = pl.ds(pl.multiple_of(c * chunk, chunk), chunk)
            u_c, dy_c = u_ref[rows, :], dy_ref[rows, :]
            x0r_c, x0i_c = x0r_ref[c], x0i_ref[c]
            xr, xi = xsr_ref[rows, :].astype(F32), xsi_ref[rows, :].astype(F32)
            dcr_ref[...] += b_tn(dy_c, xr)
            dci_ref[...] -= b_tn(dy_c, xi)
            inj_r = a_r * lcr + a_i * lci
            inj_i = a_r * lci - a_i * lcr
            is_last = row == last_idx
            lr = b_nn(dy_c, cr_ref[...]) + jnp.where(is_last, inj_r, 0.0)
            li = -b_nn(dy_c, ci_ref[...]) + jnp.where(is_last, inj_i, 0.0)
            lr, li = _s5_scan(lr, li, a_r, -a_i, not rev, chunk, scr)
            du_ref[rows, :] = dua_ref[rows, :] + b_nt(lr, br_ref[...]) + b_nt(li, bi_ref[...])
            dbr_ref[...] += b_tn(u_c, lr)
            dbi_ref[...] += b_tn(u_c, li)
            if rev:
                pr, pi = pltpu.roll(xr, chunk - 1, 0), pltpu.roll(xi, chunk - 1, 0)
            else:
                pr, pi = pltpu.roll(xr, 1, 0), pltpu.roll(xi, 1, 0)
            is_first = row == first_idx
            pr, pi = jnp.where(is_first, x0r_c, pr), jnp.where(is_first, x0i_c, pi)
            dar = dar + jnp.sum(lr * pr + li * pi, axis=0, keepdims=True)
            dai = dai + jnp.sum(li * pr - lr * pi, axis=0, keepdims=True)
            return _row_pick(lr, first_idx), _row_pick(li, first_idx), dar, dai

        z = jnp.zeros((1, 1024), F32)
        _, _, dar, dai = lax.fori_loop(0, nch, step, (z, z, z, z))
        dar_ref[...] = dar
        dai_ref[...] = dai

    vm = pl.BlockSpec(memory_space=pltpu.VMEM)
    big = jax.ShapeDtypeStruct((256, 1024), F32)
    vec = jax.ShapeDtypeStruct((1, 1024), F32)
    return pl.pallas_call(
        body, name=name, in_specs=[vm] * 13, out_specs=[vm] * 7,
        out_shape=[jax.ShapeDtypeStruct((t, 256), F32), vec, vec, big, big, big, big],
        scratch_shapes=_s5_scratch(chunk), compiler_params=_cparams(),
    )(u, dy, du_acc, x0r, x0i, xsr, xsi, a_re, a_im, bb_re, bb_im, cc_re, cc_im)


POOL_HALO = 8


def pool_apply(u_pad, n, transpose, name, tile=ROW_TILE):
    tile = min(tile, n)
    ext = tile + 2 * POOL_HALO
    trel = np.arange(ext)[None, :] - POOL_HALO - np.arange(tile)[:, None]
    if transpose:
        trel = -trel
    band4 = np.concatenate([((trel >= -(1 << w)) & (trel <= (1 << w) - 1)) for w in range(4)], axis=0).astype(np.float32)

    def body(u_ref, band_ref, lm_ref, o_ref):
        lax.fori_loop(0, n // tile, functools.partial(step, u_ref, band_ref, lm_ref, o_ref), 0)

    def step(u_ref, band_ref, lm_ref, o_ref, i, carry):
        val = u_ref[pl.ds(pl.multiple_of(i * tile, tile), ext), :]
        lane = lax.broadcasted_iota(jnp.int32, (ext, 256), 1)
        half = jnp.left_shift(1, jnp.right_shift(lane, 6))
        trow = lax.broadcasted_iota(jnp.int32, (ext, 256), 0) + (i * tile - POOL_HALO)
        cnt = jnp.minimum(trow + half, n) - jnp.maximum(trow - half, 0)
        inv = 1.0 / jnp.maximum(cnt, 1).astype(F32)
        src = val * inv if transpose else val
        acc = _fold_heads(mdot(band_ref[...], src), lm_ref)
        centre = val[POOL_HALO:POOL_HALO + tile]
        if not transpose:
            acc = acc * inv[POOL_HALO:POOL_HALO + tile]
        o_ref[pl.ds(pl.multiple_of(i * tile, tile), tile), :] = acc - centre
        return carry

    vm = pl.BlockSpec(memory_space=pltpu.VMEM)
    return pl.pallas_call(
        body, name=name, in_specs=[vm] * 3, out_specs=vm,
        out_shape=jax.ShapeDtypeStruct((n, 256), F32), compiler_params=_cparams(),
    )(u_pad, jnp.asarray(band4), _na_head_masks())


NA_SCALE = 64.0 ** -0.5
NEG = -1e30


def _call_with_exchange(compute, name, args, out_shapes, comm, scratch=()):
    vm = pl.BlockSpec(memory_space=pltpu.VMEM)
    n_in, n_out = len(args), len(out_shapes)
    if comm is None:
        outs = pl.pallas_call(compute, name=name, in_specs=[vm] * n_in, out_specs=[vm] * n_out, out_shape=out_shapes,
                              scratch_shapes=list(scratch), compiler_params=_cparams())(*args)
        return outs, None
    arrays, scatter = comm
    n = len(arrays)

    def body(*refs):
        c_in = refs[n_in:n_in + n]
        c_out = refs[n_in + n + n_out:n_in + 2 * n + n_out]
        scr = refs[n_in + 2 * n + n_out:n_in + 2 * n + n_out + len(scratch)]
        finish = _exchange_issue(c_in, c_out, scatter, *refs[n_in + 2 * n + n_out + len(scratch):])
        compute(*refs[:n_in], *refs[n_in + n:n_in + n + n_out], *scr)
        finish()

    hbm = pl.BlockSpec(memory_space=pl.ANY)
    outs = pl.pallas_call(
        body, name=name, in_specs=[vm] * n_in + [hbm] * n, out_specs=[vm] * n_out + [hbm] * n,
        out_shape=list(out_shapes) + _exchange_out_shapes(arrays, scatter), scratch_shapes=list(scratch) + _exchange_sems(n),
        compiler_params=_cparams(has_side_effects=True),
    )(*args, *arrays)
    return outs[:n_out], outs[n_out:]


def _na_head_masks():
    return jnp.asarray(np.stack([(np.arange(256) // 64 == h) for h in range(4)]).astype(np.float32).reshape(4, 1, 256))


def _na_window(r, rows):
    start = jnp.clip(r - 4, 0, rows - 8)
    return start, start - r + 7


def _na_probs(qh, kw, kc, bias):
    s_c = b_nt(qh, kc)
    m = jnp.max(s_c, axis=-1, keepdims=True)
    if kw is not None:
        s_w = b_nt(qh, kw) + bias
        m = jnp.maximum(m, jnp.max(s_w, axis=-1, keepdims=True))
        p_w = jnp.exp(s_w - m)
    p_c = jnp.exp(s_c - m)
    l = jnp.sum(p_c, axis=-1, keepdims=True)
    if kw is not None:
        l = l + jnp.sum(p_w, axis=-1, keepdims=True)
        return p_w / l, p_c / l
    return None, p_c / l


def na_fwd(q, k, v, bias8, n_ctx_rows, name, comm=None):
    t = q.shape[0]
    m_ctx = n_ctx_rows
    rows = (t - m_ctx) // GRID_W
    hm = _na_head_masks()

    def body(q_ref, k_ref, v_ref, b_ref, hm_ref, o_ref):
        kc, vc = k_ref[0:m_ctx, :], v_ref[0:m_ctx, :]

        def ctx_step(i, _):
            rs = pl.ds(pl.multiple_of(i * 64, 64), 64)
            q4 = _stack_heads(q_ref[rs, :] * NA_SCALE, hm_ref)
            _, p_c = _na_probs(q4, None, kc, None)
            o_ref[rs, :] = _fold_heads(b_nn(p_c, vc), hm_ref)
            return 0

        lax.fori_loop(0, m_ctx // 64, ctx_step, 0)

        def lat_step(r, _):
            start, off = _na_window(r, rows)
            rs = pl.ds(pl.multiple_of(m_ctx + r * 64, 64), 64)
            ws = pl.ds(pl.multiple_of(m_ctx + start * 64, 64), 512)
            q4 = _stack_heads(q_ref[rs, :] * NA_SCALE, hm_ref)
            kw, vw = k_ref[ws, :], v_ref[ws, :]
            p_w, p_c = _na_probs(q4, kw, kc, b_ref[off])
            o_ref[rs, :] = _fold_heads(b_nn(p_w, vw) + b_nn(p_c, vc), hm_ref)
            return 0

        lax.fori_loop(0, rows, lat_step, 0)

    (o,), received = _call_with_exchange(body, name, [q, k, v, bias8, hm], [jax.ShapeDtypeStruct((t, 256), F32)], comm)
    return o if comm is None else (o, received)


def na_bwd(q, k, v, do, bias8, n_ctx_rows, name, comm=None):
    t = q.shape[0]
    m_ctx = n_ctx_rows
    rows = (t - m_ctx) // GRID_W
    hm = _na_head_masks()

    def body(q_ref, k_ref, v_ref, do_ref, b_ref, hm_ref, dq_ref, dk_ref, dv_ref, db_ref):
        kc, vc = k_ref[0:m_ctx, :], v_ref[0:m_ctx, :]
        dk_ref[...] = jnp.zeros_like(dk_ref)
        dv_ref[...] = jnp.zeros_like(dv_ref)
        db_ref[...] = jnp.zeros_like(db_ref)

        def head_terms(qh, doh, kw, vw, bias):
            p_w, p_c = _na_probs(qh, kw, kc, bias)
            dp_c = b_nt(doh, vc)
            delta = jnp.sum(p_c * dp_c, axis=-1, keepdims=True)
            if kw is not None:
                dp_w = b_nt(doh, vw)
                delta = delta + jnp.sum(p_w * dp_w, axis=-1, keepdims=True)
                ds_w = p_w * (dp_w - delta)
            else:
                ds_w = None
            ds_c = p_c * (dp_c - delta)
            return p_w, p_c, ds_w, ds_c

        def ctx_step(i, carry):
            dkc, dvc = carry
            rs = pl.ds(pl.multiple_of(i * 64, 64), 64)
            q4, do4 = _stack_heads(q_ref[rs, :] * NA_SCALE, hm_ref), _stack_heads(do_ref[rs, :], hm_ref)
            _, p_c, _, ds_c = head_terms(q4, do4, None, None, None)
            dq_ref[rs, :] = _fold_heads(b_nn(ds_c, kc), hm_ref) * NA_SCALE
            return dkc + b_tn(ds_c, q4), dvc + b_tn(p_c, do4)

        zc = jnp.zeros((m_ctx, 256), F32)
        carry = lax.fori_loop(0, m_ctx // 64, ctx_step, (zc, zc))

        def lat_step(r, carry):
            dkc, dvc = carry
            start, off = _na_window(r, rows)
            rs = pl.ds(pl.multiple_of(m_ctx + r * 64, 64), 64)
            ws = pl.ds(pl.multiple_of(m_ctx + start * 64, 64), 512)
            q4, do4 = _stack_heads(q_ref[rs, :] * NA_SCALE, hm_ref), _stack_heads(do_ref[rs, :], hm_ref)
            kw, vw = k_ref[ws, :], v_ref[ws, :]
            p_w, p_c, ds_w, ds_c = head_terms(q4, do4, kw, vw, b_ref[off])
            dq_ref[rs, :] = _fold_heads(b_nn(ds_w, kw) + b_nn(ds_c, kc), hm_ref) * NA_SCALE
            dk_ref[ws, :] += b_tn(ds_w, q4)
            dv_ref[ws, :] += b_tn(p_w, do4)
            db_ref[off] += ds_w
            return dkc + b_tn(ds_c, q4), dvc + b_tn(p_c, do4)

        dkc, dvc = lax.fori_loop(0, rows, lat_step, carry)
        dk_ref[0:m_ctx, :] = dkc
        dv_ref[0:m_ctx, :] = dvc

    row = jax.ShapeDtypeStruct((t, 256), F32)
    return _call_with_exchange(body, name, [q, k, v, do, bias8, hm], [row, row, row, jax.ShapeDtypeStruct(bias8.shape, F32)], comm)


def _na_toeplitz():
    col = np.arange(GRID_W)
    dd = (col[None, :] - col[:, None] + 15).reshape(-1)
    tt = np.zeros((GRID_W * GRID_W, 128), np.float32)
    ok = (dd >= 0) & (dd <= 30)
    tt[np.arange(GRID_W * GRID_W)[ok], dd[ok]] = 1.0
    return tt


def _na_bias8(rpb, name):
    col = np.arange(GRID_W)
    cs = np.clip(col - 8, 0, GRID_W - 16)
    col_mask = (col[None, :] >= cs[:, None]) & (col[None, :] < cs[:, None] + 16)
    rpb2 = jnp.pad(rpb.reshape(60, 31), ((0, 4), (0, 97)))
    (toe,) = whole_fwd(lambda r_, t_: (hdot_nt(r_, t_),), name, [rpb2, jnp.asarray(_na_toeplitz())], [(64, GRID_W * GRID_W)])
    toe = toe[:60].reshape(4, 15, GRID_W, GRID_W)
    b = jnp.stack([toe[:, off:off + 8] for off in range(8)], axis=1)
    b = jnp.where(jnp.asarray(col_mask)[None, None, None], b, NEG)
    return b.transpose(1, 0, 3, 2, 4).reshape(8, 4 * GRID_W, 8 * GRID_W)


def _na_rpb_grad(dbias8, name):
    tt = _na_toeplitz()
    sel = np.zeros((64, 256), np.float32)
    for h in range(4):
        for off in range(8):
            for i in range(8):
                sel[h * 15 + off + i, h * 64 + off * 8 + i] = 1.0
    a2 = dbias8.reshape(8, 4, GRID_W, 8, GRID_W).transpose(1, 0, 3, 2, 4).reshape(256, GRID_W * GRID_W)
    (out,) = whole_fwd(lambda a, t_, s_: (hdot(s_, hdot(a, t_)),), name, [a2, jnp.asarray(tt), jnp.asarray(sel)], [(64, 128)])
    return out[:60, :31].reshape(4, 15, 31)


def f_mod(cs, b_mod, w_mod):
    s = _silu(cs)
    return bdot(s, w_mod) + b_mod, s


def loss_and_grad(z, tgt, n_ctx_rows, name, tile=ROW_TILE):
    t, d = z.shape
    tile = min(tile, n_ctx_rows)
    nct = n_ctx_rows // tile

    def body(z_ref, t_ref, dz_ref, loss_ref):
        i = pl.program_id(0)

        @pl.when(i == 0)
        def _():
            loss_ref[...] = jnp.zeros_like(loss_ref)

        @pl.when(i < nct)
        def _():
            dz_ref[...] = jnp.zeros_like(dz_ref)

        @pl.when(i >= nct)
        def _():
            diff = z_ref[...] - t_ref[...]
            dz_ref[...] = diff * (1.0 / d)
            loss_ref[...] += 0.5 * jnp.sum(jnp.sum(diff * diff, axis=-1, keepdims=True) * (1.0 / d), axis=0, keepdims=True)

    dz, loss = pl.pallas_call(
        body, name=name, grid=(t // tile,),
        in_specs=[pl.BlockSpec((tile, d), lambda i: (i, 0)),
                  pl.BlockSpec((tile, d), lambda i: (jnp.maximum(i - nct, 0), 0))],
        out_specs=[pl.BlockSpec((tile, d), lambda i: (i, 0)), pl.BlockSpec((8, 128), lambda i: (0, 0))],
        out_shape=[jax.ShapeDtypeStruct((t, d), F32), jax.ShapeDtypeStruct((8, 128), F32)],
        compiler_params=_cparams(dimension_semantics=("arbitrary",)),
    )(z, tgt)
    return loss[0, 0], dz


def adamw(parts, w, m, v, name, tile=256):
    npart, r, c = parts.shape
    tile = min(tile, r)
    assert r % tile == 0
    c1 = 1.0 / (1.0 - ADAM_B1 ** ADAM_STEP)
    c2 = 1.0 / (1.0 - ADAM_B2 ** ADAM_STEP)

    def body(p_ref, w_ref, m_ref, v_ref, g_ref, d_ref, nm_ref, nv_ref):
        g = p_ref[0].astype(F32)
        for i in range(1, npart):
            g = g + p_ref[i].astype(F32)
        nm = ADAM_B1 * m_ref[...] + (1.0 - ADAM_B1) * g
        nv = ADAM_B2 * v_ref[...] + (1.0 - ADAM_B2) * (g * g)
        g_ref[...] = g
        nm_ref[...] = nm
        nv_ref[...] = nv
        d_ref[...] = -ADAM_LR * ((nm * c1) / (jnp.sqrt(nv * c2) + ADAM_EPS) + ADAM_WD * w_ref[...])

    blk = pl.BlockSpec((tile, c), lambda i: (i, 0))
    return pl.pallas_call(
        body, name=name, grid=(r // tile,),
        in_specs=[pl.BlockSpec((npart, tile, c), lambda i: (0, i, 0)), blk, blk, blk],
        out_specs=[blk] * 4, out_shape=[jax.ShapeDtypeStruct((r, c), F32)] * 4,
        compiler_params=_cparams(dimension_semantics=("arbitrary",)),
    )(parts, w, m, v)


def _peer(x, y, c, k):
    return (1 - x if k & 4 else x, 1 - y if k & 2 else y, 1 - c if k & 1 else c)


def _exchange_out_shapes(arrays, scatter):
    return [jax.ShapeDtypeStruct(a.shape if s else (N_DEV,) + a.shape, a.dtype) for a, s in zip(arrays, scatter)]


def _exchange_sems(n):
    return [pltpu.SemaphoreType.DMA((n, N_DEV - 1)), pltpu.SemaphoreType.DMA((n, N_DEV - 1)), pltpu.SemaphoreType.DMA((n,))]


def _exchange_issue(ins, outs, scatter, send_sems, recv_sems, local_sems):
    n = len(ins)
    x, y, c = lax.axis_index("x"), lax.axis_index("y"), lax.axis_index("c")
    me = 4 * x + 2 * y + c

    def index_of(p):
        return 4 * p[0] + 2 * p[1] + p[2]

    local = []
    for a in range(n):
        src_me = ins[a].at[me] if scatter[a] else ins[a]
        loc = pltpu.make_async_copy(src_me, outs[a].at[me], local_sems.at[a])
        loc.start()
        local.append(loc)
    for k in range(1, N_DEV):
        peer = _peer(x, y, c, k)
        for a in range(n):
            src = ins[a].at[index_of(peer)] if scatter[a] else ins[a]
            pltpu.make_async_remote_copy(
                src_ref=src, dst_ref=outs[a].at[me], send_sem=send_sems.at[a, k - 1], recv_sem=recv_sems.at[a, k - 1],
                device_id=peer, device_id_type=pl.DeviceIdType.MESH).start()

    def finish():
        for k in range(1, N_DEV):
            peer = _peer(x, y, c, k)
            for a in range(n):
                src = ins[a].at[index_of(peer)] if scatter[a] else ins[a]
                cp = pltpu.make_async_remote_copy(
                    src_ref=src, dst_ref=outs[a].at[index_of(peer)], send_sem=send_sems.at[a, k - 1],
                    recv_sem=recv_sems.at[a, k - 1], device_id=peer, device_id_type=pl.DeviceIdType.MESH)
                cp.wait_send()
                cp.wait_recv()
        for loc in local:
            loc.wait()

    return finish


def gather_two_level(arrays, name):
    n = len(arrays)

    def body(*refs):
        ins, outs = refs[:n], refs[n:2 * n]
        send_sems, recv_sems, local_sems = refs[2 * n:]
        x, y, c = lax.axis_index("x"), lax.axis_index("y"), lax.axis_index("c")
        sibling = (x, y, 1 - c)
        chips = [(1 - x, y), (x, 1 - y), (1 - x, 1 - y)]

        def slot(a, p):
            return outs[a].at[4 * p[0] + 2 * p[1] + p[2]]

        def copy(a, k, src, block, to):
            return pltpu.make_async_remote_copy(src_ref=src, dst_ref=slot(a, block), send_sem=send_sems.at[a, k],
                                                recv_sem=recv_sems.at[a, k], device_id=to, device_id_type=pl.DeviceIdType.MESH)

        me = (x, y, c)
        started, local = [], []
        for a in range(n):
            loc = pltpu.make_async_copy(ins[a], slot(a, me), local_sems.at[a])
            loc.start()
            local.append(loc)
            first = [copy(a, 0, ins[a], me, sibling)] + [copy(a, 1 + j, ins[a], me, (*chip, c)) for j, chip in enumerate(chips)]
            for cp in first:
                cp.start()
            started += first
        for j, chip in enumerate(chips):
            for a in range(n):
                copy(a, 1 + j, ins[a], (*chip, c), me).wait_recv()
                fwd = copy(a, 4 + j, slot(a, (*chip, c)), (*chip, c), sibling)
                fwd.start()
                started.append(fwd)
        for a in range(n):
            copy(a, 0, ins[a], sibling, me).wait_recv()
            for j, chip in enumerate(chips):
                copy(a, 4 + j, ins[a], (*chip, 1 - c), me).wait_recv()
        for cp in started:
            cp.wait_send()
        for loc in local:
            loc.wait()

    hbm = pl.BlockSpec(memory_space=pl.ANY)
    return pl.pallas_call(
        body, name=name, in_specs=[hbm] * n, out_specs=[hbm] * n, out_shape=_exchange_out_shapes(arrays, [False] * n),
        scratch_shapes=_exchange_sems(n), compiler_params=pltpu.CompilerParams(has_side_effects=True),
    )(*arrays)


def exchange(arrays, scatter, name):
    n = len(arrays)

    def body(*refs):
        _exchange_issue(refs[:n], refs[n:2 * n], scatter, *refs[2 * n:])()

    hbm = pl.BlockSpec(memory_space=pl.ANY)
    return pl.pallas_call(
        body, name=name, in_specs=[hbm] * n, out_specs=[hbm] * n, out_shape=_exchange_out_shapes(arrays, scatter),
        scratch_shapes=_exchange_sems(n), compiler_params=pltpu.CompilerParams(has_side_effects=True),
    )(*arrays)


def _rope_tables(n_lat, n_ctx):
    tok = np.arange(n_lat)
    freqs = 10000.0 ** (-np.arange(0, 16, 2, dtype=np.float32) / 16.0)

    def table(pos):
        ang = pos.astype(np.float32)[:, None] * freqs[None, :]
        ang = np.concatenate([ang, ang], axis=-1)
        return np.cos(ang), np.sin(ang)

    cr, sr = table(tok // GRID_W)
    cc, sc = table(tok % GRID_W)
    cos = np.tile(np.concatenate([cr, cc], axis=-1), (1, 4))
    sin = np.tile(np.concatenate([sr, sc], axis=-1), (1, 4))
    cos = np.concatenate([np.ones((n_ctx, 128), np.float32), cos], axis=0)
    sin = np.concatenate([np.zeros((n_ctx, 128), np.float32), sin], axis=0)
    return jnp.asarray(cos, F32), jnp.asarray(sin, F32)


def _pad_w_in(w):
    z = lambda n: jnp.zeros((w.shape[0], n), w.dtype)
    return jnp.concatenate([w[:, 1824:2848], w[:, 128:384], w[:, 416:672], w[:, 672:928], w[:, 928:1184], w[:, 1312:1568],
                            w[:, 1568:1824], w[:, 0:128], w[:, 384:416], z(96), w[:, 1184:1312], z(128)], axis=1)


def _unpad_w_in(wp):
    return jnp.concatenate([wp[:, C_GK:C_GK + 128], wp[:, C_GV:C_GV + 256], wp[:, C_GG:C_GG + 32], wp[:, C_NK:C_NK + 256],
                            wp[:, C_NV:C_NV + 256], wp[:, C_SU:C_SU + 256], wp[:, C_GQ:C_GQ + 128], wp[:, C_NQ:C_NQ + 256],
                            wp[:, C_PU:C_PU + 256], wp[:, C_GT:C_GT + 1024]], axis=1)


_W_IN_SEGS = [(0, 128, C_GK), (128, 256, C_GV), (384, 32, C_GG), (416, 256, C_NK), (672, 256, C_NV), (928, 256, C_SU),
              (1184, 128, C_GQ), (1312, 256, C_NQ), (1568, 256, C_PU), (1824, 1024, C_GT)]
W_IN_SHARD = N_IN // N_DEV


def _pad_w_in_blocks(blocks):
    pieces = []
    for orig, width, padded in _W_IN_SEGS:
        col = orig
        while col < orig + width:
            dev, lo = divmod(col, W_IN_SHARD)
            n = min(W_IN_SHARD - lo, orig + width - col)
            pieces.append((padded + col - orig, blocks[dev][:, lo:lo + n]))
            col += n
    pieces.sort(key=lambda p: p[0])
    out, at = [], 0
    for start, piece in pieces:
        if start > at:
            out.append(jnp.zeros((blocks.shape[1], start - at), blocks.dtype))
        out.append(piece)
        at = start + piece.shape[1]
    out.append(jnp.zeros((blocks.shape[1], PW - at), blocks.dtype))
    return jnp.concatenate(out, axis=1)


def _w_in_slabs(wp_blocks):
    slabs = []
    for dev in range(N_DEV):
        first, pieces = dev * W_IN_SHARD, []
        for orig, width, padded in _W_IN_SEGS:
            lo, hi = max(orig, first), min(orig + width, first + W_IN_SHARD)
            if lo < hi:
                a = padded + lo - orig
                blk, off = divmod(a, 1024)
                assert off + (hi - lo) <= 1024
                pieces.append(wp_blocks[blk][:, off:off + hi - lo])
        slabs.append(jnp.concatenate(pieces, axis=1))
    return jnp.stack(slabs)


def _pad_rows(u):
    return jnp.pad(u, ((POOL_HALO, POOL_HALO), (0, 0)))


def _block_diag4(w):
    out = jnp.zeros((256, 256), w.dtype)
    for i in range(4):
        out = lax.dynamic_update_slice(out, w[i], (64 * i, 64 * i))
    return out


def _layer_params(p, big, l):
    e_rep, e_tile, gmask, bdm = _s5_consts()
    wg = jnp.zeros((128, 256), F32)
    wg = lax.dynamic_update_slice(wg, p["gla_w_gate"][l, 0], (0, 0))
    wg = lax.dynamic_update_slice(wg, p["gla_w_gate"][l, 1], (16, 128))
    s5 = []
    for d in range(2):
        s5.append([p["s5_lam_re"][l, d], p["s5_lam_im"][l, d], p["s5_log_dt"][l, d].reshape(16, 1),
                   p["s5_b_re"][l, d].transpose(0, 2, 1).reshape(256, 64), p["s5_b_im"][l, d].transpose(0, 2, 1).reshape(256, 64),
                   p["s5_c_re"][l, d].reshape(256, 64), p["s5_c_im"][l, d].reshape(256, 64), e_rep, e_tile, gmask, bdm])
    havg = jnp.asarray((np.arange(256)[:, None] // 64 == np.arange(256)[None, :] // 64).astype(np.float32) / 64.0)
    e4 = jnp.asarray((np.arange(64)[:, None] == np.arange(256)[None, :] % 64).astype(np.float32))
    return dict(
        g_pre=p["g_pre"][l].reshape(1, D), g_post=p["g_post"][l].reshape(1, D), b_mod=p["b_mod"][l].reshape(1, 3 * D),
        w_mod=big["w_mod"], w_in=_pad_w_in_blocks(big["w_in_blocks"]) if "w_in_blocks" in big else _pad_w_in(big["w_in"]), w_out=big.get("w_out"),
        wg=wg, bg=p["gla_b_gate"][l].reshape(1, 256), g_norm=jnp.pad(p["gla_g_norm"][l].reshape(1, 64), ((0, 7), (0, 0))),
        bias8=_na_bias8(p["na_rpb"][l], f"na_bias_l{l}"), s5=s5, s5_d=p["s5_d"][l].reshape(1, 256), w_glu=None if big.get("s5_w_glu") is None else big["s5_w_glu"].astype(F32),
        b_glu=p["s5_b_glu"][l].reshape(1, 256), wpool=_block_diag4(p["pool_w"][l]), pool_scale=p["pool_scale"][l].reshape(1, 256),
        havg=havg, e4=e4)


def _matmul_tile(t, tile, steps):
    return t // steps if t % (8 * steps) == 0 else tile


def _cols(pz, start, width):
    return pz[:, start:start + width]


def _layer_fwd(z, modseg, lp, cos, sin, m_ctx, tile, s5_chunk, l, comm=None, late=None):
    t = z.shape[0]
    nct = m_ctx // tile
    nm = lambda s: f"{s}_l{l}"
    (h,) = rowwise_fwd(f_pre, nm("pre"), [z], [modseg], [lp["g_pre"]], [D], tile, nct)
    mm_tile = _matmul_tile(t, tile, 4)
    (gt,) = mm_nn_cols(h, lp["w_in"], C_GT, [1024], nm("in_proj_a"), tm=mm_tile)
    pv, nk, nv, su = mm_nn_cols(h, lp["w_in"], C_GV, [256] * 4, nm("in_proj_b"), tm=mm_tile)
    nq, pu, pk, pg, pq = mm_nn_cols(h, lp["w_in"], C_NQ, [256, 256, 128, 128, 128], nm("in_proj_c"), tm=mm_tile)
    q_r, k_r, lgf, lgb = rowwise_fwd(f_gla_prep, nm("gla_prep"), [pk, pg, pq, cos, sin], [], [lp["wg"], lp["bg"]], [128] * 4, tile, nct)
    half = None if comm is None else comm[0].shape[0] // 2
    spread = None if comm is None else [comm[0][:half], comm[1], comm[2], comm[3], comm[0][half:]]
    part = (lambda idx: None) if comm is None else (lambda idx: ([spread[i] for i in idx], [False] * len(idx)))
    (o1, st_f), got_late = gla_scan_fwd(q_r, k_r, pv, lgf, jnp.zeros((t, 256), F32), m_ctx, False, nm("gla_f"),
                                        None if late is None else (list(late), [False, False]))
    if late is not None:
        lp["w_out"], lp["w_glu"] = _gathered(got_late[0], False), _gathered(got_late[1], False).astype(F32)
    (o_gla, st_b), got_out = gla_scan_fwd(q_r, k_r, pv, lgb, o1, m_ctx, True, nm("gla_r"), part([2, 3]))
    received = None
    if comm is None:
        o_na = na_fwd(nq, nk, nv, lp["bias8"], m_ctx, nm("na"))
    else:
        o_na, got_in = na_fwd(nq, nk, nv, lp["bias8"], m_ctx, nm("na"), part([1]))
    s5p = [whole_fwd(f_s5_params, nm(f"s5_par{d}"), lp["s5"][d], [(1, 1024)] * 2 + [(256, 1024)] * 4) for d in range(2)]
    (y1, *states_f), got_mod_a = s5_scan_fwd(su, jnp.zeros((t, 256), F32), *s5p[0], m_ctx, s5_chunk, False, nm("s5_f"), part([0]))
    (y5, *states_b), got_mod_b = s5_scan_fwd(su, y1, *s5p[1], m_ctx, s5_chunk, True, nm("s5_r"), part([4]))
    if comm is not None:
        received = [jnp.concatenate([got_mod_a[0], got_mod_b[0]], axis=1), got_in[0], got_out[0], got_out[1]]
    pm = jnp.concatenate([pool_apply(_pad_rows(pu[:m_ctx]), m_ctx, False, nm("pool_c")),
                          pool_apply(_pad_rows(pu[m_ctx:]), t - m_ctx, False, nm("pool_x"))], axis=0)
    mix_rows = [o_gla, o_na, y5, su, pm, gt]
    mix_globs = [lp["g_norm"], lp["s5_d"], lp["w_glu"], lp["b_glu"], lp["wpool"], lp["pool_scale"], lp["havg"], lp["e4"]]
    (yg,) = rowwise_fwd(f_mix, nm("mix"), mix_rows, [], mix_globs, [D], tile, nct)
    out = mm_nn([yg], lp["w_out"], nm("out_proj"), tm=mm_tile)
    (z_new,) = rowwise_fwd(f_post, nm("post"), [z, out], [modseg], [lp["g_post"]], [D], tile, nct)
    saved = dict(z=z, h=h, pv=pv, nk=nk, nv=nv, su=su, nq=nq, pk=pk, pg=pg, pq=pq, q_r=q_r, k_r=k_r, lgf=lgf, lgb=lgb,
                 st_f=st_f, st_b=st_b, s5p=s5p, x0f=tuple(states_f), x0b=tuple(states_b), mix_rows=mix_rows, mix_globs=mix_globs,
                 yg=yg, out=out)
    return z_new, saved, received


def _f_pre_res(x, mod, g_pre):
    return f_pre(x, mod, g_pre)[0], x


def _layer_bwd(dz_new, sv, modseg, lp, cos, sin, m_ctx, tile, s5_chunk, l, comm=None, gdt=F32, send_early=False, as_slabs=False):
    t = dz_new.shape[0]
    nct = m_ctx // tile
    nm = lambda s: f"{s}_l{l}"
    g = {}
    dz_res, dout, dmod_post, g["g_post"] = rowwise_bwd(f_post, nm("post_b"), [sv["z"], sv["out"]], [modseg], [lp["g_post"]],
                                                       [dz_new], tile, nct, [True, True], [True])
    dyg = mm_nt([dout], lp["w_out"], nm("out_proj_dx"), tm=_matmul_tile(t, tile, 4))
    dw_tile = _matmul_tile(t, tile, 4)
    (g["w_out"],) = mm_tn(sv["yg"], [dout], nm("out_proj_dw"), tm=dw_tile, out_dtype=gdt)
    res = rowwise_bwd(f_mix, nm("mix_b"), sv["mix_rows"], [], sv["mix_globs"], [dyg], tile, nct, [True] * 6, [True] * 6 + [False] * 2)
    do_gla, do_na, dy5, dsu_a, dpm, dgt = res[:6]
    g["g_norm"], g["s5_d"], g["w_glu"], g["b_glu"], g["wpool"], g["pool_scale"] = res[6:]
    dpu = jnp.concatenate([pool_apply(_pad_rows(dpm[:m_ctx]), m_ctx, True, nm("pool_c_b")),
                           pool_apply(_pad_rows(dpm[m_ctx:]), t - m_ctx, True, nm("pool_x_b"))], axis=0)
    r_b = s5_scan_bwd(sv["su"], dy5, dsu_a, *sv["x0b"], *sv["s5p"][1], m_ctx, s5_chunk, True, nm("s5_r_b"))
    r_f = s5_scan_bwd(sv["su"], dy5, r_b[0], *sv["x0f"], *sv["s5p"][0], m_ctx, s5_chunk, False, nm("s5_f_b"))
    dsu = r_f[0]
    g["s5"] = [whole_bwd(f_s5_params, nm(f"s5_par{d}_b"), lp["s5"][d], list(r[1:]), [True] * 7 + [False] * 4)
               for d, r in ((0, r_f), (1, r_b))]
    part = (lambda idx: None) if comm is None else (lambda idx: ([comm[i] for i in idx], [True] * len(idx)))
    (dnq, dnk, dnv, dbias8), got_in = na_bwd(sv["nq"], sv["nk"], sv["nv"], do_na, lp["bias8"], m_ctx, nm("na_b"), part([0]))
    g["rpb"] = _na_rpb_grad(dbias8, nm("na_rpb_b"))
    zq, zv = jnp.zeros((t, 128), F32), jnp.zeros((t, 256), F32)
    early = ([_slabs(g["w_out"], False).astype(BF16), _slabs(g["w_glu"], False).astype(BF16)], [True, True]) if send_early else None
    (dq1, dk1, dv1, dlgb), g["early"] = gla_scan_bwd(sv["q_r"], sv["k_r"], sv["pv"], sv["lgb"], sv["st_b"], do_gla, (zq, zq, zv), m_ctx, True,
                                                     nm("gla_r_b"), early)
    (dq_r, dk_r, dpv, dlgf), got_out = gla_scan_bwd(sv["q_r"], sv["k_r"], sv["pv"], sv["lgf"], sv["st_f"], do_gla, (dq1, dk1, dv1), m_ctx, False,
                                                    nm("gla_f_b"), part([1, 2]))
    received = None if comm is None else [got_in[0], got_out[0], got_out[1]]
    dpk, dpg, dpq, g["wg"], g["bg"] = rowwise_bwd(f_gla_prep, nm("gla_prep_b"), [sv["pk"], sv["pg"], sv["pq"], cos, sin], [],
                                                  [lp["wg"], lp["bg"]], [dq_r, dk_r, dlgf, dlgb], tile, nct,
                                                  [True, True, True, False, False], [True, True])
    parts = [dgt, dpv, dnk, dnv, dsu, dnq, dpu, dpk, dpg, dpq, jnp.zeros((t, 128), F32)]
    dh = mm_nt(parts, lp["w_in"], nm("in_proj_dx"), tm=_matmul_tile(t, tile, 8))
    dw_blocks = mm_tn(sv["h"], parts, nm("in_proj_dw"), tm=dw_tile, out_dtype=gdt)
    if as_slabs:
        g["w_in_slabs"] = _w_in_slabs(dw_blocks)
    else:
        g["w_in"] = _unpad_w_in(jnp.concatenate(dw_blocks, axis=1))
    dz, dmod_pre, g["g_pre"] = rowwise_bwd(_f_pre_res, nm("pre_b"), [sv["z"]], [modseg], [lp["g_pre"]], [dh, dz_res], tile, nct, [True], [True])
    return dz, dmod_pre, dmod_post, g, received


def _f_mod_sum(cs, b_mod, w_mod):
    mod, _ = f_mod(cs, b_mod, w_mod)
    return mod, cs


def local_step(x, c, ctx, tgt, p, shards=None, tile=ROW_TILE, s5_chunk=S5_CHUNK):
    n_lat, m_ctx = x.shape[0], ctx.shape[0]
    n_layers = p["g_pre"].shape[0]
    z = jnp.concatenate([ctx, x], axis=0)
    cos, sin = _rope_tables(n_lat, m_ctx)
    cs = jnp.concatenate([c.reshape(1, D), p["c_ctx"].reshape(1, D), jnp.zeros((6, D), F32)], axis=0)
    gather = [False] * len(_SHARDED)
    lps, mods, silus, saves = [], [], [], []
    got = gather_two_level(shards[0][:2], "gather_weights_l0") if shards is not None else None
    for l in range(n_layers):
        if shards is None:
            big = {n: p[n][l] for n in _SHARDED}
        else:
            big = {n: _gathered(g, _BY_COLS[n]) for n, g in zip(_SHARDED, got) if n != "w_in"}
            big["w_in_blocks"] = got[_SHARDED.index("w_in")]
        lp = _layer_params(p, big, l)
        mod8, s8 = whole_fwd(f_mod, f"mod_l{l}", [cs, lp["b_mod"], lp["w_mod"]], [(8, 3 * D), (8, D)])
        modseg = mod8[:2].reshape(2, 1, 3 * D)
        comm = shards[l + 1] if shards is not None and l + 1 < n_layers else None
        late = shards[0][2:] if shards is not None and l == 0 else None
        z, sv, got = _layer_fwd(z, modseg, lp, cos, sin, m_ctx, tile, s5_chunk, l, comm, late)
        lps.append(lp); mods.append(modseg); silus.append(s8); saves.append(sv)
    loss, dz = loss_and_grad(z, tgt, m_ctx, "loss", tile)
    grads, received = [None] * n_layers, [None] * n_layers
    gdt = F32 if shards is None else BF16
    dcs = jnp.zeros((8, D), F32)
    pending = None
    for l in reversed(range(n_layers)):
        lp = lps[l]
        dz, dmod_pre, dmod_post, g, got = _layer_bwd(dz, saves[l], mods[l], lp, cos, sin, m_ctx, tile, s5_chunk, l, pending, gdt,
                                                     send_early=shards is not None and l == 0, as_slabs=shards is not None)
        if pending is not None:
            received[l + 1] = got
        dmod = jnp.concatenate([dmod_pre.reshape(2, 3 * D)[:, :2 * D], dmod_post.reshape(2, 3 * D)[:, 2 * D:]], axis=1)
        dmod8 = jnp.pad(dmod, ((0, 6), (0, 0)))
        dcs, g["b_mod"] = whole_bwd(_f_mod_sum, f"mod_b_l{l}", [cs, lp["b_mod"], lp["w_mod"]], [dmod8, dcs], [True, True, False])
        if shards is None:
            g["w_mod"] = jnp.concatenate(mm_tn(silus[l], [dmod8[:, :D], dmod8[:, D:2 * D], dmod8[:, 2 * D:]], f"mod_dw_l{l}", tm=8), axis=1)
        else:
            g["mod_s"], g["mod_d"] = silus[l][:2], dmod
        grads[l] = g
        if shards is not None:
            pending = _layer_sends(g)
    if shards is not None:
        got_in, got_small = exchange([pending[0], _small_sends(dcs[1], grads)], [True, False], "exchange_grads_l0")
        received[0] = [got_in] + list(grads[0]["early"]) + [got_small]
    return loss, dz[m_ctx:], dcs[1], grads, received


_WEIGHTS = ["c_ctx", "w_mod", "b_mod", "g_pre", "g_post", "w_in", "w_out", "gla_w_gate", "gla_b_gate", "gla_g_norm", "na_rpb",
            "s5_lam_re", "s5_lam_im", "s5_log_dt", "s5_b_re", "s5_b_im", "s5_c_re", "s5_c_im", "s5_d", "s5_w_glu", "s5_b_glu",
            "pool_w", "pool_scale"]
_INPUTS = ["x", "c", "ctx"] + _WEIGHTS + ["loss_target"] + ["m_" + n for n in _WEIGHTS] + ["v_" + n for n in _WEIGHTS]
_SHARDED = ["w_mod", "w_in", "w_out", "s5_w_glu"]
_BY_COLS = {"w_mod": True, "w_in": True, "w_out": False, "s5_w_glu": False}
_GRAD_SHARDED = ["w_in", "w_out", "s5_w_glu"]
_SMALL = [n for n in _WEIGHTS if n not in _SHARDED]
_SMALL_PER_LAYER = [n for n in _SMALL if n != "c_ctx"]
_PACK_ROWS = 256


def _pack_plan(like):
    tiled = [i for i, a in enumerate(like) if a.size % 1024 == 0]
    loose = [i for i, a in enumerate(like) if a.size % 1024 != 0]
    tail = -(-sum(like[i].size for i in loose) // 1024) * 8
    rows = sum(like[i].size // 128 for i in tiled) + tail
    return tiled, loose, tail, -(-rows // _PACK_ROWS) * _PACK_ROWS - rows


def _pack_rows(like, index):
    tiled, _, _, _ = _pack_plan(like)
    row = 0
    for i in tiled:
        n = like[i].size // 128
        if i == index:
            return row, row + n
        row += n
    raise ValueError("not a tile-aligned entry")


def _pack(arrs):
    tiled, loose, tail, fill = _pack_plan(arrs)
    dt = arrs[0].dtype
    flat = jnp.concatenate([arrs[i].reshape(-1) for i in loose])
    flat = jnp.pad(flat, (0, tail * 128 - flat.shape[0])).reshape(tail, 128)
    return jnp.concatenate([arrs[i].reshape(-1, 128) for i in tiled] + [flat, jnp.zeros((fill, 128), dt)], axis=0)


def _unpack(packed, like):
    tiled, loose, tail, _ = _pack_plan(like)
    out, row = [None] * len(like), 0
    for i in tiled:
        n = like[i].size // 128
        out[i] = packed[row:row + n].reshape(like[i].shape)
        row += n
    flat, pos = packed[row:row + tail].reshape(-1), 0
    for i in loose:
        out[i] = flat[pos:pos + like[i].size].reshape(like[i].shape)
        pos += like[i].size
    return out


def _gathered(g, cols):
    if cols:
        return g.transpose(1, 0, 2).reshape(g.shape[1], N_DEV * g.shape[2])
    return g.reshape(N_DEV * g.shape[1], g.shape[2])


def _slabs(w, cols):
    r, c = w.shape
    if cols:
        return w.reshape(r, N_DEV, c // N_DEV).transpose(1, 0, 2)
    return w.reshape(N_DEV, r // N_DEV, c)


def _layer_small(g):
    s5 = lambda i, f: jnp.stack([f(g["s5"][d][i]) for d in range(2)])
    return {
        "b_mod": g["b_mod"].reshape(3 * D), "g_pre": g["g_pre"].reshape(D), "g_post": g["g_post"].reshape(D),
        "gla_w_gate": jnp.stack([g["wg"][0:16, 0:128], g["wg"][16:32, 128:256]]),
        "gla_b_gate": g["bg"].reshape(2, 128), "gla_g_norm": g["g_norm"][0], "na_rpb": g["rpb"],
        "s5_lam_re": s5(0, lambda a: a), "s5_lam_im": s5(1, lambda a: a), "s5_log_dt": s5(2, lambda a: a.reshape(16)),
        "s5_b_re": s5(3, lambda a: a.reshape(16, 16, 64).transpose(0, 2, 1)),
        "s5_b_im": s5(4, lambda a: a.reshape(16, 16, 64).transpose(0, 2, 1)),
        "s5_c_re": s5(5, lambda a: a.reshape(16, 16, 64)), "s5_c_im": s5(6, lambda a: a.reshape(16, 16, 64)),
        "s5_d": g["s5_d"].reshape(256), "s5_b_glu": g["b_glu"].reshape(256),
        "pool_w": jnp.stack([g["wpool"][64 * i:64 * i + 64, 64 * i:64 * i + 64] for i in range(4)]),
        "pool_scale": g["pool_scale"].reshape(256),
    }


def _layer_sends(g):
    return [g["w_in_slabs"].astype(BF16), _slabs(g["w_out"], False).astype(BF16), _slabs(g["w_glu"], False).astype(BF16)]


def _small_sends(d_c_ctx, grads):
    per_layer = [_layer_small(g) for g in grads]
    full = {n: jnp.stack([s[n] for s in per_layer]) for n in _SMALL_PER_LAYER}
    full["c_ctx"] = d_c_ctx
    factors = [jnp.stack([g["mod_s"] for g in grads]), jnp.stack([g["mod_d"] for g in grads])]
    return _pack([full[n] for n in _SMALL] + factors).astype(BF16)


def kernel(x, c, ctx, c_ctx, w_mod, b_mod, g_pre, g_post, w_in, w_out, gla_w_gate, gla_b_gate, gla_g_norm, na_rpb, s5_lam_re, s5_lam_im, s5_log_dt, s5_b_re, s5_b_im, s5_c_re, s5_c_im, s5_d, s5_w_glu, s5_b_glu, pool_w, pool_scale, loss_target, m_c_ctx, m_w_mod, m_b_mod, m_g_pre, m_g_post, m_w_in, m_w_out, m_gla_w_gate, m_gla_b_gate, m_gla_g_norm, m_na_rpb, m_s5_lam_re, m_s5_lam_im, m_s5_log_dt, m_s5_b_re, m_s5_b_im, m_s5_c_re, m_s5_c_im, m_s5_d, m_s5_w_glu, m_s5_b_glu, m_pool_w, m_pool_scale, v_c_ctx, v_w_mod, v_b_mod, v_g_pre, v_g_post, v_w_in, v_w_out, v_gla_w_gate, v_gla_b_gate, v_gla_g_norm, v_na_rpb, v_s5_lam_re, v_s5_lam_im, v_s5_log_dt, v_s5_b_re, v_s5_b_im, v_s5_c_re, v_s5_c_im, v_s5_d, v_s5_w_glu, v_s5_b_glu, v_pool_w, v_pool_scale):
    given = dict(zip(_INPUTS, (x, c, ctx, c_ctx, w_mod, b_mod, g_pre, g_post, w_in, w_out, gla_w_gate, gla_b_gate, gla_g_norm, na_rpb, s5_lam_re, s5_lam_im, s5_log_dt, s5_b_re, s5_b_im, s5_c_re, s5_c_im, s5_d, s5_w_glu, s5_b_glu, pool_w, pool_scale, loss_target, m_c_ctx, m_w_mod, m_b_mod, m_g_pre, m_g_post, m_w_in, m_w_out, m_gla_w_gate, m_gla_b_gate, m_gla_g_norm, m_na_rpb, m_s5_lam_re, m_s5_lam_im, m_s5_log_dt, m_s5_b_re, m_s5_b_im, m_s5_c_re, m_s5_c_im, m_s5_d, m_s5_w_glu, m_s5_b_glu, m_pool_w, m_pool_scale, v_c_ctx, v_w_mod, v_b_mod, v_g_pre, v_g_post, v_w_in, v_w_out, v_gla_w_gate, v_gla_b_gate, v_gla_g_norm, v_na_rpb, v_s5_lam_re, v_s5_lam_im, v_s5_log_dt, v_s5_b_re, v_s5_b_im, v_s5_c_re, v_s5_c_im, v_s5_d, v_s5_w_glu, v_s5_b_glu, v_pool_w, v_pool_scale)))
    n_layers = w_in.shape[0]
    shards = [[given[n][l].astype(BF16) for n in _SHARDED] for l in range(n_layers)]
    p = {n: given[n] for n in _SMALL}
    loss, grad_x, _, _, received = local_step(x[0], c, ctx[0], loss_target[0], p, shards)
    final = {}
    for n in _GRAD_SHARDED:
        per_layer = [adamw(received[l][_GRAD_SHARDED.index(n)], given[n][l], given["m_" + n][l], given["v_" + n][l], f"adamw_{n}_l{l}")
                     for l in range(n_layers)]
        final[n] = [jnp.stack([res[kind] for res in per_layer]) for kind in range(4)]
    factor_like = [jnp.zeros((n_layers, 2, D), F32), jnp.zeros((n_layers, 2, 3 * D), F32)]
    like = [given[n] for n in _SMALL] + factor_like
    small_recv = received[0][-1]
    rows_s, rows_d = _pack_rows(like, len(_SMALL)), _pack_rows(like, len(_SMALL) + 1)
    fac_s = small_recv[:, rows_s[0]:rows_s[1]].reshape(N_DEV, n_layers, 2, D)
    fac_d = small_recv[:, rows_d[0]:rows_d[1]].reshape(N_DEV, n_layers, 2, 3 * D)
    me = 4 * lax.axis_index("x") + 2 * lax.axis_index("y") + lax.axis_index("c")
    cols = w_mod.shape[2]
    per_layer = []
    for l in range(n_layers):
        s_all = fac_s[:, l].reshape(2 * N_DEV, D)
        d_mine = lax.dynamic_slice_in_dim(fac_d[:, l].reshape(2 * N_DEV, 3 * D), me * cols, cols, axis=1)
        (g_mod,) = mm_tn(s_all, [d_mine], f"mod_dw_l{l}", tm=2 * N_DEV, tn=cols)
        per_layer.append(adamw(g_mod[None], given["w_mod"][l], given["m_w_mod"][l], given["v_w_mod"][l], f"adamw_w_mod_l{l}"))
    final["w_mod"] = [jnp.stack([res[kind] for res in per_layer]) for kind in range(4)]
    res = adamw(small_recv, _pack(like), _pack([given["m_" + n] for n in _SMALL] + factor_like),
                _pack([given["v_" + n] for n in _SMALL] + factor_like), "adamw_small")
    unpacked = [_unpack(packed, like) for packed in res]
    for i, n in enumerate(_SMALL):
        final[n] = [unpacked[kind][i] for kind in range(4)]
    loss = lax.psum(loss, ("x", "y", "c"))
    return (loss, grad_x[None], *[final[n][0] for n in _WEIGHTS], *[final[n][1] for n in _WEIGHTS],
            *[final[n][2] for n in _WEIGHTS], *[final[n][3] for n in _WEIGHTS])
```

```python
import functools
import math

import numpy as np
import jax
import jax.numpy as jnp
from jax import lax
from jax.experimental import pallas as pl
from jax.experimental.pallas import tpu as pltpu

F32 = jnp.float32
BF16 = jnp.bfloat16
HIGHEST = lax.Precision.HIGHEST
HIGH = lax.Precision.HIGH

D = 1024
GRID_W = 64
EPS = 1e-6
N_DEV = 8
C_GT, C_GV, C_NK, C_NV, C_SU, C_NQ, C_PU, C_GK, C_GG, C_GQ, C_END = 0, 1024, 1280, 1536, 1792, 2048, 2304, 2560, 2688, 2816, 2944
PW = 3072
N_CTX_ORIG = 416
N_IN = 2848
GLA_CHUNK = 128
S5_CHUNK = 256
ROW_TILE = 256
VMEM_LIMIT = 56 * 1024 * 1024

ADAM_LR, ADAM_B1, ADAM_B2, ADAM_EPS, ADAM_WD, ADAM_STEP = 0.001, 0.9, 0.999, 1e-08, 0.01, 10


def _cparams(**kw):
    return pltpu.CompilerParams(vmem_limit_bytes=VMEM_LIMIT, **kw)


def _dg(a, b, ca, cb, precision=None):
    return lax.dot_general(a, b, (((ca,), (cb,)), ((), ())), precision=precision, preferred_element_type=F32)


def hdot(a, b):
    return _dg(a, b, 1, 0, HIGHEST)


def hdot_nt(a, b):
    return _dg(a, b, 1, 1, HIGHEST)


def hdot_tn(a, b):
    return _dg(a, b, 0, 0, HIGHEST)


def mdot(a, b):
    return _dg(a, b, 1, 0, HIGH)


def mdot_nt(a, b):
    return _dg(a, b, 1, 1, HIGH)


def mdot_tn(a, b):
    return _dg(a, b, 0, 0, HIGH)


def b_nn(a, b):
    return _dg(a.astype(BF16), b.astype(BF16), 1, 0)


def b_nt(a, b):
    return _dg(a.astype(BF16), b.astype(BF16), 1, 1)


def b_tn(a, b):
    return _dg(a.astype(BF16), b.astype(BF16), 0, 0)


@jax.custom_vjp
def bdot(a, b):
    return b_nn(a, b)


def _bdot_fwd(a, b):
    return b_nn(a, b), (a, b)


def _bdot_bwd(res, ct):
    a, b = res
    return b_nt(ct, b).astype(a.dtype), b_tn(a, ct).astype(b.dtype)


bdot.defvjp(_bdot_fwd, _bdot_bwd)


def _log_sigmoid(z):
    return jnp.minimum(z, 0.0) - jnp.log(1.0 + jnp.exp(-jnp.abs(z)))


def _silu(z):
    return z * jax.nn.sigmoid(z)


def _gelu(z):
    return 0.5 * z * (1.0 + jnp.tanh(math.sqrt(2.0 / math.pi) * (z + 0.044715 * (z * z * z))))


def _cat(vals):
    return vals[0] if len(vals) == 1 else jnp.concatenate(vals, axis=-1)


def mm_nn(a_parts, b, name, tm=ROW_TILE, tn=1024):
    t = a_parts[0].shape[0]
    k, n = b.shape
    na = len(a_parts)
    tn = min(tn, n)

    def body(*refs):
        a = _cat([r[...].astype(BF16) for r in refs[:na]])
        refs[na + 1][...] = _dg(a, refs[na][...].astype(BF16), 1, 0)

    return pl.pallas_call(
        body, name=name, grid=(n // tn, t // tm),
        in_specs=[pl.BlockSpec((tm, p.shape[1]), lambda j, i: (i, 0)) for p in a_parts]
        + [pl.BlockSpec((k, tn), lambda j, i: (0, j))],
        out_specs=pl.BlockSpec((tm, tn), lambda j, i: (i, j)),
        out_shape=jax.ShapeDtypeStruct((t, n), F32),
        compiler_params=_cparams(dimension_semantics=("arbitrary", "arbitrary")),
    )(*a_parts, b)


def mm_nn_cols(a, b, start, widths, name, tm=ROW_TILE):
    t, k = a.shape
    tn = 1024
    assert start % tn == 0 and sum(widths) <= tn

    def body(a_ref, b_ref, *o_refs):
        r = _dg(a_ref[...].astype(BF16), b_ref[...].astype(BF16), 1, 0)
        off = 0
        for o_ref, w in zip(o_refs, widths):
            o_ref[...] = r[:, off:off + w]
            off += w

    return pl.pallas_call(
        body, name=name, grid=(t // tm,),
        in_specs=[pl.BlockSpec((tm, k), lambda i: (i, 0)), pl.BlockSpec((k, tn), lambda i: (0, start // tn))],
        out_specs=[pl.BlockSpec((tm, w), lambda i: (i, 0)) for w in widths],
        out_shape=[jax.ShapeDtypeStruct((t, w), F32) for w in widths],
        compiler_params=_cparams(dimension_semantics=("arbitrary",)),
    )(a, b)


def mm_nt(a_parts, b, name, tm=ROW_TILE):
    t = a_parts[0].shape[0]
    n, k = b.shape
    na = len(a_parts)

    def body(*refs):
        a = _cat([r[...].astype(BF16) for r in refs[:na]])
        refs[na + 1][...] = _dg(a, refs[na][...].astype(BF16), 1, 1)

    return pl.pallas_call(
        body, name=name, grid=(t // tm,),
        in_specs=[pl.BlockSpec((tm, p.shape[1]), lambda i: (i, 0)) for p in a_parts]
        + [pl.BlockSpec((n, k), lambda i: (0, 0))],
        out_specs=pl.BlockSpec((tm, n), lambda i: (i, 0)),
        out_shape=jax.ShapeDtypeStruct((t, n), F32),
        compiler_params=_cparams(dimension_semantics=("arbitrary",)),
    )(*a_parts, b)


def mm_tn(a, b_parts, name, tm=ROW_TILE, tn=1024, out_dtype=F32):
    t, k = a.shape
    widths = [p.shape[1] for p in b_parts]
    n = sum(widths)
    assert n % tn == 0
    groups, cur, acc = [], [], 0
    for idx, w in enumerate(widths):
        cur.append(idx)
        acc += w
        if acc == tn:
            groups.append(cur)
            cur, acc = [], 0
        assert acc < tn
    assert not cur
    outs = []
    for gi, grp in enumerate(groups):
        parts = [b_parts[i] for i in grp]
        npart = len(parts)
        nsteps = t // tm

        def body(*refs, npart=npart, nsteps=nsteps):
            a_v = refs[0][...].astype(BF16)
            b_v = _cat([r[...].astype(BF16) for r in refs[1:1 + npart]])
            o_ref, acc_ref = refs[1 + npart], refs[2 + npart]
            r = _dg(a_v, b_v, 0, 0)

            @pl.when(pl.program_id(0) == 0)
            def _():
                acc_ref[...] = r

            @pl.when(pl.program_id(0) != 0)
            def _():
                acc_ref[...] += r

            @pl.when(pl.program_id(0) == nsteps - 1)
            def _():
                o_ref[...] = acc_ref[...].astype(o_ref.dtype)

        outs.append(pl.pallas_call(
            body, name=f"{name}_{gi}", grid=(nsteps,),
            in_specs=[pl.BlockSpec((tm, k), lambda i: (i, 0))]
            + [pl.BlockSpec((tm, p.shape[1]), lambda i: (i, 0)) for p in parts],
            out_specs=pl.BlockSpec((k, tn), lambda i: (0, 0)),
            out_shape=jax.ShapeDtypeStruct((k, tn), out_dtype),
            scratch_shapes=[pltpu.VMEM((k, tn), F32)],
            compiler_params=_cparams(dimension_semantics=("arbitrary",)),
        )(a, *parts))
    return outs


def _seg_of(i, nct):
    return jnp.where(i < nct, 1, 0)


def rowwise_fwd(fn, name, rows, segs, globs, out_widths, tile, nct):
    t = rows[0].shape[0]
    nr, ns, ng = len(rows), len(segs), len(globs)

    def body(*refs):
        vals = [r[...] for r in refs[:nr]] + [r[0] for r in refs[nr:nr + ns]] + [r[...] for r in refs[nr + ns:nr + ns + ng]]
        outs = fn(*vals)
        for o_ref, o in zip(refs[nr + ns + ng:], outs):
            o_ref[...] = o

    return pl.pallas_call(
        body, name=name, grid=(t // tile,),
        in_specs=[pl.BlockSpec((tile, r.shape[1]), lambda i: (i, 0)) for r in rows]
        + [pl.BlockSpec((1, 1, s.shape[2]), lambda i: (_seg_of(i, nct), 0, 0)) for s in segs]
        + [pl.BlockSpec(g.shape, lambda i: (0, 0)) for g in globs],
        out_specs=[pl.BlockSpec((tile, w), lambda i: (i, 0)) for w in out_widths],
        out_shape=[jax.ShapeDtypeStruct((t, w), F32) for w in out_widths],
        compiler_params=_cparams(dimension_semantics=("arbitrary",)),
    )(*rows, *segs, *globs)


def rowwise_bwd(fn, name, rows, segs, globs, cts, tile, nct, row_diff, glob_diff):
    t = rows[0].shape[0]
    nr, ns, ng, nc = len(rows), len(segs), len(globs), len(cts)
    d_rows = [i for i in range(nr) if row_diff[i]]
    d_globs = [i for i in range(ng) if glob_diff[i]]

    def body(*refs):
        in_refs, out_refs = refs[:nr + ns + ng + nc], refs[nr + ns + ng + nc:]
        row_v = [r[...] for r in in_refs[:nr]]
        seg_v = [r[0] for r in in_refs[nr:nr + ns]]
        glob_v = [r[...] for r in in_refs[nr + ns:nr + ns + ng]]
        ct_v = tuple(r[...] for r in in_refs[nr + ns + ng:])

        def wrapped(dr, sv, dg):
            rv = list(row_v)
            for j, i in enumerate(d_rows):
                rv[i] = dr[j]
            gv = list(glob_v)
            for j, i in enumerate(d_globs):
                gv[i] = dg[j]
            return tuple(fn(*rv, *sv, *gv))

        _, vjp = jax.vjp(wrapped, [row_v[i] for i in d_rows], seg_v, [glob_v[i] for i in d_globs])
        c_rows, c_segs, c_globs = vjp(ct_v)
        i = pl.program_id(0)
        k = 0
        for c in c_rows:
            out_refs[k][...] = c
            k += 1
        seg_first = jnp.logical_or(i == 0, i == nct)
        for c in c_segs:
            ref = out_refs[k]
            k += 1

            @pl.when(seg_first)
            def _(ref=ref, c=c):
                ref[0] = c

            @pl.when(jnp.logical_not(seg_first))
            def _(ref=ref, c=c):
                ref[0] += c
        for c in c_globs:
            ref = out_refs[k]
            k += 1

            @pl.when(i == 0)
            def _(ref=ref, c=c):
                ref[...] = c

            @pl.when(i != 0)
            def _(ref=ref, c=c):
                ref[...] += c

    return pl.pallas_call(
        body, name=name, grid=(t // tile,),
        in_specs=[pl.BlockSpec((tile, r.shape[1]), lambda i: (i, 0)) for r in rows]
        + [pl.BlockSpec((1, 1, s.shape[2]), lambda i: (_seg_of(i, nct), 0, 0)) for s in segs]
        + [pl.BlockSpec(g.shape, lambda i: (0, 0)) for g in globs]
        + [pl.BlockSpec((tile, c.shape[1]), lambda i: (i, 0)) for c in cts],
        out_specs=[pl.BlockSpec((tile, rows[i].shape[1]), lambda i: (i, 0)) for i in d_rows]
        + [pl.BlockSpec((1, 1, s.shape[2]), lambda i: (_seg_of(i, nct), 0, 0)) for s in segs]
        + [pl.BlockSpec(globs[i].shape, lambda i: (0, 0)) for i in d_globs],
        out_shape=[jax.ShapeDtypeStruct(rows[i].shape, F32) for i in d_rows]
        + [jax.ShapeDtypeStruct(s.shape, F32) for s in segs]
        + [jax.ShapeDtypeStruct(globs[i].shape, F32) for i in d_globs],
        compiler_params=_cparams(dimension_semantics=("arbitrary",)),
    )(*rows, *segs, *globs, *cts)


def f_pre(x, mod, g_pre):
    shift, scale = mod[:, :D], mod[:, D:2 * D]
    rs = lax.rsqrt(jnp.mean(x * x, axis=-1, keepdims=True) + EPS)
    return ((x * rs) * g_pre * (1.0 + scale) + shift,)


def f_post(x, out, mod, g_post):
    gate = mod[:, 2 * D:]
    rs = lax.rsqrt(jnp.mean(out * out, axis=-1, keepdims=True) + EPS)
    return (x + gate * ((out * rs) * g_post),)


def f_mix(o_gla, o_na, y5, u5, pm, gcols, g_norm, s5_d, w_glu, b_glu, wpool, pool_scale, havg, e4):
    ms = mdot(o_gla * o_gla, havg)
    y_gla = o_gla * lax.rsqrt(ms + EPS) * jnp.sum(hdot(g_norm, e4), axis=0, keepdims=True)
    g = _gelu(u5 * s5_d + y5)
    y_s5 = g * jax.nn.sigmoid(bdot(g, w_glu) + b_glu)
    y_pool = bdot(pm, wpool) * pool_scale
    ycat = jnp.concatenate([y_gla, o_na, y_s5, y_pool], axis=-1)
    return (ycat * _silu(gcols),)


@jax.custom_vjp
def _rot_half16(x):
    lane = lax.broadcasted_iota(jnp.int32, x.shape, 1)
    first = jnp.bitwise_and(lane, 15) < 8
    return jnp.where(first, -pltpu.roll(x, x.shape[1] - 8, 1), pltpu.roll(x, 8, 1))


def _rot_fwd(x):
    return _rot_half16(x), None


def _rot_bwd(_, ct):
    return (-_rot_half16(ct),)


_rot_half16.defvjp(_rot_fwd, _rot_bwd)


def f_gla_prep(pk, pg, pq, cos, sin, wg, bg):
    z = bdot(pg, wg) + bg
    lg = _log_sigmoid(z) * (1.0 / 16.0)
    k_r = pk * cos + _rot_half16(pk) * sin
    q_r = (pq * cos + _rot_half16(pq) * sin) * (32.0 ** -0.5)
    return q_r, k_r, lg[:, :128], lg[:, 128:]


def _gla_consts(rev):
    c = GLA_CHUNK
    i = np.arange(c)
    inc = (i[None, :] >= i[:, None]) if rev else (i[None, :] <= i[:, None])
    mq = np.stack([(np.arange(128) // 32 == h) for h in range(4)]).astype(np.float32).reshape(4, 1, 128)
    mv = np.stack([(np.arange(256) // 64 == h) for h in range(4)]).astype(np.float32).reshape(4, 1, 256)
    bdt = (np.arange(256)[:, None] // 64 == np.arange(128)[None, :] // 32).astype(np.float32)
    inc = inc.astype(np.float32)
    return jnp.asarray(inc), jnp.asarray(inc.T.copy()), jnp.asarray(mq), jnp.asarray(mv), jnp.asarray(bdt)


def _stack_heads(x, m_ref):
    return jnp.concatenate([x * m_ref[h] for h in range(4)], axis=0)


def _tile4(m):
    return jnp.concatenate([m, m, m, m], axis=0)


def _fold_heads(r4, m_ref):
    r = r4.shape[0] // 4
    out = m_ref[0] * r4[0:r]
    for h in range(1, 4):
        out = out + m_ref[h] * r4[h * r:(h + 1) * r]
    return out


def _gla_chunk_of(s, n_ctx_chunks, n_chunks, rev):
    if not rev:
        return s
    return jnp.where(s < n_ctx_chunks, n_ctx_chunks - 1 - s, n_ctx_chunks + n_chunks - 1 - s)


def gla_scan_fwd(q, k, v, lg, acc, n_ctx_rows, rev, name, comm=None):
    t = q.shape[0]
    nch, ncc = t // GLA_CHUNK, n_ctx_rows // GLA_CHUNK
    inc, inc_t, mq, mv, bdt = _gla_consts(rev)

    def body(q_ref, k_ref, v_ref, lg_ref, acc_ref, inc_ref, inct_ref, mq_ref, mv_ref, bdt_ref, o_ref, st_ref):
        lmask, lmask_t = inc_ref[...], inct_ref[...]
        bd = bdt_ref[...]

        def step(s, st):
            c = _gla_chunk_of(s, ncc, nch, rev)
            rows = pl.ds(pl.multiple_of(c * GLA_CHUNK, GLA_CHUNK), GLA_CHUNK)
            qc, kc, vc, lgc = q_ref[rows, :], k_ref[rows, :], v_ref[rows, :], lg_ref[rows, :]
            st_ref[c] = st
            b = mdot(lmask, lgc)
            blast = jnp.sum(lgc, axis=0, keepdims=True)
            qe, ke, kd = qc * jnp.exp(b), kc * jnp.exp(-b), kc * jnp.exp(blast - b)
            ke4, v4 = _stack_heads(ke, mq_ref), _stack_heads(vc, mv_ref)
            at = _tile4(lmask_t) * b_nt(ke4, qe)
            o_ref[rows, :] = acc_ref[rows, :] + b_nt(qe, st) + b_tn(at, v4)
            return st * jnp.exp(blast) + bd * mdot_tn(vc, kd)

        lax.fori_loop(0, nch, step, jnp.zeros((256, 128), F32))

    return _call_with_exchange(body, name, [q, k, v, lg, acc, inc, inc_t, mq, mv, bdt],
                               [jax.ShapeDtypeStruct((t, 256), F32), jax.ShapeDtypeStruct((nch, 256, 128), F32)], comm)


def gla_scan_bwd(q, k, v, lg, st, do, acc, n_ctx_rows, rev, name, comm=None):
    t = q.shape[0]
    nch, ncc = t // GLA_CHUNK, n_ctx_rows // GLA_CHUNK
    inc, inc_t, mq, mv, bdt = _gla_consts(rev)

    def body(q_ref, k_ref, v_ref, lg_ref, st_ref, do_ref, aq_ref, ak_ref, av_ref, inc_ref, inct_ref, mq_ref, mv_ref, bdt_ref,
             dq_ref, dk_ref, dv_ref, dlg_ref):
        lmask, lmask_t = inc_ref[...], inct_ref[...]
        bd = bdt_ref[...]

        def step(j, carry):
            dst, gsum = carry
            s = nch - 1 - j
            c = _gla_chunk_of(s, ncc, nch, rev)
            rows = pl.ds(pl.multiple_of(c * GLA_CHUNK, GLA_CHUNK), GLA_CHUNK)
            qc, kc, vc, lgc, doc = q_ref[rows, :], k_ref[rows, :], v_ref[rows, :], lg_ref[rows, :], do_ref[rows, :]
            stc = st_ref[c]
            b = mdot(lmask, lgc)
            blast = jnp.sum(lgc, axis=0, keepdims=True)
            eb, enb, edb = jnp.exp(b), jnp.exp(-b), jnp.exp(blast - b)
            qe, ke, kd = qc * eb, kc * enb, kc * edb
            ke4, v4 = _stack_heads(ke, mq_ref), _stack_heads(vc, mv_ref)
            lm4 = _tile4(lmask_t)
            at = lm4 * b_nt(ke4, qe)
            dat = lm4 * mdot_nt(v4, doc)
            dqe = mdot(doc, stc) + mdot_tn(dat, ke4)
            dke = _fold_heads(mdot(dat, qe), mq_ref)
            dv = b_nt(kd, dst) + _fold_heads(b_nn(at, doc), mv_ref)
            dkd = mdot(vc, dst)
            dq = dqe * eb
            dk = dke * enb + dkd * edb
            g = qc * dq - kc * dk
            dlg_ref[rows, :] = mdot_tn(lmask, g) + gsum
            dq_ref[rows, :] = aq_ref[rows, :] + dq
            dk_ref[rows, :] = ak_ref[rows, :] + dk
            dv_ref[rows, :] = av_ref[rows, :] + dv
            dst_new = dst * jnp.exp(blast) + bd * mdot_tn(doc, qe)
            return dst_new, gsum + jnp.sum(g, axis=0, keepdims=True)

        lax.fori_loop(0, nch, step, (jnp.zeros((256, 128), F32), jnp.zeros((1, 128), F32)))

    return _call_with_exchange(body, name, [q, k, v, lg, st, do, *acc, inc, inc_t, mq, mv, bdt],
                               [jax.ShapeDtypeStruct((t, 128), F32), jax.ShapeDtypeStruct((t, 128), F32),
                                jax.ShapeDtypeStruct((t, 256), F32), jax.ShapeDtypeStruct((t, 128), F32)], comm)


def whole_fwd(fn, name, args, out_shapes):
    def body(*refs):
        outs = fn(*[r[...] for r in refs[:len(args)]])
        for o_ref, o in zip(refs[len(args):], outs):
            o_ref[...] = o

    vm = pl.BlockSpec(memory_space=pltpu.VMEM)
    return pl.pallas_call(
        body, name=name, in_specs=[vm] * len(args), out_specs=[vm] * len(out_shapes),
        out_shape=[jax.ShapeDtypeStruct(s, F32) for s in out_shapes], compiler_params=_cparams(),
    )(*args)


def whole_bwd(fn, name, args, cts, diff):
    d_idx = [i for i in range(len(args)) if diff[i]]

    def body(*refs):
        vals = [r[...] for r in refs[:len(args)]]
        ct_v = tuple(r[...] for r in refs[len(args):len(args) + len(cts)])

        def wrapped(dv):
            av = list(vals)
            for j, i in enumerate(d_idx):
                av[i] = dv[j]
            return tuple(fn(*av))

        _, vjp = jax.vjp(wrapped, [vals[i] for i in d_idx])
        (c_args,) = vjp(ct_v)
        for o_ref, c in zip(refs[len(args) + len(cts):], c_args):
            o_ref[...] = c

    vm = pl.BlockSpec(memory_space=pltpu.VMEM)
    return pl.pallas_call(
        body, name=name, in_specs=[vm] * (len(args) + len(cts)), out_specs=[vm] * len(d_idx),
        out_shape=[jax.ShapeDtypeStruct(args[i].shape, F32) for i in d_idx], compiler_params=_cparams(),
    )(*args, *cts)


def _s5_consts():
    e_rep = (np.arange(256)[:, None] // 16 == np.arange(16)[None, :]).astype(np.float32)
    e_tile = (np.arange(64)[:, None] == np.arange(1024)[None, :] % 64).astype(np.float32)
    gmask = (np.arange(16)[:, None] == np.arange(1024)[None, :] // 64).astype(np.float32)
    bdm = (np.arange(256)[:, None] // 16 == np.arange(1024)[None, :] // 64).astype(np.float32)
    return jnp.asarray(e_rep), jnp.asarray(e_tile), jnp.asarray(gmask), jnp.asarray(bdm)


def f_s5_params(lam_re, lam_im, log_dt, bt_re, bt_im, ct_re, ct_im, e_rep, e_tile, gmask, bdm):
    dt = jnp.exp(log_dt)
    mag = jnp.exp(lam_re * dt)
    ang = lam_im * dt
    lb_re, lb_im = mag * jnp.cos(ang), mag * jnp.sin(ang)
    num_re, num_im = lb_re - 1.0, lb_im
    den = lam_re * lam_re + lam_im * lam_im
    coef_re = (num_re * lam_re + num_im * lam_im) / den
    coef_im = (num_im * lam_re - num_re * lam_im) / den
    cr, ci = hdot(e_rep, coef_re), hdot(e_rep, coef_im)
    bbt_re = cr * bt_re - ci * bt_im
    bbt_im = cr * bt_im + ci * bt_re
    a_re = jnp.sum(hdot(lb_re, e_tile) * gmask, axis=0, keepdims=True)
    a_im = jnp.sum(hdot(lb_im, e_tile) * gmask, axis=0, keepdims=True)
    return (a_re, a_im, hdot(bbt_re, e_tile) * bdm, hdot(bbt_im, e_tile) * bdm,
            hdot(ct_re, e_tile) * bdm, hdot(ct_im, e_tile) * bdm)


def _s5_doubling(xr, xi, pr, pi, pos, n, steps, rev):
    rows = xr.shape[0]
    for s in steps:
        if rev:
            keep = pos < (n - s)
            sr, si = pltpu.roll(xr, rows - s, 0), pltpu.roll(xi, rows - s, 0)
        else:
            keep = pos >= s
            sr, si = pltpu.roll(xr, s, 0), pltpu.roll(xi, s, 0)
        sr, si = jnp.where(keep, sr, 0.0), jnp.where(keep, si, 0.0)
        xr, xi = xr + pr * sr - pi * si, xi + pr * si + pi * sr
        pr, pi = pr * pr - pi * pi, 2.0 * pr * pi
    return xr, xi, pr, pi


SUBLANES = 8


def _s5_scan(xr, xi, a_re, a_im, rev, chunk, scr):
    xs_r, xs_i, yp_r, yp_i = scr
    ng = chunk // SUBLANES
    x3r, x3i = xr.reshape(ng, SUBLANES, 1024), xi.reshape(ng, SUBLANES, 1024)
    sub = lax.broadcasted_iota(jnp.int32, (SUBLANES, 1024), 0)
    a8r, a8i = a_re, a_im
    for s in (1, 2, 4):
        keep = sub < (SUBLANES - s) if rev else sub >= s
        mr, mi = jnp.where(keep, a8r, 0.0)[None], jnp.where(keep, a8i, 0.0)[None]
        shift = SUBLANES - s if rev else s
        sr, si = pltpu.roll(x3r, shift, 1), pltpu.roll(x3i, shift, 1)
        x3r, x3i = x3r + mr * sr - mi * si, x3i + mr * si + mi * sr
        a8r, a8i = a8r * a8r - a8i * a8i, 2.0 * a8r * a8i
    xr, xi = x3r.reshape(chunk, 1024), x3i.reshape(chunk, 1024)
    nblk = 1024 // 128
    for j in range(nblk):
        xs_r[j] = xr[:, 128 * j:128 * (j + 1)]
        xs_i[j] = xi[:, 128 * j:128 * (j + 1)]
    edge = pl.ds(0 if rev else SUBLANES - 1, ng, stride=SUBLANES)
    gr = jnp.concatenate([xs_r[j, edge, :] for j in range(nblk)], axis=-1)
    gi = jnp.concatenate([xs_i[j, edge, :] for j in range(nblk)], axis=-1)
    grow = lax.broadcasted_iota(jnp.int32, (ng, 1024), 0)
    steps = tuple(1 << k for k in range((ng - 1).bit_length()))
    gr, gi, _, _ = _s5_doubling(gr, gi, a8r, a8i, grow, ng, steps, rev)
    if rev:
        yp_r[...] = jnp.where(grow < ng - 1, pltpu.roll(gr, ng - 1, 0), 0.0)
        yp_i[...] = jnp.where(grow < ng - 1, pltpu.roll(gi, ng - 1, 0), 0.0)
    else:
        yp_r[...] = jnp.where(grow >= 1, pltpu.roll(gr, 1, 0), 0.0)
        yp_i[...] = jnp.where(grow >= 1, pltpu.roll(gi, 1, 0), 0.0)
    sub = lax.broadcasted_iota(jnp.int32, (SUBLANES, 1024), 0)
    tr, ti = jnp.zeros((SUBLANES, 1024), F32), jnp.zeros((SUBLANES, 1024), F32)
    cr, ci = a_re, a_im
    for n in range(1, SUBLANES + 1):
        r = SUBLANES - n if rev else n - 1
        tr, ti = jnp.where(sub == r, cr, tr), jnp.where(sub == r, ci, ti)
        cr, ci = cr * a_re - ci * a_im, cr * a_im + ci * a_re
    for j in range(nblk):
        lanes = slice(128 * j, 128 * (j + 1))
        tr_j, ti_j = tr[:, lanes], ti[:, lanes]
        for g in range(ng):
            rows = slice(g * SUBLANES, (g + 1) * SUBLANES)
            er, ei = yp_r[g:g + 1, lanes], yp_i[g:g + 1, lanes]
            xs_r[j, rows, :] = xs_r[j, rows, :] + tr_j * er - ti_j * ei
            xs_i[j, rows, :] = xs_i[j, rows, :] + tr_j * ei + ti_j * er
    return (jnp.concatenate([xs_r[j] for j in range(nblk)], axis=-1),
            jnp.concatenate([xs_i[j] for j in range(nblk)], axis=-1))


def _s5_scratch(chunk):
    return [pltpu.VMEM((8, chunk, 128), F32), pltpu.VMEM((8, chunk, 128), F32),
            pltpu.VMEM((chunk // SUBLANES, 1024), F32), pltpu.VMEM((chunk // SUBLANES, 1024), F32)]


def _s5_chunk_states(u_c, x0r, x0i, a_re, a_im, bb_re, bb_im, rev, chunk, scr):
    row = lax.broadcasted_iota(jnp.int32, (chunk, 1024), 0)
    first = row == (chunk - 1 if rev else 0)
    inj_r = a_re * x0r - a_im * x0i
    inj_i = a_re * x0i + a_im * x0r
    xr = b_nn(u_c, bb_re) + jnp.where(first, inj_r, 0.0)
    xi = b_nn(u_c, bb_im) + jnp.where(first, inj_i, 0.0)
    return _s5_scan(xr, xi, a_re, a_im, rev, chunk, scr)


def _row_pick(x, idx):
    row = lax.broadcasted_iota(jnp.int32, x.shape, 0)
    return jnp.sum(jnp.where(row == idx, x, 0.0), axis=0, keepdims=True)


def s5_scan_fwd(u, acc, a_re, a_im, bb_re, bb_im, cc_re, cc_im, n_ctx_rows, chunk, rev, name, comm=None):
    t = u.shape[0]
    nch, ncc = t // chunk, n_ctx_rows // chunk

    def body(u_ref, acc_ref, ar_ref, ai_ref, br_ref, bi_ref, cr_ref, ci_ref, y_ref, x0r_ref, x0i_ref, xsr_ref, xsi_ref, *scr):
        a_r, a_i = ar_ref[...], ai_ref[...]

        def step(s, carry):
            x0r, x0i = carry
            c = _gla_chunk_of(s, ncc, nch, rev)
            rows = pl.ds(pl.multiple_of(c * chunk, chunk), chunk)
            x0r_ref[c] = x0r
            x0i_ref[c] = x0i
            xr, xi = _s5_chunk_states(u_ref[rows, :], x0r, x0i, a_r, a_i, br_ref[...], bi_ref[...], rev, chunk, scr)
            y_ref[rows, :] = acc_ref[rows, :] + b_nt(xr, cr_ref[...]) - b_nt(xi, ci_ref[...])
            xsr_ref[rows, :] = xr.astype(BF16)
            xsi_ref[rows, :] = xi.astype(BF16)
            last = 0 if rev else chunk - 1
            return _row_pick(xr, last), _row_pick(xi, last)

        lax.fori_loop(0, nch, step, (jnp.zeros((1, 1024), F32), jnp.zeros((1, 1024), F32)))

    return _call_with_exchange(
        body, name, [u, acc, a_re, a_im, bb_re, bb_im, cc_re, cc_im],
        [jax.ShapeDtypeStruct((t, 256), F32), jax.ShapeDtypeStruct((nch, 1, 1024), F32),
         jax.ShapeDtypeStruct((nch, 1, 1024), F32), jax.ShapeDtypeStruct((t, 1024), BF16),
         jax.ShapeDtypeStruct((t, 1024), BF16)], comm, _s5_scratch(chunk))


def s5_scan_bwd(u, dy, du_acc, x0r, x0i, xsr, xsi, a_re, a_im, bb_re, bb_im, cc_re, cc_im, n_ctx_rows, chunk, rev, name):
    t = u.shape[0]
    nch, ncc = t // chunk, n_ctx_rows // chunk

    def body(u_ref, dy_ref, dua_ref, x0r_ref, x0i_ref, xsr_ref, xsi_ref, ar_ref, ai_ref, br_ref, bi_ref, cr_ref, ci_ref,
             du_ref, dar_ref, dai_ref, dbr_ref, dbi_ref, dcr_ref, dci_ref, *scr):
        a_r, a_i = ar_ref[...], ai_ref[...]
        for ref in (dbr_ref, dbi_ref, dcr_ref, dci_ref):
            ref[...] = jnp.zeros_like(ref)
        row = lax.broadcasted_iota(jnp.int32, (chunk, 1024), 0)
        first_idx, last_idx = (chunk - 1, 0) if rev else (0, chunk - 1)

        def step(j, carry):
            lcr, lci, dar, dai = carry
            s = nch - 1 - j
            c = _gla_chunk_of(s, ncc, nch, rev)
            rows = pl.ds(pl.multiple_of(c * chunk, chunk), chunk)
            u_c, dy_c = u_ref[rows, :], dy_ref[rows, :]
            x0r_c, x0i_c = x0r_ref[c], x0i_ref[c]
            xr, xi = xsr_ref[rows, :].astype(F32), xsi_ref[rows, :].astype(F32)
            dcr_ref[...] += b_tn(dy_c, xr)
            dci_ref[...] -= b_tn(dy_c, xi)
            inj_r = a_r * lcr + a_i * lci
            inj_i = a_r * lci - a_i * lcr
            is_last = row == last_idx
            lr = b_nn(dy_c, cr_ref[...]) + jnp.where(is_last, inj_r, 0.0)
            li = -b_nn(dy_c, ci_ref[...]) + jnp.where(is_last, inj_i, 0.0)
            lr, li = _s5_scan(lr, li, a_r, -a_i, not rev, chunk, scr)
            du_ref[rows, :] = dua_ref[rows, :] + b_nt(lr, br_ref[...]) + b_nt(li, bi_ref[...])
            dbr_ref[...] += b_tn(u_c, lr)
            dbi_ref[...] += b_tn(u_c, li)
            if rev:
                pr, pi = pltpu.roll(xr, chunk - 1, 0), pltpu.roll(xi, chunk - 1, 0)
            else:
                pr, pi = pltpu.roll(xr, 1, 0), pltpu.roll(xi, 1, 0)
            is_first = row == first_idx
            pr, pi = jnp.where(is_first, x0r_c, pr), jnp.where(is_first, x0i_c, pi)
            dar = dar + jnp.sum(lr * pr + li * pi, axis=0, keepdims=True)
            dai = dai + jnp.sum(li * pr - lr * pi, axis=0, keepdims=True)
            return _row_pick(lr, first_idx), _row_pick(li, first_idx), dar, dai

        z = jnp.zeros((1, 1024), F32)
        _, _, dar, dai = lax.fori_loop(0, nch, step, (z, z, z, z))
        dar_ref[...] = dar
        dai_ref[...] = dai

    vm = pl.BlockSpec(memory_space=pltpu.VMEM)
    big = jax.ShapeDtypeStruct((256, 1024), F32)
    vec = jax.ShapeDtypeStruct((1, 1024), F32)
    return pl.pallas_call(
        body, name=name, in_specs=[vm] * 13, out_specs=[vm] * 7,
        out_shape=[jax.ShapeDtypeStruct((t, 256), F32), vec, vec, big, big, big, big],
        scratch_shapes=_s5_scratch(chunk), compiler_params=_cparams(),
    )(u, dy, du_acc, x0r, x0i, xsr, xsi, a_re, a_im, bb_re, bb_im, cc_re, cc_im)


POOL_HALO = 8


def pool_apply(u_pad, n, transpose, name, tile=ROW_TILE):
    tile = min(tile, n)
    ext = tile + 2 * POOL_HALO
    trel = np.arange(ext)[None, :] - POOL_HALO - np.arange(tile)[:, None]
    if transpose:
        trel = -trel
    band4 = np.concatenate([((trel >= -(1 << w)) & (trel <= (1 << w) - 1)) for w in range(4)], axis=0).astype(np.float32)

    def body(u_ref, band_ref, lm_ref, o_ref):
        lax.fori_loop(0, n // tile, functools.partial(step, u_ref, band_ref, lm_ref, o_ref), 0)

    def step(u_ref, band_ref, lm_ref, o_ref, i, carry):
        val = u_ref[pl.ds(pl.multiple_of(i * tile, tile), ext), :]
        lane = lax.broadcasted_iota(jnp.int32, (ext, 256), 1)
        half = jnp.left_shift(1, jnp.right_shift(lane, 6))
        trow = lax.broadcasted_iota(jnp.int32, (ext, 256), 0) + (i * tile - POOL_HALO)
        cnt = jnp.minimum(trow + half, n) - jnp.maximum(trow - half, 0)
        inv = 1.0 / jnp.maximum(cnt, 1).astype(F32)
        src = val * inv if transpose else val
        acc = _fold_heads(mdot(band_ref[...], src), lm_ref)
        centre = val[POOL_HALO:POOL_HALO + tile]
        if not transpose:
            acc = acc * inv[POOL_HALO:POOL_HALO + tile]
        o_ref[pl.ds(pl.multiple_of(i * tile, tile), tile), :] = acc - centre
        return carry

    vm = pl.BlockSpec(memory_space=pltpu.VMEM)
    return pl.pallas_call(
        body, name=name, in_specs=[vm] * 3, out_specs=vm,
        out_shape=jax.ShapeDtypeStruct((n, 256), F32), compiler_params=_cparams(),
    )(u_pad, jnp.asarray(band4), _na_head_masks())


NA_SCALE = 64.0 ** -0.5
NEG = -1e30


def _call_with_exchange(compute, name, args, out_shapes, comm, scratch=()):
    vm = pl.BlockSpec(memory_space=pltpu.VMEM)
    n_in, n_out = len(args), len(out_shapes)
    if comm is None:
        outs = pl.pallas_call(compute, name=name, in_specs=[vm] * n_in, out_specs=[vm] * n_out, out_shape=out_shapes,
                              scratch_shapes=list(scratch), compiler_params=_cparams())(*args)
        return outs, None
    arrays, scatter = comm
    n = len(arrays)

    def body(*refs):
        c_in = refs[n_in:n_in + n]
        c_out = refs[n_in + n + n_out:n_in + 2 * n + n_out]
        scr = refs[n_in + 2 * n + n_out:n_in + 2 * n + n_out + len(scratch)]
        finish = _exchange_issue(c_in, c_out, scatter, *refs[n_in + 2 * n + n_out + len(scratch):])
        compute(*refs[:n_in], *refs[n_in + n:n_in + n + n_out], *scr)
        finish()

    hbm = pl.BlockSpec(memory_space=pl.ANY)
    outs = pl.pallas_call(
        body, name=name, in_specs=[vm] * n_in + [hbm] * n, out_specs=[vm] * n_out + [hbm] * n,
        out_shape=list(out_shapes) + _exchange_out_shapes(arrays, scatter), scratch_shapes=list(scratch) + _exchange_sems(n),
        compiler_params=_cparams(has_side_effects=True),
    )(*args, *arrays)
    return outs[:n_out], outs[n_out:]


def _na_head_masks():
    return jnp.asarray(np.stack([(np.arange(256) // 64 == h) for h in range(4)]).astype(np.float32).reshape(4, 1, 256))


def _na_window(r, rows):
    start = jnp.clip(r - 4, 0, rows - 8)
    return start, start - r + 7


def _na_probs(qh, kw, kc, bias):
    s_c = b_nt(qh, kc)
    m = jnp.max(s_c, axis=-1, keepdims=True)
    if kw is not None:
        s_w = b_nt(qh, kw) + bias
        m = jnp.maximum(m, jnp.max(s_w, axis=-1, keepdims=True))
        p_w = jnp.exp(s_w - m)
    p_c = jnp.exp(s_c - m)
    l = jnp.sum(p_c, axis=-1, keepdims=True)
    if kw is not None:
        l = l + jnp.sum(p_w, axis=-1, keepdims=True)
        return p_w / l, p_c / l
    return None, p_c / l


def na_fwd(q, k, v, bias8, n_ctx_rows, name, comm=None):
    t = q.shape[0]
    m_ctx = n_ctx_rows
    rows = (t - m_ctx) // GRID_W
    hm = _na_head_masks()

    def body(q_ref, k_ref, v_ref, b_ref, hm_ref, o_ref):
        kc, vc = k_ref[0:m_ctx, :], v_ref[0:m_ctx, :]

        def ctx_step(i, _):
            rs = pl.ds(pl.multiple_of(i * 64, 64), 64)
            q4 = _stack_heads(q_ref[rs, :] * NA_SCALE, hm_ref)
            _, p_c = _na_probs(q4, None, kc, None)
            o_ref[rs, :] = _fold_heads(b_nn(p_c, vc), hm_ref)
            return 0

        lax.fori_loop(0, m_ctx // 64, ctx_step, 0)

        def lat_step(r, _):
            start, off = _na_window(r, rows)
            rs = pl.ds(pl.multiple_of(m_ctx + r * 64, 64), 64)
            ws = pl.ds(pl.multiple_of(m_ctx + start * 64, 64), 512)
            q4 = _stack_heads(q_ref[rs, :] * NA_SCALE, hm_ref)
            kw, vw = k_ref[ws, :], v_ref[ws, :]
            p_w, p_c = _na_probs(q4, kw, kc, b_ref[off])
            o_ref[rs, :] = _fold_heads(b_nn(p_w, vw) + b_nn(p_c, vc), hm_ref)
            return 0

        lax.fori_loop(0, rows, lat_step, 0)

    (o,), received = _call_with_exchange(body, name, [q, k, v, bias8, hm], [jax.ShapeDtypeStruct((t, 256), F32)], comm)
    return o if comm is None else (o, received)


def na_bwd(q, k, v, do, bias8, n_ctx_rows, name, comm=None):
    t = q.shape[0]
    m_ctx = n_ctx_rows
    rows = (t - m_ctx) // GRID_W
    hm = _na_head_masks()

    def body(q_ref, k_ref, v_ref, do_ref, b_ref, hm_ref, dq_ref, dk_ref, dv_ref, db_ref):
        kc, vc = k_ref[0:m_ctx, :], v_ref[0:m_ctx, :]
        dk_ref[...] = jnp.zeros_like(dk_ref)
        dv_ref[...] = jnp.zeros_like(dv_ref)
        db_ref[...] = jnp.zeros_like(db_ref)

        def head_terms(qh, doh, kw, vw, bias):
            p_w, p_c = _na_probs(qh, kw, kc, bias)
            dp_c = b_nt(doh, vc)
            delta = jnp.sum(p_c * dp_c, axis=-1, keepdims=True)
            if kw is not None:
                dp_w = b_nt(doh, vw)
                delta = delta + jnp.sum(p_w * dp_w, axis=-1, keepdims=True)
                ds_w = p_w * (dp_w - delta)
            else:
                ds_w = None
            ds_c = p_c * (dp_c - delta)
            return p_w, p_c, ds_w, ds_c

        def ctx_step(i, carry):
            dkc, dvc = carry
            rs = pl.ds(pl.multiple_of(i * 64, 64), 64)
            q4, do4 = _stack_heads(q_ref[rs, :] * NA_SCALE, hm_ref), _stack_heads(do_ref[rs, :], hm_ref)
            _, p_c, _, ds_c = head_terms(q4, do4, None, None, None)
            dq_ref[rs, :] = _fold_heads(b_nn(ds_c, kc), hm_ref) * NA_SCALE
            return dkc + b_tn(ds_c, q4), dvc + b_tn(p_c, do4)

        zc = jnp.zeros((m_ctx, 256), F32)
        carry = lax.fori_loop(0, m_ctx // 64, ctx_step, (zc, zc))

        def lat_step(r, carry):
            dkc, dvc = carry
            start, off = _na_window(r, rows)
            rs = pl.ds(pl.multiple_of(m_ctx + r * 64, 64), 64)
            ws = pl.ds(pl.multiple_of(m_ctx + start * 64, 64), 512)
            q4, do4 = _stack_heads(q_ref[rs, :] * NA_SCALE, hm_ref), _stack_heads(do_ref[rs, :], hm_ref)
            kw, vw = k_ref[ws, :], v_ref[ws, :]
            p_w, p_c, ds_w, ds_c = head_terms(q4, do4, kw, vw, b_ref[off])
            dq_ref[rs, :] = _fold_heads(b_nn(ds_w, kw) + b_nn(ds_c, kc), hm_ref) * NA_SCALE
            dk_ref[ws, :] += b_tn(ds_w, q4)
            dv_ref[ws, :] += b_tn(p_w, do4)
            db_ref[off] += ds_w
            return dkc + b_tn(ds_c, q4), dvc + b_tn(p_c, do4)

        dkc, dvc = lax.fori_loop(0, rows, lat_step, carry)
        dk_ref[0:m_ctx, :] = dkc
        dv_ref[0:m_ctx, :] = dvc

    row = jax.ShapeDtypeStruct((t, 256), F32)
    return _call_with_exchange(body, name, [q, k, v, do, bias8, hm], [row, row, row, jax.ShapeDtypeStruct(bias8.shape, F32)], comm)


def _na_toeplitz():
    col = np.arange(GRID_W)
    dd = (col[None, :] - col[:, None] + 15).reshape(-1)
    tt = np.zeros((GRID_W * GRID_W, 128), np.float32)
    ok = (dd >= 0) & (dd <= 30)
    tt[np.arange(GRID_W * GRID_W)[ok], dd[ok]] = 1.0
    return tt


def _na_bias8(rpb, name):
    col = np.arange(GRID_W)
    cs = np.clip(col - 8, 0, GRID_W - 16)
    col_mask = (col[None, :] >= cs[:, None]) & (col[None, :] < cs[:, None] + 16)
    rpb2 = jnp.pad(rpb.reshape(60, 31), ((0, 4), (0, 97)))
    (toe,) = whole_fwd(lambda r_, t_: (hdot_nt(r_, t_),), name, [rpb2, jnp.asarray(_na_toeplitz())], [(64, GRID_W * GRID_W)])
    toe = toe[:60].reshape(4, 15, GRID_W, GRID_W)
    b = jnp.stack([toe[:, off:off + 8] for off in range(8)], axis=1)
    b = jnp.where(jnp.asarray(col_mask)[None, None, None], b, NEG)
    return b.transpose(1, 0, 3, 2, 4).reshape(8, 4 * GRID_W, 8 * GRID_W)


def _na_rpb_grad(dbias8, name):
    tt = _na_toeplitz()
    sel = np.zeros((64, 256), np.float32)
    for h in range(4):
        for off in range(8):
            for i in range(8):
                sel[h * 15 + off + i, h * 64 + off * 8 + i] = 1.0
    a2 = dbias8.reshape(8, 4, GRID_W, 8, GRID_W).transpose(1, 0, 3, 2, 4).reshape(256, GRID_W * GRID_W)
    (out,) = whole_fwd(lambda a, t_, s_: (hdot(s_, hdot(a, t_)),), name, [a2, jnp.asarray(tt), jnp.asarray(sel)], [(64, 128)])
    return out[:60, :31].reshape(4, 15, 31)


def f_mod(cs, b_mod, w_mod):
    s = _silu(cs)
    return bdot(s, w_mod) + b_mod, s


def loss_and_grad(z, tgt, n_ctx_rows, name, tile=ROW_TILE):
    t, d = z.shape
    tile = min(tile, n_ctx_rows)
    nct = n_ctx_rows // tile

    def body(z_ref, t_ref, dz_ref, loss_ref):
        i = pl.program_id(0)

        @pl.when(i == 0)
        def _():
            loss_ref[...] = jnp.zeros_like(loss_ref)

        @pl.when(i < nct)
        def _():
            dz_ref[...] = jnp.zeros_like(dz_ref)

        @pl.when(i >= nct)
        def _():
            diff = z_ref[...] - t_ref[...]
            dz_ref[...] = diff * (1.0 / d)
            loss_ref[...] += 0.5 * jnp.sum(jnp.sum(diff * diff, axis=-1, keepdims=True) * (1.0 / d), axis=0, keepdims=True)

    dz, loss = pl.pallas_call(
        body, name=name, grid=(t // tile,),
        in_specs=[pl.BlockSpec((tile, d), lambda i: (i, 0)),
                  pl.BlockSpec((tile, d), lambda i: (jnp.maximum(i - nct, 0), 0))],
        out_specs=[pl.BlockSpec((tile, d), lambda i: (i, 0)), pl.BlockSpec((8, 128), lambda i: (0, 0))],
        out_shape=[jax.ShapeDtypeStruct((t, d), F32), jax.ShapeDtypeStruct((8, 128), F32)],
        compiler_params=_cparams(dimension_semantics=("arbitrary",)),
    )(z, tgt)
    return loss[0, 0], dz


def adamw(parts, w, m, v, name, tile=256):
    npart, r, c = parts.shape
    tile = min(tile, r)
    assert r % tile == 0
    c1 = 1.0 / (1.0 - ADAM_B1 ** ADAM_STEP)
    c2 = 1.0 / (1.0 - ADAM_B2 ** ADAM_STEP)

    def body(p_ref, w_ref, m_ref, v_ref, g_ref, d_ref, nm_ref, nv_ref):
        g = p_ref[0].astype(F32)
        for i in range(1, npart):
            g = g + p_ref[i].astype(F32)
        nm = ADAM_B1 * m_ref[...] + (1.0 - ADAM_B1) * g
        nv = ADAM_B2 * v_ref[...] + (1.0 - ADAM_B2) * (g * g)
        g_ref[...] = g
        nm_ref[...] = nm
        nv_ref[...] = nv
        d_ref[...] = -ADAM_LR * ((nm * c1) / (jnp.sqrt(nv * c2) + ADAM_EPS) + ADAM_WD * w_ref[...])

    blk = pl.BlockSpec((tile, c), lambda i: (i, 0))
    return pl.pallas_call(
        body, name=name, grid=(r // tile,),
        in_specs=[pl.BlockSpec((npart, tile, c), lambda i: (0, i, 0)), blk, blk, blk],
        out_specs=[blk] * 4, out_shape=[jax.ShapeDtypeStruct((r, c), F32)] * 4,
        compiler_params=_cparams(dimension_semantics=("arbitrary",)),
    )(parts, w, m, v)


def _peer(x, y, c, k):
    return (1 - x if k & 4 else x, 1 - y if k & 2 else y, 1 - c if k & 1 else c)


def _exchange_out_shapes(arrays, scatter):
    return [jax.ShapeDtypeStruct(a.shape if s else (N_DEV,) + a.shape, a.dtype) for a, s in zip(arrays, scatter)]


def _exchange_sems(n):
    return [pltpu.SemaphoreType.DMA((n, N_DEV - 1)), pltpu.SemaphoreType.DMA((n, N_DEV - 1)), pltpu.SemaphoreType.DMA((n,))]


def _exchange_issue(ins, outs, scatter, send_sems, recv_sems, local_sems):
    n = len(ins)
    x, y, c = lax.axis_index("x"), lax.axis_index("y"), lax.axis_index("c")
    me = 4 * x + 2 * y + c

    def index_of(p):
        return 4 * p[0] + 2 * p[1] + p[2]

    local = []
    for a in range(n):
        src_me = ins[a].at[me] if scatter[a] else ins[a]
        loc = pltpu.make_async_copy(src_me, outs[a].at[me], local_sems.at[a])
        loc.start()
        local.append(loc)
    for k in range(1, N_DEV):
        peer = _peer(x, y, c, k)
        for a in range(n):
            src = ins[a].at[index_of(peer)] if scatter[a] else ins[a]
            pltpu.make_async_remote_copy(
                src_ref=src, dst_ref=outs[a].at[me], send_sem=send_sems.at[a, k - 1], recv_sem=recv_sems.at[a, k - 1],
                device_id=peer, device_id_type=pl.DeviceIdType.MESH).start()

    def finish():
        for k in range(1, N_DEV):
            peer = _peer(x, y, c, k)
            for a in range(n):
                src = ins[a].at[index_of(peer)] if scatter[a] else ins[a]
                cp = pltpu.make_async_remote_copy(
                    src_ref=src, dst_ref=outs[a].at[index_of(peer)], send_sem=send_sems.at[a, k - 1],
                    recv_sem=recv_sems.at[a, k - 1], device_id=peer, device_id_type=pl.DeviceIdType.MESH)
                cp.wait_send()
                cp.wait_recv()
        for loc in local:
            loc.wait()

    return finish


def gather_two_level(arrays, name):
    n = len(arrays)

    def body(*refs):
        ins, outs = refs[:n], refs[n:2 * n]
        send_sems, recv_sems, local_sems = refs[2 * n:]
        x, y, c = lax.axis_index("x"), lax.axis_index("y"), lax.axis_index("c")
        sibling = (x, y, 1 - c)
        chips = [(1 - x, y), (x, 1 - y), (1 - x, 1 - y)]

        def slot(a, p):
            return outs[a].at[4 * p[0] + 2 * p[1] + p[2]]

        def copy(a, k, src, block, to):
            return pltpu.make_async_remote_copy(src_ref=src, dst_ref=slot(a, block), send_sem=send_sems.at[a, k],
                                                recv_sem=recv_sems.at[a, k], device_id=to, device_id_type=pl.DeviceIdType.MESH)

        me = (x, y, c)
        started, local = [], []
        for a in range(n):
            loc = pltpu.make_async_copy(ins[a], slot(a, me), local_sems.at[a])
            loc.start()
            local.append(loc)
            first = [copy(a, 0, ins[a], me, sibling)] + [copy(a, 1 + j, ins[a], me, (*chip, c)) for j, chip in enumerate(chips)]
            for cp in first:
                cp.start()
            started += first
        for j, chip in enumerate(chips):
            for a in range(n):
                copy(a, 1 + j, ins[a], (*chip, c), me).wait_recv()
                fwd = copy(a, 4 + j, slot(a, (*chip, c)), (*chip, c), sibling)
                fwd.start()
                started.append(fwd)
        for a in range(n):
            copy(a, 0, ins[a], sibling, me).wait_recv()
            for j, chip in enumerate(chips):
                copy(a, 4 + j, ins[a], (*chip, 1 - c), me).wait_recv()
        for cp in started:
            cp.wait_send()
        for loc in local:
            loc.wait()

    hbm = pl.BlockSpec(memory_space=pl.ANY)
    return pl.pallas_call(
        body, name=name, in_specs=[hbm] * n, out_specs=[hbm] * n, out_shape=_exchange_out_shapes(arrays, [False] * n),
        scratch_shapes=_exchange_sems(n), compiler_params=pltpu.CompilerParams(has_side_effects=True),
    )(*arrays)


def exchange(arrays, scatter, name):
    n = len(arrays)

    def body(*refs):
        _exchange_issue(refs[:n], refs[n:2 * n], scatter, *refs[2 * n:])()

    hbm = pl.BlockSpec(memory_space=pl.ANY)
    return pl.pallas_call(
        body, name=name, in_specs=[hbm] * n, out_specs=[hbm] * n, out_shape=_exchange_out_shapes(arrays, scatter),
        scratch_shapes=_exchange_sems(n), compiler_params=pltpu.CompilerParams(has_side_effects=True),
    )(*arrays)


def _rope_tables(n_lat, n_ctx):
    tok = np.arange(n_lat)
    freqs = 10000.0 ** (-np.arange(0, 16, 2, dtype=np.float32) / 16.0)

    def table(pos):
        ang = pos.astype(np.float32)[:, None] * freqs[None, :]
        ang = np.concatenate([ang, ang], axis=-1)
        return np.cos(ang), np.sin(ang)

    cr, sr = table(tok // GRID_W)
    cc, sc = table(tok % GRID_W)
    cos = np.tile(np.concatenate([cr, cc], axis=-1), (1, 4))
    sin = np.tile(np.concatenate([sr, sc], axis=-1), (1, 4))
    cos = np.concatenate([np.ones((n_ctx, 128), np.float32), cos], axis=0)
    sin = np.concatenate([np.zeros((n_ctx, 128), np.float32), sin], axis=0)
    return jnp.asarray(cos, F32), jnp.asarray(sin, F32)


def _pad_w_in(w):
    z = lambda n: jnp.zeros((w.shape[0], n), w.dtype)
    return jnp.concatenate([w[:, 1824:2848], w[:, 128:384], w[:, 416:672], w[:, 672:928], w[:, 928:1184], w[:, 1312:1568],
                            w[:, 1568:1824], w[:, 0:128], w[:, 384:416], z(96), w[:, 1184:1312], z(128)], axis=1)


def _unpad_w_in(wp):
    return jnp.concatenate([wp[:, C_GK:C_GK + 128], wp[:, C_GV:C_GV + 256], wp[:, C_GG:C_GG + 32], wp[:, C_NK:C_NK + 256],
                            wp[:, C_NV:C_NV + 256], wp[:, C_SU:C_SU + 256], wp[:, C_GQ:C_GQ + 128], wp[:, C_NQ:C_NQ + 256],
                            wp[:, C_PU:C_PU + 256], wp[:, C_GT:C_GT + 1024]], axis=1)


_W_IN_SEGS = [(0, 128, C_GK), (128, 256, C_GV), (384, 32, C_GG), (416, 256, C_NK), (672, 256, C_NV), (928, 256, C_SU),
              (1184, 128, C_GQ), (1312, 256, C_NQ), (1568, 256, C_PU), (1824, 1024, C_GT)]
W_IN_SHARD = N_IN // N_DEV


def _pad_w_in_blocks(blocks):
    pieces = []
    for orig, width, padded in _W_IN_SEGS:
        col = orig
        while col < orig + width:
            dev, lo = divmod(col, W_IN_SHARD)
            n = min(W_IN_SHARD - lo, orig + width - col)
            pieces.append((padded + col - orig, blocks[dev][:, lo:lo + n]))
            col += n
    pieces.sort(key=lambda p: p[0])
    out, at = [], 0
    for start, piece in pieces:
        if start > at:
            out.append(jnp.zeros((blocks.shape[1], start - at), blocks.dtype))
        out.append(piece)
        at = start + piece.shape[1]
    out.append(jnp.zeros((blocks.shape[1], PW - at), blocks.dtype))
    return jnp.concatenate(out, axis=1)


def _w_in_slabs(wp_blocks):
    slabs = []
    for dev in range(N_DEV):
        first, pieces = dev * W_IN_SHARD, []
        for orig, width, padded in _W_IN_SEGS:
            lo, hi = max(orig, first), min(orig + width, first + W_IN_SHARD)
            if lo < hi:
                a = padded + lo - orig
                blk, off = divmod(a, 1024)
                assert off + (hi - lo) <= 1024
                pieces.append(wp_blocks[blk][:, off:off + hi - lo])
        slabs.append(jnp.concatenate(pieces, axis=1))
    return jnp.stack(slabs)


def _pad_rows(u):
    return jnp.pad(u, ((POOL_HALO, POOL_HALO), (0, 0)))


def _block_diag4(w):
    out = jnp.zeros((256, 256), w.dtype)
    for i in range(4):
        out = lax.dynamic_update_slice(out, w[i], (64 * i, 64 * i))
    return out


def _layer_params(p, big, l):
    e_rep, e_tile, gmask, bdm = _s5_consts()
    wg = jnp.zeros((128, 256), F32)
    wg = lax.dynamic_update_slice(wg, p["gla_w_gate"][l, 0], (0, 0))
    wg = lax.dynamic_update_slice(wg, p["gla_w_gate"][l, 1], (16, 128))
    s5 = []
    for d in range(2):
        s5.append([p["s5_lam_re"][l, d], p["s5_lam_im"][l, d], p["s5_log_dt"][l, d].reshape(16, 1),
                   p["s5_b_re"][l, d].transpose(0, 2, 1).reshape(256, 64), p["s5_b_im"][l, d].transpose(0, 2, 1).reshape(256, 64),
                   p["s5_c_re"][l, d].reshape(256, 64), p["s5_c_im"][l, d].reshape(256, 64), e_rep, e_tile, gmask, bdm])
    havg = jnp.asarray((np.arange(256)[:, None] // 64 == np.arange(256)[None, :] // 64).astype(np.float32) / 64.0)
    e4 = jnp.asarray((np.arange(64)[:, None] == np.arange(256)[None, :] % 64).astype(np.float32))
    return dict(
        g_pre=p["g_pre"][l].reshape(1, D), g_post=p["g_post"][l].reshape(1, D), b_mod=p["b_mod"][l].reshape(1, 3 * D),
        w_mod=big["w_mod"], w_in=_pad_w_in_blocks(big["w_in_blocks"]) if "w_in_blocks" in big else _pad_w_in(big["w_in"]), w_out=big.get("w_out"),
        wg=wg, bg=p["gla_b_gate"][l].reshape(1, 256), g_norm=jnp.pad(p["gla_g_norm"][l].reshape(1, 64), ((0, 7), (0, 0))),
        bias8=_na_bias8(p["na_rpb"][l], f"na_bias_l{l}"), s5=s5, s5_d=p["s5_d"][l].reshape(1, 256), w_glu=None if big.get("s5_w_glu") is None else big["s5_w_glu"].astype(F32),
        b_glu=p["s5_b_glu"][l].reshape(1, 256), wpool=_block_diag4(p["pool_w"][l]), pool_scale=p["pool_scale"][l].reshape(1, 256),
        havg=havg, e4=e4)


def _matmul_tile(t, tile, steps):
    return t // steps if t % (8 * steps) == 0 else tile


def _cols(pz, start, width):
    return pz[:, start:start + width]


def _layer_fwd(z, modseg, lp, cos, sin, m_ctx, tile, s5_chunk, l, comm=None, late=None):
    t = z.shape[0]
    nct = m_ctx // tile
    nm = lambda s: f"{s}_l{l}"
    (h,) = rowwise_fwd(f_pre, nm("pre"), [z], [modseg], [lp["g_pre"]], [D], tile, nct)
    mm_tile = _matmul_tile(t, tile, 4)
    (gt,) = mm_nn_cols(h, lp["w_in"], C_GT, [1024], nm("in_proj_a"), tm=mm_tile)
    pv, nk, nv, su = mm_nn_cols(h, lp["w_in"], C_GV, [256] * 4, nm("in_proj_b"), tm=mm_tile)
    nq, pu, pk, pg, pq = mm_nn_cols(h, lp["w_in"], C_NQ, [256, 256, 128, 128, 128], nm("in_proj_c"), tm=mm_tile)
    q_r, k_r, lgf, lgb = rowwise_fwd(f_gla_prep, nm("gla_prep"), [pk, pg, pq, cos, sin], [], [lp["wg"], lp["bg"]], [128] * 4, tile, nct)
    half = None if comm is None else comm[0].shape[0] // 2
    spread = None if comm is None else [comm[0][:half], comm[1], comm[2], comm[3], comm[0][half:]]
    part = (lambda idx: None) if comm is None else (lambda idx: ([spread[i] for i in idx], [False] * len(idx)))
    (o1, st_f), got_late = gla_scan_fwd(q_r, k_r, pv, lgf, jnp.zeros((t, 256), F32), m_ctx, False, nm("gla_f"),
                                        None if late is None else (list(late), [False, False]))
    if late is not None:
        lp["w_out"], lp["w_glu"] = _gathered(got_late[0], False), _gathered(got_late[1], False).astype(F32)
    (o_gla, st_b), got_out = gla_scan_fwd(q_r, k_r, pv, lgb, o1, m_ctx, True, nm("gla_r"), part([2, 3]))
    received = None
    if comm is None:
        o_na = na_fwd(nq, nk, nv, lp["bias8"], m_ctx, nm("na"))
    else:
        o_na, got_in = na_fwd(nq, nk, nv, lp["bias8"], m_ctx, nm("na"), part([1]))
    s5p = [whole_fwd(f_s5_params, nm(f"s5_par{d}"), lp["s5"][d], [(1, 1024)] * 2 + [(256, 1024)] * 4) for d in range(2)]
    (y1, *states_f), got_mod_a = s5_scan_fwd(su, jnp.zeros((t, 256), F32), *s5p[0], m_ctx, s5_chunk, False, nm("s5_f"), part([0]))
    (y5, *states_b), got_mod_b = s5_scan_fwd(su, y1, *s5p[1], m_ctx, s5_chunk, True, nm("s5_r"), part([4]))
    if comm is not None:
        received = [jnp.concatenate([got_mod_a[0], got_mod_b[0]], axis=1), got_in[0], got_out[0], got_out[1]]
    pm = jnp.concatenate([pool_apply(_pad_rows(pu[:m_ctx]), m_ctx, False, nm("pool_c")),
                          pool_apply(_pad_rows(pu[m_ctx:]), t - m_ctx, False, nm("pool_x"))], axis=0)
    mix_rows = [o_gla, o_na, y5, su, pm, gt]
    mix_globs = [lp["g_norm"], lp["s5_d"], lp["w_glu"], lp["b_glu"], lp["wpool"], lp["pool_scale"], lp["havg"], lp["e4"]]
    (yg,) = rowwise_fwd(f_mix, nm("mix"), mix_rows, [], mix_globs, [D], tile, nct)
    out = mm_nn([yg], lp["w_out"], nm("out_proj"), tm=mm_tile)
    (z_new,) = rowwise_fwd(f_post, nm("post"), [z, out], [modseg], [lp["g_post"]], [D], tile, nct)
    saved = dict(z=z, h=h, pv=pv, nk=nk, nv=nv, su=su, nq=nq, pk=pk, pg=pg, pq=pq, q_r=q_r, k_r=k_r, lgf=lgf, lgb=lgb,
                 st_f=st_f, st_b=st_b, s5p=s5p, x0f=tuple(states_f), x0b=tuple(states_b), mix_rows=mix_rows, mix_globs=mix_globs,
                 yg=yg, out=out)
    return z_new, saved, received


def _f_pre_res(x, mod, g_pre):
    return f_pre(x, mod, g_pre)[0], x


def _layer_bwd(dz_new, sv, modseg, lp, cos, sin, m_ctx, tile, s5_chunk, l, comm=None, gdt=F32, send_early=False, as_slabs=False):
    t = dz_new.shape[0]
    nct = m_ctx // tile
    nm = lambda s: f"{s}_l{l}"
    g = {}
    dz_res, dout, dmod_post, g["g_post"] = rowwise_bwd(f_post, nm("post_b"), [sv["z"], sv["out"]], [modseg], [lp["g_post"]],
                                                       [dz_new], tile, nct, [True, True], [True])
    dyg = mm_nt([dout], lp["w_out"], nm("out_proj_dx"), tm=_matmul_tile(t, tile, 4))
    dw_tile = _matmul_tile(t, tile, 4)
    (g["w_out"],) = mm_tn(sv["yg"], [dout], nm("out_proj_dw"), tm=dw_tile, out_dtype=gdt)
    res = rowwise_bwd(f_mix, nm("mix_b"), sv["mix_rows"], [], sv["mix_globs"], [dyg], tile, nct, [True] * 6, [True] * 6 + [False] * 2)
    do_gla, do_na, dy5, dsu_a, dpm, dgt = res[:6]
    g["g_norm"], g["s5_d"], g["w_glu"], g["b_glu"], g["wpool"], g["pool_scale"] = res[6:]
    dpu = jnp.concatenate([pool_apply(_pad_rows(dpm[:m_ctx]), m_ctx, True, nm("pool_c_b")),
                           pool_apply(_pad_rows(dpm[m_ctx:]), t - m_ctx, True, nm("pool_x_b"))], axis=0)
    r_b = s5_scan_bwd(sv["su"], dy5, dsu_a, *sv["x0b"], *sv["s5p"][1], m_ctx, s5_chunk, True, nm("s5_r_b"))
    r_f = s5_scan_bwd(sv["su"], dy5, r_b[0], *sv["x0f"], *sv["s5p"][0], m_ctx, s5_chunk, False, nm("s5_f_b"))
    dsu = r_f[0]
    g["s5"] = [whole_bwd(f_s5_params, nm(f"s5_par{d}_b"), lp["s5"][d], list(r[1:]), [True] * 7 + [False] * 4)
               for d, r in ((0, r_f), (1, r_b))]
    part = (lambda idx: None) if comm is None else (lambda idx: ([comm[i] for i in idx], [True] * len(idx)))
    (dnq, dnk, dnv, dbias8), got_in = na_bwd(sv["nq"], sv["nk"], sv["nv"], do_na, lp["bias8"], m_ctx, nm("na_b"), part([0]))
    g["rpb"] = _na_rpb_grad(dbias8, nm("na_rpb_b"))
    zq, zv = jnp.zeros((t, 128), F32), jnp.zeros((t, 256), F32)
    early = ([_slabs(g["w_out"], False).astype(BF16), _slabs(g["w_glu"], False).astype(BF16)], [True, True]) if send_early else None
    (dq1, dk1, dv1, dlgb), g["early"] = gla_scan_bwd(sv["q_r"], sv["k_r"], sv["pv"], sv["lgb"], sv["st_b"], do_gla, (zq, zq, zv), m_ctx, True,
                                                     nm("gla_r_b"), early)
    (dq_r, dk_r, dpv, dlgf), got_out = gla_scan_bwd(sv["q_r"], sv["k_r"], sv["pv"], sv["lgf"], sv["st_f"], do_gla, (dq1, dk1, dv1), m_ctx, False,
                                                    nm("gla_f_b"), part([1, 2]))
    received = None if comm is None else [got_in[0], got_out[0], got_out[1]]
    dpk, dpg, dpq, g["wg"], g["bg"] = rowwise_bwd(f_gla_prep, nm("gla_prep_b"), [sv["pk"], sv["pg"], sv["pq"], cos, sin], [],
                                                  [lp["wg"], lp["bg"]], [dq_r, dk_r, dlgf, dlgb], tile, nct,
                                                  [True, True, True, False, False], [True, True])
    parts = [dgt, dpv, dnk, dnv, dsu, dnq, dpu, dpk, dpg, dpq, jnp.zeros((t, 128), F32)]
    dh = mm_nt(parts, lp["w_in"], nm("in_proj_dx"), tm=_matmul_tile(t, tile, 4))
    dw_blocks = mm_tn(sv["h"], parts, nm("in_proj_dw"), tm=dw_tile, out_dtype=gdt)
    if as_slabs:
        g["w_in_slabs"] = _w_in_slabs(dw_blocks)
    else:
        g["w_in"] = _unpad_w_in(jnp.concatenate(dw_blocks, axis=1))
    dz, dmod_pre, g["g_pre"] = rowwise_bwd(_f_pre_res, nm("pre_b"), [sv["z"]], [modseg], [lp["g_pre"]], [dh, dz_res], tile, nct, [True], [True])
    return dz, dmod_pre, dmod_post, g, received


def _f_mod_sum(cs, b_mod, w_mod):
    mod, _ = f_mod(cs, b_mod, w_mod)
    return mod, cs


def local_step(x, c, ctx, tgt, p, shards=None, tile=ROW_TILE, s5_chunk=S5_CHUNK):
    n_lat, m_ctx = x.shape[0], ctx.shape[0]
    n_layers = p["g_pre"].shape[0]
    z = jnp.concatenate([ctx, x], axis=0)
    cos, sin = _rope_tables(n_lat, m_ctx)
    cs = jnp.concatenate([c.reshape(1, D), p["c_ctx"].reshape(1, D), jnp.zeros((6, D), F32)], axis=0)
    gather = [False] * len(_SHARDED)
    lps, mods, silus, saves = [], [], [], []
    got = gather_two_level(shards[0][:2], "gather_weights_l0") if shards is not None else None
    for l in range(n_layers):
        if shards is None:
            big = {n: p[n][l] for n in _SHARDED}
        else:
            big = {n: _gathered(g, _BY_COLS[n]) for n, g in zip(_SHARDED, got) if n != "w_in"}
            big["w_in_blocks"] = got[_SHARDED.index("w_in")]
        lp = _layer_params(p, big, l)
        mod8, s8 = whole_fwd(f_mod, f"mod_l{l}", [cs, lp["b_mod"], lp["w_mod"]], [(8, 3 * D), (8, D)])
        modseg = mod8[:2].reshape(2, 1, 3 * D)
        comm = shards[l + 1] if shards is not None and l + 1 < n_layers else None
        late = shards[0][2:] if shards is not None and l == 0 else None
        z, sv, got = _layer_fwd(z, modseg, lp, cos, sin, m_ctx, tile, s5_chunk, l, comm, late)
        lps.append(lp); mods.append(modseg); silus.append(s8); saves.append(sv)
    loss, dz = loss_and_grad(z, tgt, m_ctx, "loss", tile)
    grads, received = [None] * n_layers, [None] * n_layers
    gdt = F32 if shards is None else BF16
    dcs = jnp.zeros((8, D), F32)
    pending = None
    for l in reversed(range(n_layers)):
        lp = lps[l]
        dz, dmod_pre, dmod_post, g, got = _layer_bwd(dz, saves[l], mods[l], lp, cos, sin, m_ctx, tile, s5_chunk, l, pending, gdt,
                                                     send_early=shards is not None and l == 0, as_slabs=shards is not None)
        if pending is not None:
            received[l + 1] = got
        dmod = jnp.concatenate([dmod_pre.reshape(2, 3 * D)[:, :2 * D], dmod_post.reshape(2, 3 * D)[:, 2 * D:]], axis=1)
        dmod8 = jnp.pad(dmod, ((0, 6), (0, 0)))
        dcs, g["b_mod"] = whole_bwd(_f_mod_sum, f"mod_b_l{l}", [cs, lp["b_mod"], lp["w_mod"]], [dmod8, dcs], [True, True, False])
        if shards is None:
            g["w_mod"] = jnp.concatenate(mm_tn(silus[l], [dmod8[:, :D], dmod8[:, D:2 * D], dmod8[:, 2 * D:]], f"mod_dw_l{l}", tm=8), axis=1)
        else:
            g["mod_s"], g["mod_d"] = silus[l][:2], dmod
        grads[l] = g
        if shards is not None:
            pending = _layer_sends(g)
    if shards is not None:
        got_in, got_small = exchange([pending[0], _small_sends(dcs[1], grads)], [True, False], "exchange_grads_l0")
        received[0] = [got_in] + list(grads[0]["early"]) + [got_small]
    return loss, dz[m_ctx:], dcs[1], grads, received


_WEIGHTS = ["c_ctx", "w_mod", "b_mod", "g_pre", "g_post", "w_in", "w_out", "gla_w_gate", "gla_b_gate", "gla_g_norm", "na_rpb",
            "s5_lam_re", "s5_lam_im", "s5_log_dt", "s5_b_re", "s5_b_im", "s5_c_re", "s5_c_im", "s5_d", "s5_w_glu", "s5_b_glu",
            "pool_w", "pool_scale"]
_INPUTS = ["x", "c", "ctx"] + _WEIGHTS + ["loss_target"] + ["m_" + n for n in _WEIGHTS] + ["v_" + n for n in _WEIGHTS]
_SHARDED = ["w_mod", "w_in", "w_out", "s5_w_glu"]
_BY_COLS = {"w_mod": True, "w_in": True, "w_out": False, "s5_w_glu": False}
_GRAD_SHARDED = ["w_in", "w_out", "s5_w_glu"]
_SMALL = [n for n in _WEIGHTS if n not in _SHARDED]
_SMALL_PER_LAYER = [n for n in _SMALL if n != "c_ctx"]
_PACK_ROWS = 256


def _pack_plan(like):
    tiled = [i for i, a in enumerate(like) if a.size % 1024 == 0]
    loose = [i for i, a in enumerate(like) if a.size % 1024 != 0]
    tail = -(-sum(like[i].size for i in loose) // 1024) * 8
    rows = sum(like[i].size // 128 for i in tiled) + tail
    return tiled, loose, tail, -(-rows // _PACK_ROWS) * _PACK_ROWS - rows


def _pack_rows(like, index):
    tiled, _, _, _ = _pack_plan(like)
    row = 0
    for i in tiled:
        n = like[i].size // 128
        if i == index:
            return row, row + n
        row += n
    raise ValueError("not a tile-aligned entry")


def _pack(arrs):
    tiled, loose, tail, fill = _pack_plan(arrs)
    dt = arrs[0].dtype
    flat = jnp.concatenate([arrs[i].reshape(-1) for i in loose])
    flat = jnp.pad(flat, (0, tail * 128 - flat.shape[0])).reshape(tail, 128)
    return jnp.concatenate([arrs[i].reshape(-1, 128) for i in tiled] + [flat, jnp.zeros((fill, 128), dt)], axis=0)


def _unpack(packed, like):
    tiled, loose, tail, _ = _pack_plan(like)
    out, row = [None] * len(like), 0
    for i in tiled:
        n = like[i].size // 128
        out[i] = packed[row:row + n].reshape(like[i].shape)
        row += n
    flat, pos = packed[row:row + tail].reshape(-1), 0
    for i in loose:
        out[i] = flat[pos:pos + like[i].size].reshape(like[i].shape)
        pos += like[i].size
    return out


def _gathered(g, cols):
    if cols:
        return g.transpose(1, 0, 2).reshape(g.shape[1], N_DEV * g.shape[2])
    return g.reshape(N_DEV * g.shape[1], g.shape[2])


def _slabs(w, cols):
    r, c = w.shape
    if cols:
        return w.reshape(r, N_DEV, c // N_DEV).transpose(1, 0, 2)
    return w.reshape(N_DEV, r // N_DEV, c)


def _layer_small(g):
    s5 = lambda i, f: jnp.stack([f(g["s5"][d][i]) for d in range(2)])
    return {
        "b_mod": g["b_mod"].reshape(3 * D), "g_pre": g["g_pre"].reshape(D), "g_post": g["g_post"].reshape(D),
        "gla_w_gate": jnp.stack([g["wg"][0:16, 0:128], g["wg"][16:32, 128:256]]),
        "gla_b_gate": g["bg"].reshape(2, 128), "gla_g_norm": g["g_norm"][0], "na_rpb": g["rpb"],
        "s5_lam_re": s5(0, lambda a: a), "s5_lam_im": s5(1, lambda a: a), "s5_log_dt": s5(2, lambda a: a.reshape(16)),
        "s5_b_re": s5(3, lambda a: a.reshape(16, 16, 64).transpose(0, 2, 1)),
        "s5_b_im": s5(4, lambda a: a.reshape(16, 16, 64).transpose(0, 2, 1)),
        "s5_c_re": s5(5, lambda a: a.reshape(16, 16, 64)), "s5_c_im": s5(6, lambda a: a.reshape(16, 16, 64)),
        "s5_d": g["s5_d"].reshape(256), "s5_b_glu": g["b_glu"].reshape(256),
        "pool_w": jnp.stack([g["wpool"][64 * i:64 * i + 64, 64 * i:64 * i + 64] for i in range(4)]),
        "pool_scale": g["pool_scale"].reshape(256),
    }


def _layer_sends(g):
    return [g["w_in_slabs"].astype(BF16), _slabs(g["w_out"], False).astype(BF16), _slabs(g["w_glu"], False).astype(BF16)]


def _small_sends(d_c_ctx, grads):
    per_layer = [_layer_small(g) for g in grads]
    full = {n: jnp.stack([s[n] for s in per_layer]) for n in _SMALL_PER_LAYER}
    full["c_ctx"] = d_c_ctx
    factors = [jnp.stack([g["mod_s"] for g in grads]), jnp.stack([g["mod_d"] for g in grads])]
    return _pack([full[n] for n in _SMALL] + factors).astype(BF16)


def kernel(x, c, ctx, c_ctx, w_mod, b_mod, g_pre, g_post, w_in, w_out, gla_w_gate, gla_b_gate, gla_g_norm, na_rpb, s5_lam_re, s5_lam_im, s5_log_dt, s5_b_re, s5_b_im, s5_c_re, s5_c_im, s5_d, s5_w_glu, s5_b_glu, pool_w, pool_scale, loss_target, m_c_ctx, m_w_mod, m_b_mod, m_g_pre, m_g_post, m_w_in, m_w_out, m_gla_w_gate, m_gla_b_gate, m_gla_g_norm, m_na_rpb, m_s5_lam_re, m_s5_lam_im, m_s5_log_dt, m_s5_b_re, m_s5_b_im, m_s5_c_re, m_s5_c_im, m_s5_d, m_s5_w_glu, m_s5_b_glu, m_pool_w, m_pool_scale, v_c_ctx, v_w_mod, v_b_mod, v_g_pre, v_g_post, v_w_in, v_w_out, v_gla_w_gate, v_gla_b_gate, v_gla_g_norm, v_na_rpb, v_s5_lam_re, v_s5_lam_im, v_s5_log_dt, v_s5_b_re, v_s5_b_im, v_s5_c_re, v_s5_c_im, v_s5_d, v_s5_w_glu, v_s5_b_glu, v_pool_w, v_pool_scale):
    given = dict(zip(_INPUTS, (x, c, ctx, c_ctx, w_mod, b_mod, g_pre, g_post, w_in, w_out, gla_w_gate, gla_b_gate, gla_g_norm, na_rpb, s5_lam_re, s5_lam_im, s5_log_dt, s5_b_re, s5_b_im, s5_c_re, s5_c_im, s5_d, s5_w_glu, s5_b_glu, pool_w, pool_scale, loss_target, m_c_ctx, m_w_mod, m_b_mod, m_g_pre, m_g_post, m_w_in, m_w_out, m_gla_w_gate, m_gla_b_gate, m_gla_g_norm, m_na_rpb, m_s5_lam_re, m_s5_lam_im, m_s5_log_dt, m_s5_b_re, m_s5_b_im, m_s5_c_re, m_s5_c_im, m_s5_d, m_s5_w_glu, m_s5_b_glu, m_pool_w, m_pool_scale, v_c_ctx, v_w_mod, v_b_mod, v_g_pre, v_g_post, v_w_in, v_w_out, v_gla_w_gate, v_gla_b_gate, v_gla_g_norm, v_na_rpb, v_s5_lam_re, v_s5_lam_im, v_s5_log_dt, v_s5_b_re, v_s5_b_im, v_s5_c_re, v_s5_c_im, v_s5_d, v_s5_w_glu, v_s5_b_glu, v_pool_w, v_pool_scale)))
    n_layers = w_in.shape[0]
    shards = [[given[n][l].astype(BF16) for n in _SHARDED] for l in range(n_layers)]
    p = {n: given[n] for n in _SMALL}
    loss, grad_x, _, _, received = local_step(x[0], c, ctx[0], loss_target[0], p, shards)
    final = {}
    for n in _GRAD_SHARDED:
        per_layer = [adamw(received[l][_GRAD_SHARDED.index(n)], given[n][l], given["m_" + n][l], given["v_" + n][l], f"adamw_{n}_l{l}")
                     for l in range(n_layers)]
        final[n] = [jnp.stack([res[kind] for res in per_layer]) for kind in range(4)]
    factor_like = [jnp.zeros((n_layers, 2, D), F32), jnp.zeros((n_layers, 2, 3 * D), F32)]
    like = [given[n] for n in _SMALL] + factor_like
    small_recv = received[0][-1]
    rows_s, rows_d = _pack_rows(like, len(_SMALL)), _pack_rows(like, len(_SMALL) + 1)
    fac_s = small_recv[:, rows_s[0]:rows_s[1]].reshape(N_DEV, n_layers, 2, D)
    fac_d = small_recv[:, rows_d[0]:rows_d[1]].reshape(N_DEV, n_layers, 2, 3 * D)
    me = 4 * lax.axis_index("x") + 2 * lax.axis_index("y") + lax.axis_index("c")
    cols = w_mod.shape[2]
    per_layer = []
    for l in range(n_layers):
        s_all = fac_s[:, l].reshape(2 * N_DEV, D)
        d_mine = lax.dynamic_slice_in_dim(fac_d[:, l].reshape(2 * N_DEV, 3 * D), me * cols, cols, axis=1)
        (g_mod,) = mm_tn(s_all, [d_mine], f"mod_dw_l{l}", tm=2 * N_DEV, tn=cols)
        per_layer.append(adamw(g_mod[None], given["w_mod"][l], given["m_w_mod"][l], given["v_w_mod"][l], f"adamw_w_mod_l{l}"))
    final["w_mod"] = [jnp.stack([res[kind] for res in per_layer]) for kind in range(4)]
    res = adamw(small_recv, _pack(like), _pack([given["m_" + n] for n in _SMALL] + factor_like),
                _pack([given["v_" + n] for n in _SMALL] + factor_like), "adamw_small")
    unpacked = [_unpack(packed, like) for packed in res]
    for i, n in enumerate(_SMALL):
        final[n] = [unpacked[kind][i] for kind in range(4)]
    loss = lax.psum(loss, ("x", "y", "c"))
    return (loss, grad_x[None], *[final[n][0] for n in _WEIGHTS], *[final[n][1] for n in _WEIGHTS],
            *[final[n][2] for n in _WEIGHTS], *[final[n][3] for n in _WEIGHTS])
```
